```python
import jax, jax.numpy as jnp
from jax import lax
import numpy as np

D_MODEL = 1024
BATCH = 8
SEQ = 4096
DEPTH = 1

HEAD_DIM = 64
RWKV_HEADS = 8
RWKV_WIDTH = RWKV_HEADS * HEAD_DIM
ATT_Q_HEADS = 8
ATT_KV_HEADS = 2
ATT_GROUP = ATT_Q_HEADS // ATT_KV_HEADS
ATT_WIDTH = ATT_Q_HEADS * HEAD_DIM
KV_WIDTH = ATT_KV_HEADS * HEAD_DIM
WINDOW = 128
BLOCK = 128
DECAY_LORA = 32
ICLR_LORA = 32
GATE_LORA = 96
D_FF = 2816
N_BRANCH = 2
RMS_EPS = 1e-6
GN_EPS = 64e-5

RWKV_COLS = 3 * RWKV_WIDTH + DECAY_LORA + ICLR_LORA + GATE_LORA
ATT_COLS = ATT_WIDTH + 2 * KV_WIDTH
GATE_COLS = N_BRANCH * D_MODEL
IN_COLS = RWKV_COLS + ATT_COLS + GATE_COLS

kernel_name = "rwkv7_swa_sink_macaron_hybrid"


def rms_norm(x, g, eps=RMS_EPS):
    xf = x.astype(jnp.float32)
    y = xf * lax.rsqrt(jnp.mean(xf * xf, axis=-1, keepdims=True) + eps)
    return (y * g.astype(jnp.float32)).astype(x.dtype)


def swiglu(h, w_gate, w_up, w_down):
    return (jax.nn.silu(h @ w_gate) * (h @ w_up)) @ w_down


def token_shift(p):
    return jnp.pad(p, ((0, 0), (1, 0), (0, 0)))[:, :-1]


def wkv7_scan(r, w, k, v, a, b):
    Bsz, T, H, N = r.shape
    xs = tuple(jnp.moveaxis(t, 1, 0) for t in (r, w, k, v, a, b))

    def step(S, inp):
        r_t, w_t, k_t, v_t, a_t, b_t = inp
        sa = jnp.einsum('bhij,bhj->bhi', S, a_t)
        S = S * w_t[:, :, None, :] + sa[..., None] * b_t[:, :, None, :] + v_t[..., None] * k_t[:, :, None, :]
        y_t = jnp.einsum('bhij,bhj->bhi', S, r_t)
        return S, y_t

    S0 = jnp.zeros((Bsz, H, N, N), jnp.float32)
    _, ys = lax.scan(step, S0, xs)
    return jnp.moveaxis(ys, 0, 1)


def rwkv7_branch(p, mu, w0, w_lora_up, a0, a_lora_up, g_lora_up, k_k, k_a, r_k, ln_w, ln_b):
    Bsz, T, _ = p.shape
    f32 = jnp.float32
    p = p + (token_shift(p) - p) * mu
    r, k, v, xw, xa, xg = jnp.split(
        p, [RWKV_WIDTH, 2 * RWKV_WIDTH, 3 * RWKV_WIDTH,
            3 * RWKV_WIDTH + DECAY_LORA, 3 * RWKV_WIDTH + DECAY_LORA + ICLR_LORA], axis=-1)
    w_log = -jax.nn.softplus(-(w0 + jnp.tanh(xw) @ w_lora_up)) - 0.5
    decay = jnp.exp(-jnp.exp(w_log.astype(f32)))
    a = jax.nn.sigmoid(a0 + xa @ a_lora_up)
    g = jax.nn.sigmoid(xg) @ g_lora_up

    def heads(t):
        return t.reshape(Bsz, T, RWKV_HEADS, HEAD_DIM).astype(f32)

    r, k, v, decay, a = heads(r), heads(k), heads(v), heads(decay), heads(a)
    k_k = k_k.reshape(RWKV_HEADS, HEAD_DIM).astype(f32)
    k_a = k_a.reshape(RWKV_HEADS, HEAD_DIM).astype(f32)
    kk = k * k_k
    kk = kk / jnp.maximum(jnp.sqrt(jnp.sum(kk * kk, axis=-1, keepdims=True)), 1e-12)
    k = k * (1.0 + (a - 1.0) * k_a)
    y = wkv7_scan(r, decay, k, v, -kk, kk * a)
    mean = jnp.mean(y, axis=-1, keepdims=True)
    var = jnp.mean(jnp.square(y - mean), axis=-1, keepdims=True)
    y = (y - mean) * lax.rsqrt(var + GN_EPS)
    y = y * ln_w.reshape(RWKV_HEADS, HEAD_DIM).astype(f32) + ln_b.reshape(RWKV_HEADS, HEAD_DIM).astype(f32)
    bonus = jnp.sum(r * k * r_k.astype(f32), axis=-1, keepdims=True) * v
    y = (y + bonus).reshape(Bsz, T, RWKV_WIDTH) * g.astype(f32)
    return y.astype(p.dtype)


def sliding_window_attention(q, k, v, sinks):
    Bsz, T = q.shape[:2]
    nb = T // BLOCK
    f32 = jnp.float32
    qb = q.reshape(Bsz, nb, BLOCK, ATT_KV_HEADS, ATT_GROUP, HEAD_DIM)

    def with_prev(t):
        tb = t.reshape(Bsz, nb, BLOCK, ATT_KV_HEADS, HEAD_DIM)
        prev = jnp.pad(tb, ((0, 0), (1, 0), (0, 0), (0, 0), (0, 0)))[:, :-1]
        return jnp.concatenate([prev, tb], axis=2)

    kc, vc = with_prev(k), with_prev(v)
    scale = HEAD_DIM ** -0.5
    s = jnp.einsum('bnqhgd,bnkhd->bnhgqk', qb, kc).astype(f32) * scale
    qi = jnp.arange(BLOCK)[:, None]
    kj = jnp.arange(2 * BLOCK)[None, :]
    band = (kj <= qi + BLOCK) & (kj > qi + BLOCK - WINDOW)
    valid = (jnp.arange(nb)[:, None, None] > 0) | (kj >= BLOCK)[None]
    mask = (band[None] & valid)[None, :, None, None]
    s = jnp.where(mask, s, -jnp.inf)
    sink = sinks.astype(f32).reshape(1, 1, ATT_KV_HEADS, ATT_GROUP, 1, 1)
    m = jnp.maximum(jnp.max(s, axis=-1, keepdims=True), sink)
    pexp = jnp.exp(s - m)
    denom = jnp.sum(pexp, axis=-1, keepdims=True) + jnp.exp(sink - m)
    probs = (pexp / denom).astype(v.dtype)
    o = jnp.einsum('bnhgqk,bnkhd->bnqhgd', probs, vc)
    return o.reshape(Bsz, T, ATT_WIDTH)


def attention_branch(p, q_norm, k_norm, sinks):
    Bsz, T, _ = p.shape
    q, k, v = jnp.split(p, [ATT_WIDTH, ATT_WIDTH + KV_WIDTH], axis=-1)
    q = rms_norm(q.reshape(Bsz, T, ATT_Q_HEADS, HEAD_DIM), q_norm)
    k = rms_norm(k.reshape(Bsz, T, ATT_KV_HEADS, HEAD_DIM), k_norm)
    v = v.reshape(Bsz, T, ATT_KV_HEADS, HEAD_DIM)
    return sliding_window_attention(q, k, v, sinks)


def _fwd_setup_inputs(seed: int = 0) -> dict:
    key = jax.random.key(seed)
    ks = iter(jax.random.split(key, 40))
    L = DEPTH

    def nrm(shape, scale):
        return jax.random.normal(next(ks), shape, jnp.float32) * scale

    def gain(shape):
        return 1.0 + nrm(shape, 0.02)

    return {
        "x": jax.random.normal(next(ks), (BATCH, SEQ, D_MODEL), jnp.float32),
        "ffn1_norm": gain((L, D_MODEL)),
        "ffn1_w_gate": nrm((L, D_MODEL, D_FF), D_MODEL ** -0.5),
        "ffn1_w_up": nrm((L, D_MODEL, D_FF), D_MODEL ** -0.5),
        "ffn1_w_down": nrm((L, D_FF, D_MODEL), D_FF ** -0.5),
        "mix_norm": gain((L, D_MODEL)),
        "w_in": nrm((L, D_MODEL, IN_COLS), D_MODEL ** -0.5),
        "rwkv_mu": jax.random.uniform(next(ks), (L, RWKV_COLS), jnp.float32),
        "rwkv_w0": jax.random.uniform(next(ks), (L, RWKV_WIDTH), jnp.float32, -6.5, -1.5),
        "rwkv_w_lora_up": nrm((L, DECAY_LORA, RWKV_WIDTH), 0.1),
        "rwkv_a0": nrm((L, RWKV_WIDTH), 0.1),
        "rwkv_a_lora_up": nrm((L, ICLR_LORA, RWKV_WIDTH), 0.1),
        "rwkv_g_lora_up": nrm((L, GATE_LORA, RWKV_WIDTH), GATE_LORA ** -0.5),
        "rwkv_k_k": 0.85 + nrm((L, RWKV_WIDTH), 0.02),
        "rwkv_k_a": gain((L, RWKV_WIDTH)),
        "rwkv_r_k": nrm((L, RWKV_HEADS, HEAD_DIM), 0.1),
        "rwkv_ln_w": gain((L, RWKV_WIDTH)),
        "rwkv_ln_b": nrm((L, RWKV_WIDTH), 0.02),
        "attn_q_norm": gain((L, HEAD_DIM)),
        "attn_k_norm": gain((L, HEAD_DIM)),
        "attn_sinks": nrm((L, ATT_Q_HEADS), 0.5),
        "w_branch_rwkv": nrm((L, RWKV_WIDTH, D_MODEL), RWKV_WIDTH ** -0.5),
        "w_branch_attn": nrm((L, ATT_WIDTH, D_MODEL), ATT_WIDTH ** -0.5),
        "w_out": nrm((L, D_MODEL, D_MODEL), D_MODEL ** -0.5),
        "ffn2_norm": gain((L, D_MODEL)),
        "ffn2_w_gate": nrm((L, D_MODEL, D_FF), D_MODEL ** -0.5),
        "ffn2_w_up": nrm((L, D_MODEL, D_FF), D_MODEL ** -0.5),
        "ffn2_w_down": nrm((L, D_FF, D_MODEL), D_FF ** -0.5),
        "final_norm": gain((L, D_MODEL)),
    }


def _fwd_reference(x, ffn1_norm, ffn1_w_gate, ffn1_w_up, ffn1_w_down, mix_norm, w_in,
              rwkv_mu, rwkv_w0, rwkv_w_lora_up, rwkv_a0, rwkv_a_lora_up, rwkv_g_lora_up,
              rwkv_k_k, rwkv_k_a, rwkv_r_k, rwkv_ln_w, rwkv_ln_b,
              attn_q_norm, attn_k_norm, attn_sinks,
              w_branch_rwkv, w_branch_attn, w_out,
              ffn2_norm, ffn2_w_gate, ffn2_w_up, ffn2_w_down, final_norm):
    for l in range(DEPTH):
        x = x + 0.5 * swiglu(rms_norm(x, ffn1_norm[l]), ffn1_w_gate[l], ffn1_w_up[l], ffn1_w_down[l])
        h = rms_norm(x, mix_norm[l])
        proj = h @ w_in[l]
        p_rwkv, p_att, p_gate = jnp.split(proj, [RWKV_COLS, RWKV_COLS + ATT_COLS], axis=-1)
        y_rwkv = rwkv7_branch(p_rwkv, rwkv_mu[l], rwkv_w0[l], rwkv_w_lora_up[l], rwkv_a0[l],
                              rwkv_a_lora_up[l], rwkv_g_lora_up[l], rwkv_k_k[l], rwkv_k_a[l],
                              rwkv_r_k[l], rwkv_ln_w[l], rwkv_ln_b[l])
        y_att = attention_branch(p_att, attn_q_norm[l], attn_k_norm[l], attn_sinks[l])
        gate_rwkv, gate_att = jnp.split(jax.nn.sigmoid(p_gate), N_BRANCH, axis=-1)
        merged = gate_rwkv * (y_rwkv @ w_branch_rwkv[l]) + gate_att * (y_att @ w_branch_attn[l])
        x = x + merged @ w_out[l]
        x = x + 0.5 * swiglu(rms_norm(x, ffn2_norm[l]), ffn2_w_gate[l], ffn2_w_up[l], ffn2_w_down[l])
        x = rms_norm(x, final_norm[l])
    return x


import jax as _jax
import jax.numpy as _jnp

TWIN_FORMAT = 'train_step'
FWD_PARAMS = ['x', 'ffn1_norm', 'ffn1_w_gate', 'ffn1_w_up', 'ffn1_w_down', 'mix_norm', 'w_in', 'rwkv_mu', 'rwkv_w0', 'rwkv_w_lora_up', 'rwkv_a0', 'rwkv_a_lora_up', 'rwkv_g_lora_up', 'rwkv_k_k', 'rwkv_k_a', 'rwkv_r_k', 'rwkv_ln_w', 'rwkv_ln_b', 'attn_q_norm', 'attn_k_norm', 'attn_sinks', 'w_branch_rwkv', 'w_branch_attn', 'w_out', 'ffn2_norm', 'ffn2_w_gate', 'ffn2_w_up', 'ffn2_w_down', 'final_norm']
TWIN_WEIGHTS = ['ffn1_norm', 'ffn1_w_gate', 'ffn1_w_up', 'ffn1_w_down', 'mix_norm', 'w_in', 'rwkv_mu', 'rwkv_w0', 'rwkv_w_lora_up', 'rwkv_a0', 'rwkv_a_lora_up', 'rwkv_g_lora_up', 'rwkv_k_k', 'rwkv_k_a', 'rwkv_r_k', 'rwkv_ln_w', 'rwkv_ln_b', 'attn_q_norm', 'attn_k_norm', 'attn_sinks', 'w_branch_rwkv', 'w_branch_attn', 'w_out', 'ffn2_norm', 'ffn2_w_gate', 'ffn2_w_up', 'ffn2_w_down', 'final_norm']
TWIN_DIFF_INPUT = 'x'
TWIN_INPUTS = ['x', 'ffn1_norm', 'ffn1_w_gate', 'ffn1_w_up', 'ffn1_w_down', 'mix_norm', 'w_in', 'rwkv_mu', 'rwkv_w0', 'rwkv_w_lora_up', 'rwkv_a0', 'rwkv_a_lora_up', 'rwkv_g_lora_up', 'rwkv_k_k', 'rwkv_k_a', 'rwkv_r_k', 'rwkv_ln_w', 'rwkv_ln_b', 'attn_q_norm', 'attn_k_norm', 'attn_sinks', 'w_branch_rwkv', 'w_branch_attn', 'w_out', 'ffn2_norm', 'ffn2_w_gate', 'ffn2_w_up', 'ffn2_w_down', 'final_norm', 'loss_target', 'm_ffn1_norm', 'm_ffn1_w_gate', 'm_ffn1_w_up', 'm_ffn1_w_down', 'm_mix_norm', 'm_w_in', 'm_rwkv_mu', 'm_rwkv_w0', 'm_rwkv_w_lora_up', 'm_rwkv_a0', 'm_rwkv_a_lora_up', 'm_rwkv_g_lora_up', 'm_rwkv_k_k', 'm_rwkv_k_a', 'm_rwkv_r_k', 'm_rwkv_ln_w', 'm_rwkv_ln_b', 'm_attn_q_norm', 'm_attn_k_norm', 'm_attn_sinks', 'm_w_branch_rwkv', 'm_w_branch_attn', 'm_w_out', 'm_ffn2_norm', 'm_ffn2_w_gate', 'm_ffn2_w_up', 'm_ffn2_w_down', 'm_final_norm', 'v_ffn1_norm', 'v_ffn1_w_gate', 'v_ffn1_w_up', 'v_ffn1_w_down', 'v_mix_norm', 'v_w_in', 'v_rwkv_mu', 'v_rwkv_w0', 'v_rwkv_w_lora_up', 'v_rwkv_a0', 'v_rwkv_a_lora_up', 'v_rwkv_g_lora_up', 'v_rwkv_k_k', 'v_rwkv_k_a', 'v_rwkv_r_k', 'v_rwkv_ln_w', 'v_rwkv_ln_b', 'v_attn_q_norm', 'v_attn_k_norm', 'v_attn_sinks', 'v_w_branch_rwkv', 'v_w_branch_attn', 'v_w_out', 'v_ffn2_norm', 'v_ffn2_w_gate', 'v_ffn2_w_up', 'v_ffn2_w_down', 'v_final_norm']
TWIN_OUTPUTS = ['loss', 'grad_x', 'grad_ffn1_norm', 'grad_ffn1_w_gate', 'grad_ffn1_w_up', 'grad_ffn1_w_down', 'grad_mix_norm', 'grad_w_in', 'grad_rwkv_mu', 'grad_rwkv_w0', 'grad_rwkv_w_lora_up', 'grad_rwkv_a0', 'grad_rwkv_a_lora_up', 'grad_rwkv_g_lora_up', 'grad_rwkv_k_k', 'grad_rwkv_k_a', 'grad_rwkv_r_k', 'grad_rwkv_ln_w', 'grad_rwkv_ln_b', 'grad_attn_q_norm', 'grad_attn_k_norm', 'grad_attn_sinks', 'grad_w_branch_rwkv', 'grad_w_branch_attn', 'grad_w_out', 'grad_ffn2_norm', 'grad_ffn2_w_gate', 'grad_ffn2_w_up', 'grad_ffn2_w_down', 'grad_final_norm', 'delta_ffn1_norm', 'delta_ffn1_w_gate', 'delta_ffn1_w_up', 'delta_ffn1_w_down', 'delta_mix_norm', 'delta_w_in', 'delta_rwkv_mu', 'delta_rwkv_w0', 'delta_rwkv_w_lora_up', 'delta_rwkv_a0', 'delta_rwkv_a_lora_up', 'delta_rwkv_g_lora_up', 'delta_rwkv_k_k', 'delta_rwkv_k_a', 'delta_rwkv_r_k', 'delta_rwkv_ln_w', 'delta_rwkv_ln_b', 'delta_attn_q_norm', 'delta_attn_k_norm', 'delta_attn_sinks', 'delta_w_branch_rwkv', 'delta_w_branch_attn', 'delta_w_out', 'delta_ffn2_norm', 'delta_ffn2_w_gate', 'delta_ffn2_w_up', 'delta_ffn2_w_down', 'delta_final_norm', 'new_m_ffn1_norm', 'new_m_ffn1_w_gate', 'new_m_ffn1_w_up', 'new_m_ffn1_w_down', 'new_m_mix_norm', 'new_m_w_in', 'new_m_rwkv_mu', 'new_m_rwkv_w0', 'new_m_rwkv_w_lora_up', 'new_m_rwkv_a0', 'new_m_rwkv_a_lora_up', 'new_m_rwkv_g_lora_up', 'new_m_rwkv_k_k', 'new_m_rwkv_k_a', 'new_m_rwkv_r_k', 'new_m_rwkv_ln_w', 'new_m_rwkv_ln_b', 'new_m_attn_q_norm', 'new_m_attn_k_norm', 'new_m_attn_sinks', 'new_m_w_branch_rwkv', 'new_m_w_branch_attn', 'new_m_w_out', 'new_m_ffn2_norm', 'new_m_ffn2_w_gate', 'new_m_ffn2_w_up', 'new_m_ffn2_w_down', 'new_m_final_norm', 'new_v_ffn1_norm', 'new_v_ffn1_w_gate', 'new_v_ffn1_w_up', 'new_v_ffn1_w_down', 'new_v_mix_norm', 'new_v_w_in', 'new_v_rwkv_mu', 'new_v_rwkv_w0', 'new_v_rwkv_w_lora_up', 'new_v_rwkv_a0', 'new_v_rwkv_a_lora_up', 'new_v_rwkv_g_lora_up', 'new_v_rwkv_k_k', 'new_v_rwkv_k_a', 'new_v_rwkv_r_k', 'new_v_rwkv_ln_w', 'new_v_rwkv_ln_b', 'new_v_attn_q_norm', 'new_v_attn_k_norm', 'new_v_attn_sinks', 'new_v_w_branch_rwkv', 'new_v_w_branch_attn', 'new_v_w_out', 'new_v_ffn2_norm', 'new_v_ffn2_w_gate', 'new_v_ffn2_w_up', 'new_v_ffn2_w_down', 'new_v_final_norm']
TWIN_LEAF_KINDS = {'loss': 'loss', 'grad_x': 'grad_x', 'grad_ffn1_norm': 'grad_w', 'grad_ffn1_w_gate': 'grad_w', 'grad_ffn1_w_up': 'grad_w', 'grad_ffn1_w_down': 'grad_w', 'grad_mix_norm': 'grad_w', 'grad_w_in': 'grad_w', 'grad_rwkv_mu': 'grad_w', 'grad_rwkv_w0': 'grad_w', 'grad_rwkv_w_lora_up': 'grad_w', 'grad_rwkv_a0': 'grad_w', 'grad_rwkv_a_lora_up': 'grad_w', 'grad_rwkv_g_lora_up': 'grad_w', 'grad_rwkv_k_k': 'grad_w', 'grad_rwkv_k_a': 'grad_w', 'grad_rwkv_r_k': 'grad_w', 'grad_rwkv_ln_w': 'grad_w', 'grad_rwkv_ln_b': 'grad_w', 'grad_attn_q_norm': 'grad_w', 'grad_attn_k_norm': 'grad_w', 'grad_attn_sinks': 'grad_w', 'grad_w_branch_rwkv': 'grad_w', 'grad_w_branch_attn': 'grad_w', 'grad_w_out': 'grad_w', 'grad_ffn2_norm': 'grad_w', 'grad_ffn2_w_gate': 'grad_w', 'grad_ffn2_w_up': 'grad_w', 'grad_ffn2_w_down': 'grad_w', 'grad_final_norm': 'grad_w', 'delta_ffn1_norm': 'delta_w', 'delta_ffn1_w_gate': 'delta_w', 'delta_ffn1_w_up': 'delta_w', 'delta_ffn1_w_down': 'delta_w', 'delta_mix_norm': 'delta_w', 'delta_w_in': 'delta_w', 'delta_rwkv_mu': 'delta_w', 'delta_rwkv_w0': 'delta_w', 'delta_rwkv_w_lora_up': 'delta_w', 'delta_rwkv_a0': 'delta_w', 'delta_rwkv_a_lora_up': 'delta_w', 'delta_rwkv_g_lora_up': 'delta_w', 'delta_rwkv_k_k': 'delta_w', 'delta_rwkv_k_a': 'delta_w', 'delta_rwkv_r_k': 'delta_w', 'delta_rwkv_ln_w': 'delta_w', 'delta_rwkv_ln_b': 'delta_w', 'delta_attn_q_norm': 'delta_w', 'delta_attn_k_norm': 'delta_w', 'delta_attn_sinks': 'delta_w', 'delta_w_branch_rwkv': 'delta_w', 'delta_w_branch_attn': 'delta_w', 'delta_w_out': 'delta_w', 'delta_ffn2_norm': 'delta_w', 'delta_ffn2_w_gate': 'delta_w', 'delta_ffn2_w_up': 'delta_w', 'delta_ffn2_w_down': 'delta_w', 'delta_final_norm': 'delta_w', 'new_m_ffn1_norm': 'new_m', 'new_m_ffn1_w_gate': 'new_m', 'new_m_ffn1_w_up': 'new_m', 'new_m_ffn1_w_down': 'new_m', 'new_m_mix_norm': 'new_m', 'new_m_w_in': 'new_m', 'new_m_rwkv_mu': 'new_m', 'new_m_rwkv_w0': 'new_m', 'new_m_rwkv_w_lora_up': 'new_m', 'new_m_rwkv_a0': 'new_m', 'new_m_rwkv_a_lora_up': 'new_m', 'new_m_rwkv_g_lora_up': 'new_m', 'new_m_rwkv_k_k': 'new_m', 'new_m_rwkv_k_a': 'new_m', 'new_m_rwkv_r_k': 'new_m', 'new_m_rwkv_ln_w': 'new_m', 'new_m_rwkv_ln_b': 'new_m', 'new_m_attn_q_norm': 'new_m', 'new_m_attn_k_norm': 'new_m', 'new_m_attn_sinks': 'new_m', 'new_m_w_branch_rwkv': 'new_m', 'new_m_w_branch_attn': 'new_m', 'new_m_w_out': 'new_m', 'new_m_ffn2_norm': 'new_m', 'new_m_ffn2_w_gate': 'new_m', 'new_m_ffn2_w_up': 'new_m', 'new_m_ffn2_w_down': 'new_m', 'new_m_final_norm': 'new_m', 'new_v_ffn1_norm': 'new_v', 'new_v_ffn1_w_gate': 'new_v', 'new_v_ffn1_w_up': 'new_v', 'new_v_ffn1_w_down': 'new_v', 'new_v_mix_norm': 'new_v', 'new_v_w_in': 'new_v', 'new_v_rwkv_mu': 'new_v', 'new_v_rwkv_w0': 'new_v', 'new_v_rwkv_w_lora_up': 'new_v', 'new_v_rwkv_a0': 'new_v', 'new_v_rwkv_a_lora_up': 'new_v', 'new_v_rwkv_g_lora_up': 'new_v', 'new_v_rwkv_k_k': 'new_v', 'new_v_rwkv_k_a': 'new_v', 'new_v_rwkv_r_k': 'new_v', 'new_v_rwkv_ln_w': 'new_v', 'new_v_rwkv_ln_b': 'new_v', 'new_v_attn_q_norm': 'new_v', 'new_v_attn_k_norm': 'new_v', 'new_v_attn_sinks': 'new_v', 'new_v_w_branch_rwkv': 'new_v', 'new_v_w_branch_attn': 'new_v', 'new_v_w_out': 'new_v', 'new_v_ffn2_norm': 'new_v', 'new_v_ffn2_w_gate': 'new_v', 'new_v_ffn2_w_up': 'new_v', 'new_v_ffn2_w_down': 'new_v', 'new_v_final_norm': 'new_v'}


def _forward(args):
    return _fwd_reference(*[args[k] for k in FWD_PARAMS])


def _output_shape():
    out = _jax.eval_shape(lambda: _forward(_fwd_setup_inputs(0)))
    return out.shape, out.dtype

N_MICROBATCH = 1
ADAM_LR = 0.001
ADAM_B1 = 0.9
ADAM_B2 = 0.999
ADAM_EPS = 1e-08
ADAM_WD = 0.01
ADAM_STEP = 10
PER_EXAMPLE_BATCH_AXIS = {'x': 0, 'loss_target': 0}
SHARED_INPUTS = []
_WEIGHT_DTYPES = {'ffn1_norm': _jnp.float32, 'ffn1_w_gate': _jnp.float32, 'ffn1_w_up': _jnp.float32, 'ffn1_w_down': _jnp.float32, 'mix_norm': _jnp.float32, 'w_in': _jnp.float32, 'rwkv_mu': _jnp.float32, 'rwkv_w0': _jnp.float32, 'rwkv_w_lora_up': _jnp.float32, 'rwkv_a0': _jnp.float32, 'rwkv_a_lora_up': _jnp.float32, 'rwkv_g_lora_up': _jnp.float32, 'rwkv_k_k': _jnp.float32, 'rwkv_k_a': _jnp.float32, 'rwkv_r_k': _jnp.float32, 'rwkv_ln_w': _jnp.float32, 'rwkv_ln_b': _jnp.float32, 'attn_q_norm': _jnp.float32, 'attn_k_norm': _jnp.float32, 'attn_sinks': _jnp.float32, 'w_branch_rwkv': _jnp.float32, 'w_branch_attn': _jnp.float32, 'w_out': _jnp.float32, 'ffn2_norm': _jnp.float32, 'ffn2_w_gate': _jnp.float32, 'ffn2_w_up': _jnp.float32, 'ffn2_w_down': _jnp.float32, 'final_norm': _jnp.float32}
MOMENT_SCALE = {'ffn1_norm': 9.258605e-02, 'ffn1_w_gate': 3.608384e-02, 'ffn1_w_up': 3.501579e-02, 'ffn1_w_down': 5.816156e-02, 'mix_norm': 1.060882e-01, 'w_in': 5.046595e-02, 'rwkv_mu': 1.229970e-01, 'rwkv_w0': 2.734694e-02, 'rwkv_w_lora_up': 3.071309e-03, 'rwkv_a0': 2.977016e-02, 'rwkv_a_lora_up': 2.841098e-02, 'rwkv_g_lora_up': 7.075079e-02, 'rwkv_k_k': 9.043002e-02, 'rwkv_k_a': 7.574944e-02, 'rwkv_r_k': 1.669356e-01, 'rwkv_ln_w': 6.883189e-02, 'rwkv_ln_b': 7.179176e-02, 'attn_q_norm': 6.645896e-02, 'attn_k_norm': 6.665851e-02, 'attn_sinks': 2.183231e-02, 'w_branch_rwkv': 5.051059e-02, 'w_branch_attn': 1.839800e-02, 'w_out': 5.302934e-02, 'ffn2_norm': 6.792882e-02, 'ffn2_w_gate': 2.923183e-02, 'ffn2_w_up': 2.835404e-02, 'ffn2_w_down': 4.686853e-02, 'final_norm': 3.194731e+01}


def _to_microbatches(a, axis):
    t = _jnp.moveaxis(a, axis, 0)
    t = t.reshape((N_MICROBATCH, t.shape[0] // N_MICROBATCH) + t.shape[1:])
    return _jnp.moveaxis(t, 1, axis + 1)


def setup_inputs(seed: int = 0) -> dict:
    inp = _fwd_setup_inputs(seed)
    key = _jax.random.fold_in(_jax.random.key(seed), 7919)
    shape, _ = _output_shape()
    out = dict(inp)
    out["loss_target"] = _jax.random.normal(_jax.random.fold_in(key, 0), shape, _jnp.float32)
    for i, name in enumerate(TWIN_WEIGHTS):
        w = inp[name].astype(_jnp.float32)
        if MOMENT_SCALE is None:
            s = _jnp.sqrt(_jnp.mean(_jnp.square(w)) + 1e-30)
        else:
            s = MOMENT_SCALE[name]
        km, kv = _jax.random.split(_jax.random.fold_in(key, i + 1))
        out[name] = w
        out["m_" + name] = s * _jax.random.normal(km, w.shape, _jnp.float32)
        out["v_" + name] = (s * s) * _jax.random.uniform(kv, w.shape, _jnp.float32, 0.5, 1.5)
    if N_MICROBATCH > 1:
        for name, axis in PER_EXAMPLE_BATCH_AXIS.items():
            out[name] = _to_microbatches(out[name], axis)
    return {'x': out['x'], 'ffn1_norm': out['ffn1_norm'], 'ffn1_w_gate': out['ffn1_w_gate'], 'ffn1_w_up': out['ffn1_w_up'], 'ffn1_w_down': out['ffn1_w_down'], 'mix_norm': out['mix_norm'], 'w_in': out['w_in'], 'rwkv_mu': out['rwkv_mu'], 'rwkv_w0': out['rwkv_w0'], 'rwkv_w_lora_up': out['rwkv_w_lora_up'], 'rwkv_a0': out['rwkv_a0'], 'rwkv_a_lora_up': out['rwkv_a_lora_up'], 'rwkv_g_lora_up': out['rwkv_g_lora_up'], 'rwkv_k_k': out['rwkv_k_k'], 'rwkv_k_a': out['rwkv_k_a'], 'rwkv_r_k': out['rwkv_r_k'], 'rwkv_ln_w': out['rwkv_ln_w'], 'rwkv_ln_b': out['rwkv_ln_b'], 'attn_q_norm': out['attn_q_norm'], 'attn_k_norm': out['attn_k_norm'], 'attn_sinks': out['attn_sinks'], 'w_branch_rwkv': out['w_branch_rwkv'], 'w_branch_attn': out['w_branch_attn'], 'w_out': out['w_out'], 'ffn2_norm': out['ffn2_norm'], 'ffn2_w_gate': out['ffn2_w_gate'], 'ffn2_w_up': out['ffn2_w_up'], 'ffn2_w_down': out['ffn2_w_down'], 'final_norm': out['final_norm'], 'loss_target': out['loss_target'], 'm_ffn1_norm': out['m_ffn1_norm'], 'm_ffn1_w_gate': out['m_ffn1_w_gate'], 'm_ffn1_w_up': out['m_ffn1_w_up'], 'm_ffn1_w_down': out['m_ffn1_w_down'], 'm_mix_norm': out['m_mix_norm'], 'm_w_in': out['m_w_in'], 'm_rwkv_mu': out['m_rwkv_mu'], 'm_rwkv_w0': out['m_rwkv_w0'], 'm_rwkv_w_lora_up': out['m_rwkv_w_lora_up'], 'm_rwkv_a0': out['m_rwkv_a0'], 'm_rwkv_a_lora_up': out['m_rwkv_a_lora_up'], 'm_rwkv_g_lora_up': out['m_rwkv_g_lora_up'], 'm_rwkv_k_k': out['m_rwkv_k_k'], 'm_rwkv_k_a': out['m_rwkv_k_a'], 'm_rwkv_r_k': out['m_rwkv_r_k'], 'm_rwkv_ln_w': out['m_rwkv_ln_w'], 'm_rwkv_ln_b': out['m_rwkv_ln_b'], 'm_attn_q_norm': out['m_attn_q_norm'], 'm_attn_k_norm': out['m_attn_k_norm'], 'm_attn_sinks': out['m_attn_sinks'], 'm_w_branch_rwkv': out['m_w_branch_rwkv'], 'm_w_branch_attn': out['m_w_branch_attn'], 'm_w_out': out['m_w_out'], 'm_ffn2_norm': out['m_ffn2_norm'], 'm_ffn2_w_gate': out['m_ffn2_w_gate'], 'm_ffn2_w_up': out['m_ffn2_w_up'], 'm_ffn2_w_down': out['m_ffn2_w_down'], 'm_final_norm': out['m_final_norm'], 'v_ffn1_norm': out['v_ffn1_norm'], 'v_ffn1_w_gate': out['v_ffn1_w_gate'], 'v_ffn1_w_up': out['v_ffn1_w_up'], 'v_ffn1_w_down': out['v_ffn1_w_down'], 'v_mix_norm': out['v_mix_norm'], 'v_w_in': out['v_w_in'], 'v_rwkv_mu': out['v_rwkv_mu'], 'v_rwkv_w0': out['v_rwkv_w0'], 'v_rwkv_w_lora_up': out['v_rwkv_w_lora_up'], 'v_rwkv_a0': out['v_rwkv_a0'], 'v_rwkv_a_lora_up': out['v_rwkv_a_lora_up'], 'v_rwkv_g_lora_up': out['v_rwkv_g_lora_up'], 'v_rwkv_k_k': out['v_rwkv_k_k'], 'v_rwkv_k_a': out['v_rwkv_k_a'], 'v_rwkv_r_k': out['v_rwkv_r_k'], 'v_rwkv_ln_w': out['v_rwkv_ln_w'], 'v_rwkv_ln_b': out['v_rwkv_ln_b'], 'v_attn_q_norm': out['v_attn_q_norm'], 'v_attn_k_norm': out['v_attn_k_norm'], 'v_attn_sinks': out['v_attn_sinks'], 'v_w_branch_rwkv': out['v_w_branch_rwkv'], 'v_w_branch_attn': out['v_w_branch_attn'], 'v_w_out': out['v_w_out'], 'v_ffn2_norm': out['v_ffn2_norm'], 'v_ffn2_w_gate': out['v_ffn2_w_gate'], 'v_ffn2_w_up': out['v_ffn2_w_up'], 'v_ffn2_w_down': out['v_ffn2_w_down'], 'v_final_norm': out['v_final_norm']}


def _loss(weights, diff, rest, loss_target):
    with _jax.named_scope("forward"):
        args = {**rest, TWIN_DIFF_INPUT: diff, **{k: w.astype(_WEIGHT_DTYPES[k]) for k, w in weights.items()}}
        y = _forward(args)
    with _jax.named_scope("loss_head"):
        err = _jnp.square(y.astype(_jnp.float32) - loss_target)
        return 0.5 * _jnp.sum(_jnp.mean(err, axis=-1)) if err.ndim else 0.5 * err


def _adamw(w, g, m, v):
    m = ADAM_B1 * m + (1.0 - ADAM_B1) * g
    v = ADAM_B2 * v + (1.0 - ADAM_B2) * _jnp.square(g)
    m_hat = m / (1.0 - ADAM_B1 ** ADAM_STEP)
    v_hat = v / (1.0 - ADAM_B2 ** ADAM_STEP)
    delta = -ADAM_LR * (m_hat / (_jnp.sqrt(v_hat) + ADAM_EPS) + ADAM_WD * w)
    return delta, m, v


def reference(x, ffn1_norm, ffn1_w_gate, ffn1_w_up, ffn1_w_down, mix_norm, w_in, rwkv_mu, rwkv_w0, rwkv_w_lora_up, rwkv_a0, rwkv_a_lora_up, rwkv_g_lora_up, rwkv_k_k, rwkv_k_a, rwkv_r_k, rwkv_ln_w, rwkv_ln_b, attn_q_norm, attn_k_norm, attn_sinks, w_branch_rwkv, w_branch_attn, w_out, ffn2_norm, ffn2_w_gate, ffn2_w_up, ffn2_w_down, final_norm, loss_target, m_ffn1_norm, m_ffn1_w_gate, m_ffn1_w_up, m_ffn1_w_down, m_mix_norm, m_w_in, m_rwkv_mu, m_rwkv_w0, m_rwkv_w_lora_up, m_rwkv_a0, m_rwkv_a_lora_up, m_rwkv_g_lora_up, m_rwkv_k_k, m_rwkv_k_a, m_rwkv_r_k, m_rwkv_ln_w, m_rwkv_ln_b, m_attn_q_norm, m_attn_k_norm, m_attn_sinks, m_w_branch_rwkv, m_w_branch_attn, m_w_out, m_ffn2_norm, m_ffn2_w_gate, m_ffn2_w_up, m_ffn2_w_down, m_final_norm, v_ffn1_norm, v_ffn1_w_gate, v_ffn1_w_up, v_ffn1_w_down, v_mix_norm, v_w_in, v_rwkv_mu, v_rwkv_w0, v_rwkv_w_lora_up, v_rwkv_a0, v_rwkv_a_lora_up, v_rwkv_g_lora_up, v_rwkv_k_k, v_rwkv_k_a, v_rwkv_r_k, v_rwkv_ln_w, v_rwkv_ln_b, v_attn_q_norm, v_attn_k_norm, v_attn_sinks, v_w_branch_rwkv, v_w_branch_attn, v_w_out, v_ffn2_norm, v_ffn2_w_gate, v_ffn2_w_up, v_ffn2_w_down, v_final_norm):
    given = dict(x=x, ffn1_norm=ffn1_norm, ffn1_w_gate=ffn1_w_gate, ffn1_w_up=ffn1_w_up, ffn1_w_down=ffn1_w_down, mix_norm=mix_norm, w_in=w_in, rwkv_mu=rwkv_mu, rwkv_w0=rwkv_w0, rwkv_w_lora_up=rwkv_w_lora_up, rwkv_a0=rwkv_a0, rwkv_a_lora_up=rwkv_a_lora_up, rwkv_g_lora_up=rwkv_g_lora_up, rwkv_k_k=rwkv_k_k, rwkv_k_a=rwkv_k_a, rwkv_r_k=rwkv_r_k, rwkv_ln_w=rwkv_ln_w, rwkv_ln_b=rwkv_ln_b, attn_q_norm=attn_q_norm, attn_k_norm=attn_k_norm, attn_sinks=attn_sinks, w_branch_rwkv=w_branch_rwkv, w_branch_attn=w_branch_attn, w_out=w_out, ffn2_norm=ffn2_norm, ffn2_w_gate=ffn2_w_gate, ffn2_w_up=ffn2_w_up, ffn2_w_down=ffn2_w_down, final_norm=final_norm, loss_target=loss_target, m_ffn1_norm=m_ffn1_norm, m_ffn1_w_gate=m_ffn1_w_gate, m_ffn1_w_up=m_ffn1_w_up, m_ffn1_w_down=m_ffn1_w_down, m_mix_norm=m_mix_norm, m_w_in=m_w_in, m_rwkv_mu=m_rwkv_mu, m_rwkv_w0=m_rwkv_w0, m_rwkv_w_lora_up=m_rwkv_w_lora_up, m_rwkv_a0=m_rwkv_a0, m_rwkv_a_lora_up=m_rwkv_a_lora_up, m_rwkv_g_lora_up=m_rwkv_g_lora_up, m_rwkv_k_k=m_rwkv_k_k, m_rwkv_k_a=m_rwkv_k_a, m_rwkv_r_k=m_rwkv_r_k, m_rwkv_ln_w=m_rwkv_ln_w, m_rwkv_ln_b=m_rwkv_ln_b, m_attn_q_norm=m_attn_q_norm, m_attn_k_norm=m_attn_k_norm, m_attn_sinks=m_attn_sinks, m_w_branch_rwkv=m_w_branch_rwkv, m_w_branch_attn=m_w_branch_attn, m_w_out=m_w_out, m_ffn2_norm=m_ffn2_norm, m_ffn2_w_gate=m_ffn2_w_gate, m_ffn2_w_up=m_ffn2_w_up, m_ffn2_w_down=m_ffn2_w_down, m_final_norm=m_final_norm, v_ffn1_norm=v_ffn1_norm, v_ffn1_w_gate=v_ffn1_w_gate, v_ffn1_w_up=v_ffn1_w_up, v_ffn1_w_down=v_ffn1_w_down, v_mix_norm=v_mix_norm, v_w_in=v_w_in, v_rwkv_mu=v_rwkv_mu, v_rwkv_w0=v_rwkv_w0, v_rwkv_w_lora_up=v_rwkv_w_lora_up, v_rwkv_a0=v_rwkv_a0, v_rwkv_a_lora_up=v_rwkv_a_lora_up, v_rwkv_g_lora_up=v_rwkv_g_lora_up, v_rwkv_k_k=v_rwkv_k_k, v_rwkv_k_a=v_rwkv_k_a, v_rwkv_r_k=v_rwkv_r_k, v_rwkv_ln_w=v_rwkv_ln_w, v_rwkv_ln_b=v_rwkv_ln_b, v_attn_q_norm=v_attn_q_norm, v_attn_k_norm=v_attn_k_norm, v_attn_sinks=v_attn_sinks, v_w_branch_rwkv=v_w_branch_rwkv, v_w_branch_attn=v_w_branch_attn, v_w_out=v_w_out, v_ffn2_norm=v_ffn2_norm, v_ffn2_w_gate=v_ffn2_w_gate, v_ffn2_w_up=v_ffn2_w_up, v_ffn2_w_down=v_ffn2_w_down, v_final_norm=v_final_norm)
    weights = {n: given[n] for n in TWIN_WEIGHTS}
    shared = {n: given[n] for n in SHARED_INPUTS}
    per_example = {n: given[n] for n in ['x']}
    grad_fn = _jax.value_and_grad(_loss, argnums=(0, 1))

    def one_microbatch(ex, loss_target):
        ex = dict(ex)
        diff = ex.pop(TWIN_DIFF_INPUT)
        return grad_fn(weights, diff, {**shared, **ex}, loss_target)

    if N_MICROBATCH == 1:
        loss, (grad_w, grad_x) = one_microbatch(per_example, given["loss_target"])
    else:
        def body(carry, xs):
            loss_sum, grad_sum = carry
            l_k, (gw_k, gx_k) = one_microbatch(xs[0], xs[1])
            with _jax.named_scope("update"):
                return (loss_sum + l_k, _jax.tree.map(_jnp.add, grad_sum, gw_k)), gx_k

        init = (_jnp.zeros((), _jnp.float32), _jax.tree.map(_jnp.zeros_like, weights))
        (loss, grad_w), grad_x = _jax.lax.scan(body, init, (per_example, given["loss_target"]))
    with _jax.named_scope("update"):
        delta_w, new_m, new_v = {}, {}, {}
        for n in TWIN_WEIGHTS:
            delta_w[n], new_m[n], new_v[n] = _adamw(weights[n], grad_w[n], given["m_" + n], given["v_" + n])
    return (loss, grad_x, *[grad_w[n] for n in TWIN_WEIGHTS], *[delta_w[n] for n in TWIN_WEIGHTS],
            *[new_m[n] for n in TWIN_WEIGHTS], *[new_v[n] for n in TWIN_WEIGHTS])
```

```python
import functools

import jax
import jax.numpy as jnp
from jax import lax
from jax.experimental import pallas as pl
from jax.experimental.pallas import tpu as pltpu

F32 = jnp.float32
BF16 = jnp.bfloat16

D_MODEL = 1024
D_FF = 2816
HEAD_DIM = 64
N_HEADS = 8
RW = 512
KVW = 128
ATT_GROUP = 4
WINDOW = 128
BLOCK = 128
DECAY_LORA, ICLR_LORA, GATE_LORA = 32, 32, 96
RWKV_COLS = 3 * RW + DECAY_LORA + ICLR_LORA + GATE_LORA
ATT_COLS = RW + 2 * KVW
GATE_COLS = 2 * D_MODEL
RWKV_PAD = 3 * RW + 3 * 128
RMS_EPS = 1e-6
GN_EPS = 64e-5
N_CHIPS = 4
LANE = 128
FLAT_W = 1024
FLAT_ALIGN = 16
NEG_BIG = -1e30

ADAM_LR, ADAM_B1, ADAM_B2, ADAM_EPS, ADAM_WD, ADAM_STEP = 0.001, 0.9, 0.999, 1e-08, 0.01, 10

VMEM_LIMIT = 56 * 1024 * 1024

WEIGHT_NAMES = ['ffn1_norm', 'ffn1_w_gate', 'ffn1_w_up', 'ffn1_w_down', 'mix_norm', 'w_in', 'rwkv_mu', 'rwkv_w0',
                'rwkv_w_lora_up', 'rwkv_a0', 'rwkv_a_lora_up', 'rwkv_g_lora_up', 'rwkv_k_k', 'rwkv_k_a', 'rwkv_r_k',
                'rwkv_ln_w', 'rwkv_ln_b', 'attn_q_norm', 'attn_k_norm', 'attn_sinks', 'w_branch_rwkv',
                'w_branch_attn', 'w_out', 'ffn2_norm', 'ffn2_w_gate', 'ffn2_w_up', 'ffn2_w_down', 'final_norm']
BIG = [('ffn1_w_gate', 1), ('ffn1_w_up', 1), ('ffn1_w_down', 0), ('w_in', 1), ('rwkv_w_lora_up', 1),
       ('rwkv_a_lora_up', 1), ('rwkv_g_lora_up', 1), ('w_branch_rwkv', 1), ('w_branch_attn', 1), ('w_out', 0),
       ('ffn2_w_gate', 1), ('ffn2_w_up', 1), ('ffn2_w_down', 0)]
SMALL = ['ffn1_norm', 'mix_norm', 'rwkv_mu', 'rwkv_w0', 'rwkv_a0', 'rwkv_k_k', 'rwkv_k_a', 'rwkv_r_k', 'rwkv_ln_w',
         'rwkv_ln_b', 'attn_q_norm', 'attn_k_norm', 'attn_sinks', 'ffn2_norm', 'final_norm']


def _pcall(body, **kw):
    return pl.pallas_call(body, **kw)


def _params(sem=None, **kw):
    if sem is not None:
        kw['dimension_semantics'] = sem
    return pltpu.CompilerParams(vmem_limit_bytes=VMEM_LIMIT, **kw)


def _tile(n, cap, mult):
    best = None
    for t in range(mult, min(n, cap) + 1, mult):
        if n % t == 0:
            best = t
    return best or n


def _sigmoid(z):
    return 1.0 / (1.0 + jnp.exp(-z))


def _softplus(z):
    return jnp.maximum(z, 0.0) + jnp.log(1.0 + jnp.exp(-jnp.abs(z)))


def _bdot(a, b, dims=(((1,), (0,)), ((), ()))):
    return lax.dot_general(a.astype(BF16), b.astype(BF16), dims, preferred_element_type=F32)


_NT = (((1,), (1,)), ((), ()))
_TN = (((0,), (0,)), ((), ()))


def _block_diag(n, seg):
    r = lax.broadcasted_iota(jnp.int32, (n, n), 0) // seg
    c = lax.broadcasted_iota(jnp.int32, (n, n), 1) // seg
    return jnp.where(r == c, 1.0, 0.0).astype(BF16)


def _segsum(x, bd):
    hi = x.astype(BF16)
    r1 = x - hi.astype(F32)
    mid = r1.astype(BF16)
    lo = (r1 - mid.astype(F32)).astype(BF16)
    dot = functools.partial(lax.dot_general, dimension_numbers=(((1,), (0,)), ((), ())), preferred_element_type=F32)
    return dot(hi, bd) + dot(mid, bd) + dot(lo, bd)


_LORA_EDGES = (3 * RW, 3 * RW + DECAY_LORA, 3 * RW + DECAY_LORA + ICLR_LORA, RWKV_COLS)


def _pad_rwkv_cols(x):
    parts = [x[..., :3 * RW]]
    for lo, hi in zip(_LORA_EDGES[:-1], _LORA_EDGES[1:]):
        parts.append(jnp.pad(x[..., lo:hi], [(0, 0)] * (x.ndim - 1) + [(0, 128 - (hi - lo))]))
    return jnp.concatenate(parts, axis=-1)


def _unpad_rwkv_cols(x):
    parts = [x[..., :3 * RW]]
    for j, (lo, hi) in enumerate(zip(_LORA_EDGES[:-1], _LORA_EDGES[1:])):
        parts.append(x[..., 3 * RW + 128 * j:3 * RW + 128 * j + (hi - lo)])
    return jnp.concatenate(parts, axis=-1)


def _pad_rows(x, rows):
    return jnp.pad(x, [(0, rows - x.shape[0])] + [(0, 0)] * (x.ndim - 1))


def mm(a, b, *, name, ta=False, tb=False, scale=None, res=None, out_dtype=F32):
    M, K = (a.shape[1], a.shape[0]) if ta else a.shape
    N = b.shape[0] if tb else b.shape[1]
    assert (b.shape[1] if tb else b.shape[0]) == K
    tm, tn, tk = _tile(M, 512, 128), _tile(N, 1408, 128), _tile(K, 1408, 128)
    nk = K // tk
    dims = (((0 if ta else 1,), (1 if tb else 0,)), ((), ()))

    def body(*refs):
        if res is None:
            a_ref, b_ref, o_ref, acc_ref = refs
            r_ref = None
        else:
            a_ref, b_ref, r_ref, o_ref, acc_ref = refs
        k = pl.program_id(2)
        part = _bdot(a_ref[...], b_ref[...], dims)

        @pl.when(k == 0)
        def _():
            acc_ref[...] = part

        @pl.when(k > 0)
        def _():
            acc_ref[...] += part

        @pl.when(k == nk - 1)
        def _():
            o = acc_ref[...]
            if scale is not None:
                o = o * scale
            if r_ref is not None:
                o = o + r_ref[...].astype(F32)
            o_ref[...] = o.astype(out_dtype)

    a_spec = pl.BlockSpec((tk, tm), lambda i, j, k: (k, i)) if ta else pl.BlockSpec((tm, tk), lambda i, j, k: (i, k))
    b_spec = pl.BlockSpec((tn, tk), lambda i, j, k: (j, k)) if tb else pl.BlockSpec((tk, tn), lambda i, j, k: (k, j))
    o_spec = pl.BlockSpec((tm, tn), lambda i, j, k: (i, j))
    in_specs = [a_spec, b_spec] + ([o_spec] if res is not None else [])
    args = (a, b) + ((res,) if res is not None else ())
    return _pcall(
        body, name=name, grid=(M // tm, N // tn, nk), in_specs=in_specs, out_specs=o_spec,
        out_shape=jax.ShapeDtypeStruct((M, N), out_dtype), scratch_shapes=[pltpu.VMEM((tm, tn), F32)],
        compiler_params=_params(("parallel", "parallel", "arbitrary")),
    )(*args)


def _row_spec(tr, c):
    return pl.BlockSpec((tr, c), lambda i: (i, 0))


def _full_spec(shape):
    return pl.BlockSpec(shape, lambda i: (0,) * len(shape))


def _acc_rows(ref, val, i):
    @pl.when(i == 0)
    def _():
        ref[...] = val

    @pl.when(i > 0)
    def _():
        ref[...] += val


def rms_fwd(x, g, *, name):
    T, D = x.shape
    tr = _tile(T, 512, 8)

    def body(x_ref, g_ref, h_ref):
        xv = x_ref[...]
        r = lax.rsqrt(jnp.mean(xv * xv, axis=-1, keepdims=True) + RMS_EPS)
        h_ref[...] = (xv * r * g_ref[...]).astype(BF16)

    return _pcall(body, name=name, grid=(T // tr,), in_specs=[_row_spec(tr, D), _full_spec((1, D))],
                  out_specs=_row_spec(tr, D), out_shape=jax.ShapeDtypeStruct((T, D), BF16),
                  compiler_params=_params(("parallel",)))(x, g)


def rms_bwd(dh, x, g, res, *, name):
    T, D = x.shape
    tr = _tile(T, 256, 8)

    def body(dh_ref, x_ref, g_ref, res_ref, dx_ref, dg_ref):
        i = pl.program_id(0)
        xv, dhv = x_ref[...], dh_ref[...].astype(F32)
        r = lax.rsqrt(jnp.mean(xv * xv, axis=-1, keepdims=True) + RMS_EPS)
        xh = xv * r
        dxh = dhv * g_ref[...]
        dx_ref[...] = res_ref[...] + r * (dxh - xh * jnp.mean(dxh * xh, axis=-1, keepdims=True))
        _acc_rows(dg_ref, jnp.sum(dhv * xh, axis=0, keepdims=True), i)

    return _pcall(body, name=name, grid=(T // tr,),
                  in_specs=[_row_spec(tr, D), _row_spec(tr, D), _full_spec((1, D)), _row_spec(tr, D)],
                  out_specs=[_row_spec(tr, D), _full_spec((1, D))],
                  out_shape=[jax.ShapeDtypeStruct((T, D), F32), jax.ShapeDtypeStruct((1, D), F32)],
                  compiler_params=_params(("arbitrary",)))(dh, x, g, res)


def final_loss(x, tgt, g, *, name):
    T, D = x.shape
    tr = _tile(T, 256, 8)

    def body(x_ref, t_ref, g_ref, dx_ref, dg_ref, loss_ref):
        i = pl.program_id(0)
        xv = x_ref[...]
        r = lax.rsqrt(jnp.mean(xv * xv, axis=-1, keepdims=True) + RMS_EPS)
        xh = xv * r
        e = xh * g_ref[...] - t_ref[...]
        part = 0.5 * jnp.sum(jnp.mean(e * e, axis=-1, keepdims=True), axis=0, keepdims=True)
        dy = e * (1.0 / D)
        dxh = dy * g_ref[...]
        dx_ref[...] = r * (dxh - xh * jnp.mean(dxh * xh, axis=-1, keepdims=True))
        _acc_rows(dg_ref, jnp.sum(dy * xh, axis=0, keepdims=True), i)
        _acc_rows(loss_ref, jnp.broadcast_to(part, (1, LANE)), i)

    return _pcall(body, name=name, grid=(T // tr,),
                  in_specs=[_row_spec(tr, D), _row_spec(tr, D), _full_spec((1, D))],
                  out_specs=[_row_spec(tr, D), _full_spec((1, D)), _full_spec((1, LANE))],
                  out_shape=[jax.ShapeDtypeStruct((T, D), F32), jax.ShapeDtypeStruct((1, D), F32),
                             jax.ShapeDtypeStruct((1, LANE), F32)],
                  compiler_params=_params(("arbitrary",)))(x, tgt, g)


def swiglu_fwd(gate, up, *, name):
    T, F = gate.shape
    tr = _tile(T, 256, 8)

    def body(g_ref, u_ref, a_ref):
        gv = g_ref[...]
        a_ref[...] = (gv * _sigmoid(gv) * u_ref[...]).astype(BF16)

    return _pcall(body, name=name, grid=(T // tr,), in_specs=[_row_spec(tr, F), _row_spec(tr, F)],
                  out_specs=_row_spec(tr, F), out_shape=jax.ShapeDtypeStruct((T, F), BF16),
                  compiler_params=_params(("parallel",)))(gate, up)


def swiglu_bwd(da, gate, up, *, name):
    T, F = gate.shape
    tr = _tile(T, 256, 8)

    def body(da_ref, g_ref, u_ref, dg_ref, du_ref):
        gv, dav = g_ref[...], da_ref[...]
        s = _sigmoid(gv)
        du_ref[...] = (dav * gv * s).astype(BF16)
        dg_ref[...] = (dav * u_ref[...] * s * (1.0 + gv * (1.0 - s))).astype(BF16)

    return _pcall(body, name=name, grid=(T // tr,), in_specs=[_row_spec(tr, F)] * 3,
                  out_specs=[_row_spec(tr, F)] * 2, out_shape=[jax.ShapeDtypeStruct((T, F), BF16)] * 2,
                  compiler_params=_params(("parallel",)))(da, gate, up)


def merge_fwd(br, ba, pg, *, name):
    T, D = br.shape
    tr = _tile(T, 256, 8)

    def body(br_ref, ba_ref, pg_ref, o_ref):
        pgv = pg_ref[...]
        o_ref[...] = (_sigmoid(pgv[:, :D]) * br_ref[...] + _sigmoid(pgv[:, D:]) * ba_ref[...]).astype(BF16)

    return _pcall(body, name=name, grid=(T // tr,), in_specs=[_row_spec(tr, D), _row_spec(tr, D), _row_spec(tr, 2 * D)],
                  out_specs=_row_spec(tr, D), out_shape=jax.ShapeDtypeStruct((T, D), BF16),
                  compiler_params=_params(("parallel",)))(br, ba, pg)


def merge_bwd(dm, br, ba, pg, *, name):
    T, D = br.shape
    tr = _tile(T, 256, 8)

    def body(dm_ref, br_ref, ba_ref, pg_ref, dbr_ref, dba_ref, dpg_ref):
        pgv, dmv = pg_ref[...], dm_ref[...]
        sr, sa = _sigmoid(pgv[:, :D]), _sigmoid(pgv[:, D:])
        dbr_ref[...] = (dmv * sr).astype(BF16)
        dba_ref[...] = (dmv * sa).astype(BF16)
        dpg_ref[:, :D] = dmv * br_ref[...] * sr * (1.0 - sr)
        dpg_ref[:, D:] = dmv * ba_ref[...] * sa * (1.0 - sa)

    return _pcall(body, name=name, grid=(T // tr,),
                  in_specs=[_row_spec(tr, D), _row_spec(tr, D), _row_spec(tr, D), _row_spec(tr, 2 * D)],
                  out_specs=[_row_spec(tr, D), _row_spec(tr, D), _row_spec(tr, 2 * D)],
                  out_shape=[jax.ShapeDtypeStruct((T, D), BF16), jax.ShapeDtypeStruct((T, D), BF16),
                             jax.ShapeDtypeStruct((T, 2 * D), F32)],
                  compiler_params=_params(("parallel",)))(dm, br, ba, pg)


def _rwkv_mix(p, prev, mu, w0, a0, k_k, k_a, wlw, wla, wlg, bd):
    pp = p + (prev - p) * mu
    r, k, v = pp[:, 0:RW], pp[:, RW:2 * RW], pp[:, 2 * RW:3 * RW]
    xw, xa, xg = pp[:, 3 * RW:3 * RW + 128], pp[:, 3 * RW + 128:3 * RW + 256], pp[:, 3 * RW + 256:3 * RW + 384]
    th = jnp.tanh(xw)
    z = -(w0 + _bdot(th, wlw))
    e = jnp.exp(-_softplus(z) - 0.5)
    decay = jnp.exp(-e)
    a = _sigmoid(a0 + _bdot(xa, wla))
    sg = _sigmoid(xg)
    kkr = k * k_k
    n = jnp.sqrt(_segsum(kkr * kkr, bd))
    kk = kkr / jnp.maximum(n, 1e-12)
    k2 = k * (1.0 + (a - 1.0) * k_a)
    return dict(r=r, k=k, v=v, xa=xa, th=th, z=z, e=e, decay=decay, a=a, sg=sg, n=n, kk=kk, k2=k2)


def _seg_matrix(n, shift):
    r = lax.shift_right_logical(lax.broadcasted_iota(jnp.int32, (n, n), 0), shift)
    c = lax.shift_right_logical(lax.broadcasted_iota(jnp.int32, (n, n), 1), shift)
    return jnp.where(r == c, 1.0, 0.0).astype(BF16)


def rwkv_pre_fwd(p, pshift, mu, w0, a0, k_k, k_a, wlw, wla, wlg, *, name):
    T = p.shape[0]
    tr = _tile(T, 256, 8)

    def body(p_ref, ps_ref, mu_ref, w0_ref, a0_ref, kk_ref, ka_ref, wlw_ref, wla_ref, wlg_ref,
             r_ref, w_ref, k_ref, v_ref, a_ref, b_ref, g_ref):
        pv, prev = p_ref[...], ps_ref[...]
        m = _rwkv_mix(pv, prev, mu_ref[...], w0_ref[...], a0_ref[...], kk_ref[...], ka_ref[...],
                      wlw_ref[...], wla_ref[...], wlg_ref[...], _seg_matrix(RW, 6))
        r_ref[...] = m['r']
        w_ref[...] = m['decay']
        k_ref[...] = m['k2']
        v_ref[...] = m['v']
        a_ref[...] = -m['kk']
        b_ref[...] = m['kk'] * m['a']
        g_ref[...] = m['sg']

    vec = _row_spec(tr, RW)
    return _pcall(
        body, name=name, grid=(T // tr,),
        in_specs=[_row_spec(tr, RWKV_PAD), _row_spec(tr, RWKV_PAD), _full_spec((1, RWKV_PAD))] + [_full_spec((1, RW))] * 4
        + [_full_spec((128, RW))] * 3,
        out_specs=[vec] * 6 + [_row_spec(tr, 128)],
        out_shape=[jax.ShapeDtypeStruct((T, RW), F32)] * 6 + [jax.ShapeDtypeStruct((T, 128), F32)],
        compiler_params=_params(("parallel",)),
    )(p, pshift, mu, w0, a0, k_k, k_a, wlw, wla, wlg)


def _group_norm(y, bd):
    mean = _segsum(y, bd) * (1.0 / HEAD_DIM)
    yc = y - mean
    rstd = lax.rsqrt(_segsum(yc * yc, bd) * (1.0 / HEAD_DIM) + GN_EPS)
    return yc * rstd, rstd


def rwkv_post_fwd(y, r, k2, v, sg, wlg, ln_w, ln_b, r_k, *, name):
    T = y.shape[0]
    tr = _tile(T, 256, 8)

    def body(y_ref, r_ref, k_ref, v_ref, sg_ref, wlg_ref, lw_ref, lb_ref, rk_ref, o_ref):
        bd = _seg_matrix(RW, 6)
        yn, _ = _group_norm(y_ref[...], bd)
        s = _segsum(r_ref[...] * k_ref[...] * rk_ref[...], bd)
        g = _bdot(sg_ref[...], wlg_ref[...])
        o_ref[...] = ((yn * lw_ref[...] + lb_ref[...] + s * v_ref[...]) * g).astype(BF16)

    vec = _row_spec(tr, RW)
    return _pcall(body, name=name, grid=(T // tr,),
                  in_specs=[vec] * 4 + [_row_spec(tr, 128), _full_spec((128, RW))] + [_full_spec((1, RW))] * 3, out_specs=vec,
                  out_shape=jax.ShapeDtypeStruct((T, RW), BF16), compiler_params=_params(("parallel",)))(
                      y, r, k2, v, sg, wlg, ln_w, ln_b, r_k)


def rwkv_post_bwd(dyr, y, r, k2, v, sg, wlg, ln_w, ln_b, r_k, *, name):
    T = y.shape[0]
    tr = _tile(T, 256, 8)

    def body(dyr_ref, y_ref, r_ref, k_ref, v_ref, sg_ref, wlg_ref, lw_ref, lb_ref, rk_ref,
             dy_ref, dz_ref, dg_ref, dlw_ref, dlb_ref):
        i = pl.program_id(0)
        bd = _seg_matrix(RW, 6)
        yn, rstd = _group_norm(y_ref[...], bd)
        s = _segsum(r_ref[...] * k_ref[...] * rk_ref[...], bd)
        dyrv = dyr_ref[...]
        dg_ref[...] = dyrv * (yn * lw_ref[...] + lb_ref[...] + s * v_ref[...])
        dz = dyrv * _bdot(sg_ref[...], wlg_ref[...])
        dz_ref[...] = dz
        dyn = dz * lw_ref[...]
        inv = 1.0 / HEAD_DIM
        dy_ref[...] = rstd * (dyn - _segsum(dyn, bd) * inv - yn * (_segsum(dyn * yn, bd) * inv))
        _acc_rows(dlw_ref, jnp.sum(dz * yn, axis=0, keepdims=True), i)
        _acc_rows(dlb_ref, jnp.sum(dz, axis=0, keepdims=True), i)

    vec = _row_spec(tr, RW)
    one = _full_spec((1, RW))
    return _pcall(body, name=name, grid=(T // tr,),
                  in_specs=[vec] * 5 + [_row_spec(tr, 128), _full_spec((128, RW))] + [one] * 3, out_specs=[vec] * 3 + [one] * 2,
                  out_shape=[jax.ShapeDtypeStruct((T, RW), F32)] * 3 + [jax.ShapeDtypeStruct((1, RW), F32)] * 2,
                  compiler_params=_params(("arbitrary",)))(dyr, y, r, k2, v, sg, wlg, ln_w, ln_b, r_k)


def rwkv_pre_bwd(p, pshift, dr_w, dw_w, dk_w, dv_w, da_w, db_w, dz, dg, mu, w0, a0, k_k, k_a, r_k, wlw, wla, wlg, *, name):
    T = p.shape[0]
    tr = _tile(T, 256, 8)
    n = T // tr

    def body(p_ref, ps_ref, dr_ref, dw_ref, dk_ref, dv_ref, da_ref, db_ref, dz_ref, dg_ref,
             mu_ref, w0_ref, a0_ref, kk_ref, ka_ref, rk_ref, wlw_ref, wla_ref, wlg_ref,
             dp_ref, dmu_ref, dw0_ref, da0_ref, dkk_ref, dka_ref, drk_ref, dwlw_ref, dwla_ref, dwlg_ref,
             carry, dpp, acc_w, acc_a, acc_g):
        i = pl.program_id(0)

        @pl.when(i == 0)
        def _():
            carry[...] = jnp.zeros_like(carry)

        pv, prev, mu = p_ref[...], ps_ref[...], mu_ref[...]
        bd = _seg_matrix(RW, 6)
        k_k, k_a, r_k = kk_ref[...], ka_ref[...], rk_ref[...]
        m = _rwkv_mix(pv, prev, mu, w0_ref[...], a0_ref[...], k_k, k_a, wlw_ref[...], wla_ref[...], wlg_ref[...], bd)
        r, k, v, a, kk, k2 = m['r'], m['k'], m['v'], m['a'], m['kk'], m['k2']
        dzv, dgv = dz_ref[...], dg_ref[...]
        s = _segsum(r * k2 * r_k, bd)
        ds = _segsum(dzv * v, bd)
        dr = dr_ref[...] + ds * k2 * r_k
        dk2 = dk_ref[...] + ds * r * r_k
        dv = dv_ref[...] + dzv * s
        dbv = db_ref[...]
        dkk = dbv * a - da_ref[...]
        da = dbv * kk + dk2 * k * k_a
        dk = dk2 * (1.0 + (a - 1.0) * k_a)
        nmax = jnp.maximum(m['n'], 1e-12)
        dkkr = jnp.where(m['n'] > 1e-12, dkk - kk * _segsum(dkk * kk, bd), dkk) / nmax
        dk = dk + dkkr * k_k
        dapre = da * a * (1.0 - a)
        dwpre = dw_ref[...] * m['decay'] * (-m['e']) * _sigmoid(m['z'])
        dth = _bdot(dwpre, wlw_ref[...], _NT)
        dxa = _bdot(dapre, wla_ref[...], _NT)
        dsg = _bdot(dgv, wlg_ref[...], _NT)
        dpp[:, 0:RW] = dr
        dpp[:, RW:2 * RW] = dk
        dpp[:, 2 * RW:3 * RW] = dv
        dpp[:, 3 * RW:3 * RW + 128] = dth * (1.0 - m['th'] * m['th'])
        dpp[:, 3 * RW + 128:3 * RW + 256] = dxa
        dpp[:, 3 * RW + 256:3 * RW + 384] = dsg * m['sg'] * (1.0 - m['sg'])
        d = dpp[...]
        zed = d * mu
        last = lax.broadcasted_iota(jnp.int32, pv.shape, 0) == tr - 1
        dp_ref[...] = d * (1.0 - mu) + jnp.where(last, carry[0:1, :], pltpu.roll(zed, tr - 1, 0))
        carry[...] = zed[0:8, :]

        def colsum(x):
            return jnp.sum(x, axis=0, keepdims=True)

        _acc_rows(dmu_ref, colsum(d * (prev - pv)), i)
        _acc_rows(dw0_ref, colsum(dwpre), i)
        _acc_rows(da0_ref, colsum(dapre), i)
        _acc_rows(dkk_ref, colsum(dkkr * k), i)
        _acc_rows(dka_ref, colsum(dk2 * k * (a - 1.0)), i)
        _acc_rows(drk_ref, colsum(ds * r * k2), i)
        _acc_rows(acc_w, _bdot(m['th'], dwpre, _TN), i)
        _acc_rows(acc_a, _bdot(m['xa'], dapre, _TN), i)
        _acc_rows(acc_g, _bdot(m['sg'], dgv, _TN), i)

        @pl.when(i == n - 1)
        def _():
            dwlw_ref[...] = acc_w[...]
            dwla_ref[...] = acc_a[...]
            dwlg_ref[...] = acc_g[...]

    rev = lambda c: pl.BlockSpec((tr, c), lambda i: (n - 1 - i, 0))
    one, lora = _full_spec((1, RW)), _full_spec((128, RW))
    return _pcall(
        body, name=name, grid=(n,),
        in_specs=[rev(RWKV_PAD), rev(RWKV_PAD)] + [rev(RW)] * 8 + [_full_spec((1, RWKV_PAD))] + [one] * 5 + [lora] * 3,
        out_specs=[rev(RWKV_PAD), _full_spec((1, RWKV_PAD))] + [one] * 5 + [lora] * 3,
        out_shape=[jax.ShapeDtypeStruct((T, RWKV_PAD), F32), jax.ShapeDtypeStruct((1, RWKV_PAD), F32)]
        + [jax.ShapeDtypeStruct((1, RW), F32)] * 5 + [jax.ShapeDtypeStruct((128, RW), F32)] * 3,
        scratch_shapes=[pltpu.VMEM((8, RWKV_PAD), F32), pltpu.VMEM((tr, RWKV_PAD), F32)] + [pltpu.VMEM((128, RW), F32)] * 3,
        compiler_params=_params(("arbitrary",)),
    )(p, pshift, dr_w, dw_w, dk_w, dv_w, da_w, db_w, dz, dg, mu, w0, a0, k_k, k_a, r_k, wlw, wla, wlg)


def _qk_norm(x, g, bd):
    r = lax.rsqrt(_segsum(x * x, bd) * (1.0 / HEAD_DIM) + RMS_EPS)
    return x * r * g, r


def _att_mask(i):
    qi = lax.broadcasted_iota(jnp.int32, (BLOCK, 2 * BLOCK), 0)
    kj = lax.broadcasted_iota(jnp.int32, (BLOCK, 2 * BLOCK), 1)
    band = (kj <= qi + BLOCK) & (kj > qi + BLOCK - WINDOW)
    return band & ((kj >= BLOCK) | (i > 0))


def _att_probs(qh, kh, mask, sink):
    s = _bdot(qh, kh, _NT) * (HEAD_DIM ** -0.5)
    s = jnp.where(mask, s, NEG_BIG)
    m = jnp.maximum(jnp.max(s, axis=-1, keepdims=True), sink)
    pexp = jnp.exp(s - m)
    psink = jnp.exp(sink - m)
    inv = 1.0 / (jnp.sum(pexp, axis=-1, keepdims=True) + psink)
    return pexp * inv, psink * inv


def _att_blocks(n):
    cur = pl.BlockSpec((BLOCK, ATT_COLS), lambda i: (i, 0))
    prev = pl.BlockSpec((BLOCK, ATT_COLS), lambda i: (jnp.maximum(i - 1, 0), 0))
    return cur, prev


def _att_qkv(cur, prev, qn_g, kn_g):
    bq, bk = _seg_matrix(RW, 6), _seg_matrix(KVW, 6)
    qn, rq = _qk_norm(cur[:, 0:RW], qn_g, bq)
    kcur, rkc = _qk_norm(cur[:, RW:RW + KVW], kn_g, bk)
    kprev, _ = _qk_norm(prev[:, RW:RW + KVW], kn_g, bk)
    kc = jnp.concatenate([kprev, kcur], axis=0)
    vc = jnp.concatenate([prev[:, RW + KVW:], cur[:, RW + KVW:]], axis=0)
    return qn, rq, kc, vc, rkc


def att_fwd(pa, qn_g, kn_g, sinks, *, name):
    T = pa.shape[0]
    n = T // BLOCK

    def body(cur_ref, prev_ref, qg_ref, kg_ref, sk_ref, o_ref):
        i = pl.program_id(0)
        qn, _, kc, vc, _ = _att_qkv(cur_ref[...], prev_ref[...], qg_ref[...], kg_ref[...])
        mask = _att_mask(i)
        outs = []
        for h in range(N_HEADS):
            kv = slice((h // ATT_GROUP) * HEAD_DIM, (h // ATT_GROUP + 1) * HEAD_DIM)
            probs, _ = _att_probs(qn[:, h * HEAD_DIM:(h + 1) * HEAD_DIM], kc[:, kv], mask, sk_ref[0:1, h:h + 1])
            outs.append(_bdot(probs, vc[:, kv]))
        o_ref[...] = jnp.concatenate(outs, axis=1)

    cur, prev = _att_blocks(n)
    return _pcall(body, name=name, grid=(n,),
                  in_specs=[cur, prev, _full_spec((1, RW)), _full_spec((1, KVW)), _full_spec((1, LANE))],
                  out_specs=pl.BlockSpec((BLOCK, RW), lambda i: (i, 0)), out_shape=jax.ShapeDtypeStruct((T, RW), F32),
                  compiler_params=_params(("parallel",)))(pa, pa, qn_g, kn_g, sinks)


def att_bwd(pa, do, qn_g, kn_g, sinks, *, name):
    T = pa.shape[0]
    n = T // BLOCK

    def body(cur_ref, prev_ref, do_ref, qg_ref, kg_ref, sk_ref,
             dq_ref, dko_ref, dkn_ref, dvo_ref, dvn_ref, dqg_ref, dsk_ref):
        i = pl.program_id(0)
        cur = cur_ref[...]
        qn, rq, kc, vc, _ = _att_qkv(cur, prev_ref[...], qg_ref[...], kg_ref[...])
        mask = _att_mask(i)
        dov = do_ref[...]
        lane = lax.broadcasted_iota(jnp.int32, (1, LANE), 1)
        dsink = jnp.zeros((1, LANE), F32)
        dqn, dkc, dvc = [], [None, None], [None, None]
        for h in range(N_HEADS):
            g = h // ATT_GROUP
            kv = slice(g * HEAD_DIM, (g + 1) * HEAD_DIM)
            qh, doh = qn[:, h * HEAD_DIM:(h + 1) * HEAD_DIM], dov[:, h * HEAD_DIM:(h + 1) * HEAD_DIM]
            probs, psink = _att_probs(qh, kc[:, kv], mask, sk_ref[0:1, h:h + 1])
            dprobs = _bdot(doh, vc[:, kv], _NT)
            delta = jnp.sum(probs * dprobs, axis=-1, keepdims=True)
            ds = probs * (dprobs - delta) * (HEAD_DIM ** -0.5)
            dsink = dsink + jnp.where(lane == h, -jnp.sum(psink * delta, axis=0, keepdims=True), 0.0)
            dqn.append(_bdot(ds, kc[:, kv]))
            dk_h, dv_h = _bdot(ds, qh, _TN), _bdot(probs, doh, _TN)
            dkc[g] = dk_h if dkc[g] is None else dkc[g] + dk_h
            dvc[g] = dv_h if dvc[g] is None else dvc[g] + dv_h
        dqn = jnp.concatenate(dqn, axis=1)
        dk = jnp.concatenate(dkc, axis=1)
        dv = jnp.concatenate(dvc, axis=1)
        dkn_ref[...], dko_ref[...] = dk[0:BLOCK], dk[BLOCK:]
        dvn_ref[...], dvo_ref[...] = dv[0:BLOCK], dv[BLOCK:]
        qhat = cur[:, 0:RW] * rq
        dqh = dqn * qg_ref[...]
        dq_ref[...] = rq * (dqh - qhat * (_segsum(dqh * qhat, _seg_matrix(RW, 6)) * (1.0 / HEAD_DIM)))
        prod = dqn * qhat
        fold = prod[:, 0:HEAD_DIM]
        for h in range(1, N_HEADS):
            fold = fold + prod[:, h * HEAD_DIM:(h + 1) * HEAD_DIM]
        _acc_rows(dqg_ref, jnp.sum(fold, axis=0, keepdims=True), i)
        _acc_rows(dsk_ref, dsink, i)

    cur, prev = _att_blocks(n)
    kvb = pl.BlockSpec((BLOCK, KVW), lambda i: (i, 0))
    qb = pl.BlockSpec((BLOCK, RW), lambda i: (i, 0))
    return _pcall(body, name=name, grid=(n,),
                  in_specs=[cur, prev, qb, _full_spec((1, RW)), _full_spec((1, KVW)), _full_spec((1, LANE))],
                  out_specs=[qb, kvb, kvb, kvb, kvb, _full_spec((1, HEAD_DIM)), _full_spec((1, LANE))],
                  out_shape=[jax.ShapeDtypeStruct((T, RW), F32)] + [jax.ShapeDtypeStruct((T, KVW), F32)] * 4
                  + [jax.ShapeDtypeStruct((1, HEAD_DIM), F32), jax.ShapeDtypeStruct((1, LANE), F32)],
                  compiler_params=_params(("arbitrary",)))(pa, pa, do, qn_g, kn_g, sinks)


def att_kv_bwd(pa, dq, dko, dkn, dvo, dvn, kn_g, *, name):
    T = pa.shape[0]
    n = T // BLOCK

    def body(pa_ref, dq_ref, dko_ref, dkn_ref, dvo_ref, dvn_ref, kg_ref, dpa_ref, dkg_ref):
        i = pl.program_id(0)
        more = i < n - 1
        dkn_tot = dko_ref[...] + jnp.where(more, dkn_ref[...], 0.0)
        dv_tot = dvo_ref[...] + jnp.where(more, dvn_ref[...], 0.0)
        kraw = pa_ref[:, RW:RW + KVW]
        bk = _seg_matrix(KVW, 6)
        _, rk = _qk_norm(kraw, kg_ref[...], bk)
        khat = kraw * rk
        dkh = dkn_tot * kg_ref[...]
        dpa_ref[:, 0:RW] = dq_ref[...]
        dpa_ref[:, RW:RW + KVW] = rk * (dkh - khat * (_segsum(dkh * khat, bk) * (1.0 / HEAD_DIM)))
        dpa_ref[:, RW + KVW:] = dv_tot
        prod = dkn_tot * khat
        _acc_rows(dkg_ref, jnp.sum(prod[:, 0:HEAD_DIM] + prod[:, HEAD_DIM:], axis=0, keepdims=True), i)

    kvb = pl.BlockSpec((BLOCK, KVW), lambda i: (i, 0))
    nxt = pl.BlockSpec((BLOCK, KVW), lambda i: (jnp.minimum(i + 1, n - 1), 0))
    return _pcall(body, name=name, grid=(n,),
                  in_specs=[pl.BlockSpec((BLOCK, ATT_COLS), lambda i: (i, 0)), pl.BlockSpec((BLOCK, RW), lambda i: (i, 0)),
                            kvb, nxt, kvb, nxt, _full_spec((1, KVW))],
                  out_specs=[pl.BlockSpec((BLOCK, ATT_COLS), lambda i: (i, 0)), _full_spec((1, HEAD_DIM))],
                  out_shape=[jax.ShapeDtypeStruct((T, ATT_COLS), F32), jax.ShapeDtypeStruct((1, HEAD_DIM), F32)],
                  compiler_params=_params(("arbitrary",)))(pa, dq, dko, dkn, dvo, dvn, kn_g)


WKV_CHUNK = 64
WKV_GROUP = 8


def _eye64():
    return lax.broadcasted_iota(jnp.int32, (HEAD_DIM, HEAD_DIM), 0) == lax.broadcasted_iota(jnp.int32, (HEAD_DIM, HEAD_DIM), 1)


def _to_col(row, eye):
    return jnp.sum(jnp.where(eye, row, 0.0), axis=1, keepdims=True)


def _to_row(col, eye):
    return jnp.sum(jnp.where(eye, col, 0.0), axis=0, keepdims=True)


def wkv_fwd(r, w, k, v, a, b, *, name):
    T = r.shape[0]
    ch = min(WKV_CHUNK, T)
    ngroups = ch // WKV_GROUP

    def body(r_ref, w_ref, k_ref, v_ref, a_ref, b_ref, y_ref, st_ref, s_scr):
        @pl.when(pl.program_id(0) == 0)
        def _():
            s_scr[...] = jnp.zeros_like(s_scr)

        eye = _eye64()

        def group(gi, carry):
            t0 = pl.multiple_of(gi * WKV_GROUP, WKV_GROUP)
            rows = pl.ds(t0, WKV_GROUP)
            R, W, K, V, A, B = (ref[rows, :] for ref in (r_ref, w_ref, k_ref, v_ref, a_ref, b_ref))
            tiles = []
            for h in range(N_HEADS):
                sl = slice(h * HEAD_DIM, (h + 1) * HEAD_DIM)
                S = s_scr[h]
                yrows = []
                for s in range(WKV_GROUP):
                    sa = jnp.sum(S * A[s:s + 1, sl], axis=1, keepdims=True)
                    S = S * W[s:s + 1, sl] + sa * B[s:s + 1, sl] + _to_col(V[s:s + 1, sl], eye) * K[s:s + 1, sl]
                    st_ref[t0 + s, h] = S
                    yrows.append(_to_row(jnp.sum(S * R[s:s + 1, sl], axis=1, keepdims=True), eye))
                s_scr[h] = S
                tiles.append(jnp.concatenate(yrows, axis=0))
            y_ref[rows, :] = jnp.concatenate(tiles, axis=1)
            return carry

        lax.fori_loop(0, ngroups, group, 0)

    vec = pl.BlockSpec((ch, RW), lambda c: (c, 0))
    return _pcall(
        body, name=name, grid=(T // ch,), in_specs=[vec] * 6,
        out_specs=[vec, pl.BlockSpec((ch, N_HEADS, HEAD_DIM, HEAD_DIM), lambda c: (c, 0, 0, 0))],
        out_shape=[jax.ShapeDtypeStruct((T, RW), F32), jax.ShapeDtypeStruct((T, N_HEADS, HEAD_DIM, HEAD_DIM), F32)],
        scratch_shapes=[pltpu.VMEM((N_HEADS, HEAD_DIM, HEAD_DIM), F32)],
        compiler_params=_params(("arbitrary",)),
    )(r, w, k, v, a, b)


def wkv_bwd(r, w, k, v, a, b, dy, states, *, name):
    T = r.shape[0]
    ch = min(WKV_CHUNK, T)
    nchunks = T // ch
    ngroups = ch // WKV_GROUP

    def body(r_ref, w_ref, k_ref, v_ref, a_ref, b_ref, dy_ref, st_ref, stp_ref,
             dr_ref, dw_ref, dk_ref, dv_ref, da_ref, db_ref, ds_scr):
        step = pl.program_id(0)

        @pl.when(step == 0)
        def _():
            ds_scr[...] = jnp.zeros_like(ds_scr)

        has_prev_chunk = step < nchunks - 1
        eye = _eye64()

        def group(gj, carry):
            gi = ngroups - 1 - gj
            t0 = pl.multiple_of(gi * WKV_GROUP, WKV_GROUP)
            rows = pl.ds(t0, WKV_GROUP)
            R, W, K, V, A, B, DY = (ref[rows, :] for ref in (r_ref, w_ref, k_ref, v_ref, a_ref, b_ref, dy_ref))
            outs = [[] for _ in range(6)]
            for h in range(N_HEADS):
                sl = slice(h * HEAD_DIM, (h + 1) * HEAD_DIM)
                dS = ds_scr[h]
                St = st_ref[t0 + WKV_GROUP - 1, h]
                before = jnp.where(gi > 0, st_ref[jnp.maximum(t0 - 1, 0), h],
                                   jnp.where(has_prev_chunk, stp_ref[0, h], 0.0))
                got = [[None] * WKV_GROUP for _ in range(6)]
                for s in reversed(range(WKV_GROUP)):
                    Sp = st_ref[t0 + s - 1, h] if s > 0 else before
                    r_row, w_row, k_row = R[s:s + 1, sl], W[s:s + 1, sl], K[s:s + 1, sl]
                    a_row, b_row = A[s:s + 1, sl], B[s:s + 1, sl]
                    dy_col = _to_col(DY[s:s + 1, sl], eye)
                    v_col = _to_col(V[s:s + 1, sl], eye)
                    dS = dS + dy_col * r_row
                    got[0][s] = jnp.sum(St * dy_col, axis=0, keepdims=True)
                    got[3][s] = _to_row(jnp.sum(dS * k_row, axis=1, keepdims=True), eye)
                    got[2][s] = jnp.sum(dS * v_col, axis=0, keepdims=True)
                    sa = jnp.sum(Sp * a_row, axis=1, keepdims=True)
                    dsa = jnp.sum(dS * b_row, axis=1, keepdims=True)
                    got[5][s] = jnp.sum(dS * sa, axis=0, keepdims=True)
                    got[1][s] = jnp.sum(dS * Sp, axis=0, keepdims=True)
                    got[4][s] = jnp.sum(Sp * dsa, axis=0, keepdims=True)
                    dS = dS * w_row + dsa * a_row
                    St = Sp
                ds_scr[h] = dS
                for q in range(6):
                    outs[q].append(jnp.concatenate(got[q], axis=0))
            for q, ref in enumerate((dr_ref, dw_ref, dk_ref, dv_ref, da_ref, db_ref)):
                ref[rows, :] = jnp.concatenate(outs[q], axis=1)
            return carry

        lax.fori_loop(0, ngroups, group, 0)

    vec = pl.BlockSpec((ch, RW), lambda c: (nchunks - 1 - c, 0))
    st_spec = pl.BlockSpec((ch, N_HEADS, HEAD_DIM, HEAD_DIM), lambda c: (nchunks - 1 - c, 0, 0, 0))
    stp_spec = pl.BlockSpec((1, N_HEADS, HEAD_DIM, HEAD_DIM),
                            lambda c: (jnp.maximum((nchunks - 1 - c) * ch - 1, 0), 0, 0, 0))
    return _pcall(
        body, name=name, grid=(nchunks,), in_specs=[vec] * 7 + [st_spec, stp_spec], out_specs=[vec] * 6,
        out_shape=[jax.ShapeDtypeStruct((T, RW), F32)] * 6,
        scratch_shapes=[pltpu.VMEM((N_HEADS, HEAD_DIM, HEAD_DIM), F32)],
        compiler_params=_params(("arbitrary",)),
    )(r, w, k, v, a, b, dy, states, states)


_HBM = pl.BlockSpec(memory_space=pltpu.HBM)
_MESH = pl.DeviceIdType.MESH


def _place():
    x, y, c = lax.axis_index("x"), lax.axis_index("y"), lax.axis_index("c")
    return x, y, c, [(1 - x, y), (x, 1 - y), (1 - x, 1 - y)]


def all_gather_chips(flat, *, name):
    R, W = flat.shape

    def body(w_ref, o_ref, send_sems, recv_sems, local_sem):
        x, y, c, chips = _place()
        me = 2 * x + y

        def copy(j, block, to):
            return pltpu.make_async_remote_copy(src_ref=w_ref, dst_ref=o_ref.at[block], send_sem=send_sems.at[j],
                                                recv_sem=recv_sems.at[j], device_id=to, device_id_type=_MESH)

        mine = pltpu.make_async_copy(w_ref, o_ref.at[me], local_sem)
        mine.start()
        sends = [copy(j, me, (qx, qy, c)) for j, (qx, qy) in enumerate(chips)]
        for cp in sends:
            cp.start()
        for j, (qx, qy) in enumerate(chips):
            copy(j, 2 * qx + qy, (qx, qy, c)).wait_recv()
        for cp in sends:
            cp.wait_send()
        mine.wait()

    return _pcall(body, name=name, in_specs=[_HBM], out_specs=_HBM, out_shape=jax.ShapeDtypeStruct((N_CHIPS, R, W), flat.dtype),
                  scratch_shapes=[pltpu.SemaphoreType.DMA((3,)), pltpu.SemaphoreType.DMA((3,)), pltpu.SemaphoreType.DMA(())],
                  compiler_params=_params())(flat)


def exchange_chips(blocks, *, name):
    _, R, W = blocks.shape

    def body(g_ref, o_ref, send_sems, recv_sems, local_sem):
        x, y, c, chips = _place()
        me = 2 * x + y

        def copy(j, src_block, dst_block, to):
            return pltpu.make_async_remote_copy(src_ref=g_ref.at[src_block], dst_ref=o_ref.at[dst_block],
                                                send_sem=send_sems.at[j], recv_sem=recv_sems.at[j], device_id=to,
                                                device_id_type=_MESH)

        mine = pltpu.make_async_copy(g_ref.at[me], o_ref.at[me], local_sem)
        mine.start()
        sends = [copy(j, 2 * qx + qy, me, (qx, qy, c)) for j, (qx, qy) in enumerate(chips)]
        for cp in sends:
            cp.start()
        for j, (qx, qy) in enumerate(chips):
            copy(j, me, 2 * qx + qy, (qx, qy, c)).wait_recv()
        for cp in sends:
            cp.wait_send()
        mine.wait()

    return _pcall(body, name=name, in_specs=[_HBM], out_specs=_HBM, out_shape=jax.ShapeDtypeStruct(blocks.shape, blocks.dtype),
                  scratch_shapes=[pltpu.SemaphoreType.DMA((3,)), pltpu.SemaphoreType.DMA((3,)), pltpu.SemaphoreType.DMA(())],
                  compiler_params=_params())(blocks)


def exchange_pair(flat, *, name):
    def body(p_ref, o_ref, send_sem, recv_sem):
        x, y, c, _ = _place()
        cp = pltpu.make_async_remote_copy(src_ref=p_ref, dst_ref=o_ref, send_sem=send_sem, recv_sem=recv_sem,
                                          device_id=(x, y, 1 - c), device_id_type=_MESH)
        cp.start()
        cp.wait_recv()
        cp.wait_send()

    return _pcall(body, name=name, in_specs=[_HBM], out_specs=_HBM, out_shape=jax.ShapeDtypeStruct(flat.shape, flat.dtype),
                  scratch_shapes=[pltpu.SemaphoreType.DMA(()), pltpu.SemaphoreType.DMA(())],
                  compiler_params=_params())(flat)


def sum_blocks(land, *, name):
    _, R, W = land.shape
    tr = _tile(R, 256, 8)

    def body(l_ref, o_ref):
        o_ref[...] = ((l_ref[0] + l_ref[1]) + l_ref[2]) + l_ref[3]

    return _pcall(body, name=name, grid=(R // tr,), in_specs=[pl.BlockSpec((N_CHIPS, tr, W), lambda i: (0, i, 0))],
                  out_specs=_row_spec(tr, W), out_shape=jax.ShapeDtypeStruct((R, W), F32),
                  compiler_params=_params(("parallel",)))(land)


def adamw(w, m, v, ga, gb, *, name):
    R, W = w.shape
    tr = _tile(R, 256, 8)

    def body(w_ref, m_ref, v_ref, ga_ref, gb_ref, g_ref, d_ref, nm_ref, nv_ref):
        g = ga_ref[...] + gb_ref[...]
        g_ref[...] = g
        nm = ADAM_B1 * m_ref[...] + (1.0 - ADAM_B1) * g
        nv = ADAM_B2 * v_ref[...] + (1.0 - ADAM_B2) * (g * g)
        nm_ref[...] = nm
        nv_ref[...] = nv
        m_hat = nm / (1.0 - ADAM_B1 ** ADAM_STEP)
        v_hat = nv / (1.0 - ADAM_B2 ** ADAM_STEP)
        d_ref[...] = -ADAM_LR * (m_hat / (jnp.sqrt(v_hat) + ADAM_EPS) + ADAM_WD * w_ref[...])

    spec = _row_spec(tr, W)
    return _pcall(body, name=name, grid=(R // tr,), in_specs=[spec] * 5, out_specs=[spec] * 4,
                  out_shape=[jax.ShapeDtypeStruct((R, W), F32)] * 4, compiler_params=_params(("parallel",)))(w, m, v, ga, gb)


def _part_rows(numel):
    per = FLAT_W * FLAT_ALIGN
    return -(-numel // per) * FLAT_ALIGN


def _pack(parts, lead=()):
    out = []
    for p in parts:
        flat = p.reshape(lead + (-1,))
        rows = _part_rows(flat.shape[-1])
        flat = jnp.pad(flat, [(0, 0)] * len(lead) + [(0, rows * FLAT_W - flat.shape[-1])])
        out.append(flat.reshape(lead + (rows, FLAT_W)))
    return jnp.concatenate(out, axis=len(lead))


def _unpack(flat, shapes, lead=()):
    out, off = [], 0
    for shp in shapes:
        numel = 1
        for s in shp:
            numel *= s
        rows = _part_rows(numel)
        part = flat[..., off:off + rows, :].reshape(lead + (rows * FLAT_W,))[..., :numel]
        out.append(part.reshape(lead + tuple(shp)))
        off += rows
    return out


def _to_blocks(full, axis):
    r, c = full.shape
    if axis == 1:
        return full.reshape(r, N_CHIPS, c // N_CHIPS).transpose(1, 0, 2)
    return full.reshape(N_CHIPS, r // N_CHIPS, c)


def _from_blocks(blocks, axis):
    _, r, c = blocks.shape
    if axis == 1:
        return blocks.transpose(1, 0, 2).reshape(r, N_CHIPS * c)
    return blocks.reshape(N_CHIPS * r, c)


def _ffn_fwd(x, norm, wg, wu, wd, tag):
    h = rms_fwd(x, norm, name=tag + "_norm")
    gate = mm(h, wg, name=tag + "_gate")
    up = mm(h, wu, name=tag + "_up")
    act = swiglu_fwd(gate, up, name=tag + "_act")
    out = mm(act, wd, scale=0.5, res=x, name=tag + "_down")
    return out, (h, gate, up, act)


def _ffn_bwd(dout, x, saved, norm, wg, wu, wd, tag):
    h, gate, up, act = saved
    dact = mm(dout, wd, tb=True, scale=0.5, name=tag + "_dact")
    dwd = mm(act, dout, ta=True, scale=0.5, name=tag + "_dwd")
    dgate, dup = swiglu_bwd(dact, gate, up, name=tag + "_dswiglu")
    dwg = mm(h, dgate, ta=True, name=tag + "_dwg")
    dwu = mm(h, dup, ta=True, name=tag + "_dwu")
    dh = mm(dgate, wg, tb=True, name=tag + "_dh_gate")
    dh = mm(dup, wu, tb=True, res=dh, name=tag + "_dh_up")
    dx, dnorm = rms_bwd(dh, x, norm, dout, name=tag + "_dnorm")
    return dx, dnorm, dwg, dwu, dwd


def _step(A):
    x, tgt = A['x'][0], A['loss_target'][0]
    T = x.shape[0]
    w = {n: A[n][0] for n in WEIGHT_NAMES}
    row = lambda a: a.reshape(1, -1)

    big_shapes = [w[n].shape for n, _ in BIG]
    gathered = all_gather_chips(_pack([w[n].astype(BF16) for n, _ in BIG]), name="gather_weights")
    full = {n: _from_blocks(b, axis) for (n, axis), b in zip(BIG, _unpack(gathered, big_shapes, lead=(N_CHIPS,)))}
    w_in_r = _pad_rwkv_cols(full['w_in'][:, :RWKV_COLS])
    w_in_a = full['w_in'][:, RWKV_COLS:RWKV_COLS + ATT_COLS]
    w_in_g = full['w_in'][:, RWKV_COLS + ATT_COLS:]
    wlw, wla, wlg = (_pad_rows(full[n], 128).astype(F32) for n in ('rwkv_w_lora_up', 'rwkv_a_lora_up', 'rwkv_g_lora_up'))
    mu = _pad_rwkv_cols(row(w['rwkv_mu']))
    w0, a0, k_k, k_a, r_k, ln_w, ln_b = (row(w[n]) for n in ('rwkv_w0', 'rwkv_a0', 'rwkv_k_k', 'rwkv_k_a', 'rwkv_r_k',
                                                               'rwkv_ln_w', 'rwkv_ln_b'))
    qg = jnp.tile(row(w['attn_q_norm']), (1, N_HEADS))
    kg = jnp.tile(row(w['attn_k_norm']), (1, KVW // HEAD_DIM))
    sinks = jnp.pad(row(w['attn_sinks']), ((0, 0), (0, LANE - N_HEADS)))
    n1, nmix, n2, nfin = (row(w[n]) for n in ('ffn1_norm', 'mix_norm', 'ffn2_norm', 'final_norm'))

    x1, ffn1 = _ffn_fwd(x, n1, full['ffn1_w_gate'], full['ffn1_w_up'], full['ffn1_w_down'], "ffn1")
    h2 = rms_fwd(x1, nmix, name="mix_norm")
    pr = mm(h2, w_in_r, name="proj_rwkv")
    pa = mm(h2, w_in_a, name="proj_att")
    pg = mm(h2, w_in_g, name="proj_gate")
    pr_shift = jnp.pad(pr, ((1, 0), (0, 0)))[:-1]
    r, dec, k2, v, a, b, sg = rwkv_pre_fwd(pr, pr_shift, mu, w0, a0, k_k, k_a, wlw, wla, wlg, name="rwkv_pre")
    y, states = wkv_fwd(r, dec, k2, v, a, b, name="wkv_fwd")
    yr = rwkv_post_fwd(y, r, k2, v, sg, wlg, ln_w, ln_b, r_k, name="rwkv_post")
    ya = att_fwd(pa, qg, kg, sinks, name="att_fwd")
    br = mm(yr, full['w_branch_rwkv'], name="branch_rwkv")
    ba = mm(ya, full['w_branch_attn'], name="branch_att")
    mg = merge_fwd(br, ba, pg, name="merge")
    x2 = mm(mg, full['w_out'], res=x1, name="mix_out")
    x3, ffn2 = _ffn_fwd(x2, n2, full['ffn2_w_gate'], full['ffn2_w_up'], full['ffn2_w_down'], "ffn2")
    dx3, d_nfin, loss = final_loss(x3, tgt, nfin, name="final_loss")

    G = {'final_norm': d_nfin}
    dx2, G['ffn2_norm'], G['ffn2_w_gate'], G['ffn2_w_up'], G['ffn2_w_down'] = _ffn_bwd(
        dx3, x2, ffn2, n2, full['ffn2_w_gate'], full['ffn2_w_up'], full['ffn2_w_down'], "ffn2")
    dmg = mm(dx2, full['w_out'], tb=True, name="d_merge")
    G['w_out'] = mm(mg, dx2, ta=True, name="d_w_out")
    dbr, dba, dpg = merge_bwd(dmg, br, ba, pg, name="merge_bwd")
    dyr = mm(dbr, full['w_branch_rwkv'], tb=True, name="d_y_rwkv")
    G['w_branch_rwkv'] = mm(yr, dbr, ta=True, name="d_w_branch_rwkv")
    dya = mm(dba, full['w_branch_attn'], tb=True, name="d_y_att")
    G['w_branch_attn'] = mm(ya, dba, ta=True, name="d_w_branch_att")
    dy, dz, dg, G['rwkv_ln_w'], G['rwkv_ln_b'] = rwkv_post_bwd(dyr, y, r, k2, v, sg, wlg, ln_w, ln_b, r_k, name="rwkv_post_bwd")
    wkv_grads = wkv_bwd(r, dec, k2, v, a, b, dy, states, name="wkv_bwd")
    (dpr, d_mu, G['rwkv_w0'], G['rwkv_a0'], G['rwkv_k_k'], G['rwkv_k_a'], G['rwkv_r_k'], d_wlw, d_wla, d_wlg) = rwkv_pre_bwd(
        pr, pr_shift, *wkv_grads, dz, dg, mu, w0, a0, k_k, k_a, r_k, wlw, wla, wlg, name="rwkv_pre_bwd")
    G['rwkv_mu'] = _unpad_rwkv_cols(d_mu)
    G['rwkv_w_lora_up'], G['rwkv_a_lora_up'], G['rwkv_g_lora_up'] = d_wlw[:DECAY_LORA], d_wla[:ICLR_LORA], d_wlg[:GATE_LORA]
    dq, dko, dkn, dvo, dvn, G['attn_q_norm'], d_sinks = att_bwd(pa, dya, qg, kg, sinks, name="att_bwd")
    G['attn_sinks'] = d_sinks[:, :N_HEADS]
    dpa, G['attn_k_norm'] = att_kv_bwd(pa, dq, dko, dkn, dvo, dvn, kg, name="att_kv_bwd")
    d_w_in_r = mm(h2, dpr, ta=True, name="d_w_in_rwkv")
    d_w_in_a = mm(h2, dpa, ta=True, name="d_w_in_att")
    d_w_in_g = mm(h2, dpg, ta=True, name="d_w_in_gate")
    G['w_in'] = jnp.concatenate([_unpad_rwkv_cols(d_w_in_r), d_w_in_a, d_w_in_g], axis=1)
    dh2 = mm(dpr, w_in_r, tb=True, name="d_h2_rwkv")
    dh2 = mm(dpa, w_in_a, tb=True, res=dh2, name="d_h2_att")
    dh2 = mm(dpg, w_in_g, tb=True, res=dh2, name="d_h2_gate")
    dx1, G['mix_norm'] = rms_bwd(dh2, x1, nmix, dx2, name="d_mix_norm")
    dx0, G['ffn1_norm'], G['ffn1_w_gate'], G['ffn1_w_up'], G['ffn1_w_down'] = _ffn_bwd(
        dx1, x, ffn1, n1, full['ffn1_w_gate'], full['ffn1_w_up'], full['ffn1_w_down'], "ffn1")

    small_shapes = [(w[n].size,) for n in SMALL] + [(1,)]
    small = jnp.concatenate([G[n].reshape(-1) for n in SMALL] + [loss[0, :1]])
    small4 = jnp.broadcast_to(small[None], (N_CHIPS, small.shape[0]))
    blocks = _pack([_to_blocks(G[n], axis) for n, axis in BIG] + [small4], lead=(N_CHIPS,))
    landed = exchange_chips(blocks, name="exchange_grads")
    mine = sum_blocks(landed, name="sum_chips")
    theirs = exchange_pair(mine, name="exchange_cores")

    def local(prefix):
        parts = [A[prefix + n][0] for n, _ in BIG]
        parts.append(jnp.concatenate([A[prefix + n].reshape(-1) for n in SMALL] + [jnp.zeros((1,), F32)]))
        return _pack(parts)

    flat_out = adamw(local(''), local('m_'), local('v_'), mine, theirs, name="adamw")
    shapes = big_shapes + [(small.shape[0],)]
    result = {}
    for kind, flat in zip(('grad_', 'delta_', 'new_m_', 'new_v_'), flat_out):
        parts = _unpack(flat, shapes)
        for (n, _), part in zip(BIG, parts[:-1]):
            result[kind + n] = part[None]
        for n, part in zip(SMALL + ['loss'], _unpack_vec(parts[-1], small_shapes)):
            result[kind + n] = part.reshape(A[n].shape) if n != 'loss' else part.reshape(())
    outs = [result['grad_loss'], dx0[None]]
    for kind in ('grad_', 'delta_', 'new_m_', 'new_v_'):
        outs += [result[kind + n] for n in WEIGHT_NAMES]
    return tuple(outs)


def _unpack_vec(vec, shapes):
    out, off = [], 0
    for (n,) in shapes:
        out.append(vec[off:off + n])
        off += n
    return out


def kernel(x, ffn1_norm, ffn1_w_gate, ffn1_w_up, ffn1_w_down, mix_norm, w_in, rwkv_mu, rwkv_w0, rwkv_w_lora_up, rwkv_a0, rwkv_a_lora_up, rwkv_g_lora_up, rwkv_k_k, rwkv_k_a, rwkv_r_k, rwkv_ln_w, rwkv_ln_b, attn_q_norm, attn_k_norm, attn_sinks, w_branch_rwkv, w_branch_attn, w_out, ffn2_norm, ffn2_w_gate, ffn2_w_up, ffn2_w_down, final_norm, loss_target, m_ffn1_norm, m_ffn1_w_gate, m_ffn1_w_up, m_ffn1_w_down, m_mix_norm, m_w_in, m_rwkv_mu, m_rwkv_w0, m_rwkv_w_lora_up, m_rwkv_a0, m_rwkv_a_lora_up, m_rwkv_g_lora_up, m_rwkv_k_k, m_rwkv_k_a, m_rwkv_r_k, m_rwkv_ln_w, m_rwkv_ln_b, m_attn_q_norm, m_attn_k_norm, m_attn_sinks, m_w_branch_rwkv, m_w_branch_attn, m_w_out, m_ffn2_norm, m_ffn2_w_gate, m_ffn2_w_up, m_ffn2_w_down, m_final_norm, v_ffn1_norm, v_ffn1_w_gate, v_ffn1_w_up, v_ffn1_w_down, v_mix_norm, v_w_in, v_rwkv_mu, v_rwkv_w0, v_rwkv_w_lora_up, v_rwkv_a0, v_rwkv_a_lora_up, v_rwkv_g_lora_up, v_rwkv_k_k, v_rwkv_k_a, v_rwkv_r_k, v_rwkv_ln_w, v_rwkv_ln_b, v_attn_q_norm, v_attn_k_norm, v_attn_sinks, v_w_branch_rwkv, v_w_branch_attn, v_w_out, v_ffn2_norm, v_ffn2_w_gate, v_ffn2_w_up, v_ffn2_w_down, v_final_norm):
    return _step(dict(locals()))
```

```python
import functools

import jax
import jax.numpy as jnp
from jax import lax
from jax.experimental import pallas as pl
from jax.experimental.pallas import tpu as pltpu

F32 = jnp.float32
BF16 = jnp.bfloat16

D_MODEL = 1024
D_FF = 2816
HEAD_DIM = 64
N_HEADS = 8
RW = 512
KVW = 128
ATT_GROUP = 4
WINDOW = 128
BLOCK = 128
DECAY_LORA, ICLR_LORA, GATE_LORA = 32, 32, 96
RWKV_COLS = 3 * RW + DECAY_LORA + ICLR_LORA + GATE_LORA
ATT_COLS = RW + 2 * KVW
GATE_COLS = 2 * D_MODEL
RWKV_PAD = 3 * RW + 3 * 128
RMS_EPS = 1e-6
GN_EPS = 64e-5
N_CHIPS = 4
LANE = 128
FLAT_W = 1024
FLAT_ALIGN = 16
NEG_BIG = -1e30

ADAM_LR, ADAM_B1, ADAM_B2, ADAM_EPS, ADAM_WD, ADAM_STEP = 0.001, 0.9, 0.999, 1e-08, 0.01, 10

VMEM_LIMIT = 56 * 1024 * 1024

WEIGHT_NAMES = ['ffn1_norm', 'ffn1_w_gate', 'ffn1_w_up', 'ffn1_w_down', 'mix_norm', 'w_in', 'rwkv_mu', 'rwkv_w0',
                'rwkv_w_lora_up', 'rwkv_a0', 'rwkv_a_lora_up', 'rwkv_g_lora_up', 'rwkv_k_k', 'rwkv_k_a', 'rwkv_r_k',
                'rwkv_ln_w', 'rwkv_ln_b', 'attn_q_norm', 'attn_k_norm', 'attn_sinks', 'w_branch_rwkv',
                'w_branch_attn', 'w_out', 'ffn2_norm', 'ffn2_w_gate', 'ffn2_w_up', 'ffn2_w_down', 'final_norm']
BIG = [('ffn1_w_gate', 1), ('ffn1_w_up', 1), ('ffn1_w_down', 0), ('w_in', 1), ('rwkv_w_lora_up', 1),
       ('rwkv_a_lora_up', 1), ('rwkv_g_lora_up', 1), ('w_branch_rwkv', 1), ('w_branch_attn', 1), ('w_out', 0),
       ('ffn2_w_gate', 1), ('ffn2_w_up', 1), ('ffn2_w_down', 0)]
SMALL = ['ffn1_norm', 'mix_norm', 'rwkv_mu', 'rwkv_w0', 'rwkv_a0', 'rwkv_k_k', 'rwkv_k_a', 'rwkv_r_k', 'rwkv_ln_w',
         'rwkv_ln_b', 'attn_q_norm', 'attn_k_norm', 'attn_sinks', 'ffn2_norm', 'final_norm']


def _pcall(body, **kw):
    return pl.pallas_call(body, **kw)


def _params(sem=None, **kw):
    if sem is not None:
        kw['dimension_semantics'] = sem
    return pltpu.CompilerParams(vmem_limit_bytes=VMEM_LIMIT, **kw)


def _tile(n, cap, mult):
    best = None
    for t in range(mult, min(n, cap) + 1, mult):
        if n % t == 0:
            best = t
    return best or n


def _sigmoid(z):
    return 1.0 / (1.0 + jnp.exp(-z))


def _softplus(z):
    return jnp.maximum(z, 0.0) + jnp.log(1.0 + jnp.exp(-jnp.abs(z)))


def _bdot(a, b, dims=(((1,), (0,)), ((), ()))):
    return lax.dot_general(a.astype(BF16), b.astype(BF16), dims, preferred_element_type=F32)


_NT = (((1,), (1,)), ((), ()))
_TN = (((0,), (0,)), ((), ()))


def _block_diag(n, seg):
    r = lax.broadcasted_iota(jnp.int32, (n, n), 0) // seg
    c = lax.broadcasted_iota(jnp.int32, (n, n), 1) // seg
    return jnp.where(r == c, 1.0, 0.0).astype(BF16)


def _segsum(x, bd):
    hi = x.astype(BF16)
    r1 = x - hi.astype(F32)
    mid = r1.astype(BF16)
    lo = (r1 - mid.astype(F32)).astype(BF16)
    dot = functools.partial(lax.dot_general, dimension_numbers=(((1,), (0,)), ((), ())), preferred_element_type=F32)
    return dot(hi, bd) + dot(mid, bd) + dot(lo, bd)


_LORA_EDGES = (3 * RW, 3 * RW + DECAY_LORA, 3 * RW + DECAY_LORA + ICLR_LORA, RWKV_COLS)


def _pad_rwkv_cols(x):
    parts = [x[..., :3 * RW]]
    for lo, hi in zip(_LORA_EDGES[:-1], _LORA_EDGES[1:]):
        parts.append(jnp.pad(x[..., lo:hi], [(0, 0)] * (x.ndim - 1) + [(0, 128 - (hi - lo))]))
    return jnp.concatenate(parts, axis=-1)


def _unpad_rwkv_cols(x):
    parts = [x[..., :3 * RW]]
    for j, (lo, hi) in enumerate(zip(_LORA_EDGES[:-1], _LORA_EDGES[1:])):
        parts.append(x[..., 3 * RW + 128 * j:3 * RW + 128 * j + (hi - lo)])
    return jnp.concatenate(parts, axis=-1)


def _pad_rows(x, rows):
    return jnp.pad(x, [(0, rows - x.shape[0])] + [(0, 0)] * (x.ndim - 1))


def mm(a, b, *, name, ta=False, tb=False, scale=None, res=None, out_dtype=F32):
    M, K = (a.shape[1], a.shape[0]) if ta else a.shape
    N = b.shape[0] if tb else b.shape[1]
    assert (b.shape[1] if tb else b.shape[0]) == K
    tm, tn, tk = _tile(M, 512, 128), _tile(N, 1408, 128), _tile(K, 1408, 128)
    nk = K // tk
    dims = (((0 if ta else 1,), (1 if tb else 0,)), ((), ()))

    def body(*refs):
        if res is None:
            a_ref, b_ref, o_ref, acc_ref = refs
            r_ref = None
        else:
            a_ref, b_ref, r_ref, o_ref, acc_ref = refs
        k = pl.program_id(2)
        part = _bdot(a_ref[...], b_ref[...], dims)

        @pl.when(k == 0)
        def _():
            acc_ref[...] = part

        @pl.when(k > 0)
        def _():
            acc_ref[...] += part

        @pl.when(k == nk - 1)
        def _():
            o = acc_ref[...]
            if scale is not None:
                o = o * scale
            if r_ref is not None:
                o = o + r_ref[...].astype(F32)
            o_ref[...] = o.astype(out_dtype)

    a_spec = pl.BlockSpec((tk, tm), lambda i, j, k: (k, i)) if ta else pl.BlockSpec((tm, tk), lambda i, j, k: (i, k))
    b_spec = pl.BlockSpec((tn, tk), lambda i, j, k: (j, k)) if tb else pl.BlockSpec((tk, tn), lambda i, j, k: (k, j))
    o_spec = pl.BlockSpec((tm, tn), lambda i, j, k: (i, j))
    in_specs = [a_spec, b_spec] + ([o_spec] if res is not None else [])
    args = (a, b) + ((res,) if res is not None else ())
    return _pcall(
        body, name=name, grid=(M // tm, N // tn, nk), in_specs=in_specs, out_specs=o_spec,
        out_shape=jax.ShapeDtypeStruct((M, N), out_dtype), scratch_shapes=[pltpu.VMEM((tm, tn), F32)],
        compiler_params=_params(("parallel", "parallel", "arbitrary")),
    )(*args)


def _row_spec(tr, c):
    return pl.BlockSpec((tr, c), lambda i: (i, 0))


def _full_spec(shape):
    return pl.BlockSpec(shape, lambda i: (0,) * len(shape))


def _acc_rows(ref, val, i):
    @pl.when(i == 0)
    def _():
        ref[...] = val

    @pl.when(i > 0)
    def _():
        ref[...] += val


def rms_fwd(x, g, *, name):
    T, D = x.shape
    tr = _tile(T, 512, 8)

    def body(x_ref, g_ref, h_ref):
        xv = x_ref[...]
        r = lax.rsqrt(jnp.mean(xv * xv, axis=-1, keepdims=True) + RMS_EPS)
        h_ref[...] = (xv * r * g_ref[...]).astype(BF16)

    return _pcall(body, name=name, grid=(T // tr,), in_specs=[_row_spec(tr, D), _full_spec((1, D))],
                  out_specs=_row_spec(tr, D), out_shape=jax.ShapeDtypeStruct((T, D), BF16),
                  compiler_params=_params(("parallel",)))(x, g)


def rms_bwd(dh, x, g, res, *, name):
    T, D = x.shape
    tr = _tile(T, 256, 8)

    def body(dh_ref, x_ref, g_ref, res_ref, dx_ref, dg_ref):
        i = pl.program_id(0)
        xv, dhv = x_ref[...], dh_ref[...].astype(F32)
        r = lax.rsqrt(jnp.mean(xv * xv, axis=-1, keepdims=True) + RMS_EPS)
        xh = xv * r
        dxh = dhv * g_ref[...]
        dx_ref[...] = res_ref[...] + r * (dxh - xh * jnp.mean(dxh * xh, axis=-1, keepdims=True))
        _acc_rows(dg_ref, jnp.sum(dhv * xh, axis=0, keepdims=True), i)

    return _pcall(body, name=name, grid=(T // tr,),
                  in_specs=[_row_spec(tr, D), _row_spec(tr, D), _full_spec((1, D)), _row_spec(tr, D)],
                  out_specs=[_row_spec(tr, D), _full_spec((1, D))],
                  out_shape=[jax.ShapeDtypeStruct((T, D), F32), jax.ShapeDtypeStruct((1, D), F32)],
                  compiler_params=_params(("arbitrary",)))(dh, x, g, res)


def final_loss(x, tgt, g, *, name):
    T, D = x.shape
    tr = _tile(T, 256, 8)

    def body(x_ref, t_ref, g_ref, dx_ref, dg_ref, loss_ref):
        i = pl.program_id(0)
        xv = x_ref[...]
        r = lax.rsqrt(jnp.mean(xv * xv, axis=-1, keepdims=True) + RMS_EPS)
        xh = xv * r
        e = xh * g_ref[...] - t_ref[...]
        part = 0.5 * jnp.sum(jnp.mean(e * e, axis=-1, keepdims=True), axis=0, keepdims=True)
        dy = e * (1.0 / D)
        dxh = dy * g_ref[...]
        dx_ref[...] = r * (dxh - xh * jnp.mean(dxh * xh, axis=-1, keepdims=True))
        _acc_rows(dg_ref, jnp.sum(dy * xh, axis=0, keepdims=True), i)
        _acc_rows(loss_ref, jnp.broadcast_to(part, (1, LANE)), i)

    return _pcall(body, name=name, grid=(T // tr,),
                  in_specs=[_row_spec(tr, D), _row_spec(tr, D), _full_spec((1, D))],
                  out_specs=[_row_spec(tr, D), _full_spec((1, D)), _full_spec((1, LANE))],
                  out_shape=[jax.ShapeDtypeStruct((T, D), F32), jax.ShapeDtypeStruct((1, D), F32),
                             jax.ShapeDtypeStruct((1, LANE), F32)],
                  compiler_params=_params(("arbitrary",)))(x, tgt, g)


def swiglu_fwd(gate, up, *, name):
    T, F = gate.shape
    tr = _tile(T, 256, 8)

    def body(g_ref, u_ref, a_ref):
        gv = g_ref[...]
        a_ref[...] = (gv * _sigmoid(gv) * u_ref[...]).astype(BF16)

    return _pcall(body, name=name, grid=(T // tr,), in_specs=[_row_spec(tr, F), _row_spec(tr, F)],
                  out_specs=_row_spec(tr, F), out_shape=jax.ShapeDtypeStruct((T, F), BF16),
                  compiler_params=_params(("parallel",)))(gate, up)


def swiglu_bwd(da, gate, up, *, name):
    T, F = gate.shape
    tr = _tile(T, 256, 8)

    def body(da_ref, g_ref, u_ref, dg_ref, du_ref):
        gv, dav = g_ref[...], da_ref[...]
        s = _sigmoid(gv)
        du_ref[...] = (dav * gv * s).astype(BF16)
        dg_ref[...] = (dav * u_ref[...] * s * (1.0 + gv * (1.0 - s))).astype(BF16)

    return _pcall(body, name=name, grid=(T // tr,), in_specs=[_row_spec(tr, F)] * 3,
                  out_specs=[_row_spec(tr, F)] * 2, out_shape=[jax.ShapeDtypeStruct((T, F), BF16)] * 2,
                  compiler_params=_params(("parallel",)))(da, gate, up)


def merge_fwd(br, ba, pg, *, name):
    T, D = br.shape
    tr = _tile(T, 256, 8)

    def body(br_ref, ba_ref, pg_ref, o_ref):
        pgv = pg_ref[...]
        o_ref[...] = (_sigmoid(pgv[:, :D]) * br_ref[...] + _sigmoid(pgv[:, D:]) * ba_ref[...]).astype(BF16)

    return _pcall(body, name=name, grid=(T // tr,), in_specs=[_row_spec(tr, D), _row_spec(tr, D), _row_spec(tr, 2 * D)],
                  out_specs=_row_spec(tr, D), out_shape=jax.ShapeDtypeStruct((T, D), BF16),
                  compiler_params=_params(("parallel",)))(br, ba, pg)


def merge_bwd(dm, br, ba, pg, *, name):
    T, D = br.shape
    tr = _tile(T, 256, 8)

    def body(dm_ref, br_ref, ba_ref, pg_ref, dbr_ref, dba_ref, dpg_ref):
        pgv, dmv = pg_ref[...], dm_ref[...]
        sr, sa = _sigmoid(pgv[:, :D]), _sigmoid(pgv[:, D:])
        dbr_ref[...] = (dmv * sr).astype(BF16)
        dba_ref[...] = (dmv * sa).astype(BF16)
        dpg_ref[:, :D] = dmv * br_ref[...] * sr * (1.0 - sr)
        dpg_ref[:, D:] = dmv * ba_ref[...] * sa * (1.0 - sa)

    return _pcall(body, name=name, grid=(T // tr,),
                  in_specs=[_row_spec(tr, D), _row_spec(tr, D), _row_spec(tr, D), _row_spec(tr, 2 * D)],
                  out_specs=[_row_spec(tr, D), _row_spec(tr, D), _row_spec(tr, 2 * D)],
                  out_shape=[jax.ShapeDtypeStruct((T, D), BF16), jax.ShapeDtypeStruct((T, D), BF16),
                             jax.ShapeDtypeStruct((T, 2 * D), F32)],
                  compiler_params=_params(("parallel",)))(dm, br, ba, pg)


def _rwkv_mix(p, prev, mu, w0, a0, k_k, k_a, wlw, wla, wlg, bd):
    pp = p + (prev - p) * mu
    r, k, v = pp[:, 0:RW], pp[:, RW:2 * RW], pp[:, 2 * RW:3 * RW]
    xw, xa, xg = pp[:, 3 * RW:3 * RW + 128], pp[:, 3 * RW + 128:3 * RW + 256], pp[:, 3 * RW + 256:3 * RW + 384]
    th = jnp.tanh(xw)
    z = -(w0 + _bdot(th, wlw))
    e = jnp.exp(-_softplus(z) - 0.5)
    decay = jnp.exp(-e)
    a = _sigmoid(a0 + _bdot(xa, wla))
    sg = _sigmoid(xg)
    kkr = k * k_k
    n = jnp.sqrt(_segsum(kkr * kkr, bd))
    kk = kkr / jnp.maximum(n, 1e-12)
    k2 = k * (1.0 + (a - 1.0) * k_a)
    return dict(r=r, k=k, v=v, xa=xa, th=th, z=z, e=e, decay=decay, a=a, sg=sg, n=n, kk=kk, k2=k2)


def _seg_matrix(n, shift):
    r = lax.shift_right_logical(lax.broadcasted_iota(jnp.int32, (n, n), 0), shift)
    c = lax.shift_right_logical(lax.broadcasted_iota(jnp.int32, (n, n), 1), shift)
    return jnp.where(r == c, 1.0, 0.0).astype(BF16)


def rwkv_pre_fwd(p, pshift, mu, w0, a0, k_k, k_a, wlw, wla, wlg, *, name):
    T = p.shape[0]
    tr = _tile(T, 256, 8)

    def body(p_ref, ps_ref, mu_ref, w0_ref, a0_ref, kk_ref, ka_ref, wlw_ref, wla_ref, wlg_ref,
             r_ref, w_ref, k_ref, v_ref, a_ref, b_ref, g_ref):
        pv, prev = p_ref[...], ps_ref[...]
        m = _rwkv_mix(pv, prev, mu_ref[...], w0_ref[...], a0_ref[...], kk_ref[...], ka_ref[...],
                      wlw_ref[...], wla_ref[...], wlg_ref[...], _seg_matrix(RW, 6))
        r_ref[...] = m['r']
        w_ref[...] = m['decay']
        k_ref[...] = m['k2']
        v_ref[...] = m['v']
        a_ref[...] = -m['kk']
        b_ref[...] = m['kk'] * m['a']
        g_ref[...] = m['sg']

    vec = _row_spec(tr, RW)
    return _pcall(
        body, name=name, grid=(T // tr,),
        in_specs=[_row_spec(tr, RWKV_PAD), _row_spec(tr, RWKV_PAD), _full_spec((1, RWKV_PAD))] + [_full_spec((1, RW))] * 4
        + [_full_spec((128, RW))] * 3,
        out_specs=[vec] * 6 + [_row_spec(tr, 128)],
        out_shape=[jax.ShapeDtypeStruct((T, RW), F32)] * 6 + [jax.ShapeDtypeStruct((T, 128), F32)],
        compiler_params=_params(("parallel",)),
    )(p, pshift, mu, w0, a0, k_k, k_a, wlw, wla, wlg)


def _group_norm(y, bd):
    mean = _segsum(y, bd) * (1.0 / HEAD_DIM)
    yc = y - mean
    rstd = lax.rsqrt(_segsum(yc * yc, bd) * (1.0 / HEAD_DIM) + GN_EPS)
    return yc * rstd, rstd


def rwkv_post_fwd(y, r, k2, v, sg, wlg, ln_w, ln_b, r_k, *, name):
    T = y.shape[0]
    tr = _tile(T, 256, 8)

    def body(y_ref, r_ref, k_ref, v_ref, sg_ref, wlg_ref, lw_ref, lb_ref, rk_ref, o_ref):
        bd = _seg_matrix(RW, 6)
        yn, _ = _group_norm(y_ref[...], bd)
        s = _segsum(r_ref[...] * k_ref[...] * rk_ref[...], bd)
        g = _bdot(sg_ref[...], wlg_ref[...])
        o_ref[...] = ((yn * lw_ref[...] + lb_ref[...] + s * v_ref[...]) * g).astype(BF16)

    vec = _row_spec(tr, RW)
    return _pcall(body, name=name, grid=(T // tr,),
                  in_specs=[vec] * 4 + [_row_spec(tr, 128), _full_spec((128, RW))] + [_full_spec((1, RW))] * 3, out_specs=vec,
                  out_shape=jax.ShapeDtypeStruct((T, RW), BF16), compiler_params=_params(("parallel",)))(
                      y, r, k2, v, sg, wlg, ln_w, ln_b, r_k)


def rwkv_post_bwd(dyr, y, r, k2, v, sg, wlg, ln_w, ln_b, r_k, *, name):
    T = y.shape[0]
    tr = _tile(T, 256, 8)

    def body(dyr_ref, y_ref, r_ref, k_ref, v_ref, sg_ref, wlg_ref, lw_ref, lb_ref, rk_ref,
             dy_ref, dz_ref, dg_ref, dlw_ref, dlb_ref):
        i = pl.program_id(0)
        bd = _seg_matrix(RW, 6)
        yn, rstd = _group_norm(y_ref[...], bd)
        s = _segsum(r_ref[...] * k_ref[...] * rk_ref[...], bd)
        dyrv = dyr_ref[...]
        dg_ref[...] = dyrv * (yn * lw_ref[...] + lb_ref[...] + s * v_ref[...])
        dz = dyrv * _bdot(sg_ref[...], wlg_ref[...])
        dz_ref[...] = dz
        dyn = dz * lw_ref[...]
        inv = 1.0 / HEAD_DIM
        dy_ref[...] = rstd * (dyn - _segsum(dyn, bd) * inv - yn * (_segsum(dyn * yn, bd) * inv))
        _acc_rows(dlw_ref, jnp.sum(dz * yn, axis=0, keepdims=True), i)
        _acc_rows(dlb_ref, jnp.sum(dz, axis=0, keepdims=True), i)

    vec = _row_spec(tr, RW)
    one = _full_spec((1, RW))
    return _pcall(body, name=name, grid=(T // tr,),
                  in_specs=[vec] * 5 + [_row_spec(tr, 128), _full_spec((128, RW))] + [one] * 3, out_specs=[vec] * 3 + [one] * 2,
                  out_shape=[jax.ShapeDtypeStruct((T, RW), F32)] * 3 + [jax.ShapeDtypeStruct((1, RW), F32)] * 2,
                  compiler_params=_params(("arbitrary",)))(dyr, y, r, k2, v, sg, wlg, ln_w, ln_b, r_k)


def rwkv_pre_bwd(p, pshift, dr_w, dw_w, dk_w, dv_w, da_w, db_w, dz, dg, mu, w0, a0, k_k, k_a, r_k, wlw, wla, wlg, *, name):
    T = p.shape[0]
    tr = _tile(T, 256, 8)
    n = T // tr

    def body(p_ref, ps_ref, dr_ref, dw_ref, dk_ref, dv_ref, da_ref, db_ref, dz_ref, dg_ref,
             mu_ref, w0_ref, a0_ref, kk_ref, ka_ref, rk_ref, wlw_ref, wla_ref, wlg_ref,
             dp_ref, dmu_ref, dw0_ref, da0_ref, dkk_ref, dka_ref, drk_ref, dwlw_ref, dwla_ref, dwlg_ref,
             carry, dpp, acc_w, acc_a, acc_g):
        i = pl.program_id(0)

        @pl.when(i == 0)
        def _():
            carry[...] = jnp.zeros_like(carry)

        pv, prev, mu = p_ref[...], ps_ref[...], mu_ref[...]
        bd = _seg_matrix(RW, 6)
        k_k, k_a, r_k = kk_ref[...], ka_ref[...], rk_ref[...]
        m = _rwkv_mix(pv, prev, mu, w0_ref[...], a0_ref[...], k_k, k_a, wlw_ref[...], wla_ref[...], wlg_ref[...], bd)
        r, k, v, a, kk, k2 = m['r'], m['k'], m['v'], m['a'], m['kk'], m['k2']
        dzv, dgv = dz_ref[...], dg_ref[...]
        s = _segsum(r * k2 * r_k, bd)
        ds = _segsum(dzv * v, bd)
        dr = dr_ref[...] + ds * k2 * r_k
        dk2 = dk_ref[...] + ds * r * r_k
        dv = dv_ref[...] + dzv * s
        dbv = db_ref[...]
        dkk = dbv * a - da_ref[...]
        da = dbv * kk + dk2 * k * k_a
        dk = dk2 * (1.0 + (a - 1.0) * k_a)
        nmax = jnp.maximum(m['n'], 1e-12)
        dkkr = jnp.where(m['n'] > 1e-12, dkk - kk * _segsum(dkk * kk, bd), dkk) / nmax
        dk = dk + dkkr * k_k
        dapre = da * a * (1.0 - a)
        dwpre = dw_ref[...] * m['decay'] * (-m['e']) * _sigmoid(m['z'])
        dth = _bdot(dwpre, wlw_ref[...], _NT)
        dxa = _bdot(dapre, wla_ref[...], _NT)
        dsg = _bdot(dgv, wlg_ref[...], _NT)
        dpp[:, 0:RW] = dr
        dpp[:, RW:2 * RW] = dk
        dpp[:, 2 * RW:3 * RW] = dv
        dpp[:, 3 * RW:3 * RW + 128] = dth * (1.0 - m['th'] * m['th'])
        dpp[:, 3 * RW + 128:3 * RW + 256] = dxa
        dpp[:, 3 * RW + 256:3 * RW + 384] = dsg * m['sg'] * (1.0 - m['sg'])
        d = dpp[...]
        zed = d * mu
        last = lax.broadcasted_iota(jnp.int32, pv.shape, 0) == tr - 1
        dp_ref[...] = d * (1.0 - mu) + jnp.where(last, carry[0:1, :], pltpu.roll(zed, tr - 1, 0))
        carry[...] = zed[0:8, :]

        def colsum(x):
            return jnp.sum(x, axis=0, keepdims=True)

        _acc_rows(dmu_ref, colsum(d * (prev - pv)), i)
        _acc_rows(dw0_ref, colsum(dwpre), i)
        _acc_rows(da0_ref, colsum(dapre), i)
        _acc_rows(dkk_ref, colsum(dkkr * k), i)
        _acc_rows(dka_ref, colsum(dk2 * k * (a - 1.0)), i)
        _acc_rows(drk_ref, colsum(ds * r * k2), i)
        _acc_rows(acc_w, _bdot(m['th'], dwpre, _TN), i)
        _acc_rows(acc_a, _bdot(m['xa'], dapre, _TN), i)
        _acc_rows(acc_g, _bdot(m['sg'], dgv, _TN), i)

        @pl.when(i == n - 1)
        def _():
            dwlw_ref[...] = acc_w[...]
            dwla_ref[...] = acc_a[...]
            dwlg_ref[...] = acc_g[...]

    rev = lambda c: pl.BlockSpec((tr, c), lambda i: (n - 1 - i, 0))
    one, lora = _full_spec((1, RW)), _full_spec((128, RW))
    return _pcall(
        body, name=name, grid=(n,),
        in_specs=[rev(RWKV_PAD), rev(RWKV_PAD)] + [rev(RW)] * 8 + [_full_spec((1, RWKV_PAD))] + [one] * 5 + [lora] * 3,
        out_specs=[rev(RWKV_PAD), _full_spec((1, RWKV_PAD))] + [one] * 5 + [lora] * 3,
        out_shape=[jax.ShapeDtypeStruct((T, RWKV_PAD), F32), jax.ShapeDtypeStruct((1, RWKV_PAD), F32)]
        + [jax.ShapeDtypeStruct((1, RW), F32)] * 5 + [jax.ShapeDtypeStruct((128, RW), F32)] * 3,
        scratch_shapes=[pltpu.VMEM((8, RWKV_PAD), F32), pltpu.VMEM((tr, RWKV_PAD), F32)] + [pltpu.VMEM((128, RW), F32)] * 3,
        compiler_params=_params(("arbitrary",)),
    )(p, pshift, dr_w, dw_w, dk_w, dv_w, da_w, db_w, dz, dg, mu, w0, a0, k_k, k_a, r_k, wlw, wla, wlg)


def _qk_norm(x, g, bd):
    r = lax.rsqrt(_segsum(x * x, bd) * (1.0 / HEAD_DIM) + RMS_EPS)
    return x * r * g, r


def _att_mask(i):
    qi = lax.broadcasted_iota(jnp.int32, (BLOCK, 2 * BLOCK), 0)
    kj = lax.broadcasted_iota(jnp.int32, (BLOCK, 2 * BLOCK), 1)
    band = (kj <= qi + BLOCK) & (kj > qi + BLOCK - WINDOW)
    return band & ((kj >= BLOCK) | (i > 0))


def _att_probs(qh, kh, mask, sink):
    s = _bdot(qh, kh, _NT) * (HEAD_DIM ** -0.5)
    s = jnp.where(mask, s, NEG_BIG)
    m = jnp.maximum(jnp.max(s, axis=-1, keepdims=True), sink)
    pexp = jnp.exp(s - m)
    psink = jnp.exp(sink - m)
    inv = 1.0 / (jnp.sum(pexp, axis=-1, keepdims=True) + psink)
    return pexp * inv, psink * inv


def _att_blocks(n):
    cur = pl.BlockSpec((BLOCK, ATT_COLS), lambda i: (i, 0))
    prev = pl.BlockSpec((BLOCK, ATT_COLS), lambda i: (jnp.maximum(i - 1, 0), 0))
    return cur, prev


def _att_qkv(cur, prev, qn_g, kn_g):
    bq, bk = _seg_matrix(RW, 6), _seg_matrix(KVW, 6)
    qn, rq = _qk_norm(cur[:, 0:RW], qn_g, bq)
    kcur, rkc = _qk_norm(cur[:, RW:RW + KVW], kn_g, bk)
    kprev, _ = _qk_norm(prev[:, RW:RW + KVW], kn_g, bk)
    kc = jnp.concatenate([kprev, kcur], axis=0)
    vc = jnp.concatenate([prev[:, RW + KVW:], cur[:, RW + KVW:]], axis=0)
    return qn, rq, kc, vc, rkc


def att_fwd(pa, qn_g, kn_g, sinks, *, name):
    T = pa.shape[0]
    n = T // BLOCK

    def body(cur_ref, prev_ref, qg_ref, kg_ref, sk_ref, o_ref):
        i = pl.program_id(0)
        qn, _, kc, vc, _ = _att_qkv(cur_ref[...], prev_ref[...], qg_ref[...], kg_ref[...])
        mask = _att_mask(i)
        outs = []
        for h in range(N_HEADS):
            kv = slice((h // ATT_GROUP) * HEAD_DIM, (h // ATT_GROUP + 1) * HEAD_DIM)
            probs, _ = _att_probs(qn[:, h * HEAD_DIM:(h + 1) * HEAD_DIM], kc[:, kv], mask, sk_ref[0:1, h:h + 1])
            outs.append(_bdot(probs, vc[:, kv]))
        o_ref[...] = jnp.concatenate(outs, axis=1)

    cur, prev = _att_blocks(n)
    return _pcall(body, name=name, grid=(n,),
                  in_specs=[cur, prev, _full_spec((1, RW)), _full_spec((1, KVW)), _full_spec((1, LANE))],
                  out_specs=pl.BlockSpec((BLOCK, RW), lambda i: (i, 0)), out_shape=jax.ShapeDtypeStruct((T, RW), F32),
                  compiler_params=_params(("parallel",)))(pa, pa, qn_g, kn_g, sinks)


def att_bwd(pa, do, qn_g, kn_g, sinks, *, name):
    T = pa.shape[0]
    n = T // BLOCK

    def body(cur_ref, prev_ref, do_ref, qg_ref, kg_ref, sk_ref,
             dq_ref, dko_ref, dkn_ref, dvo_ref, dvn_ref, dqg_ref, dsk_ref):
        i = pl.program_id(0)
        cur = cur_ref[...]
        qn, rq, kc, vc, _ = _att_qkv(cur, prev_ref[...], qg_ref[...], kg_ref[...])
        mask = _att_mask(i)
        dov = do_ref[...]
        lane = lax.broadcasted_iota(jnp.int32, (1, LANE), 1)
        dsink = jnp.zeros((1, LANE), F32)
        dqn, dkc, dvc = [], [None, None], [None, None]
        for h in range(N_HEADS):
            g = h // ATT_GROUP
            kv = slice(g * HEAD_DIM, (g + 1) * HEAD_DIM)
            qh, doh = qn[:, h * HEAD_DIM:(h + 1) * HEAD_DIM], dov[:, h * HEAD_DIM:(h + 1) * HEAD_DIM]
            probs, psink = _att_probs(qh, kc[:, kv], mask, sk_ref[0:1, h:h + 1])
            dprobs = _bdot(doh, vc[:, kv], _NT)
            delta = jnp.sum(probs * dprobs, axis=-1, keepdims=True)
            ds = probs * (dprobs - delta) * (HEAD_DIM ** -0.5)
            dsink = dsink + jnp.where(lane == h, -jnp.sum(psink * delta, axis=0, keepdims=True), 0.0)
            dqn.append(_bdot(ds, kc[:, kv]))
            dk_h, dv_h = _bdot(ds, qh, _TN), _bdot(probs, doh, _TN)
            dkc[g] = dk_h if dkc[g] is None else dkc[g] + dk_h
            dvc[g] = dv_h if dvc[g] is None else dvc[g] + dv_h
        dqn = jnp.concatenate(dqn, axis=1)
        dk = jnp.concatenate(dkc, axis=1)
        dv = jnp.concatenate(dvc, axis=1)
        dkn_ref[...], dko_ref[...] = dk[0:BLOCK], dk[BLOCK:]
        dvn_ref[...], dvo_ref[...] = dv[0:BLOCK], dv[BLOCK:]
        qhat = cur[:, 0:RW] * rq
        dqh = dqn * qg_ref[...]
        dq_ref[...] = rq * (dqh - qhat * (_segsum(dqh * qhat, _seg_matrix(RW, 6)) * (1.0 / HEAD_DIM)))
        prod = dqn * qhat
        fold = prod[:, 0:HEAD_DIM]
        for h in range(1, N_HEADS):
            fold = fold + prod[:, h * HEAD_DIM:(h + 1) * HEAD_DIM]
        _acc_rows(dqg_ref, jnp.sum(fold, axis=0, keepdims=True), i)
        _acc_rows(dsk_ref, dsink, i)

    cur, prev = _att_blocks(n)
    kvb = pl.BlockSpec((BLOCK, KVW), lambda i: (i, 0))
    qb = pl.BlockSpec((BLOCK, RW), lambda i: (i, 0))
    return _pcall(body, name=name, grid=(n,),
                  in_specs=[cur, prev, qb, _full_spec((1, RW)), _full_spec((1, KVW)), _full_spec((1, LANE))],
                  out_specs=[qb, kvb, kvb, kvb, kvb, _full_spec((1, HEAD_DIM)), _full_spec((1, LANE))],
                  out_shape=[jax.ShapeDtypeStruct((T, RW), F32)] + [jax.ShapeDtypeStruct((T, KVW), F32)] * 4
                  + [jax.ShapeDtypeStruct((1, HEAD_DIM), F32), jax.ShapeDtypeStruct((1, LANE), F32)],
                  compiler_params=_params(("arbitrary",)))(pa, pa, do, qn_g, kn_g, sinks)


def att_kv_bwd(pa, dq, dko, dkn, dvo, dvn, kn_g, *, name):
    T = pa.shape[0]
    n = T // BLOCK

    def body(pa_ref, dq_ref, dko_ref, dkn_ref, dvo_ref, dvn_ref, kg_ref, dpa_ref, dkg_ref):
        i = pl.program_id(0)
        more = i < n - 1
        dkn_tot = dko_ref[...] + jnp.where(more, dkn_ref[...], 0.0)
        dv_tot = dvo_ref[...] + jnp.where(more, dvn_ref[...], 0.0)
        kraw = pa_ref[:, RW:RW + KVW]
        bk = _seg_matrix(KVW, 6)
        _, rk = _qk_norm(kraw, kg_ref[...], bk)
        khat = kraw * rk
        dkh = dkn_tot * kg_ref[...]
        dpa_ref[:, 0:RW] = dq_ref[...]
        dpa_ref[:, RW:RW + KVW] = rk * (dkh - khat * (_segsum(dkh * khat, bk) * (1.0 / HEAD_DIM)))
        dpa_ref[:, RW + KVW:] = dv_tot
        prod = dkn_tot * khat
        _acc_rows(dkg_ref, jnp.sum(prod[:, 0:HEAD_DIM] + prod[:, HEAD_DIM:], axis=0, keepdims=True), i)

    kvb = pl.BlockSpec((BLOCK, KVW), lambda i: (i, 0))
    nxt = pl.BlockSpec((BLOCK, KVW), lambda i: (jnp.minimum(i + 1, n - 1), 0))
    return _pcall(body, name=name, grid=(n,),
                  in_specs=[pl.BlockSpec((BLOCK, ATT_COLS), lambda i: (i, 0)), pl.BlockSpec((BLOCK, RW), lambda i: (i, 0)),
                            kvb, nxt, kvb, nxt, _full_spec((1, KVW))],
                  out_specs=[pl.BlockSpec((BLOCK, ATT_COLS), lambda i: (i, 0)), _full_spec((1, HEAD_DIM))],
                  out_shape=[jax.ShapeDtypeStruct((T, ATT_COLS), F32), jax.ShapeDtypeStruct((1, HEAD_DIM), F32)],
                  compiler_params=_params(("arbitrary",)))(pa, dq, dko, dkn, dvo, dvn, kn_g)


WKV_CHUNK = 64
WKV_GROUP = 8


def _diag_mask():
    i = lax.broadcasted_iota(jnp.int32, (HEAD_DIM, RW), 0)
    j = lax.broadcasted_iota(jnp.int32, (HEAD_DIM, RW), 1) & (HEAD_DIM - 1)
    return i == j


def _headsums(xs, bd):
    half = RW // 2
    rows = []
    for x in xs:
        hi = x.astype(BF16)
        r1 = x - hi.astype(F32)
        mid = r1.astype(BF16)
        lo = (r1 - mid.astype(F32)).astype(BF16)
        for sl in (slice(0, half), slice(half, RW)):
            rows.append(jnp.concatenate([hi[:, sl], mid[:, sl], lo[:, sl]], axis=1))
    out = lax.dot_general(jnp.concatenate(rows, axis=0), bd, (((1,), (0,)), ((), ())), preferred_element_type=F32)
    return [jnp.concatenate([out[2 * n * HEAD_DIM:(2 * n + 1) * HEAD_DIM], out[(2 * n + 1) * HEAD_DIM:(2 * n + 2) * HEAD_DIM]],
                            axis=1) for n in range(len(xs))]


def _heads_matrix():
    bd = _seg_matrix(RW // 2, 6)
    return jnp.concatenate([bd, bd, bd], axis=0)


def _headsum(x):
    low = lax.broadcasted_iota(jnp.int32, (HEAD_DIM, LANE), 1) < HEAD_DIM
    tiles = []
    for c in range(RW // LANE):
        xt = x[:, c * LANE:(c + 1) * LANE]
        s_lo = jnp.sum(jnp.where(low, xt, 0.0), axis=1, keepdims=True)
        s_hi = jnp.sum(jnp.where(low, 0.0, xt), axis=1, keepdims=True)
        tiles.append(jnp.where(low, s_lo, s_hi))
    return jnp.concatenate(tiles, axis=1)


def _cols(rows, diag, bd):
    return _headsums([jnp.where(diag, r, 0.0) for r in rows], bd)


def _row(x, diag):
    return jnp.sum(jnp.where(diag, x, 0.0), axis=0, keepdims=True)


def wkv_fwd(r, w, k, v, a, b, *, name):
    T = r.shape[0]
    ch = min(WKV_CHUNK, T)
    ngroups = ch // WKV_GROUP

    def body(r_ref, w_ref, k_ref, v_ref, a_ref, b_ref, y_ref, st_ref, s_scr):
        @pl.when(pl.program_id(0) == 0)
        def _():
            s_scr[...] = jnp.zeros_like(s_scr)

        diag, bd = _diag_mask(), _heads_matrix()

        def group(gi, S):
            t0 = pl.multiple_of(gi * WKV_GROUP, WKV_GROUP)
            rows = pl.ds(t0, WKV_GROUP)
            R, W, K, V, A, B = (ref[rows, :] for ref in (r_ref, w_ref, k_ref, v_ref, a_ref, b_ref))
            vcols = _cols([V[s:s + 1] for s in range(WKV_GROUP)], diag, bd)
            yrows = []
            for s in range(WKV_GROUP):
                sa = _headsum(S * A[s:s + 1])
                S = S * W[s:s + 1] + sa * B[s:s + 1] + vcols[s] * K[s:s + 1]
                st_ref[t0 + s] = S
                yrows.append(_row(_headsum(S * R[s:s + 1]), diag))
            y_ref[rows, :] = jnp.concatenate(yrows, axis=0)
            return S

        s_scr[...] = lax.fori_loop(0, ngroups, group, s_scr[...])

    vec = pl.BlockSpec((ch, RW), lambda c: (c, 0))
    return _pcall(
        body, name=name, grid=(T // ch,), in_specs=[vec] * 6,
        out_specs=[vec, pl.BlockSpec((ch, HEAD_DIM, RW), lambda c: (c, 0, 0))],
        out_shape=[jax.ShapeDtypeStruct((T, RW), F32), jax.ShapeDtypeStruct((T, HEAD_DIM, RW), F32)],
        scratch_shapes=[pltpu.VMEM((HEAD_DIM, RW), F32)],
        compiler_params=_params(("arbitrary",)),
    )(r, w, k, v, a, b)


def wkv_bwd(r, w, k, v, a, b, dy, states, *, name):
    T = r.shape[0]
    ch = min(WKV_CHUNK, T)
    nchunks = T // ch
    ngroups = ch // WKV_GROUP

    def body(r_ref, w_ref, k_ref, v_ref, a_ref, b_ref, dy_ref, st_ref, stp_ref,
             dr_ref, dw_ref, dk_ref, dv_ref, da_ref, db_ref, ds_scr):
        step = pl.program_id(0)

        @pl.when(step == 0)
        def _():
            ds_scr[...] = jnp.zeros_like(ds_scr)

        has_prev_chunk = step < nchunks - 1
        diag, bd = _diag_mask(), _heads_matrix()
        colsum = lambda x: jnp.sum(x, axis=0, keepdims=True)

        def group(gj, dS):
            gi = ngroups - 1 - gj
            t0 = pl.multiple_of(gi * WKV_GROUP, WKV_GROUP)
            rows = pl.ds(t0, WKV_GROUP)
            R, W, K, V, A, B, DY = (ref[rows, :] for ref in (r_ref, w_ref, k_ref, v_ref, a_ref, b_ref, dy_ref))
            before = jnp.where(gi > 0, st_ref[jnp.maximum(t0 - 1, 0)], jnp.where(has_prev_chunk, stp_ref[0], 0.0))
            prev_state = lambda s: st_ref[t0 + s - 1] if s > 0 else before
            steps = range(WKV_GROUP)
            dycols = _cols([DY[s:s + 1] for s in steps], diag, bd)
            vcols = _cols([V[s:s + 1] for s in steps], diag, bd)
            sas = _headsums([prev_state(s) * A[s:s + 1] for s in steps], bd)
            got = [[None] * WKV_GROUP for _ in range(6)]
            for s in reversed(steps):
                Sp = prev_state(s)
                dS = dS + dycols[s] * R[s:s + 1]
                got[0][s] = colsum(st_ref[t0 + s] * dycols[s])
                got[3][s] = _row(_headsum(dS * K[s:s + 1]), diag)
                got[2][s] = colsum(dS * vcols[s])
                dsa = _headsum(dS * B[s:s + 1])
                got[5][s] = colsum(dS * sas[s])
                got[1][s] = colsum(dS * Sp)
                got[4][s] = colsum(Sp * dsa)
                dS = dS * W[s:s + 1] + dsa * A[s:s + 1]
            for q, ref in enumerate((dr_ref, dw_ref, dk_ref, dv_ref, da_ref, db_ref)):
                ref[rows, :] = jnp.concatenate(got[q], axis=0)
            return dS

        ds_scr[...] = lax.fori_loop(0, ngroups, group, ds_scr[...])

    vec = pl.BlockSpec((ch, RW), lambda c: (nchunks - 1 - c, 0))
    st_spec = pl.BlockSpec((ch, HEAD_DIM, RW), lambda c: (nchunks - 1 - c, 0, 0))
    stp_spec = pl.BlockSpec((1, HEAD_DIM, RW), lambda c: (jnp.maximum((nchunks - 1 - c) * ch - 1, 0), 0, 0))
    return _pcall(
        body, name=name, grid=(nchunks,), in_specs=[vec] * 7 + [st_spec, stp_spec], out_specs=[vec] * 6,
        out_shape=[jax.ShapeDtypeStruct((T, RW), F32)] * 6,
        scratch_shapes=[pltpu.VMEM((HEAD_DIM, RW), F32)],
        compiler_params=_params(("arbitrary",)),
    )(r, w, k, v, a, b, dy, states, states)


_HBM = pl.BlockSpec(memory_space=pltpu.HBM)
_MESH = pl.DeviceIdType.MESH


def _place():
    x, y, c = lax.axis_index("x"), lax.axis_index("y"), lax.axis_index("c")
    return x, y, c, [(1 - x, y), (x, 1 - y), (1 - x, 1 - y)]


def all_gather_chips(flat, *, name):
    R, W = flat.shape

    def body(w_ref, o_ref, send_sems, recv_sems, local_sem):
        x, y, c, chips = _place()
        me = 2 * x + y

        def copy(j, block, to):
            return pltpu.make_async_remote_copy(src_ref=w_ref, dst_ref=o_ref.at[block], send_sem=send_sems.at[j],
                                                recv_sem=recv_sems.at[j], device_id=to, device_id_type=_MESH)

        mine = pltpu.make_async_copy(w_ref, o_ref.at[me], local_sem)
        mine.start()
        sends = [copy(j, me, (qx, qy, c)) for j, (qx, qy) in enumerate(chips)]
        for cp in sends:
            cp.start()
        for j, (qx, qy) in enumerate(chips):
            copy(j, 2 * qx + qy, (qx, qy, c)).wait_recv()
        for cp in sends:
            cp.wait_send()
        mine.wait()

    return _pcall(body, name=name, in_specs=[_HBM], out_specs=_HBM, out_shape=jax.ShapeDtypeStruct((N_CHIPS, R, W), flat.dtype),
                  scratch_shapes=[pltpu.SemaphoreType.DMA((3,)), pltpu.SemaphoreType.DMA((3,)), pltpu.SemaphoreType.DMA(())],
                  compiler_params=_params())(flat)


def exchange_chips(blocks, *, name):
    _, R, W = blocks.shape

    def body(g_ref, o_ref, send_sems, recv_sems, local_sem):
        x, y, c, chips = _place()
        me = 2 * x + y

        def copy(j, src_block, dst_block, to):
            return pltpu.make_async_remote_copy(src_ref=g_ref.at[src_block], dst_ref=o_ref.at[dst_block],
                                                send_sem=send_sems.at[j], recv_sem=recv_sems.at[j], device_id=to,
                                                device_id_type=_MESH)

        mine = pltpu.make_async_copy(g_ref.at[me], o_ref.at[me], local_sem)
        mine.start()
        sends = [copy(j, 2 * qx + qy, me, (qx, qy, c)) for j, (qx, qy) in enumerate(chips)]
        for cp in sends:
            cp.start()
        for j, (qx, qy) in enumerate(chips):
            copy(j, me, 2 * qx + qy, (qx, qy, c)).wait_recv()
        for cp in sends:
            cp.wait_send()
        mine.wait()

    return _pcall(body, name=name, in_specs=[_HBM], out_specs=_HBM, out_shape=jax.ShapeDtypeStruct(blocks.shape, blocks.dtype),
                  scratch_shapes=[pltpu.SemaphoreType.DMA((3,)), pltpu.SemaphoreType.DMA((3,)), pltpu.SemaphoreType.DMA(())],
                  compiler_params=_params())(blocks)


def exchange_pair(flat, *, name):
    def body(p_ref, o_ref, send_sem, recv_sem):
        x, y, c, _ = _place()
        cp = pltpu.make_async_remote_copy(src_ref=p_ref, dst_ref=o_ref, send_sem=send_sem, recv_sem=recv_sem,
                                          device_id=(x, y, 1 - c), device_id_type=_MESH)
        cp.start()
        cp.wait_recv()
        cp.wait_send()

    return _pcall(body, name=name, in_specs=[_HBM], out_specs=_HBM, out_shape=jax.ShapeDtypeStruct(flat.shape, flat.dtype),
                  scratch_shapes=[pltpu.SemaphoreType.DMA(()), pltpu.SemaphoreType.DMA(())],
                  compiler_params=_params())(flat)


def sum_blocks(land, *, name):
    _, R, W = land.shape
    tr = _tile(R, 256, 8)

    def body(l_ref, o_ref):
        o_ref[...] = ((l_ref[0] + l_ref[1]) + l_ref[2]) + l_ref[3]

    return _pcall(body, name=name, grid=(R // tr,), in_specs=[pl.BlockSpec((N_CHIPS, tr, W), lambda i: (0, i, 0))],
                  out_specs=_row_spec(tr, W), out_shape=jax.ShapeDtypeStruct((R, W), F32),
                  compiler_params=_params(("parallel",)))(land)


def adamw(w, m, v, ga, gb, *, name):
    R, W = w.shape
    tr = _tile(R, 256, 8)

    def body(w_ref, m_ref, v_ref, ga_ref, gb_ref, g_ref, d_ref, nm_ref, nv_ref):
        g = ga_ref[...] + gb_ref[...]
        g_ref[...] = g
        nm = ADAM_B1 * m_ref[...] + (1.0 - ADAM_B1) * g
        nv = ADAM_B2 * v_ref[...] + (1.0 - ADAM_B2) * (g * g)
        nm_ref[...] = nm
        nv_ref[...] = nv
        m_hat = nm / (1.0 - ADAM_B1 ** ADAM_STEP)
        v_hat = nv / (1.0 - ADAM_B2 ** ADAM_STEP)
        d_ref[...] = -ADAM_LR * (m_hat / (jnp.sqrt(v_hat) + ADAM_EPS) + ADAM_WD * w_ref[...])

    spec = _row_spec(tr, W)
    return _pcall(body, name=name, grid=(R // tr,), in_specs=[spec] * 5, out_specs=[spec] * 4,
                  out_shape=[jax.ShapeDtypeStruct((R, W), F32)] * 4, compiler_params=_params(("parallel",)))(w, m, v, ga, gb)


def _part_rows(numel):
    per = FLAT_W * FLAT_ALIGN
    return -(-numel // per) * FLAT_ALIGN


def _pack(parts, lead=()):
    out = []
    for p in parts:
        flat = p.reshape(lead + (-1,))
        rows = _part_rows(flat.shape[-1])
        flat = jnp.pad(flat, [(0, 0)] * len(lead) + [(0, rows * FLAT_W - flat.shape[-1])])
        out.append(flat.reshape(lead + (rows, FLAT_W)))
    return jnp.concatenate(out, axis=len(lead))


def _unpack(flat, shapes, lead=()):
    out, off = [], 0
    for shp in shapes:
        numel = 1
        for s in shp:
            numel *= s
        rows = _part_rows(numel)
        part = flat[..., off:off + rows, :].reshape(lead + (rows * FLAT_W,))[..., :numel]
        out.append(part.reshape(lead + tuple(shp)))
        off += rows
    return out


def _to_blocks(full, axis):
    r, c = full.shape
    if axis == 1:
        return full.reshape(r, N_CHIPS, c // N_CHIPS).transpose(1, 0, 2)
    return full.reshape(N_CHIPS, r // N_CHIPS, c)


def _from_blocks(blocks, axis):
    _, r, c = blocks.shape
    if axis == 1:
        return blocks.transpose(1, 0, 2).reshape(r, N_CHIPS * c)
    return blocks.reshape(N_CHIPS * r, c)


def _ffn_fwd(x, norm, wg, wu, wd, tag):
    h = rms_fwd(x, norm, name=tag + "_norm")
    gate = mm(h, wg, name=tag + "_gate")
    up = mm(h, wu, name=tag + "_up")
    act = swiglu_fwd(gate, up, name=tag + "_act")
    out = mm(act, wd, scale=0.5, res=x, name=tag + "_down")
    return out, (h, gate, up, act)


def _ffn_bwd(dout, x, saved, norm, wg, wu, wd, tag):
    h, gate, up, act = saved
    dact = mm(dout, wd, tb=True, scale=0.5, name=tag + "_dact")
    dwd = mm(act, dout, ta=True, scale=0.5, name=tag + "_dwd")
    dgate, dup = swiglu_bwd(dact, gate, up, name=tag + "_dswiglu")
    dwg = mm(h, dgate, ta=True, name=tag + "_dwg")
    dwu = mm(h, dup, ta=True, name=tag + "_dwu")
    dh = mm(dgate, wg, tb=True, name=tag + "_dh_gate")
    dh = mm(dup, wu, tb=True, res=dh, name=tag + "_dh_up")
    dx, dnorm = rms_bwd(dh, x, norm, dout, name=tag + "_dnorm")
    return dx, dnorm, dwg, dwu, dwd


def _step(A):
    x, tgt = A['x'][0], A['loss_target'][0]
    T = x.shape[0]
    w = {n: A[n][0] for n in WEIGHT_NAMES}
    row = lambda a: a.reshape(1, -1)

    big_shapes = [w[n].shape for n, _ in BIG]
    gathered = all_gather_chips(_pack([w[n].astype(BF16) for n, _ in BIG]), name="gather_weights")
    full = {n: _from_blocks(b, axis) for (n, axis), b in zip(BIG, _unpack(gathered, big_shapes, lead=(N_CHIPS,)))}
    w_in_r = _pad_rwkv_cols(full['w_in'][:, :RWKV_COLS])
    w_in_a = full['w_in'][:, RWKV_COLS:RWKV_COLS + ATT_COLS]
    w_in_g = full['w_in'][:, RWKV_COLS + ATT_COLS:]
    wlw, wla, wlg = (_pad_rows(full[n], 128).astype(F32) for n in ('rwkv_w_lora_up', 'rwkv_a_lora_up', 'rwkv_g_lora_up'))
    mu = _pad_rwkv_cols(row(w['rwkv_mu']))
    w0, a0, k_k, k_a, r_k, ln_w, ln_b = (row(w[n]) for n in ('rwkv_w0', 'rwkv_a0', 'rwkv_k_k', 'rwkv_k_a', 'rwkv_r_k',
                                                               'rwkv_ln_w', 'rwkv_ln_b'))
    qg = jnp.tile(row(w['attn_q_norm']), (1, N_HEADS))
    kg = jnp.tile(row(w['attn_k_norm']), (1, KVW // HEAD_DIM))
    sinks = jnp.pad(row(w['attn_sinks']), ((0, 0), (0, LANE - N_HEADS)))
    n1, nmix, n2, nfin = (row(w[n]) for n in ('ffn1_norm', 'mix_norm', 'ffn2_norm', 'final_norm'))

    x1, ffn1 = _ffn_fwd(x, n1, full['ffn1_w_gate'], full['ffn1_w_up'], full['ffn1_w_down'], "ffn1")
    h2 = rms_fwd(x1, nmix, name="mix_norm")
    pr = mm(h2, w_in_r, name="proj_rwkv")
    pa = mm(h2, w_in_a, name="proj_att")
    pg = mm(h2, w_in_g, name="proj_gate")
    pr_shift = jnp.pad(pr, ((1, 0), (0, 0)))[:-1]
    r, dec, k2, v, a, b, sg = rwkv_pre_fwd(pr, pr_shift, mu, w0, a0, k_k, k_a, wlw, wla, wlg, name="rwkv_pre")
    y, states = wkv_fwd(r, dec, k2, v, a, b, name="wkv_fwd")
    yr = rwkv_post_fwd(y, r, k2, v, sg, wlg, ln_w, ln_b, r_k, name="rwkv_post")
    ya = att_fwd(pa, qg, kg, sinks, name="att_fwd")
    br = mm(yr, full['w_branch_rwkv'], name="branch_rwkv")
    ba = mm(ya, full['w_branch_attn'], name="branch_att")
    mg = merge_fwd(br, ba, pg, name="merge")
    x2 = mm(mg, full['w_out'], res=x1, name="mix_out")
    x3, ffn2 = _ffn_fwd(x2, n2, full['ffn2_w_gate'], full['ffn2_w_up'], full['ffn2_w_down'], "ffn2")
    dx3, d_nfin, loss = final_loss(x3, tgt, nfin, name="final_loss")

    G = {'final_norm': d_nfin}
    dx2, G['ffn2_norm'], G['ffn2_w_gate'], G['ffn2_w_up'], G['ffn2_w_down'] = _ffn_bwd(
        dx3, x2, ffn2, n2, full['ffn2_w_gate'], full['ffn2_w_up'], full['ffn2_w_down'], "ffn2")
    dmg = mm(dx2, full['w_out'], tb=True, name="d_merge")
    G['w_out'] = mm(mg, dx2, ta=True, name="d_w_out")
    dbr, dba, dpg = merge_bwd(dmg, br, ba, pg, name="merge_bwd")
    dyr = mm(dbr, full['w_branch_rwkv'], tb=True, name="d_y_rwkv")
    G['w_branch_rwkv'] = mm(yr, dbr, ta=True, name="d_w_branch_rwkv")
    dya = mm(dba, full['w_branch_attn'], tb=True, name="d_y_att")
    G['w_branch_attn'] = mm(ya, dba, ta=True, name="d_w_branch_att")
    dy, dz, dg, G['rwkv_ln_w'], G['rwkv_ln_b'] = rwkv_post_bwd(dyr, y, r, k2, v, sg, wlg, ln_w, ln_b, r_k, name="rwkv_post_bwd")
    wkv_grads = wkv_bwd(r, dec, k2, v, a, b, dy, states, name="wkv_bwd")
    (dpr, d_mu, G['rwkv_w0'], G['rwkv_a0'], G['rwkv_k_k'], G['rwkv_k_a'], G['rwkv_r_k'], d_wlw, d_wla, d_wlg) = rwkv_pre_bwd(
        pr, pr_shift, *wkv_grads, dz, dg, mu, w0, a0, k_k, k_a, r_k, wlw, wla, wlg, name="rwkv_pre_bwd")
    G['rwkv_mu'] = _unpad_rwkv_cols(d_mu)
    G['rwkv_w_lora_up'], G['rwkv_a_lora_up'], G['rwkv_g_lora_up'] = d_wlw[:DECAY_LORA], d_wla[:ICLR_LORA], d_wlg[:GATE_LORA]
    dq, dko, dkn, dvo, dvn, G['attn_q_norm'], d_sinks = att_bwd(pa, dya, qg, kg, sinks, name="att_bwd")
    G['attn_sinks'] = d_sinks[:, :N_HEADS]
    dpa, G['attn_k_norm'] = att_kv_bwd(pa, dq, dko, dkn, dvo, dvn, kg, name="att_kv_bwd")
    d_w_in_r = mm(h2, dpr, ta=True, name="d_w_in_rwkv")
    d_w_in_a = mm(h2, dpa, ta=True, name="d_w_in_att")
    d_w_in_g = mm(h2, dpg, ta=True, name="d_w_in_gate")
    G['w_in'] = jnp.concatenate([_unpad_rwkv_cols(d_w_in_r), d_w_in_a, d_w_in_g], axis=1)
    dh2 = mm(dpr, w_in_r, tb=True, name="d_h2_rwkv")
    dh2 = mm(dpa, w_in_a, tb=True, res=dh2, name="d_h2_att")
    dh2 = mm(dpg, w_in_g, tb=True, res=dh2, name="d_h2_gate")
    dx1, G['mix_norm'] = rms_bwd(dh2, x1, nmix, dx2, name="d_mix_norm")
    dx0, G['ffn1_norm'], G['ffn1_w_gate'], G['ffn1_w_up'], G['ffn1_w_down'] = _ffn_bwd(
        dx1, x, ffn1, n1, full['ffn1_w_gate'], full['ffn1_w_up'], full['ffn1_w_down'], "ffn1")

    small_shapes = [(w[n].size,) for n in SMALL] + [(1,)]
    small = jnp.concatenate([G[n].reshape(-1) for n in SMALL] + [loss[0, :1]])
    small4 = jnp.broadcast_to(small[None], (N_CHIPS, small.shape[0]))
    blocks = _pack([_to_blocks(G[n], axis) for n, axis in BIG] + [small4], lead=(N_CHIPS,))
    landed = exchange_chips(blocks, name="exchange_grads")
    mine = sum_blocks(landed, name="sum_chips")
    theirs = exchange_pair(mine, name="exchange_cores")

    def local(prefix):
        parts = [A[prefix + n][0] for n, _ in BIG]
        parts.append(jnp.concatenate([A[prefix + n].reshape(-1) for n in SMALL] + [jnp.zeros((1,), F32)]))
        return _pack(parts)

    flat_out = adamw(local(''), local('m_'), local('v_'), mine, theirs, name="adamw")
    shapes = big_shapes + [(small.shape[0],)]
    result = {}
    for kind, flat in zip(('grad_', 'delta_', 'new_m_', 'new_v_'), flat_out):
        parts = _unpack(flat, shapes)
        for (n, _), part in zip(BIG, parts[:-1]):
            result[kind + n] = part[None]
        for n, part in zip(SMALL + ['loss'], _unpack_vec(parts[-1], small_shapes)):
            result[kind + n] = part.reshape(A[n].shape) if n != 'loss' else part.reshape(())
    outs = [result['grad_loss'], dx0[None]]
    for kind in ('grad_', 'delta_', 'new_m_', 'new_v_'):
        outs += [result[kind + n] for n in WEIGHT_NAMES]
    return tuple(outs)


def _unpack_vec(vec, shapes):
    out, off = [], 0
    for (n,) in shapes:
        out.append(vec[off:off + n])
        off += n
    return out


def kernel(x, ffn1_norm, ffn1_w_gate, ffn1_w_up, ffn1_w_down, mix_norm, w_in, rwkv_mu, rwkv_w0, rwkv_w_lora_up, rwkv_a0, rwkv_a_lora_up, rwkv_g_lora_up, rwkv_k_k, rwkv_k_a, rwkv_r_k, rwkv_ln_w, rwkv_ln_b, attn_q_norm, attn_k_norm, attn_sinks, w_branch_rwkv, w_branch_attn, w_out, ffn2_norm, ffn2_w_gate, ffn2_w_up, ffn2_w_down, final_norm, loss_target, m_ffn1_norm, m_ffn1_w_gate, m_ffn1_w_up, m_ffn1_w_down, m_mix_norm, m_w_in, m_rwkv_mu, m_rwkv_w0, m_rwkv_w_lora_up, m_rwkv_a0, m_rwkv_a_lora_up, m_rwkv_g_lora_up, m_rwkv_k_k, m_rwkv_k_a, m_rwkv_r_k, m_rwkv_ln_w, m_rwkv_ln_b, m_attn_q_norm, m_attn_k_norm, m_attn_sinks, m_w_branch_rwkv, m_w_branch_attn, m_w_out, m_ffn2_norm, m_ffn2_w_gate, m_ffn2_w_up, m_ffn2_w_down, m_final_norm, v_ffn1_norm, v_ffn1_w_gate, v_ffn1_w_up, v_ffn1_w_down, v_mix_norm, v_w_in, v_rwkv_mu, v_rwkv_w0, v_rwkv_w_lora_up, v_rwkv_a0, v_rwkv_a_lora_up, v_rwkv_g_lora_up, v_rwkv_k_k, v_rwkv_k_a, v_rwkv_r_k, v_rwkv_ln_w, v_rwkv_ln_b, v_attn_q_norm, v_attn_k_norm, v_attn_sinks, v_w_branch_rwkv, v_w_branch_attn, v_w_out, v_ffn2_norm, v_ffn2_w_gate, v_ffn2_w_up, v_ffn2_w_down, v_final_norm):
    return _step(dict(locals()))
```

```python
import functools

import jax
import jax.numpy as jnp
from jax import lax
from jax.experimental import pallas as pl
from jax.experimental.pallas import tpu as pltpu

F32 = jnp.float32
BF16 = jnp.bfloat16

D_MODEL = 1024
D_FF = 2816
HEAD_DIM = 64
N_HEADS = 8
RW = 512
KVW = 128
ATT_GROUP = 4
WINDOW = 128
BLOCK = 128
DECAY_LORA, ICLR_LORA, GATE_LORA = 32, 32, 96
RWKV_COLS = 3 * RW + DECAY_LORA + ICLR_LORA + GATE_LORA
ATT_COLS = RW + 2 * KVW
GATE_COLS = 2 * D_MODEL
RWKV_PAD = 3 * RW + 3 * 128
RMS_EPS = 1e-6
GN_EPS = 64e-5
N_CHIPS = 4
LANE = 128
FLAT_W = 1024
SMALL_ROWS = 32
NEG_BIG = -1e30

ADAM_LR, ADAM_B1, ADAM_B2, ADAM_EPS, ADAM_WD, ADAM_STEP = 0.001, 0.9, 0.999, 1e-08, 0.01, 10

VMEM_LIMIT = 56 * 1024 * 1024

WEIGHT_NAMES = ['ffn1_norm', 'ffn1_w_gate', 'ffn1_w_up', 'ffn1_w_down', 'mix_norm', 'w_in', 'rwkv_mu', 'rwkv_w0',
                'rwkv_w_lora_up', 'rwkv_a0', 'rwkv_a_lora_up', 'rwkv_g_lora_up', 'rwkv_k_k', 'rwkv_k_a', 'rwkv_r_k',
                'rwkv_ln_w', 'rwkv_ln_b', 'attn_q_norm', 'attn_k_norm', 'attn_sinks', 'w_branch_rwkv',
                'w_branch_attn', 'w_out', 'ffn2_norm', 'ffn2_w_gate', 'ffn2_w_up', 'ffn2_w_down', 'final_norm']
BIG = [('ffn1_w_gate', 1), ('ffn1_w_up', 1), ('ffn1_w_down', 0), ('w_in', 1), ('rwkv_w_lora_up', 1),
       ('rwkv_a_lora_up', 1), ('rwkv_g_lora_up', 1), ('w_branch_rwkv', 1), ('w_branch_attn', 1), ('w_out', 0),
       ('ffn2_w_gate', 1), ('ffn2_w_up', 1), ('ffn2_w_down', 0)]
SMALL = ['ffn1_norm', 'mix_norm', 'rwkv_mu', 'rwkv_w0', 'rwkv_a0', 'rwkv_k_k', 'rwkv_k_a', 'rwkv_r_k', 'rwkv_ln_w',
         'rwkv_ln_b', 'attn_q_norm', 'attn_k_norm', 'attn_sinks', 'ffn2_norm', 'final_norm']


def _pcall(body, **kw):
    return pl.pallas_call(body, **kw)


def _params(sem=None, **kw):
    if sem is not None:
        kw['dimension_semantics'] = sem
    return pltpu.CompilerParams(vmem_limit_bytes=VMEM_LIMIT, **kw)


def _tile(n, cap, mult):
    best = None
    for t in range(mult, min(n, cap) + 1, mult):
        if n % t == 0:
            best = t
    return best or n


def _sigmoid(z):
    return 1.0 / (1.0 + jnp.exp(-z))


def _softplus(z):
    return jnp.maximum(z, 0.0) + jnp.log(1.0 + jnp.exp(-jnp.abs(z)))


def _bdot(a, b, dims=(((1,), (0,)), ((), ()))):
    return lax.dot_general(a.astype(BF16), b.astype(BF16), dims, preferred_element_type=F32)


_NT = (((1,), (1,)), ((), ()))
_TN = (((0,), (0,)), ((), ()))


def _block_diag(n, seg):
    r = lax.broadcasted_iota(jnp.int32, (n, n), 0) // seg
    c = lax.broadcasted_iota(jnp.int32, (n, n), 1) // seg
    return jnp.where(r == c, 1.0, 0.0).astype(BF16)


def _segsum(x, bd):
    hi = x.astype(BF16)
    r1 = x - hi.astype(F32)
    mid = r1.astype(BF16)
    lo = (r1 - mid.astype(F32)).astype(BF16)
    dot = functools.partial(lax.dot_general, dimension_numbers=(((1,), (0,)), ((), ())), preferred_element_type=F32)
    return dot(hi, bd) + dot(mid, bd) + dot(lo, bd)


_LORA_EDGES = (3 * RW, 3 * RW + DECAY_LORA, 3 * RW + DECAY_LORA + ICLR_LORA, RWKV_COLS)


def _pad_rwkv_cols(x):
    parts = [x[..., :3 * RW]]
    for lo, hi in zip(_LORA_EDGES[:-1], _LORA_EDGES[1:]):
        parts.append(jnp.pad(x[..., lo:hi], [(0, 0)] * (x.ndim - 1) + [(0, 128 - (hi - lo))]))
    return jnp.concatenate(parts, axis=-1)


def _unpad_rwkv_cols(x):
    parts = [x[..., :3 * RW]]
    for j, (lo, hi) in enumerate(zip(_LORA_EDGES[:-1], _LORA_EDGES[1:])):
        parts.append(x[..., 3 * RW + 128 * j:3 * RW + 128 * j + (hi - lo)])
    return jnp.concatenate(parts, axis=-1)


def _pad_rows(x, rows):
    return jnp.pad(x, [(0, rows - x.shape[0])] + [(0, 0)] * (x.ndim - 1))


def mm(a, b, *, name, ta=False, tb=False, scale=None, res=None, out_dtype=F32):
    M, K = (a.shape[1], a.shape[0]) if ta else a.shape
    N = b.shape[0] if tb else b.shape[1]
    assert (b.shape[1] if tb else b.shape[0]) == K
    tm, tn, tk = _tile(M, 512, 128), _tile(N, 1408, 128), _tile(K, 1408, 128)
    nk = K // tk
    dims = (((0 if ta else 1,), (1 if tb else 0,)), ((), ()))

    def body(*refs):
        if res is None:
            a_ref, b_ref, o_ref, acc_ref = refs
            r_ref = None
        else:
            a_ref, b_ref, r_ref, o_ref, acc_ref = refs
        k = pl.program_id(2)
        part = _bdot(a_ref[...], b_ref[...], dims)

        @pl.when(k == 0)
        def _():
            acc_ref[...] = part

        @pl.when(k > 0)
        def _():
            acc_ref[...] += part

        @pl.when(k == nk - 1)
        def _():
            o = acc_ref[...]
            if scale is not None:
                o = o * scale
            if r_ref is not None:
                o = o + r_ref[...].astype(F32)
            o_ref[...] = o.astype(out_dtype)

    a_spec = pl.BlockSpec((tk, tm), lambda i, j, k: (k, i)) if ta else pl.BlockSpec((tm, tk), lambda i, j, k: (i, k))
    b_spec = pl.BlockSpec((tn, tk), lambda i, j, k: (j, k)) if tb else pl.BlockSpec((tk, tn), lambda i, j, k: (k, j))
    o_spec = pl.BlockSpec((tm, tn), lambda i, j, k: (i, j))
    in_specs = [a_spec, b_spec] + ([o_spec] if res is not None else [])
    args = (a, b) + ((res,) if res is not None else ())
    return _pcall(
        body, name=name, grid=(M // tm, N // tn, nk), in_specs=in_specs, out_specs=o_spec,
        out_shape=jax.ShapeDtypeStruct((M, N), out_dtype), scratch_shapes=[pltpu.VMEM((tm, tn), F32)],
        compiler_params=_params(("parallel", "parallel", "arbitrary")),
    )(*args)


def _row_spec(tr, c):
    return pl.BlockSpec((tr, c), lambda i: (i, 0))


def _full_spec(shape):
    return pl.BlockSpec(shape, lambda i: (0,) * len(shape))


def _acc_rows(ref, val, i):
    @pl.when(i == 0)
    def _():
        ref[...] = val

    @pl.when(i > 0)
    def _():
        ref[...] += val


def rms_fwd(x, g, *, name):
    T, D = x.shape
    tr = _tile(T, 512, 8)

    def body(x_ref, g_ref, h_ref):
        xv = x_ref[...]
        r = lax.rsqrt(jnp.mean(xv * xv, axis=-1, keepdims=True) + RMS_EPS)
        h_ref[...] = (xv * r * g_ref[...]).astype(BF16)

    return _pcall(body, name=name, grid=(T // tr,), in_specs=[_row_spec(tr, D), _full_spec((1, D))],
                  out_specs=_row_spec(tr, D), out_shape=jax.ShapeDtypeStruct((T, D), BF16),
                  compiler_params=_params(("parallel",)))(x, g)


def rms_bwd(dh, x, g, res, *, name):
    T, D = x.shape
    tr = _tile(T, 256, 8)

    def body(dh_ref, x_ref, g_ref, res_ref, dx_ref, dg_ref):
        i = pl.program_id(0)
        xv, dhv = x_ref[...], dh_ref[...].astype(F32)
        r = lax.rsqrt(jnp.mean(xv * xv, axis=-1, keepdims=True) + RMS_EPS)
        xh = xv * r
        dxh = dhv * g_ref[...]
        dx_ref[...] = res_ref[...] + r * (dxh - xh * jnp.mean(dxh * xh, axis=-1, keepdims=True))
        _acc_rows(dg_ref, jnp.sum(dhv * xh, axis=0, keepdims=True), i)

    return _pcall(body, name=name, grid=(T // tr,),
                  in_specs=[_row_spec(tr, D), _row_spec(tr, D), _full_spec((1, D)), _row_spec(tr, D)],
                  out_specs=[_row_spec(tr, D), _full_spec((1, D))],
                  out_shape=[jax.ShapeDtypeStruct((T, D), F32), jax.ShapeDtypeStruct((1, D), F32)],
                  compiler_params=_params(("arbitrary",)))(dh, x, g, res)


def final_loss(x, tgt, g, *, name):
    T, D = x.shape
    tr = _tile(T, 256, 8)

    def body(x_ref, t_ref, g_ref, dx_ref, dg_ref, loss_ref):
        i = pl.program_id(0)
        xv = x_ref[...]
        r = lax.rsqrt(jnp.mean(xv * xv, axis=-1, keepdims=True) + RMS_EPS)
        xh = xv * r
        e = xh * g_ref[...] - t_ref[...]
        part = 0.5 * jnp.sum(jnp.mean(e * e, axis=-1, keepdims=True), axis=0, keepdims=True)
        dy = e * (1.0 / D)
        dxh = dy * g_ref[...]
        dx_ref[...] = r * (dxh - xh * jnp.mean(dxh * xh, axis=-1, keepdims=True))
        _acc_rows(dg_ref, jnp.sum(dy * xh, axis=0, keepdims=True), i)
        _acc_rows(loss_ref, jnp.broadcast_to(part, (1, LANE)), i)

    return _pcall(body, name=name, grid=(T // tr,),
                  in_specs=[_row_spec(tr, D), _row_spec(tr, D), _full_spec((1, D))],
                  out_specs=[_row_spec(tr, D), _full_spec((1, D)), _full_spec((1, LANE))],
                  out_shape=[jax.ShapeDtypeStruct((T, D), F32), jax.ShapeDtypeStruct((1, D), F32),
                             jax.ShapeDtypeStruct((1, LANE), F32)],
                  compiler_params=_params(("arbitrary",)))(x, tgt, g)


def swiglu_fwd(gate, up, *, name):
    T, F = gate.shape
    tr = _tile(T, 256, 8)

    def body(g_ref, u_ref, a_ref):
        gv = g_ref[...]
        a_ref[...] = (gv * _sigmoid(gv) * u_ref[...]).astype(BF16)

    return _pcall(body, name=name, grid=(T // tr,), in_specs=[_row_spec(tr, F), _row_spec(tr, F)],
                  out_specs=_row_spec(tr, F), out_shape=jax.ShapeDtypeStruct((T, F), BF16),
                  compiler_params=_params(("parallel",)))(gate, up)


def swiglu_bwd(da, gate, up, *, name):
    T, F = gate.shape
    tr = _tile(T, 256, 8)

    def body(da_ref, g_ref, u_ref, dg_ref, du_ref):
        gv, dav = g_ref[...], da_ref[...]
        s = _sigmoid(gv)
        du_ref[...] = (dav * gv * s).astype(BF16)
        dg_ref[...] = (dav * u_ref[...] * s * (1.0 + gv * (1.0 - s))).astype(BF16)

    return _pcall(body, name=name, grid=(T // tr,), in_specs=[_row_spec(tr, F)] * 3,
                  out_specs=[_row_spec(tr, F)] * 2, out_shape=[jax.ShapeDtypeStruct((T, F), BF16)] * 2,
                  compiler_params=_params(("parallel",)))(da, gate, up)


def merge_fwd(br, ba, pg, *, name):
    T, D = br.shape
    tr = _tile(T, 256, 8)

    def body(br_ref, ba_ref, pg_ref, o_ref):
        pgv = pg_ref[...]
        o_ref[...] = (_sigmoid(pgv[:, :D]) * br_ref[...] + _sigmoid(pgv[:, D:]) * ba_ref[...]).astype(BF16)

    return _pcall(body, name=name, grid=(T // tr,), in_specs=[_row_spec(tr, D), _row_spec(tr, D), _row_spec(tr, 2 * D)],
                  out_specs=_row_spec(tr, D), out_shape=jax.ShapeDtypeStruct((T, D), BF16),
                  compiler_params=_params(("parallel",)))(br, ba, pg)


def merge_bwd(dm, br, ba, pg, *, name):
    T, D = br.shape
    tr = _tile(T, 256, 8)

    def body(dm_ref, br_ref, ba_ref, pg_ref, dbr_ref, dba_ref, dpg_ref):
        pgv, dmv = pg_ref[...], dm_ref[...]
        sr, sa = _sigmoid(pgv[:, :D]), _sigmoid(pgv[:, D:])
        dbr_ref[...] = (dmv * sr).astype(BF16)
        dba_ref[...] = (dmv * sa).astype(BF16)
        dpg_ref[:, :D] = dmv * br_ref[...] * sr * (1.0 - sr)
        dpg_ref[:, D:] = dmv * ba_ref[...] * sa * (1.0 - sa)

    return _pcall(body, name=name, grid=(T // tr,),
                  in_specs=[_row_spec(tr, D), _row_spec(tr, D), _row_spec(tr, D), _row_spec(tr, 2 * D)],
                  out_specs=[_row_spec(tr, D), _row_spec(tr, D), _row_spec(tr, 2 * D)],
                  out_shape=[jax.ShapeDtypeStruct((T, D), BF16), jax.ShapeDtypeStruct((T, D), BF16),
                             jax.ShapeDtypeStruct((T, 2 * D), F32)],
                  compiler_params=_params(("parallel",)))(dm, br, ba, pg)


def _rwkv_mix(p, prev, mu, w0, a0, k_k, k_a, wlw, wla, wlg, bd):
    pp = p + (prev - p) * mu
    r, k, v = pp[:, 0:RW], pp[:, RW:2 * RW], pp[:, 2 * RW:3 * RW]
    xw, xa, xg = pp[:, 3 * RW:3 * RW + 128], pp[:, 3 * RW + 128:3 * RW + 256], pp[:, 3 * RW + 256:3 * RW + 384]
    th = jnp.tanh(xw)
    z = -(w0 + _bdot(th, wlw))
    e = jnp.exp(-_softplus(z) - 0.5)
    decay = jnp.exp(-e)
    a = _sigmoid(a0 + _bdot(xa, wla))
    sg = _sigmoid(xg)
    kkr = k * k_k
    n = jnp.sqrt(_segsum(kkr * kkr, bd))
    kk = kkr / jnp.maximum(n, 1e-12)
    k2 = k * (1.0 + (a - 1.0) * k_a)
    return dict(r=r, k=k, v=v, xa=xa, th=th, z=z, e=e, decay=decay, a=a, sg=sg, n=n, kk=kk, k2=k2)


def _seg_matrix(n, shift):
    r = lax.shift_right_logical(lax.broadcasted_iota(jnp.int32, (n, n), 0), shift)
    c = lax.shift_right_logical(lax.broadcasted_iota(jnp.int32, (n, n), 1), shift)
    return jnp.where(r == c, 1.0, 0.0).astype(BF16)


def rwkv_pre_fwd(p, pshift, mu, w0, a0, k_k, k_a, wlw, wla, wlg, *, name):
    T = p.shape[0]
    tr = _tile(T, 256, 8)

    def body(p_ref, ps_ref, mu_ref, w0_ref, a0_ref, kk_ref, ka_ref, wlw_ref, wla_ref, wlg_ref,
             r_ref, w_ref, k_ref, v_ref, a_ref, b_ref, g_ref):
        pv, prev = p_ref[...], ps_ref[...]
        m = _rwkv_mix(pv, prev, mu_ref[...], w0_ref[...], a0_ref[...], kk_ref[...], ka_ref[...],
                      wlw_ref[...], wla_ref[...], wlg_ref[...], _seg_matrix(RW, 6))
        r_ref[...] = m['r']
        w_ref[...] = m['decay']
        k_ref[...] = m['k2']
        v_ref[...] = m['v']
        a_ref[...] = -m['kk']
        b_ref[...] = m['kk'] * m['a']
        g_ref[...] = m['sg']

    vec = _row_spec(tr, RW)
    return _pcall(
        body, name=name, grid=(T // tr,),
        in_specs=[_row_spec(tr, RWKV_PAD), _row_spec(tr, RWKV_PAD), _full_spec((1, RWKV_PAD))] + [_full_spec((1, RW))] * 4
        + [_full_spec((128, RW))] * 3,
        out_specs=[vec] * 6 + [_row_spec(tr, 128)],
        out_shape=[jax.ShapeDtypeStruct((T, RW), F32)] * 6 + [jax.ShapeDtypeStruct((T, 128), F32)],
        compiler_params=_params(("parallel",)),
    )(p, pshift, mu, w0, a0, k_k, k_a, wlw, wla, wlg)


def _group_norm(y, bd):
    mean = _segsum(y, bd) * (1.0 / HEAD_DIM)
    yc = y - mean
    rstd = lax.rsqrt(_segsum(yc * yc, bd) * (1.0 / HEAD_DIM) + GN_EPS)
    return yc * rstd, rstd


def rwkv_post_fwd(y, r, k2, v, sg, wlg, ln_w, ln_b, r_k, *, name):
    T = y.shape[0]
    tr = _tile(T, 256, 8)

    def body(y_ref, r_ref, k_ref, v_ref, sg_ref, wlg_ref, lw_ref, lb_ref, rk_ref, o_ref):
        bd = _seg_matrix(RW, 6)
        yn, _ = _group_norm(y_ref[...], bd)
        s = _segsum(r_ref[...] * k_ref[...] * rk_ref[...], bd)
        g = _bdot(sg_ref[...], wlg_ref[...])
        o_ref[...] = ((yn * lw_ref[...] + lb_ref[...] + s * v_ref[...]) * g).astype(BF16)

    vec = _row_spec(tr, RW)
    return _pcall(body, name=name, grid=(T // tr,),
                  in_specs=[vec] * 4 + [_row_spec(tr, 128), _full_spec((128, RW))] + [_full_spec((1, RW))] * 3, out_specs=vec,
                  out_shape=jax.ShapeDtypeStruct((T, RW), BF16), compiler_params=_params(("parallel",)))(
                      y, r, k2, v, sg, wlg, ln_w, ln_b, r_k)


def rwkv_post_bwd(dyr, y, r, k2, v, sg, wlg, ln_w, ln_b, r_k, *, name):
    T = y.shape[0]
    tr = _tile(T, 256, 8)

    def body(dyr_ref, y_ref, r_ref, k_ref, v_ref, sg_ref, wlg_ref, lw_ref, lb_ref, rk_ref,
             dy_ref, dz_ref, dg_ref, dlw_ref, dlb_ref):
        i = pl.program_id(0)
        bd = _seg_matrix(RW, 6)
        yn, rstd = _group_norm(y_ref[...], bd)
        s = _segsum(r_ref[...] * k_ref[...] * rk_ref[...], bd)
        dyrv = dyr_ref[...]
        dg_ref[...] = dyrv * (yn * lw_ref[...] + lb_ref[...] + s * v_ref[...])
        dz = dyrv * _bdot(sg_ref[...], wlg_ref[...])
        dz_ref[...] = dz
        dyn = dz * lw_ref[...]
        inv = 1.0 / HEAD_DIM
        dy_ref[...] = rstd * (dyn - _segsum(dyn, bd) * inv - yn * (_segsum(dyn * yn, bd) * inv))
        _acc_rows(dlw_ref, jnp.sum(dz * yn, axis=0, keepdims=True), i)
        _acc_rows(dlb_ref, jnp.sum(dz, axis=0, keepdims=True), i)

    vec = _row_spec(tr, RW)
    one = _full_spec((1, RW))
    return _pcall(body, name=name, grid=(T // tr,),
                  in_specs=[vec] * 5 + [_row_spec(tr, 128), _full_spec((128, RW))] + [one] * 3, out_specs=[vec] * 3 + [one] * 2,
                  out_shape=[jax.ShapeDtypeStruct((T, RW), F32)] * 3 + [jax.ShapeDtypeStruct((1, RW), F32)] * 2,
                  compiler_params=_params(("arbitrary",)))(dyr, y, r, k2, v, sg, wlg, ln_w, ln_b, r_k)


def rwkv_pre_bwd(p, pshift, dr_w, dw_w, dk_w, dv_w, da_w, db_w, dz, dg, mu, w0, a0, k_k, k_a, r_k, wlw, wla, wlg, *, name):
    T = p.shape[0]
    tr = _tile(T, 256, 8)
    n = T // tr

    def body(p_ref, ps_ref, dr_ref, dw_ref, dk_ref, dv_ref, da_ref, db_ref, dz_ref, dg_ref,
             mu_ref, w0_ref, a0_ref, kk_ref, ka_ref, rk_ref, wlw_ref, wla_ref, wlg_ref,
             dp_ref, dmu_ref, dw0_ref, da0_ref, dkk_ref, dka_ref, drk_ref, dwlw_ref, dwla_ref, dwlg_ref,
             carry, dpp, acc_w, acc_a, acc_g):
        i = pl.program_id(0)

        @pl.when(i == 0)
        def _():
            carry[...] = jnp.zeros_like(carry)

        pv, prev, mu = p_ref[...], ps_ref[...], mu_ref[...]
        bd = _seg_matrix(RW, 6)
        k_k, k_a, r_k = kk_ref[...], ka_ref[...], rk_ref[...]
        m = _rwkv_mix(pv, prev, mu, w0_ref[...], a0_ref[...], k_k, k_a, wlw_ref[...], wla_ref[...], wlg_ref[...], bd)
        r, k, v, a, kk, k2 = m['r'], m['k'], m['v'], m['a'], m['kk'], m['k2']
        dzv, dgv = dz_ref[...], dg_ref[...]
        s = _segsum(r * k2 * r_k, bd)
        ds = _segsum(dzv * v, bd)
        dr = dr_ref[...] + ds * k2 * r_k
        dk2 = dk_ref[...] + ds * r * r_k
        dv = dv_ref[...] + dzv * s
        dbv = db_ref[...]
        dkk = dbv * a - da_ref[...]
        da = dbv * kk + dk2 * k * k_a
        dk = dk2 * (1.0 + (a - 1.0) * k_a)
        nmax = jnp.maximum(m['n'], 1e-12)
        dkkr = jnp.where(m['n'] > 1e-12, dkk - kk * _segsum(dkk * kk, bd), dkk) / nmax
        dk = dk + dkkr * k_k
        dapre = da * a * (1.0 - a)
        dwpre = dw_ref[...] * m['decay'] * (-m['e']) * _sigmoid(m['z'])
        dth = _bdot(dwpre, wlw_ref[...], _NT)
        dxa = _bdot(dapre, wla_ref[...], _NT)
        dsg = _bdot(dgv, wlg_ref[...], _NT)
        dpp[:, 0:RW] = dr
        dpp[:, RW:2 * RW] = dk
        dpp[:, 2 * RW:3 * RW] = dv
        dpp[:, 3 * RW:3 * RW + 128] = dth * (1.0 - m['th'] * m['th'])
        dpp[:, 3 * RW + 128:3 * RW + 256] = dxa
        dpp[:, 3 * RW + 256:3 * RW + 384] = dsg * m['sg'] * (1.0 - m['sg'])
        d = dpp[...]
        zed = d * mu
        last = lax.broadcasted_iota(jnp.int32, pv.shape, 0) == tr - 1
        dp_ref[...] = d * (1.0 - mu) + jnp.where(last, carry[0:1, :], pltpu.roll(zed, tr - 1, 0))
        carry[...] = zed[0:8, :]

        def colsum(x):
            return jnp.sum(x, axis=0, keepdims=True)

        _acc_rows(dmu_ref, colsum(d * (prev - pv)), i)
        _acc_rows(dw0_ref, colsum(dwpre), i)
        _acc_rows(da0_ref, colsum(dapre), i)
        _acc_rows(dkk_ref, colsum(dkkr * k), i)
        _acc_rows(dka_ref, colsum(dk2 * k * (a - 1.0)), i)
        _acc_rows(drk_ref, colsum(ds * r * k2), i)
        _acc_rows(acc_w, _bdot(m['th'], dwpre, _TN), i)
        _acc_rows(acc_a, _bdot(m['xa'], dapre, _TN), i)
        _acc_rows(acc_g, _bdot(m['sg'], dgv, _TN), i)

        @pl.when(i == n - 1)
        def _():
            dwlw_ref[...] = acc_w[...]
            dwla_ref[...] = acc_a[...]
            dwlg_ref[...] = acc_g[...]

    rev = lambda c: pl.BlockSpec((tr, c), lambda i: (n - 1 - i, 0))
    one, lora = _full_spec((1, RW)), _full_spec((128, RW))
    return _pcall(
        body, name=name, grid=(n,),
        in_specs=[rev(RWKV_PAD), rev(RWKV_PAD)] + [rev(RW)] * 8 + [_full_spec((1, RWKV_PAD))] + [one] * 5 + [lora] * 3,
        out_specs=[rev(RWKV_PAD), _full_spec((1, RWKV_PAD))] + [one] * 5 + [lora] * 3,
        out_shape=[jax.ShapeDtypeStruct((T, RWKV_PAD), F32), jax.ShapeDtypeStruct((1, RWKV_PAD), F32)]
        + [jax.ShapeDtypeStruct((1, RW), F32)] * 5 + [jax.ShapeDtypeStruct((128, RW), F32)] * 3,
        scratch_shapes=[pltpu.VMEM((8, RWKV_PAD), F32), pltpu.VMEM((tr, RWKV_PAD), F32)] + [pltpu.VMEM((128, RW), F32)] * 3,
        compiler_params=_params(("arbitrary",)),
    )(p, pshift, dr_w, dw_w, dk_w, dv_w, da_w, db_w, dz, dg, mu, w0, a0, k_k, k_a, r_k, wlw, wla, wlg)


def _qk_norm(x, g, bd):
    r = lax.rsqrt(_segsum(x * x, bd) * (1.0 / HEAD_DIM) + RMS_EPS)
    return x * r * g, r


def _att_mask(i):
    qi = lax.broadcasted_iota(jnp.int32, (BLOCK, 2 * BLOCK), 0)
    kj = lax.broadcasted_iota(jnp.int32, (BLOCK, 2 * BLOCK), 1)
    band = (kj <= qi + BLOCK) & (kj > qi + BLOCK - WINDOW)
    return band & ((kj >= BLOCK) | (i > 0))


def _att_probs(qh, kh, mask, sink):
    s = _bdot(qh, kh, _NT) * (HEAD_DIM ** -0.5)
    s = jnp.where(mask, s, NEG_BIG)
    m = jnp.maximum(jnp.max(s, axis=-1, keepdims=True), sink)
    pexp = jnp.exp(s - m)
    psink = jnp.exp(sink - m)
    inv = 1.0 / (jnp.sum(pexp, axis=-1, keepdims=True) + psink)
    return pexp * inv, psink * inv


def _att_blocks(n):
    cur = pl.BlockSpec((BLOCK, ATT_COLS), lambda i: (i, 0))
    prev = pl.BlockSpec((BLOCK, ATT_COLS), lambda i: (jnp.maximum(i - 1, 0), 0))
    return cur, prev


def _att_qkv(cur, prev, qn_g, kn_g):
    bq, bk = _seg_matrix(RW, 6), _seg_matrix(KVW, 6)
    qn, rq = _qk_norm(cur[:, 0:RW], qn_g, bq)
    kcur, rkc = _qk_norm(cur[:, RW:RW + KVW], kn_g, bk)
    kprev, _ = _qk_norm(prev[:, RW:RW + KVW], kn_g, bk)
    kc = jnp.concatenate([kprev, kcur], axis=0)
    vc = jnp.concatenate([prev[:, RW + KVW:], cur[:, RW + KVW:]], axis=0)
    return qn, rq, kc, vc, rkc


def att_fwd(pa, qn_g, kn_g, sinks, *, name):
    T = pa.shape[0]
    n = T // BLOCK

    def body(cur_ref, prev_ref, qg_ref, kg_ref, sk_ref, o_ref):
        i = pl.program_id(0)
        qn, _, kc, vc, _ = _att_qkv(cur_ref[...], prev_ref[...], qg_ref[...], kg_ref[...])
        mask = _att_mask(i)
        outs = []
        for h in range(N_HEADS):
            kv = slice((h // ATT_GROUP) * HEAD_DIM, (h // ATT_GROUP + 1) * HEAD_DIM)
            probs, _ = _att_probs(qn[:, h * HEAD_DIM:(h + 1) * HEAD_DIM], kc[:, kv], mask, sk_ref[0:1, h:h + 1])
            outs.append(_bdot(probs, vc[:, kv]))
        o_ref[...] = jnp.concatenate(outs, axis=1)

    cur, prev = _att_blocks(n)
    return _pcall(body, name=name, grid=(n,),
                  in_specs=[cur, prev, _full_spec((1, RW)), _full_spec((1, KVW)), _full_spec((1, LANE))],
                  out_specs=pl.BlockSpec((BLOCK, RW), lambda i: (i, 0)), out_shape=jax.ShapeDtypeStruct((T, RW), F32),
                  compiler_params=_params(("parallel",)))(pa, pa, qn_g, kn_g, sinks)


def att_bwd(pa, do, qn_g, kn_g, sinks, *, name):
    T = pa.shape[0]
    n = T // BLOCK

    def body(cur_ref, prev_ref, do_ref, qg_ref, kg_ref, sk_ref,
             dq_ref, dko_ref, dkn_ref, dvo_ref, dvn_ref, dqg_ref, dsk_ref):
        i = pl.program_id(0)
        cur = cur_ref[...]
        qn, rq, kc, vc, _ = _att_qkv(cur, prev_ref[...], qg_ref[...], kg_ref[...])
        mask = _att_mask(i)
        dov = do_ref[...]
        lane = lax.broadcasted_iota(jnp.int32, (1, LANE), 1)
        dsink = jnp.zeros((1, LANE), F32)
        dqn, dkc, dvc = [], [None, None], [None, None]
        for h in range(N_HEADS):
            g = h // ATT_GROUP
            kv = slice(g * HEAD_DIM, (g + 1) * HEAD_DIM)
            qh, doh = qn[:, h * HEAD_DIM:(h + 1) * HEAD_DIM], dov[:, h * HEAD_DIM:(h + 1) * HEAD_DIM]
            probs, psink = _att_probs(qh, kc[:, kv], mask, sk_ref[0:1, h:h + 1])
            dprobs = _bdot(doh, vc[:, kv], _NT)
            delta = jnp.sum(probs * dprobs, axis=-1, keepdims=True)
            ds = probs * (dprobs - delta) * (HEAD_DIM ** -0.5)
            dsink = dsink + jnp.where(lane == h, -jnp.sum(psink * delta, axis=0, keepdims=True), 0.0)
            dqn.append(_bdot(ds, kc[:, kv]))
            dk_h, dv_h = _bdot(ds, qh, _TN), _bdot(probs, doh, _TN)
            dkc[g] = dk_h if dkc[g] is None else dkc[g] + dk_h
            dvc[g] = dv_h if dvc[g] is None else dvc[g] + dv_h
        dqn = jnp.concatenate(dqn, axis=1)
        dk = jnp.concatenate(dkc, axis=1)
        dv = jnp.concatenate(dvc, axis=1)
        dkn_ref[...], dko_ref[...] = dk[0:BLOCK], dk[BLOCK:]
        dvn_ref[...], dvo_ref[...] = dv[0:BLOCK], dv[BLOCK:]
        qhat = cur[:, 0:RW] * rq
        dqh = dqn * qg_ref[...]
        dq_ref[...] = rq * (dqh - qhat * (_segsum(dqh * qhat, _seg_matrix(RW, 6)) * (1.0 / HEAD_DIM)))
        prod = dqn * qhat
        fold = prod[:, 0:HEAD_DIM]
        for h in range(1, N_HEADS):
            fold = fold + prod[:, h * HEAD_DIM:(h + 1) * HEAD_DIM]
        _acc_rows(dqg_ref, jnp.sum(fold, axis=0, keepdims=True), i)
        _acc_rows(dsk_ref, dsink, i)

    cur, prev = _att_blocks(n)
    kvb = pl.BlockSpec((BLOCK, KVW), lambda i: (i, 0))
    qb = pl.BlockSpec((BLOCK, RW), lambda i: (i, 0))
    return _pcall(body, name=name, grid=(n,),
                  in_specs=[cur, prev, qb, _full_spec((1, RW)), _full_spec((1, KVW)), _full_spec((1, LANE))],
                  out_specs=[qb, kvb, kvb, kvb, kvb, _full_spec((1, HEAD_DIM)), _full_spec((1, LANE))],
                  out_shape=[jax.ShapeDtypeStruct((T, RW), F32)] + [jax.ShapeDtypeStruct((T, KVW), F32)] * 4
                  + [jax.ShapeDtypeStruct((1, HEAD_DIM), F32), jax.ShapeDtypeStruct((1, LANE), F32)],
                  compiler_params=_params(("arbitrary",)))(pa, pa, do, qn_g, kn_g, sinks)


def att_kv_bwd(pa, dq, dko, dkn, dvo, dvn, kn_g, *, name):
    T = pa.shape[0]
    n = T // BLOCK

    def body(pa_ref, dq_ref, dko_ref, dkn_ref, dvo_ref, dvn_ref, kg_ref, dpa_ref, dkg_ref):
        i = pl.program_id(0)
        more = i < n - 1
        dkn_tot = dko_ref[...] + jnp.where(more, dkn_ref[...], 0.0)
        dv_tot = dvo_ref[...] + jnp.where(more, dvn_ref[...], 0.0)
        kraw = pa_ref[:, RW:RW + KVW]
        bk = _seg_matrix(KVW, 6)
        _, rk = _qk_norm(kraw, kg_ref[...], bk)
        khat = kraw * rk
        dkh = dkn_tot * kg_ref[...]
        dpa_ref[:, 0:RW] = dq_ref[...]
        dpa_ref[:, RW:RW + KVW] = rk * (dkh - khat * (_segsum(dkh * khat, bk) * (1.0 / HEAD_DIM)))
        dpa_ref[:, RW + KVW:] = dv_tot
        prod = dkn_tot * khat
        _acc_rows(dkg_ref, jnp.sum(prod[:, 0:HEAD_DIM] + prod[:, HEAD_DIM:], axis=0, keepdims=True), i)

    kvb = pl.BlockSpec((BLOCK, KVW), lambda i: (i, 0))
    nxt = pl.BlockSpec((BLOCK, KVW), lambda i: (jnp.minimum(i + 1, n - 1), 0))
    return _pcall(body, name=name, grid=(n,),
                  in_specs=[pl.BlockSpec((BLOCK, ATT_COLS), lambda i: (i, 0)), pl.BlockSpec((BLOCK, RW), lambda i: (i, 0)),
                            kvb, nxt, kvb, nxt, _full_spec((1, KVW))],
                  out_specs=[pl.BlockSpec((BLOCK, ATT_COLS), lambda i: (i, 0)), _full_spec((1, HEAD_DIM))],
                  out_shape=[jax.ShapeDtypeStruct((T, ATT_COLS), F32), jax.ShapeDtypeStruct((1, HEAD_DIM), F32)],
                  compiler_params=_params(("arbitrary",)))(pa, dq, dko, dkn, dvo, dvn, kn_g)


WKV_CHUNK = 64
WKV_GROUP = 8


def _diag_mask():
    i = lax.broadcasted_iota(jnp.int32, (HEAD_DIM, RW), 0)
    j = lax.broadcasted_iota(jnp.int32, (HEAD_DIM, RW), 1) & (HEAD_DIM - 1)
    return i == j


def _headsums(xs, bd):
    half = RW // 2
    rows = []
    for x in xs:
        hi = x.astype(BF16)
        r1 = x - hi.astype(F32)
        mid = r1.astype(BF16)
        lo = (r1 - mid.astype(F32)).astype(BF16)
        for sl in (slice(0, half), slice(half, RW)):
            rows.append(jnp.concatenate([hi[:, sl], mid[:, sl], lo[:, sl]], axis=1))
    out = lax.dot_general(jnp.concatenate(rows, axis=0), bd, (((1,), (0,)), ((), ())), preferred_element_type=F32)
    return [jnp.concatenate([out[2 * n * HEAD_DIM:(2 * n + 1) * HEAD_DIM], out[(2 * n + 1) * HEAD_DIM:(2 * n + 2) * HEAD_DIM]],
                            axis=1) for n in range(len(xs))]


def _heads_matrix():
    bd = _seg_matrix(RW // 2, 6)
    return jnp.concatenate([bd, bd, bd], axis=0)


def _headsum(x):
    low = lax.broadcasted_iota(jnp.int32, (HEAD_DIM, LANE), 1) < HEAD_DIM
    tiles = []
    for c in range(RW // LANE):
        xt = x[:, c * LANE:(c + 1) * LANE]
        s_lo = jnp.sum(jnp.where(low, xt, 0.0), axis=1, keepdims=True)
        s_hi = jnp.sum(jnp.where(low, 0.0, xt), axis=1, keepdims=True)
        tiles.append(jnp.where(low, s_lo, s_hi))
    return jnp.concatenate(tiles, axis=1)


def _cols(rows, diag, bd):
    return _headsums([jnp.where(diag, r, 0.0) for r in rows], bd)


def _row(x, diag):
    return jnp.sum(jnp.where(diag, x, 0.0), axis=0, keepdims=True)


def wkv_fwd(r, w, k, v, a, b, *, name):
    T = r.shape[0]
    ch = min(WKV_CHUNK, T)
    ngroups = ch // WKV_GROUP

    def body(r_ref, w_ref, k_ref, v_ref, a_ref, b_ref, y_ref, st_ref, s_scr):
        @pl.when(pl.program_id(0) == 0)
        def _():
            s_scr[...] = jnp.zeros_like(s_scr)

        diag, bd = _diag_mask(), _heads_matrix()

        def group(gi, S):
            t0 = pl.multiple_of(gi * WKV_GROUP, WKV_GROUP)
            rows = pl.ds(t0, WKV_GROUP)
            R, W, K, V, A, B = (ref[rows, :] for ref in (r_ref, w_ref, k_ref, v_ref, a_ref, b_ref))
            vcols = _cols([V[s:s + 1] for s in range(WKV_GROUP)], diag, bd)
            yrows = []
            for s in range(WKV_GROUP):
                sa = _headsum(S * A[s:s + 1])
                S = S * W[s:s + 1] + sa * B[s:s + 1] + vcols[s] * K[s:s + 1]
                st_ref[t0 + s] = S
                yrows.append(_row(_headsum(S * R[s:s + 1]), diag))
            y_ref[rows, :] = jnp.concatenate(yrows, axis=0)
            return S

        s_scr[...] = lax.fori_loop(0, ngroups, group, s_scr[...])

    vec = pl.BlockSpec((ch, RW), lambda c: (c, 0))
    return _pcall(
        body, name=name, grid=(T // ch,), in_specs=[vec] * 6,
        out_specs=[vec, pl.BlockSpec((ch, HEAD_DIM, RW), lambda c: (c, 0, 0))],
        out_shape=[jax.ShapeDtypeStruct((T, RW), F32), jax.ShapeDtypeStruct((T, HEAD_DIM, RW), F32)],
        scratch_shapes=[pltpu.VMEM((HEAD_DIM, RW), F32)],
        compiler_params=_params(("arbitrary",)),
    )(r, w, k, v, a, b)


def wkv_bwd(r, w, k, v, a, b, dy, states, *, name):
    T = r.shape[0]
    ch = min(WKV_CHUNK, T)
    nchunks = T // ch
    ngroups = ch // WKV_GROUP

    def body(r_ref, w_ref, k_ref, v_ref, a_ref, b_ref, dy_ref, st_ref, stp_ref,
             dr_ref, dw_ref, dk_ref, dv_ref, da_ref, db_ref, ds_scr):
        step = pl.program_id(0)

        @pl.when(step == 0)
        def _():
            ds_scr[...] = jnp.zeros_like(ds_scr)

        has_prev_chunk = step < nchunks - 1
        diag, bd = _diag_mask(), _heads_matrix()
        colsum = lambda x: jnp.sum(x, axis=0, keepdims=True)

        def group(gj, dS):
            gi = ngroups - 1 - gj
            t0 = pl.multiple_of(gi * WKV_GROUP, WKV_GROUP)
            rows = pl.ds(t0, WKV_GROUP)
            R, W, K, V, A, B, DY = (ref[rows, :] for ref in (r_ref, w_ref, k_ref, v_ref, a_ref, b_ref, dy_ref))
            before = jnp.where(gi > 0, st_ref[jnp.maximum(t0 - 1, 0)], jnp.where(has_prev_chunk, stp_ref[0], 0.0))
            prev_state = lambda s: st_ref[t0 + s - 1] if s > 0 else before
            steps = range(WKV_GROUP)
            dycols = _cols([DY[s:s + 1] for s in steps], diag, bd)
            vcols = _cols([V[s:s + 1] for s in steps], diag, bd)
            sas = _headsums([prev_state(s) * A[s:s + 1] for s in steps], bd)
            got = [[None] * WKV_GROUP for _ in range(6)]
            for s in reversed(steps):
                Sp = prev_state(s)
                dS = dS + dycols[s] * R[s:s + 1]
                got[0][s] = colsum(st_ref[t0 + s] * dycols[s])
                got[3][s] = _row(_headsum(dS * K[s:s + 1]), diag)
                got[2][s] = colsum(dS * vcols[s])
                dsa = _headsum(dS * B[s:s + 1])
                got[5][s] = colsum(dS * sas[s])
                got[1][s] = colsum(dS * Sp)
                got[4][s] = colsum(Sp * dsa)
                dS = dS * W[s:s + 1] + dsa * A[s:s + 1]
            for q, ref in enumerate((dr_ref, dw_ref, dk_ref, dv_ref, da_ref, db_ref)):
                ref[rows, :] = jnp.concatenate(got[q], axis=0)
            return dS

        ds_scr[...] = lax.fori_loop(0, ngroups, group, ds_scr[...])

    vec = pl.BlockSpec((ch, RW), lambda c: (nchunks - 1 - c, 0))
    st_spec = pl.BlockSpec((ch, HEAD_DIM, RW), lambda c: (nchunks - 1 - c, 0, 0))
    stp_spec = pl.BlockSpec((1, HEAD_DIM, RW), lambda c: (jnp.maximum((nchunks - 1 - c) * ch - 1, 0), 0, 0))
    return _pcall(
        body, name=name, grid=(nchunks,), in_specs=[vec] * 7 + [st_spec, stp_spec], out_specs=[vec] * 6,
        out_shape=[jax.ShapeDtypeStruct((T, RW), F32)] * 6,
        scratch_shapes=[pltpu.VMEM((HEAD_DIM, RW), F32)],
        compiler_params=_params(("arbitrary",)),
    )(r, w, k, v, a, b, dy, states, states)


_HBM = pl.BlockSpec(memory_space=pltpu.HBM)
_MESH = pl.DeviceIdType.MESH


def _place():
    x, y, c = lax.axis_index("x"), lax.axis_index("y"), lax.axis_index("c")
    return x, y, c, [(1 - x, y), (x, 1 - y), (1 - x, 1 - y)]


def _remote(src, dst, send_sem, recv_sem, to):
    return pltpu.make_async_remote_copy(src_ref=src, dst_ref=dst, send_sem=send_sem, recv_sem=recv_sem, device_id=to,
                                        device_id_type=_MESH)


def _dma_sems(*counts):
    return [pltpu.SemaphoreType.DMA((n,)) for n in counts]


def gather_weights(shards, *, name):
    n = len(shards)

    def body(*refs):
        ins, outs = refs[:n], refs[n:2 * n]
        ici_send, ici_recv, d2d_send, d2d_recv, local_sems = refs[2 * n:]
        x, y, c, chips = _place()
        me, sibling = 2 * x + y, (x, y, 1 - c)

        def half(i, which):
            rh = ins[i].shape[0] // 2
            return pl.ds(which * rh, rh)

        local = [pltpu.make_async_copy(ins[i], outs[i].at[me], local_sems.at[i]) for i in range(n)]
        for cp in local:
            cp.start()
        sends = []
        for i in range(n):
            for j, (qx, qy) in enumerate(chips):
                k = 3 * i + j
                sends.append(_remote(ins[i].at[half(i, c)], outs[i].at[me, half(i, c)], ici_send.at[k], ici_recv.at[k], (qx, qy, c)))
        for cp in sends:
            cp.start()
        passed = []
        for i in range(n):
            for j, (qx, qy) in enumerate(chips):
                k = 3 * i + j
                landed = outs[i].at[2 * qx + qy, half(i, c)]
                _remote(landed, landed, ici_send.at[k], ici_recv.at[k], (qx, qy, c)).wait_recv()
                cp = _remote(landed, landed, d2d_send.at[k], d2d_recv.at[k], sibling)
                cp.start()
                passed.append(cp)
        for i in range(n):
            for j, (qx, qy) in enumerate(chips):
                k = 3 * i + j
                theirs = outs[i].at[2 * qx + qy, half(i, 1 - c)]
                _remote(theirs, theirs, d2d_send.at[k], d2d_recv.at[k], sibling).wait_recv()
        for cp in sends + passed:
            cp.wait_send()
        for cp in local:
            cp.wait()

    return _pcall(body, name=name, in_specs=[_HBM] * n, out_specs=[_HBM] * n,
                  out_shape=[jax.ShapeDtypeStruct((N_CHIPS,) + s.shape, s.dtype) for s in shards],
                  scratch_shapes=_dma_sems(3 * n, 3 * n, 3 * n, 3 * n, n), compiler_params=_params())(*shards)


def to_sibling(arrays, take_other_half, *, name):
    n = len(arrays)

    def body(*refs):
        ins, outs = refs[:n], refs[n:2 * n]
        send_sems, recv_sems = refs[2 * n:]
        x, y, c, _ = _place()
        cps = []
        for i in range(n):
            src = ins[i]
            if take_other_half:
                rh = src.shape[1] // 2
                src = src.at[:, pl.ds((1 - c) * rh, rh)]
            cps.append(_remote(src, outs[i], send_sems.at[i], recv_sems.at[i], (x, y, 1 - c)))
        for cp in cps:
            cp.start()
        for cp in cps:
            cp.wait_recv()
        for cp in cps:
            cp.wait_send()

    def out_of(a):
        shape = (a.shape[0], a.shape[1] // 2, a.shape[2]) if take_other_half else a.shape
        return jax.ShapeDtypeStruct(shape, a.dtype)

    return _pcall(body, name=name, in_specs=[_HBM] * n, out_specs=[_HBM] * n, out_shape=[out_of(a) for a in arrays],
                  scratch_shapes=_dma_sems(n, n), compiler_params=_params())(*arrays)


def exchange_chips(arrays, *, name):
    n = len(arrays)

    def body(*refs):
        ins, outs = refs[:n], refs[n:2 * n]
        send_sems, recv_sems, local_sems = refs[2 * n:]
        x, y, c, chips = _place()
        me = 2 * x + y
        local = [pltpu.make_async_copy(ins[i].at[me], outs[i].at[me], local_sems.at[i]) for i in range(n)]
        for cp in local:
            cp.start()
        sends = []
        for i in range(n):
            for j, (qx, qy) in enumerate(chips):
                k = 3 * i + j
                sends.append(_remote(ins[i].at[2 * qx + qy], outs[i].at[me], send_sems.at[k], recv_sems.at[k], (qx, qy, c)))
        for cp in sends:
            cp.start()
        for i in range(n):
            for j, (qx, qy) in enumerate(chips):
                k = 3 * i + j
                landed = outs[i].at[2 * qx + qy]
                _remote(landed, landed, send_sems.at[k], recv_sems.at[k], (qx, qy, c)).wait_recv()
        for cp in sends:
            cp.wait_send()
        for cp in local:
            cp.wait()

    return _pcall(body, name=name, in_specs=[_HBM] * n, out_specs=[_HBM] * n,
                  out_shape=[jax.ShapeDtypeStruct(a.shape, a.dtype) for a in arrays],
                  scratch_shapes=_dma_sems(3 * n, 3 * n, n), compiler_params=_params())(*arrays)


def _core_index():
    return lax.axis_index("c").astype(jnp.int32).reshape(1)


def pair_sum(g, theirs, wire_dtype, *, name):
    _, R, C = g.shape
    rh = R // 2
    tr = _tile(rh, 256, 16)
    nt = rh // tr

    def body(c_ref, g_ref, t_ref, q_ref, qw_ref):
        q = g_ref[...] + t_ref[...]
        q_ref[...] = q
        qw_ref[...] = q.astype(wire_dtype)

    blk = pl.BlockSpec((1, tr, C), lambda b, i, c_ref: (b, i, 0))
    mine = pl.BlockSpec((1, tr, C), lambda b, i, c_ref: (b, c_ref[0] * nt + i, 0))
    grid_spec = pltpu.PrefetchScalarGridSpec(num_scalar_prefetch=1, grid=(N_CHIPS, nt), in_specs=[mine, blk], out_specs=[blk, blk])
    return _pcall(body, name=name, grid_spec=grid_spec,
                  out_shape=[jax.ShapeDtypeStruct((N_CHIPS, rh, C), F32), jax.ShapeDtypeStruct((N_CHIPS, rh, C), wire_dtype)],
                  compiler_params=_params(("parallel", "parallel")))(_core_index(), g, theirs)


def half_sum(own, landed, *, name):
    _, rh, C = own.shape
    tr = _tile(rh, 256, 16)

    def body(me_ref, own_ref, land_ref, o_ref):
        total = None
        for p in range(N_CHIPS):
            term = jnp.where(me_ref[0] == p, own_ref[p], land_ref[p].astype(F32))
            total = term if total is None else total + term
        o_ref[...] = total

    blk = pl.BlockSpec((N_CHIPS, tr, C), lambda i, me_ref: (0, i, 0))
    grid_spec = pltpu.PrefetchScalarGridSpec(num_scalar_prefetch=1, grid=(rh // tr,), in_specs=[blk, blk],
                                             out_specs=pl.BlockSpec((tr, C), lambda i, me_ref: (i, 0)))
    me = (2 * lax.axis_index("x") + lax.axis_index("y")).astype(jnp.int32).reshape(1)
    return _pcall(body, name=name, grid_spec=grid_spec, out_shape=jax.ShapeDtypeStruct((rh, C), F32),
                  compiler_params=_params(("parallel",)))(me, own, landed)


def adamw(w, m, v, mine, theirs, *, name):
    R, C = w.shape
    rh = R // 2
    tr = _tile(rh, 256, 8)
    nt = rh // tr

    def body(c_ref, w_ref, m_ref, v_ref, a_ref, b_ref, g_ref, d_ref, nm_ref, nv_ref):
        is_mine = (pl.program_id(0) // nt) == c_ref[0]
        g = jnp.where(is_mine, a_ref[...], b_ref[...])
        g_ref[...] = g
        nm = ADAM_B1 * m_ref[...] + (1.0 - ADAM_B1) * g
        nv = ADAM_B2 * v_ref[...] + (1.0 - ADAM_B2) * (g * g)
        nm_ref[...] = nm
        nv_ref[...] = nv
        m_hat = nm / (1.0 - ADAM_B1 ** ADAM_STEP)
        v_hat = nv / (1.0 - ADAM_B2 ** ADAM_STEP)
        d_ref[...] = -ADAM_LR * (m_hat / (jnp.sqrt(v_hat) + ADAM_EPS) + ADAM_WD * w_ref[...])

    full = pl.BlockSpec((tr, C), lambda i, c_ref: (i, 0))
    part = pl.BlockSpec((tr, C), lambda i, c_ref: (i % nt, 0))
    grid_spec = pltpu.PrefetchScalarGridSpec(num_scalar_prefetch=1, grid=(2 * nt,), in_specs=[full] * 3 + [part] * 2,
                                             out_specs=[full] * 4)
    return _pcall(body, name=name, grid_spec=grid_spec, out_shape=[jax.ShapeDtypeStruct((R, C), F32)] * 4,
                  compiler_params=_params(("parallel",)))(_core_index(), w, m, v, mine, theirs)


def _to_blocks(full, axis):
    r, c = full.shape
    if axis == 1:
        return full.reshape(r, N_CHIPS, c // N_CHIPS).transpose(1, 0, 2)
    return full.reshape(N_CHIPS, r // N_CHIPS, c)


def _from_blocks(blocks, axis):
    _, r, c = blocks.shape
    if axis == 1:
        return blocks.transpose(1, 0, 2).reshape(r, N_CHIPS * c)
    return blocks.reshape(N_CHIPS * r, c)


def _ffn_fwd(x, norm, wg, wu, wd, tag):
    h = rms_fwd(x, norm, name=tag + "_norm")
    gate = mm(h, wg, name=tag + "_gate")
    up = mm(h, wu, name=tag + "_up")
    act = swiglu_fwd(gate, up, name=tag + "_act")
    out = mm(act, wd, scale=0.5, res=x, name=tag + "_down")
    return out, (h, gate, up, act)


def _ffn_bwd(dout, x, saved, norm, wg, wu, wd, tag):
    h, gate, up, act = saved
    dact = mm(dout, wd, tb=True, scale=0.5, name=tag + "_dact")
    dwd = mm(act, dout, ta=True, scale=0.5, name=tag + "_dwd")
    dgate, dup = swiglu_bwd(dact, gate, up, name=tag + "_dswiglu")
    dwg = mm(h, dgate, ta=True, name=tag + "_dwg")
    dwu = mm(h, dup, ta=True, name=tag + "_dwu")
    dh = mm(dgate, wg, tb=True, name=tag + "_dh_gate")
    dh = mm(dup, wu, tb=True, res=dh, name=tag + "_dh_up")
    dx, dnorm = rms_bwd(dh, x, norm, dout, name=tag + "_dnorm")
    return dx, dnorm, dwg, dwu, dwd


def _step(A):
    x, tgt = A['x'][0], A['loss_target'][0]
    T = x.shape[0]
    w = {n: A[n][0] for n in WEIGHT_NAMES}
    row = lambda a: a.reshape(1, -1)

    gathered = gather_weights([w[n].astype(BF16) for n, _ in BIG], name="gather_weights")
    full = {n: _from_blocks(b, axis) for (n, axis), b in zip(BIG, gathered)}
    w_in_r = _pad_rwkv_cols(full['w_in'][:, :RWKV_COLS])
    w_in_a = full['w_in'][:, RWKV_COLS:RWKV_COLS + ATT_COLS]
    w_in_g = full['w_in'][:, RWKV_COLS + ATT_COLS:]
    wlw, wla, wlg = (_pad_rows(full[n], 128).astype(F32) for n in ('rwkv_w_lora_up', 'rwkv_a_lora_up', 'rwkv_g_lora_up'))
    mu = _pad_rwkv_cols(row(w['rwkv_mu']))
    w0, a0, k_k, k_a, r_k, ln_w, ln_b = (row(w[n]) for n in ('rwkv_w0', 'rwkv_a0', 'rwkv_k_k', 'rwkv_k_a', 'rwkv_r_k',
                                                               'rwkv_ln_w', 'rwkv_ln_b'))
    qg = jnp.tile(row(w['attn_q_norm']), (1, N_HEADS))
    kg = jnp.tile(row(w['attn_k_norm']), (1, KVW // HEAD_DIM))
    sinks = jnp.pad(row(w['attn_sinks']), ((0, 0), (0, LANE - N_HEADS)))
    n1, nmix, n2, nfin = (row(w[n]) for n in ('ffn1_norm', 'mix_norm', 'ffn2_norm', 'final_norm'))

    x1, ffn1 = _ffn_fwd(x, n1, full['ffn1_w_gate'], full['ffn1_w_up'], full['ffn1_w_down'], "ffn1")
    h2 = rms_fwd(x1, nmix, name="mix_norm")
    pr = mm(h2, w_in_r, name="proj_rwkv")
    pa = mm(h2, w_in_a, name="proj_att")
    pg = mm(h2, w_in_g, name="proj_gate")
    pr_shift = jnp.pad(pr, ((1, 0), (0, 0)))[:-1]
    r, dec, k2, v, a, b, sg = rwkv_pre_fwd(pr, pr_shift, mu, w0, a0, k_k, k_a, wlw, wla, wlg, name="rwkv_pre")
    y, states = wkv_fwd(r, dec, k2, v, a, b, name="wkv_fwd")
    yr = rwkv_post_fwd(y, r, k2, v, sg, wlg, ln_w, ln_b, r_k, name="rwkv_post")
    ya = att_fwd(pa, qg, kg, sinks, name="att_fwd")
    br = mm(yr, full['w_branch_rwkv'], name="branch_rwkv")
    ba = mm(ya, full['w_branch_attn'], name="branch_att")
    mg = merge_fwd(br, ba, pg, name="merge")
    x2 = mm(mg, full['w_out'], res=x1, name="mix_out")
    x3, ffn2 = _ffn_fwd(x2, n2, full['ffn2_w_gate'], full['ffn2_w_up'], full['ffn2_w_down'], "ffn2")
    dx3, d_nfin, loss = final_loss(x3, tgt, nfin, name="final_loss")

    G = {'final_norm': d_nfin}
    dx2, G['ffn2_norm'], G['ffn2_w_gate'], G['ffn2_w_up'], G['ffn2_w_down'] = _ffn_bwd(
        dx3, x2, ffn2, n2, full['ffn2_w_gate'], full['ffn2_w_up'], full['ffn2_w_down'], "ffn2")
    dmg = mm(dx2, full['w_out'], tb=True, name="d_merge")
    G['w_out'] = mm(mg, dx2, ta=True, name="d_w_out")
    dbr, dba, dpg = merge_bwd(dmg, br, ba, pg, name="merge_bwd")
    dyr = mm(dbr, full['w_branch_rwkv'], tb=True, name="d_y_rwkv")
    G['w_branch_rwkv'] = mm(yr, dbr, ta=True, name="d_w_branch_rwkv")
    dya = mm(dba, full['w_branch_attn'], tb=True, name="d_y_att")
    G['w_branch_attn'] = mm(ya, dba, ta=True, name="d_w_branch_att")
    dy, dz, dg, G['rwkv_ln_w'], G['rwkv_ln_b'] = rwkv_post_bwd(dyr, y, r, k2, v, sg, wlg, ln_w, ln_b, r_k, name="rwkv_post_bwd")
    wkv_grads = wkv_bwd(r, dec, k2, v, a, b, dy, states, name="wkv_bwd")
    (dpr, d_mu, G['rwkv_w0'], G['rwkv_a0'], G['rwkv_k_k'], G['rwkv_k_a'], G['rwkv_r_k'], d_wlw, d_wla, d_wlg) = rwkv_pre_bwd(
        pr, pr_shift, *wkv_grads, dz, dg, mu, w0, a0, k_k, k_a, r_k, wlw, wla, wlg, name="rwkv_pre_bwd")
    G['rwkv_mu'] = _unpad_rwkv_cols(d_mu)
    G['rwkv_w_lora_up'], G['rwkv_a_lora_up'], G['rwkv_g_lora_up'] = d_wlw[:DECAY_LORA], d_wla[:ICLR_LORA], d_wlg[:GATE_LORA]
    dq, dko, dkn, dvo, dvn, G['attn_q_norm'], d_sinks = att_bwd(pa, dya, qg, kg, sinks, name="att_bwd")
    G['attn_sinks'] = d_sinks[:, :N_HEADS]
    dpa, G['attn_k_norm'] = att_kv_bwd(pa, dq, dko, dkn, dvo, dvn, kg, name="att_kv_bwd")
    d_w_in_r = mm(h2, dpr, ta=True, name="d_w_in_rwkv")
    d_w_in_a = mm(h2, dpa, ta=True, name="d_w_in_att")
    d_w_in_g = mm(h2, dpg, ta=True, name="d_w_in_gate")
    G['w_in'] = jnp.concatenate([_unpad_rwkv_cols(d_w_in_r), d_w_in_a, d_w_in_g], axis=1)
    dh2 = mm(dpr, w_in_r, tb=True, name="d_h2_rwkv")
    dh2 = mm(dpa, w_in_a, tb=True, res=dh2, name="d_h2_att")
    dh2 = mm(dpg, w_in_g, tb=True, res=dh2, name="d_h2_gate")
    dx1, G['mix_norm'] = rms_bwd(dh2, x1, nmix, dx2, name="d_mix_norm")
    dx0, G['ffn1_norm'], G['ffn1_w_gate'], G['ffn1_w_up'], G['ffn1_w_down'] = _ffn_bwd(
        dx1, x, ffn1, n1, full['ffn1_w_gate'], full['ffn1_w_up'], full['ffn1_w_down'], "ffn1")

    small_shapes = [(w[n].size,) for n in SMALL] + [(1,)]

    def small_rows(parts):
        vec = jnp.concatenate([p.reshape(-1) for p in parts])
        return jnp.pad(vec, (0, SMALL_ROWS * FLAT_W - vec.shape[0])).reshape(SMALL_ROWS, FLAT_W)

    small = small_rows([G[n] for n in SMALL] + [loss[0, :1]])
    names = [n for n, _ in BIG] + ['small']
    blocks = [_to_blocks(G[n], axis) for n, axis in BIG] + [jnp.broadcast_to(small[None], (N_CHIPS,) + small.shape)]
    wire = [BF16] * len(BIG) + [F32]
    from_sibling = to_sibling(blocks, True, name="grads_to_sibling")
    pair = [pair_sum(g, t, wd, name="pair_sum_" + n) for g, t, wd, n in zip(blocks, from_sibling, wire, names)]
    landed = exchange_chips([q for _, q in pair], name="exchange_grads")
    halves = [half_sum(own, l, name="half_sum_" + n) for (own, _), l, n in zip(pair, landed, names)]
    other_halves = to_sibling(halves, False, name="halves_to_sibling")

    def local(prefix, n):
        if n != 'small':
            return A[prefix + n][0]
        return small_rows([A[prefix + s] for s in SMALL] + [jnp.zeros((1,), F32)])

    result = {}
    for n, mine, theirs in zip(names, halves, other_halves):
        outs4 = adamw(local('', n), local('m_', n), local('v_', n), mine, theirs, name="adamw_" + n)
        for kind, o in zip(('grad_', 'delta_', 'new_m_', 'new_v_'), outs4):
            if n != 'small':
                result[kind + n] = o[None]
            else:
                for s, part in zip(SMALL + ['loss'], _unpack_vec(o.reshape(-1), small_shapes)):
                    result[kind + s] = part.reshape(A[s].shape) if s != 'loss' else part.reshape(())
    outs = [result['grad_loss'], dx0[None]]
    for kind in ('grad_', 'delta_', 'new_m_', 'new_v_'):
        outs += [result[kind + n] for n in WEIGHT_NAMES]
    return tuple(outs)


def _unpack_vec(vec, shapes):
    out, off = [], 0
    for (n,) in shapes:
        out.append(vec[off:off + n])
        off += n
    return out


def kernel(x, ffn1_norm, ffn1_w_gate, ffn1_w_up, ffn1_w_down, mix_norm, w_in, rwkv_mu, rwkv_w0, rwkv_w_lora_up, rwkv_a0, rwkv_a_lora_up, rwkv_g_lora_up, rwkv_k_k, rwkv_k_a, rwkv_r_k, rwkv_ln_w, rwkv_ln_b, attn_q_norm, attn_k_norm, attn_sinks, w_branch_rwkv, w_branch_attn, w_out, ffn2_norm, ffn2_w_gate, ffn2_w_up, ffn2_w_down, final_norm, loss_target, m_ffn1_norm, m_ffn1_w_gate, m_ffn1_w_up, m_ffn1_w_down, m_mix_norm, m_w_in, m_rwkv_mu, m_rwkv_w0, m_rwkv_w_lora_up, m_rwkv_a0, m_rwkv_a_lora_up, m_rwkv_g_lora_up, m_rwkv_k_k, m_rwkv_k_a, m_rwkv_r_k, m_rwkv_ln_w, m_rwkv_ln_b, m_attn_q_norm, m_attn_k_norm, m_attn_sinks, m_w_branch_rwkv, m_w_branch_attn, m_w_out, m_ffn2_norm, m_ffn2_w_gate, m_ffn2_w_up, m_ffn2_w_down, m_final_norm, v_ffn1_norm, v_ffn1_w_gate, v_ffn1_w_up, v_ffn1_w_down, v_mix_norm, v_w_in, v_rwkv_mu, v_rwkv_w0, v_rwkv_w_lora_up, v_rwkv_a0, v_rwkv_a_lora_up, v_rwkv_g_lora_up, v_rwkv_k_k, v_rwkv_k_a, v_rwkv_r_k, v_rwkv_ln_w, v_rwkv_ln_b, v_attn_q_norm, v_attn_k_norm, v_attn_sinks, v_w_branch_rwkv, v_w_branch_attn, v_w_out, v_ffn2_norm, v_ffn2_w_gate, v_ffn2_w_up, v_ffn2_w_down, v_final_norm):
    return _step(dict(locals()))
```

```python
import functools

import jax
import jax.numpy as jnp
from jax import lax
from jax.experimental import pallas as pl
from jax.experimental.pallas import tpu as pltpu

F32 = jnp.float32
BF16 = jnp.bfloat16

D_MODEL = 1024
D_FF = 2816
HEAD_DIM = 64
N_HEADS = 8
RW = 512
KVW = 128
ATT_GROUP = 4
WINDOW = 128
BLOCK = 128
DECAY_LORA, ICLR_LORA, GATE_LORA = 32, 32, 96
RWKV_COLS = 3 * RW + DECAY_LORA + ICLR_LORA + GATE_LORA
ATT_COLS = RW + 2 * KVW
GATE_COLS = 2 * D_MODEL
RWKV_PAD = 3 * RW + 3 * 128
RMS_EPS = 1e-6
GN_EPS = 64e-5
N_CHIPS = 4
LANE = 128
FLAT_W = 1024
SMALL_ROWS = 32
NEG_BIG = -1e30

ADAM_LR, ADAM_B1, ADAM_B2, ADAM_EPS, ADAM_WD, ADAM_STEP = 0.001, 0.9, 0.999, 1e-08, 0.01, 10

VMEM_LIMIT = 56 * 1024 * 1024

WEIGHT_NAMES = ['ffn1_norm', 'ffn1_w_gate', 'ffn1_w_up', 'ffn1_w_down', 'mix_norm', 'w_in', 'rwkv_mu', 'rwkv_w0',
                'rwkv_w_lora_up', 'rwkv_a0', 'rwkv_a_lora_up', 'rwkv_g_lora_up', 'rwkv_k_k', 'rwkv_k_a', 'rwkv_r_k',
                'rwkv_ln_w', 'rwkv_ln_b', 'attn_q_norm', 'attn_k_norm', 'attn_sinks', 'w_branch_rwkv',
                'w_branch_attn', 'w_out', 'ffn2_norm', 'ffn2_w_gate', 'ffn2_w_up', 'ffn2_w_down', 'final_norm']
BIG = [('ffn1_w_gate', 1), ('ffn1_w_up', 1), ('ffn1_w_down', 0), ('w_in', 1), ('rwkv_w_lora_up', 1),
       ('rwkv_a_lora_up', 1), ('rwkv_g_lora_up', 1), ('w_branch_rwkv', 1), ('w_branch_attn', 1), ('w_out', 0),
       ('ffn2_w_gate', 1), ('ffn2_w_up', 1), ('ffn2_w_down', 0)]
SMALL = ['ffn1_norm', 'mix_norm', 'rwkv_mu', 'rwkv_w0', 'rwkv_a0', 'rwkv_k_k', 'rwkv_k_a', 'rwkv_r_k', 'rwkv_ln_w',
         'rwkv_ln_b', 'attn_q_norm', 'attn_k_norm', 'attn_sinks', 'ffn2_norm', 'final_norm']


def _pcall(body, **kw):
    return pl.pallas_call(body, **kw)


def _params(sem=None, **kw):
    if sem is not None:
        kw['dimension_semantics'] = sem
    return pltpu.CompilerParams(vmem_limit_bytes=VMEM_LIMIT, **kw)


def _tile(n, cap, mult):
    best = None
    for t in range(mult, min(n, cap) + 1, mult):
        if n % t == 0:
            best = t
    return best or n


def _sigmoid(z):
    return 1.0 / (1.0 + jnp.exp(-z))


def _softplus(z):
    return jnp.maximum(z, 0.0) + jnp.log(1.0 + jnp.exp(-jnp.abs(z)))


def _bdot(a, b, dims=(((1,), (0,)), ((), ()))):
    return lax.dot_general(a.astype(BF16), b.astype(BF16), dims, preferred_element_type=F32)


_NT = (((1,), (1,)), ((), ()))
_TN = (((0,), (0,)), ((), ()))


def _segsum(x, bd):
    hi = x.astype(BF16)
    r1 = x - hi.astype(F32)
    mid = r1.astype(BF16)
    lo = (r1 - mid.astype(F32)).astype(BF16)
    dot = functools.partial(lax.dot_general, dimension_numbers=(((1,), (0,)), ((), ())), preferred_element_type=F32)
    return dot(hi, bd) + dot(mid, bd) + dot(lo, bd)


_LORA_EDGES = (3 * RW, 3 * RW + DECAY_LORA, 3 * RW + DECAY_LORA + ICLR_LORA, RWKV_COLS)


def _pad_rwkv_cols(x):
    parts = [x[..., :3 * RW]]
    for lo, hi in zip(_LORA_EDGES[:-1], _LORA_EDGES[1:]):
        parts.append(jnp.pad(x[..., lo:hi], [(0, 0)] * (x.ndim - 1) + [(0, 128 - (hi - lo))]))
    return jnp.concatenate(parts, axis=-1)


def _unpad_rwkv_cols(x):
    parts = [x[..., :3 * RW]]
    for j, (lo, hi) in enumerate(zip(_LORA_EDGES[:-1], _LORA_EDGES[1:])):
        parts.append(x[..., 3 * RW + 128 * j:3 * RW + 128 * j + (hi - lo)])
    return jnp.concatenate(parts, axis=-1)


def _pad_rows(x, rows):
    return jnp.pad(x, [(0, rows - x.shape[0])] + [(0, 0)] * (x.ndim - 1))


def mm(a, b, *, name, ta=False, tb=False, scale=None, res=None, out_dtype=F32):
    M, K = (a.shape[1], a.shape[0]) if ta else a.shape
    N = b.shape[0] if tb else b.shape[1]
    assert (b.shape[1] if tb else b.shape[0]) == K
    tm, tn, tk = _tile(M, 512, 128), _tile(N, 1408, 128), _tile(K, 1408, 128)
    nk = K // tk
    dims = (((0 if ta else 1,), (1 if tb else 0,)), ((), ()))

    def body(*refs):
        if res is None:
            a_ref, b_ref, o_ref, acc_ref = refs
            r_ref = None
        else:
            a_ref, b_ref, r_ref, o_ref, acc_ref = refs
        k = pl.program_id(2)
        part = _bdot(a_ref[...], b_ref[...], dims)

        @pl.when(k == 0)
        def _():
            acc_ref[...] = part

        @pl.when(k > 0)
        def _():
            acc_ref[...] += part

        @pl.when(k == nk - 1)
        def _():
            o = acc_ref[...]
            if scale is not None:
                o = o * scale
            if r_ref is not None:
                o = o + r_ref[...].astype(F32)
            o_ref[...] = o.astype(out_dtype)

    a_spec = pl.BlockSpec((tk, tm), lambda i, j, k: (k, i)) if ta else pl.BlockSpec((tm, tk), lambda i, j, k: (i, k))
    b_spec = pl.BlockSpec((tn, tk), lambda i, j, k: (j, k)) if tb else pl.BlockSpec((tk, tn), lambda i, j, k: (k, j))
    o_spec = pl.BlockSpec((tm, tn), lambda i, j, k: (i, j))
    in_specs = [a_spec, b_spec] + ([o_spec] if res is not None else [])
    args = (a, b) + ((res,) if res is not None else ())
    return _pcall(
        body, name=name, grid=(M // tm, N // tn, nk), in_specs=in_specs, out_specs=o_spec,
        out_shape=jax.ShapeDtypeStruct((M, N), out_dtype), scratch_shapes=[pltpu.VMEM((tm, tn), F32)],
        compiler_params=_params(("parallel", "parallel", "arbitrary")),
    )(*args)


def _row_spec(tr, c):
    return pl.BlockSpec((tr, c), lambda i: (i, 0))


def _full_spec(shape):
    return pl.BlockSpec(shape, lambda i: (0,) * len(shape))


def _acc_rows(ref, val, i):
    @pl.when(i == 0)
    def _():
        ref[...] = val

    @pl.when(i > 0)
    def _():
        ref[...] += val


def rms_fwd(x, g, *, name):
    T, D = x.shape
    tr = _tile(T, 512, 8)

    def body(x_ref, g_ref, h_ref):
        xv = x_ref[...]
        r = lax.rsqrt(jnp.mean(xv * xv, axis=-1, keepdims=True) + RMS_EPS)
        h_ref[...] = (xv * r * g_ref[...]).astype(BF16)

    return _pcall(body, name=name, grid=(T // tr,), in_specs=[_row_spec(tr, D), _full_spec((1, D))],
                  out_specs=_row_spec(tr, D), out_shape=jax.ShapeDtypeStruct((T, D), BF16),
                  compiler_params=_params(("parallel",)))(x, g)


def rms_bwd(dh, x, g, res, *, name):
    T, D = x.shape
    tr = _tile(T, 256, 8)

    def body(dh_ref, x_ref, g_ref, res_ref, dx_ref, dg_ref):
        i = pl.program_id(0)
        xv, dhv = x_ref[...], dh_ref[...].astype(F32)
        r = lax.rsqrt(jnp.mean(xv * xv, axis=-1, keepdims=True) + RMS_EPS)
        xh = xv * r
        dxh = dhv * g_ref[...]
        dx_ref[...] = res_ref[...] + r * (dxh - xh * jnp.mean(dxh * xh, axis=-1, keepdims=True))
        _acc_rows(dg_ref, jnp.sum(dhv * xh, axis=0, keepdims=True), i)

    return _pcall(body, name=name, grid=(T // tr,),
                  in_specs=[_row_spec(tr, D), _row_spec(tr, D), _full_spec((1, D)), _row_spec(tr, D)],
                  out_specs=[_row_spec(tr, D), _full_spec((1, D))],
                  out_shape=[jax.ShapeDtypeStruct((T, D), F32), jax.ShapeDtypeStruct((1, D), F32)],
                  compiler_params=_params(("arbitrary",)))(dh, x, g, res)


def final_loss(x, tgt, g, *, name):
    T, D = x.shape
    tr = _tile(T, 256, 8)

    def body(x_ref, t_ref, g_ref, dx_ref, dg_ref, loss_ref):
        i = pl.program_id(0)
        xv = x_ref[...]
        r = lax.rsqrt(jnp.mean(xv * xv, axis=-1, keepdims=True) + RMS_EPS)
        xh = xv * r
        e = xh * g_ref[...] - t_ref[...]
        part = 0.5 * jnp.sum(jnp.mean(e * e, axis=-1, keepdims=True), axis=0, keepdims=True)
        dy = e * (1.0 / D)
        dxh = dy * g_ref[...]
        dx_ref[...] = r * (dxh - xh * jnp.mean(dxh * xh, axis=-1, keepdims=True))
        _acc_rows(dg_ref, jnp.sum(dy * xh, axis=0, keepdims=True), i)
        _acc_rows(loss_ref, jnp.broadcast_to(part, (1, LANE)), i)

    return _pcall(body, name=name, grid=(T // tr,),
                  in_specs=[_row_spec(tr, D), _row_spec(tr, D), _full_spec((1, D))],
                  out_specs=[_row_spec(tr, D), _full_spec((1, D)), _full_spec((1, LANE))],
                  out_shape=[jax.ShapeDtypeStruct((T, D), F32), jax.ShapeDtypeStruct((1, D), F32),
                             jax.ShapeDtypeStruct((1, LANE), F32)],
                  compiler_params=_params(("arbitrary",)))(x, tgt, g)


def swiglu_fwd(gate, up, *, name):
    T, F = gate.shape
    tr = _tile(T, 256, 8)

    def body(g_ref, u_ref, a_ref):
        gv = g_ref[...].astype(F32)
        a_ref[...] = (gv * _sigmoid(gv) * u_ref[...].astype(F32)).astype(BF16)

    return _pcall(body, name=name, grid=(T // tr,), in_specs=[_row_spec(tr, F), _row_spec(tr, F)],
                  out_specs=_row_spec(tr, F), out_shape=jax.ShapeDtypeStruct((T, F), BF16),
                  compiler_params=_params(("parallel",)))(gate, up)


def swiglu_bwd(da, gate, up, *, name):
    T, F = gate.shape
    tr = _tile(T, 256, 8)

    def body(da_ref, g_ref, u_ref, dg_ref, du_ref):
        gv, dav = g_ref[...].astype(F32), da_ref[...].astype(F32)
        s = _sigmoid(gv)
        du_ref[...] = (dav * gv * s).astype(BF16)
        dg_ref[...] = (dav * u_ref[...].astype(F32) * s * (1.0 + gv * (1.0 - s))).astype(BF16)

    return _pcall(body, name=name, grid=(T // tr,), in_specs=[_row_spec(tr, F)] * 3,
                  out_specs=[_row_spec(tr, F)] * 2, out_shape=[jax.ShapeDtypeStruct((T, F), BF16)] * 2,
                  compiler_params=_params(("parallel",)))(da, gate, up)


def merge_fwd(br, ba, pg, *, name):
    T, D = br.shape
    tr = _tile(T, 256, 8)

    def body(br_ref, ba_ref, pg_ref, o_ref):
        pgv = pg_ref[...]
        o_ref[...] = (_sigmoid(pgv[:, :D]) * br_ref[...] + _sigmoid(pgv[:, D:]) * ba_ref[...]).astype(BF16)

    return _pcall(body, name=name, grid=(T // tr,), in_specs=[_row_spec(tr, D), _row_spec(tr, D), _row_spec(tr, 2 * D)],
                  out_specs=_row_spec(tr, D), out_shape=jax.ShapeDtypeStruct((T, D), BF16),
                  compiler_params=_params(("parallel",)))(br, ba, pg)


def merge_bwd(dm, br, ba, pg, *, name):
    T, D = br.shape
    tr = _tile(T, 256, 8)

    def body(dm_ref, br_ref, ba_ref, pg_ref, dbr_ref, dba_ref, dpg_ref):
        pgv, dmv = pg_ref[...], dm_ref[...]
        sr, sa = _sigmoid(pgv[:, :D]), _sigmoid(pgv[:, D:])
        dbr_ref[...] = (dmv * sr).astype(BF16)
        dba_ref[...] = (dmv * sa).astype(BF16)
        dpg_ref[:, :D] = dmv * br_ref[...] * sr * (1.0 - sr)
        dpg_ref[:, D:] = dmv * ba_ref[...] * sa * (1.0 - sa)

    return _pcall(body, name=name, grid=(T // tr,),
                  in_specs=[_row_spec(tr, D), _row_spec(tr, D), _row_spec(tr, D), _row_spec(tr, 2 * D)],
                  out_specs=[_row_spec(tr, D), _row_spec(tr, D), _row_spec(tr, 2 * D)],
                  out_shape=[jax.ShapeDtypeStruct((T, D), BF16), jax.ShapeDtypeStruct((T, D), BF16),
                             jax.ShapeDtypeStruct((T, 2 * D), F32)],
                  compiler_params=_params(("parallel",)))(dm, br, ba, pg)


def _rwkv_mix(p, prev, mu, w0, a0, k_k, k_a, wlw, wla, wlg, bd):
    pp = p + (prev - p) * mu
    r, k, v = pp[:, 0:RW], pp[:, RW:2 * RW], pp[:, 2 * RW:3 * RW]
    xw, xa, xg = pp[:, 3 * RW:3 * RW + 128], pp[:, 3 * RW + 128:3 * RW + 256], pp[:, 3 * RW + 256:3 * RW + 384]
    th = jnp.tanh(xw)
    z = -(w0 + _bdot(th, wlw))
    e = jnp.exp(-_softplus(z) - 0.5)
    decay = jnp.exp(-e)
    a = _sigmoid(a0 + _bdot(xa, wla))
    sg = _sigmoid(xg)
    kkr = k * k_k
    n = jnp.sqrt(_segsum(kkr * kkr, bd))
    kk = kkr / jnp.maximum(n, 1e-12)
    k2 = k * (1.0 + (a - 1.0) * k_a)
    return dict(r=r, k=k, v=v, xa=xa, th=th, z=z, e=e, decay=decay, a=a, sg=sg, n=n, kk=kk, k2=k2)


def _seg_matrix(n, shift):
    r = lax.shift_right_logical(lax.broadcasted_iota(jnp.int32, (n, n), 0), shift)
    c = lax.shift_right_logical(lax.broadcasted_iota(jnp.int32, (n, n), 1), shift)
    return jnp.where(r == c, 1.0, 0.0).astype(BF16)


def rwkv_pre_fwd(p, pshift, mu, w0, a0, k_k, k_a, wlw, wla, wlg, *, name):
    T = p.shape[0]
    tr = _tile(T, 256, 8)

    def body(p_ref, ps_ref, mu_ref, w0_ref, a0_ref, kk_ref, ka_ref, wlw_ref, wla_ref, wlg_ref,
             r_ref, w_ref, k_ref, v_ref, a_ref, b_ref, g_ref):
        pv, prev = p_ref[...], ps_ref[...]
        m = _rwkv_mix(pv, prev, mu_ref[...], w0_ref[...], a0_ref[...], kk_ref[...], ka_ref[...],
                      wlw_ref[...], wla_ref[...], wlg_ref[...], _seg_matrix(RW, 6))
        r_ref[...] = m['r']
        w_ref[...] = m['decay']
        k_ref[...] = m['k2']
        v_ref[...] = m['v']
        a_ref[...] = -m['kk']
        b_ref[...] = m['kk'] * m['a']
        g_ref[...] = m['sg']

    vec = _row_spec(tr, RW)
    return _pcall(
        body, name=name, grid=(T // tr,),
        in_specs=[_row_spec(tr, RWKV_PAD), _row_spec(tr, RWKV_PAD), _full_spec((1, RWKV_PAD))] + [_full_spec((1, RW))] * 4
        + [_full_spec((128, RW))] * 3,
        out_specs=[vec] * 6 + [_row_spec(tr, 128)],
        out_shape=[jax.ShapeDtypeStruct((T, RW), F32)] * 6 + [jax.ShapeDtypeStruct((T, 128), F32)],
        compiler_params=_params(("parallel",)),
    )(p, pshift, mu, w0, a0, k_k, k_a, wlw, wla, wlg)


def _group_norm(y, bd):
    mean = _segsum(y, bd) * (1.0 / HEAD_DIM)
    yc = y - mean
    rstd = lax.rsqrt(_segsum(yc * yc, bd) * (1.0 / HEAD_DIM) + GN_EPS)
    return yc * rstd, rstd


def rwkv_post_fwd(y, r, k2, v, sg, wlg, ln_w, ln_b, r_k, *, name):
    T = y.shape[0]
    tr = _tile(T, 256, 8)

    def body(y_ref, r_ref, k_ref, v_ref, sg_ref, wlg_ref, lw_ref, lb_ref, rk_ref, o_ref):
        bd = _seg_matrix(RW, 6)
        yn, _ = _group_norm(y_ref[...], bd)
        s = _segsum(r_ref[...] * k_ref[...] * rk_ref[...], bd)
        g = _bdot(sg_ref[...], wlg_ref[...])
        o_ref[...] = ((yn * lw_ref[...] + lb_ref[...] + s * v_ref[...]) * g).astype(BF16)

    vec = _row_spec(tr, RW)
    return _pcall(body, name=name, grid=(T // tr,),
                  in_specs=[vec] * 4 + [_row_spec(tr, 128), _full_spec((128, RW))] + [_full_spec((1, RW))] * 3, out_specs=vec,
                  out_shape=jax.ShapeDtypeStruct((T, RW), BF16), compiler_params=_params(("parallel",)))(
                      y, r, k2, v, sg, wlg, ln_w, ln_b, r_k)


def rwkv_post_bwd(dyr, y, r, k2, v, sg, wlg, ln_w, ln_b, r_k, *, name):
    T = y.shape[0]
    tr = _tile(T, 256, 8)

    def body(dyr_ref, y_ref, r_ref, k_ref, v_ref, sg_ref, wlg_ref, lw_ref, lb_ref, rk_ref,
             dy_ref, dz_ref, dg_ref, dlw_ref, dlb_ref):
        i = pl.program_id(0)
        bd = _seg_matrix(RW, 6)
        yn, rstd = _group_norm(y_ref[...], bd)
        s = _segsum(r_ref[...] * k_ref[...] * rk_ref[...], bd)
        dyrv = dyr_ref[...]
        dg_ref[...] = dyrv * (yn * lw_ref[...] + lb_ref[...] + s * v_ref[...])
        dz = dyrv * _bdot(sg_ref[...], wlg_ref[...])
        dz_ref[...] = dz
        dyn = dz * lw_ref[...]
        inv = 1.0 / HEAD_DIM
        dy_ref[...] = rstd * (dyn - _segsum(dyn, bd) * inv - yn * (_segsum(dyn * yn, bd) * inv))
        _acc_rows(dlw_ref, jnp.sum(dz * yn, axis=0, keepdims=True), i)
        _acc_rows(dlb_ref, jnp.sum(dz, axis=0, keepdims=True), i)

    vec = _row_spec(tr, RW)
    one = _full_spec((1, RW))
    return _pcall(body, name=name, grid=(T // tr,),
                  in_specs=[vec] * 5 + [_row_spec(tr, 128), _full_spec((128, RW))] + [one] * 3, out_specs=[vec] * 3 + [one] * 2,
                  out_shape=[jax.ShapeDtypeStruct((T, RW), F32)] * 3 + [jax.ShapeDtypeStruct((1, RW), F32)] * 2,
                  compiler_params=_params(("arbitrary",)))(dyr, y, r, k2, v, sg, wlg, ln_w, ln_b, r_k)


def rwkv_pre_bwd(p, pshift, dr_w, dw_w, dk_w, dv_w, da_w, db_w, dz, dg, mu, w0, a0, k_k, k_a, r_k, wlw, wla, wlg, *, name):
    T = p.shape[0]
    tr = _tile(T, 256, 8)
    n = T // tr

    def body(p_ref, ps_ref, dr_ref, dw_ref, dk_ref, dv_ref, da_ref, db_ref, dz_ref, dg_ref,
             mu_ref, w0_ref, a0_ref, kk_ref, ka_ref, rk_ref, wlw_ref, wla_ref, wlg_ref,
             dp_ref, dmu_ref, dw0_ref, da0_ref, dkk_ref, dka_ref, drk_ref, dwlw_ref, dwla_ref, dwlg_ref,
             carry, dpp, acc_w, acc_a, acc_g):
        i = pl.program_id(0)

        @pl.when(i == 0)
        def _():
            carry[...] = jnp.zeros_like(carry)

        pv, prev, mu = p_ref[...], ps_ref[...], mu_ref[...]
        bd = _seg_matrix(RW, 6)
        k_k, k_a, r_k = kk_ref[...], ka_ref[...], rk_ref[...]
        m = _rwkv_mix(pv, prev, mu, w0_ref[...], a0_ref[...], k_k, k_a, wlw_ref[...], wla_ref[...], wlg_ref[...], bd)
        r, k, v, a, kk, k2 = m['r'], m['k'], m['v'], m['a'], m['kk'], m['k2']
        dzv, dgv = dz_ref[...], dg_ref[...]
        s = _segsum(r * k2 * r_k, bd)
        ds = _segsum(dzv * v, bd)
        dr = dr_ref[...] + ds * k2 * r_k
        dk2 = dk_ref[...] + ds * r * r_k
        dv = dv_ref[...] + dzv * s
        dbv = db_ref[...]
        dkk = dbv * a - da_ref[...]
        da = dbv * kk + dk2 * k * k_a
        dk = dk2 * (1.0 + (a - 1.0) * k_a)
        nmax = jnp.maximum(m['n'], 1e-12)
        dkkr = jnp.where(m['n'] > 1e-12, dkk - kk * _segsum(dkk * kk, bd), dkk) / nmax
        dk = dk + dkkr * k_k
        dapre = da * a * (1.0 - a)
        dwpre = dw_ref[...] * m['decay'] * (-m['e']) * _sigmoid(m['z'])
        dth = _bdot(dwpre, wlw_ref[...], _NT)
        dxa = _bdot(dapre, wla_ref[...], _NT)
        dsg = _bdot(dgv, wlg_ref[...], _NT)
        dpp[:, 0:RW] = dr
        dpp[:, RW:2 * RW] = dk
        dpp[:, 2 * RW:3 * RW] = dv
        dpp[:, 3 * RW:3 * RW + 128] = dth * (1.0 - m['th'] * m['th'])
        dpp[:, 3 * RW + 128:3 * RW + 256] = dxa
        dpp[:, 3 * RW + 256:3 * RW + 384] = dsg * m['sg'] * (1.0 - m['sg'])
        d = dpp[...]
        zed = d * mu
        last = lax.broadcasted_iota(jnp.int32, pv.shape, 0) == tr - 1
        dp_ref[...] = d * (1.0 - mu) + jnp.where(last, carry[0:1, :], pltpu.roll(zed, tr - 1, 0))
        carry[...] = zed[0:8, :]

        def colsum(x):
            return jnp.sum(x, axis=0, keepdims=True)

        _acc_rows(dmu_ref, colsum(d * (prev - pv)), i)
        _acc_rows(dw0_ref, colsum(dwpre), i)
        _acc_rows(da0_ref, colsum(dapre), i)
        _acc_rows(dkk_ref, colsum(dkkr * k), i)
        _acc_rows(dka_ref, colsum(dk2 * k * (a - 1.0)), i)
        _acc_rows(drk_ref, colsum(ds * r * k2), i)
        _acc_rows(acc_w, _bdot(m['th'], dwpre, _TN), i)
        _acc_rows(acc_a, _bdot(m['xa'], dapre, _TN), i)
        _acc_rows(acc_g, _bdot(m['sg'], dgv, _TN), i)

        @pl.when(i == n - 1)
        def _():
            dwlw_ref[...] = acc_w[...]
            dwla_ref[...] = acc_a[...]
            dwlg_ref[...] = acc_g[...]

    rev = lambda c: pl.BlockSpec((tr, c), lambda i: (n - 1 - i, 0))
    one, lora = _full_spec((1, RW)), _full_spec((128, RW))
    return _pcall(
        body, name=name, grid=(n,),
        in_specs=[rev(RWKV_PAD), rev(RWKV_PAD)] + [rev(RW)] * 8 + [_full_spec((1, RWKV_PAD))] + [one] * 5 + [lora] * 3,
        out_specs=[rev(RWKV_PAD), _full_spec((1, RWKV_PAD))] + [one] * 5 + [lora] * 3,
        out_shape=[jax.ShapeDtypeStruct((T, RWKV_PAD), F32), jax.ShapeDtypeStruct((1, RWKV_PAD), F32)]
        + [jax.ShapeDtypeStruct((1, RW), F32)] * 5 + [jax.ShapeDtypeStruct((128, RW), F32)] * 3,
        scratch_shapes=[pltpu.VMEM((8, RWKV_PAD), F32), pltpu.VMEM((tr, RWKV_PAD), F32)] + [pltpu.VMEM((128, RW), F32)] * 3,
        compiler_params=_params(("arbitrary",)),
    )(p, pshift, dr_w, dw_w, dk_w, dv_w, da_w, db_w, dz, dg, mu, w0, a0, k_k, k_a, r_k, wlw, wla, wlg)


def _qk_norm(x, g, bd):
    r = lax.rsqrt(_segsum(x * x, bd) * (1.0 / HEAD_DIM) + RMS_EPS)
    return x * r * g, r


def _att_mask(i):
    qi = lax.broadcasted_iota(jnp.int32, (BLOCK, 2 * BLOCK), 0)
    kj = lax.broadcasted_iota(jnp.int32, (BLOCK, 2 * BLOCK), 1)
    band = (kj <= qi + BLOCK) & (kj > qi + BLOCK - WINDOW)
    return band & ((kj >= BLOCK) | (i > 0))


def _att_probs(qh, kh, mask, sink):
    s = _bdot(qh, kh, _NT) * (HEAD_DIM ** -0.5)
    s = jnp.where(mask, s, NEG_BIG)
    m = jnp.maximum(jnp.max(s, axis=-1, keepdims=True), sink)
    pexp = jnp.exp(s - m)
    psink = jnp.exp(sink - m)
    inv = 1.0 / (jnp.sum(pexp, axis=-1, keepdims=True) + psink)
    return pexp * inv, psink * inv


def _att_blocks(n):
    cur = pl.BlockSpec((BLOCK, ATT_COLS), lambda i: (i, 0))
    prev = pl.BlockSpec((BLOCK, ATT_COLS), lambda i: (jnp.maximum(i - 1, 0), 0))
    return cur, prev


def _att_qkv(cur, prev, qn_g, kn_g):
    bq, bk = _seg_matrix(RW, 6), _seg_matrix(KVW, 6)
    qn, rq = _qk_norm(cur[:, 0:RW], qn_g, bq)
    kcur, rkc = _qk_norm(cur[:, RW:RW + KVW], kn_g, bk)
    kprev, _ = _qk_norm(prev[:, RW:RW + KVW], kn_g, bk)
    kc = jnp.concatenate([kprev, kcur], axis=0)
    vc = jnp.concatenate([prev[:, RW + KVW:], cur[:, RW + KVW:]], axis=0)
    return qn, rq, kc, vc, rkc


def att_fwd(pa, qn_g, kn_g, sinks, *, name):
    T = pa.shape[0]
    n = T // BLOCK

    def body(cur_ref, prev_ref, qg_ref, kg_ref, sk_ref, o_ref):
        i = pl.program_id(0)
        qn, _, kc, vc, _ = _att_qkv(cur_ref[...], prev_ref[...], qg_ref[...], kg_ref[...])
        mask = _att_mask(i)
        outs = []
        for h in range(N_HEADS):
            kv = slice((h // ATT_GROUP) * HEAD_DIM, (h // ATT_GROUP + 1) * HEAD_DIM)
            probs, _ = _att_probs(qn[:, h * HEAD_DIM:(h + 1) * HEAD_DIM], kc[:, kv], mask, sk_ref[0:1, h:h + 1])
            outs.append(_bdot(probs, vc[:, kv]))
        o_ref[...] = jnp.concatenate(outs, axis=1)

    cur, prev = _att_blocks(n)
    return _pcall(body, name=name, grid=(n,),
                  in_specs=[cur, prev, _full_spec((1, RW)), _full_spec((1, KVW)), _full_spec((1, LANE))],
                  out_specs=pl.BlockSpec((BLOCK, RW), lambda i: (i, 0)), out_shape=jax.ShapeDtypeStruct((T, RW), F32),
                  compiler_params=_params(("parallel",)))(pa, pa, qn_g, kn_g, sinks)


def att_bwd(pa, do, qn_g, kn_g, sinks, *, name):
    T = pa.shape[0]
    n = T // BLOCK

    def body(cur_ref, prev_ref, do_ref, qg_ref, kg_ref, sk_ref,
             dq_ref, dko_ref, dkn_ref, dvo_ref, dvn_ref, dqg_ref, dsk_ref):
        i = pl.program_id(0)
        cur = cur_ref[...]
        qn, rq, kc, vc, _ = _att_qkv(cur, prev_ref[...], qg_ref[...], kg_ref[...])
        mask = _att_mask(i)
        dov = do_ref[...]
        lane = lax.broadcasted_iota(jnp.int32, (1, LANE), 1)
        dsink = jnp.zeros((1, LANE), F32)
        dqn, dkc, dvc = [], [None, None], [None, None]
        for h in range(N_HEADS):
            g = h // ATT_GROUP
            kv = slice(g * HEAD_DIM, (g + 1) * HEAD_DIM)
            qh, doh = qn[:, h * HEAD_DIM:(h + 1) * HEAD_DIM], dov[:, h * HEAD_DIM:(h + 1) * HEAD_DIM]
            probs, psink = _att_probs(qh, kc[:, kv], mask, sk_ref[0:1, h:h + 1])
            dprobs = _bdot(doh, vc[:, kv], _NT)
            delta = jnp.sum(probs * dprobs, axis=-1, keepdims=True)
            ds = probs * (dprobs - delta) * (HEAD_DIM ** -0.5)
            dsink = dsink + jnp.where(lane == h, -jnp.sum(psink * delta, axis=0, keepdims=True), 0.0)
            dqn.append(_bdot(ds, kc[:, kv]))
            dk_h, dv_h = _bdot(ds, qh, _TN), _bdot(probs, doh, _TN)
            dkc[g] = dk_h if dkc[g] is None else dkc[g] + dk_h
            dvc[g] = dv_h if dvc[g] is None else dvc[g] + dv_h
        dqn = jnp.concatenate(dqn, axis=1)
        dk = jnp.concatenate(dkc, axis=1)
        dv = jnp.concatenate(dvc, axis=1)
        dkn_ref[...], dko_ref[...] = dk[0:BLOCK], dk[BLOCK:]
        dvn_ref[...], dvo_ref[...] = dv[0:BLOCK], dv[BLOCK:]
        qhat = cur[:, 0:RW] * rq
        dqh = dqn * qg_ref[...]
        dq_ref[...] = rq * (dqh - qhat * (_segsum(dqh * qhat, _seg_matrix(RW, 6)) * (1.0 / HEAD_DIM)))
        prod = dqn * qhat
        fold = prod[:, 0:HEAD_DIM]
        for h in range(1, N_HEADS):
            fold = fold + prod[:, h * HEAD_DIM:(h + 1) * HEAD_DIM]
        _acc_rows(dqg_ref, jnp.sum(fold, axis=0, keepdims=True), i)
        _acc_rows(dsk_ref, dsink, i)

    cur, prev = _att_blocks(n)
    kvb = pl.BlockSpec((BLOCK, KVW), lambda i: (i, 0))
    qb = pl.BlockSpec((BLOCK, RW), lambda i: (i, 0))
    return _pcall(body, name=name, grid=(n,),
                  in_specs=[cur, prev, qb, _full_spec((1, RW)), _full_spec((1, KVW)), _full_spec((1, LANE))],
                  out_specs=[qb, kvb, kvb, kvb, kvb, _full_spec((1, HEAD_DIM)), _full_spec((1, LANE))],
                  out_shape=[jax.ShapeDtypeStruct((T, RW), F32)] + [jax.ShapeDtypeStruct((T, KVW), F32)] * 4
                  + [jax.ShapeDtypeStruct((1, HEAD_DIM), F32), jax.ShapeDtypeStruct((1, LANE), F32)],
                  compiler_params=_params(("arbitrary",)))(pa, pa, do, qn_g, kn_g, sinks)


def att_kv_bwd(pa, dq, dko, dkn, dvo, dvn, kn_g, *, name):
    T = pa.shape[0]
    n = T // BLOCK

    def body(pa_ref, dq_ref, dko_ref, dkn_ref, dvo_ref, dvn_ref, kg_ref, dpa_ref, dkg_ref):
        i = pl.program_id(0)
        more = i < n - 1
        dkn_tot = dko_ref[...] + jnp.where(more, dkn_ref[...], 0.0)
        dv_tot = dvo_ref[...] + jnp.where(more, dvn_ref[...], 0.0)
        kraw = pa_ref[:, RW:RW + KVW]
        bk = _seg_matrix(KVW, 6)
        _, rk = _qk_norm(kraw, kg_ref[...], bk)
        khat = kraw * rk
        dkh = dkn_tot * kg_ref[...]
        dpa_ref[:, 0:RW] = dq_ref[...]
        dpa_ref[:, RW:RW + KVW] = rk * (dkh - khat * (_segsum(dkh * khat, bk) * (1.0 / HEAD_DIM)))
        dpa_ref[:, RW + KVW:] = dv_tot
        prod = dkn_tot * khat
        _acc_rows(dkg_ref, jnp.sum(prod[:, 0:HEAD_DIM] + prod[:, HEAD_DIM:], axis=0, keepdims=True), i)

    kvb = pl.BlockSpec((BLOCK, KVW), lambda i: (i, 0))
    nxt = pl.BlockSpec((BLOCK, KVW), lambda i: (jnp.minimum(i + 1, n - 1), 0))
    return _pcall(body, name=name, grid=(n,),
                  in_specs=[pl.BlockSpec((BLOCK, ATT_COLS), lambda i: (i, 0)), pl.BlockSpec((BLOCK, RW), lambda i: (i, 0)),
                            kvb, nxt, kvb, nxt, _full_spec((1, KVW))],
                  out_specs=[pl.BlockSpec((BLOCK, ATT_COLS), lambda i: (i, 0)), _full_spec((1, HEAD_DIM))],
                  out_shape=[jax.ShapeDtypeStruct((T, ATT_COLS), F32), jax.ShapeDtypeStruct((1, HEAD_DIM), F32)],
                  compiler_params=_params(("arbitrary",)))(pa, dq, dko, dkn, dvo, dvn, kn_g)


WKV_CHUNK = 64
WKV_GROUP = 8


def _diag_mask():
    i = lax.broadcasted_iota(jnp.int32, (HEAD_DIM, RW), 0)
    j = lax.broadcasted_iota(jnp.int32, (HEAD_DIM, RW), 1) & (HEAD_DIM - 1)
    return i == j


def _headsums(xs, pieces):
    half = RW // 2
    bd = _seg_matrix(half, 6)
    bd = jnp.concatenate([bd] * pieces, axis=0)
    rows = []
    for x in xs:
        parts, rest = [], x
        for n in range(pieces):
            p = rest.astype(BF16)
            parts.append(p)
            if n + 1 < pieces:
                rest = rest - p.astype(F32)
        for sl in (slice(0, half), slice(half, RW)):
            rows.append(jnp.concatenate([p[:, sl] for p in parts], axis=1))
    out = lax.dot_general(jnp.concatenate(rows, axis=0), bd, (((1,), (0,)), ((), ())), preferred_element_type=F32)
    return [jnp.concatenate([out[2 * n * HEAD_DIM:(2 * n + 1) * HEAD_DIM], out[(2 * n + 1) * HEAD_DIM:(2 * n + 2) * HEAD_DIM]],
                            axis=1) for n in range(len(xs))]


def _headsum(x):
    low = lax.broadcasted_iota(jnp.int32, (HEAD_DIM, LANE), 1) < HEAD_DIM
    tiles = []
    for c in range(RW // LANE):
        xt = x[:, c * LANE:(c + 1) * LANE]
        s_lo = jnp.sum(jnp.where(low, xt, 0.0), axis=1, keepdims=True)
        s_hi = jnp.sum(jnp.where(low, 0.0, xt), axis=1, keepdims=True)
        tiles.append(jnp.where(low, s_lo, s_hi))
    return jnp.concatenate(tiles, axis=1)


def _cols(rows, diag):
    return _headsums([jnp.where(diag, r, 0.0) for r in rows], 2)


def _row(x, diag):
    return jnp.sum(jnp.where(diag, x, 0.0), axis=0, keepdims=True)


def wkv_fwd(r, w, k, v, a, b, *, name):
    T = r.shape[0]
    ch = min(WKV_CHUNK, T)
    ngroups = ch // WKV_GROUP

    def body(r_ref, w_ref, k_ref, v_ref, a_ref, b_ref, y_ref, st_ref, s_scr):
        @pl.when(pl.program_id(0) == 0)
        def _():
            s_scr[...] = jnp.zeros_like(s_scr)

        diag = _diag_mask()

        def group(gi, S):
            t0 = pl.multiple_of(gi * WKV_GROUP, WKV_GROUP)
            rows = pl.ds(t0, WKV_GROUP)
            R, W, K, V, A, B = (ref[rows, :] for ref in (r_ref, w_ref, k_ref, v_ref, a_ref, b_ref))
            vcols = _cols([V[s:s + 1] for s in range(WKV_GROUP)], diag)
            yrows = []
            for s in range(WKV_GROUP):
                sa = _headsum(S * A[s:s + 1])
                S = S * W[s:s + 1] + sa * B[s:s + 1] + vcols[s] * K[s:s + 1]
                st_ref[t0 + s] = S
                yrows.append(_row(_headsum(S * R[s:s + 1]), diag))
            y_ref[rows, :] = jnp.concatenate(yrows, axis=0)
            return S

        s_scr[...] = lax.fori_loop(0, ngroups, group, s_scr[...])

    vec = pl.BlockSpec((ch, RW), lambda c: (c, 0))
    return _pcall(
        body, name=name, grid=(T // ch,), in_specs=[vec] * 6,
        out_specs=[vec, pl.BlockSpec((ch, HEAD_DIM, RW), lambda c: (c, 0, 0))],
        out_shape=[jax.ShapeDtypeStruct((T, RW), F32), jax.ShapeDtypeStruct((T, HEAD_DIM, RW), F32)],
        scratch_shapes=[pltpu.VMEM((HEAD_DIM, RW), F32)],
        compiler_params=_params(("arbitrary",)),
    )(r, w, k, v, a, b)


def wkv_bwd(r, w, k, v, a, b, dy, states, *, name):
    T = r.shape[0]
    ch = min(WKV_CHUNK, T)
    nchunks = T // ch
    ngroups = ch // WKV_GROUP

    def body(r_ref, w_ref, k_ref, v_ref, a_ref, b_ref, dy_ref, st_ref, stp_ref,
             dr_ref, dw_ref, dk_ref, dv_ref, da_ref, db_ref, ds_scr):
        step = pl.program_id(0)

        @pl.when(step == 0)
        def _():
            ds_scr[...] = jnp.zeros_like(ds_scr)

        has_prev_chunk = step < nchunks - 1
        diag = _diag_mask()
        colsum = lambda x: jnp.sum(x, axis=0, keepdims=True)

        def group(gj, dS):
            gi = ngroups - 1 - gj
            t0 = pl.multiple_of(gi * WKV_GROUP, WKV_GROUP)
            rows = pl.ds(t0, WKV_GROUP)
            R, W, K, V, A, B, DY = (ref[rows, :] for ref in (r_ref, w_ref, k_ref, v_ref, a_ref, b_ref, dy_ref))
            before = jnp.where(gi > 0, st_ref[jnp.maximum(t0 - 1, 0)], jnp.where(has_prev_chunk, stp_ref[0], 0.0))
            prev_state = lambda s: st_ref[t0 + s - 1] if s > 0 else before
            steps = range(WKV_GROUP)
            dycols = _cols([DY[s:s + 1] for s in steps], diag)
            vcols = _cols([V[s:s + 1] for s in steps], diag)
            sas = _headsums([prev_state(s) * A[s:s + 1] for s in steps], 1)
            got = [[None] * WKV_GROUP for _ in range(6)]
            for s in reversed(steps):
                Sp = prev_state(s)
                dS = dS + dycols[s] * R[s:s + 1]
                got[0][s] = colsum(st_ref[t0 + s] * dycols[s])
                got[3][s] = _row(_headsum(dS * K[s:s + 1]), diag)
                got[2][s] = colsum(dS * vcols[s])
                dsa = _headsum(dS * B[s:s + 1])
                got[5][s] = colsum(dS * sas[s])
                got[1][s] = colsum(dS * Sp)
                got[4][s] = colsum(Sp * dsa)
                dS = dS * W[s:s + 1] + dsa * A[s:s + 1]
            for q, ref in enumerate((dr_ref, dw_ref, dk_ref, dv_ref, da_ref, db_ref)):
                ref[rows, :] = jnp.concatenate(got[q], axis=0)
            return dS

        ds_scr[...] = lax.fori_loop(0, ngroups, group, ds_scr[...])

    vec = pl.BlockSpec((ch, RW), lambda c: (nchunks - 1 - c, 0))
    st_spec = pl.BlockSpec((ch, HEAD_DIM, RW), lambda c: (nchunks - 1 - c, 0, 0))
    stp_spec = pl.BlockSpec((1, HEAD_DIM, RW), lambda c: (jnp.maximum((nchunks - 1 - c) * ch - 1, 0), 0, 0))
    return _pcall(
        body, name=name, grid=(nchunks,), in_specs=[vec] * 7 + [st_spec, stp_spec], out_specs=[vec] * 6,
        out_shape=[jax.ShapeDtypeStruct((T, RW), F32)] * 6,
        scratch_shapes=[pltpu.VMEM((HEAD_DIM, RW), F32)],
        compiler_params=_params(("arbitrary",)),
    )(r, w, k, v, a, b, dy, states, states)


_HBM = pl.BlockSpec(memory_space=pltpu.HBM)
_MESH = pl.DeviceIdType.MESH


def _place():
    x, y, c = lax.axis_index("x"), lax.axis_index("y"), lax.axis_index("c")
    return x, y, c, [(1 - x, y), (x, 1 - y), (1 - x, 1 - y)]


def _remote(src, dst, send_sem, recv_sem, to):
    return pltpu.make_async_remote_copy(src_ref=src, dst_ref=dst, send_sem=send_sem, recv_sem=recv_sem, device_id=to,
                                        device_id_type=_MESH)


def _dma_sems(*counts):
    return [pltpu.SemaphoreType.DMA((n,)) for n in counts]


def gather_weights(shards, *, name):
    n = len(shards)

    def body(*refs):
        ins, outs = refs[:n], refs[n:2 * n]
        ici_send, ici_recv, d2d_send, d2d_recv, local_sems = refs[2 * n:]
        x, y, c, chips = _place()
        me, sibling = 2 * x + y, (x, y, 1 - c)

        def half(i, which):
            rh = ins[i].shape[0] // 2
            return pl.ds(which * rh, rh)

        local = [pltpu.make_async_copy(ins[i], outs[i].at[me], local_sems.at[i]) for i in range(n)]
        for cp in local:
            cp.start()
        sends = []
        for i in range(n):
            for j, (qx, qy) in enumerate(chips):
                k = 3 * i + j
                sends.append(_remote(ins[i].at[half(i, c)], outs[i].at[me, half(i, c)], ici_send.at[k], ici_recv.at[k], (qx, qy, c)))
        for cp in sends:
            cp.start()
        passed = []
        for i in range(n):
            for j, (qx, qy) in enumerate(chips):
                k = 3 * i + j
                landed = outs[i].at[2 * qx + qy, half(i, c)]
                _remote(landed, landed, ici_send.at[k], ici_recv.at[k], (qx, qy, c)).wait_recv()
                cp = _remote(landed, landed, d2d_send.at[k], d2d_recv.at[k], sibling)
                cp.start()
                passed.append(cp)
        for i in range(n):
            for j, (qx, qy) in enumerate(chips):
                k = 3 * i + j
                theirs = outs[i].at[2 * qx + qy, half(i, 1 - c)]
                _remote(theirs, theirs, d2d_send.at[k], d2d_recv.at[k], sibling).wait_recv()
        for cp in sends + passed:
            cp.wait_send()
        for cp in local:
            cp.wait()

    return _pcall(body, name=name, in_specs=[_HBM] * n, out_specs=[_HBM] * n,
                  out_shape=[jax.ShapeDtypeStruct((N_CHIPS,) + s.shape, s.dtype) for s in shards],
                  scratch_shapes=_dma_sems(3 * n, 3 * n, 3 * n, 3 * n, n), compiler_params=_params())(*shards)


def to_sibling(arrays, take_other_half, *, name):
    n = len(arrays)

    def body(*refs):
        ins, outs = refs[:n], refs[n:2 * n]
        send_sems, recv_sems = refs[2 * n:]
        x, y, c, _ = _place()
        cps = []
        for i in range(n):
            src = ins[i]
            if take_other_half:
                rh = src.shape[1] // 2
                src = src.at[:, pl.ds((1 - c) * rh, rh)]
            cps.append(_remote(src, outs[i], send_sems.at[i], recv_sems.at[i], (x, y, 1 - c)))
        for cp in cps:
            cp.start()
        for cp in cps:
            cp.wait_recv()
        for cp in cps:
            cp.wait_send()

    def out_of(a):
        shape = (a.shape[0], a.shape[1] // 2, a.shape[2]) if take_other_half else a.shape
        return jax.ShapeDtypeStruct(shape, a.dtype)

    return _pcall(body, name=name, in_specs=[_HBM] * n, out_specs=[_HBM] * n, out_shape=[out_of(a) for a in arrays],
                  scratch_shapes=_dma_sems(n, n), compiler_params=_params())(*arrays)


def exchange_chips(arrays, *, name):
    n = len(arrays)

    def body(*refs):
        ins, outs = refs[:n], refs[n:2 * n]
        send_sems, recv_sems, local_sems = refs[2 * n:]
        x, y, c, chips = _place()
        me = 2 * x + y
        local = [pltpu.make_async_copy(ins[i].at[me], outs[i].at[me], local_sems.at[i]) for i in range(n)]
        for cp in local:
            cp.start()
        sends = []
        for i in range(n):
            for j, (qx, qy) in enumerate(chips):
                k = 3 * i + j
                sends.append(_remote(ins[i].at[2 * qx + qy], outs[i].at[me], send_sems.at[k], recv_sems.at[k], (qx, qy, c)))
        for cp in sends:
            cp.start()
        for i in range(n):
            for j, (qx, qy) in enumerate(chips):
                k = 3 * i + j
                landed = outs[i].at[2 * qx + qy]
                _remote(landed, landed, send_sems.at[k], recv_sems.at[k], (qx, qy, c)).wait_recv()
        for cp in sends:
            cp.wait_send()
        for cp in local:
            cp.wait()

    return _pcall(body, name=name, in_specs=[_HBM] * n, out_specs=[_HBM] * n,
                  out_shape=[jax.ShapeDtypeStruct(a.shape, a.dtype) for a in arrays],
                  scratch_shapes=_dma_sems(3 * n, 3 * n, n), compiler_params=_params())(*arrays)


def _core_index():
    return lax.axis_index("c").astype(jnp.int32).reshape(1)


def pair_sum(g, theirs, wire_dtype, *, name):
    _, R, C = g.shape
    rh = R // 2
    tr = _tile(rh, 256, 16)
    nt = rh // tr

    def body(c_ref, g_ref, t_ref, q_ref, qw_ref):
        q = g_ref[...] + t_ref[...]
        q_ref[...] = q
        qw_ref[...] = q.astype(wire_dtype)

    blk = pl.BlockSpec((1, tr, C), lambda b, i, c_ref: (b, i, 0))
    mine = pl.BlockSpec((1, tr, C), lambda b, i, c_ref: (b, c_ref[0] * nt + i, 0))
    grid_spec = pltpu.PrefetchScalarGridSpec(num_scalar_prefetch=1, grid=(N_CHIPS, nt), in_specs=[mine, blk], out_specs=[blk, blk])
    return _pcall(body, name=name, grid_spec=grid_spec,
                  out_shape=[jax.ShapeDtypeStruct((N_CHIPS, rh, C), F32), jax.ShapeDtypeStruct((N_CHIPS, rh, C), wire_dtype)],
                  compiler_params=_params(("parallel", "parallel")))(_core_index(), g, theirs)


def half_sum(own, landed, *, name):
    _, rh, C = own.shape
    tr = _tile(rh, 256, 16)

    def body(me_ref, own_ref, land_ref, o_ref):
        total = None
        for p in range(N_CHIPS):
            term = jnp.where(me_ref[0] == p, own_ref[p], land_ref[p].astype(F32))
            total = term if total is None else total + term
        o_ref[...] = total

    blk = pl.BlockSpec((N_CHIPS, tr, C), lambda i, me_ref: (0, i, 0))
    grid_spec = pltpu.PrefetchScalarGridSpec(num_scalar_prefetch=1, grid=(rh // tr,), in_specs=[blk, blk],
                                             out_specs=pl.BlockSpec((tr, C), lambda i, me_ref: (i, 0)))
    me = (2 * lax.axis_index("x") + lax.axis_index("y")).astype(jnp.int32).reshape(1)
    return _pcall(body, name=name, grid_spec=grid_spec, out_shape=jax.ShapeDtypeStruct((rh, C), F32),
                  compiler_params=_params(("parallel",)))(me, own, landed)


def adamw(w, m, v, mine, theirs, *, name):
    R, C = w.shape
    rh = R // 2
    tr = _tile(rh, 256, 8)
    nt = rh // tr

    def body(c_ref, w_ref, m_ref, v_ref, a_ref, b_ref, g_ref, d_ref, nm_ref, nv_ref):
        is_mine = (pl.program_id(0) // nt) == c_ref[0]
        g = jnp.where(is_mine, a_ref[...], b_ref[...])
        g_ref[...] = g
        nm = ADAM_B1 * m_ref[...] + (1.0 - ADAM_B1) * g
        nv = ADAM_B2 * v_ref[...] + (1.0 - ADAM_B2) * (g * g)
        nm_ref[...] = nm
        nv_ref[...] = nv
        m_hat = nm / (1.0 - ADAM_B1 ** ADAM_STEP)
        v_hat = nv / (1.0 - ADAM_B2 ** ADAM_STEP)
        d_ref[...] = -ADAM_LR * (m_hat / (jnp.sqrt(v_hat) + ADAM_EPS) + ADAM_WD * w_ref[...])

    full = pl.BlockSpec((tr, C), lambda i, c_ref: (i, 0))
    part = pl.BlockSpec((tr, C), lambda i, c_ref: (i % nt, 0))
    grid_spec = pltpu.PrefetchScalarGridSpec(num_scalar_prefetch=1, grid=(2 * nt,), in_specs=[full] * 3 + [part] * 2,
                                             out_specs=[full] * 4)
    return _pcall(body, name=name, grid_spec=grid_spec, out_shape=[jax.ShapeDtypeStruct((R, C), F32)] * 4,
                  compiler_params=_params(("parallel",)))(_core_index(), w, m, v, mine, theirs)


def _to_blocks(full, axis):
    r, c = full.shape
    if axis == 1:
        return full.reshape(r, N_CHIPS, c // N_CHIPS).transpose(1, 0, 2)
    return full.reshape(N_CHIPS, r // N_CHIPS, c)


def _from_blocks(blocks, axis):
    _, r, c = blocks.shape
    if axis == 1:
        return blocks.transpose(1, 0, 2).reshape(r, N_CHIPS * c)
    return blocks.reshape(N_CHIPS * r, c)


def _ffn_fwd(x, norm, wg, wu, wd, tag):
    h = rms_fwd(x, norm, name=tag + "_norm")
    gate = mm(h, wg, out_dtype=BF16, name=tag + "_gate")
    up = mm(h, wu, out_dtype=BF16, name=tag + "_up")
    act = swiglu_fwd(gate, up, name=tag + "_act")
    out = mm(act, wd, scale=0.5, res=x, name=tag + "_down")
    return out, (h, gate, up, act)


def _ffn_bwd(dout, x, saved, norm, wg, wu, wd, tag):
    h, gate, up, act = saved
    dact = mm(dout, wd, tb=True, scale=0.5, out_dtype=BF16, name=tag + "_dact")
    dwd = mm(act, dout, ta=True, scale=0.5, name=tag + "_dwd")
    dgate, dup = swiglu_bwd(dact, gate, up, name=tag + "_dswiglu")
    dwg = mm(h, dgate, ta=True, name=tag + "_dwg")
    dwu = mm(h, dup, ta=True, name=tag + "_dwu")
    dh = mm(dgate, wg, tb=True, name=tag + "_dh_gate")
    dh = mm(dup, wu, tb=True, res=dh, name=tag + "_dh_up")
    dx, dnorm = rms_bwd(dh, x, norm, dout, name=tag + "_dnorm")
    return dx, dnorm, dwg, dwu, dwd


def _step(A):
    x, tgt = A['x'][0], A['loss_target'][0]
    T = x.shape[0]
    w = {n: A[n][0] for n in WEIGHT_NAMES}
    row = lambda a: a.reshape(1, -1)

    gathered = gather_weights([w[n].astype(BF16) for n, _ in BIG], name="gather_weights")
    full = {n: _from_blocks(b, axis) for (n, axis), b in zip(BIG, gathered)}
    w_in_r = _pad_rwkv_cols(full['w_in'][:, :RWKV_COLS])
    w_in_a = full['w_in'][:, RWKV_COLS:RWKV_COLS + ATT_COLS]
    w_in_g = full['w_in'][:, RWKV_COLS + ATT_COLS:]
    wlw, wla, wlg = (_pad_rows(full[n], 128).astype(F32) for n in ('rwkv_w_lora_up', 'rwkv_a_lora_up', 'rwkv_g_lora_up'))
    mu = _pad_rwkv_cols(row(w['rwkv_mu']))
    w0, a0, k_k, k_a, r_k, ln_w, ln_b = (row(w[n]) for n in ('rwkv_w0', 'rwkv_a0', 'rwkv_k_k', 'rwkv_k_a', 'rwkv_r_k',
                                                               'rwkv_ln_w', 'rwkv_ln_b'))
    qg = jnp.tile(row(w['attn_q_norm']), (1, N_HEADS))
    kg = jnp.tile(row(w['attn_k_norm']), (1, KVW // HEAD_DIM))
    sinks = jnp.pad(row(w['attn_sinks']), ((0, 0), (0, LANE - N_HEADS)))
    n1, nmix, n2, nfin = (row(w[n]) for n in ('ffn1_norm', 'mix_norm', 'ffn2_norm', 'final_norm'))

    x1, ffn1 = _ffn_fwd(x, n1, full['ffn1_w_gate'], full['ffn1_w_up'], full['ffn1_w_down'], "ffn1")
    h2 = rms_fwd(x1, nmix, name="mix_norm")
    pr = mm(h2, w_in_r, name="proj_rwkv")
    pa = mm(h2, w_in_a, name="proj_att")
    pg = mm(h2, w_in_g, name="proj_gate")
    pr_shift = jnp.pad(pr, ((1, 0), (0, 0)))[:-1]
    r, dec, k2, v, a, b, sg = rwkv_pre_fwd(pr, pr_shift, mu, w0, a0, k_k, k_a, wlw, wla, wlg, name="rwkv_pre")
    y, states = wkv_fwd(r, dec, k2, v, a, b, name="wkv_fwd")
    yr = rwkv_post_fwd(y, r, k2, v, sg, wlg, ln_w, ln_b, r_k, name="rwkv_post")
    ya = att_fwd(pa, qg, kg, sinks, name="att_fwd")
    br = mm(yr, full['w_branch_rwkv'], name="branch_rwkv")
    ba = mm(ya, full['w_branch_attn'], name="branch_att")
    mg = merge_fwd(br, ba, pg, name="merge")
    x2 = mm(mg, full['w_out'], res=x1, name="mix_out")
    x3, ffn2 = _ffn_fwd(x2, n2, full['ffn2_w_gate'], full['ffn2_w_up'], full['ffn2_w_down'], "ffn2")
    dx3, d_nfin, loss = final_loss(x3, tgt, nfin, name="final_loss")

    G = {'final_norm': d_nfin}
    dx2, G['ffn2_norm'], G['ffn2_w_gate'], G['ffn2_w_up'], G['ffn2_w_down'] = _ffn_bwd(
        dx3, x2, ffn2, n2, full['ffn2_w_gate'], full['ffn2_w_up'], full['ffn2_w_down'], "ffn2")
    dmg = mm(dx2, full['w_out'], tb=True, name="d_merge")
    G['w_out'] = mm(mg, dx2, ta=True, name="d_w_out")
    dbr, dba, dpg = merge_bwd(dmg, br, ba, pg, name="merge_bwd")
    dyr = mm(dbr, full['w_branch_rwkv'], tb=True, name="d_y_rwkv")
    G['w_branch_rwkv'] = mm(yr, dbr, ta=True, name="d_w_branch_rwkv")
    dya = mm(dba, full['w_branch_attn'], tb=True, name="d_y_att")
    G['w_branch_attn'] = mm(ya, dba, ta=True, name="d_w_branch_att")
    dy, dz, dg, G['rwkv_ln_w'], G['rwkv_ln_b'] = rwkv_post_bwd(dyr, y, r, k2, v, sg, wlg, ln_w, ln_b, r_k, name="rwkv_post_bwd")
    wkv_grads = wkv_bwd(r, dec, k2, v, a, b, dy, states, name="wkv_bwd")
    (dpr, d_mu, G['rwkv_w0'], G['rwkv_a0'], G['rwkv_k_k'], G['rwkv_k_a'], G['rwkv_r_k'], d_wlw, d_wla, d_wlg) = rwkv_pre_bwd(
        pr, pr_shift, *wkv_grads, dz, dg, mu, w0, a0, k_k, k_a, r_k, wlw, wla, wlg, name="rwkv_pre_bwd")
    G['rwkv_mu'] = _unpad_rwkv_cols(d_mu)
    G['rwkv_w_lora_up'], G['rwkv_a_lora_up'], G['rwkv_g_lora_up'] = d_wlw[:DECAY_LORA], d_wla[:ICLR_LORA], d_wlg[:GATE_LORA]
    dq, dko, dkn, dvo, dvn, G['attn_q_norm'], d_sinks = att_bwd(pa, dya, qg, kg, sinks, name="att_bwd")
    G['attn_sinks'] = d_sinks[:, :N_HEADS]
    dpa, G['attn_k_norm'] = att_kv_bwd(pa, dq, dko, dkn, dvo, dvn, kg, name="att_kv_bwd")
    d_w_in_r = mm(h2, dpr, ta=True, name="d_w_in_rwkv")
    d_w_in_a = mm(h2, dpa, ta=True, name="d_w_in_att")
    d_w_in_g = mm(h2, dpg, ta=True, name="d_w_in_gate")
    G['w_in'] = jnp.concatenate([_unpad_rwkv_cols(d_w_in_r), d_w_in_a, d_w_in_g], axis=1)
    dh2 = mm(dpr, w_in_r, tb=True, name="d_h2_rwkv")
    dh2 = mm(dpa, w_in_a, tb=True, res=dh2, name="d_h2_att")
    dh2 = mm(dpg, w_in_g, tb=True, res=dh2, name="d_h2_gate")
    dx1, G['mix_norm'] = rms_bwd(dh2, x1, nmix, dx2, name="d_mix_norm")
    dx0, G['ffn1_norm'], G['ffn1_w_gate'], G['ffn1_w_up'], G['ffn1_w_down'] = _ffn_bwd(
        dx1, x, ffn1, n1, full['ffn1_w_gate'], full['ffn1_w_up'], full['ffn1_w_down'], "ffn1")

    small_shapes = [(w[n].size,) for n in SMALL] + [(1,)]

    def small_rows(parts):
        vec = jnp.concatenate([p.reshape(-1) for p in parts])
        return jnp.pad(vec, (0, SMALL_ROWS * FLAT_W - vec.shape[0])).reshape(SMALL_ROWS, FLAT_W)

    small = small_rows([G[n] for n in SMALL] + [loss[0, :1]])
    names = [n for n, _ in BIG] + ['small']
    blocks = [_to_blocks(G[n], axis) for n, axis in BIG] + [jnp.broadcast_to(small[None], (N_CHIPS,) + small.shape)]
    wire = [BF16] * len(BIG) + [F32]
    from_sibling = to_sibling(blocks, True, name="grads_to_sibling")
    pair = [pair_sum(g, t, wd, name="pair_sum_" + n) for g, t, wd, n in zip(blocks, from_sibling, wire, names)]
    landed = exchange_chips([q for _, q in pair], name="exchange_grads")
    halves = [half_sum(own, l, name="half_sum_" + n) for (own, _), l, n in zip(pair, landed, names)]
    other_halves = to_sibling(halves, False, name="halves_to_sibling")

    def local(prefix, n):
        if n != 'small':
            return A[prefix + n][0]
        return small_rows([A[prefix + s] for s in SMALL] + [jnp.zeros((1,), F32)])

    result = {}
    for n, mine, theirs in zip(names, halves, other_halves):
        outs4 = adamw(local('', n), local('m_', n), local('v_', n), mine, theirs, name="adamw_" + n)
        for kind, o in zip(('grad_', 'delta_', 'new_m_', 'new_v_'), outs4):
            if n != 'small':
                result[kind + n] = o[None]
            else:
                for s, part in zip(SMALL + ['loss'], _unpack_vec(o.reshape(-1), small_shapes)):
                    result[kind + s] = part.reshape(A[s].shape) if s != 'loss' else part.reshape(())
    outs = [result['grad_loss'], dx0[None]]
    for kind in ('grad_', 'delta_', 'new_m_', 'new_v_'):
        outs += [result[kind + n] for n in WEIGHT_NAMES]
    return tuple(outs)


def _unpack_vec(vec, shapes):
    out, off = [], 0
    for (n,) in shapes:
        out.append(vec[off:off + n])
        off += n
    return out


def kernel(x, ffn1_norm, ffn1_w_gate, ffn1_w_up, ffn1_w_down, mix_norm, w_in, rwkv_mu, rwkv_w0, rwkv_w_lora_up, rwkv_a0, rwkv_a_lora_up, rwkv_g_lora_up, rwkv_k_k, rwkv_k_a, rwkv_r_k, rwkv_ln_w, rwkv_ln_b, attn_q_norm, attn_k_norm, attn_sinks, w_branch_rwkv, w_branch_attn, w_out, ffn2_norm, ffn2_w_gate, ffn2_w_up, ffn2_w_down, final_norm, loss_target, m_ffn1_norm, m_ffn1_w_gate, m_ffn1_w_up, m_ffn1_w_down, m_mix_norm, m_w_in, m_rwkv_mu, m_rwkv_w0, m_rwkv_w_lora_up, m_rwkv_a0, m_rwkv_a_lora_up, m_rwkv_g_lora_up, m_rwkv_k_k, m_rwkv_k_a, m_rwkv_r_k, m_rwkv_ln_w, m_rwkv_ln_b, m_attn_q_norm, m_attn_k_norm, m_attn_sinks, m_w_branch_rwkv, m_w_branch_attn, m_w_out, m_ffn2_norm, m_ffn2_w_gate, m_ffn2_w_up, m_ffn2_w_down, m_final_norm, v_ffn1_norm, v_ffn1_w_gate, v_ffn1_w_up, v_ffn1_w_down, v_mix_norm, v_w_in, v_rwkv_mu, v_rwkv_w0, v_rwkv_w_lora_up, v_rwkv_a0, v_rwkv_a_lora_up, v_rwkv_g_lora_up, v_rwkv_k_k, v_rwkv_k_a, v_rwkv_r_k, v_rwkv_ln_w, v_rwkv_ln_b, v_attn_q_norm, v_attn_k_norm, v_attn_sinks, v_w_branch_rwkv, v_w_branch_attn, v_w_out, v_ffn2_norm, v_ffn2_w_gate, v_ffn2_w_up, v_ffn2_w_down, v_final_norm):
    return _step(dict(locals()))
```

```python
import functools

import jax
import jax.numpy as jnp
from jax import lax
from jax.experimental import pallas as pl
from jax.experimental.pallas import tpu as pltpu

F32 = jnp.float32
BF16 = jnp.bfloat16

D_MODEL = 1024
D_FF = 2816
HEAD_DIM = 64
N_HEADS = 8
RW = 512
KVW = 128
ATT_GROUP = 4
WINDOW = 128
BLOCK = 128
DECAY_LORA, ICLR_LORA, GATE_LORA = 32, 32, 96
RWKV_COLS = 3 * RW + DECAY_LORA + ICLR_LORA + GATE_LORA
ATT_COLS = RW + 2 * KVW
GATE_COLS = 2 * D_MODEL
RWKV_PAD = 3 * RW + 3 * 128
RMS_EPS = 1e-6
GN_EPS = 64e-5
N_CHIPS = 4
LANE = 128
FLAT_W = 1024
SMALL_ROWS = 32
NEG_BIG = -1e30

ADAM_LR, ADAM_B1, ADAM_B2, ADAM_EPS, ADAM_WD, ADAM_STEP = 0.001, 0.9, 0.999, 1e-08, 0.01, 10

VMEM_LIMIT = 56 * 1024 * 1024

WEIGHT_NAMES = ['ffn1_norm', 'ffn1_w_gate', 'ffn1_w_up', 'ffn1_w_down', 'mix_norm', 'w_in', 'rwkv_mu', 'rwkv_w0',
                'rwkv_w_lora_up', 'rwkv_a0', 'rwkv_a_lora_up', 'rwkv_g_lora_up', 'rwkv_k_k', 'rwkv_k_a', 'rwkv_r_k',
                'rwkv_ln_w', 'rwkv_ln_b', 'attn_q_norm', 'attn_k_norm', 'attn_sinks', 'w_branch_rwkv',
                'w_branch_attn', 'w_out', 'ffn2_norm', 'ffn2_w_gate', 'ffn2_w_up', 'ffn2_w_down', 'final_norm']
BIG = [('ffn1_w_gate', 1), ('ffn1_w_up', 1), ('ffn1_w_down', 0), ('w_in', 1), ('rwkv_w_lora_up', 1),
       ('rwkv_a_lora_up', 1), ('rwkv_g_lora_up', 1), ('w_branch_rwkv', 1), ('w_branch_attn', 1), ('w_out', 0),
       ('ffn2_w_gate', 1), ('ffn2_w_up', 1), ('ffn2_w_down', 0)]
SMALL = ['ffn1_norm', 'mix_norm', 'rwkv_mu', 'rwkv_w0', 'rwkv_a0', 'rwkv_k_k', 'rwkv_k_a', 'rwkv_r_k', 'rwkv_ln_w',
         'rwkv_ln_b', 'attn_q_norm', 'attn_k_norm', 'attn_sinks', 'ffn2_norm', 'final_norm']


def _pcall(body, **kw):
    return pl.pallas_call(body, **kw)


def _params(sem=None, **kw):
    if sem is not None:
        kw['dimension_semantics'] = sem
    return pltpu.CompilerParams(vmem_limit_bytes=VMEM_LIMIT, **kw)


def _tile(n, cap, mult):
    best = None
    for t in range(mult, min(n, cap) + 1, mult):
        if n % t == 0:
            best = t
    return best or n


def _sigmoid(z):
    return 1.0 / (1.0 + jnp.exp(-z))


def _softplus(z):
    return jnp.maximum(z, 0.0) + jnp.log(1.0 + jnp.exp(-jnp.abs(z)))


def _bdot(a, b, dims=(((1,), (0,)), ((), ()))):
    return lax.dot_general(a.astype(BF16), b.astype(BF16), dims, preferred_element_type=F32)


_NT = (((1,), (1,)), ((), ()))
_TN = (((0,), (0,)), ((), ()))


def _segsum(x, bd):
    hi = x.astype(BF16)
    r1 = x - hi.astype(F32)
    mid = r1.astype(BF16)
    lo = (r1 - mid.astype(F32)).astype(BF16)
    dot = functools.partial(lax.dot_general, dimension_numbers=(((1,), (0,)), ((), ())), preferred_element_type=F32)
    return dot(hi, bd) + dot(mid, bd) + dot(lo, bd)


_LORA_EDGES = (3 * RW, 3 * RW + DECAY_LORA, 3 * RW + DECAY_LORA + ICLR_LORA, RWKV_COLS)


def _pad_rwkv_cols(x):
    parts = [x[..., :3 * RW]]
    for lo, hi in zip(_LORA_EDGES[:-1], _LORA_EDGES[1:]):
        parts.append(jnp.pad(x[..., lo:hi], [(0, 0)] * (x.ndim - 1) + [(0, 128 - (hi - lo))]))
    return jnp.concatenate(parts, axis=-1)


def _unpad_rwkv_cols(x):
    parts = [x[..., :3 * RW]]
    for j, (lo, hi) in enumerate(zip(_LORA_EDGES[:-1], _LORA_EDGES[1:])):
        parts.append(x[..., 3 * RW + 128 * j:3 * RW + 128 * j + (hi - lo)])
    return jnp.concatenate(parts, axis=-1)


def _pad_rows(x, rows):
    return jnp.pad(x, [(0, rows - x.shape[0])] + [(0, 0)] * (x.ndim - 1))


def mm(a, b, *, name, ta=False, tb=False, scale=None, res=None, out_dtype=F32):
    M, K = (a.shape[1], a.shape[0]) if ta else a.shape
    N = b.shape[0] if tb else b.shape[1]
    assert (b.shape[1] if tb else b.shape[0]) == K
    tm, tn, tk = _tile(M, 1408 if ta else 512, 128), _tile(N, 1408, 128), _tile(K, 1408, 128)
    nk = K // tk
    dims = (((0 if ta else 1,), (1 if tb else 0,)), ((), ()))

    def body(*refs):
        if res is None:
            a_ref, b_ref, o_ref, acc_ref = refs
            r_ref = None
        else:
            a_ref, b_ref, r_ref, o_ref, acc_ref = refs
        k = pl.program_id(2)
        part = _bdot(a_ref[...], b_ref[...], dims)

        @pl.when(k == 0)
        def _():
            acc_ref[...] = part

        @pl.when(k > 0)
        def _():
            acc_ref[...] += part

        @pl.when(k == nk - 1)
        def _():
            o = acc_ref[...]
            if scale is not None:
                o = o * scale
            if r_ref is not None:
                o = o + r_ref[...].astype(F32)
            o_ref[...] = o.astype(out_dtype)

    a_spec = pl.BlockSpec((tk, tm), lambda i, j, k: (k, i)) if ta else pl.BlockSpec((tm, tk), lambda i, j, k: (i, k))
    b_spec = pl.BlockSpec((tn, tk), lambda i, j, k: (j, k)) if tb else pl.BlockSpec((tk, tn), lambda i, j, k: (k, j))
    o_spec = pl.BlockSpec((tm, tn), lambda i, j, k: (i, j))
    in_specs = [a_spec, b_spec] + ([o_spec] if res is not None else [])
    args = (a, b) + ((res,) if res is not None else ())
    return _pcall(
        body, name=name, grid=(M // tm, N // tn, nk), in_specs=in_specs, out_specs=o_spec,
        out_shape=jax.ShapeDtypeStruct((M, N), out_dtype), scratch_shapes=[pltpu.VMEM((tm, tn), F32)],
        compiler_params=_params(("parallel", "parallel", "arbitrary")),
    )(*args)


def _row_spec(tr, c):
    return pl.BlockSpec((tr, c), lambda i: (i, 0))


def _full_spec(shape):
    return pl.BlockSpec(shape, lambda i: (0,) * len(shape))


def _acc_rows(ref, val, i):
    @pl.when(i == 0)
    def _():
        ref[...] = val

    @pl.when(i > 0)
    def _():
        ref[...] += val


def rms_fwd(x, g, *, name):
    T, D = x.shape
    tr = _tile(T, 512, 8)

    def body(x_ref, g_ref, h_ref):
        xv = x_ref[...]
        r = lax.rsqrt(jnp.mean(xv * xv, axis=-1, keepdims=True) + RMS_EPS)
        h_ref[...] = (xv * r * g_ref[...]).astype(BF16)

    return _pcall(body, name=name, grid=(T // tr,), in_specs=[_row_spec(tr, D), _full_spec((1, D))],
                  out_specs=_row_spec(tr, D), out_shape=jax.ShapeDtypeStruct((T, D), BF16),
                  compiler_params=_params(("parallel",)))(x, g)


def rms_bwd(dh, x, g, res, *, name):
    T, D = x.shape
    tr = _tile(T, 256, 8)

    def body(dh_ref, x_ref, g_ref, res_ref, dx_ref, dg_ref):
        i = pl.program_id(0)
        xv, dhv = x_ref[...], dh_ref[...].astype(F32)
        r = lax.rsqrt(jnp.mean(xv * xv, axis=-1, keepdims=True) + RMS_EPS)
        xh = xv * r
        dxh = dhv * g_ref[...]
        dx_ref[...] = res_ref[...] + r * (dxh - xh * jnp.mean(dxh * xh, axis=-1, keepdims=True))
        _acc_rows(dg_ref, jnp.sum(dhv * xh, axis=0, keepdims=True), i)

    return _pcall(body, name=name, grid=(T // tr,),
                  in_specs=[_row_spec(tr, D), _row_spec(tr, D), _full_spec((1, D)), _row_spec(tr, D)],
                  out_specs=[_row_spec(tr, D), _full_spec((1, D))],
                  out_shape=[jax.ShapeDtypeStruct((T, D), F32), jax.ShapeDtypeStruct((1, D), F32)],
                  compiler_params=_params(("arbitrary",)))(dh, x, g, res)


def final_loss(x, tgt, g, *, name):
    T, D = x.shape
    tr = _tile(T, 256, 8)

    def body(x_ref, t_ref, g_ref, dx_ref, dg_ref, loss_ref):
        i = pl.program_id(0)
        xv = x_ref[...]
        r = lax.rsqrt(jnp.mean(xv * xv, axis=-1, keepdims=True) + RMS_EPS)
        xh = xv * r
        e = xh * g_ref[...] - t_ref[...]
        part = 0.5 * jnp.sum(jnp.mean(e * e, axis=-1, keepdims=True), axis=0, keepdims=True)
        dy = e * (1.0 / D)
        dxh = dy * g_ref[...]
        dx_ref[...] = r * (dxh - xh * jnp.mean(dxh * xh, axis=-1, keepdims=True))
        _acc_rows(dg_ref, jnp.sum(dy * xh, axis=0, keepdims=True), i)
        _acc_rows(loss_ref, jnp.broadcast_to(part, (1, LANE)), i)

    return _pcall(body, name=name, grid=(T // tr,),
                  in_specs=[_row_spec(tr, D), _row_spec(tr, D), _full_spec((1, D))],
                  out_specs=[_row_spec(tr, D), _full_spec((1, D)), _full_spec((1, LANE))],
                  out_shape=[jax.ShapeDtypeStruct((T, D), F32), jax.ShapeDtypeStruct((1, D), F32),
                             jax.ShapeDtypeStruct((1, LANE), F32)],
                  compiler_params=_params(("arbitrary",)))(x, tgt, g)


def swiglu_fwd(gate, up, *, name):
    T, F = gate.shape
    tr = _tile(T, 256, 8)

    def body(g_ref, u_ref, a_ref):
        gv = g_ref[...].astype(F32)
        a_ref[...] = (gv * _sigmoid(gv) * u_ref[...].astype(F32)).astype(BF16)

    return _pcall(body, name=name, grid=(T // tr,), in_specs=[_row_spec(tr, F), _row_spec(tr, F)],
                  out_specs=_row_spec(tr, F), out_shape=jax.ShapeDtypeStruct((T, F), BF16),
                  compiler_params=_params(("parallel",)))(gate, up)


def swiglu_bwd(da, gate, up, *, name):
    T, F = gate.shape
    tr = _tile(T, 256, 8)

    def body(da_ref, g_ref, u_ref, dg_ref, du_ref):
        gv, dav = g_ref[...].astype(F32), da_ref[...].astype(F32)
        s = _sigmoid(gv)
        du_ref[...] = (dav * gv * s).astype(BF16)
        dg_ref[...] = (dav * u_ref[...].astype(F32) * s * (1.0 + gv * (1.0 - s))).astype(BF16)

    return _pcall(body, name=name, grid=(T // tr,), in_specs=[_row_spec(tr, F)] * 3,
                  out_specs=[_row_spec(tr, F)] * 2, out_shape=[jax.ShapeDtypeStruct((T, F), BF16)] * 2,
                  compiler_params=_params(("parallel",)))(da, gate, up)


def merge_fwd(br, ba, pg, *, name):
    T, D = br.shape
    tr = _tile(T, 256, 8)

    def body(br_ref, ba_ref, pg_ref, o_ref):
        pgv = pg_ref[...]
        o_ref[...] = (_sigmoid(pgv[:, :D]) * br_ref[...] + _sigmoid(pgv[:, D:]) * ba_ref[...]).astype(BF16)

    return _pcall(body, name=name, grid=(T // tr,), in_specs=[_row_spec(tr, D), _row_spec(tr, D), _row_spec(tr, 2 * D)],
                  out_specs=_row_spec(tr, D), out_shape=jax.ShapeDtypeStruct((T, D), BF16),
                  compiler_params=_params(("parallel",)))(br, ba, pg)


def merge_bwd(dm, br, ba, pg, *, name):
    T, D = br.shape
    tr = _tile(T, 256, 8)

    def body(dm_ref, br_ref, ba_ref, pg_ref, dbr_ref, dba_ref, dpg_ref):
        pgv, dmv = pg_ref[...], dm_ref[...]
        sr, sa = _sigmoid(pgv[:, :D]), _sigmoid(pgv[:, D:])
        dbr_ref[...] = (dmv * sr).astype(BF16)
        dba_ref[...] = (dmv * sa).astype(BF16)
        dpg_ref[:, :D] = dmv * br_ref[...] * sr * (1.0 - sr)
        dpg_ref[:, D:] = dmv * ba_ref[...] * sa * (1.0 - sa)

    return _pcall(body, name=name, grid=(T // tr,),
                  in_specs=[_row_spec(tr, D), _row_spec(tr, D), _row_spec(tr, D), _row_spec(tr, 2 * D)],
                  out_specs=[_row_spec(tr, D), _row_spec(tr, D), _row_spec(tr, 2 * D)],
                  out_shape=[jax.ShapeDtypeStruct((T, D), BF16), jax.ShapeDtypeStruct((T, D), BF16),
                             jax.ShapeDtypeStruct((T, 2 * D), F32)],
                  compiler_params=_params(("parallel",)))(dm, br, ba, pg)


def _rwkv_mix(p, prev, mu, w0, a0, k_k, k_a, wlw, wla, wlg, bd):
    pp = p + (prev - p) * mu
    r, k, v = pp[:, 0:RW], pp[:, RW:2 * RW], pp[:, 2 * RW:3 * RW]
    xw, xa, xg = pp[:, 3 * RW:3 * RW + 128], pp[:, 3 * RW + 128:3 * RW + 256], pp[:, 3 * RW + 256:3 * RW + 384]
    th = jnp.tanh(xw)
    z = -(w0 + _bdot(th, wlw))
    e = jnp.exp(-_softplus(z) - 0.5)
    decay = jnp.exp(-e)
    a = _sigmoid(a0 + _bdot(xa, wla))
    sg = _sigmoid(xg)
    kkr = k * k_k
    n = jnp.sqrt(_segsum(kkr * kkr, bd))
    kk = kkr / jnp.maximum(n, 1e-12)
    k2 = k * (1.0 + (a - 1.0) * k_a)
    return dict(r=r, k=k, v=v, xa=xa, th=th, z=z, e=e, decay=decay, a=a, sg=sg, n=n, kk=kk, k2=k2)


def _seg_matrix(n, shift):
    r = lax.shift_right_logical(lax.broadcasted_iota(jnp.int32, (n, n), 0), shift)
    c = lax.shift_right_logical(lax.broadcasted_iota(jnp.int32, (n, n), 1), shift)
    return jnp.where(r == c, 1.0, 0.0).astype(BF16)


def rwkv_pre_fwd(p, pshift, mu, w0, a0, k_k, k_a, wlw, wla, wlg, *, name):
    T = p.shape[0]
    tr = _tile(T, 256, 8)

    def body(p_ref, ps_ref, mu_ref, w0_ref, a0_ref, kk_ref, ka_ref, wlw_ref, wla_ref, wlg_ref,
             r_ref, w_ref, k_ref, v_ref, a_ref, b_ref, g_ref):
        pv, prev = p_ref[...], ps_ref[...]
        m = _rwkv_mix(pv, prev, mu_ref[...], w0_ref[...], a0_ref[...], kk_ref[...], ka_ref[...],
                      wlw_ref[...], wla_ref[...], wlg_ref[...], _seg_matrix(RW, 6))
        r_ref[...] = m['r']
        w_ref[...] = m['decay']
        k_ref[...] = m['k2']
        v_ref[...] = m['v']
        a_ref[...] = -m['kk']
        b_ref[...] = m['kk'] * m['a']
        g_ref[...] = m['sg']

    vec = _row_spec(tr, RW)
    return _pcall(
        body, name=name, grid=(T // tr,),
        in_specs=[_row_spec(tr, RWKV_PAD), _row_spec(tr, RWKV_PAD), _full_spec((1, RWKV_PAD))] + [_full_spec((1, RW))] * 4
        + [_full_spec((128, RW))] * 3,
        out_specs=[vec] * 6 + [_row_spec(tr, 128)],
        out_shape=[jax.ShapeDtypeStruct((T, RW), F32)] * 6 + [jax.ShapeDtypeStruct((T, 128), F32)],
        compiler_params=_params(("parallel",)),
    )(p, pshift, mu, w0, a0, k_k, k_a, wlw, wla, wlg)


def _group_norm(y, bd):
    mean = _segsum(y, bd) * (1.0 / HEAD_DIM)
    yc = y - mean
    rstd = lax.rsqrt(_segsum(yc * yc, bd) * (1.0 / HEAD_DIM) + GN_EPS)
    return yc * rstd, rstd


def rwkv_post_fwd(y, r, k2, v, sg, wlg, ln_w, ln_b, r_k, *, name):
    T = y.shape[0]
    tr = _tile(T, 256, 8)

    def body(y_ref, r_ref, k_ref, v_ref, sg_ref, wlg_ref, lw_ref, lb_ref, rk_ref, o_ref):
        bd = _seg_matrix(RW, 6)
        yn, _ = _group_norm(y_ref[...], bd)
        s = _segsum(r_ref[...] * k_ref[...] * rk_ref[...], bd)
        g = _bdot(sg_ref[...], wlg_ref[...])
        o_ref[...] = ((yn * lw_ref[...] + lb_ref[...] + s * v_ref[...]) * g).astype(BF16)

    vec = _row_spec(tr, RW)
    return _pcall(body, name=name, grid=(T // tr,),
                  in_specs=[vec] * 4 + [_row_spec(tr, 128), _full_spec((128, RW))] + [_full_spec((1, RW))] * 3, out_specs=vec,
                  out_shape=jax.ShapeDtypeStruct((T, RW), BF16), compiler_params=_params(("parallel",)))(
                      y, r, k2, v, sg, wlg, ln_w, ln_b, r_k)


def rwkv_post_bwd(dyr, y, r, k2, v, sg, wlg, ln_w, ln_b, r_k, *, name):
    T = y.shape[0]
    tr = _tile(T, 256, 8)

    def body(dyr_ref, y_ref, r_ref, k_ref, v_ref, sg_ref, wlg_ref, lw_ref, lb_ref, rk_ref,
             dy_ref, dz_ref, dg_ref, dlw_ref, dlb_ref):
        i = pl.program_id(0)
        bd = _seg_matrix(RW, 6)
        yn, rstd = _group_norm(y_ref[...], bd)
        s = _segsum(r_ref[...] * k_ref[...] * rk_ref[...], bd)
        dyrv = dyr_ref[...]
        dg_ref[...] = dyrv * (yn * lw_ref[...] + lb_ref[...] + s * v_ref[...])
        dz = dyrv * _bdot(sg_ref[...], wlg_ref[...])
        dz_ref[...] = dz
        dyn = dz * lw_ref[...]
        inv = 1.0 / HEAD_DIM
        dy_ref[...] = rstd * (dyn - _segsum(dyn, bd) * inv - yn * (_segsum(dyn * yn, bd) * inv))
        _acc_rows(dlw_ref, jnp.sum(dz * yn, axis=0, keepdims=True), i)
        _acc_rows(dlb_ref, jnp.sum(dz, axis=0, keepdims=True), i)

    vec = _row_spec(tr, RW)
    one = _full_spec((1, RW))
    return _pcall(body, name=name, grid=(T // tr,),
                  in_specs=[vec] * 5 + [_row_spec(tr, 128), _full_spec((128, RW))] + [one] * 3, out_specs=[vec] * 3 + [one] * 2,
                  out_shape=[jax.ShapeDtypeStruct((T, RW), F32)] * 3 + [jax.ShapeDtypeStruct((1, RW), F32)] * 2,
                  compiler_params=_params(("arbitrary",)))(dyr, y, r, k2, v, sg, wlg, ln_w, ln_b, r_k)


def rwkv_pre_bwd(p, pshift, dr_w, dw_w, dk_w, dv_w, da_w, db_w, dz, dg, mu, w0, a0, k_k, k_a, r_k, wlw, wla, wlg, *, name):
    T = p.shape[0]
    tr = _tile(T, 256, 8)
    n = T // tr

    def body(p_ref, ps_ref, dr_ref, dw_ref, dk_ref, dv_ref, da_ref, db_ref, dz_ref, dg_ref,
             mu_ref, w0_ref, a0_ref, kk_ref, ka_ref, rk_ref, wlw_ref, wla_ref, wlg_ref,
             dp_ref, dmu_ref, dw0_ref, da0_ref, dkk_ref, dka_ref, drk_ref, dwlw_ref, dwla_ref, dwlg_ref,
             carry, dpp, acc_w, acc_a, acc_g):
        i = pl.program_id(0)

        @pl.when(i == 0)
        def _():
            carry[...] = jnp.zeros_like(carry)

        pv, prev, mu = p_ref[...], ps_ref[...], mu_ref[...]
        bd = _seg_matrix(RW, 6)
        k_k, k_a, r_k = kk_ref[...], ka_ref[...], rk_ref[...]
        m = _rwkv_mix(pv, prev, mu, w0_ref[...], a0_ref[...], k_k, k_a, wlw_ref[...], wla_ref[...], wlg_ref[...], bd)
        r, k, v, a, kk, k2 = m['r'], m['k'], m['v'], m['a'], m['kk'], m['k2']
        dzv, dgv = dz_ref[...], dg_ref[...]
        s = _segsum(r * k2 * r_k, bd)
        ds = _segsum(dzv * v, bd)
        dr = dr_ref[...] + ds * k2 * r_k
        dk2 = dk_ref[...] + ds * r * r_k
        dv = dv_ref[...] + dzv * s
        dbv = db_ref[...]
        dkk = dbv * a - da_ref[...]
        da = dbv * kk + dk2 * k * k_a
        dk = dk2 * (1.0 + (a - 1.0) * k_a)
        nmax = jnp.maximum(m['n'], 1e-12)
        dkkr = jnp.where(m['n'] > 1e-12, dkk - kk * _segsum(dkk * kk, bd), dkk) / nmax
        dk = dk + dkkr * k_k
        dapre = da * a * (1.0 - a)
        dwpre = dw_ref[...] * m['decay'] * (-m['e']) * _sigmoid(m['z'])
        dth = _bdot(dwpre, wlw_ref[...], _NT)
        dxa = _bdot(dapre, wla_ref[...], _NT)
        dsg = _bdot(dgv, wlg_ref[...], _NT)
        dpp[:, 0:RW] = dr
        dpp[:, RW:2 * RW] = dk
        dpp[:, 2 * RW:3 * RW] = dv
        dpp[:, 3 * RW:3 * RW + 128] = dth * (1.0 - m['th'] * m['th'])
        dpp[:, 3 * RW + 128:3 * RW + 256] = dxa
        dpp[:, 3 * RW + 256:3 * RW + 384] = dsg * m['sg'] * (1.0 - m['sg'])
        d = dpp[...]
        zed = d * mu
        last = lax.broadcasted_iota(jnp.int32, pv.shape, 0) == tr - 1
        dp_ref[...] = d * (1.0 - mu) + jnp.where(last, carry[0:1, :], pltpu.roll(zed, tr - 1, 0))
        carry[...] = zed[0:8, :]

        def colsum(x):
            return jnp.sum(x, axis=0, keepdims=True)

        _acc_rows(dmu_ref, colsum(d * (prev - pv)), i)
        _acc_rows(dw0_ref, colsum(dwpre), i)
        _acc_rows(da0_ref, colsum(dapre), i)
        _acc_rows(dkk_ref, colsum(dkkr * k), i)
        _acc_rows(dka_ref, colsum(dk2 * k * (a - 1.0)), i)
        _acc_rows(drk_ref, colsum(ds * r * k2), i)
        _acc_rows(acc_w, _bdot(m['th'], dwpre, _TN), i)
        _acc_rows(acc_a, _bdot(m['xa'], dapre, _TN), i)
        _acc_rows(acc_g, _bdot(m['sg'], dgv, _TN), i)

        @pl.when(i == n - 1)
        def _():
            dwlw_ref[...] = acc_w[...]
            dwla_ref[...] = acc_a[...]
            dwlg_ref[...] = acc_g[...]

    rev = lambda c: pl.BlockSpec((tr, c), lambda i: (n - 1 - i, 0))
    one, lora = _full_spec((1, RW)), _full_spec((128, RW))
    return _pcall(
        body, name=name, grid=(n,),
        in_specs=[rev(RWKV_PAD), rev(RWKV_PAD)] + [rev(RW)] * 8 + [_full_spec((1, RWKV_PAD))] + [one] * 5 + [lora] * 3,
        out_specs=[rev(RWKV_PAD), _full_spec((1, RWKV_PAD))] + [one] * 5 + [lora] * 3,
        out_shape=[jax.ShapeDtypeStruct((T, RWKV_PAD), F32), jax.ShapeDtypeStruct((1, RWKV_PAD), F32)]
        + [jax.ShapeDtypeStruct((1, RW), F32)] * 5 + [jax.ShapeDtypeStruct((128, RW), F32)] * 3,
        scratch_shapes=[pltpu.VMEM((8, RWKV_PAD), F32), pltpu.VMEM((tr, RWKV_PAD), F32)] + [pltpu.VMEM((128, RW), F32)] * 3,
        compiler_params=_params(("arbitrary",)),
    )(p, pshift, dr_w, dw_w, dk_w, dv_w, da_w, db_w, dz, dg, mu, w0, a0, k_k, k_a, r_k, wlw, wla, wlg)


def _qk_norm(x, g, bd):
    r = lax.rsqrt(_segsum(x * x, bd) * (1.0 / HEAD_DIM) + RMS_EPS)
    return x * r * g, r


def _att_mask(i):
    qi = lax.broadcasted_iota(jnp.int32, (BLOCK, 2 * BLOCK), 0)
    kj = lax.broadcasted_iota(jnp.int32, (BLOCK, 2 * BLOCK), 1)
    band = (kj <= qi + BLOCK) & (kj > qi + BLOCK - WINDOW)
    return band & ((kj >= BLOCK) | (i > 0))


def _att_probs(qh, kh, mask, sink):
    s = _bdot(qh, kh, _NT) * (HEAD_DIM ** -0.5)
    s = jnp.where(mask, s, NEG_BIG)
    m = jnp.maximum(jnp.max(s, axis=-1, keepdims=True), sink)
    pexp = jnp.exp(s - m)
    psink = jnp.exp(sink - m)
    inv = 1.0 / (jnp.sum(pexp, axis=-1, keepdims=True) + psink)
    return pexp * inv, psink * inv


def _att_blocks(n):
    cur = pl.BlockSpec((BLOCK, ATT_COLS), lambda i: (i, 0))
    prev = pl.BlockSpec((BLOCK, ATT_COLS), lambda i: (jnp.maximum(i - 1, 0), 0))
    return cur, prev


def _att_qkv(cur, prev, qn_g, kn_g):
    bq, bk = _seg_matrix(RW, 6), _seg_matrix(KVW, 6)
    qn, rq = _qk_norm(cur[:, 0:RW], qn_g, bq)
    kcur, rkc = _qk_norm(cur[:, RW:RW + KVW], kn_g, bk)
    kprev, _ = _qk_norm(prev[:, RW:RW + KVW], kn_g, bk)
    kc = jnp.concatenate([kprev, kcur], axis=0)
    vc = jnp.concatenate([prev[:, RW + KVW:], cur[:, RW + KVW:]], axis=0)
    return qn, rq, kc, vc, rkc


def att_fwd(pa, qn_g, kn_g, sinks, *, name):
    T = pa.shape[0]
    n = T // BLOCK

    def body(cur_ref, prev_ref, qg_ref, kg_ref, sk_ref, o_ref):
        i = pl.program_id(0)
        qn, _, kc, vc, _ = _att_qkv(cur_ref[...], prev_ref[...], qg_ref[...], kg_ref[...])
        mask = _att_mask(i)
        outs = []
        for h in range(N_HEADS):
            kv = slice((h // ATT_GROUP) * HEAD_DIM, (h // ATT_GROUP + 1) * HEAD_DIM)
            probs, _ = _att_probs(qn[:, h * HEAD_DIM:(h + 1) * HEAD_DIM], kc[:, kv], mask, sk_ref[0:1, h:h + 1])
            outs.append(_bdot(probs, vc[:, kv]))
        o_ref[...] = jnp.concatenate(outs, axis=1)

    cur, prev = _att_blocks(n)
    return _pcall(body, name=name, grid=(n,),
                  in_specs=[cur, prev, _full_spec((1, RW)), _full_spec((1, KVW)), _full_spec((1, LANE))],
                  out_specs=pl.BlockSpec((BLOCK, RW), lambda i: (i, 0)), out_shape=jax.ShapeDtypeStruct((T, RW), F32),
                  compiler_params=_params(("parallel",)))(pa, pa, qn_g, kn_g, sinks)


def att_bwd(pa, do, qn_g, kn_g, sinks, *, name):
    T = pa.shape[0]
    n = T // BLOCK

    def body(cur_ref, prev_ref, do_ref, qg_ref, kg_ref, sk_ref,
             dq_ref, dko_ref, dkn_ref, dvo_ref, dvn_ref, dqg_ref, dsk_ref):
        i = pl.program_id(0)
        cur = cur_ref[...]
        qn, rq, kc, vc, _ = _att_qkv(cur, prev_ref[...], qg_ref[...], kg_ref[...])
        mask = _att_mask(i)
        dov = do_ref[...]
        lane = lax.broadcasted_iota(jnp.int32, (1, LANE), 1)
        dsink = jnp.zeros((1, LANE), F32)
        dqn, dkc, dvc = [], [None, None], [None, None]
        for h in range(N_HEADS):
            g = h // ATT_GROUP
            kv = slice(g * HEAD_DIM, (g + 1) * HEAD_DIM)
            qh, doh = qn[:, h * HEAD_DIM:(h + 1) * HEAD_DIM], dov[:, h * HEAD_DIM:(h + 1) * HEAD_DIM]
            probs, psink = _att_probs(qh, kc[:, kv], mask, sk_ref[0:1, h:h + 1])
            dprobs = _bdot(doh, vc[:, kv], _NT)
            delta = jnp.sum(probs * dprobs, axis=-1, keepdims=True)
            ds = probs * (dprobs - delta) * (HEAD_DIM ** -0.5)
            dsink = dsink + jnp.where(lane == h, -jnp.sum(psink * delta, axis=0, keepdims=True), 0.0)
            dqn.append(_bdot(ds, kc[:, kv]))
            dk_h, dv_h = _bdot(ds, qh, _TN), _bdot(probs, doh, _TN)
            dkc[g] = dk_h if dkc[g] is None else dkc[g] + dk_h
            dvc[g] = dv_h if dvc[g] is None else dvc[g] + dv_h
        dqn = jnp.concatenate(dqn, axis=1)
        dk = jnp.concatenate(dkc, axis=1)
        dv = jnp.concatenate(dvc, axis=1)
        dkn_ref[...], dko_ref[...] = dk[0:BLOCK], dk[BLOCK:]
        dvn_ref[...], dvo_ref[...] = dv[0:BLOCK], dv[BLOCK:]
        qhat = cur[:, 0:RW] * rq
        dqh = dqn * qg_ref[...]
        dq_ref[...] = rq * (dqh - qhat * (_segsum(dqh * qhat, _seg_matrix(RW, 6)) * (1.0 / HEAD_DIM)))
        prod = dqn * qhat
        fold = prod[:, 0:HEAD_DIM]
        for h in range(1, N_HEADS):
            fold = fold + prod[:, h * HEAD_DIM:(h + 1) * HEAD_DIM]
        _acc_rows(dqg_ref, jnp.sum(fold, axis=0, keepdims=True), i)
        _acc_rows(dsk_ref, dsink, i)

    cur, prev = _att_blocks(n)
    kvb = pl.BlockSpec((BLOCK, KVW), lambda i: (i, 0))
    qb = pl.BlockSpec((BLOCK, RW), lambda i: (i, 0))
    return _pcall(body, name=name, grid=(n,),
                  in_specs=[cur, prev, qb, _full_spec((1, RW)), _full_spec((1, KVW)), _full_spec((1, LANE))],
                  out_specs=[qb, kvb, kvb, kvb, kvb, _full_spec((1, HEAD_DIM)), _full_spec((1, LANE))],
                  out_shape=[jax.ShapeDtypeStruct((T, RW), F32)] + [jax.ShapeDtypeStruct((T, KVW), F32)] * 4
                  + [jax.ShapeDtypeStruct((1, HEAD_DIM), F32), jax.ShapeDtypeStruct((1, LANE), F32)],
                  compiler_params=_params(("arbitrary",)))(pa, pa, do, qn_g, kn_g, sinks)


def att_kv_bwd(pa, dq, dko, dkn, dvo, dvn, kn_g, *, name):
    T = pa.shape[0]
    n = T // BLOCK

    def body(pa_ref, dq_ref, dko_ref, dkn_ref, dvo_ref, dvn_ref, kg_ref, dpa_ref, dkg_ref):
        i = pl.program_id(0)
        more = i < n - 1
        dkn_tot = dko_ref[...] + jnp.where(more, dkn_ref[...], 0.0)
        dv_tot = dvo_ref[...] + jnp.where(more, dvn_ref[...], 0.0)
        kraw = pa_ref[:, RW:RW + KVW]
        bk = _seg_matrix(KVW, 6)
        _, rk = _qk_norm(kraw, kg_ref[...], bk)
        khat = kraw * rk
        dkh = dkn_tot * kg_ref[...]
        dpa_ref[:, 0:RW] = dq_ref[...]
        dpa_ref[:, RW:RW + KVW] = rk * (dkh - khat * (_segsum(dkh * khat, bk) * (1.0 / HEAD_DIM)))
        dpa_ref[:, RW + KVW:] = dv_tot
        prod = dkn_tot * khat
        _acc_rows(dkg_ref, jnp.sum(prod[:, 0:HEAD_DIM] + prod[:, HEAD_DIM:], axis=0, keepdims=True), i)

    kvb = pl.BlockSpec((BLOCK, KVW), lambda i: (i, 0))
    nxt = pl.BlockSpec((BLOCK, KVW), lambda i: (jnp.minimum(i + 1, n - 1), 0))
    return _pcall(body, name=name, grid=(n,),
                  in_specs=[pl.BlockSpec((BLOCK, ATT_COLS), lambda i: (i, 0)), pl.BlockSpec((BLOCK, RW), lambda i: (i, 0)),
                            kvb, nxt, kvb, nxt, _full_spec((1, KVW))],
                  out_specs=[pl.BlockSpec((BLOCK, ATT_COLS), lambda i: (i, 0)), _full_spec((1, HEAD_DIM))],
                  out_shape=[jax.ShapeDtypeStruct((T, ATT_COLS), F32), jax.ShapeDtypeStruct((1, HEAD_DIM), F32)],
                  compiler_params=_params(("arbitrary",)))(pa, dq, dko, dkn, dvo, dvn, kn_g)


WKV_CHUNK = 64
WKV_GROUP = 8


def _diag_mask():
    i = lax.broadcasted_iota(jnp.int32, (HEAD_DIM, RW), 0)
    j = lax.broadcasted_iota(jnp.int32, (HEAD_DIM, RW), 1) & (HEAD_DIM - 1)
    return i == j


def _headsums(xs, pieces):
    half = RW // 2
    bd = _seg_matrix(half, 6)
    bd = jnp.concatenate([bd] * pieces, axis=0)
    rows = []
    for x in xs:
        parts, rest = [], x
        for n in range(pieces):
            p = rest.astype(BF16)
            parts.append(p)
            if n + 1 < pieces:
                rest = rest - p.astype(F32)
        for sl in (slice(0, half), slice(half, RW)):
            rows.append(jnp.concatenate([p[:, sl] for p in parts], axis=1))
    out = lax.dot_general(jnp.concatenate(rows, axis=0), bd, (((1,), (0,)), ((), ())), preferred_element_type=F32)
    return [jnp.concatenate([out[2 * n * HEAD_DIM:(2 * n + 1) * HEAD_DIM], out[(2 * n + 1) * HEAD_DIM:(2 * n + 2) * HEAD_DIM]],
                            axis=1) for n in range(len(xs))]


def _headsum(x):
    low = lax.broadcasted_iota(jnp.int32, (HEAD_DIM, LANE), 1) < HEAD_DIM
    tiles = []
    for c in range(RW // LANE):
        xt = x[:, c * LANE:(c + 1) * LANE]
        s_lo = jnp.sum(jnp.where(low, xt, 0.0), axis=1, keepdims=True)
        s_hi = jnp.sum(jnp.where(low, 0.0, xt), axis=1, keepdims=True)
        tiles.append(jnp.where(low, s_lo, s_hi))
    return jnp.concatenate(tiles, axis=1)


def _cols(rows, diag):
    return _headsums([jnp.where(diag, r, 0.0) for r in rows], 2)


def _row(x, diag):
    return jnp.sum(jnp.where(diag, x, 0.0), axis=0, keepdims=True)


def wkv_fwd(r, w, k, v, a, b, *, name, gather=()):
    T = r.shape[0]
    ch = min(WKV_CHUNK, T)
    ngroups = ch // WKV_GROUP
    nchunks = T // ch
    ng = len(gather)

    def body(*refs):
        r_ref, w_ref, k_ref, v_ref, a_ref, b_ref = refs[:6]
        y_ref, st_ref = refs[6 + ng:8 + ng]
        s_scr = refs[8 + 2 * ng]
        step = pl.program_id(0)
        if ng:
            plan = _GatherPlan(refs[6:6 + ng], refs[8 + ng:8 + 2 * ng], refs[9 + 2 * ng:])
            pl.when(step == 0)(plan.start)
            pl.when(step == nchunks // 2)(plan.relay)

        @pl.when(step == 0)
        def _():
            s_scr[...] = jnp.zeros_like(s_scr)

        diag = _diag_mask()

        def group(gi, S):
            t0 = pl.multiple_of(gi * WKV_GROUP, WKV_GROUP)
            rows = pl.ds(t0, WKV_GROUP)
            R, W, K, V, A, B = (ref[rows, :] for ref in (r_ref, w_ref, k_ref, v_ref, a_ref, b_ref))
            vcols = _cols([V[s:s + 1] for s in range(WKV_GROUP)], diag)
            yrows = []
            for s in range(WKV_GROUP):
                sa = _headsum(S * A[s:s + 1])
                S = S * W[s:s + 1] + sa * B[s:s + 1] + vcols[s] * K[s:s + 1]
                st_ref[t0 + s] = S
                yrows.append(_row(_headsum(S * R[s:s + 1]), diag))
            y_ref[rows, :] = jnp.concatenate(yrows, axis=0)
            return S

        s_scr[...] = lax.fori_loop(0, ngroups, group, s_scr[...])
        if ng:
            pl.when(step == nchunks - 1)(plan.finish)

    vec = pl.BlockSpec((ch, RW), lambda c: (c, 0))
    return _pcall(
        body, name=name, grid=(nchunks,), in_specs=[vec] * 6 + [_HBM] * ng,
        out_specs=[vec, pl.BlockSpec((ch, HEAD_DIM, RW), lambda c: (c, 0, 0))] + [_HBM] * ng,
        out_shape=[jax.ShapeDtypeStruct((T, RW), F32), jax.ShapeDtypeStruct((T, HEAD_DIM, RW), F32)] + _gathered_shapes(gather),
        scratch_shapes=[pltpu.VMEM((HEAD_DIM, RW), F32)] + (_GatherPlan.sems(ng) if ng else []),
        compiler_params=_params(("arbitrary",)),
    )(r, w, k, v, a, b, *gather)


def wkv_bwd(r, w, k, v, a, b, dy, states, *, name, exchange=()):
    T = r.shape[0]
    ch = min(WKV_CHUNK, T)
    nchunks = T // ch
    ngroups = ch // WKV_GROUP
    ne = len(exchange)

    def body(*refs):
        r_ref, w_ref, k_ref, v_ref, a_ref, b_ref, dy_ref, st_ref, stp_ref = refs[:9]
        dr_ref, dw_ref, dk_ref, dv_ref, da_ref, db_ref = refs[9 + ne:15 + ne]
        ds_scr = refs[15 + 2 * ne]
        step = pl.program_id(0)
        if ne:
            plan = _ExchangePlan(refs[9:9 + ne], refs[15 + ne:15 + 2 * ne], refs[16 + 2 * ne:])
            pl.when(step == 0)(plan.start)

        @pl.when(step == 0)
        def _():
            ds_scr[...] = jnp.zeros_like(ds_scr)

        has_prev_chunk = step < nchunks - 1
        diag = _diag_mask()
        colsum = lambda x: jnp.sum(x, axis=0, keepdims=True)

        def group(gj, dS):
            gi = ngroups - 1 - gj
            t0 = pl.multiple_of(gi * WKV_GROUP, WKV_GROUP)
            rows = pl.ds(t0, WKV_GROUP)
            R, W, K, V, A, B, DY = (ref[rows, :] for ref in (r_ref, w_ref, k_ref, v_ref, a_ref, b_ref, dy_ref))
            before = jnp.where(gi > 0, st_ref[jnp.maximum(t0 - 1, 0)], jnp.where(has_prev_chunk, stp_ref[0], 0.0))
            prev_state = lambda s: st_ref[t0 + s - 1] if s > 0 else before
            steps = range(WKV_GROUP)
            dycols = _cols([DY[s:s + 1] for s in steps], diag)
            vcols = _cols([V[s:s + 1] for s in steps], diag)
            sas = _headsums([prev_state(s) * A[s:s + 1] for s in steps], 1)
            got = [[None] * WKV_GROUP for _ in range(6)]
            for s in reversed(steps):
                Sp = prev_state(s)
                dS = dS + dycols[s] * R[s:s + 1]
                got[0][s] = colsum(st_ref[t0 + s] * dycols[s])
                got[3][s] = _row(_headsum(dS * K[s:s + 1]), diag)
                got[2][s] = colsum(dS * vcols[s])
                dsa = _headsum(dS * B[s:s + 1])
                got[5][s] = colsum(dS * sas[s])
                got[1][s] = colsum(dS * Sp)
                got[4][s] = colsum(Sp * dsa)
                dS = dS * W[s:s + 1] + dsa * A[s:s + 1]
            for q, ref in enumerate((dr_ref, dw_ref, dk_ref, dv_ref, da_ref, db_ref)):
                ref[rows, :] = jnp.concatenate(got[q], axis=0)
            return dS

        ds_scr[...] = lax.fori_loop(0, ngroups, group, ds_scr[...])
        if ne:
            pl.when(step == nchunks - 1)(plan.finish)

    vec = pl.BlockSpec((ch, RW), lambda c: (nchunks - 1 - c, 0))
    st_spec = pl.BlockSpec((ch, HEAD_DIM, RW), lambda c: (nchunks - 1 - c, 0, 0))
    stp_spec = pl.BlockSpec((1, HEAD_DIM, RW), lambda c: (jnp.maximum((nchunks - 1 - c) * ch - 1, 0), 0, 0))
    return _pcall(
        body, name=name, grid=(nchunks,), in_specs=[vec] * 7 + [st_spec, stp_spec] + [_HBM] * ne,
        out_specs=[vec] * 6 + [_HBM] * ne,
        out_shape=[jax.ShapeDtypeStruct((T, RW), F32)] * 6 + [jax.ShapeDtypeStruct(e.shape, e.dtype) for e in exchange],
        scratch_shapes=[pltpu.VMEM((HEAD_DIM, RW), F32)] + (_ExchangePlan.sems(ne) if ne else []),
        compiler_params=_params(("arbitrary",)),
    )(r, w, k, v, a, b, dy, states, states, *exchange)


_HBM = pl.BlockSpec(memory_space=pltpu.HBM)
_MESH = pl.DeviceIdType.MESH


def _place():
    x, y, c = lax.axis_index("x"), lax.axis_index("y"), lax.axis_index("c")
    return x, y, c, [(1 - x, y), (x, 1 - y), (1 - x, 1 - y)]


def _remote(src, dst, send_sem, recv_sem, to):
    return pltpu.make_async_remote_copy(src_ref=src, dst_ref=dst, send_sem=send_sem, recv_sem=recv_sem, device_id=to,
                                        device_id_type=_MESH)


def _dma_sems(*counts):
    return [pltpu.SemaphoreType.DMA((n,)) for n in counts]


class _GatherPlan:
    def __init__(self, ins, outs, sems):
        self.ins, self.outs, self.n = ins, outs, len(ins)
        self.ici_send, self.ici_recv, self.d2d_send, self.d2d_recv, self.local_sems = sems
        x, y, c, chips = _place()
        self.c, self.me, self.sibling = c, 2 * x + y, (x, y, 1 - c)
        self.peers = [(2 * qx + qy, (qx, qy, c)) for qx, qy in chips]

    @staticmethod
    def sems(n):
        return _dma_sems(3 * n, 3 * n, 3 * n, 3 * n, n)

    def _half(self, i, which):
        rh = self.ins[i].shape[0] // 2
        return pl.ds(which * rh, rh)

    def _local(self, i):
        return pltpu.make_async_copy(self.ins[i], self.outs[i].at[self.me], self.local_sems.at[i])

    def _send(self, i, j):
        k, mine = 3 * i + j, self._half(i, self.c)
        return _remote(self.ins[i].at[mine], self.outs[i].at[self.me, mine], self.ici_send.at[k], self.ici_recv.at[k],
                       self.peers[j][1])

    def _landed(self, i, j):
        k, piece = 3 * i + j, self.outs[i].at[self.peers[j][0], self._half(i, self.c)]
        return _remote(piece, piece, self.ici_send.at[k], self.ici_recv.at[k], self.peers[j][1])

    def _pass(self, i, j, which):
        k, piece = 3 * i + j, self.outs[i].at[self.peers[j][0], self._half(i, which)]
        return _remote(piece, piece, self.d2d_send.at[k], self.d2d_recv.at[k], self.sibling)

    def _all(self):
        return [(i, j) for i in range(self.n) for j in range(3)]

    def start(self):
        for i in range(self.n):
            self._local(i).start()
        for i, j in self._all():
            self._send(i, j).start()

    def relay(self):
        for i, j in self._all():
            self._landed(i, j).wait_recv()
            self._pass(i, j, self.c).start()

    def finish(self):
        for i, j in self._all():
            self._pass(i, j, 1 - self.c).wait_recv()
        for i, j in self._all():
            self._send(i, j).wait_send()
            self._pass(i, j, self.c).wait_send()
        for i in range(self.n):
            self._local(i).wait()


def _gathered_shapes(shards):
    return [jax.ShapeDtypeStruct((N_CHIPS,) + s.shape, s.dtype) for s in shards]


def gather_weights(shards, *, name):
    n = len(shards)

    def body(*refs):
        plan = _GatherPlan(refs[:n], refs[n:2 * n], refs[2 * n:])
        plan.start()
        plan.relay()
        plan.finish()

    return _pcall(body, name=name, in_specs=[_HBM] * n, out_specs=[_HBM] * n, out_shape=_gathered_shapes(shards),
                  scratch_shapes=_GatherPlan.sems(n), compiler_params=_params())(*shards)


def to_sibling(arrays, take_other_half, *, name):
    n = len(arrays)

    def body(*refs):
        ins, outs = refs[:n], refs[n:2 * n]
        send_sems, recv_sems = refs[2 * n:]
        x, y, c, _ = _place()
        cps = []
        for i in range(n):
            src = ins[i]
            if take_other_half:
                rh = src.shape[1] // 2
                src = src.at[:, pl.ds((1 - c) * rh, rh)]
            cps.append(_remote(src, outs[i], send_sems.at[i], recv_sems.at[i], (x, y, 1 - c)))
        for cp in cps:
            cp.start()
        for cp in cps:
            cp.wait_recv()
        for cp in cps:
            cp.wait_send()

    def out_of(a):
        shape = (a.shape[0], a.shape[1] // 2, a.shape[2]) if take_other_half else a.shape
        return jax.ShapeDtypeStruct(shape, a.dtype)

    return _pcall(body, name=name, in_specs=[_HBM] * n, out_specs=[_HBM] * n, out_shape=[out_of(a) for a in arrays],
                  scratch_shapes=_dma_sems(n, n), compiler_params=_params())(*arrays)


def exchange_chips(arrays, *, name):
    n = len(arrays)

    def body(*refs):
        plan = _ExchangePlan(refs[:n], refs[n:2 * n], refs[2 * n:])
        plan.start()
        plan.finish()

    return _pcall(body, name=name, in_specs=[_HBM] * n, out_specs=[_HBM] * n,
                  out_shape=[jax.ShapeDtypeStruct(a.shape, a.dtype) for a in arrays],
                  scratch_shapes=_ExchangePlan.sems(n), compiler_params=_params())(*arrays)


class _ExchangePlan:
    def __init__(self, ins, outs, sems):
        self.ins, self.outs, self.n = ins, outs, len(ins)
        self.send_sems, self.recv_sems, self.local_sems = sems
        x, y, c, chips = _place()
        self.me = 2 * x + y
        self.peers = [(2 * qx + qy, (qx, qy, c)) for qx, qy in chips]

    @staticmethod
    def sems(n):
        return _dma_sems(3 * n, 3 * n, n)

    def _local(self, i):
        return pltpu.make_async_copy(self.ins[i].at[self.me], self.outs[i].at[self.me], self.local_sems.at[i])

    def _send(self, i, j):
        k = 3 * i + j
        return _remote(self.ins[i].at[self.peers[j][0]], self.outs[i].at[self.me], self.send_sems.at[k], self.recv_sems.at[k],
                       self.peers[j][1])

    def _landed(self, i, j):
        k, piece = 3 * i + j, self.outs[i].at[self.peers[j][0]]
        return _remote(piece, piece, self.send_sems.at[k], self.recv_sems.at[k], self.peers[j][1])

    def start(self):
        for i in range(self.n):
            self._local(i).start()
            for j in range(3):
                self._send(i, j).start()

    def finish(self):
        for i in range(self.n):
            for j in range(3):
                self._landed(i, j).wait_recv()
        for i in range(self.n):
            for j in range(3):
                self._send(i, j).wait_send()
            self._local(i).wait()


def _core_index():
    return lax.axis_index("c").astype(jnp.int32).reshape(1)


def pair_sum(g, theirs, wire_dtype, *, name):
    _, R, C = g.shape
    rh = R // 2
    tr = _tile(rh, 256, 16)
    nt = rh // tr

    def body(c_ref, g_ref, t_ref, q_ref, qw_ref):
        q = g_ref[...] + t_ref[...]
        q_ref[...] = q
        qw_ref[...] = q.astype(wire_dtype)

    blk = pl.BlockSpec((1, tr, C), lambda b, i, c_ref: (b, i, 0))
    mine = pl.BlockSpec((1, tr, C), lambda b, i, c_ref: (b, c_ref[0] * nt + i, 0))
    grid_spec = pltpu.PrefetchScalarGridSpec(num_scalar_prefetch=1, grid=(N_CHIPS, nt), in_specs=[mine, blk], out_specs=[blk, blk])
    return _pcall(body, name=name, grid_spec=grid_spec,
                  out_shape=[jax.ShapeDtypeStruct((N_CHIPS, rh, C), F32), jax.ShapeDtypeStruct((N_CHIPS, rh, C), wire_dtype)],
                  compiler_params=_params(("parallel", "parallel")))(_core_index(), g, theirs)


def half_sum(own, landed, *, name):
    _, rh, C = own.shape
    tr = _tile(rh, 256, 16)

    def body(me_ref, own_ref, land_ref, o_ref):
        total = None
        for p in range(N_CHIPS):
            term = jnp.where(me_ref[0] == p, own_ref[p], land_ref[p].astype(F32))
            total = term if total is None else total + term
        o_ref[...] = total

    blk = pl.BlockSpec((N_CHIPS, tr, C), lambda i, me_ref: (0, i, 0))
    grid_spec = pltpu.PrefetchScalarGridSpec(num_scalar_prefetch=1, grid=(rh // tr,), in_specs=[blk, blk],
                                             out_specs=pl.BlockSpec((tr, C), lambda i, me_ref: (i, 0)))
    me = (2 * lax.axis_index("x") + lax.axis_index("y")).astype(jnp.int32).reshape(1)
    return _pcall(body, name=name, grid_spec=grid_spec, out_shape=jax.ShapeDtypeStruct((rh, C), F32),
                  compiler_params=_params(("parallel",)))(me, own, landed)


def adamw(w, m, v, mine, theirs, *, name):
    R, C = w.shape
    rh = R // 2
    tr = _tile(rh, 256, 8)
    nt = rh // tr

    def body(c_ref, w_ref, m_ref, v_ref, a_ref, b_ref, g_ref, d_ref, nm_ref, nv_ref):
        is_mine = (pl.program_id(0) // nt) == c_ref[0]
        g = jnp.where(is_mine, a_ref[...], b_ref[...])
        g_ref[...] = g
        nm = ADAM_B1 * m_ref[...] + (1.0 - ADAM_B1) * g
        nv = ADAM_B2 * v_ref[...] + (1.0 - ADAM_B2) * (g * g)
        nm_ref[...] = nm
        nv_ref[...] = nv
        m_hat = nm / (1.0 - ADAM_B1 ** ADAM_STEP)
        v_hat = nv / (1.0 - ADAM_B2 ** ADAM_STEP)
        d_ref[...] = -ADAM_LR * (m_hat / (jnp.sqrt(v_hat) + ADAM_EPS) + ADAM_WD * w_ref[...])

    full = pl.BlockSpec((tr, C), lambda i, c_ref: (i, 0))
    part = pl.BlockSpec((tr, C), lambda i, c_ref: (i % nt, 0))
    grid_spec = pltpu.PrefetchScalarGridSpec(num_scalar_prefetch=1, grid=(2 * nt,), in_specs=[full] * 3 + [part] * 2,
                                             out_specs=[full] * 4)
    return _pcall(body, name=name, grid_spec=grid_spec, out_shape=[jax.ShapeDtypeStruct((R, C), F32)] * 4,
                  compiler_params=_params(("parallel",)))(_core_index(), w, m, v, mine, theirs)


def _to_blocks(full, axis):
    r, c = full.shape
    if axis == 1:
        return full.reshape(r, N_CHIPS, c // N_CHIPS).transpose(1, 0, 2)
    return full.reshape(N_CHIPS, r // N_CHIPS, c)


def _from_blocks(blocks, axis):
    _, r, c = blocks.shape
    if axis == 1:
        return blocks.transpose(1, 0, 2).reshape(r, N_CHIPS * c)
    return blocks.reshape(N_CHIPS * r, c)


def _ffn_fwd(x, norm, wg, wu, wd, tag):
    h = rms_fwd(x, norm, name=tag + "_norm")
    gate = mm(h, wg, out_dtype=BF16, name=tag + "_gate")
    up = mm(h, wu, out_dtype=BF16, name=tag + "_up")
    act = swiglu_fwd(gate, up, name=tag + "_act")
    out = mm(act, wd, scale=0.5, res=x, name=tag + "_down")
    return out, (h, gate, up, act)


def _ffn_bwd(dout, x, saved, norm, wg, wu, wd, tag):
    h, gate, up, act = saved
    dact = mm(dout, wd, tb=True, scale=0.5, out_dtype=BF16, name=tag + "_dact")
    dwd = mm(act, dout, ta=True, scale=0.5, name=tag + "_dwd")
    dgate, dup = swiglu_bwd(dact, gate, up, name=tag + "_dswiglu")
    dwg = mm(h, dgate, ta=True, name=tag + "_dwg")
    dwu = mm(h, dup, ta=True, name=tag + "_dwu")
    dh = mm(dgate, wg, tb=True, name=tag + "_dh_gate")
    dh = mm(dup, wu, tb=True, res=dh, name=tag + "_dh_up")
    dx, dnorm = rms_bwd(dh, x, norm, dout, name=tag + "_dnorm")
    return dx, dnorm, dwg, dwu, dwd


EARLY_WEIGHTS = ['ffn1_w_gate', 'ffn1_w_up', 'ffn1_w_down', 'w_in', 'rwkv_w_lora_up', 'rwkv_a_lora_up', 'rwkv_g_lora_up']
LATE_WEIGHTS = ['w_branch_rwkv', 'w_branch_attn', 'w_out', 'ffn2_w_gate', 'ffn2_w_up', 'ffn2_w_down']


def _pair_sums(names, blocks, tag):
    from_sibling = to_sibling(blocks, True, name=tag + "_grads_to_sibling")
    return [pair_sum(g, t, F32 if n == 'small' else BF16, name="pair_sum_" + n)
            for g, t, n in zip(blocks, from_sibling, names)]


def _step(A):
    x, tgt = A['x'][0], A['loss_target'][0]
    T = x.shape[0]
    w = {n: A[n][0] for n in WEIGHT_NAMES}
    row = lambda a: a.reshape(1, -1)

    axis_of = dict(BIG)
    shard = lambda n: w[n].astype(BF16)
    gathered = gather_weights([shard(n) for n in EARLY_WEIGHTS], name="gather_weights")
    full = {n: _from_blocks(b, axis_of[n]) for n, b in zip(EARLY_WEIGHTS, gathered)}
    w_in_r = _pad_rwkv_cols(full['w_in'][:, :RWKV_COLS])
    w_in_a = full['w_in'][:, RWKV_COLS:RWKV_COLS + ATT_COLS]
    w_in_g = full['w_in'][:, RWKV_COLS + ATT_COLS:]
    wlw, wla, wlg = (_pad_rows(full[n], 128).astype(F32) for n in ('rwkv_w_lora_up', 'rwkv_a_lora_up', 'rwkv_g_lora_up'))
    mu = _pad_rwkv_cols(row(w['rwkv_mu']))
    w0, a0, k_k, k_a, r_k, ln_w, ln_b = (row(w[n]) for n in ('rwkv_w0', 'rwkv_a0', 'rwkv_k_k', 'rwkv_k_a', 'rwkv_r_k',
                                                               'rwkv_ln_w', 'rwkv_ln_b'))
    qg = jnp.tile(row(w['attn_q_norm']), (1, N_HEADS))
    kg = jnp.tile(row(w['attn_k_norm']), (1, KVW // HEAD_DIM))
    sinks = jnp.pad(row(w['attn_sinks']), ((0, 0), (0, LANE - N_HEADS)))
    n1, nmix, n2, nfin = (row(w[n]) for n in ('ffn1_norm', 'mix_norm', 'ffn2_norm', 'final_norm'))

    x1, ffn1 = _ffn_fwd(x, n1, full['ffn1_w_gate'], full['ffn1_w_up'], full['ffn1_w_down'], "ffn1")
    h2 = rms_fwd(x1, nmix, name="mix_norm")
    pr = mm(h2, w_in_r, name="proj_rwkv")
    pa = mm(h2, w_in_a, name="proj_att")
    pg = mm(h2, w_in_g, name="proj_gate")
    pr_shift = jnp.pad(pr, ((1, 0), (0, 0)))[:-1]
    r, dec, k2, v, a, b, sg = rwkv_pre_fwd(pr, pr_shift, mu, w0, a0, k_k, k_a, wlw, wla, wlg, name="rwkv_pre")
    y, states, *gathered = wkv_fwd(r, dec, k2, v, a, b, name="wkv_fwd", gather=[shard(n) for n in LATE_WEIGHTS])
    full.update({n: _from_blocks(b, axis_of[n]) for n, b in zip(LATE_WEIGHTS, gathered)})
    yr = rwkv_post_fwd(y, r, k2, v, sg, wlg, ln_w, ln_b, r_k, name="rwkv_post")
    ya = att_fwd(pa, qg, kg, sinks, name="att_fwd")
    br = mm(yr, full['w_branch_rwkv'], name="branch_rwkv")
    ba = mm(ya, full['w_branch_attn'], name="branch_att")
    mg = merge_fwd(br, ba, pg, name="merge")
    x2 = mm(mg, full['w_out'], res=x1, name="mix_out")
    x3, ffn2 = _ffn_fwd(x2, n2, full['ffn2_w_gate'], full['ffn2_w_up'], full['ffn2_w_down'], "ffn2")
    dx3, d_nfin, loss = final_loss(x3, tgt, nfin, name="final_loss")

    G = {'final_norm': d_nfin}
    dx2, G['ffn2_norm'], G['ffn2_w_gate'], G['ffn2_w_up'], G['ffn2_w_down'] = _ffn_bwd(
        dx3, x2, ffn2, n2, full['ffn2_w_gate'], full['ffn2_w_up'], full['ffn2_w_down'], "ffn2")
    dmg = mm(dx2, full['w_out'], tb=True, name="d_merge")
    G['w_out'] = mm(mg, dx2, ta=True, name="d_w_out")
    dbr, dba, dpg = merge_bwd(dmg, br, ba, pg, name="merge_bwd")
    dyr = mm(dbr, full['w_branch_rwkv'], tb=True, name="d_y_rwkv")
    G['w_branch_rwkv'] = mm(yr, dbr, ta=True, name="d_w_branch_rwkv")
    dya = mm(dba, full['w_branch_attn'], tb=True, name="d_y_att")
    G['w_branch_attn'] = mm(ya, dba, ta=True, name="d_w_branch_att")
    dy, dz, dg, G['rwkv_ln_w'], G['rwkv_ln_b'] = rwkv_post_bwd(dyr, y, r, k2, v, sg, wlg, ln_w, ln_b, r_k, name="rwkv_post_bwd")
    late_pair = _pair_sums(LATE_WEIGHTS, [_to_blocks(G[n], axis_of[n]) for n in LATE_WEIGHTS], "late")
    res = wkv_bwd(r, dec, k2, v, a, b, dy, states, name="wkv_bwd", exchange=[q for _, q in late_pair])
    wkv_grads, late_landed = res[:6], res[6:]
    (dpr, d_mu, G['rwkv_w0'], G['rwkv_a0'], G['rwkv_k_k'], G['rwkv_k_a'], G['rwkv_r_k'], d_wlw, d_wla, d_wlg) = rwkv_pre_bwd(
        pr, pr_shift, *wkv_grads, dz, dg, mu, w0, a0, k_k, k_a, r_k, wlw, wla, wlg, name="rwkv_pre_bwd")
    G['rwkv_mu'] = _unpad_rwkv_cols(d_mu)
    G['rwkv_w_lora_up'], G['rwkv_a_lora_up'], G['rwkv_g_lora_up'] = d_wlw[:DECAY_LORA], d_wla[:ICLR_LORA], d_wlg[:GATE_LORA]
    dq, dko, dkn, dvo, dvn, G['attn_q_norm'], d_sinks = att_bwd(pa, dya, qg, kg, sinks, name="att_bwd")
    G['attn_sinks'] = d_sinks[:, :N_HEADS]
    dpa, G['attn_k_norm'] = att_kv_bwd(pa, dq, dko, dkn, dvo, dvn, kg, name="att_kv_bwd")
    d_w_in_r = mm(h2, dpr, ta=True, name="d_w_in_rwkv")
    d_w_in_a = mm(h2, dpa, ta=True, name="d_w_in_att")
    d_w_in_g = mm(h2, dpg, ta=True, name="d_w_in_gate")
    G['w_in'] = jnp.concatenate([_unpad_rwkv_cols(d_w_in_r), d_w_in_a, d_w_in_g], axis=1)
    dh2 = mm(dpr, w_in_r, tb=True, name="d_h2_rwkv")
    dh2 = mm(dpa, w_in_a, tb=True, res=dh2, name="d_h2_att")
    dh2 = mm(dpg, w_in_g, tb=True, res=dh2, name="d_h2_gate")
    dx1, G['mix_norm'] = rms_bwd(dh2, x1, nmix, dx2, name="d_mix_norm")
    dx0, G['ffn1_norm'], G['ffn1_w_gate'], G['ffn1_w_up'], G['ffn1_w_down'] = _ffn_bwd(
        dx1, x, ffn1, n1, full['ffn1_w_gate'], full['ffn1_w_up'], full['ffn1_w_down'], "ffn1")

    small_shapes = [(w[n].size,) for n in SMALL] + [(1,)]

    def small_rows(parts):
        vec = jnp.concatenate([p.reshape(-1) for p in parts])
        return jnp.pad(vec, (0, SMALL_ROWS * FLAT_W - vec.shape[0])).reshape(SMALL_ROWS, FLAT_W)

    small = small_rows([G[n] for n in SMALL] + [loss[0, :1]])
    early_names = EARLY_WEIGHTS + ['small']
    early_blocks = [_to_blocks(G[n], axis_of[n]) for n in EARLY_WEIGHTS] + [jnp.broadcast_to(small[None], (N_CHIPS,) + small.shape)]
    early_pair = _pair_sums(early_names, early_blocks, "early")
    early_landed = exchange_chips([q for _, q in early_pair], name="exchange_grads")
    names = early_names + LATE_WEIGHTS
    pair, landed = early_pair + late_pair, list(early_landed) + list(late_landed)
    halves = [half_sum(own, l, name="half_sum_" + n) for (own, _), l, n in zip(pair, landed, names)]
    other_halves = to_sibling(halves, False, name="halves_to_sibling")

    def local(prefix, n):
        if n != 'small':
            return A[prefix + n][0]
        return small_rows([A[prefix + s] for s in SMALL] + [jnp.zeros((1,), F32)])

    result = {}
    for n, mine, theirs in zip(names, halves, other_halves):
        outs4 = adamw(local('', n), local('m_', n), local('v_', n), mine, theirs, name="adamw_" + n)
        for kind, o in zip(('grad_', 'delta_', 'new_m_', 'new_v_'), outs4):
            if n != 'small':
                result[kind + n] = o[None]
            else:
                for s, part in zip(SMALL + ['loss'], _unpack_vec(o.reshape(-1), small_shapes)):
                    result[kind + s] = part.reshape(A[s].shape) if s != 'loss' else part.reshape(())
    outs = [result['grad_loss'], dx0[None]]
    for kind in ('grad_', 'delta_', 'new_m_', 'new_v_'):
        outs += [result[kind + n] for n in WEIGHT_NAMES]
    return tuple(outs)


def _unpack_vec(vec, shapes):
    out, off = [], 0
    for (n,) in shapes:
        out.append(vec[off:off + n])
        off += n
    return out


def kernel(x, ffn1_norm, ffn1_w_gate, ffn1_w_up, ffn1_w_down, mix_norm, w_in, rwkv_mu, rwkv_w0, rwkv_w_lora_up, rwkv_a0, rwkv_a_lora_up, rwkv_g_lora_up, rwkv_k_k, rwkv_k_a, rwkv_r_k, rwkv_ln_w, rwkv_ln_b, attn_q_norm, attn_k_norm, attn_sinks, w_branch_rwkv, w_branch_attn, w_out, ffn2_norm, ffn2_w_gate, ffn2_w_up, ffn2_w_down, final_norm, loss_target, m_ffn1_norm, m_ffn1_w_gate, m_ffn1_w_up, m_ffn1_w_down, m_mix_norm, m_w_in, m_rwkv_mu, m_rwkv_w0, m_rwkv_w_lora_up, m_rwkv_a0, m_rwkv_a_lora_up, m_rwkv_g_lora_up, m_rwkv_k_k, m_rwkv_k_a, m_rwkv_r_k, m_rwkv_ln_w, m_rwkv_ln_b, m_attn_q_norm, m_attn_k_norm, m_attn_sinks, m_w_branch_rwkv, m_w_branch_attn, m_w_out, m_ffn2_norm, m_ffn2_w_gate, m_ffn2_w_up, m_ffn2_w_down, m_final_norm, v_ffn1_norm, v_ffn1_w_gate, v_ffn1_w_up, v_ffn1_w_down, v_mix_norm, v_w_in, v_rwkv_mu, v_rwkv_w0, v_rwkv_w_lora_up, v_rwkv_a0, v_rwkv_a_lora_up, v_rwkv_g_lora_up, v_rwkv_k_k, v_rwkv_k_a, v_rwkv_r_k, v_rwkv_ln_w, v_rwkv_ln_b, v_attn_q_norm, v_attn_k_norm, v_attn_sinks, v_w_branch_rwkv, v_w_branch_attn, v_w_out, v_ffn2_norm, v_ffn2_w_gate, v_ffn2_w_up, v_ffn2_w_down, v_final_norm):
    return _step(dict(locals()))
```

```python
import functools

import jax
import jax.numpy as jnp
from jax import lax
from jax.experimental import pallas as pl
from jax.experimental.pallas import tpu as pltpu

F32 = jnp.float32
BF16 = jnp.bfloat16

D_MODEL = 1024
D_FF = 2816
HEAD_DIM = 64
N_HEADS = 8
RW = 512
KVW = 128
ATT_GROUP = 4
WINDOW = 128
BLOCK = 128
DECAY_LORA, ICLR_LORA, GATE_LORA = 32, 32, 96
RWKV_COLS = 3 * RW + DECAY_LORA + ICLR_LORA + GATE_LORA
ATT_COLS = RW + 2 * KVW
GATE_COLS = 2 * D_MODEL
RWKV_PAD = 3 * RW + 3 * 128
RMS_EPS = 1e-6
GN_EPS = 64e-5
N_CHIPS = 4
LANE = 128
FLAT_W = 1024
SMALL_ROWS = 32
NEG_BIG = -1e30

ADAM_LR, ADAM_B1, ADAM_B2, ADAM_EPS, ADAM_WD, ADAM_STEP = 0.001, 0.9, 0.999, 1e-08, 0.01, 10

VMEM_LIMIT = 56 * 1024 * 1024

WEIGHT_NAMES = ['ffn1_norm', 'ffn1_w_gate', 'ffn1_w_up', 'ffn1_w_down', 'mix_norm', 'w_in', 'rwkv_mu', 'rwkv_w0',
                'rwkv_w_lora_up', 'rwkv_a0', 'rwkv_a_lora_up', 'rwkv_g_lora_up', 'rwkv_k_k', 'rwkv_k_a', 'rwkv_r_k',
                'rwkv_ln_w', 'rwkv_ln_b', 'attn_q_norm', 'attn_k_norm', 'attn_sinks', 'w_branch_rwkv',
                'w_branch_attn', 'w_out', 'ffn2_norm', 'ffn2_w_gate', 'ffn2_w_up', 'ffn2_w_down', 'final_norm']
BIG = [('ffn1_w_gate', 1), ('ffn1_w_up', 1), ('ffn1_w_down', 0), ('w_in', 1), ('rwkv_w_lora_up', 1),
       ('rwkv_a_lora_up', 1), ('rwkv_g_lora_up', 1), ('w_branch_rwkv', 1), ('w_branch_attn', 1), ('w_out', 0),
       ('ffn2_w_gate', 1), ('ffn2_w_up', 1), ('ffn2_w_down', 0)]
SMALL = ['ffn1_norm', 'mix_norm', 'rwkv_mu', 'rwkv_w0', 'rwkv_a0', 'rwkv_k_k', 'rwkv_k_a', 'rwkv_r_k', 'rwkv_ln_w',
         'rwkv_ln_b', 'attn_q_norm', 'attn_k_norm', 'attn_sinks', 'ffn2_norm', 'final_norm']


def _pcall(body, **kw):
    return pl.pallas_call(body, **kw)


def _params(sem=None, **kw):
    if sem is not None:
        kw['dimension_semantics'] = sem
    return pltpu.CompilerParams(vmem_limit_bytes=VMEM_LIMIT, **kw)


def _tile(n, cap, mult):
    best = None
    for t in range(mult, min(n, cap) + 1, mult):
        if n % t == 0:
            best = t
    return best or n


def _sigmoid(z):
    return 1.0 / (1.0 + jnp.exp(-z))


def _softplus(z):
    return jnp.maximum(z, 0.0) + jnp.log(1.0 + jnp.exp(-jnp.abs(z)))


def _bdot(a, b, dims=(((1,), (0,)), ((), ()))):
    return lax.dot_general(a.astype(BF16), b.astype(BF16), dims, preferred_element_type=F32)


_NT = (((1,), (1,)), ((), ()))
_TN = (((0,), (0,)), ((), ()))


def _segsum(x, bd):
    hi = x.astype(BF16)
    r1 = x - hi.astype(F32)
    mid = r1.astype(BF16)
    lo = (r1 - mid.astype(F32)).astype(BF16)
    dot = functools.partial(lax.dot_general, dimension_numbers=(((1,), (0,)), ((), ())), preferred_element_type=F32)
    return dot(hi, bd) + dot(mid, bd) + dot(lo, bd)


_LORA_EDGES = (3 * RW, 3 * RW + DECAY_LORA, 3 * RW + DECAY_LORA + ICLR_LORA, RWKV_COLS)


def _pad_rwkv_cols(x):
    parts = [x[..., :3 * RW]]
    for lo, hi in zip(_LORA_EDGES[:-1], _LORA_EDGES[1:]):
        parts.append(jnp.pad(x[..., lo:hi], [(0, 0)] * (x.ndim - 1) + [(0, 128 - (hi - lo))]))
    return jnp.concatenate(parts, axis=-1)


def _unpad_rwkv_cols(x):
    parts = [x[..., :3 * RW]]
    for j, (lo, hi) in enumerate(zip(_LORA_EDGES[:-1], _LORA_EDGES[1:])):
        parts.append(x[..., 3 * RW + 128 * j:3 * RW + 128 * j + (hi - lo)])
    return jnp.concatenate(parts, axis=-1)


def _pad_rows(x, rows):
    return jnp.pad(x, [(0, rows - x.shape[0])] + [(0, 0)] * (x.ndim - 1))


def mm(a, b, *, name, ta=False, tb=False, scale=None, res=None, out_dtype=F32, carry=None):
    M, K = (a.shape[1], a.shape[0]) if ta else a.shape
    N = b.shape[0] if tb else b.shape[1]
    assert (b.shape[1] if tb else b.shape[0]) == K
    tm, tn, tk = _tile(M, 1408 if ta else 512, 128), _tile(N, 1408, 128), _tile(K, 1408, 128)
    nk = K // tk
    grid = (M // tm, N // tn, nk)
    dims = (((0 if ta else 1,), (1 if tb else 0,)), ((), ()))
    plan_cls, carried = carry if carry else (None, ())
    nc, nin = len(carried), 2 + (res is not None)

    def body(*refs):
        a_ref, b_ref = refs[:2]
        r_ref = refs[2] if res is not None else None
        o_ref, acc_ref = refs[nin + nc], refs[nin + 2 * nc + 1]
        k = pl.program_id(2)
        if nc:
            plan = plan_cls(refs[nin:nin + nc], refs[nin + nc + 1:nin + 2 * nc + 1], refs[nin + 2 * nc + 2:])
            at = lambda which: functools.reduce(jnp.logical_and, [pl.program_id(d) == (0 if which == 0 else grid[d] - 1)
                                                                 for d in range(3)])
            pl.when(at(0))(plan.start)
        part = _bdot(a_ref[...], b_ref[...], dims)

        @pl.when(k == 0)
        def _():
            acc_ref[...] = part

        @pl.when(k > 0)
        def _():
            acc_ref[...] += part

        @pl.when(k == nk - 1)
        def _():
            o = acc_ref[...]
            if scale is not None:
                o = o * scale
            if r_ref is not None:
                o = o + r_ref[...].astype(F32)
            o_ref[...] = o.astype(out_dtype)

        if nc:
            pl.when(at(1))(plan.finish)

    a_spec = pl.BlockSpec((tk, tm), lambda i, j, k: (k, i)) if ta else pl.BlockSpec((tm, tk), lambda i, j, k: (i, k))
    b_spec = pl.BlockSpec((tn, tk), lambda i, j, k: (j, k)) if tb else pl.BlockSpec((tk, tn), lambda i, j, k: (k, j))
    o_spec = pl.BlockSpec((tm, tn), lambda i, j, k: (i, j))
    in_specs = [a_spec, b_spec] + ([o_spec] if res is not None else [])
    args = (a, b) + ((res,) if res is not None else ())
    out_shape = jax.ShapeDtypeStruct((M, N), out_dtype)
    if not nc:
        return _pcall(
            body, name=name, grid=grid, in_specs=in_specs, out_specs=o_spec, out_shape=out_shape,
            scratch_shapes=[pltpu.VMEM((tm, tn), F32)], compiler_params=_params(("parallel", "parallel", "arbitrary")),
        )(*args)
    return _pcall(
        body, name=name, grid=grid, in_specs=in_specs + [_HBM] * nc, out_specs=[o_spec] + [_HBM] * nc,
        out_shape=[out_shape] + plan_cls.out_shapes(carried), scratch_shapes=[pltpu.VMEM((tm, tn), F32)] + plan_cls.sems(nc),
        compiler_params=_params(("arbitrary", "arbitrary", "arbitrary")),
    )(*args, *carried)


def _row_spec(tr, c):
    return pl.BlockSpec((tr, c), lambda i: (i, 0))


def _full_spec(shape):
    return pl.BlockSpec(shape, lambda i: (0,) * len(shape))


def _acc_rows(ref, val, i):
    @pl.when(i == 0)
    def _():
        ref[...] = val

    @pl.when(i > 0)
    def _():
        ref[...] += val


def rms_fwd(x, g, *, name):
    T, D = x.shape
    tr = _tile(T, 512, 8)

    def body(x_ref, g_ref, h_ref):
        xv = x_ref[...]
        r = lax.rsqrt(jnp.mean(xv * xv, axis=-1, keepdims=True) + RMS_EPS)
        h_ref[...] = (xv * r * g_ref[...]).astype(BF16)

    return _pcall(body, name=name, grid=(T // tr,), in_specs=[_row_spec(tr, D), _full_spec((1, D))],
                  out_specs=_row_spec(tr, D), out_shape=jax.ShapeDtypeStruct((T, D), BF16),
                  compiler_params=_params(("parallel",)))(x, g)


def rms_bwd(dh, x, g, res, *, name):
    T, D = x.shape
    tr = _tile(T, 256, 8)

    def body(dh_ref, x_ref, g_ref, res_ref, dx_ref, dg_ref):
        i = pl.program_id(0)
        xv, dhv = x_ref[...], dh_ref[...].astype(F32)
        r = lax.rsqrt(jnp.mean(xv * xv, axis=-1, keepdims=True) + RMS_EPS)
        xh = xv * r
        dxh = dhv * g_ref[...]
        dx_ref[...] = res_ref[...] + r * (dxh - xh * jnp.mean(dxh * xh, axis=-1, keepdims=True))
        _acc_rows(dg_ref, jnp.sum(dhv * xh, axis=0, keepdims=True), i)

    return _pcall(body, name=name, grid=(T // tr,),
                  in_specs=[_row_spec(tr, D), _row_spec(tr, D), _full_spec((1, D)), _row_spec(tr, D)],
                  out_specs=[_row_spec(tr, D), _full_spec((1, D))],
                  out_shape=[jax.ShapeDtypeStruct((T, D), F32), jax.ShapeDtypeStruct((1, D), F32)],
                  compiler_params=_params(("arbitrary",)))(dh, x, g, res)


def final_loss(x, tgt, g, *, name):
    T, D = x.shape
    tr = _tile(T, 256, 8)

    def body(x_ref, t_ref, g_ref, dx_ref, dg_ref, loss_ref):
        i = pl.program_id(0)
        xv = x_ref[...]
        r = lax.rsqrt(jnp.mean(xv * xv, axis=-1, keepdims=True) + RMS_EPS)
        xh = xv * r
        e = xh * g_ref[...] - t_ref[...]
        part = 0.5 * jnp.sum(jnp.mean(e * e, axis=-1, keepdims=True), axis=0, keepdims=True)
        dy = e * (1.0 / D)
        dxh = dy * g_ref[...]
        dx_ref[...] = r * (dxh - xh * jnp.mean(dxh * xh, axis=-1, keepdims=True))
        _acc_rows(dg_ref, jnp.sum(dy * xh, axis=0, keepdims=True), i)
        _acc_rows(loss_ref, jnp.broadcast_to(part, (1, LANE)), i)

    return _pcall(body, name=name, grid=(T // tr,),
                  in_specs=[_row_spec(tr, D), _row_spec(tr, D), _full_spec((1, D))],
                  out_specs=[_row_spec(tr, D), _full_spec((1, D)), _full_spec((1, LANE))],
                  out_shape=[jax.ShapeDtypeStruct((T, D), F32), jax.ShapeDtypeStruct((1, D), F32),
                             jax.ShapeDtypeStruct((1, LANE), F32)],
                  compiler_params=_params(("arbitrary",)))(x, tgt, g)


def swiglu_fwd(gate, up, *, name):
    T, F = gate.shape
    tr = _tile(T, 256, 8)

    def body(g_ref, u_ref, a_ref):
        gv = g_ref[...].astype(F32)
        a_ref[...] = (gv * _sigmoid(gv) * u_ref[...].astype(F32)).astype(BF16)

    return _pcall(body, name=name, grid=(T // tr,), in_specs=[_row_spec(tr, F), _row_spec(tr, F)],
                  out_specs=_row_spec(tr, F), out_shape=jax.ShapeDtypeStruct((T, F), BF16),
                  compiler_params=_params(("parallel",)))(gate, up)


def swiglu_bwd(da, gate, up, *, name):
    T, F = gate.shape
    tr = _tile(T, 256, 8)

    def body(da_ref, g_ref, u_ref, dg_ref, du_ref):
        gv, dav = g_ref[...].astype(F32), da_ref[...].astype(F32)
        s = _sigmoid(gv)
        du_ref[...] = (dav * gv * s).astype(BF16)
        dg_ref[...] = (dav * u_ref[...].astype(F32) * s * (1.0 + gv * (1.0 - s))).astype(BF16)

    return _pcall(body, name=name, grid=(T // tr,), in_specs=[_row_spec(tr, F)] * 3,
                  out_specs=[_row_spec(tr, F)] * 2, out_shape=[jax.ShapeDtypeStruct((T, F), BF16)] * 2,
                  compiler_params=_params(("parallel",)))(da, gate, up)


def merge_fwd(br, ba, pg, *, name):
    T, D = br.shape
    tr = _tile(T, 256, 8)

    def body(br_ref, ba_ref, pg_ref, o_ref):
        pgv = pg_ref[...]
        o_ref[...] = (_sigmoid(pgv[:, :D]) * br_ref[...] + _sigmoid(pgv[:, D:]) * ba_ref[...]).astype(BF16)

    return _pcall(body, name=name, grid=(T // tr,), in_specs=[_row_spec(tr, D), _row_spec(tr, D), _row_spec(tr, 2 * D)],
                  out_specs=_row_spec(tr, D), out_shape=jax.ShapeDtypeStruct((T, D), BF16),
                  compiler_params=_params(("parallel",)))(br, ba, pg)


def merge_bwd(dm, br, ba, pg, *, name):
    T, D = br.shape
    tr = _tile(T, 256, 8)

    def body(dm_ref, br_ref, ba_ref, pg_ref, dbr_ref, dba_ref, dpg_ref):
        pgv, dmv = pg_ref[...], dm_ref[...]
        sr, sa = _sigmoid(pgv[:, :D]), _sigmoid(pgv[:, D:])
        dbr_ref[...] = (dmv * sr).astype(BF16)
        dba_ref[...] = (dmv * sa).astype(BF16)
        dpg_ref[:, :D] = dmv * br_ref[...] * sr * (1.0 - sr)
        dpg_ref[:, D:] = dmv * ba_ref[...] * sa * (1.0 - sa)

    return _pcall(body, name=name, grid=(T // tr,),
                  in_specs=[_row_spec(tr, D), _row_spec(tr, D), _row_spec(tr, D), _row_spec(tr, 2 * D)],
                  out_specs=[_row_spec(tr, D), _row_spec(tr, D), _row_spec(tr, 2 * D)],
                  out_shape=[jax.ShapeDtypeStruct((T, D), BF16), jax.ShapeDtypeStruct((T, D), BF16),
                             jax.ShapeDtypeStruct((T, 2 * D), F32)],
                  compiler_params=_params(("parallel",)))(dm, br, ba, pg)


def _rwkv_mix(p, prev, mu, w0, a0, k_k, k_a, wlw, wla, wlg, bd):
    pp = p + (prev - p) * mu
    r, k, v = pp[:, 0:RW], pp[:, RW:2 * RW], pp[:, 2 * RW:3 * RW]
    xw, xa, xg = pp[:, 3 * RW:3 * RW + 128], pp[:, 3 * RW + 128:3 * RW + 256], pp[:, 3 * RW + 256:3 * RW + 384]
    th = jnp.tanh(xw)
    z = -(w0 + _bdot(th, wlw))
    e = jnp.exp(-_softplus(z) - 0.5)
    decay = jnp.exp(-e)
    a = _sigmoid(a0 + _bdot(xa, wla))
    sg = _sigmoid(xg)
    kkr = k * k_k
    n = jnp.sqrt(_segsum(kkr * kkr, bd))
    kk = kkr / jnp.maximum(n, 1e-12)
    k2 = k * (1.0 + (a - 1.0) * k_a)
    return dict(r=r, k=k, v=v, xa=xa, th=th, z=z, e=e, decay=decay, a=a, sg=sg, n=n, kk=kk, k2=k2)


def _seg_matrix(n, shift):
    r = lax.shift_right_logical(lax.broadcasted_iota(jnp.int32, (n, n), 0), shift)
    c = lax.shift_right_logical(lax.broadcasted_iota(jnp.int32, (n, n), 1), shift)
    return jnp.where(r == c, 1.0, 0.0).astype(BF16)


def rwkv_pre_fwd(p, pshift, mu, w0, a0, k_k, k_a, wlw, wla, wlg, *, name):
    T = p.shape[0]
    tr = _tile(T, 256, 8)

    def body(p_ref, ps_ref, mu_ref, w0_ref, a0_ref, kk_ref, ka_ref, wlw_ref, wla_ref, wlg_ref,
             r_ref, w_ref, k_ref, v_ref, a_ref, b_ref, g_ref):
        pv, prev = p_ref[...], ps_ref[...]
        m = _rwkv_mix(pv, prev, mu_ref[...], w0_ref[...], a0_ref[...], kk_ref[...], ka_ref[...],
                      wlw_ref[...], wla_ref[...], wlg_ref[...], _seg_matrix(RW, 6))
        r_ref[...] = m['r']
        w_ref[...] = m['decay']
        k_ref[...] = m['k2']
        v_ref[...] = m['v']
        a_ref[...] = -m['kk']
        b_ref[...] = m['kk'] * m['a']
        g_ref[...] = m['sg']

    vec = _row_spec(tr, RW)
    return _pcall(
        body, name=name, grid=(T // tr,),
        in_specs=[_row_spec(tr, RWKV_PAD), _row_spec(tr, RWKV_PAD), _full_spec((1, RWKV_PAD))] + [_full_spec((1, RW))] * 4
        + [_full_spec((128, RW))] * 3,
        out_specs=[vec] * 6 + [_row_spec(tr, 128)],
        out_shape=[jax.ShapeDtypeStruct((T, RW), F32)] * 6 + [jax.ShapeDtypeStruct((T, 128), F32)],
        compiler_params=_params(("parallel",)),
    )(p, pshift, mu, w0, a0, k_k, k_a, wlw, wla, wlg)


def _group_norm(y, bd):
    mean = _segsum(y, bd) * (1.0 / HEAD_DIM)
    yc = y - mean
    rstd = lax.rsqrt(_segsum(yc * yc, bd) * (1.0 / HEAD_DIM) + GN_EPS)
    return yc * rstd, rstd


def rwkv_post_fwd(y, r, k2, v, sg, wlg, ln_w, ln_b, r_k, *, name):
    T = y.shape[0]
    tr = _tile(T, 256, 8)

    def body(y_ref, r_ref, k_ref, v_ref, sg_ref, wlg_ref, lw_ref, lb_ref, rk_ref, o_ref):
        bd = _seg_matrix(RW, 6)
        yn, _ = _group_norm(y_ref[...], bd)
        s = _segsum(r_ref[...] * k_ref[...] * rk_ref[...], bd)
        g = _bdot(sg_ref[...], wlg_ref[...])
        o_ref[...] = ((yn * lw_ref[...] + lb_ref[...] + s * v_ref[...]) * g).astype(BF16)

    vec = _row_spec(tr, RW)
    return _pcall(body, name=name, grid=(T // tr,),
                  in_specs=[vec] * 4 + [_row_spec(tr, 128), _full_spec((128, RW))] + [_full_spec((1, RW))] * 3, out_specs=vec,
                  out_shape=jax.ShapeDtypeStruct((T, RW), BF16), compiler_params=_params(("parallel",)))(
                      y, r, k2, v, sg, wlg, ln_w, ln_b, r_k)


def rwkv_post_bwd(dyr, y, r, k2, v, sg, wlg, ln_w, ln_b, r_k, *, name):
    T = y.shape[0]
    tr = _tile(T, 256, 8)

    def body(dyr_ref, y_ref, r_ref, k_ref, v_ref, sg_ref, wlg_ref, lw_ref, lb_ref, rk_ref,
             dy_ref, dz_ref, dg_ref, dlw_ref, dlb_ref):
        i = pl.program_id(0)
        bd = _seg_matrix(RW, 6)
        yn, rstd = _group_norm(y_ref[...], bd)
        s = _segsum(r_ref[...] * k_ref[...] * rk_ref[...], bd)
        dyrv = dyr_ref[...]
        dg_ref[...] = dyrv * (yn * lw_ref[...] + lb_ref[...] + s * v_ref[...])
        dz = dyrv * _bdot(sg_ref[...], wlg_ref[...])
        dz_ref[...] = dz
        dyn = dz * lw_ref[...]
        inv = 1.0 / HEAD_DIM
        dy_ref[...] = rstd * (dyn - _segsum(dyn, bd) * inv - yn * (_segsum(dyn * yn, bd) * inv))
        _acc_rows(dlw_ref, jnp.sum(dz * yn, axis=0, keepdims=True), i)
        _acc_rows(dlb_ref, jnp.sum(dz, axis=0, keepdims=True), i)

    vec = _row_spec(tr, RW)
    one = _full_spec((1, RW))
    return _pcall(body, name=name, grid=(T // tr,),
                  in_specs=[vec] * 5 + [_row_spec(tr, 128), _full_spec((128, RW))] + [one] * 3, out_specs=[vec] * 3 + [one] * 2,
                  out_shape=[jax.ShapeDtypeStruct((T, RW), F32)] * 3 + [jax.ShapeDtypeStruct((1, RW), F32)] * 2,
                  compiler_params=_params(("arbitrary",)))(dyr, y, r, k2, v, sg, wlg, ln_w, ln_b, r_k)


def rwkv_pre_bwd(p, pshift, dr_w, dw_w, dk_w, dv_w, da_w, db_w, dz, dg, mu, w0, a0, k_k, k_a, r_k, wlw, wla, wlg, *, name):
    T = p.shape[0]
    tr = _tile(T, 256, 8)
    n = T // tr

    def body(p_ref, ps_ref, dr_ref, dw_ref, dk_ref, dv_ref, da_ref, db_ref, dz_ref, dg_ref,
             mu_ref, w0_ref, a0_ref, kk_ref, ka_ref, rk_ref, wlw_ref, wla_ref, wlg_ref,
             dp_ref, dmu_ref, dw0_ref, da0_ref, dkk_ref, dka_ref, drk_ref, dwlw_ref, dwla_ref, dwlg_ref,
             carry, dpp, acc_w, acc_a, acc_g):
        i = pl.program_id(0)

        @pl.when(i == 0)
        def _():
            carry[...] = jnp.zeros_like(carry)

        pv, prev, mu = p_ref[...], ps_ref[...], mu_ref[...]
        bd = _seg_matrix(RW, 6)
        k_k, k_a, r_k = kk_ref[...], ka_ref[...], rk_ref[...]
        m = _rwkv_mix(pv, prev, mu, w0_ref[...], a0_ref[...], k_k, k_a, wlw_ref[...], wla_ref[...], wlg_ref[...], bd)
        r, k, v, a, kk, k2 = m['r'], m['k'], m['v'], m['a'], m['kk'], m['k2']
        dzv, dgv = dz_ref[...], dg_ref[...]
        s = _segsum(r * k2 * r_k, bd)
        ds = _segsum(dzv * v, bd)
        dr = dr_ref[...] + ds * k2 * r_k
        dk2 = dk_ref[...] + ds * r * r_k
        dv = dv_ref[...] + dzv * s
        dbv = db_ref[...]
        dkk = dbv * a - da_ref[...]
        da = dbv * kk + dk2 * k * k_a
        dk = dk2 * (1.0 + (a - 1.0) * k_a)
        nmax = jnp.maximum(m['n'], 1e-12)
        dkkr = jnp.where(m['n'] > 1e-12, dkk - kk * _segsum(dkk * kk, bd), dkk) / nmax
        dk = dk + dkkr * k_k
        dapre = da * a * (1.0 - a)
        dwpre = dw_ref[...] * m['decay'] * (-m['e']) * _sigmoid(m['z'])
        dth = _bdot(dwpre, wlw_ref[...], _NT)
        dxa = _bdot(dapre, wla_ref[...], _NT)
        dsg = _bdot(dgv, wlg_ref[...], _NT)
        dpp[:, 0:RW] = dr
        dpp[:, RW:2 * RW] = dk
        dpp[:, 2 * RW:3 * RW] = dv
        dpp[:, 3 * RW:3 * RW + 128] = dth * (1.0 - m['th'] * m['th'])
        dpp[:, 3 * RW + 128:3 * RW + 256] = dxa
        dpp[:, 3 * RW + 256:3 * RW + 384] = dsg * m['sg'] * (1.0 - m['sg'])
        d = dpp[...]
        zed = d * mu
        last = lax.broadcasted_iota(jnp.int32, pv.shape, 0) == tr - 1
        dp_ref[...] = d * (1.0 - mu) + jnp.where(last, carry[0:1, :], pltpu.roll(zed, tr - 1, 0))
        carry[...] = zed[0:8, :]

        def colsum(x):
            return jnp.sum(x, axis=0, keepdims=True)

        _acc_rows(dmu_ref, colsum(d * (prev - pv)), i)
        _acc_rows(dw0_ref, colsum(dwpre), i)
        _acc_rows(da0_ref, colsum(dapre), i)
        _acc_rows(dkk_ref, colsum(dkkr * k), i)
        _acc_rows(dka_ref, colsum(dk2 * k * (a - 1.0)), i)
        _acc_rows(drk_ref, colsum(ds * r * k2), i)
        _acc_rows(acc_w, _bdot(m['th'], dwpre, _TN), i)
        _acc_rows(acc_a, _bdot(m['xa'], dapre, _TN), i)
        _acc_rows(acc_g, _bdot(m['sg'], dgv, _TN), i)

        @pl.when(i == n - 1)
        def _():
            dwlw_ref[...] = acc_w[...]
            dwla_ref[...] = acc_a[...]
            dwlg_ref[...] = acc_g[...]

    rev = lambda c: pl.BlockSpec((tr, c), lambda i: (n - 1 - i, 0))
    one, lora = _full_spec((1, RW)), _full_spec((128, RW))
    return _pcall(
        body, name=name, grid=(n,),
        in_specs=[rev(RWKV_PAD), rev(RWKV_PAD)] + [rev(RW)] * 8 + [_full_spec((1, RWKV_PAD))] + [one] * 5 + [lora] * 3,
        out_specs=[rev(RWKV_PAD), _full_spec((1, RWKV_PAD))] + [one] * 5 + [lora] * 3,
        out_shape=[jax.ShapeDtypeStruct((T, RWKV_PAD), F32), jax.ShapeDtypeStruct((1, RWKV_PAD), F32)]
        + [jax.ShapeDtypeStruct((1, RW), F32)] * 5 + [jax.ShapeDtypeStruct((128, RW), F32)] * 3,
        scratch_shapes=[pltpu.VMEM((8, RWKV_PAD), F32), pltpu.VMEM((tr, RWKV_PAD), F32)] + [pltpu.VMEM((128, RW), F32)] * 3,
        compiler_params=_params(("arbitrary",)),
    )(p, pshift, dr_w, dw_w, dk_w, dv_w, da_w, db_w, dz, dg, mu, w0, a0, k_k, k_a, r_k, wlw, wla, wlg)


def _qk_norm(x, g, bd):
    r = lax.rsqrt(_segsum(x * x, bd) * (1.0 / HEAD_DIM) + RMS_EPS)
    return x * r * g, r


def _att_mask(i):
    qi = lax.broadcasted_iota(jnp.int32, (BLOCK, 2 * BLOCK), 0)
    kj = lax.broadcasted_iota(jnp.int32, (BLOCK, 2 * BLOCK), 1)
    band = (kj <= qi + BLOCK) & (kj > qi + BLOCK - WINDOW)
    return band & ((kj >= BLOCK) | (i > 0))


_HQK = (((2,), (2,)), ((0,), (0,)))
_HPV = (((2,), (1,)), ((0,), (0,)))
_HTN = (((1,), (1,)), ((0,), (0,)))


def _heads(x, n):
    return jnp.stack([x[:, h * HEAD_DIM:(h + 1) * HEAD_DIM] for h in range(n)])


def _unheads(x3):
    return jnp.concatenate([x3[h] for h in range(x3.shape[0])], axis=1)


def _kv_heads(x):
    x2 = _heads(x, KVW // HEAD_DIM)
    return jnp.concatenate([x2[g:g + 1] for g in range(KVW // HEAD_DIM) for _ in range(ATT_GROUP)], axis=0)


def _sinks3(sk):
    return jnp.stack([sk[0:1, h:h + 1] for h in range(N_HEADS)])


def _att_probs(q3, k3, mask, sink):
    s = _bdot(q3, k3, _HQK) * (HEAD_DIM ** -0.5)
    s = jnp.where(mask[None], s, NEG_BIG)
    m = jnp.maximum(jnp.max(s, axis=-1, keepdims=True), sink)
    pexp = jnp.exp(s - m)
    psink = jnp.exp(sink - m)
    inv = 1.0 / (jnp.sum(pexp, axis=-1, keepdims=True) + psink)
    return pexp * inv, psink * inv


def _att_blocks(n):
    cur = pl.BlockSpec((BLOCK, ATT_COLS), lambda i: (i, 0))
    prev = pl.BlockSpec((BLOCK, ATT_COLS), lambda i: (jnp.maximum(i - 1, 0), 0))
    return cur, prev


def _att_qkv(cur, prev, qn_g, kn_g):
    bq, bk = _seg_matrix(RW, 6), _seg_matrix(KVW, 6)
    qn, rq = _qk_norm(cur[:, 0:RW], qn_g, bq)
    kcur, rkc = _qk_norm(cur[:, RW:RW + KVW], kn_g, bk)
    kprev, _ = _qk_norm(prev[:, RW:RW + KVW], kn_g, bk)
    kc = jnp.concatenate([kprev, kcur], axis=0)
    vc = jnp.concatenate([prev[:, RW + KVW:], cur[:, RW + KVW:]], axis=0)
    return qn, rq, kc, vc, rkc


def att_fwd(pa, qn_g, kn_g, sinks, *, name):
    T = pa.shape[0]
    n = T // BLOCK

    def body(cur_ref, prev_ref, qg_ref, kg_ref, sk_ref, o_ref):
        i = pl.program_id(0)
        qn, _, kc, vc, _ = _att_qkv(cur_ref[...], prev_ref[...], qg_ref[...], kg_ref[...])
        probs, _ = _att_probs(_heads(qn, N_HEADS), _kv_heads(kc), _att_mask(i), _sinks3(sk_ref[...]))
        o_ref[...] = _unheads(_bdot(probs, _kv_heads(vc), _HPV))

    cur, prev = _att_blocks(n)
    return _pcall(body, name=name, grid=(n,),
                  in_specs=[cur, prev, _full_spec((1, RW)), _full_spec((1, KVW)), _full_spec((1, LANE))],
                  out_specs=pl.BlockSpec((BLOCK, RW), lambda i: (i, 0)), out_shape=jax.ShapeDtypeStruct((T, RW), F32),
                  compiler_params=_params(("parallel",)))(pa, pa, qn_g, kn_g, sinks)


def att_bwd(pa, do, qn_g, kn_g, sinks, *, name):
    T = pa.shape[0]
    n = T // BLOCK

    def body(cur_ref, prev_ref, do_ref, qg_ref, kg_ref, sk_ref,
             dq_ref, dko_ref, dkn_ref, dvo_ref, dvn_ref, dqg_ref, dsk_ref):
        i = pl.program_id(0)
        cur = cur_ref[...]
        qn, rq, kc, vc, _ = _att_qkv(cur, prev_ref[...], qg_ref[...], kg_ref[...])
        q3, k3, v3, do3 = _heads(qn, N_HEADS), _kv_heads(kc), _kv_heads(vc), _heads(do_ref[...], N_HEADS)
        probs, psink = _att_probs(q3, k3, _att_mask(i), _sinks3(sk_ref[...]))
        dprobs = _bdot(do3, v3, _HQK)
        delta = jnp.sum(probs * dprobs, axis=-1, keepdims=True)
        ds = probs * (dprobs - delta) * (HEAD_DIM ** -0.5)
        dsink3 = -jnp.sum(psink * delta, axis=1, keepdims=True)
        lane = lax.broadcasted_iota(jnp.int32, (1, LANE), 1)
        dsink = jnp.zeros((1, LANE), F32)
        for h in range(N_HEADS):
            dsink = dsink + jnp.where(lane == h, dsink3[h], 0.0)
        dqn = _unheads(_bdot(ds, k3, _HPV))

        def per_kv_head(x3):
            groups = [sum(x3[g * ATT_GROUP + j] for j in range(ATT_GROUP)) for g in range(KVW // HEAD_DIM)]
            return jnp.concatenate(groups, axis=1)

        dk, dv = per_kv_head(_bdot(ds, q3, _HTN)), per_kv_head(_bdot(probs, do3, _HTN))
        dkn_ref[...], dko_ref[...] = dk[0:BLOCK], dk[BLOCK:]
        dvn_ref[...], dvo_ref[...] = dv[0:BLOCK], dv[BLOCK:]
        qhat = cur[:, 0:RW] * rq
        dqh = dqn * qg_ref[...]
        dq_ref[...] = rq * (dqh - qhat * (_segsum(dqh * qhat, _seg_matrix(RW, 6)) * (1.0 / HEAD_DIM)))
        prod = dqn * qhat
        fold = prod[:, 0:HEAD_DIM]
        for h in range(1, N_HEADS):
            fold = fold + prod[:, h * HEAD_DIM:(h + 1) * HEAD_DIM]
        _acc_rows(dqg_ref, jnp.sum(fold, axis=0, keepdims=True), i)
        _acc_rows(dsk_ref, dsink, i)

    cur, prev = _att_blocks(n)
    kvb = pl.BlockSpec((BLOCK, KVW), lambda i: (i, 0))
    qb = pl.BlockSpec((BLOCK, RW), lambda i: (i, 0))
    return _pcall(body, name=name, grid=(n,),
                  in_specs=[cur, prev, qb, _full_spec((1, RW)), _full_spec((1, KVW)), _full_spec((1, LANE))],
                  out_specs=[qb, kvb, kvb, kvb, kvb, _full_spec((1, HEAD_DIM)), _full_spec((1, LANE))],
                  out_shape=[jax.ShapeDtypeStruct((T, RW), F32)] + [jax.ShapeDtypeStruct((T, KVW), F32)] * 4
                  + [jax.ShapeDtypeStruct((1, HEAD_DIM), F32), jax.ShapeDtypeStruct((1, LANE), F32)],
                  compiler_params=_params(("arbitrary",)))(pa, pa, do, qn_g, kn_g, sinks)


def att_kv_bwd(pa, dq, dko, dkn, dvo, dvn, kn_g, *, name):
    T = pa.shape[0]
    n = T // BLOCK

    def body(pa_ref, dq_ref, dko_ref, dkn_ref, dvo_ref, dvn_ref, kg_ref, dpa_ref, dkg_ref):
        i = pl.program_id(0)
        more = i < n - 1
        dkn_tot = dko_ref[...] + jnp.where(more, dkn_ref[...], 0.0)
        dv_tot = dvo_ref[...] + jnp.where(more, dvn_ref[...], 0.0)
        kraw = pa_ref[:, RW:RW + KVW]
        bk = _seg_matrix(KVW, 6)
        _, rk = _qk_norm(kraw, kg_ref[...], bk)
        khat = kraw * rk
        dkh = dkn_tot * kg_ref[...]
        dpa_ref[:, 0:RW] = dq_ref[...]
        dpa_ref[:, RW:RW + KVW] = rk * (dkh - khat * (_segsum(dkh * khat, bk) * (1.0 / HEAD_DIM)))
        dpa_ref[:, RW + KVW:] = dv_tot
        prod = dkn_tot * khat
        _acc_rows(dkg_ref, jnp.sum(prod[:, 0:HEAD_DIM] + prod[:, HEAD_DIM:], axis=0, keepdims=True), i)

    kvb = pl.BlockSpec((BLOCK, KVW), lambda i: (i, 0))
    nxt = pl.BlockSpec((BLOCK, KVW), lambda i: (jnp.minimum(i + 1, n - 1), 0))
    return _pcall(body, name=name, grid=(n,),
                  in_specs=[pl.BlockSpec((BLOCK, ATT_COLS), lambda i: (i, 0)), pl.BlockSpec((BLOCK, RW), lambda i: (i, 0)),
                            kvb, nxt, kvb, nxt, _full_spec((1, KVW))],
                  out_specs=[pl.BlockSpec((BLOCK, ATT_COLS), lambda i: (i, 0)), _full_spec((1, HEAD_DIM))],
                  out_shape=[jax.ShapeDtypeStruct((T, ATT_COLS), F32), jax.ShapeDtypeStruct((1, HEAD_DIM), F32)],
                  compiler_params=_params(("arbitrary",)))(pa, dq, dko, dkn, dvo, dvn, kn_g)


WKV_CHUNK = 64
WKV_GROUP = 8


def _diag_mask():
    i = lax.broadcasted_iota(jnp.int32, (HEAD_DIM, RW), 0)
    j = lax.broadcasted_iota(jnp.int32, (HEAD_DIM, RW), 1) & (HEAD_DIM - 1)
    return i == j


def _headsums(xs, pieces):
    half = RW // 2
    bd = _seg_matrix(half, 6)
    bd = jnp.concatenate([bd] * pieces, axis=0)
    rows = []
    for x in xs:
        parts, rest = [], x
        for n in range(pieces):
            p = rest.astype(BF16)
            parts.append(p)
            if n + 1 < pieces:
                rest = rest - p.astype(F32)
        for sl in (slice(0, half), slice(half, RW)):
            rows.append(jnp.concatenate([p[:, sl] for p in parts], axis=1))
    out = lax.dot_general(jnp.concatenate(rows, axis=0), bd, (((1,), (0,)), ((), ())), preferred_element_type=F32)
    return [jnp.concatenate([out[2 * n * HEAD_DIM:(2 * n + 1) * HEAD_DIM], out[(2 * n + 1) * HEAD_DIM:(2 * n + 2) * HEAD_DIM]],
                            axis=1) for n in range(len(xs))]


def _headsum(x):
    low = lax.broadcasted_iota(jnp.int32, (HEAD_DIM, LANE), 1) < HEAD_DIM
    tiles = []
    for c in range(RW // LANE):
        xt = x[:, c * LANE:(c + 1) * LANE]
        s_lo = jnp.sum(jnp.where(low, xt, 0.0), axis=1, keepdims=True)
        s_hi = jnp.sum(jnp.where(low, 0.0, xt), axis=1, keepdims=True)
        tiles.append(jnp.where(low, s_lo, s_hi))
    return jnp.concatenate(tiles, axis=1)


def _cols(rows, diag):
    return _headsums([jnp.where(diag, r, 0.0) for r in rows], 2)


def _row(x, diag):
    return jnp.sum(jnp.where(diag, x, 0.0), axis=0, keepdims=True)


def wkv_fwd(r, w, k, v, a, b, *, name, gather=()):
    T = r.shape[0]
    ch = min(WKV_CHUNK, T)
    ngroups = ch // WKV_GROUP
    nchunks = T // ch
    ng = len(gather)

    def body(*refs):
        r_ref, w_ref, k_ref, v_ref, a_ref, b_ref = refs[:6]
        y_ref, st_ref = refs[6 + ng:8 + ng]
        s_scr = refs[8 + 2 * ng]
        step = pl.program_id(0)
        if ng:
            plan = _GatherPlan(refs[6:6 + ng], refs[8 + ng:8 + 2 * ng], refs[9 + 2 * ng:])
            pl.when(step == 0)(plan.start)
            pl.when(step == nchunks // 2)(plan.relay)

        @pl.when(step == 0)
        def _():
            s_scr[...] = jnp.zeros_like(s_scr)

        diag = _diag_mask()

        def group(gi, S):
            t0 = pl.multiple_of(gi * WKV_GROUP, WKV_GROUP)
            rows = pl.ds(t0, WKV_GROUP)
            R, W, K, V, A, B = (ref[rows, :] for ref in (r_ref, w_ref, k_ref, v_ref, a_ref, b_ref))
            vcols = _cols([V[s:s + 1] for s in range(WKV_GROUP)], diag)
            yrows = []
            for s in range(WKV_GROUP):
                sa = _headsum(S * A[s:s + 1])
                S = S * W[s:s + 1] + sa * B[s:s + 1] + vcols[s] * K[s:s + 1]
                st_ref[t0 + s] = S
                yrows.append(_row(_headsum(S * R[s:s + 1]), diag))
            y_ref[rows, :] = jnp.concatenate(yrows, axis=0)
            return S

        s_scr[...] = lax.fori_loop(0, ngroups, group, s_scr[...])
        if ng:
            pl.when(step == nchunks - 1)(plan.finish_relayed)

    vec = pl.BlockSpec((ch, RW), lambda c: (c, 0))
    return _pcall(
        body, name=name, grid=(nchunks,), in_specs=[vec] * 6 + [_HBM] * ng,
        out_specs=[vec, pl.BlockSpec((ch, HEAD_DIM, RW), lambda c: (c, 0, 0))] + [_HBM] * ng,
        out_shape=[jax.ShapeDtypeStruct((T, RW), F32), jax.ShapeDtypeStruct((T, HEAD_DIM, RW), F32)] + _gathered_shapes(gather),
        scratch_shapes=[pltpu.VMEM((HEAD_DIM, RW), F32)] + (_GatherPlan.sems(ng) if ng else []),
        compiler_params=_params(("arbitrary",)),
    )(r, w, k, v, a, b, *gather)


def wkv_bwd(r, w, k, v, a, b, dy, states, *, name, exchange=()):
    T = r.shape[0]
    ch = min(WKV_CHUNK, T)
    nchunks = T // ch
    ngroups = ch // WKV_GROUP
    ne = len(exchange)

    def body(*refs):
        r_ref, w_ref, k_ref, v_ref, a_ref, b_ref, dy_ref, st_ref, stp_ref = refs[:9]
        dr_ref, dw_ref, dk_ref, dv_ref, da_ref, db_ref = refs[9 + ne:15 + ne]
        ds_scr = refs[15 + 2 * ne]
        step = pl.program_id(0)
        if ne:
            plan = _ExchangePlan(refs[9:9 + ne], refs[15 + ne:15 + 2 * ne], refs[16 + 2 * ne:])
            pl.when(step == 0)(plan.start)

        @pl.when(step == 0)
        def _():
            ds_scr[...] = jnp.zeros_like(ds_scr)

        has_prev_chunk = step < nchunks - 1
        diag = _diag_mask()
        colsum = lambda x: jnp.sum(x, axis=0, keepdims=True)

        def group(gj, dS):
            gi = ngroups - 1 - gj
            t0 = pl.multiple_of(gi * WKV_GROUP, WKV_GROUP)
            rows = pl.ds(t0, WKV_GROUP)
            R, W, K, V, A, B, DY = (ref[rows, :] for ref in (r_ref, w_ref, k_ref, v_ref, a_ref, b_ref, dy_ref))
            before = jnp.where(gi > 0, st_ref[jnp.maximum(t0 - 1, 0)], jnp.where(has_prev_chunk, stp_ref[0], 0.0))
            prev_state = lambda s: st_ref[t0 + s - 1] if s > 0 else before
            steps = range(WKV_GROUP)
            dycols = _cols([DY[s:s + 1] for s in steps], diag)
            vcols = _cols([V[s:s + 1] for s in steps], diag)
            sas = _headsums([prev_state(s) * A[s:s + 1] for s in steps], 1)
            got = [[None] * WKV_GROUP for _ in range(6)]
            for s in reversed(steps):
                Sp = prev_state(s)
                dS = dS + dycols[s] * R[s:s + 1]
                got[0][s] = colsum(st_ref[t0 + s] * dycols[s])
                got[3][s] = _row(_headsum(dS * K[s:s + 1]), diag)
                got[2][s] = colsum(dS * vcols[s])
                dsa = _headsum(dS * B[s:s + 1])
                got[5][s] = colsum(dS * sas[s])
                got[1][s] = colsum(dS * Sp)
                got[4][s] = colsum(Sp * dsa)
                dS = dS * W[s:s + 1] + dsa * A[s:s + 1]
            for q, ref in enumerate((dr_ref, dw_ref, dk_ref, dv_ref, da_ref, db_ref)):
                ref[rows, :] = jnp.concatenate(got[q], axis=0)
            return dS

        ds_scr[...] = lax.fori_loop(0, ngroups, group, ds_scr[...])
        if ne:
            pl.when(step == nchunks - 1)(plan.finish)

    vec = pl.BlockSpec((ch, RW), lambda c: (nchunks - 1 - c, 0))
    st_spec = pl.BlockSpec((ch, HEAD_DIM, RW), lambda c: (nchunks - 1 - c, 0, 0))
    stp_spec = pl.BlockSpec((1, HEAD_DIM, RW), lambda c: (jnp.maximum((nchunks - 1 - c) * ch - 1, 0), 0, 0))
    return _pcall(
        body, name=name, grid=(nchunks,), in_specs=[vec] * 7 + [st_spec, stp_spec] + [_HBM] * ne,
        out_specs=[vec] * 6 + [_HBM] * ne,
        out_shape=[jax.ShapeDtypeStruct((T, RW), F32)] * 6 + [jax.ShapeDtypeStruct(e.shape, e.dtype) for e in exchange],
        scratch_shapes=[pltpu.VMEM((HEAD_DIM, RW), F32)] + (_ExchangePlan.sems(ne) if ne else []),
        compiler_params=_params(("arbitrary",)),
    )(r, w, k, v, a, b, dy, states, states, *exchange)


_HBM = pl.BlockSpec(memory_space=pltpu.HBM)
_MESH = pl.DeviceIdType.MESH


def _place():
    x, y, c = lax.axis_index("x"), lax.axis_index("y"), lax.axis_index("c")
    return x, y, c, [(1 - x, y), (x, 1 - y), (1 - x, 1 - y)]


def _remote(src, dst, send_sem, recv_sem, to):
    return pltpu.make_async_remote_copy(src_ref=src, dst_ref=dst, send_sem=send_sem, recv_sem=recv_sem, device_id=to,
                                        device_id_type=_MESH)


def _dma_sems(*counts):
    return [pltpu.SemaphoreType.DMA((n,)) for n in counts]


class _GatherPlan:
    def __init__(self, ins, outs, sems):
        self.ins, self.outs, self.n = ins, outs, len(ins)
        self.ici_send, self.ici_recv, self.d2d_send, self.d2d_recv, self.local_sems = sems
        x, y, c, chips = _place()
        self.c, self.me, self.sibling = c, 2 * x + y, (x, y, 1 - c)
        self.peers = [(2 * qx + qy, (qx, qy, c)) for qx, qy in chips]

    @staticmethod
    def sems(n):
        return _dma_sems(3 * n, 3 * n, 3 * n, 3 * n, n)

    def _half(self, i, which):
        rh = self.ins[i].shape[0] // 2
        return pl.ds(which * rh, rh)

    def _local(self, i):
        return pltpu.make_async_copy(self.ins[i], self.outs[i].at[self.me], self.local_sems.at[i])

    def _send(self, i, j):
        k, mine = 3 * i + j, self._half(i, self.c)
        return _remote(self.ins[i].at[mine], self.outs[i].at[self.me, mine], self.ici_send.at[k], self.ici_recv.at[k],
                       self.peers[j][1])

    def _landed(self, i, j):
        k, piece = 3 * i + j, self.outs[i].at[self.peers[j][0], self._half(i, self.c)]
        return _remote(piece, piece, self.ici_send.at[k], self.ici_recv.at[k], self.peers[j][1])

    def _pass(self, i, j, which):
        k, piece = 3 * i + j, self.outs[i].at[self.peers[j][0], self._half(i, which)]
        return _remote(piece, piece, self.d2d_send.at[k], self.d2d_recv.at[k], self.sibling)

    def _all(self):
        return [(i, j) for i in range(self.n) for j in range(3)]

    def start(self):
        for i in range(self.n):
            self._local(i).start()
        for i, j in self._all():
            self._send(i, j).start()

    def relay(self):
        for i, j in self._all():
            self._landed(i, j).wait_recv()
            self._pass(i, j, self.c).start()

    def finish_relayed(self):
        for i, j in self._all():
            self._pass(i, j, 1 - self.c).wait_recv()
        for i, j in self._all():
            self._send(i, j).wait_send()
            self._pass(i, j, self.c).wait_send()
        for i in range(self.n):
            self._local(i).wait()

    def finish(self):
        self.relay()
        self.finish_relayed()

    @staticmethod
    def out_shapes(shards):
        return _gathered_shapes(shards)


def _gathered_shapes(shards):
    return [jax.ShapeDtypeStruct((N_CHIPS,) + s.shape, s.dtype) for s in shards]


def gather_weights(shards, *, name):
    n = len(shards)

    def body(*refs):
        plan = _GatherPlan(refs[:n], refs[n:2 * n], refs[2 * n:])
        plan.start()
        plan.finish()

    return _pcall(body, name=name, in_specs=[_HBM] * n, out_specs=[_HBM] * n, out_shape=_gathered_shapes(shards),
                  scratch_shapes=_GatherPlan.sems(n), compiler_params=_params())(*shards)


def to_sibling(arrays, take_other_half, *, name):
    n = len(arrays)

    def body(*refs):
        ins, outs = refs[:n], refs[n:2 * n]
        send_sems, recv_sems = refs[2 * n:]
        x, y, c, _ = _place()
        cps = []
        for i in range(n):
            src = ins[i]
            if take_other_half:
                rh = src.shape[1] // 2
                src = src.at[:, pl.ds((1 - c) * rh, rh)]
            cps.append(_remote(src, outs[i], send_sems.at[i], recv_sems.at[i], (x, y, 1 - c)))
        for cp in cps:
            cp.start()
        for cp in cps:
            cp.wait_recv()
        for cp in cps:
            cp.wait_send()

    def out_of(a):
        shape = (a.shape[0], a.shape[1] // 2, a.shape[2]) if take_other_half else a.shape
        return jax.ShapeDtypeStruct(shape, a.dtype)

    return _pcall(body, name=name, in_specs=[_HBM] * n, out_specs=[_HBM] * n, out_shape=[out_of(a) for a in arrays],
                  scratch_shapes=_dma_sems(n, n), compiler_params=_params())(*arrays)


def exchange_chips(arrays, *, name):
    n = len(arrays)

    def body(*refs):
        plan = _ExchangePlan(refs[:n], refs[n:2 * n], refs[2 * n:])
        plan.start()
        plan.finish()

    return _pcall(body, name=name, in_specs=[_HBM] * n, out_specs=[_HBM] * n,
                  out_shape=[jax.ShapeDtypeStruct(a.shape, a.dtype) for a in arrays],
                  scratch_shapes=_ExchangePlan.sems(n), compiler_params=_params())(*arrays)


class _ExchangePlan:
    def __init__(self, ins, outs, sems):
        self.ins, self.outs, self.n = ins, outs, len(ins)
        self.send_sems, self.recv_sems, self.local_sems = sems
        x, y, c, chips = _place()
        self.me = 2 * x + y
        self.peers = [(2 * qx + qy, (qx, qy, c)) for qx, qy in chips]

    @staticmethod
    def sems(n):
        return _dma_sems(3 * n, 3 * n, n)

    @staticmethod
    def out_shapes(arrays):
        return [jax.ShapeDtypeStruct(a.shape, a.dtype) for a in arrays]

    def _local(self, i):
        return pltpu.make_async_copy(self.ins[i].at[self.me], self.outs[i].at[self.me], self.local_sems.at[i])

    def _send(self, i, j):
        k = 3 * i + j
        return _remote(self.ins[i].at[self.peers[j][0]], self.outs[i].at[self.me], self.send_sems.at[k], self.recv_sems.at[k],
                       self.peers[j][1])

    def _landed(self, i, j):
        k, piece = 3 * i + j, self.outs[i].at[self.peers[j][0]]
        return _remote(piece, piece, self.send_sems.at[k], self.recv_sems.at[k], self.peers[j][1])

    def start(self):
        for i in range(self.n):
            self._local(i).start()
            for j in range(3):
                self._send(i, j).start()

    def finish(self):
        for i in range(self.n):
            for j in range(3):
                self._landed(i, j).wait_recv()
        for i in range(self.n):
            for j in range(3):
                self._send(i, j).wait_send()
            self._local(i).wait()


def _core_index():
    return lax.axis_index("c").astype(jnp.int32).reshape(1)


def pair_sum(g, theirs, wire_dtype, *, name):
    _, R, C = g.shape
    rh = R // 2
    tr = _tile(rh, 256, 16)
    nt = rh // tr

    def body(c_ref, g_ref, t_ref, q_ref, qw_ref):
        q = g_ref[...] + t_ref[...]
        q_ref[...] = q
        qw_ref[...] = q.astype(wire_dtype)

    blk = pl.BlockSpec((1, tr, C), lambda b, i, c_ref: (b, i, 0))
    mine = pl.BlockSpec((1, tr, C), lambda b, i, c_ref: (b, c_ref[0] * nt + i, 0))
    grid_spec = pltpu.PrefetchScalarGridSpec(num_scalar_prefetch=1, grid=(N_CHIPS, nt), in_specs=[mine, blk], out_specs=[blk, blk])
    return _pcall(body, name=name, grid_spec=grid_spec,
                  out_shape=[jax.ShapeDtypeStruct((N_CHIPS, rh, C), F32), jax.ShapeDtypeStruct((N_CHIPS, rh, C), wire_dtype)],
                  compiler_params=_params(("parallel", "parallel")))(_core_index(), g, theirs)


def half_sum(own, landed, *, name):
    _, rh, C = own.shape
    tr = _tile(rh, 256, 16)

    def body(me_ref, own_ref, land_ref, o_ref):
        total = None
        for p in range(N_CHIPS):
            term = jnp.where(me_ref[0] == p, own_ref[p], land_ref[p].astype(F32))
            total = term if total is None else total + term
        o_ref[...] = total

    blk = pl.BlockSpec((N_CHIPS, tr, C), lambda i, me_ref: (0, i, 0))
    grid_spec = pltpu.PrefetchScalarGridSpec(num_scalar_prefetch=1, grid=(rh // tr,), in_specs=[blk, blk],
                                             out_specs=pl.BlockSpec((tr, C), lambda i, me_ref: (i, 0)))
    me = (2 * lax.axis_index("x") + lax.axis_index("y")).astype(jnp.int32).reshape(1)
    return _pcall(body, name=name, grid_spec=grid_spec, out_shape=jax.ShapeDtypeStruct((rh, C), F32),
                  compiler_params=_params(("parallel",)))(me, own, landed)


def adamw(w, m, v, mine, theirs, *, name):
    R, C = w.shape
    rh = R // 2
    tr = _tile(rh, 256, 8)
    nt = rh // tr

    def body(c_ref, w_ref, m_ref, v_ref, a_ref, b_ref, g_ref, d_ref, nm_ref, nv_ref):
        is_mine = (pl.program_id(0) // nt) == c_ref[0]
        g = jnp.where(is_mine, a_ref[...], b_ref[...])
        g_ref[...] = g
        nm = ADAM_B1 * m_ref[...] + (1.0 - ADAM_B1) * g
        nv = ADAM_B2 * v_ref[...] + (1.0 - ADAM_B2) * (g * g)
        nm_ref[...] = nm
        nv_ref[...] = nv
        m_hat = nm / (1.0 - ADAM_B1 ** ADAM_STEP)
        v_hat = nv / (1.0 - ADAM_B2 ** ADAM_STEP)
        d_ref[...] = -ADAM_LR * (m_hat / (jnp.sqrt(v_hat) + ADAM_EPS) + ADAM_WD * w_ref[...])

    full = pl.BlockSpec((tr, C), lambda i, c_ref: (i, 0))
    part = pl.BlockSpec((tr, C), lambda i, c_ref: (i % nt, 0))
    grid_spec = pltpu.PrefetchScalarGridSpec(num_scalar_prefetch=1, grid=(2 * nt,), in_specs=[full] * 3 + [part] * 2,
                                             out_specs=[full] * 4)
    return _pcall(body, name=name, grid_spec=grid_spec, out_shape=[jax.ShapeDtypeStruct((R, C), F32)] * 4,
                  compiler_params=_params(("parallel",)))(_core_index(), w, m, v, mine, theirs)


def _to_blocks(full, axis):
    r, c = full.shape
    if axis == 1:
        return full.reshape(r, N_CHIPS, c // N_CHIPS).transpose(1, 0, 2)
    return full.reshape(N_CHIPS, r // N_CHIPS, c)


def _from_blocks(blocks, axis):
    _, r, c = blocks.shape
    if axis == 1:
        return blocks.transpose(1, 0, 2).reshape(r, N_CHIPS * c)
    return blocks.reshape(N_CHIPS * r, c)


def _ffn_fwd(x, norm, wg, wu, wd, tag, carry=None):
    h = rms_fwd(x, norm, name=tag + "_norm")
    gate, *carried = mm(h, wg, out_dtype=BF16, name=tag + "_gate", carry=carry) if carry else (
        mm(h, wg, out_dtype=BF16, name=tag + "_gate"),)
    up = mm(h, wu, out_dtype=BF16, name=tag + "_up")
    act = swiglu_fwd(gate, up, name=tag + "_act")
    out = mm(act, wd, scale=0.5, res=x, name=tag + "_down")
    return out, (h, gate, up, act), carried


def _ffn_bwd(dout, x, saved, norm, wg, wu, wd, tag, carry=None):
    h, gate, up, act = saved
    dact, *carried = mm(dout, wd, tb=True, scale=0.5, out_dtype=BF16, name=tag + "_dact", carry=carry) if carry else (
        mm(dout, wd, tb=True, scale=0.5, out_dtype=BF16, name=tag + "_dact"),)
    dwd = mm(act, dout, ta=True, scale=0.5, name=tag + "_dwd")
    dgate, dup = swiglu_bwd(dact, gate, up, name=tag + "_dswiglu")
    dwg = mm(h, dgate, ta=True, name=tag + "_dwg")
    dwu = mm(h, dup, ta=True, name=tag + "_dwu")
    dh = mm(dgate, wg, tb=True, name=tag + "_dh_gate")
    dh = mm(dup, wu, tb=True, res=dh, name=tag + "_dh_up")
    dx, dnorm = rms_bwd(dh, x, norm, dout, name=tag + "_dnorm")
    return dx, dnorm, dwg, dwu, dwd, carried


FIRST_WEIGHTS = ['ffn1_w_gate', 'ffn1_w_up', 'ffn1_w_down']
MID_WEIGHTS = ['w_in', 'rwkv_w_lora_up', 'rwkv_a_lora_up', 'rwkv_g_lora_up']
LATE_WEIGHTS = ['w_branch_rwkv', 'w_branch_attn', 'w_out', 'ffn2_w_gate', 'ffn2_w_up', 'ffn2_w_down']


def _pair_sums(names, blocks, tag):
    from_sibling = to_sibling(blocks, True, name=tag + "_grads_to_sibling")
    return [pair_sum(g, t, F32 if n == 'small' else BF16, name="pair_sum_" + n)
            for g, t, n in zip(blocks, from_sibling, names)]


def _step(A):
    x, tgt = A['x'][0], A['loss_target'][0]
    T = x.shape[0]
    w = {n: A[n][0] for n in WEIGHT_NAMES}
    row = lambda a: a.reshape(1, -1)

    axis_of = dict(BIG)
    shard = lambda n: w[n].astype(BF16)
    n1, nmix, n2, nfin = (row(w[n]) for n in ('ffn1_norm', 'mix_norm', 'ffn2_norm', 'final_norm'))
    gathered = gather_weights([shard(n) for n in FIRST_WEIGHTS], name="gather_weights")
    full = {n: _from_blocks(b, axis_of[n]) for n, b in zip(FIRST_WEIGHTS, gathered)}
    x1, ffn1, gathered = _ffn_fwd(x, n1, full['ffn1_w_gate'], full['ffn1_w_up'], full['ffn1_w_down'], "ffn1",
                                  carry=(_GatherPlan, [shard(n) for n in MID_WEIGHTS]))
    full.update({n: _from_blocks(b, axis_of[n]) for n, b in zip(MID_WEIGHTS, gathered)})
    w_in_r = _pad_rwkv_cols(full['w_in'][:, :RWKV_COLS])
    w_in_a = full['w_in'][:, RWKV_COLS:RWKV_COLS + ATT_COLS]
    w_in_g = full['w_in'][:, RWKV_COLS + ATT_COLS:]
    wlw, wla, wlg = (_pad_rows(full[n], 128).astype(F32) for n in ('rwkv_w_lora_up', 'rwkv_a_lora_up', 'rwkv_g_lora_up'))
    mu = _pad_rwkv_cols(row(w['rwkv_mu']))
    w0, a0, k_k, k_a, r_k, ln_w, ln_b = (row(w[n]) for n in ('rwkv_w0', 'rwkv_a0', 'rwkv_k_k', 'rwkv_k_a', 'rwkv_r_k',
                                                               'rwkv_ln_w', 'rwkv_ln_b'))
    qg = jnp.tile(row(w['attn_q_norm']), (1, N_HEADS))
    kg = jnp.tile(row(w['attn_k_norm']), (1, KVW // HEAD_DIM))
    sinks = jnp.pad(row(w['attn_sinks']), ((0, 0), (0, LANE - N_HEADS)))

    h2 = rms_fwd(x1, nmix, name="mix_norm")
    pr = mm(h2, w_in_r, name="proj_rwkv")
    pa = mm(h2, w_in_a, name="proj_att")
    pg = mm(h2, w_in_g, name="proj_gate")
    pr_shift = jnp.pad(pr, ((1, 0), (0, 0)))[:-1]
    r, dec, k2, v, a, b, sg = rwkv_pre_fwd(pr, pr_shift, mu, w0, a0, k_k, k_a, wlw, wla, wlg, name="rwkv_pre")
    y, states, *gathered = wkv_fwd(r, dec, k2, v, a, b, name="wkv_fwd", gather=[shard(n) for n in LATE_WEIGHTS])
    full.update({n: _from_blocks(b, axis_of[n]) for n, b in zip(LATE_WEIGHTS, gathered)})
    yr = rwkv_post_fwd(y, r, k2, v, sg, wlg, ln_w, ln_b, r_k, name="rwkv_post")
    ya = att_fwd(pa, qg, kg, sinks, name="att_fwd")
    br = mm(yr, full['w_branch_rwkv'], name="branch_rwkv")
    ba = mm(ya, full['w_branch_attn'], name="branch_att")
    mg = merge_fwd(br, ba, pg, name="merge")
    x2 = mm(mg, full['w_out'], res=x1, name="mix_out")
    x3, ffn2, _ = _ffn_fwd(x2, n2, full['ffn2_w_gate'], full['ffn2_w_up'], full['ffn2_w_down'], "ffn2")
    dx3, d_nfin, loss = final_loss(x3, tgt, nfin, name="final_loss")

    G = {'final_norm': d_nfin}
    dx2, G['ffn2_norm'], G['ffn2_w_gate'], G['ffn2_w_up'], G['ffn2_w_down'], _ = _ffn_bwd(
        dx3, x2, ffn2, n2, full['ffn2_w_gate'], full['ffn2_w_up'], full['ffn2_w_down'], "ffn2")
    dmg = mm(dx2, full['w_out'], tb=True, name="d_merge")
    G['w_out'] = mm(mg, dx2, ta=True, name="d_w_out")
    dbr, dba, dpg = merge_bwd(dmg, br, ba, pg, name="merge_bwd")
    dyr = mm(dbr, full['w_branch_rwkv'], tb=True, name="d_y_rwkv")
    G['w_branch_rwkv'] = mm(yr, dbr, ta=True, name="d_w_branch_rwkv")
    dya = mm(dba, full['w_branch_attn'], tb=True, name="d_y_att")
    G['w_branch_attn'] = mm(ya, dba, ta=True, name="d_w_branch_att")
    dy, dz, dg, G['rwkv_ln_w'], G['rwkv_ln_b'] = rwkv_post_bwd(dyr, y, r, k2, v, sg, wlg, ln_w, ln_b, r_k, name="rwkv_post_bwd")
    late_pair = _pair_sums(LATE_WEIGHTS, [_to_blocks(G[n], axis_of[n]) for n in LATE_WEIGHTS], "late")
    res = wkv_bwd(r, dec, k2, v, a, b, dy, states, name="wkv_bwd", exchange=[q for _, q in late_pair])
    wkv_grads, late_landed = res[:6], res[6:]
    (dpr, d_mu, G['rwkv_w0'], G['rwkv_a0'], G['rwkv_k_k'], G['rwkv_k_a'], G['rwkv_r_k'], d_wlw, d_wla, d_wlg) = rwkv_pre_bwd(
        pr, pr_shift, *wkv_grads, dz, dg, mu, w0, a0, k_k, k_a, r_k, wlw, wla, wlg, name="rwkv_pre_bwd")
    G['rwkv_mu'] = _unpad_rwkv_cols(d_mu)
    G['rwkv_w_lora_up'], G['rwkv_a_lora_up'], G['rwkv_g_lora_up'] = d_wlw[:DECAY_LORA], d_wla[:ICLR_LORA], d_wlg[:GATE_LORA]
    dq, dko, dkn, dvo, dvn, G['attn_q_norm'], d_sinks = att_bwd(pa, dya, qg, kg, sinks, name="att_bwd")
    G['attn_sinks'] = d_sinks[:, :N_HEADS]
    dpa, G['attn_k_norm'] = att_kv_bwd(pa, dq, dko, dkn, dvo, dvn, kg, name="att_kv_bwd")
    d_w_in_r = mm(h2, dpr, ta=True, name="d_w_in_rwkv")
    d_w_in_a = mm(h2, dpa, ta=True, name="d_w_in_att")
    d_w_in_g = mm(h2, dpg, ta=True, name="d_w_in_gate")
    G['w_in'] = jnp.concatenate([_unpad_rwkv_cols(d_w_in_r), d_w_in_a, d_w_in_g], axis=1)
    dh2 = mm(dpr, w_in_r, tb=True, name="d_h2_rwkv")
    dh2 = mm(dpa, w_in_a, tb=True, res=dh2, name="d_h2_att")
    dh2 = mm(dpg, w_in_g, tb=True, res=dh2, name="d_h2_gate")
    dx1, G['mix_norm'] = rms_bwd(dh2, x1, nmix, dx2, name="d_mix_norm")
    mid_pair = _pair_sums(MID_WEIGHTS, [_to_blocks(G[n], axis_of[n]) for n in MID_WEIGHTS], "mid")
    dx0, G['ffn1_norm'], G['ffn1_w_gate'], G['ffn1_w_up'], G['ffn1_w_down'], mid_landed = _ffn_bwd(
        dx1, x, ffn1, n1, full['ffn1_w_gate'], full['ffn1_w_up'], full['ffn1_w_down'], "ffn1",
        carry=(_ExchangePlan, [q for _, q in mid_pair]))

    small_shapes = [(w[n].size,) for n in SMALL] + [(1,)]

    def small_rows(parts):
        vec = jnp.concatenate([p.reshape(-1) for p in parts])
        return jnp.pad(vec, (0, SMALL_ROWS * FLAT_W - vec.shape[0])).reshape(SMALL_ROWS, FLAT_W)

    small = small_rows([G[n] for n in SMALL] + [loss[0, :1]])
    first_names = FIRST_WEIGHTS + ['small']
    first_blocks = [_to_blocks(G[n], axis_of[n]) for n in FIRST_WEIGHTS] + [jnp.broadcast_to(small[None], (N_CHIPS,) + small.shape)]
    first_pair = _pair_sums(first_names, first_blocks, "first")
    first_landed = exchange_chips([q for _, q in first_pair], name="exchange_grads")
    names = first_names + MID_WEIGHTS + LATE_WEIGHTS
    pair, landed = first_pair + mid_pair + late_pair, list(first_landed) + list(mid_landed) + list(late_landed)
    halves = [half_sum(own, l, name="half_sum_" + n) for (own, _), l, n in zip(pair, landed, names)]
    other_halves = to_sibling(halves, False, name="halves_to_sibling")

    def local(prefix, n):
        if n != 'small':
            return A[prefix + n][0]
        return small_rows([A[prefix + s] for s in SMALL] + [jnp.zeros((1,), F32)])

    result = {}
    for n, mine, theirs in zip(names, halves, other_halves):
        outs4 = adamw(local('', n), local('m_', n), local('v_', n), mine, theirs, name="adamw_" + n)
        for kind, o in zip(('grad_', 'delta_', 'new_m_', 'new_v_'), outs4):
            if n != 'small':
                result[kind + n] = o[None]
            else:
                for s, part in zip(SMALL + ['loss'], _unpack_vec(o.reshape(-1), small_shapes)):
                    result[kind + s] = part.reshape(A[s].shape) if s != 'loss' else part.reshape(())
    outs = [result['grad_loss'], dx0[None]]
    for kind in ('grad_', 'delta_', 'new_m_', 'new_v_'):
        outs += [result[kind + n] for n in WEIGHT_NAMES]
    return tuple(outs)


def _unpack_vec(vec, shapes):
    out, off = [], 0
    for (n,) in shapes:
        out.append(vec[off:off + n])
        off += n
    return out


def kernel(x, ffn1_norm, ffn1_w_gate, ffn1_w_up, ffn1_w_down, mix_norm, w_in, rwkv_mu, rwkv_w0, rwkv_w_lora_up, rwkv_a0, rwkv_a_lora_up, rwkv_g_lora_up, rwkv_k_k, rwkv_k_a, rwkv_r_k, rwkv_ln_w, rwkv_ln_b, attn_q_norm, attn_k_norm, attn_sinks, w_branch_rwkv, w_branch_attn, w_out, ffn2_norm, ffn2_w_gate, ffn2_w_up, ffn2_w_down, final_norm, loss_target, m_ffn1_norm, m_ffn1_w_gate, m_ffn1_w_up, m_ffn1_w_down, m_mix_norm, m_w_in, m_rwkv_mu, m_rwkv_w0, m_rwkv_w_lora_up, m_rwkv_a0, m_rwkv_a_lora_up, m_rwkv_g_lora_up, m_rwkv_k_k, m_rwkv_k_a, m_rwkv_r_k, m_rwkv_ln_w, m_rwkv_ln_b, m_attn_q_norm, m_attn_k_norm, m_attn_sinks, m_w_branch_rwkv, m_w_branch_attn, m_w_out, m_ffn2_norm, m_ffn2_w_gate, m_ffn2_w_up, m_ffn2_w_down, m_final_norm, v_ffn1_norm, v_ffn1_w_gate, v_ffn1_w_up, v_ffn1_w_down, v_mix_norm, v_w_in, v_rwkv_mu, v_rwkv_w0, v_rwkv_w_lora_up, v_rwkv_a0, v_rwkv_a_lora_up, v_rwkv_g_lora_up, v_rwkv_k_k, v_rwkv_k_a, v_rwkv_r_k, v_rwkv_ln_w, v_rwkv_ln_b, v_attn_q_norm, v_attn_k_norm, v_attn_sinks, v_w_branch_rwkv, v_w_branch_attn, v_w_out, v_ffn2_norm, v_ffn2_w_gate, v_ffn2_w_up, v_ffn2_w_down, v_final_norm):
    return _step(dict(locals()))
```

```python
import functools

import jax
import jax.numpy as jnp
from jax import lax
from jax.experimental import pallas as pl
from jax.experimental.pallas import tpu as pltpu

F32 = jnp.float32
BF16 = jnp.bfloat16

D_MODEL = 1024
D_FF = 2816
HEAD_DIM = 64
N_HEADS = 8
RW = 512
KVW = 128
ATT_GROUP = 4
WINDOW = 128
BLOCK = 128
DECAY_LORA, ICLR_LORA, GATE_LORA = 32, 32, 96
RWKV_COLS = 3 * RW + DECAY_LORA + ICLR_LORA + GATE_LORA
ATT_COLS = RW + 2 * KVW
GATE_COLS = 2 * D_MODEL
RWKV_PAD = 3 * RW + 3 * 128
RMS_EPS = 1e-6
GN_EPS = 64e-5
N_CHIPS = 4
LANE = 128
FLAT_W = 1024
SMALL_ROWS = 32
NEG_BIG = -1e30

ADAM_LR, ADAM_B1, ADAM_B2, ADAM_EPS, ADAM_WD, ADAM_STEP = 0.001, 0.9, 0.999, 1e-08, 0.01, 10

VMEM_LIMIT = 56 * 1024 * 1024

WEIGHT_NAMES = ['ffn1_norm', 'ffn1_w_gate', 'ffn1_w_up', 'ffn1_w_down', 'mix_norm', 'w_in', 'rwkv_mu', 'rwkv_w0',
                'rwkv_w_lora_up', 'rwkv_a0', 'rwkv_a_lora_up', 'rwkv_g_lora_up', 'rwkv_k_k', 'rwkv_k_a', 'rwkv_r_k',
                'rwkv_ln_w', 'rwkv_ln_b', 'attn_q_norm', 'attn_k_norm', 'attn_sinks', 'w_branch_rwkv',
                'w_branch_attn', 'w_out', 'ffn2_norm', 'ffn2_w_gate', 'ffn2_w_up', 'ffn2_w_down', 'final_norm']
BIG = [('ffn1_w_gate', 1), ('ffn1_w_up', 1), ('ffn1_w_down', 0), ('w_in', 1), ('rwkv_w_lora_up', 1),
       ('rwkv_a_lora_up', 1), ('rwkv_g_lora_up', 1), ('w_branch_rwkv', 1), ('w_branch_attn', 1), ('w_out', 0),
       ('ffn2_w_gate', 1), ('ffn2_w_up', 1), ('ffn2_w_down', 0)]
SMALL = ['ffn1_norm', 'mix_norm', 'rwkv_mu', 'rwkv_w0', 'rwkv_a0', 'rwkv_k_k', 'rwkv_k_a', 'rwkv_r_k', 'rwkv_ln_w',
         'rwkv_ln_b', 'attn_q_norm', 'attn_k_norm', 'attn_sinks', 'ffn2_norm', 'final_norm']


def _pcall(body, **kw):
    return pl.pallas_call(body, **kw)


def _params(sem=None, **kw):
    if sem is not None:
        kw['dimension_semantics'] = sem
    return pltpu.CompilerParams(vmem_limit_bytes=VMEM_LIMIT, **kw)


def _tile(n, cap, mult):
    best = None
    for t in range(mult, min(n, cap) + 1, mult):
        if n % t == 0:
            best = t
    return best or n


def _sigmoid(z):
    return 1.0 / (1.0 + jnp.exp(-z))


def _softplus(z):
    return jnp.maximum(z, 0.0) + jnp.log(1.0 + jnp.exp(-jnp.abs(z)))


def _bdot(a, b, dims=(((1,), (0,)), ((), ()))):
    return lax.dot_general(a.astype(BF16), b.astype(BF16), dims, preferred_element_type=F32)


_NT = (((1,), (1,)), ((), ()))
_TN = (((0,), (0,)), ((), ()))


def _segsum(x, bd):
    hi = x.astype(BF16)
    r1 = x - hi.astype(F32)
    mid = r1.astype(BF16)
    lo = (r1 - mid.astype(F32)).astype(BF16)
    dot = functools.partial(lax.dot_general, dimension_numbers=(((1,), (0,)), ((), ())), preferred_element_type=F32)
    return dot(hi, bd) + dot(mid, bd) + dot(lo, bd)


_LORA_EDGES = (3 * RW, 3 * RW + DECAY_LORA, 3 * RW + DECAY_LORA + ICLR_LORA, RWKV_COLS)


def _pad_rwkv_cols(x):
    parts = [x[..., :3 * RW]]
    for lo, hi in zip(_LORA_EDGES[:-1], _LORA_EDGES[1:]):
        parts.append(jnp.pad(x[..., lo:hi], [(0, 0)] * (x.ndim - 1) + [(0, 128 - (hi - lo))]))
    return jnp.concatenate(parts, axis=-1)


def _unpad_rwkv_cols(x):
    parts = [x[..., :3 * RW]]
    for j, (lo, hi) in enumerate(zip(_LORA_EDGES[:-1], _LORA_EDGES[1:])):
        parts.append(x[..., 3 * RW + 128 * j:3 * RW + 128 * j + (hi - lo)])
    return jnp.concatenate(parts, axis=-1)


def _pad_rows(x, rows):
    return jnp.pad(x, [(0, rows - x.shape[0])] + [(0, 0)] * (x.ndim - 1))


def mm(a, b, *, name, ta=False, tb=False, scale=None, res=None, out_dtype=F32, carry=None):
    M, K = (a.shape[1], a.shape[0]) if ta else a.shape
    N = b.shape[0] if tb else b.shape[1]
    assert (b.shape[1] if tb else b.shape[0]) == K
    tm, tn, tk = _tile(M, 1408 if ta else 512, 128), _tile(N, 1408, 128), _tile(K, 1408, 128)
    nk = K // tk
    grid = (M // tm, N // tn, nk)
    dims = (((0 if ta else 1,), (1 if tb else 0,)), ((), ()))
    plan_cls, carried = carry if carry else (None, ())
    nc, nin = len(carried), 2 + (res is not None)

    def body(*refs):
        a_ref, b_ref = refs[:2]
        r_ref = refs[2] if res is not None else None
        o_ref, acc_ref = refs[nin + nc], refs[nin + 2 * nc + 1]
        k = pl.program_id(2)
        if nc:
            plan = plan_cls(refs[nin:nin + nc], refs[nin + nc + 1:nin + 2 * nc + 1], refs[nin + 2 * nc + 2:])
            at = lambda which: functools.reduce(jnp.logical_and, [pl.program_id(d) == (0 if which == 0 else grid[d] - 1)
                                                                 for d in range(3)])
            pl.when(at(0))(plan.start)
        part = _bdot(a_ref[...], b_ref[...], dims)

        @pl.when(k == 0)
        def _():
            acc_ref[...] = part

        @pl.when(k > 0)
        def _():
            acc_ref[...] += part

        @pl.when(k == nk - 1)
        def _():
            o = acc_ref[...]
            if scale is not None:
                o = o * scale
            if r_ref is not None:
                o = o + r_ref[...].astype(F32)
            o_ref[...] = o.astype(out_dtype)

        if nc:
            pl.when(at(1))(plan.finish)

    a_spec = pl.BlockSpec((tk, tm), lambda i, j, k: (k, i)) if ta else pl.BlockSpec((tm, tk), lambda i, j, k: (i, k))
    b_spec = pl.BlockSpec((tn, tk), lambda i, j, k: (j, k)) if tb else pl.BlockSpec((tk, tn), lambda i, j, k: (k, j))
    o_spec = pl.BlockSpec((tm, tn), lambda i, j, k: (i, j))
    in_specs = [a_spec, b_spec] + ([o_spec] if res is not None else [])
    args = (a, b) + ((res,) if res is not None else ())
    out_shape = jax.ShapeDtypeStruct((M, N), out_dtype)
    if not nc:
        return _pcall(
            body, name=name, grid=grid, in_specs=in_specs, out_specs=o_spec, out_shape=out_shape,
            scratch_shapes=[pltpu.VMEM((tm, tn), F32)], compiler_params=_params(("parallel", "parallel", "arbitrary")),
        )(*args)
    return _pcall(
        body, name=name, grid=grid, in_specs=in_specs + [_HBM] * nc, out_specs=[o_spec] + [_HBM] * nc,
        out_shape=[out_shape] + plan_cls.out_shapes(carried), scratch_shapes=[pltpu.VMEM((tm, tn), F32)] + plan_cls.sems(nc),
        compiler_params=_params(("arbitrary", "arbitrary", "arbitrary")),
    )(*args, *carried)


def mm_fused(a, bs, finish, out_dtypes, *, name, tb=False, extras=(), carry=None):
    M, K = a.shape
    N = bs[0].shape[0] if tb else bs[0].shape[1]
    tm, tn = _tile(M, 512, 128), _tile(N, 1408, 128)
    grid = (M // tm, N // tn)
    dims = (((1,), (1 if tb else 0,)), ((), ()))
    plan_cls, carried = carry if carry else (None, ())
    nc, nb, nx, no = len(carried), len(bs), len(extras), len(out_dtypes)
    nin = 1 + nb + nx

    def body(*refs):
        a_ref, b_refs, x_refs = refs[0], refs[1:1 + nb], refs[1 + nb:nin]
        o_refs = refs[nin + nc:nin + nc + no]
        if nc:
            plan = plan_cls(refs[nin:nin + nc], refs[nin + nc + no:nin + 2 * nc + no], refs[nin + 2 * nc + no:])
            at = lambda which: jnp.logical_and(*[pl.program_id(d) == (0 if which == 0 else grid[d] - 1) for d in range(2)])
            pl.when(at(0))(plan.start)
        av = a_ref[...]
        outs = finish([_bdot(av, b_ref[...], dims) for b_ref in b_refs], [x_ref[...] for x_ref in x_refs])
        for o_ref, o in zip(o_refs, outs):
            o_ref[...] = o.astype(o_ref.dtype)
        if nc:
            pl.when(at(1))(plan.finish)

    a_spec = pl.BlockSpec((tm, K), lambda i, j: (i, 0))
    b_spec = pl.BlockSpec((tn, K), lambda i, j: (j, 0)) if tb else pl.BlockSpec((K, tn), lambda i, j: (0, j))
    o_spec = pl.BlockSpec((tm, tn), lambda i, j: (i, j))
    out_shape = [jax.ShapeDtypeStruct((M, N), d) for d in out_dtypes]
    if not nc:
        return _pcall(body, name=name, grid=grid, in_specs=[a_spec] + [b_spec] * nb + [o_spec] * nx, out_specs=[o_spec] * no,
                      out_shape=out_shape, compiler_params=_params(("parallel", "parallel")))(a, *bs, *extras)
    return _pcall(body, name=name, grid=grid, in_specs=[a_spec] + [b_spec] * nb + [o_spec] * nx + [_HBM] * nc,
                  out_specs=[o_spec] * no + [_HBM] * nc, out_shape=out_shape + plan_cls.out_shapes(carried),
                  scratch_shapes=plan_cls.sems(nc), compiler_params=_params(("arbitrary", "arbitrary")))(a, *bs, *extras, *carried)


def _swiglu(products, _):
    g, u = products
    return g, u, g * _sigmoid(g) * u


def _swiglu_bwd(products, extras):
    (da,), (gate, up) = products, extras
    gv = gate.astype(F32)
    s = _sigmoid(gv)
    return da * 0.5 * up.astype(F32) * s * (1.0 + gv * (1.0 - s)), da * 0.5 * gv * s


def _row_spec(tr, c):
    return pl.BlockSpec((tr, c), lambda i: (i, 0))


def _full_spec(shape):
    return pl.BlockSpec(shape, lambda i: (0,) * len(shape))


def _acc_rows(ref, val, i):
    @pl.when(i == 0)
    def _():
        ref[...] = val

    @pl.when(i > 0)
    def _():
        ref[...] += val


def rms_fwd(x, g, *, name):
    T, D = x.shape
    tr = _tile(T, 512, 8)

    def body(x_ref, g_ref, h_ref):
        xv = x_ref[...]
        r = lax.rsqrt(jnp.mean(xv * xv, axis=-1, keepdims=True) + RMS_EPS)
        h_ref[...] = (xv * r * g_ref[...]).astype(BF16)

    return _pcall(body, name=name, grid=(T // tr,), in_specs=[_row_spec(tr, D), _full_spec((1, D))],
                  out_specs=_row_spec(tr, D), out_shape=jax.ShapeDtypeStruct((T, D), BF16),
                  compiler_params=_params(("parallel",)))(x, g)


def rms_bwd(dh, x, g, res, *, name):
    T, D = x.shape
    tr = _tile(T, 256, 8)

    def body(dh_ref, x_ref, g_ref, res_ref, dx_ref, dg_ref):
        i = pl.program_id(0)
        xv, dhv = x_ref[...], dh_ref[...].astype(F32)
        r = lax.rsqrt(jnp.mean(xv * xv, axis=-1, keepdims=True) + RMS_EPS)
        xh = xv * r
        dxh = dhv * g_ref[...]
        dx_ref[...] = res_ref[...] + r * (dxh - xh * jnp.mean(dxh * xh, axis=-1, keepdims=True))
        _acc_rows(dg_ref, jnp.sum(dhv * xh, axis=0, keepdims=True), i)

    return _pcall(body, name=name, grid=(T // tr,),
                  in_specs=[_row_spec(tr, D), _row_spec(tr, D), _full_spec((1, D)), _row_spec(tr, D)],
                  out_specs=[_row_spec(tr, D), _full_spec((1, D))],
                  out_shape=[jax.ShapeDtypeStruct((T, D), F32), jax.ShapeDtypeStruct((1, D), F32)],
                  compiler_params=_params(("arbitrary",)))(dh, x, g, res)


def final_loss(x, tgt, g, *, name):
    T, D = x.shape
    tr = _tile(T, 256, 8)

    def body(x_ref, t_ref, g_ref, dx_ref, dg_ref, loss_ref):
        i = pl.program_id(0)
        xv = x_ref[...]
        r = lax.rsqrt(jnp.mean(xv * xv, axis=-1, keepdims=True) + RMS_EPS)
        xh = xv * r
        e = xh * g_ref[...] - t_ref[...]
        part = 0.5 * jnp.sum(jnp.mean(e * e, axis=-1, keepdims=True), axis=0, keepdims=True)
        dy = e * (1.0 / D)
        dxh = dy * g_ref[...]
        dx_ref[...] = r * (dxh - xh * jnp.mean(dxh * xh, axis=-1, keepdims=True))
        _acc_rows(dg_ref, jnp.sum(dy * xh, axis=0, keepdims=True), i)
        _acc_rows(loss_ref, jnp.broadcast_to(part, (1, LANE)), i)

    return _pcall(body, name=name, grid=(T // tr,),
                  in_specs=[_row_spec(tr, D), _row_spec(tr, D), _full_spec((1, D))],
                  out_specs=[_row_spec(tr, D), _full_spec((1, D)), _full_spec((1, LANE))],
                  out_shape=[jax.ShapeDtypeStruct((T, D), F32), jax.ShapeDtypeStruct((1, D), F32),
                             jax.ShapeDtypeStruct((1, LANE), F32)],
                  compiler_params=_params(("arbitrary",)))(x, tgt, g)


def merge_fwd(br, ba, pg, *, name):
    T, D = br.shape
    tr = _tile(T, 256, 8)

    def body(br_ref, ba_ref, pg_ref, o_ref):
        pgv = pg_ref[...]
        o_ref[...] = (_sigmoid(pgv[:, :D]) * br_ref[...] + _sigmoid(pgv[:, D:]) * ba_ref[...]).astype(BF16)

    return _pcall(body, name=name, grid=(T // tr,), in_specs=[_row_spec(tr, D), _row_spec(tr, D), _row_spec(tr, 2 * D)],
                  out_specs=_row_spec(tr, D), out_shape=jax.ShapeDtypeStruct((T, D), BF16),
                  compiler_params=_params(("parallel",)))(br, ba, pg)


def merge_bwd(dm, br, ba, pg, *, name):
    T, D = br.shape
    tr = _tile(T, 256, 8)

    def body(dm_ref, br_ref, ba_ref, pg_ref, dbr_ref, dba_ref, dpg_ref):
        pgv, dmv = pg_ref[...], dm_ref[...]
        sr, sa = _sigmoid(pgv[:, :D]), _sigmoid(pgv[:, D:])
        dbr_ref[...] = (dmv * sr).astype(BF16)
        dba_ref[...] = (dmv * sa).astype(BF16)
        dpg_ref[:, :D] = dmv * br_ref[...] * sr * (1.0 - sr)
        dpg_ref[:, D:] = dmv * ba_ref[...] * sa * (1.0 - sa)

    return _pcall(body, name=name, grid=(T // tr,),
                  in_specs=[_row_spec(tr, D), _row_spec(tr, D), _row_spec(tr, D), _row_spec(tr, 2 * D)],
                  out_specs=[_row_spec(tr, D), _row_spec(tr, D), _row_spec(tr, 2 * D)],
                  out_shape=[jax.ShapeDtypeStruct((T, D), BF16), jax.ShapeDtypeStruct((T, D), BF16),
                             jax.ShapeDtypeStruct((T, 2 * D), F32)],
                  compiler_params=_params(("parallel",)))(dm, br, ba, pg)


def _rwkv_mix(p, prev, mu, w0, a0, k_k, k_a, wlw, wla, wlg, bd):
    pp = p + (prev - p) * mu
    r, k, v = pp[:, 0:RW], pp[:, RW:2 * RW], pp[:, 2 * RW:3 * RW]
    xw, xa, xg = pp[:, 3 * RW:3 * RW + 128], pp[:, 3 * RW + 128:3 * RW + 256], pp[:, 3 * RW + 256:3 * RW + 384]
    th = jnp.tanh(xw)
    z = -(w0 + _bdot(th, wlw))
    e = jnp.exp(-_softplus(z) - 0.5)
    decay = jnp.exp(-e)
    a = _sigmoid(a0 + _bdot(xa, wla))
    sg = _sigmoid(xg)
    kkr = k * k_k
    n = jnp.sqrt(_segsum(kkr * kkr, bd))
    kk = kkr / jnp.maximum(n, 1e-12)
    k2 = k * (1.0 + (a - 1.0) * k_a)
    return dict(r=r, k=k, v=v, xa=xa, th=th, z=z, e=e, decay=decay, a=a, sg=sg, n=n, kk=kk, k2=k2)


def _seg_matrix(n, shift):
    r = lax.shift_right_logical(lax.broadcasted_iota(jnp.int32, (n, n), 0), shift)
    c = lax.shift_right_logical(lax.broadcasted_iota(jnp.int32, (n, n), 1), shift)
    return jnp.where(r == c, 1.0, 0.0).astype(BF16)


def rwkv_pre_fwd(p, pshift, mu, w0, a0, k_k, k_a, wlw, wla, wlg, *, name):
    T = p.shape[0]
    tr = _tile(T, 256, 8)

    def body(p_ref, ps_ref, mu_ref, w0_ref, a0_ref, kk_ref, ka_ref, wlw_ref, wla_ref, wlg_ref,
             r_ref, w_ref, k_ref, v_ref, a_ref, b_ref, g_ref):
        pv, prev = p_ref[...], ps_ref[...]
        m = _rwkv_mix(pv, prev, mu_ref[...], w0_ref[...], a0_ref[...], kk_ref[...], ka_ref[...],
                      wlw_ref[...], wla_ref[...], wlg_ref[...], _seg_matrix(RW, 6))
        r_ref[...] = m['r']
        w_ref[...] = m['decay']
        k_ref[...] = m['k2']
        v_ref[...] = m['v']
        a_ref[...] = -m['kk']
        b_ref[...] = m['kk'] * m['a']
        g_ref[...] = m['sg']

    vec = _row_spec(tr, RW)
    return _pcall(
        body, name=name, grid=(T // tr,),
        in_specs=[_row_spec(tr, RWKV_PAD), _row_spec(tr, RWKV_PAD), _full_spec((1, RWKV_PAD))] + [_full_spec((1, RW))] * 4
        + [_full_spec((128, RW))] * 3,
        out_specs=[vec] * 6 + [_row_spec(tr, 128)],
        out_shape=[jax.ShapeDtypeStruct((T, RW), F32)] * 6 + [jax.ShapeDtypeStruct((T, 128), F32)],
        compiler_params=_params(("parallel",)),
    )(p, pshift, mu, w0, a0, k_k, k_a, wlw, wla, wlg)


def _group_norm(y, bd):
    mean = _segsum(y, bd) * (1.0 / HEAD_DIM)
    yc = y - mean
    rstd = lax.rsqrt(_segsum(yc * yc, bd) * (1.0 / HEAD_DIM) + GN_EPS)
    return yc * rstd, rstd


def rwkv_post_fwd(y, r, k2, v, sg, wlg, ln_w, ln_b, r_k, *, name):
    T = y.shape[0]
    tr = _tile(T, 256, 8)

    def body(y_ref, r_ref, k_ref, v_ref, sg_ref, wlg_ref, lw_ref, lb_ref, rk_ref, o_ref):
        bd = _seg_matrix(RW, 6)
        yn, _ = _group_norm(y_ref[...], bd)
        s = _segsum(r_ref[...] * k_ref[...] * rk_ref[...], bd)
        g = _bdot(sg_ref[...], wlg_ref[...])
        o_ref[...] = ((yn * lw_ref[...] + lb_ref[...] + s * v_ref[...]) * g).astype(BF16)

    vec = _row_spec(tr, RW)
    return _pcall(body, name=name, grid=(T // tr,),
                  in_specs=[vec] * 4 + [_row_spec(tr, 128), _full_spec((128, RW))] + [_full_spec((1, RW))] * 3, out_specs=vec,
                  out_shape=jax.ShapeDtypeStruct((T, RW), BF16), compiler_params=_params(("parallel",)))(
                      y, r, k2, v, sg, wlg, ln_w, ln_b, r_k)


def rwkv_post_bwd(dyr, y, r, k2, v, sg, wlg, ln_w, ln_b, r_k, *, name):
    T = y.shape[0]
    tr = _tile(T, 256, 8)

    def body(dyr_ref, y_ref, r_ref, k_ref, v_ref, sg_ref, wlg_ref, lw_ref, lb_ref, rk_ref,
             dy_ref, dz_ref, dg_ref, dlw_ref, dlb_ref):
        i = pl.program_id(0)
        bd = _seg_matrix(RW, 6)
        yn, rstd = _group_norm(y_ref[...], bd)
        s = _segsum(r_ref[...] * k_ref[...] * rk_ref[...], bd)
        dyrv = dyr_ref[...]
        dg_ref[...] = dyrv * (yn * lw_ref[...] + lb_ref[...] + s * v_ref[...])
        dz = dyrv * _bdot(sg_ref[...], wlg_ref[...])
        dz_ref[...] = dz
        dyn = dz * lw_ref[...]
        inv = 1.0 / HEAD_DIM
        dy_ref[...] = rstd * (dyn - _segsum(dyn, bd) * inv - yn * (_segsum(dyn * yn, bd) * inv))
        _acc_rows(dlw_ref, jnp.sum(dz * yn, axis=0, keepdims=True), i)
        _acc_rows(dlb_ref, jnp.sum(dz, axis=0, keepdims=True), i)

    vec = _row_spec(tr, RW)
    one = _full_spec((1, RW))
    return _pcall(body, name=name, grid=(T // tr,),
                  in_specs=[vec] * 5 + [_row_spec(tr, 128), _full_spec((128, RW))] + [one] * 3, out_specs=[vec] * 3 + [one] * 2,
                  out_shape=[jax.ShapeDtypeStruct((T, RW), F32)] * 3 + [jax.ShapeDtypeStruct((1, RW), F32)] * 2,
                  compiler_params=_params(("arbitrary",)))(dyr, y, r, k2, v, sg, wlg, ln_w, ln_b, r_k)


def rwkv_pre_bwd(p, pshift, dr_w, dw_w, dk_w, dv_w, da_w, db_w, dz, dg, mu, w0, a0, k_k, k_a, r_k, wlw, wla, wlg, *, name):
    T = p.shape[0]
    tr = _tile(T, 256, 8)
    n = T // tr

    def body(p_ref, ps_ref, dr_ref, dw_ref, dk_ref, dv_ref, da_ref, db_ref, dz_ref, dg_ref,
             mu_ref, w0_ref, a0_ref, kk_ref, ka_ref, rk_ref, wlw_ref, wla_ref, wlg_ref,
             dp_ref, dmu_ref, dw0_ref, da0_ref, dkk_ref, dka_ref, drk_ref, dwlw_ref, dwla_ref, dwlg_ref,
             carry, dpp, acc_w, acc_a, acc_g):
        i = pl.program_id(0)

        @pl.when(i == 0)
        def _():
            carry[...] = jnp.zeros_like(carry)

        pv, prev, mu = p_ref[...], ps_ref[...], mu_ref[...]
        bd = _seg_matrix(RW, 6)
        k_k, k_a, r_k = kk_ref[...], ka_ref[...], rk_ref[...]
        m = _rwkv_mix(pv, prev, mu, w0_ref[...], a0_ref[...], k_k, k_a, wlw_ref[...], wla_ref[...], wlg_ref[...], bd)
        r, k, v, a, kk, k2 = m['r'], m['k'], m['v'], m['a'], m['kk'], m['k2']
        dzv, dgv = dz_ref[...], dg_ref[...]
        s = _segsum(r * k2 * r_k, bd)
        ds = _segsum(dzv * v, bd)
        dr = dr_ref[...] + ds * k2 * r_k
        dk2 = dk_ref[...] + ds * r * r_k
        dv = dv_ref[...] + dzv * s
        dbv = db_ref[...]
        dkk = dbv * a - da_ref[...]
        da = dbv * kk + dk2 * k * k_a
        dk = dk2 * (1.0 + (a - 1.0) * k_a)
        nmax = jnp.maximum(m['n'], 1e-12)
        dkkr = jnp.where(m['n'] > 1e-12, dkk - kk * _segsum(dkk * kk, bd), dkk) / nmax
        dk = dk + dkkr * k_k
        dapre = da * a * (1.0 - a)
        dwpre = dw_ref[...] * m['decay'] * (-m['e']) * _sigmoid(m['z'])
        dth = _bdot(dwpre, wlw_ref[...], _NT)
        dxa = _bdot(dapre, wla_ref[...], _NT)
        dsg = _bdot(dgv, wlg_ref[...], _NT)
        dpp[:, 0:RW] = dr
        dpp[:, RW:2 * RW] = dk
        dpp[:, 2 * RW:3 * RW] = dv
        dpp[:, 3 * RW:3 * RW + 128] = dth * (1.0 - m['th'] * m['th'])
        dpp[:, 3 * RW + 128:3 * RW + 256] = dxa
        dpp[:, 3 * RW + 256:3 * RW + 384] = dsg * m['sg'] * (1.0 - m['sg'])
        d = dpp[...]
        zed = d * mu
        last = lax.broadcasted_iota(jnp.int32, pv.shape, 0) == tr - 1
        dp_ref[...] = d * (1.0 - mu) + jnp.where(last, carry[0:1, :], pltpu.roll(zed, tr - 1, 0))
        carry[...] = zed[0:8, :]

        def colsum(x):
            return jnp.sum(x, axis=0, keepdims=True)

        _acc_rows(dmu_ref, colsum(d * (prev - pv)), i)
        _acc_rows(dw0_ref, colsum(dwpre), i)
        _acc_rows(da0_ref, colsum(dapre), i)
        _acc_rows(dkk_ref, colsum(dkkr * k), i)
        _acc_rows(dka_ref, colsum(dk2 * k * (a - 1.0)), i)
        _acc_rows(drk_ref, colsum(ds * r * k2), i)
        _acc_rows(acc_w, _bdot(m['th'], dwpre, _TN), i)
        _acc_rows(acc_a, _bdot(m['xa'], dapre, _TN), i)
        _acc_rows(acc_g, _bdot(m['sg'], dgv, _TN), i)

        @pl.when(i == n - 1)
        def _():
            dwlw_ref[...] = acc_w[...]
            dwla_ref[...] = acc_a[...]
            dwlg_ref[...] = acc_g[...]

    rev = lambda c: pl.BlockSpec((tr, c), lambda i: (n - 1 - i, 0))
    one, lora = _full_spec((1, RW)), _full_spec((128, RW))
    return _pcall(
        body, name=name, grid=(n,),
        in_specs=[rev(RWKV_PAD), rev(RWKV_PAD)] + [rev(RW)] * 8 + [_full_spec((1, RWKV_PAD))] + [one] * 5 + [lora] * 3,
        out_specs=[rev(RWKV_PAD), _full_spec((1, RWKV_PAD))] + [one] * 5 + [lora] * 3,
        out_shape=[jax.ShapeDtypeStruct((T, RWKV_PAD), F32), jax.ShapeDtypeStruct((1, RWKV_PAD), F32)]
        + [jax.ShapeDtypeStruct((1, RW), F32)] * 5 + [jax.ShapeDtypeStruct((128, RW), F32)] * 3,
        scratch_shapes=[pltpu.VMEM((8, RWKV_PAD), F32), pltpu.VMEM((tr, RWKV_PAD), F32)] + [pltpu.VMEM((128, RW), F32)] * 3,
        compiler_params=_params(("arbitrary",)),
    )(p, pshift, dr_w, dw_w, dk_w, dv_w, da_w, db_w, dz, dg, mu, w0, a0, k_k, k_a, r_k, wlw, wla, wlg)


def _qk_norm(x, g, bd):
    r = lax.rsqrt(_segsum(x * x, bd) * (1.0 / HEAD_DIM) + RMS_EPS)
    return x * r * g, r


def _att_mask(i):
    qi = lax.broadcasted_iota(jnp.int32, (BLOCK, 2 * BLOCK), 0)
    kj = lax.broadcasted_iota(jnp.int32, (BLOCK, 2 * BLOCK), 1)
    band = (kj <= qi + BLOCK) & (kj > qi + BLOCK - WINDOW)
    return band & ((kj >= BLOCK) | (i > 0))


_HQK = (((2,), (2,)), ((0,), (0,)))
_HPV = (((2,), (1,)), ((0,), (0,)))
_HTN = (((1,), (1,)), ((0,), (0,)))


def _heads(x, n):
    return jnp.stack([x[:, h * HEAD_DIM:(h + 1) * HEAD_DIM] for h in range(n)])


def _unheads(x3):
    return jnp.concatenate([x3[h] for h in range(x3.shape[0])], axis=1)


def _kv_heads(x):
    x2 = _heads(x, KVW // HEAD_DIM)
    return jnp.concatenate([x2[g:g + 1] for g in range(KVW // HEAD_DIM) for _ in range(ATT_GROUP)], axis=0)


def _sinks3(sk):
    return jnp.stack([sk[0:1, h:h + 1] for h in range(N_HEADS)])


def _att_probs(q3, k3, mask, sink):
    s = _bdot(q3, k3, _HQK) * (HEAD_DIM ** -0.5)
    s = jnp.where(mask[None], s, NEG_BIG)
    m = jnp.maximum(jnp.max(s, axis=-1, keepdims=True), sink)
    pexp = jnp.exp(s - m)
    psink = jnp.exp(sink - m)
    inv = 1.0 / (jnp.sum(pexp, axis=-1, keepdims=True) + psink)
    return pexp * inv, psink * inv


def _att_blocks(n):
    cur = pl.BlockSpec((BLOCK, ATT_COLS), lambda i: (i, 0))
    prev = pl.BlockSpec((BLOCK, ATT_COLS), lambda i: (jnp.maximum(i - 1, 0), 0))
    return cur, prev


def _att_qkv(cur, prev, qn_g, kn_g):
    bq, bk = _seg_matrix(RW, 6), _seg_matrix(KVW, 6)
    qn, rq = _qk_norm(cur[:, 0:RW], qn_g, bq)
    kcur, rkc = _qk_norm(cur[:, RW:RW + KVW], kn_g, bk)
    kprev, _ = _qk_norm(prev[:, RW:RW + KVW], kn_g, bk)
    kc = jnp.concatenate([kprev, kcur], axis=0)
    vc = jnp.concatenate([prev[:, RW + KVW:], cur[:, RW + KVW:]], axis=0)
    return qn, rq, kc, vc, rkc


def att_fwd(pa, qn_g, kn_g, sinks, *, name):
    T = pa.shape[0]
    n = T // BLOCK

    def body(cur_ref, prev_ref, qg_ref, kg_ref, sk_ref, o_ref):
        i = pl.program_id(0)
        qn, _, kc, vc, _ = _att_qkv(cur_ref[...], prev_ref[...], qg_ref[...], kg_ref[...])
        probs, _ = _att_probs(_heads(qn, N_HEADS), _kv_heads(kc), _att_mask(i), _sinks3(sk_ref[...]))
        o_ref[...] = _unheads(_bdot(probs, _kv_heads(vc), _HPV))

    cur, prev = _att_blocks(n)
    return _pcall(body, name=name, grid=(n,),
                  in_specs=[cur, prev, _full_spec((1, RW)), _full_spec((1, KVW)), _full_spec((1, LANE))],
                  out_specs=pl.BlockSpec((BLOCK, RW), lambda i: (i, 0)), out_shape=jax.ShapeDtypeStruct((T, RW), F32),
                  compiler_params=_params(("parallel",)))(pa, pa, qn_g, kn_g, sinks)


def att_bwd(pa, do, qn_g, kn_g, sinks, *, name):
    T = pa.shape[0]
    n = T // BLOCK

    def body(cur_ref, prev_ref, do_ref, qg_ref, kg_ref, sk_ref,
             dq_ref, dko_ref, dkn_ref, dvo_ref, dvn_ref, dqg_ref, dsk_ref):
        i = pl.program_id(0)
        cur = cur_ref[...]
        qn, rq, kc, vc, _ = _att_qkv(cur, prev_ref[...], qg_ref[...], kg_ref[...])
        q3, k3, v3, do3 = _heads(qn, N_HEADS), _kv_heads(kc), _kv_heads(vc), _heads(do_ref[...], N_HEADS)
        probs, psink = _att_probs(q3, k3, _att_mask(i), _sinks3(sk_ref[...]))
        dprobs = _bdot(do3, v3, _HQK)
        delta = jnp.sum(probs * dprobs, axis=-1, keepdims=True)
        ds = probs * (dprobs - delta) * (HEAD_DIM ** -0.5)
        dsink3 = -jnp.sum(psink * delta, axis=1, keepdims=True)
        lane = lax.broadcasted_iota(jnp.int32, (1, LANE), 1)
        dsink = jnp.zeros((1, LANE), F32)
        for h in range(N_HEADS):
            dsink = dsink + jnp.where(lane == h, dsink3[h], 0.0)
        dqn = _unheads(_bdot(ds, k3, _HPV))

        def per_kv_head(x3):
            groups = [sum(x3[g * ATT_GROUP + j] for j in range(ATT_GROUP)) for g in range(KVW // HEAD_DIM)]
            return jnp.concatenate(groups, axis=1)

        dk, dv = per_kv_head(_bdot(ds, q3, _HTN)), per_kv_head(_bdot(probs, do3, _HTN))
        dkn_ref[...], dko_ref[...] = dk[0:BLOCK], dk[BLOCK:]
        dvn_ref[...], dvo_ref[...] = dv[0:BLOCK], dv[BLOCK:]
        qhat = cur[:, 0:RW] * rq
        dqh = dqn * qg_ref[...]
        dq_ref[...] = rq * (dqh - qhat * (_segsum(dqh * qhat, _seg_matrix(RW, 6)) * (1.0 / HEAD_DIM)))
        prod = dqn * qhat
        fold = prod[:, 0:HEAD_DIM]
        for h in range(1, N_HEADS):
            fold = fold + prod[:, h * HEAD_DIM:(h + 1) * HEAD_DIM]
        _acc_rows(dqg_ref, jnp.sum(fold, axis=0, keepdims=True), i)
        _acc_rows(dsk_ref, dsink, i)

    cur, prev = _att_blocks(n)
    kvb = pl.BlockSpec((BLOCK, KVW), lambda i: (i, 0))
    qb = pl.BlockSpec((BLOCK, RW), lambda i: (i, 0))
    return _pcall(body, name=name, grid=(n,),
                  in_specs=[cur, prev, qb, _full_spec((1, RW)), _full_spec((1, KVW)), _full_spec((1, LANE))],
                  out_specs=[qb, kvb, kvb, kvb, kvb, _full_spec((1, HEAD_DIM)), _full_spec((1, LANE))],
                  out_shape=[jax.ShapeDtypeStruct((T, RW), F32)] + [jax.ShapeDtypeStruct((T, KVW), F32)] * 4
                  + [jax.ShapeDtypeStruct((1, HEAD_DIM), F32), jax.ShapeDtypeStruct((1, LANE), F32)],
                  compiler_params=_params(("arbitrary",)))(pa, pa, do, qn_g, kn_g, sinks)


def att_kv_bwd(pa, dq, dko, dkn, dvo, dvn, kn_g, *, name):
    T = pa.shape[0]
    n = T // BLOCK

    def body(pa_ref, dq_ref, dko_ref, dkn_ref, dvo_ref, dvn_ref, kg_ref, dpa_ref, dkg_ref):
        i = pl.program_id(0)
        more = i < n - 1
        dkn_tot = dko_ref[...] + jnp.where(more, dkn_ref[...], 0.0)
        dv_tot = dvo_ref[...] + jnp.where(more, dvn_ref[...], 0.0)
        kraw = pa_ref[:, RW:RW + KVW]
        bk = _seg_matrix(KVW, 6)
        _, rk = _qk_norm(kraw, kg_ref[...], bk)
        khat = kraw * rk
        dkh = dkn_tot * kg_ref[...]
        dpa_ref[:, 0:RW] = dq_ref[...]
        dpa_ref[:, RW:RW + KVW] = rk * (dkh - khat * (_segsum(dkh * khat, bk) * (1.0 / HEAD_DIM)))
        dpa_ref[:, RW + KVW:] = dv_tot
        prod = dkn_tot * khat
        _acc_rows(dkg_ref, jnp.sum(prod[:, 0:HEAD_DIM] + prod[:, HEAD_DIM:], axis=0, keepdims=True), i)

    kvb = pl.BlockSpec((BLOCK, KVW), lambda i: (i, 0))
    nxt = pl.BlockSpec((BLOCK, KVW), lambda i: (jnp.minimum(i + 1, n - 1), 0))
    return _pcall(body, name=name, grid=(n,),
                  in_specs=[pl.BlockSpec((BLOCK, ATT_COLS), lambda i: (i, 0)), pl.BlockSpec((BLOCK, RW), lambda i: (i, 0)),
                            kvb, nxt, kvb, nxt, _full_spec((1, KVW))],
                  out_specs=[pl.BlockSpec((BLOCK, ATT_COLS), lambda i: (i, 0)), _full_spec((1, HEAD_DIM))],
                  out_shape=[jax.ShapeDtypeStruct((T, ATT_COLS), F32), jax.ShapeDtypeStruct((1, HEAD_DIM), F32)],
                  compiler_params=_params(("arbitrary",)))(pa, dq, dko, dkn, dvo, dvn, kn_g)


WKV_CHUNK = 64
WKV_GROUP = 8


def _diag_mask():
    i = lax.broadcasted_iota(jnp.int32, (HEAD_DIM, RW), 0)
    j = lax.broadcasted_iota(jnp.int32, (HEAD_DIM, RW), 1) & (HEAD_DIM - 1)
    return i == j


def _headsums(xs, pieces):
    half = RW // 2
    bd = _seg_matrix(half, 6)
    bd = jnp.concatenate([bd] * pieces, axis=0)
    rows = []
    for x in xs:
        parts, rest = [], x
        for n in range(pieces):
            p = rest.astype(BF16)
            parts.append(p)
            if n + 1 < pieces:
                rest = rest - p.astype(F32)
        for sl in (slice(0, half), slice(half, RW)):
            rows.append(jnp.concatenate([p[:, sl] for p in parts], axis=1))
    out = lax.dot_general(jnp.concatenate(rows, axis=0), bd, (((1,), (0,)), ((), ())), preferred_element_type=F32)
    return [jnp.concatenate([out[2 * n * HEAD_DIM:(2 * n + 1) * HEAD_DIM], out[(2 * n + 1) * HEAD_DIM:(2 * n + 2) * HEAD_DIM]],
                            axis=1) for n in range(len(xs))]


def _headsum(x):
    low = lax.broadcasted_iota(jnp.int32, (HEAD_DIM, LANE), 1) < HEAD_DIM
    tiles = []
    for c in range(RW // LANE):
        xt = x[:, c * LANE:(c + 1) * LANE]
        s_lo = jnp.sum(jnp.where(low, xt, 0.0), axis=1, keepdims=True)
        s_hi = jnp.sum(jnp.where(low, 0.0, xt), axis=1, keepdims=True)
        tiles.append(jnp.where(low, s_lo, s_hi))
    return jnp.concatenate(tiles, axis=1)


def _cols(rows, diag):
    return _headsums([jnp.where(diag, r, 0.0) for r in rows], 2)


def _row(x, diag):
    return jnp.sum(jnp.where(diag, x, 0.0), axis=0, keepdims=True)


def wkv_fwd(r, w, k, v, a, b, *, name, gather=()):
    T = r.shape[0]
    ch = min(WKV_CHUNK, T)
    ngroups = ch // WKV_GROUP
    nchunks = T // ch
    ng = len(gather)

    def body(*refs):
        r_ref, w_ref, k_ref, v_ref, a_ref, b_ref = refs[:6]
        y_ref, st_ref = refs[6 + ng:8 + ng]
        s_scr = refs[8 + 2 * ng]
        step = pl.program_id(0)
        if ng:
            plan = _GatherPlan(refs[6:6 + ng], refs[8 + ng:8 + 2 * ng], refs[9 + 2 * ng:])
            pl.when(step == 0)(plan.start)
            pl.when(step == nchunks // 2)(plan.relay)

        @pl.when(step == 0)
        def _():
            s_scr[...] = jnp.zeros_like(s_scr)

        diag = _diag_mask()

        def group(gi, S):
            t0 = pl.multiple_of(gi * WKV_GROUP, WKV_GROUP)
            rows = pl.ds(t0, WKV_GROUP)
            R, W, K, V, A, B = (ref[rows, :] for ref in (r_ref, w_ref, k_ref, v_ref, a_ref, b_ref))
            vcols = _cols([V[s:s + 1] for s in range(WKV_GROUP)], diag)
            yrows = []
            for s in range(WKV_GROUP):
                sa = _headsum(S * A[s:s + 1])
                S = S * W[s:s + 1] + sa * B[s:s + 1] + vcols[s] * K[s:s + 1]
                st_ref[t0 + s] = S
                yrows.append(_row(_headsums([S * R[s:s + 1]], 2)[0], diag))
            y_ref[rows, :] = jnp.concatenate(yrows, axis=0)
            return S

        s_scr[...] = lax.fori_loop(0, ngroups, group, s_scr[...])
        if ng:
            pl.when(step == nchunks - 1)(plan.finish_relayed)

    vec = pl.BlockSpec((ch, RW), lambda c: (c, 0))
    return _pcall(
        body, name=name, grid=(nchunks,), in_specs=[vec] * 6 + [_HBM] * ng,
        out_specs=[vec, pl.BlockSpec((ch, HEAD_DIM, RW), lambda c: (c, 0, 0))] + [_HBM] * ng,
        out_shape=[jax.ShapeDtypeStruct((T, RW), F32), jax.ShapeDtypeStruct((T, HEAD_DIM, RW), F32)] + _gathered_shapes(gather),
        scratch_shapes=[pltpu.VMEM((HEAD_DIM, RW), F32)] + (_GatherPlan.sems(ng) if ng else []),
        compiler_params=_params(("arbitrary",)),
    )(r, w, k, v, a, b, *gather)


def wkv_bwd(r, w, k, v, a, b, dy, states, *, name, exchange=()):
    T = r.shape[0]
    ch = min(WKV_CHUNK, T)
    nchunks = T // ch
    ngroups = ch // WKV_GROUP
    ne = len(exchange)

    def body(*refs):
        r_ref, w_ref, k_ref, v_ref, a_ref, b_ref, dy_ref, st_ref, stp_ref = refs[:9]
        dr_ref, dw_ref, dk_ref, dv_ref, da_ref, db_ref = refs[9 + ne:15 + ne]
        ds_scr = refs[15 + 2 * ne]
        step = pl.program_id(0)
        if ne:
            plan = _ExchangePlan(refs[9:9 + ne], refs[15 + ne:15 + 2 * ne], refs[16 + 2 * ne:])
            pl.when(step == 0)(plan.start)

        @pl.when(step == 0)
        def _():
            ds_scr[...] = jnp.zeros_like(ds_scr)

        has_prev_chunk = step < nchunks - 1
        diag = _diag_mask()
        colsum = lambda x: jnp.sum(x, axis=0, keepdims=True)

        def group(gj, dS):
            gi = ngroups - 1 - gj
            t0 = pl.multiple_of(gi * WKV_GROUP, WKV_GROUP)
            rows = pl.ds(t0, WKV_GROUP)
            R, W, K, V, A, B, DY = (ref[rows, :] for ref in (r_ref, w_ref, k_ref, v_ref, a_ref, b_ref, dy_ref))
            before = jnp.where(gi > 0, st_ref[jnp.maximum(t0 - 1, 0)], jnp.where(has_prev_chunk, stp_ref[0], 0.0))
            prev_state = lambda s: st_ref[t0 + s - 1] if s > 0 else before
            steps = range(WKV_GROUP)
            dycols = _cols([DY[s:s + 1] for s in steps], diag)
            vcols = _cols([V[s:s + 1] for s in steps], diag)
            sas = _headsums([prev_state(s) * A[s:s + 1] for s in steps], 1)
            got = [[None] * WKV_GROUP for _ in range(6)]
            for s in reversed(steps):
                Sp = prev_state(s)
                dS = dS + dycols[s] * R[s:s + 1]
                got[0][s] = colsum(st_ref[t0 + s] * dycols[s])
                got[3][s] = _row(_headsum(dS * K[s:s + 1]), diag)
                got[2][s] = colsum(dS * vcols[s])
                dsa = _headsum(dS * B[s:s + 1])
                got[5][s] = colsum(dS * sas[s])
                got[1][s] = colsum(dS * Sp)
                got[4][s] = colsum(Sp * dsa)
                dS = dS * W[s:s + 1] + dsa * A[s:s + 1]
            for q, ref in enumerate((dr_ref, dw_ref, dk_ref, dv_ref, da_ref, db_ref)):
                ref[rows, :] = jnp.concatenate(got[q], axis=0)
            return dS

        ds_scr[...] = lax.fori_loop(0, ngroups, group, ds_scr[...])
        if ne:
            pl.when(step == nchunks - 1)(plan.finish)

    vec = pl.BlockSpec((ch, RW), lambda c: (nchunks - 1 - c, 0))
    st_spec = pl.BlockSpec((ch, HEAD_DIM, RW), lambda c: (nchunks - 1 - c, 0, 0))
    stp_spec = pl.BlockSpec((1, HEAD_DIM, RW), lambda c: (jnp.maximum((nchunks - 1 - c) * ch - 1, 0), 0, 0))
    return _pcall(
        body, name=name, grid=(nchunks,), in_specs=[vec] * 7 + [st_spec, stp_spec] + [_HBM] * ne,
        out_specs=[vec] * 6 + [_HBM] * ne,
        out_shape=[jax.ShapeDtypeStruct((T, RW), F32)] * 6 + [jax.ShapeDtypeStruct(e.shape, e.dtype) for e in exchange],
        scratch_shapes=[pltpu.VMEM((HEAD_DIM, RW), F32)] + (_ExchangePlan.sems(ne) if ne else []),
        compiler_params=_params(("arbitrary",)),
    )(r, w, k, v, a, b, dy, states, states, *exchange)


_HBM = pl.BlockSpec(memory_space=pltpu.HBM)
_MESH = pl.DeviceIdType.MESH


def _place():
    x, y, c = lax.axis_index("x"), lax.axis_index("y"), lax.axis_index("c")
    return x, y, c, [(1 - x, y), (x, 1 - y), (1 - x, 1 - y)]


def _remote(src, dst, send_sem, recv_sem, to):
    return pltpu.make_async_remote_copy(src_ref=src, dst_ref=dst, send_sem=send_sem, recv_sem=recv_sem, device_id=to,
                                        device_id_type=_MESH)


def _dma_sems(*counts):
    return [pltpu.SemaphoreType.DMA((n,)) for n in counts]


class _GatherPlan:
    def __init__(self, ins, outs, sems):
        self.ins, self.outs, self.n = ins, outs, len(ins)
        self.ici_send, self.ici_recv, self.d2d_send, self.d2d_recv, self.local_sems = sems
        x, y, c, chips = _place()
        self.c, self.me, self.sibling = c, 2 * x + y, (x, y, 1 - c)
        self.peers = [(2 * qx + qy, (qx, qy, c)) for qx, qy in chips]

    @staticmethod
    def sems(n):
        return _dma_sems(3 * n, 3 * n, 3 * n, 3 * n, n)

    def _half(self, i, which):
        rh = self.ins[i].shape[0] // 2
        return pl.ds(which * rh, rh)

    def _local(self, i):
        return pltpu.make_async_copy(self.ins[i], self.outs[i].at[self.me], self.local_sems.at[i])

    def _send(self, i, j):
        k, mine = 3 * i + j, self._half(i, self.c)
        return _remote(self.ins[i].at[mine], self.outs[i].at[self.me, mine], self.ici_send.at[k], self.ici_recv.at[k],
                       self.peers[j][1])

    def _landed(self, i, j):
        k, piece = 3 * i + j, self.outs[i].at[self.peers[j][0], self._half(i, self.c)]
        return _remote(piece, piece, self.ici_send.at[k], self.ici_recv.at[k], self.peers[j][1])

    def _pass(self, i, j, which):
        k, piece = 3 * i + j, self.outs[i].at[self.peers[j][0], self._half(i, which)]
        return _remote(piece, piece, self.d2d_send.at[k], self.d2d_recv.at[k], self.sibling)

    def _all(self):
        return [(i, j) for i in range(self.n) for j in range(3)]

    def start(self):
        for i in range(self.n):
            self._local(i).start()
        for i, j in self._all():
            self._send(i, j).start()

    def relay(self):
        for i, j in self._all():
            self._landed(i, j).wait_recv()
            self._pass(i, j, self.c).start()

    def finish_relayed(self):
        for i, j in self._all():
            self._pass(i, j, 1 - self.c).wait_recv()
        for i, j in self._all():
            self._send(i, j).wait_send()
            self._pass(i, j, self.c).wait_send()
        for i in range(self.n):
            self._local(i).wait()

    def finish(self):
        self.relay()
        self.finish_relayed()

    @staticmethod
    def out_shapes(shards):
        return _gathered_shapes(shards)


def _gathered_shapes(shards):
    return [jax.ShapeDtypeStruct((N_CHIPS,) + s.shape, s.dtype) for s in shards]


def gather_weights(shards, *, name):
    n = len(shards)

    def body(*refs):
        plan = _GatherPlan(refs[:n], refs[n:2 * n], refs[2 * n:])
        plan.start()
        plan.finish()

    return _pcall(body, name=name, in_specs=[_HBM] * n, out_specs=[_HBM] * n, out_shape=_gathered_shapes(shards),
                  scratch_shapes=_GatherPlan.sems(n), compiler_params=_params())(*shards)


def to_sibling(arrays, take_other_half, *, name):
    n = len(arrays)

    def body(*refs):
        ins, outs = refs[:n], refs[n:2 * n]
        send_sems, recv_sems = refs[2 * n:]
        x, y, c, _ = _place()
        cps = []
        for i in range(n):
            src = ins[i]
            if take_other_half:
                rh = src.shape[1] // 2
                src = src.at[:, pl.ds((1 - c) * rh, rh)]
            cps.append(_remote(src, outs[i], send_sems.at[i], recv_sems.at[i], (x, y, 1 - c)))
        for cp in cps:
            cp.start()
        for cp in cps:
            cp.wait_recv()
        for cp in cps:
            cp.wait_send()

    def out_of(a):
        shape = (a.shape[0], a.shape[1] // 2, a.shape[2]) if take_other_half else a.shape
        return jax.ShapeDtypeStruct(shape, a.dtype)

    return _pcall(body, name=name, in_specs=[_HBM] * n, out_specs=[_HBM] * n, out_shape=[out_of(a) for a in arrays],
                  scratch_shapes=_dma_sems(n, n), compiler_params=_params())(*arrays)


def exchange_chips(arrays, *, name):
    n = len(arrays)

    def body(*refs):
        plan = _ExchangePlan(refs[:n], refs[n:2 * n], refs[2 * n:])
        plan.start()
        plan.finish()

    return _pcall(body, name=name, in_specs=[_HBM] * n, out_specs=[_HBM] * n,
                  out_shape=[jax.ShapeDtypeStruct(a.shape, a.dtype) for a in arrays],
                  scratch_shapes=_ExchangePlan.sems(n), compiler_params=_params())(*arrays)


class _ExchangePlan:
    def __init__(self, ins, outs, sems):
        self.ins, self.outs, self.n = ins, outs, len(ins)
        self.send_sems, self.recv_sems, self.local_sems = sems
        x, y, c, chips = _place()
        self.me = 2 * x + y
        self.peers = [(2 * qx + qy, (qx, qy, c)) for qx, qy in chips]

    @staticmethod
    def sems(n):
        return _dma_sems(3 * n, 3 * n, n)

    @staticmethod
    def out_shapes(arrays):
        return [jax.ShapeDtypeStruct(a.shape, a.dtype) for a in arrays]

    def _local(self, i):
        return pltpu.make_async_copy(self.ins[i].at[self.me], self.outs[i].at[self.me], self.local_sems.at[i])

    def _send(self, i, j):
        k = 3 * i + j
        return _remote(self.ins[i].at[self.peers[j][0]], self.outs[i].at[self.me], self.send_sems.at[k], self.recv_sems.at[k],
                       self.peers[j][1])

    def _landed(self, i, j):
        k, piece = 3 * i + j, self.outs[i].at[self.peers[j][0]]
        return _remote(piece, piece, self.send_sems.at[k], self.recv_sems.at[k], self.peers[j][1])

    def start(self):
        for i in range(self.n):
            self._local(i).start()
            for j in range(3):
                self._send(i, j).start()

    def finish(self):
        for i in range(self.n):
            for j in range(3):
                self._landed(i, j).wait_recv()
        for i in range(self.n):
            for j in range(3):
                self._send(i, j).wait_send()
            self._local(i).wait()


def _core_index():
    return lax.axis_index("c").astype(jnp.int32).reshape(1)


def pair_sum(g, theirs, wire_dtype, *, name):
    _, R, C = g.shape
    rh = R // 2
    tr = _tile(rh, 256, 16)
    nt = rh // tr

    def body(c_ref, g_ref, t_ref, q_ref, qw_ref):
        q = g_ref[...] + t_ref[...]
        q_ref[...] = q
        qw_ref[...] = q.astype(wire_dtype)

    blk = pl.BlockSpec((1, tr, C), lambda b, i, c_ref: (b, i, 0))
    mine = pl.BlockSpec((1, tr, C), lambda b, i, c_ref: (b, c_ref[0] * nt + i, 0))
    grid_spec = pltpu.PrefetchScalarGridSpec(num_scalar_prefetch=1, grid=(N_CHIPS, nt), in_specs=[mine, blk], out_specs=[blk, blk])
    return _pcall(body, name=name, grid_spec=grid_spec,
                  out_shape=[jax.ShapeDtypeStruct((N_CHIPS, rh, C), F32), jax.ShapeDtypeStruct((N_CHIPS, rh, C), wire_dtype)],
                  compiler_params=_params(("parallel", "parallel")))(_core_index(), g, theirs)


def half_sum(own, landed, *, name):
    _, rh, C = own.shape
    tr = _tile(rh, 256, 16)

    def body(me_ref, own_ref, land_ref, o_ref):
        total = None
        for p in range(N_CHIPS):
            term = jnp.where(me_ref[0] == p, own_ref[p], land_ref[p].astype(F32))
            total = term if total is None else total + term
        o_ref[...] = total

    blk = pl.BlockSpec((N_CHIPS, tr, C), lambda i, me_ref: (0, i, 0))
    grid_spec = pltpu.PrefetchScalarGridSpec(num_scalar_prefetch=1, grid=(rh // tr,), in_specs=[blk, blk],
                                             out_specs=pl.BlockSpec((tr, C), lambda i, me_ref: (i, 0)))
    me = (2 * lax.axis_index("x") + lax.axis_index("y")).astype(jnp.int32).reshape(1)
    return _pcall(body, name=name, grid_spec=grid_spec, out_shape=jax.ShapeDtypeStruct((rh, C), F32),
                  compiler_params=_params(("parallel",)))(me, own, landed)


def adamw(w, m, v, mine, theirs, *, name):
    R, C = w.shape
    rh = R // 2
    tr = _tile(rh, 256, 8)
    nt = rh // tr

    def body(c_ref, w_ref, m_ref, v_ref, a_ref, b_ref, g_ref, d_ref, nm_ref, nv_ref):
        is_mine = (pl.program_id(0) // nt) == c_ref[0]
        g = jnp.where(is_mine, a_ref[...], b_ref[...])
        g_ref[...] = g
        nm = ADAM_B1 * m_ref[...] + (1.0 - ADAM_B1) * g
        nv = ADAM_B2 * v_ref[...] + (1.0 - ADAM_B2) * (g * g)
        nm_ref[...] = nm
        nv_ref[...] = nv
        m_hat = nm / (1.0 - ADAM_B1 ** ADAM_STEP)
        v_hat = nv / (1.0 - ADAM_B2 ** ADAM_STEP)
        d_ref[...] = -ADAM_LR * (m_hat / (jnp.sqrt(v_hat) + ADAM_EPS) + ADAM_WD * w_ref[...])

    full = pl.BlockSpec((tr, C), lambda i, c_ref: (i, 0))
    part = pl.BlockSpec((tr, C), lambda i, c_ref: (i % nt, 0))
    grid_spec = pltpu.PrefetchScalarGridSpec(num_scalar_prefetch=1, grid=(2 * nt,), in_specs=[full] * 3 + [part] * 2,
                                             out_specs=[full] * 4)
    return _pcall(body, name=name, grid_spec=grid_spec, out_shape=[jax.ShapeDtypeStruct((R, C), F32)] * 4,
                  compiler_params=_params(("parallel",)))(_core_index(), w, m, v, mine, theirs)


def _to_blocks(full, axis):
    r, c = full.shape
    if axis == 1:
        return full.reshape(r, N_CHIPS, c // N_CHIPS).transpose(1, 0, 2)
    return full.reshape(N_CHIPS, r // N_CHIPS, c)


def _from_blocks(blocks, axis):
    _, r, c = blocks.shape
    if axis == 1:
        return blocks.transpose(1, 0, 2).reshape(r, N_CHIPS * c)
    return blocks.reshape(N_CHIPS * r, c)


def _ffn_fwd(x, norm, wg, wu, wd, tag, carry=None):
    h = rms_fwd(x, norm, name=tag + "_norm")
    gate, up, act, *carried = mm_fused(h, [wg, wu], _swiglu, [BF16] * 3, name=tag + "_gate_up", carry=carry)
    out = mm(act, wd, scale=0.5, res=x, name=tag + "_down")
    return out, (h, gate, up, act), carried


def _ffn_bwd(dout, x, saved, norm, wg, wu, wd, tag, carry=None):
    h, gate, up, act = saved
    dgate, dup, *carried = mm_fused(dout, [wd], _swiglu_bwd, [BF16] * 2, tb=True, extras=[gate, up], name=tag + "_dact",
                                    carry=carry)
    dwd = mm(act, dout, ta=True, scale=0.5, name=tag + "_dwd")
    dwg = mm(h, dgate, ta=True, name=tag + "_dwg")
    dwu = mm(h, dup, ta=True, name=tag + "_dwu")
    dh = mm(dgate, wg, tb=True, name=tag + "_dh_gate")
    dh = mm(dup, wu, tb=True, res=dh, name=tag + "_dh_up")
    dx, dnorm = rms_bwd(dh, x, norm, dout, name=tag + "_dnorm")
    return dx, dnorm, dwg, dwu, dwd, carried


FIRST_WEIGHTS = ['ffn1_w_gate', 'ffn1_w_up', 'ffn1_w_down']
MID_WEIGHTS = ['w_in', 'rwkv_w_lora_up', 'rwkv_a_lora_up', 'rwkv_g_lora_up']
LATE_WEIGHTS = ['w_branch_rwkv', 'w_branch_attn', 'w_out', 'ffn2_w_gate', 'ffn2_w_up', 'ffn2_w_down']


def _pair_sums(names, blocks, tag):
    from_sibling = to_sibling(blocks, True, name=tag + "_grads_to_sibling")
    return [pair_sum(g, t, F32 if n == 'small' else BF16, name="pair_sum_" + n)
            for g, t, n in zip(blocks, from_sibling, names)]


def _step(A):
    x, tgt = A['x'][0], A['loss_target'][0]
    T = x.shape[0]
    w = {n: A[n][0] for n in WEIGHT_NAMES}
    row = lambda a: a.reshape(1, -1)

    axis_of = dict(BIG)
    shard = lambda n: w[n].astype(BF16)
    n1, nmix, n2, nfin = (row(w[n]) for n in ('ffn1_norm', 'mix_norm', 'ffn2_norm', 'final_norm'))
    gathered = gather_weights([shard(n) for n in FIRST_WEIGHTS], name="gather_weights")
    full = {n: _from_blocks(b, axis_of[n]) for n, b in zip(FIRST_WEIGHTS, gathered)}
    x1, ffn1, gathered = _ffn_fwd(x, n1, full['ffn1_w_gate'], full['ffn1_w_up'], full['ffn1_w_down'], "ffn1",
                                  carry=(_GatherPlan, [shard(n) for n in MID_WEIGHTS]))
    full.update({n: _from_blocks(b, axis_of[n]) for n, b in zip(MID_WEIGHTS, gathered)})
    w_in_r = _pad_rwkv_cols(full['w_in'][:, :RWKV_COLS])
    w_in_a = full['w_in'][:, RWKV_COLS:RWKV_COLS + ATT_COLS]
    w_in_g = full['w_in'][:, RWKV_COLS + ATT_COLS:]
    wlw, wla, wlg = (_pad_rows(full[n], 128).astype(F32) for n in ('rwkv_w_lora_up', 'rwkv_a_lora_up', 'rwkv_g_lora_up'))
    mu = _pad_rwkv_cols(row(w['rwkv_mu']))
    w0, a0, k_k, k_a, r_k, ln_w, ln_b = (row(w[n]) for n in ('rwkv_w0', 'rwkv_a0', 'rwkv_k_k', 'rwkv_k_a', 'rwkv_r_k',
                                                               'rwkv_ln_w', 'rwkv_ln_b'))
    qg = jnp.tile(row(w['attn_q_norm']), (1, N_HEADS))
    kg = jnp.tile(row(w['attn_k_norm']), (1, KVW // HEAD_DIM))
    sinks = jnp.pad(row(w['attn_sinks']), ((0, 0), (0, LANE - N_HEADS)))

    h2 = rms_fwd(x1, nmix, name="mix_norm")
    pr = mm(h2, w_in_r, name="proj_rwkv")
    pa = mm(h2, w_in_a, name="proj_att")
    pg = mm(h2, w_in_g, name="proj_gate")
    pr_shift = jnp.pad(pr, ((1, 0), (0, 0)))[:-1]
    r, dec, k2, v, a, b, sg = rwkv_pre_fwd(pr, pr_shift, mu, w0, a0, k_k, k_a, wlw, wla, wlg, name="rwkv_pre")
    y, states, *gathered = wkv_fwd(r, dec, k2, v, a, b, name="wkv_fwd", gather=[shard(n) for n in LATE_WEIGHTS])
    full.update({n: _from_blocks(b, axis_of[n]) for n, b in zip(LATE_WEIGHTS, gathered)})
    yr = rwkv_post_fwd(y, r, k2, v, sg, wlg, ln_w, ln_b, r_k, name="rwkv_post")
    ya = att_fwd(pa, qg, kg, sinks, name="att_fwd")
    br = mm(yr, full['w_branch_rwkv'], name="branch_rwkv")
    ba = mm(ya, full['w_branch_attn'], name="branch_att")
    mg = merge_fwd(br, ba, pg, name="merge")
    x2 = mm(mg, full['w_out'], res=x1, name="mix_out")
    x3, ffn2, _ = _ffn_fwd(x2, n2, full['ffn2_w_gate'], full['ffn2_w_up'], full['ffn2_w_down'], "ffn2")
    dx3, d_nfin, loss = final_loss(x3, tgt, nfin, name="final_loss")

    G = {'final_norm': d_nfin}
    dx2, G['ffn2_norm'], G['ffn2_w_gate'], G['ffn2_w_up'], G['ffn2_w_down'], _ = _ffn_bwd(
        dx3, x2, ffn2, n2, full['ffn2_w_gate'], full['ffn2_w_up'], full['ffn2_w_down'], "ffn2")
    dmg = mm(dx2, full['w_out'], tb=True, name="d_merge")
    G['w_out'] = mm(mg, dx2, ta=True, name="d_w_out")
    dbr, dba, dpg = merge_bwd(dmg, br, ba, pg, name="merge_bwd")
    dyr = mm(dbr, full['w_branch_rwkv'], tb=True, name="d_y_rwkv")
    G['w_branch_rwkv'] = mm(yr, dbr, ta=True, name="d_w_branch_rwkv")
    dya = mm(dba, full['w_branch_attn'], tb=True, name="d_y_att")
    G['w_branch_attn'] = mm(ya, dba, ta=True, name="d_w_branch_att")
    dy, dz, dg, G['rwkv_ln_w'], G['rwkv_ln_b'] = rwkv_post_bwd(dyr, y, r, k2, v, sg, wlg, ln_w, ln_b, r_k, name="rwkv_post_bwd")
    late_pair = _pair_sums(LATE_WEIGHTS, [_to_blocks(G[n], axis_of[n]) for n in LATE_WEIGHTS], "late")
    res = wkv_bwd(r, dec, k2, v, a, b, dy, states, name="wkv_bwd", exchange=[q for _, q in late_pair])
    wkv_grads, late_landed = res[:6], res[6:]
    (dpr, d_mu, G['rwkv_w0'], G['rwkv_a0'], G['rwkv_k_k'], G['rwkv_k_a'], G['rwkv_r_k'], d_wlw, d_wla, d_wlg) = rwkv_pre_bwd(
        pr, pr_shift, *wkv_grads, dz, dg, mu, w0, a0, k_k, k_a, r_k, wlw, wla, wlg, name="rwkv_pre_bwd")
    G['rwkv_mu'] = _unpad_rwkv_cols(d_mu)
    G['rwkv_w_lora_up'], G['rwkv_a_lora_up'], G['rwkv_g_lora_up'] = d_wlw[:DECAY_LORA], d_wla[:ICLR_LORA], d_wlg[:GATE_LORA]
    dq, dko, dkn, dvo, dvn, G['attn_q_norm'], d_sinks = att_bwd(pa, dya, qg, kg, sinks, name="att_bwd")
    G['attn_sinks'] = d_sinks[:, :N_HEADS]
    dpa, G['attn_k_norm'] = att_kv_bwd(pa, dq, dko, dkn, dvo, dvn, kg, name="att_kv_bwd")
    d_w_in_r = mm(h2, dpr, ta=True, name="d_w_in_rwkv")
    d_w_in_a = mm(h2, dpa, ta=True, name="d_w_in_att")
    d_w_in_g = mm(h2, dpg, ta=True, name="d_w_in_gate")
    G['w_in'] = jnp.concatenate([_unpad_rwkv_cols(d_w_in_r), d_w_in_a, d_w_in_g], axis=1)
    dh2 = mm(dpr, w_in_r, tb=True, name="d_h2_rwkv")
    dh2 = mm(dpa, w_in_a, tb=True, res=dh2, name="d_h2_att")
    dh2 = mm(dpg, w_in_g, tb=True, res=dh2, name="d_h2_gate")
    dx1, G['mix_norm'] = rms_bwd(dh2, x1, nmix, dx2, name="d_mix_norm")
    mid_pair = _pair_sums(MID_WEIGHTS, [_to_blocks(G[n], axis_of[n]) for n in MID_WEIGHTS], "mid")
    dx0, G['ffn1_norm'], G['ffn1_w_gate'], G['ffn1_w_up'], G['ffn1_w_down'], mid_landed = _ffn_bwd(
        dx1, x, ffn1, n1, full['ffn1_w_gate'], full['ffn1_w_up'], full['ffn1_w_down'], "ffn1",
        carry=(_ExchangePlan, [q for _, q in mid_pair]))

    small_shapes = [(w[n].size,) for n in SMALL] + [(1,)]

    def small_rows(parts):
        vec = jnp.concatenate([p.reshape(-1) for p in parts])
        return jnp.pad(vec, (0, SMALL_ROWS * FLAT_W - vec.shape[0])).reshape(SMALL_ROWS, FLAT_W)

    small = small_rows([G[n] for n in SMALL] + [loss[0, :1]])
    first_names = FIRST_WEIGHTS + ['small']
    first_blocks = [_to_blocks(G[n], axis_of[n]) for n in FIRST_WEIGHTS] + [jnp.broadcast_to(small[None], (N_CHIPS,) + small.shape)]
    first_pair = _pair_sums(first_names, first_blocks, "first")
    first_landed = exchange_chips([q for _, q in first_pair], name="exchange_grads")
    names = first_names + MID_WEIGHTS + LATE_WEIGHTS
    pair, landed = first_pair + mid_pair + late_pair, list(first_landed) + list(mid_landed) + list(late_landed)
    halves = [half_sum(own, l, name="half_sum_" + n) for (own, _), l, n in zip(pair, landed, names)]
    other_halves = to_sibling(halves, False, name="halves_to_sibling")

    def local(prefix, n):
        if n != 'small':
            return A[prefix + n][0]
        return small_rows([A[prefix + s] for s in SMALL] + [jnp.zeros((1,), F32)])

    result = {}
    for n, mine, theirs in zip(names, halves, other_halves):
        outs4 = adamw(local('', n), local('m_', n), local('v_', n), mine, theirs, name="adamw_" + n)
        for kind, o in zip(('grad_', 'delta_', 'new_m_', 'new_v_'), outs4):
            if n != 'small':
                result[kind + n] = o[None]
            else:
                for s, part in zip(SMALL + ['loss'], _unpack_vec(o.reshape(-1), small_shapes)):
                    result[kind + s] = part.reshape(A[s].shape) if s != 'loss' else part.reshape(())
    outs = [result['grad_loss'], dx0[None]]
    for kind in ('grad_', 'delta_', 'new_m_', 'new_v_'):
        outs += [result[kind + n] for n in WEIGHT_NAMES]
    return tuple(outs)


def _unpack_vec(vec, shapes):
    out, off = [], 0
    for (n,) in shapes:
        out.append(vec[off:off + n])
        off += n
    return out


def kernel(x, ffn1_norm, ffn1_w_gate, ffn1_w_up, ffn1_w_down, mix_norm, w_in, rwkv_mu, rwkv_w0, rwkv_w_lora_up, rwkv_a0, rwkv_a_lora_up, rwkv_g_lora_up, rwkv_k_k, rwkv_k_a, rwkv_r_k, rwkv_ln_w, rwkv_ln_b, attn_q_norm, attn_k_norm, attn_sinks, w_branch_rwkv, w_branch_attn, w_out, ffn2_norm, ffn2_w_gate, ffn2_w_up, ffn2_w_down, final_norm, loss_target, m_ffn1_norm, m_ffn1_w_gate, m_ffn1_w_up, m_ffn1_w_down, m_mix_norm, m_w_in, m_rwkv_mu, m_rwkv_w0, m_rwkv_w_lora_up, m_rwkv_a0, m_rwkv_a_lora_up, m_rwkv_g_lora_up, m_rwkv_k_k, m_rwkv_k_a, m_rwkv_r_k, m_rwkv_ln_w, m_rwkv_ln_b, m_attn_q_norm, m_attn_k_norm, m_attn_sinks, m_w_branch_rwkv, m_w_branch_attn, m_w_out, m_ffn2_norm, m_ffn2_w_gate, m_ffn2_w_up, m_ffn2_w_down, m_final_norm, v_ffn1_norm, v_ffn1_w_gate, v_ffn1_w_up, v_ffn1_w_down, v_mix_norm, v_w_in, v_rwkv_mu, v_rwkv_w0, v_rwkv_w_lora_up, v_rwkv_a0, v_rwkv_a_lora_up, v_rwkv_g_lora_up, v_rwkv_k_k, v_rwkv_k_a, v_rwkv_r_k, v_rwkv_ln_w, v_rwkv_ln_b, v_attn_q_norm, v_attn_k_norm, v_attn_sinks, v_w_branch_rwkv, v_w_branch_attn, v_w_out, v_ffn2_norm, v_ffn2_w_gate, v_ffn2_w_up, v_ffn2_w_down, v_final_norm):
    return _step(dict(locals()))
```

```python
import functools

import jax
import jax.numpy as jnp
from jax import lax
from jax.experimental import pallas as pl
from jax.experimental.pallas import tpu as pltpu

F32 = jnp.float32
BF16 = jnp.bfloat16

D_MODEL = 1024
D_FF = 2816
HEAD_DIM = 64
N_HEADS = 8
RW = 512
KVW = 128
ATT_GROUP = 4
WINDOW = 128
BLOCK = 128
DECAY_LORA, ICLR_LORA, GATE_LORA = 32, 32, 96
RWKV_COLS = 3 * RW + DECAY_LORA + ICLR_LORA + GATE_LORA
ATT_COLS = RW + 2 * KVW
GATE_COLS = 2 * D_MODEL
RWKV_PAD = 3 * RW + 3 * 128
RMS_EPS = 1e-6
GN_EPS = 64e-5
N_CHIPS = 4
LANE = 128
FLAT_W = 1024
SMALL_ROWS = 32
NEG_BIG = -1e30

ADAM_LR, ADAM_B1, ADAM_B2, ADAM_EPS, ADAM_WD, ADAM_STEP = 0.001, 0.9, 0.999, 1e-08, 0.01, 10

VMEM_LIMIT = 56 * 1024 * 1024

WEIGHT_NAMES = ['ffn1_norm', 'ffn1_w_gate', 'ffn1_w_up', 'ffn1_w_down', 'mix_norm', 'w_in', 'rwkv_mu', 'rwkv_w0',
                'rwkv_w_lora_up', 'rwkv_a0', 'rwkv_a_lora_up', 'rwkv_g_lora_up', 'rwkv_k_k', 'rwkv_k_a', 'rwkv_r_k',
                'rwkv_ln_w', 'rwkv_ln_b', 'attn_q_norm', 'attn_k_norm', 'attn_sinks', 'w_branch_rwkv',
                'w_branch_attn', 'w_out', 'ffn2_norm', 'ffn2_w_gate', 'ffn2_w_up', 'ffn2_w_down', 'final_norm']
BIG = [('ffn1_w_gate', 1), ('ffn1_w_up', 1), ('ffn1_w_down', 0), ('w_in', 1), ('rwkv_w_lora_up', 1),
       ('rwkv_a_lora_up', 1), ('rwkv_g_lora_up', 1), ('w_branch_rwkv', 1), ('w_branch_attn', 1), ('w_out', 0),
       ('ffn2_w_gate', 1), ('ffn2_w_up', 1), ('ffn2_w_down', 0)]
SMALL = ['ffn1_norm', 'mix_norm', 'rwkv_mu', 'rwkv_w0', 'rwkv_a0', 'rwkv_k_k', 'rwkv_k_a', 'rwkv_r_k', 'rwkv_ln_w',
         'rwkv_ln_b', 'attn_q_norm', 'attn_k_norm', 'attn_sinks', 'ffn2_norm', 'final_norm']


def _pcall(body, **kw):
    return pl.pallas_call(body, **kw)


def _params(sem=None, **kw):
    if sem is not None:
        kw['dimension_semantics'] = sem
    return pltpu.CompilerParams(vmem_limit_bytes=VMEM_LIMIT, **kw)


def _tile(n, cap, mult):
    best = None
    for t in range(mult, min(n, cap) + 1, mult):
        if n % t == 0:
            best = t
    return best or n


def _sigmoid(z):
    return 1.0 / (1.0 + jnp.exp(-z))


def _softplus(z):
    return jnp.maximum(z, 0.0) + jnp.log(1.0 + jnp.exp(-jnp.abs(z)))


def _bdot(a, b, dims=(((1,), (0,)), ((), ()))):
    return lax.dot_general(a.astype(BF16), b.astype(BF16), dims, preferred_element_type=F32)


_NT = (((1,), (1,)), ((), ()))
_TN = (((0,), (0,)), ((), ()))


def _segsum(x, bd):
    hi = x.astype(BF16)
    r1 = x - hi.astype(F32)
    mid = r1.astype(BF16)
    lo = (r1 - mid.astype(F32)).astype(BF16)
    dot = functools.partial(lax.dot_general, dimension_numbers=(((1,), (0,)), ((), ())), preferred_element_type=F32)
    return dot(hi, bd) + dot(mid, bd) + dot(lo, bd)


_LORA_EDGES = (3 * RW, 3 * RW + DECAY_LORA, 3 * RW + DECAY_LORA + ICLR_LORA, RWKV_COLS)


def _pad_rwkv_cols(x):
    parts = [x[..., :3 * RW]]
    for lo, hi in zip(_LORA_EDGES[:-1], _LORA_EDGES[1:]):
        parts.append(jnp.pad(x[..., lo:hi], [(0, 0)] * (x.ndim - 1) + [(0, 128 - (hi - lo))]))
    return jnp.concatenate(parts, axis=-1)


def _unpad_rwkv_cols(x):
    parts = [x[..., :3 * RW]]
    for j, (lo, hi) in enumerate(zip(_LORA_EDGES[:-1], _LORA_EDGES[1:])):
        parts.append(x[..., 3 * RW + 128 * j:3 * RW + 128 * j + (hi - lo)])
    return jnp.concatenate(parts, axis=-1)


def _pad_rows(x, rows):
    return jnp.pad(x, [(0, rows - x.shape[0])] + [(0, 0)] * (x.ndim - 1))


def mm(a, b, *, name, ta=False, tb=False, scale=None, res=None, out_dtype=F32, carry=None):
    M, K = (a.shape[1], a.shape[0]) if ta else a.shape
    N = b.shape[0] if tb else b.shape[1]
    assert (b.shape[1] if tb else b.shape[0]) == K
    tm, tn, tk = _tile(M, 1408 if ta else 512, 128), _tile(N, 1408, 128), _tile(K, 1408, 128)
    nk = K // tk
    grid = (M // tm, N // tn, nk)
    dims = (((0 if ta else 1,), (1 if tb else 0,)), ((), ()))
    plan_cls, carried = carry if carry else (None, ())
    nc, nin = len(carried), 2 + (res is not None)

    def body(*refs):
        a_ref, b_ref = refs[:2]
        r_ref = refs[2] if res is not None else None
        o_ref, acc_ref = refs[nin + nc], refs[nin + 2 * nc + 1]
        k = pl.program_id(2)
        if nc:
            plan = plan_cls(refs[nin:nin + nc], refs[nin + nc + 1:nin + 2 * nc + 1], refs[nin + 2 * nc + 2:])
            at = lambda which: functools.reduce(jnp.logical_and, [pl.program_id(d) == (0 if which == 0 else grid[d] - 1)
                                                                 for d in range(3)])
            pl.when(at(0))(plan.start)
        part = _bdot(a_ref[...], b_ref[...], dims)

        @pl.when(k == 0)
        def _():
            acc_ref[...] = part

        @pl.when(k > 0)
        def _():
            acc_ref[...] += part

        @pl.when(k == nk - 1)
        def _():
            o = acc_ref[...]
            if scale is not None:
                o = o * scale
            if r_ref is not None:
                o = o + r_ref[...].astype(F32)
            o_ref[...] = o.astype(out_dtype)

        if nc:
            pl.when(at(1))(plan.finish)

    a_spec = pl.BlockSpec((tk, tm), lambda i, j, k: (k, i)) if ta else pl.BlockSpec((tm, tk), lambda i, j, k: (i, k))
    b_spec = pl.BlockSpec((tn, tk), lambda i, j, k: (j, k)) if tb else pl.BlockSpec((tk, tn), lambda i, j, k: (k, j))
    o_spec = pl.BlockSpec((tm, tn), lambda i, j, k: (i, j))
    in_specs = [a_spec, b_spec] + ([o_spec] if res is not None else [])
    args = (a, b) + ((res,) if res is not None else ())
    out_shape = jax.ShapeDtypeStruct((M, N), out_dtype)
    if not nc:
        return _pcall(
            body, name=name, grid=grid, in_specs=in_specs, out_specs=o_spec, out_shape=out_shape,
            scratch_shapes=[pltpu.VMEM((tm, tn), F32)], compiler_params=_params(("parallel", "parallel", "arbitrary")),
        )(*args)
    return _pcall(
        body, name=name, grid=grid, in_specs=in_specs + [_HBM] * nc, out_specs=[o_spec] + [_HBM] * nc,
        out_shape=[out_shape] + plan_cls.out_shapes(carried), scratch_shapes=[pltpu.VMEM((tm, tn), F32)] + plan_cls.sems(nc),
        compiler_params=_params(("arbitrary", "arbitrary", "arbitrary")),
    )(*args, *carried)


def mm_fused(a, bs, finish, out_dtypes, *, name, tb=False, extras=(), carry=None):
    M, K = a.shape
    N = bs[0].shape[0] if tb else bs[0].shape[1]
    tm, tn = _tile(M, 512, 128), _tile(N, 1408, 128)
    grid = (M // tm, N // tn)
    dims = (((1,), (1 if tb else 0,)), ((), ()))
    plan_cls, carried = carry if carry else (None, ())
    nc, nb, nx, no = len(carried), len(bs), len(extras), len(out_dtypes)
    nin = 1 + nb + nx

    def body(*refs):
        a_ref, b_refs, x_refs = refs[0], refs[1:1 + nb], refs[1 + nb:nin]
        o_refs = refs[nin + nc:nin + nc + no]
        if nc:
            plan = plan_cls(refs[nin:nin + nc], refs[nin + nc + no:nin + 2 * nc + no], refs[nin + 2 * nc + no:])
            at = lambda which: jnp.logical_and(*[pl.program_id(d) == (0 if which == 0 else grid[d] - 1) for d in range(2)])
            pl.when(at(0))(plan.start)
        av = a_ref[...]
        outs = finish([_bdot(av, b_ref[...], dims) for b_ref in b_refs], [x_ref[...] for x_ref in x_refs])
        for o_ref, o in zip(o_refs, outs):
            o_ref[...] = o.astype(o_ref.dtype)
        if nc:
            pl.when(at(1))(plan.finish)

    a_spec = pl.BlockSpec((tm, K), lambda i, j: (i, 0))
    b_spec = pl.BlockSpec((tn, K), lambda i, j: (j, 0)) if tb else pl.BlockSpec((K, tn), lambda i, j: (0, j))
    o_spec = pl.BlockSpec((tm, tn), lambda i, j: (i, j))
    out_shape = [jax.ShapeDtypeStruct((M, N), d) for d in out_dtypes]
    if not nc:
        return _pcall(body, name=name, grid=grid, in_specs=[a_spec] + [b_spec] * nb + [o_spec] * nx, out_specs=[o_spec] * no,
                      out_shape=out_shape, compiler_params=_params(("parallel", "parallel")))(a, *bs, *extras)
    return _pcall(body, name=name, grid=grid, in_specs=[a_spec] + [b_spec] * nb + [o_spec] * nx + [_HBM] * nc,
                  out_specs=[o_spec] * no + [_HBM] * nc, out_shape=out_shape + plan_cls.out_shapes(carried),
                  scratch_shapes=plan_cls.sems(nc), compiler_params=_params(("arbitrary", "arbitrary")))(a, *bs, *extras, *carried)


def _swiglu(products, _):
    g, u = products
    return g, u, g * _sigmoid(g) * u


def _swiglu_bwd(products, extras):
    (da,), (gate, up) = products, extras
    gv = gate.astype(F32)
    s = _sigmoid(gv)
    return da * 0.5 * up.astype(F32) * s * (1.0 + gv * (1.0 - s)), da * 0.5 * gv * s


def _row_spec(tr, c):
    return pl.BlockSpec((tr, c), lambda i: (i, 0))


def _full_spec(shape):
    return pl.BlockSpec(shape, lambda i: (0,) * len(shape))


def _acc_rows(ref, val, i):
    @pl.when(i == 0)
    def _():
        ref[...] = val

    @pl.when(i > 0)
    def _():
        ref[...] += val


def rms_fwd(x, g, *, name):
    T, D = x.shape
    tr = _tile(T, 512, 8)

    def body(x_ref, g_ref, h_ref):
        xv = x_ref[...]
        r = lax.rsqrt(jnp.mean(xv * xv, axis=-1, keepdims=True) + RMS_EPS)
        h_ref[...] = (xv * r * g_ref[...]).astype(BF16)

    return _pcall(body, name=name, grid=(T // tr,), in_specs=[_row_spec(tr, D), _full_spec((1, D))],
                  out_specs=_row_spec(tr, D), out_shape=jax.ShapeDtypeStruct((T, D), BF16),
                  compiler_params=_params(("parallel",)))(x, g)


def rms_bwd(dh, x, g, res, *, name):
    T, D = x.shape
    tr = _tile(T, 256, 8)

    def body(dh_ref, x_ref, g_ref, res_ref, dx_ref, dg_ref):
        i = pl.program_id(0)
        xv, dhv = x_ref[...], dh_ref[...].astype(F32)
        r = lax.rsqrt(jnp.mean(xv * xv, axis=-1, keepdims=True) + RMS_EPS)
        xh = xv * r
        dxh = dhv * g_ref[...]
        dx_ref[...] = res_ref[...] + r * (dxh - xh * jnp.mean(dxh * xh, axis=-1, keepdims=True))
        _acc_rows(dg_ref, jnp.sum(dhv * xh, axis=0, keepdims=True), i)

    return _pcall(body, name=name, grid=(T // tr,),
                  in_specs=[_row_spec(tr, D), _row_spec(tr, D), _full_spec((1, D)), _row_spec(tr, D)],
                  out_specs=[_row_spec(tr, D), _full_spec((1, D))],
                  out_shape=[jax.ShapeDtypeStruct((T, D), F32), jax.ShapeDtypeStruct((1, D), F32)],
                  compiler_params=_params(("arbitrary",)))(dh, x, g, res)


def final_loss(x, tgt, g, *, name):
    T, D = x.shape
    tr = _tile(T, 256, 8)

    def body(x_ref, t_ref, g_ref, dx_ref, dg_ref, loss_ref):
        i = pl.program_id(0)
        xv = x_ref[...]
        r = lax.rsqrt(jnp.mean(xv * xv, axis=-1, keepdims=True) + RMS_EPS)
        xh = xv * r
        e = xh * g_ref[...] - t_ref[...]
        part = 0.5 * jnp.sum(jnp.mean(e * e, axis=-1, keepdims=True), axis=0, keepdims=True)
        dy = e * (1.0 / D)
        dxh = dy * g_ref[...]
        dx_ref[...] = r * (dxh - xh * jnp.mean(dxh * xh, axis=-1, keepdims=True))
        _acc_rows(dg_ref, jnp.sum(dy * xh, axis=0, keepdims=True), i)
        _acc_rows(loss_ref, jnp.broadcast_to(part, (1, LANE)), i)

    return _pcall(body, name=name, grid=(T // tr,),
                  in_specs=[_row_spec(tr, D), _row_spec(tr, D), _full_spec((1, D))],
                  out_specs=[_row_spec(tr, D), _full_spec((1, D)), _full_spec((1, LANE))],
                  out_shape=[jax.ShapeDtypeStruct((T, D), F32), jax.ShapeDtypeStruct((1, D), F32),
                             jax.ShapeDtypeStruct((1, LANE), F32)],
                  compiler_params=_params(("arbitrary",)))(x, tgt, g)


def merge_fwd(br, ba, pg, *, name):
    T, D = br.shape
    tr = _tile(T, 256, 8)

    def body(br_ref, ba_ref, pg_ref, o_ref):
        pgv = pg_ref[...]
        o_ref[...] = (_sigmoid(pgv[:, :D]) * br_ref[...] + _sigmoid(pgv[:, D:]) * ba_ref[...]).astype(BF16)

    return _pcall(body, name=name, grid=(T // tr,), in_specs=[_row_spec(tr, D), _row_spec(tr, D), _row_spec(tr, 2 * D)],
                  out_specs=_row_spec(tr, D), out_shape=jax.ShapeDtypeStruct((T, D), BF16),
                  compiler_params=_params(("parallel",)))(br, ba, pg)


def merge_bwd(dm, br, ba, pg, *, name):
    T, D = br.shape
    tr = _tile(T, 256, 8)

    def body(dm_ref, br_ref, ba_ref, pg_ref, dbr_ref, dba_ref, dpg_ref):
        pgv, dmv = pg_ref[...], dm_ref[...]
        sr, sa = _sigmoid(pgv[:, :D]), _sigmoid(pgv[:, D:])
        dbr_ref[...] = (dmv * sr).astype(BF16)
        dba_ref[...] = (dmv * sa).astype(BF16)
        dpg_ref[:, :D] = dmv * br_ref[...] * sr * (1.0 - sr)
        dpg_ref[:, D:] = dmv * ba_ref[...] * sa * (1.0 - sa)

    return _pcall(body, name=name, grid=(T // tr,),
                  in_specs=[_row_spec(tr, D), _row_spec(tr, D), _row_spec(tr, D), _row_spec(tr, 2 * D)],
                  out_specs=[_row_spec(tr, D), _row_spec(tr, D), _row_spec(tr, 2 * D)],
                  out_shape=[jax.ShapeDtypeStruct((T, D), BF16), jax.ShapeDtypeStruct((T, D), BF16),
                             jax.ShapeDtypeStruct((T, 2 * D), F32)],
                  compiler_params=_params(("parallel",)))(dm, br, ba, pg)


def _rwkv_mix(p, prev, mu, w0, a0, k_k, k_a, wlw, wla, wlg, bd):
    pp = p + (prev - p) * mu
    r, k, v = pp[:, 0:RW], pp[:, RW:2 * RW], pp[:, 2 * RW:3 * RW]
    xw, xa, xg = pp[:, 3 * RW:3 * RW + 128], pp[:, 3 * RW + 128:3 * RW + 256], pp[:, 3 * RW + 256:3 * RW + 384]
    th = jnp.tanh(xw)
    z = -(w0 + _bdot(th, wlw))
    e = jnp.exp(-_softplus(z) - 0.5)
    decay = jnp.exp(-e)
    a = _sigmoid(a0 + _bdot(xa, wla))
    sg = _sigmoid(xg)
    kkr = k * k_k
    n = jnp.sqrt(_segsum(kkr * kkr, bd))
    kk = kkr / jnp.maximum(n, 1e-12)
    k2 = k * (1.0 + (a - 1.0) * k_a)
    return dict(r=r, k=k, v=v, xa=xa, th=th, z=z, e=e, decay=decay, a=a, sg=sg, n=n, kk=kk, k2=k2)


def _seg_matrix(n, shift):
    r = lax.shift_right_logical(lax.broadcasted_iota(jnp.int32, (n, n), 0), shift)
    c = lax.shift_right_logical(lax.broadcasted_iota(jnp.int32, (n, n), 1), shift)
    return jnp.where(r == c, 1.0, 0.0).astype(BF16)


def rwkv_pre_fwd(p, pshift, mu, w0, a0, k_k, k_a, wlw, wla, wlg, *, name):
    T = p.shape[0]
    tr = _tile(T, 256, 8)

    def body(p_ref, ps_ref, mu_ref, w0_ref, a0_ref, kk_ref, ka_ref, wlw_ref, wla_ref, wlg_ref,
             r_ref, w_ref, k_ref, v_ref, a_ref, b_ref, g_ref):
        pv, prev = p_ref[...], ps_ref[...]
        m = _rwkv_mix(pv, prev, mu_ref[...], w0_ref[...], a0_ref[...], kk_ref[...], ka_ref[...],
                      wlw_ref[...], wla_ref[...], wlg_ref[...], _seg_matrix(RW, 6))
        r_ref[...] = m['r']
        w_ref[...] = m['decay']
        k_ref[...] = m['k2']
        v_ref[...] = m['v']
        a_ref[...] = -m['kk']
        b_ref[...] = m['kk'] * m['a']
        g_ref[...] = m['sg']

    vec = _row_spec(tr, RW)
    return _pcall(
        body, name=name, grid=(T // tr,),
        in_specs=[_row_spec(tr, RWKV_PAD), _row_spec(tr, RWKV_PAD), _full_spec((1, RWKV_PAD))] + [_full_spec((1, RW))] * 4
        + [_full_spec((128, RW))] * 3,
        out_specs=[vec] * 6 + [_row_spec(tr, 128)],
        out_shape=[jax.ShapeDtypeStruct((T, RW), F32)] * 6 + [jax.ShapeDtypeStruct((T, 128), F32)],
        compiler_params=_params(("parallel",)),
    )(p, pshift, mu, w0, a0, k_k, k_a, wlw, wla, wlg)


def _group_norm(y, bd):
    mean = _segsum(y, bd) * (1.0 / HEAD_DIM)
    yc = y - mean
    rstd = lax.rsqrt(_segsum(yc * yc, bd) * (1.0 / HEAD_DIM) + GN_EPS)
    return yc * rstd, rstd


def rwkv_post_fwd(y, r, k2, v, sg, wlg, ln_w, ln_b, r_k, *, name):
    T = y.shape[0]
    tr = _tile(T, 256, 8)

    def body(y_ref, r_ref, k_ref, v_ref, sg_ref, wlg_ref, lw_ref, lb_ref, rk_ref, o_ref):
        bd = _seg_matrix(RW, 6)
        yn, _ = _group_norm(y_ref[...], bd)
        s = _segsum(r_ref[...] * k_ref[...] * rk_ref[...], bd)
        g = _bdot(sg_ref[...], wlg_ref[...])
        o_ref[...] = ((yn * lw_ref[...] + lb_ref[...] + s * v_ref[...]) * g).astype(BF16)

    vec = _row_spec(tr, RW)
    return _pcall(body, name=name, grid=(T // tr,),
                  in_specs=[vec] * 4 + [_row_spec(tr, 128), _full_spec((128, RW))] + [_full_spec((1, RW))] * 3, out_specs=vec,
                  out_shape=jax.ShapeDtypeStruct((T, RW), BF16), compiler_params=_params(("parallel",)))(
                      y, r, k2, v, sg, wlg, ln_w, ln_b, r_k)


def rwkv_post_bwd(dyr, y, r, k2, v, sg, wlg, ln_w, ln_b, r_k, *, name):
    T = y.shape[0]
    tr = _tile(T, 256, 8)

    def body(dyr_ref, y_ref, r_ref, k_ref, v_ref, sg_ref, wlg_ref, lw_ref, lb_ref, rk_ref,
             dy_ref, dz_ref, dg_ref, dlw_ref, dlb_ref):
        i = pl.program_id(0)
        bd = _seg_matrix(RW, 6)
        yn, rstd = _group_norm(y_ref[...], bd)
        s = _segsum(r_ref[...] * k_ref[...] * rk_ref[...], bd)
        dyrv = dyr_ref[...]
        dg_ref[...] = dyrv * (yn * lw_ref[...] + lb_ref[...] + s * v_ref[...])
        dz = dyrv * _bdot(sg_ref[...], wlg_ref[...])
        dz_ref[...] = dz
        dyn = dz * lw_ref[...]
        inv = 1.0 / HEAD_DIM
        dy_ref[...] = rstd * (dyn - _segsum(dyn, bd) * inv - yn * (_segsum(dyn * yn, bd) * inv))
        _acc_rows(dlw_ref, jnp.sum(dz * yn, axis=0, keepdims=True), i)
        _acc_rows(dlb_ref, jnp.sum(dz, axis=0, keepdims=True), i)

    vec = _row_spec(tr, RW)
    one = _full_spec((1, RW))
    return _pcall(body, name=name, grid=(T // tr,),
                  in_specs=[vec] * 5 + [_row_spec(tr, 128), _full_spec((128, RW))] + [one] * 3, out_specs=[vec] * 3 + [one] * 2,
                  out_shape=[jax.ShapeDtypeStruct((T, RW), F32)] * 3 + [jax.ShapeDtypeStruct((1, RW), F32)] * 2,
                  compiler_params=_params(("arbitrary",)))(dyr, y, r, k2, v, sg, wlg, ln_w, ln_b, r_k)


def rwkv_pre_bwd(p, pshift, dr_w, dw_w, dk_w, dv_w, da_w, db_w, dz, dg, mu, w0, a0, k_k, k_a, r_k, wlw, wla, wlg, *, name):
    T = p.shape[0]
    tr = _tile(T, 256, 8)
    n = T // tr

    def body(p_ref, ps_ref, dr_ref, dw_ref, dk_ref, dv_ref, da_ref, db_ref, dz_ref, dg_ref,
             mu_ref, w0_ref, a0_ref, kk_ref, ka_ref, rk_ref, wlw_ref, wla_ref, wlg_ref,
             dp_ref, dmu_ref, dw0_ref, da0_ref, dkk_ref, dka_ref, drk_ref, dwlw_ref, dwla_ref, dwlg_ref,
             carry, dpp, acc_w, acc_a, acc_g):
        i = pl.program_id(0)

        @pl.when(i == 0)
        def _():
            carry[...] = jnp.zeros_like(carry)

        pv, prev, mu = p_ref[...], ps_ref[...], mu_ref[...]
        bd = _seg_matrix(RW, 6)
        k_k, k_a, r_k = kk_ref[...], ka_ref[...], rk_ref[...]
        m = _rwkv_mix(pv, prev, mu, w0_ref[...], a0_ref[...], k_k, k_a, wlw_ref[...], wla_ref[...], wlg_ref[...], bd)
        r, k, v, a, kk, k2 = m['r'], m['k'], m['v'], m['a'], m['kk'], m['k2']
        dzv, dgv = dz_ref[...], dg_ref[...]
        s = _segsum(r * k2 * r_k, bd)
        ds = _segsum(dzv * v, bd)
        dr = dr_ref[...] + ds * k2 * r_k
        dk2 = dk_ref[...] + ds * r * r_k
        dv = dv_ref[...] + dzv * s
        dbv = db_ref[...]
        dkk = dbv * a - da_ref[...]
        da = dbv * kk + dk2 * k * k_a
        dk = dk2 * (1.0 + (a - 1.0) * k_a)
        nmax = jnp.maximum(m['n'], 1e-12)
        dkkr = jnp.where(m['n'] > 1e-12, dkk - kk * _segsum(dkk * kk, bd), dkk) / nmax
        dk = dk + dkkr * k_k
        dapre = da * a * (1.0 - a)
        dwpre = dw_ref[...] * m['decay'] * (-m['e']) * _sigmoid(m['z'])
        dth = _bdot(dwpre, wlw_ref[...], _NT)
        dxa = _bdot(dapre, wla_ref[...], _NT)
        dsg = _bdot(dgv, wlg_ref[...], _NT)
        dpp[:, 0:RW] = dr
        dpp[:, RW:2 * RW] = dk
        dpp[:, 2 * RW:3 * RW] = dv
        dpp[:, 3 * RW:3 * RW + 128] = dth * (1.0 - m['th'] * m['th'])
        dpp[:, 3 * RW + 128:3 * RW + 256] = dxa
        dpp[:, 3 * RW + 256:3 * RW + 384] = dsg * m['sg'] * (1.0 - m['sg'])
        d = dpp[...]
        zed = d * mu
        last = lax.broadcasted_iota(jnp.int32, pv.shape, 0) == tr - 1
        dp_ref[...] = d * (1.0 - mu) + jnp.where(last, carry[0:1, :], pltpu.roll(zed, tr - 1, 0))
        carry[...] = zed[0:8, :]

        def colsum(x):
            return jnp.sum(x, axis=0, keepdims=True)

        _acc_rows(dmu_ref, colsum(d * (prev - pv)), i)
        _acc_rows(dw0_ref, colsum(dwpre), i)
        _acc_rows(da0_ref, colsum(dapre), i)
        _acc_rows(dkk_ref, colsum(dkkr * k), i)
        _acc_rows(dka_ref, colsum(dk2 * k * (a - 1.0)), i)
        _acc_rows(drk_ref, colsum(ds * r * k2), i)
        _acc_rows(acc_w, _bdot(m['th'], dwpre, _TN), i)
        _acc_rows(acc_a, _bdot(m['xa'], dapre, _TN), i)
        _acc_rows(acc_g, _bdot(m['sg'], dgv, _TN), i)

        @pl.when(i == n - 1)
        def _():
            dwlw_ref[...] = acc_w[...]
            dwla_ref[...] = acc_a[...]
            dwlg_ref[...] = acc_g[...]

    rev = lambda c: pl.BlockSpec((tr, c), lambda i: (n - 1 - i, 0))
    one, lora = _full_spec((1, RW)), _full_spec((128, RW))
    return _pcall(
        body, name=name, grid=(n,),
        in_specs=[rev(RWKV_PAD), rev(RWKV_PAD)] + [rev(RW)] * 8 + [_full_spec((1, RWKV_PAD))] + [one] * 5 + [lora] * 3,
        out_specs=[rev(RWKV_PAD), _full_spec((1, RWKV_PAD))] + [one] * 5 + [lora] * 3,
        out_shape=[jax.ShapeDtypeStruct((T, RWKV_PAD), F32), jax.ShapeDtypeStruct((1, RWKV_PAD), F32)]
        + [jax.ShapeDtypeStruct((1, RW), F32)] * 5 + [jax.ShapeDtypeStruct((128, RW), F32)] * 3,
        scratch_shapes=[pltpu.VMEM((8, RWKV_PAD), F32), pltpu.VMEM((tr, RWKV_PAD), F32)] + [pltpu.VMEM((128, RW), F32)] * 3,
        compiler_params=_params(("arbitrary",)),
    )(p, pshift, dr_w, dw_w, dk_w, dv_w, da_w, db_w, dz, dg, mu, w0, a0, k_k, k_a, r_k, wlw, wla, wlg)


def _qk_norm(x, g, bd):
    r = lax.rsqrt(_segsum(x * x, bd) * (1.0 / HEAD_DIM) + RMS_EPS)
    return x * r * g, r


def _att_mask(i):
    qi = lax.broadcasted_iota(jnp.int32, (BLOCK, 2 * BLOCK), 0)
    kj = lax.broadcasted_iota(jnp.int32, (BLOCK, 2 * BLOCK), 1)
    band = (kj <= qi + BLOCK) & (kj > qi + BLOCK - WINDOW)
    return band & ((kj >= BLOCK) | (i > 0))


_HQK = (((2,), (2,)), ((0,), (0,)))
_HPV = (((2,), (1,)), ((0,), (0,)))
_HTN = (((1,), (1,)), ((0,), (0,)))


def _heads(x, n):
    return jnp.stack([x[:, h * HEAD_DIM:(h + 1) * HEAD_DIM] for h in range(n)])


def _unheads(x3):
    return jnp.concatenate([x3[h] for h in range(x3.shape[0])], axis=1)


def _kv_heads(x):
    x2 = _heads(x, KVW // HEAD_DIM)
    return jnp.concatenate([x2[g:g + 1] for g in range(KVW // HEAD_DIM) for _ in range(ATT_GROUP)], axis=0)


def _sinks3(sk):
    return jnp.stack([sk[0:1, h:h + 1] for h in range(N_HEADS)])


def _att_probs(q3, k3, mask, sink):
    s = _bdot(q3, k3, _HQK) * (HEAD_DIM ** -0.5)
    s = jnp.where(mask[None], s, NEG_BIG)
    m = jnp.maximum(jnp.max(s, axis=-1, keepdims=True), sink)
    pexp = jnp.exp(s - m)
    psink = jnp.exp(sink - m)
    inv = 1.0 / (jnp.sum(pexp, axis=-1, keepdims=True) + psink)
    return pexp * inv, psink * inv


def _att_blocks(n):
    cur = pl.BlockSpec((BLOCK, ATT_COLS), lambda i: (i, 0))
    prev = pl.BlockSpec((BLOCK, ATT_COLS), lambda i: (jnp.maximum(i - 1, 0), 0))
    return cur, prev


def _att_qkv(cur, prev, qn_g, kn_g):
    bq, bk = _seg_matrix(RW, 6), _seg_matrix(KVW, 6)
    qn, rq = _qk_norm(cur[:, 0:RW], qn_g, bq)
    kcur, rkc = _qk_norm(cur[:, RW:RW + KVW], kn_g, bk)
    kprev, _ = _qk_norm(prev[:, RW:RW + KVW], kn_g, bk)
    kc = jnp.concatenate([kprev, kcur], axis=0)
    vc = jnp.concatenate([prev[:, RW + KVW:], cur[:, RW + KVW:]], axis=0)
    return qn, rq, kc, vc, rkc


def att_fwd(pa, qn_g, kn_g, sinks, *, name):
    T = pa.shape[0]
    n = T // BLOCK

    def body(cur_ref, prev_ref, qg_ref, kg_ref, sk_ref, o_ref):
        i = pl.program_id(0)
        qn, _, kc, vc, _ = _att_qkv(cur_ref[...], prev_ref[...], qg_ref[...], kg_ref[...])
        probs, _ = _att_probs(_heads(qn, N_HEADS), _kv_heads(kc), _att_mask(i), _sinks3(sk_ref[...]))
        o_ref[...] = _unheads(_bdot(probs, _kv_heads(vc), _HPV))

    cur, prev = _att_blocks(n)
    return _pcall(body, name=name, grid=(n,),
                  in_specs=[cur, prev, _full_spec((1, RW)), _full_spec((1, KVW)), _full_spec((1, LANE))],
                  out_specs=pl.BlockSpec((BLOCK, RW), lambda i: (i, 0)), out_shape=jax.ShapeDtypeStruct((T, RW), F32),
                  compiler_params=_params(("parallel",)))(pa, pa, qn_g, kn_g, sinks)


def att_bwd(pa, do, qn_g, kn_g, sinks, *, name):
    T = pa.shape[0]
    n = T // BLOCK

    def body(cur_ref, prev_ref, do_ref, qg_ref, kg_ref, sk_ref,
             dq_ref, dko_ref, dkn_ref, dvo_ref, dvn_ref, dqg_ref, dsk_ref):
        i = pl.program_id(0)
        cur = cur_ref[...]
        qn, rq, kc, vc, _ = _att_qkv(cur, prev_ref[...], qg_ref[...], kg_ref[...])
        q3, k3, v3, do3 = _heads(qn, N_HEADS), _kv_heads(kc), _kv_heads(vc), _heads(do_ref[...], N_HEADS)
        probs, psink = _att_probs(q3, k3, _att_mask(i), _sinks3(sk_ref[...]))
        dprobs = _bdot(do3, v3, _HQK)
        delta = jnp.sum(probs * dprobs, axis=-1, keepdims=True)
        ds = probs * (dprobs - delta) * (HEAD_DIM ** -0.5)
        dsink3 = -jnp.sum(psink * delta, axis=1, keepdims=True)
        lane = lax.broadcasted_iota(jnp.int32, (1, LANE), 1)
        dsink = jnp.zeros((1, LANE), F32)
        for h in range(N_HEADS):
            dsink = dsink + jnp.where(lane == h, dsink3[h], 0.0)
        dqn = _unheads(_bdot(ds, k3, _HPV))

        def per_kv_head(x3):
            groups = [sum(x3[g * ATT_GROUP + j] for j in range(ATT_GROUP)) for g in range(KVW // HEAD_DIM)]
            return jnp.concatenate(groups, axis=1)

        dk, dv = per_kv_head(_bdot(ds, q3, _HTN)), per_kv_head(_bdot(probs, do3, _HTN))
        dkn_ref[...], dko_ref[...] = dk[0:BLOCK], dk[BLOCK:]
        dvn_ref[...], dvo_ref[...] = dv[0:BLOCK], dv[BLOCK:]
        qhat = cur[:, 0:RW] * rq
        dqh = dqn * qg_ref[...]
        dq_ref[...] = rq * (dqh - qhat * (_segsum(dqh * qhat, _seg_matrix(RW, 6)) * (1.0 / HEAD_DIM)))
        prod = dqn * qhat
        fold = prod[:, 0:HEAD_DIM]
        for h in range(1, N_HEADS):
            fold = fold + prod[:, h * HEAD_DIM:(h + 1) * HEAD_DIM]
        _acc_rows(dqg_ref, jnp.sum(fold, axis=0, keepdims=True), i)
        _acc_rows(dsk_ref, dsink, i)

    cur, prev = _att_blocks(n)
    kvb = pl.BlockSpec((BLOCK, KVW), lambda i: (i, 0))
    qb = pl.BlockSpec((BLOCK, RW), lambda i: (i, 0))
    return _pcall(body, name=name, grid=(n,),
                  in_specs=[cur, prev, qb, _full_spec((1, RW)), _full_spec((1, KVW)), _full_spec((1, LANE))],
                  out_specs=[qb, kvb, kvb, kvb, kvb, _full_spec((1, HEAD_DIM)), _full_spec((1, LANE))],
                  out_shape=[jax.ShapeDtypeStruct((T, RW), F32)] + [jax.ShapeDtypeStruct((T, KVW), F32)] * 4
                  + [jax.ShapeDtypeStruct((1, HEAD_DIM), F32), jax.ShapeDtypeStruct((1, LANE), F32)],
                  compiler_params=_params(("arbitrary",)))(pa, pa, do, qn_g, kn_g, sinks)


def att_kv_bwd(pa, dq, dko, dkn, dvo, dvn, kn_g, *, name):
    T = pa.shape[0]
    n = T // BLOCK

    def body(pa_ref, dq_ref, dko_ref, dkn_ref, dvo_ref, dvn_ref, kg_ref, dpa_ref, dkg_ref):
        i = pl.program_id(0)
        more = i < n - 1
        dkn_tot = dko_ref[...] + jnp.where(more, dkn_ref[...], 0.0)
        dv_tot = dvo_ref[...] + jnp.where(more, dvn_ref[...], 0.0)
        kraw = pa_ref[:, RW:RW + KVW]
        bk = _seg_matrix(KVW, 6)
        _, rk = _qk_norm(kraw, kg_ref[...], bk)
        khat = kraw * rk
        dkh = dkn_tot * kg_ref[...]
        dpa_ref[:, 0:RW] = dq_ref[...]
        dpa_ref[:, RW:RW + KVW] = rk * (dkh - khat * (_segsum(dkh * khat, bk) * (1.0 / HEAD_DIM)))
        dpa_ref[:, RW + KVW:] = dv_tot
        prod = dkn_tot * khat
        _acc_rows(dkg_ref, jnp.sum(prod[:, 0:HEAD_DIM] + prod[:, HEAD_DIM:], axis=0, keepdims=True), i)

    kvb = pl.BlockSpec((BLOCK, KVW), lambda i: (i, 0))
    nxt = pl.BlockSpec((BLOCK, KVW), lambda i: (jnp.minimum(i + 1, n - 1), 0))
    return _pcall(body, name=name, grid=(n,),
                  in_specs=[pl.BlockSpec((BLOCK, ATT_COLS), lambda i: (i, 0)), pl.BlockSpec((BLOCK, RW), lambda i: (i, 0)),
                            kvb, nxt, kvb, nxt, _full_spec((1, KVW))],
                  out_specs=[pl.BlockSpec((BLOCK, ATT_COLS), lambda i: (i, 0)), _full_spec((1, HEAD_DIM))],
                  out_shape=[jax.ShapeDtypeStruct((T, ATT_COLS), F32), jax.ShapeDtypeStruct((1, HEAD_DIM), F32)],
                  compiler_params=_params(("arbitrary",)))(pa, dq, dko, dkn, dvo, dvn, kn_g)


WKV_CHUNK = 64
WKV_GROUP = 8


def _diag_mask():
    i = lax.broadcasted_iota(jnp.int32, (HEAD_DIM, RW), 0)
    j = lax.broadcasted_iota(jnp.int32, (HEAD_DIM, RW), 1) & (HEAD_DIM - 1)
    return i == j


def _heads_matrix():
    head = jnp.arange(RW // 2) // HEAD_DIM
    bd = (head[:, None] == head[None, :]).astype(BF16)
    return jnp.concatenate([bd, bd], axis=0)


def _headsums(xs, pieces, bd2):
    half = RW // 2
    bd = bd2[:pieces * half]
    rows = []
    for x in xs:
        parts, rest = [], x
        for n in range(pieces):
            p = rest.astype(BF16)
            parts.append(p)
            if n + 1 < pieces:
                rest = rest - p.astype(F32)
        for sl in (slice(0, half), slice(half, RW)):
            rows.append(jnp.concatenate([p[:, sl] for p in parts], axis=1))
    out = lax.dot_general(jnp.concatenate(rows, axis=0), bd, (((1,), (0,)), ((), ())), preferred_element_type=F32)
    return [jnp.concatenate([out[2 * n * HEAD_DIM:(2 * n + 1) * HEAD_DIM], out[(2 * n + 1) * HEAD_DIM:(2 * n + 2) * HEAD_DIM]],
                            axis=1) for n in range(len(xs))]


def _headsum(x):
    low = lax.broadcasted_iota(jnp.int32, (HEAD_DIM, LANE), 1) < HEAD_DIM
    tiles = []
    for c in range(RW // LANE):
        xt = x[:, c * LANE:(c + 1) * LANE]
        s_lo = jnp.sum(jnp.where(low, xt, 0.0), axis=1, keepdims=True)
        s_hi = jnp.sum(jnp.where(low, 0.0, xt), axis=1, keepdims=True)
        tiles.append(jnp.where(low, s_lo, s_hi))
    return jnp.concatenate(tiles, axis=1)


def _cols(rows, diag, bd2):
    return _headsums([jnp.where(diag, r, 0.0) for r in rows], 2, bd2)


def _row(x, diag):
    return jnp.sum(jnp.where(diag, x, 0.0), axis=0, keepdims=True)


def wkv_fwd(r, w, k, v, a, b, *, name, gather=()):
    T = r.shape[0]
    ch = min(WKV_CHUNK, T)
    ngroups = ch // WKV_GROUP
    nchunks = T // ch
    ng = len(gather)

    def body(*refs):
        r_ref, w_ref, k_ref, v_ref, a_ref, b_ref, bd_ref = refs[:7]
        y_ref, st_ref = refs[7 + ng:9 + ng]
        s_scr = refs[9 + 2 * ng]
        step = pl.program_id(0)
        if ng:
            plan = _GatherPlan(refs[7:7 + ng], refs[9 + ng:9 + 2 * ng], refs[10 + 2 * ng:])
            pl.when(step == 0)(plan.start)
            pl.when(step == nchunks // 2)(plan.relay)

        @pl.when(step == 0)
        def _():
            s_scr[...] = jnp.zeros_like(s_scr)

        diag, bd2 = _diag_mask(), bd_ref[...]

        def group(gi, S):
            t0 = pl.multiple_of(gi * WKV_GROUP, WKV_GROUP)
            rows = pl.ds(t0, WKV_GROUP)
            R, W, K, V, A, B = (ref[rows, :] for ref in (r_ref, w_ref, k_ref, v_ref, a_ref, b_ref))
            vcols = _cols([V[s:s + 1] for s in range(WKV_GROUP)], diag, bd2)
            yrows = []
            for s in range(WKV_GROUP):
                sa = _headsum(S * A[s:s + 1])
                S = S * W[s:s + 1] + sa * B[s:s + 1] + vcols[s] * K[s:s + 1]
                st_ref[t0 + s] = S
                yrows.append(_row(_headsums([S * R[s:s + 1]], 2, bd2)[0], diag))
            y_ref[rows, :] = jnp.concatenate(yrows, axis=0)
            return S

        s_scr[...] = lax.fori_loop(0, ngroups, group, s_scr[...])
        if ng:
            pl.when(step == nchunks - 1)(plan.finish_relayed)

    vec = pl.BlockSpec((ch, RW), lambda c: (c, 0))
    return _pcall(
        body, name=name, grid=(nchunks,), in_specs=[vec] * 6 + [_full_spec((RW, RW // 2))] + [_HBM] * ng,
        out_specs=[vec, pl.BlockSpec((ch, HEAD_DIM, RW), lambda c: (c, 0, 0))] + [_HBM] * ng,
        out_shape=[jax.ShapeDtypeStruct((T, RW), F32), jax.ShapeDtypeStruct((T, HEAD_DIM, RW), F32)] + _gathered_shapes(gather),
        scratch_shapes=[pltpu.VMEM((HEAD_DIM, RW), F32)] + (_GatherPlan.sems(ng) if ng else []),
        compiler_params=_params(("arbitrary",)),
    )(r, w, k, v, a, b, _heads_matrix(), *gather)


def wkv_bwd(r, w, k, v, a, b, dy, states, *, name, exchange=()):
    T = r.shape[0]
    ch = min(WKV_CHUNK, T)
    nchunks = T // ch
    ngroups = ch // WKV_GROUP
    ne = len(exchange)

    def body(*refs):
        r_ref, w_ref, k_ref, v_ref, a_ref, b_ref, dy_ref, st_ref, stp_ref, bd_ref = refs[:10]
        dr_ref, dw_ref, dk_ref, dv_ref, da_ref, db_ref = refs[10 + ne:16 + ne]
        ds_scr = refs[16 + 2 * ne]
        step = pl.program_id(0)
        if ne:
            plan = _ExchangePlan(refs[10:10 + ne], refs[16 + ne:16 + 2 * ne], refs[17 + 2 * ne:])
            pl.when(step == 0)(plan.start)

        @pl.when(step == 0)
        def _():
            ds_scr[...] = jnp.zeros_like(ds_scr)

        has_prev_chunk = step < nchunks - 1
        diag, bd2 = _diag_mask(), bd_ref[...]
        colsum = lambda x: jnp.sum(x, axis=0, keepdims=True)

        def group(gj, dS):
            gi = ngroups - 1 - gj
            t0 = pl.multiple_of(gi * WKV_GROUP, WKV_GROUP)
            rows = pl.ds(t0, WKV_GROUP)
            R, W, K, V, A, B, DY = (ref[rows, :] for ref in (r_ref, w_ref, k_ref, v_ref, a_ref, b_ref, dy_ref))
            before = jnp.where(gi > 0, st_ref[jnp.maximum(t0 - 1, 0)], jnp.where(has_prev_chunk, stp_ref[0], 0.0))
            prev_state = lambda s: st_ref[t0 + s - 1] if s > 0 else before
            steps = range(WKV_GROUP)
            dycols = _cols([DY[s:s + 1] for s in steps], diag, bd2)
            vcols = _cols([V[s:s + 1] for s in steps], diag, bd2)
            sas = _headsums([prev_state(s) * A[s:s + 1] for s in steps], 1, bd2)
            got = [[None] * WKV_GROUP for _ in range(6)]
            for s in reversed(steps):
                Sp = prev_state(s)
                dS = dS + dycols[s] * R[s:s + 1]
                got[0][s] = colsum(st_ref[t0 + s] * dycols[s])
                got[3][s] = _row(_headsums([dS * K[s:s + 1]], 2, bd2)[0], diag)
                got[2][s] = colsum(dS * vcols[s])
                dsa = _headsum(dS * B[s:s + 1])
                got[5][s] = colsum(dS * sas[s])
                got[1][s] = colsum(dS * Sp)
                got[4][s] = colsum(Sp * dsa)
                dS = dS * W[s:s + 1] + dsa * A[s:s + 1]
            for q, ref in enumerate((dr_ref, dw_ref, dk_ref, dv_ref, da_ref, db_ref)):
                ref[rows, :] = jnp.concatenate(got[q], axis=0)
            return dS

        ds_scr[...] = lax.fori_loop(0, ngroups, group, ds_scr[...])
        if ne:
            pl.when(step == nchunks - 1)(plan.finish)

    vec = pl.BlockSpec((ch, RW), lambda c: (nchunks - 1 - c, 0))
    st_spec = pl.BlockSpec((ch, HEAD_DIM, RW), lambda c: (nchunks - 1 - c, 0, 0))
    stp_spec = pl.BlockSpec((1, HEAD_DIM, RW), lambda c: (jnp.maximum((nchunks - 1 - c) * ch - 1, 0), 0, 0))
    return _pcall(
        body, name=name, grid=(nchunks,), in_specs=[vec] * 7 + [st_spec, stp_spec, _full_spec((RW, RW // 2))] + [_HBM] * ne,
        out_specs=[vec] * 6 + [_HBM] * ne,
        out_shape=[jax.ShapeDtypeStruct((T, RW), F32)] * 6 + [jax.ShapeDtypeStruct(e.shape, e.dtype) for e in exchange],
        scratch_shapes=[pltpu.VMEM((HEAD_DIM, RW), F32)] + (_ExchangePlan.sems(ne) if ne else []),
        compiler_params=_params(("arbitrary",)),
    )(r, w, k, v, a, b, dy, states, states, _heads_matrix(), *exchange)


_HBM = pl.BlockSpec(memory_space=pltpu.HBM)
_MESH = pl.DeviceIdType.MESH


def _place():
    x, y, c = lax.axis_index("x"), lax.axis_index("y"), lax.axis_index("c")
    return x, y, c, [(1 - x, y), (x, 1 - y), (1 - x, 1 - y)]


def _remote(src, dst, send_sem, recv_sem, to):
    return pltpu.make_async_remote_copy(src_ref=src, dst_ref=dst, send_sem=send_sem, recv_sem=recv_sem, device_id=to,
                                        device_id_type=_MESH)


def _dma_sems(*counts):
    return [pltpu.SemaphoreType.DMA((n,)) for n in counts]


class _GatherPlan:
    def __init__(self, ins, outs, sems):
        self.ins, self.outs, self.n = ins, outs, len(ins)
        self.ici_send, self.ici_recv, self.d2d_send, self.d2d_recv, self.local_sems = sems
        x, y, c, chips = _place()
        self.c, self.me, self.sibling = c, 2 * x + y, (x, y, 1 - c)
        self.peers = [(2 * qx + qy, (qx, qy, c)) for qx, qy in chips]

    @staticmethod
    def sems(n):
        return _dma_sems(3 * n, 3 * n, 3 * n, 3 * n, n)

    def _half(self, i, which):
        rh = self.ins[i].shape[0] // 2
        return pl.ds(which * rh, rh)

    def _local(self, i):
        return pltpu.make_async_copy(self.ins[i], self.outs[i].at[self.me], self.local_sems.at[i])

    def _send(self, i, j):
        k, mine = 3 * i + j, self._half(i, self.c)
        return _remote(self.ins[i].at[mine], self.outs[i].at[self.me, mine], self.ici_send.at[k], self.ici_recv.at[k],
                       self.peers[j][1])

    def _landed(self, i, j):
        k, piece = 3 * i + j, self.outs[i].at[self.peers[j][0], self._half(i, self.c)]
        return _remote(piece, piece, self.ici_send.at[k], self.ici_recv.at[k], self.peers[j][1])

    def _pass(self, i, j, which):
        k, piece = 3 * i + j, self.outs[i].at[self.peers[j][0], self._half(i, which)]
        return _remote(piece, piece, self.d2d_send.at[k], self.d2d_recv.at[k], self.sibling)

    def _all(self):
        return [(i, j) for i in range(self.n) for j in range(3)]

    def start(self):
        for i in range(self.n):
            self._local(i).start()
        for i, j in self._all():
            self._send(i, j).start()

    def relay(self):
        for i, j in self._all():
            self._landed(i, j).wait_recv()
            self._pass(i, j, self.c).start()

    def finish_relayed(self):
        for i, j in self._all():
            self._pass(i, j, 1 - self.c).wait_recv()
        for i, j in self._all():
            self._send(i, j).wait_send()
            self._pass(i, j, self.c).wait_send()
        for i in range(self.n):
            self._local(i).wait()

    def finish(self):
        self.relay()
        self.finish_relayed()

    @staticmethod
    def out_shapes(shards):
        return _gathered_shapes(shards)


def _gathered_shapes(shards):
    return [jax.ShapeDtypeStruct((N_CHIPS,) + s.shape, s.dtype) for s in shards]


def gather_weights(shards, *, name):
    n = len(shards)

    def body(*refs):
        plan = _GatherPlan(refs[:n], refs[n:2 * n], refs[2 * n:])
        plan.start()
        plan.finish()

    return _pcall(body, name=name, in_specs=[_HBM] * n, out_specs=[_HBM] * n, out_shape=_gathered_shapes(shards),
                  scratch_shapes=_GatherPlan.sems(n), compiler_params=_params())(*shards)


def to_sibling(arrays, take_other_half, *, name):
    n = len(arrays)

    def body(*refs):
        ins, outs = refs[:n], refs[n:2 * n]
        send_sems, recv_sems = refs[2 * n:]
        x, y, c, _ = _place()
        cps = []
        for i in range(n):
            src = ins[i]
            if take_other_half:
                rh = src.shape[1] // 2
                src = src.at[:, pl.ds((1 - c) * rh, rh)]
            cps.append(_remote(src, outs[i], send_sems.at[i], recv_sems.at[i], (x, y, 1 - c)))
        for cp in cps:
            cp.start()
        for cp in cps:
            cp.wait_recv()
        for cp in cps:
            cp.wait_send()

    def out_of(a):
        shape = (a.shape[0], a.shape[1] // 2, a.shape[2]) if take_other_half else a.shape
        return jax.ShapeDtypeStruct(shape, a.dtype)

    return _pcall(body, name=name, in_specs=[_HBM] * n, out_specs=[_HBM] * n, out_shape=[out_of(a) for a in arrays],
                  scratch_shapes=_dma_sems(n, n), compiler_params=_params())(*arrays)


def exchange_chips(arrays, *, name):
    n = len(arrays)

    def body(*refs):
        plan = _ExchangePlan(refs[:n], refs[n:2 * n], refs[2 * n:])
        plan.start()
        plan.finish()

    return _pcall(body, name=name, in_specs=[_HBM] * n, out_specs=[_HBM] * n,
                  out_shape=[jax.ShapeDtypeStruct(a.shape, a.dtype) for a in arrays],
                  scratch_shapes=_ExchangePlan.sems(n), compiler_params=_params())(*arrays)


class _ExchangePlan:
    def __init__(self, ins, outs, sems):
        self.ins, self.outs, self.n = ins, outs, len(ins)
        self.send_sems, self.recv_sems, self.local_sems = sems
        x, y, c, chips = _place()
        self.me = 2 * x + y
        self.peers = [(2 * qx + qy, (qx, qy, c)) for qx, qy in chips]

    @staticmethod
    def sems(n):
        return _dma_sems(3 * n, 3 * n, n)

    @staticmethod
    def out_shapes(arrays):
        return [jax.ShapeDtypeStruct(a.shape, a.dtype) for a in arrays]

    def _local(self, i):
        return pltpu.make_async_copy(self.ins[i].at[self.me], self.outs[i].at[self.me], self.local_sems.at[i])

    def _send(self, i, j):
        k = 3 * i + j
        return _remote(self.ins[i].at[self.peers[j][0]], self.outs[i].at[self.me], self.send_sems.at[k], self.recv_sems.at[k],
                       self.peers[j][1])

    def _landed(self, i, j):
        k, piece = 3 * i + j, self.outs[i].at[self.peers[j][0]]
        return _remote(piece, piece, self.send_sems.at[k], self.recv_sems.at[k], self.peers[j][1])

    def start(self):
        for i in range(self.n):
            self._local(i).start()
            for j in range(3):
                self._send(i, j).start()

    def finish(self):
        for i in range(self.n):
            for j in range(3):
                self._landed(i, j).wait_recv()
        for i in range(self.n):
            for j in range(3):
                self._send(i, j).wait_send()
            self._local(i).wait()


def _core_index():
    return lax.axis_index("c").astype(jnp.int32).reshape(1)


def pair_sum(g, theirs, wire_dtype, *, name):
    _, R, C = g.shape
    rh = R // 2
    tr = _tile(rh, 256, 16)
    nt = rh // tr

    def body(c_ref, g_ref, t_ref, q_ref, qw_ref):
        q = g_ref[...] + t_ref[...]
        q_ref[...] = q
        qw_ref[...] = q.astype(wire_dtype)

    blk = pl.BlockSpec((1, tr, C), lambda b, i, c_ref: (b, i, 0))
    mine = pl.BlockSpec((1, tr, C), lambda b, i, c_ref: (b, c_ref[0] * nt + i, 0))
    grid_spec = pltpu.PrefetchScalarGridSpec(num_scalar_prefetch=1, grid=(N_CHIPS, nt), in_specs=[mine, blk], out_specs=[blk, blk])
    return _pcall(body, name=name, grid_spec=grid_spec,
                  out_shape=[jax.ShapeDtypeStruct((N_CHIPS, rh, C), F32), jax.ShapeDtypeStruct((N_CHIPS, rh, C), wire_dtype)],
                  compiler_params=_params(("parallel", "parallel")))(_core_index(), g, theirs)


def half_sum(own, landed, *, name):
    _, rh, C = own.shape
    tr = _tile(rh, 256, 16)

    def body(me_ref, own_ref, land_ref, o_ref):
        total = None
        for p in range(N_CHIPS):
            term = jnp.where(me_ref[0] == p, own_ref[p], land_ref[p].astype(F32))
            total = term if total is None else total + term
        o_ref[...] = total

    blk = pl.BlockSpec((N_CHIPS, tr, C), lambda i, me_ref: (0, i, 0))
    grid_spec = pltpu.PrefetchScalarGridSpec(num_scalar_prefetch=1, grid=(rh // tr,), in_specs=[blk, blk],
                                             out_specs=pl.BlockSpec((tr, C), lambda i, me_ref: (i, 0)))
    me = (2 * lax.axis_index("x") + lax.axis_index("y")).astype(jnp.int32).reshape(1)
    return _pcall(body, name=name, grid_spec=grid_spec, out_shape=jax.ShapeDtypeStruct((rh, C), F32),
                  compiler_params=_params(("parallel",)))(me, own, landed)


def adamw(w, m, v, mine, theirs, *, name):
    _, R, C = w.shape
    rh = R // 2
    tr = _tile(rh, 256, 8)
    nt = rh // tr

    def body(c_ref, w_ref, m_ref, v_ref, a_ref, b_ref, g_ref, d_ref, nm_ref, nv_ref):
        is_mine = (pl.program_id(0) // nt) == c_ref[0]
        g = jnp.where(is_mine, a_ref[...], b_ref[...])
        g_ref[...] = g
        nm = ADAM_B1 * m_ref[...] + (1.0 - ADAM_B1) * g
        nv = ADAM_B2 * v_ref[...] + (1.0 - ADAM_B2) * (g * g)
        nm_ref[...] = nm
        nv_ref[...] = nv
        m_hat = nm / (1.0 - ADAM_B1 ** ADAM_STEP)
        v_hat = nv / (1.0 - ADAM_B2 ** ADAM_STEP)
        d_ref[...] = -ADAM_LR * (m_hat / (jnp.sqrt(v_hat) + ADAM_EPS) + ADAM_WD * w_ref[...])

    full = pl.BlockSpec((None, tr, C), lambda i, c_ref: (0, i, 0))
    a_spec = pl.BlockSpec((tr, C), lambda i, c_ref: (jnp.clip(i - c_ref[0] * nt, 0, nt - 1), 0))
    b_spec = pl.BlockSpec((tr, C), lambda i, c_ref: (jnp.clip(i - (1 - c_ref[0]) * nt, 0, nt - 1), 0))
    grid_spec = pltpu.PrefetchScalarGridSpec(num_scalar_prefetch=1, grid=(2 * nt,), in_specs=[full] * 3 + [a_spec, b_spec],
                                             out_specs=[full] * 4)
    return _pcall(body, name=name, grid_spec=grid_spec, out_shape=[jax.ShapeDtypeStruct((1, R, C), F32)] * 4,
                  compiler_params=_params(("arbitrary",)))(_core_index(), w, m, v, mine, theirs)


def _to_blocks(full, axis):
    r, c = full.shape
    if axis == 1:
        return full.reshape(r, N_CHIPS, c // N_CHIPS).transpose(1, 0, 2)
    return full.reshape(N_CHIPS, r // N_CHIPS, c)


def _from_blocks(blocks, axis):
    _, r, c = blocks.shape
    if axis == 1:
        return blocks.transpose(1, 0, 2).reshape(r, N_CHIPS * c)
    return blocks.reshape(N_CHIPS * r, c)


def _ffn_fwd(x, norm, wg, wu, wd, tag, carry=None):
    h = rms_fwd(x, norm, name=tag + "_norm")
    gate, up, act, *carried = mm_fused(h, [wg, wu], _swiglu, [BF16] * 3, name=tag + "_gate_up", carry=carry)
    out = mm(act, wd, scale=0.5, res=x, name=tag + "_down")
    return out, (h, gate, up, act), carried


def _ffn_bwd(dout, x, saved, norm, wg, wu, wd, tag, carry=None):
    h, gate, up, act = saved
    dgate, dup, *carried = mm_fused(dout, [wd], _swiglu_bwd, [BF16] * 2, tb=True, extras=[gate, up], name=tag + "_dact",
                                    carry=carry)
    dwd = mm(act, dout, ta=True, scale=0.5, name=tag + "_dwd")
    dwg = mm(h, dgate, ta=True, name=tag + "_dwg")
    dwu = mm(h, dup, ta=True, name=tag + "_dwu")
    dh = mm(dgate, wg, tb=True, name=tag + "_dh_gate")
    dh = mm(dup, wu, tb=True, res=dh, name=tag + "_dh_up")
    dx, dnorm = rms_bwd(dh, x, norm, dout, name=tag + "_dnorm")
    return dx, dnorm, dwg, dwu, dwd, carried


FIRST_WEIGHTS = ['ffn1_w_gate', 'ffn1_w_up', 'ffn1_w_down']
MID_WEIGHTS = ['w_in', 'rwkv_w_lora_up', 'rwkv_a_lora_up', 'rwkv_g_lora_up']
LATE_WEIGHTS = ['w_branch_rwkv', 'w_branch_attn', 'w_out', 'ffn2_w_gate', 'ffn2_w_up', 'ffn2_w_down']


def _pair_sums(names, blocks, tag):
    from_sibling = to_sibling(blocks, True, name=tag + "_grads_to_sibling")
    return [pair_sum(g, t, F32 if n == 'small' else BF16, name="pair_sum_" + n)
            for g, t, n in zip(blocks, from_sibling, names)]


def _step(A):
    x, tgt = A['x'][0], A['loss_target'][0]
    T = x.shape[0]
    w = {n: A[n][0] for n in WEIGHT_NAMES}
    row = lambda a: a.reshape(1, -1)

    axis_of = dict(BIG)
    shard = lambda n: w[n].astype(BF16)
    n1, nmix, n2, nfin = (row(w[n]) for n in ('ffn1_norm', 'mix_norm', 'ffn2_norm', 'final_norm'))
    gathered = gather_weights([shard(n) for n in FIRST_WEIGHTS], name="gather_weights")
    full = {n: _from_blocks(b, axis_of[n]) for n, b in zip(FIRST_WEIGHTS, gathered)}
    x1, ffn1, gathered = _ffn_fwd(x, n1, full['ffn1_w_gate'], full['ffn1_w_up'], full['ffn1_w_down'], "ffn1",
                                  carry=(_GatherPlan, [shard(n) for n in MID_WEIGHTS]))
    full.update({n: _from_blocks(b, axis_of[n]) for n, b in zip(MID_WEIGHTS, gathered)})
    w_in_r = _pad_rwkv_cols(full['w_in'][:, :RWKV_COLS])
    w_in_a = full['w_in'][:, RWKV_COLS:RWKV_COLS + ATT_COLS]
    w_in_g = full['w_in'][:, RWKV_COLS + ATT_COLS:]
    wlw, wla, wlg = (_pad_rows(full[n], 128).astype(F32) for n in ('rwkv_w_lora_up', 'rwkv_a_lora_up', 'rwkv_g_lora_up'))
    mu = _pad_rwkv_cols(row(w['rwkv_mu']))
    w0, a0, k_k, k_a, r_k, ln_w, ln_b = (row(w[n]) for n in ('rwkv_w0', 'rwkv_a0', 'rwkv_k_k', 'rwkv_k_a', 'rwkv_r_k',
                                                               'rwkv_ln_w', 'rwkv_ln_b'))
    qg = jnp.tile(row(w['attn_q_norm']), (1, N_HEADS))
    kg = jnp.tile(row(w['attn_k_norm']), (1, KVW // HEAD_DIM))
    sinks = jnp.pad(row(w['attn_sinks']), ((0, 0), (0, LANE - N_HEADS)))

    h2 = rms_fwd(x1, nmix, name="mix_norm")
    pr = mm(h2, w_in_r, name="proj_rwkv")
    pa = mm(h2, w_in_a, name="proj_att")
    pg = mm(h2, w_in_g, name="proj_gate")
    pr_shift = jnp.pad(pr, ((1, 0), (0, 0)))[:-1]
    r, dec, k2, v, a, b, sg = rwkv_pre_fwd(pr, pr_shift, mu, w0, a0, k_k, k_a, wlw, wla, wlg, name="rwkv_pre")
    y, states, *gathered = wkv_fwd(r, dec, k2, v, a, b, name="wkv_fwd", gather=[shard(n) for n in LATE_WEIGHTS])
    full.update({n: _from_blocks(b, axis_of[n]) for n, b in zip(LATE_WEIGHTS, gathered)})
    yr = rwkv_post_fwd(y, r, k2, v, sg, wlg, ln_w, ln_b, r_k, name="rwkv_post")
    ya = att_fwd(pa, qg, kg, sinks, name="att_fwd")
    br = mm(yr, full['w_branch_rwkv'], name="branch_rwkv")
    ba = mm(ya, full['w_branch_attn'], name="branch_att")
    mg = merge_fwd(br, ba, pg, name="merge")
    x2 = mm(mg, full['w_out'], res=x1, name="mix_out")
    x3, ffn2, _ = _ffn_fwd(x2, n2, full['ffn2_w_gate'], full['ffn2_w_up'], full['ffn2_w_down'], "ffn2")
    dx3, d_nfin, loss = final_loss(x3, tgt, nfin, name="final_loss")

    G = {'final_norm': d_nfin}
    dx2, G['ffn2_norm'], G['ffn2_w_gate'], G['ffn2_w_up'], G['ffn2_w_down'], _ = _ffn_bwd(
        dx3, x2, ffn2, n2, full['ffn2_w_gate'], full['ffn2_w_up'], full['ffn2_w_down'], "ffn2")
    dmg = mm(dx2, full['w_out'], tb=True, name="d_merge")
    G['w_out'] = mm(mg, dx2, ta=True, name="d_w_out")
    dbr, dba, dpg = merge_bwd(dmg, br, ba, pg, name="merge_bwd")
    dyr = mm(dbr, full['w_branch_rwkv'], tb=True, name="d_y_rwkv")
    G['w_branch_rwkv'] = mm(yr, dbr, ta=True, name="d_w_branch_rwkv")
    dya = mm(dba, full['w_branch_attn'], tb=True, name="d_y_att")
    G['w_branch_attn'] = mm(ya, dba, ta=True, name="d_w_branch_att")
    dy, dz, dg, G['rwkv_ln_w'], G['rwkv_ln_b'] = rwkv_post_bwd(dyr, y, r, k2, v, sg, wlg, ln_w, ln_b, r_k, name="rwkv_post_bwd")
    late_pair = _pair_sums(LATE_WEIGHTS, [_to_blocks(G[n], axis_of[n]) for n in LATE_WEIGHTS], "late")
    res = wkv_bwd(r, dec, k2, v, a, b, dy, states, name="wkv_bwd", exchange=[q for _, q in late_pair])
    wkv_grads, late_landed = res[:6], res[6:]
    (dpr, d_mu, G['rwkv_w0'], G['rwkv_a0'], G['rwkv_k_k'], G['rwkv_k_a'], G['rwkv_r_k'], d_wlw, d_wla, d_wlg) = rwkv_pre_bwd(
        pr, pr_shift, *wkv_grads, dz, dg, mu, w0, a0, k_k, k_a, r_k, wlw, wla, wlg, name="rwkv_pre_bwd")
    G['rwkv_mu'] = _unpad_rwkv_cols(d_mu)
    G['rwkv_w_lora_up'], G['rwkv_a_lora_up'], G['rwkv_g_lora_up'] = d_wlw[:DECAY_LORA], d_wla[:ICLR_LORA], d_wlg[:GATE_LORA]
    dq, dko, dkn, dvo, dvn, G['attn_q_norm'], d_sinks = att_bwd(pa, dya, qg, kg, sinks, name="att_bwd")
    G['attn_sinks'] = d_sinks[:, :N_HEADS]
    dpa, G['attn_k_norm'] = att_kv_bwd(pa, dq, dko, dkn, dvo, dvn, kg, name="att_kv_bwd")
    d_w_in_r = mm(h2, dpr, ta=True, name="d_w_in_rwkv")
    d_w_in_a = mm(h2, dpa, ta=True, name="d_w_in_att")
    d_w_in_g = mm(h2, dpg, ta=True, name="d_w_in_gate")
    G['w_in'] = jnp.concatenate([_unpad_rwkv_cols(d_w_in_r), d_w_in_a, d_w_in_g], axis=1)
    dh2 = mm(dpr, w_in_r, tb=True, name="d_h2_rwkv")
    dh2 = mm(dpa, w_in_a, tb=True, res=dh2, name="d_h2_att")
    dh2 = mm(dpg, w_in_g, tb=True, res=dh2, name="d_h2_gate")
    dx1, G['mix_norm'] = rms_bwd(dh2, x1, nmix, dx2, name="d_mix_norm")
    mid_pair = _pair_sums(MID_WEIGHTS, [_to_blocks(G[n], axis_of[n]) for n in MID_WEIGHTS], "mid")
    dx0, G['ffn1_norm'], G['ffn1_w_gate'], G['ffn1_w_up'], G['ffn1_w_down'], mid_landed = _ffn_bwd(
        dx1, x, ffn1, n1, full['ffn1_w_gate'], full['ffn1_w_up'], full['ffn1_w_down'], "ffn1",
        carry=(_ExchangePlan, [q for _, q in mid_pair]))

    small_shapes = [(w[n].size,) for n in SMALL] + [(1,)]

    def small_rows(parts):
        vec = jnp.concatenate([p.reshape(-1) for p in parts])
        return jnp.pad(vec, (0, SMALL_ROWS * FLAT_W - vec.shape[0])).reshape(SMALL_ROWS, FLAT_W)

    small = small_rows([G[n] for n in SMALL] + [loss[0, :1]])
    first_names = FIRST_WEIGHTS + ['small']
    first_blocks = [_to_blocks(G[n], axis_of[n]) for n in FIRST_WEIGHTS] + [jnp.broadcast_to(small[None], (N_CHIPS,) + small.shape)]
    first_pair = _pair_sums(first_names, first_blocks, "first")
    first_landed = exchange_chips([q for _, q in first_pair], name="exchange_grads")
    names = first_names + MID_WEIGHTS + LATE_WEIGHTS
    pair, landed = first_pair + mid_pair + late_pair, list(first_landed) + list(mid_landed) + list(late_landed)
    halves = [half_sum(own, l, name="half_sum_" + n) for (own, _), l, n in zip(pair, landed, names)]
    other_halves = to_sibling(halves, False, name="halves_to_sibling")

    def local(prefix, n):
        if n != 'small':
            return A[prefix + n]
        return small_rows([A[prefix + s] for s in SMALL] + [jnp.zeros((1,), F32)])[None]

    result = {}
    for n, mine, theirs in zip(names, halves, other_halves):
        outs4 = adamw(local('', n), local('m_', n), local('v_', n), mine, theirs, name="adamw_" + n)
        for kind, o in zip(('grad_', 'delta_', 'new_m_', 'new_v_'), outs4):
            if n != 'small':
                result[kind + n] = o
            else:
                for s, part in zip(SMALL + ['loss'], _unpack_vec(o.reshape(-1), small_shapes)):
                    result[kind + s] = part.reshape(A[s].shape) if s != 'loss' else part.reshape(())
    outs = [result['grad_loss'], dx0[None]]
    for kind in ('grad_', 'delta_', 'new_m_', 'new_v_'):
        outs += [result[kind + n] for n in WEIGHT_NAMES]
    return tuple(outs)


def _unpack_vec(vec, shapes):
    out, off = [], 0
    for (n,) in shapes:
        out.append(vec[off:off + n])
        off += n
    return out


def kernel(x, ffn1_norm, ffn1_w_gate, ffn1_w_up, ffn1_w_down, mix_norm, w_in, rwkv_mu, rwkv_w0, rwkv_w_lora_up, rwkv_a0, rwkv_a_lora_up, rwkv_g_lora_up, rwkv_k_k, rwkv_k_a, rwkv_r_k, rwkv_ln_w, rwkv_ln_b, attn_q_norm, attn_k_norm, attn_sinks, w_branch_rwkv, w_branch_attn, w_out, ffn2_norm, ffn2_w_gate, ffn2_w_up, ffn2_w_down, final_norm, loss_target, m_ffn1_norm, m_ffn1_w_gate, m_ffn1_w_up, m_ffn1_w_down, m_mix_norm, m_w_in, m_rwkv_mu, m_rwkv_w0, m_rwkv_w_lora_up, m_rwkv_a0, m_rwkv_a_lora_up, m_rwkv_g_lora_up, m_rwkv_k_k, m_rwkv_k_a, m_rwkv_r_k, m_rwkv_ln_w, m_rwkv_ln_b, m_attn_q_norm, m_attn_k_norm, m_attn_sinks, m_w_branch_rwkv, m_w_branch_attn, m_w_out, m_ffn2_norm, m_ffn2_w_gate, m_ffn2_w_up, m_ffn2_w_down, m_final_norm, v_ffn1_norm, v_ffn1_w_gate, v_ffn1_w_up, v_ffn1_w_down, v_mix_norm, v_w_in, v_rwkv_mu, v_rwkv_w0, v_rwkv_w_lora_up, v_rwkv_a0, v_rwkv_a_lora_up, v_rwkv_g_lora_up, v_rwkv_k_k, v_rwkv_k_a, v_rwkv_r_k, v_rwkv_ln_w, v_rwkv_ln_b, v_attn_q_norm, v_attn_k_norm, v_attn_sinks, v_w_branch_rwkv, v_w_branch_attn, v_w_out, v_ffn2_norm, v_ffn2_w_gate, v_ffn2_w_up, v_ffn2_w_down, v_final_norm):
    return _step(dict(locals()))
```

```python
import functools

import jax
import jax.numpy as jnp
from jax import lax
from jax.experimental import pallas as pl
from jax.experimental.pallas import tpu as pltpu

F32 = jnp.float32
BF16 = jnp.bfloat16

D_MODEL = 1024
D_FF = 2816
HEAD_DIM = 64
N_HEADS = 8
RW = 512
KVW = 128
ATT_GROUP = 4
WINDOW = 128
BLOCK = 128
DECAY_LORA, ICLR_LORA, GATE_LORA = 32, 32, 96
RWKV_COLS = 3 * RW + DECAY_LORA + ICLR_LORA + GATE_LORA
ATT_COLS = RW + 2 * KVW
GATE_COLS = 2 * D_MODEL
RWKV_PAD = 3 * RW + 3 * 128
RMS_EPS = 1e-6
GN_EPS = 64e-5
N_CHIPS = 4
LANE = 128
FLAT_W = 1024
SMALL_ROWS = 32
NEG_BIG = -1e30

ADAM_LR, ADAM_B1, ADAM_B2, ADAM_EPS, ADAM_WD, ADAM_STEP = 0.001, 0.9, 0.999, 1e-08, 0.01, 10

VMEM_LIMIT = 56 * 1024 * 1024

WEIGHT_NAMES = ['ffn1_norm', 'ffn1_w_gate', 'ffn1_w_up', 'ffn1_w_down', 'mix_norm', 'w_in', 'rwkv_mu', 'rwkv_w0',
                'rwkv_w_lora_up', 'rwkv_a0', 'rwkv_a_lora_up', 'rwkv_g_lora_up', 'rwkv_k_k', 'rwkv_k_a', 'rwkv_r_k',
                'rwkv_ln_w', 'rwkv_ln_b', 'attn_q_norm', 'attn_k_norm', 'attn_sinks', 'w_branch_rwkv',
                'w_branch_attn', 'w_out', 'ffn2_norm', 'ffn2_w_gate', 'ffn2_w_up', 'ffn2_w_down', 'final_norm']
BIG = [('ffn1_w_gate', 1), ('ffn1_w_up', 1), ('ffn1_w_down', 0), ('w_in', 1), ('rwkv_w_lora_up', 1),
       ('rwkv_a_lora_up', 1), ('rwkv_g_lora_up', 1), ('w_branch_rwkv', 1), ('w_branch_attn', 1), ('w_out', 0),
       ('ffn2_w_gate', 1), ('ffn2_w_up', 1), ('ffn2_w_down', 0)]
SMALL = ['ffn1_norm', 'mix_norm', 'rwkv_mu', 'rwkv_w0', 'rwkv_a0', 'rwkv_k_k', 'rwkv_k_a', 'rwkv_r_k', 'rwkv_ln_w',
         'rwkv_ln_b', 'attn_q_norm', 'attn_k_norm', 'attn_sinks', 'ffn2_norm', 'final_norm']


def _pcall(body, **kw):
    return pl.pallas_call(body, **kw)


def _params(sem=None, **kw):
    if sem is not None:
        kw['dimension_semantics'] = sem
    return pltpu.CompilerParams(vmem_limit_bytes=VMEM_LIMIT, **kw)


def _tile(n, cap, mult):
    best = None
    for t in range(mult, min(n, cap) + 1, mult):
        if n % t == 0:
            best = t
    return best or n


def _sigmoid(z):
    return 1.0 / (1.0 + jnp.exp(-z))


def _softplus(z):
    return jnp.maximum(z, 0.0) + jnp.log(1.0 + jnp.exp(-jnp.abs(z)))


def _bdot(a, b, dims=(((1,), (0,)), ((), ()))):
    return lax.dot_general(a.astype(BF16), b.astype(BF16), dims, preferred_element_type=F32)


_NT = (((1,), (1,)), ((), ()))
_TN = (((0,), (0,)), ((), ()))


def _segsum(x, bd):
    hi = x.astype(BF16)
    r1 = x - hi.astype(F32)
    mid = r1.astype(BF16)
    lo = (r1 - mid.astype(F32)).astype(BF16)
    dot = functools.partial(lax.dot_general, dimension_numbers=(((1,), (0,)), ((), ())), preferred_element_type=F32)
    return dot(hi, bd) + dot(mid, bd) + dot(lo, bd)


_LORA_EDGES = (3 * RW, 3 * RW + DECAY_LORA, 3 * RW + DECAY_LORA + ICLR_LORA, RWKV_COLS)


def _pad_rwkv_cols(x):
    parts = [x[..., :3 * RW]]
    for lo, hi in zip(_LORA_EDGES[:-1], _LORA_EDGES[1:]):
        parts.append(jnp.pad(x[..., lo:hi], [(0, 0)] * (x.ndim - 1) + [(0, 128 - (hi - lo))]))
    return jnp.concatenate(parts, axis=-1)


def _unpad_rwkv_cols(x):
    parts = [x[..., :3 * RW]]
    for j, (lo, hi) in enumerate(zip(_LORA_EDGES[:-1], _LORA_EDGES[1:])):
        parts.append(x[..., 3 * RW + 128 * j:3 * RW + 128 * j + (hi - lo)])
    return jnp.concatenate(parts, axis=-1)


def _pad_rows(x, rows):
    return jnp.pad(x, [(0, rows - x.shape[0])] + [(0, 0)] * (x.ndim - 1))


def mm(a, b, *, name, ta=False, tb=False, scale=None, res=None, out_dtype=F32, carry=None):
    M, K = (a.shape[1], a.shape[0]) if ta else a.shape
    N = b.shape[0] if tb else b.shape[1]
    assert (b.shape[1] if tb else b.shape[0]) == K
    tm, tn, tk = _tile(M, 1408 if ta else 512, 128), _tile(N, 1408, 128), _tile(K, 1408, 128)
    nk = K // tk
    grid = (M // tm, N // tn, nk)
    dims = (((0 if ta else 1,), (1 if tb else 0,)), ((), ()))
    plan_cls, carried = carry if carry else (None, ())
    nc, nin = len(carried), 2 + (res is not None)

    def body(*refs):
        a_ref, b_ref = refs[:2]
        r_ref = refs[2] if res is not None else None
        o_ref, acc_ref = refs[nin + nc], refs[nin + 2 * nc + 1]
        k = pl.program_id(2)
        if nc:
            plan = plan_cls(refs[nin:nin + nc], refs[nin + nc + 1:nin + 2 * nc + 1], refs[nin + 2 * nc + 2:])
            at = lambda which: functools.reduce(jnp.logical_and, [pl.program_id(d) == (0 if which == 0 else grid[d] - 1)
                                                                 for d in range(3)])
            pl.when(at(0))(plan.start)
        part = _bdot(a_ref[...], b_ref[...], dims)

        @pl.when(k == 0)
        def _():
            acc_ref[...] = part

        @pl.when(k > 0)
        def _():
            acc_ref[...] += part

        @pl.when(k == nk - 1)
        def _():
            o = acc_ref[...]
            if scale is not None:
                o = o * scale
            if r_ref is not None:
                o = o + r_ref[...].astype(F32)
            o_ref[...] = o.astype(out_dtype)

        if nc:
            pl.when(at(1))(plan.finish)

    a_spec = pl.BlockSpec((tk, tm), lambda i, j, k: (k, i)) if ta else pl.BlockSpec((tm, tk), lambda i, j, k: (i, k))
    b_spec = pl.BlockSpec((tn, tk), lambda i, j, k: (j, k)) if tb else pl.BlockSpec((tk, tn), lambda i, j, k: (k, j))
    o_spec = pl.BlockSpec((tm, tn), lambda i, j, k: (i, j))
    in_specs = [a_spec, b_spec] + ([o_spec] if res is not None else [])
    args = (a, b) + ((res,) if res is not None else ())
    out_shape = jax.ShapeDtypeStruct((M, N), out_dtype)
    if not nc:
        return _pcall(
            body, name=name, grid=grid, in_specs=in_specs, out_specs=o_spec, out_shape=out_shape,
            scratch_shapes=[pltpu.VMEM((tm, tn), F32)], compiler_params=_params(("parallel", "parallel", "arbitrary")),
        )(*args)
    return _pcall(
        body, name=name, grid=grid, in_specs=in_specs + [_HBM] * nc, out_specs=[o_spec] + [_HBM] * nc,
        out_shape=[out_shape] + plan_cls.out_shapes(carried), scratch_shapes=[pltpu.VMEM((tm, tn), F32)] + plan_cls.sems(nc),
        compiler_params=_params(("arbitrary", "arbitrary", "arbitrary")),
    )(*args, *carried)


def mm_fused(a, bs, finish, out_dtypes, *, name, tb=False, extras=(), carry=None):
    M, K = a.shape
    N = bs[0].shape[0] if tb else bs[0].shape[1]
    tm, tn = _tile(M, 512, 128), _tile(N, 1408, 128)
    grid = (M // tm, N // tn)
    dims = (((1,), (1 if tb else 0,)), ((), ()))
    plan_cls, carried = carry if carry else (None, ())
    nc, nb, nx, no = len(carried), len(bs), len(extras), len(out_dtypes)
    nin = 1 + nb + nx

    def body(*refs):
        a_ref, b_refs, x_refs = refs[0], refs[1:1 + nb], refs[1 + nb:nin]
        o_refs = refs[nin + nc:nin + nc + no]
        if nc:
            plan = plan_cls(refs[nin:nin + nc], refs[nin + nc + no:nin + 2 * nc + no], refs[nin + 2 * nc + no:])
            at = lambda which: jnp.logical_and(*[pl.program_id(d) == (0 if which == 0 else grid[d] - 1) for d in range(2)])
            pl.when(at(0))(plan.start)
        av = a_ref[...]
        outs = finish([_bdot(av, b_ref[...], dims) for b_ref in b_refs], [x_ref[...] for x_ref in x_refs])
        for o_ref, o in zip(o_refs, outs):
            o_ref[...] = o.astype(o_ref.dtype)
        if nc:
            pl.when(at(1))(plan.finish)

    a_spec = pl.BlockSpec((tm, K), lambda i, j: (i, 0))
    b_spec = pl.BlockSpec((tn, K), lambda i, j: (j, 0)) if tb else pl.BlockSpec((K, tn), lambda i, j: (0, j))
    o_spec = pl.BlockSpec((tm, tn), lambda i, j: (i, j))
    out_shape = [jax.ShapeDtypeStruct((M, N), d) for d in out_dtypes]
    if not nc:
        return _pcall(body, name=name, grid=grid, in_specs=[a_spec] + [b_spec] * nb + [o_spec] * nx, out_specs=[o_spec] * no,
                      out_shape=out_shape, compiler_params=_params(("parallel", "parallel")))(a, *bs, *extras)
    return _pcall(body, name=name, grid=grid, in_specs=[a_spec] + [b_spec] * nb + [o_spec] * nx + [_HBM] * nc,
                  out_specs=[o_spec] * no + [_HBM] * nc, out_shape=out_shape + plan_cls.out_shapes(carried),
                  scratch_shapes=plan_cls.sems(nc), compiler_params=_params(("arbitrary", "arbitrary")))(a, *bs, *extras, *carried)


def _swiglu(products, _):
    g, u = products
    return g, u, g * _sigmoid(g) * u


def _swiglu_bwd(products, extras):
    (da,), (gate, up) = products, extras
    gv = gate.astype(F32)
    s = _sigmoid(gv)
    return da * 0.5 * up.astype(F32) * s * (1.0 + gv * (1.0 - s)), da * 0.5 * gv * s


def _row_spec(tr, c):
    return pl.BlockSpec((tr, c), lambda i: (i, 0))


def _full_spec(shape):
    return pl.BlockSpec(shape, lambda i: (0,) * len(shape))


def _acc_rows(ref, val, i):
    @pl.when(i == 0)
    def _():
        ref[...] = val

    @pl.when(i > 0)
    def _():
        ref[...] += val


def rms_fwd(x, g, *, name):
    T, D = x.shape
    tr = _tile(T, 512, 8)

    def body(x_ref, g_ref, h_ref):
        xv = x_ref[...]
        r = lax.rsqrt(jnp.mean(xv * xv, axis=-1, keepdims=True) + RMS_EPS)
        h_ref[...] = (xv * r * g_ref[...]).astype(BF16)

    return _pcall(body, name=name, grid=(T // tr,), in_specs=[_row_spec(tr, D), _full_spec((1, D))],
                  out_specs=_row_spec(tr, D), out_shape=jax.ShapeDtypeStruct((T, D), BF16),
                  compiler_params=_params(("parallel",)))(x, g)


def rms_bwd(dh, x, g, res, *, name):
    T, D = x.shape
    tr = _tile(T, 256, 8)

    def body(dh_ref, x_ref, g_ref, res_ref, dx_ref, dg_ref):
        i = pl.program_id(0)
        xv, dhv = x_ref[...], dh_ref[...].astype(F32)
        r = lax.rsqrt(jnp.mean(xv * xv, axis=-1, keepdims=True) + RMS_EPS)
        xh = xv * r
        dxh = dhv * g_ref[...]
        dx_ref[...] = res_ref[...] + r * (dxh - xh * jnp.mean(dxh * xh, axis=-1, keepdims=True))
        _acc_rows(dg_ref, jnp.sum(dhv * xh, axis=0, keepdims=True), i)

    return _pcall(body, name=name, grid=(T // tr,),
                  in_specs=[_row_spec(tr, D), _row_spec(tr, D), _full_spec((1, D)), _row_spec(tr, D)],
                  out_specs=[_row_spec(tr, D), _full_spec((1, D))],
                  out_shape=[jax.ShapeDtypeStruct((T, D), F32), jax.ShapeDtypeStruct((1, D), F32)],
                  compiler_params=_params(("arbitrary",)))(dh, x, g, res)


def final_loss(x, tgt, g, *, name):
    T, D = x.shape
    tr = _tile(T, 256, 8)

    def body(x_ref, t_ref, g_ref, dx_ref, dg_ref, loss_ref):
        i = pl.program_id(0)
        xv = x_ref[...]
        r = lax.rsqrt(jnp.mean(xv * xv, axis=-1, keepdims=True) + RMS_EPS)
        xh = xv * r
        e = xh * g_ref[...] - t_ref[...]
        part = 0.5 * jnp.sum(jnp.mean(e * e, axis=-1, keepdims=True), axis=0, keepdims=True)
        dy = e * (1.0 / D)
        dxh = dy * g_ref[...]
        dx_ref[...] = r * (dxh - xh * jnp.mean(dxh * xh, axis=-1, keepdims=True))
        _acc_rows(dg_ref, jnp.sum(dy * xh, axis=0, keepdims=True), i)
        _acc_rows(loss_ref, jnp.broadcast_to(part, (1, LANE)), i)

    return _pcall(body, name=name, grid=(T // tr,),
                  in_specs=[_row_spec(tr, D), _row_spec(tr, D), _full_spec((1, D))],
                  out_specs=[_row_spec(tr, D), _full_spec((1, D)), _full_spec((1, LANE))],
                  out_shape=[jax.ShapeDtypeStruct((T, D), F32), jax.ShapeDtypeStruct((1, D), F32),
                             jax.ShapeDtypeStruct((1, LANE), F32)],
                  compiler_params=_params(("arbitrary",)))(x, tgt, g)


def merge_fwd(br, ba, pg, *, name):
    T, D = br.shape
    tr = _tile(T, 256, 8)

    def body(br_ref, ba_ref, pg_ref, o_ref):
        pgv = pg_ref[...]
        o_ref[...] = (_sigmoid(pgv[:, :D]) * br_ref[...] + _sigmoid(pgv[:, D:]) * ba_ref[...]).astype(BF16)

    return _pcall(body, name=name, grid=(T // tr,), in_specs=[_row_spec(tr, D), _row_spec(tr, D), _row_spec(tr, 2 * D)],
                  out_specs=_row_spec(tr, D), out_shape=jax.ShapeDtypeStruct((T, D), BF16),
                  compiler_params=_params(("parallel",)))(br, ba, pg)


def merge_bwd(dm, br, ba, pg, *, name):
    T, D = br.shape
    tr = _tile(T, 256, 8)

    def body(dm_ref, br_ref, ba_ref, pg_ref, dbr_ref, dba_ref, dpg_ref):
        pgv, dmv = pg_ref[...], dm_ref[...]
        sr, sa = _sigmoid(pgv[:, :D]), _sigmoid(pgv[:, D:])
        dbr_ref[...] = (dmv * sr).astype(BF16)
        dba_ref[...] = (dmv * sa).astype(BF16)
        dpg_ref[:, :D] = dmv * br_ref[...] * sr * (1.0 - sr)
        dpg_ref[:, D:] = dmv * ba_ref[...] * sa * (1.0 - sa)

    return _pcall(body, name=name, grid=(T // tr,),
                  in_specs=[_row_spec(tr, D), _row_spec(tr, D), _row_spec(tr, D), _row_spec(tr, 2 * D)],
                  out_specs=[_row_spec(tr, D), _row_spec(tr, D), _row_spec(tr, 2 * D)],
                  out_shape=[jax.ShapeDtypeStruct((T, D), BF16), jax.ShapeDtypeStruct((T, D), BF16),
                             jax.ShapeDtypeStruct((T, 2 * D), F32)],
                  compiler_params=_params(("parallel",)))(dm, br, ba, pg)


def _rwkv_mix(p, prev, mu, w0, a0, k_k, k_a, wlw, wla, wlg, bd):
    pp = p + (prev - p) * mu
    r, k, v = pp[:, 0:RW], pp[:, RW:2 * RW], pp[:, 2 * RW:3 * RW]
    xw, xa, xg = pp[:, 3 * RW:3 * RW + 128], pp[:, 3 * RW + 128:3 * RW + 256], pp[:, 3 * RW + 256:3 * RW + 384]
    th = jnp.tanh(xw)
    z = -(w0 + _bdot(th, wlw))
    e = jnp.exp(-_softplus(z) - 0.5)
    decay = jnp.exp(-e)
    a = _sigmoid(a0 + _bdot(xa, wla))
    sg = _sigmoid(xg)
    kkr = k * k_k
    n = jnp.sqrt(_segsum(kkr * kkr, bd))
    kk = kkr / jnp.maximum(n, 1e-12)
    k2 = k * (1.0 + (a - 1.0) * k_a)
    return dict(r=r, k=k, v=v, xa=xa, th=th, z=z, e=e, decay=decay, a=a, sg=sg, n=n, kk=kk, k2=k2)


def _seg_matrix(n, shift):
    r = lax.shift_right_logical(lax.broadcasted_iota(jnp.int32, (n, n), 0), shift)
    c = lax.shift_right_logical(lax.broadcasted_iota(jnp.int32, (n, n), 1), shift)
    return jnp.where(r == c, 1.0, 0.0).astype(BF16)


def rwkv_pre_fwd(p, pshift, mu, w0, a0, k_k, k_a, wlw, wla, wlg, *, name):
    T = p.shape[0]
    tr = _tile(T, 256, 8)

    def body(p_ref, ps_ref, mu_ref, w0_ref, a0_ref, kk_ref, ka_ref, wlw_ref, wla_ref, wlg_ref,
             r_ref, w_ref, k_ref, v_ref, a_ref, b_ref, g_ref):
        pv, prev = p_ref[...], ps_ref[...]
        m = _rwkv_mix(pv, prev, mu_ref[...], w0_ref[...], a0_ref[...], kk_ref[...], ka_ref[...],
                      wlw_ref[...], wla_ref[...], wlg_ref[...], _seg_matrix(RW, 6))
        r_ref[...] = m['r']
        w_ref[...] = m['decay']
        k_ref[...] = m['k2']
        v_ref[...] = m['v']
        a_ref[...] = -m['kk']
        b_ref[...] = m['kk'] * m['a']
        g_ref[...] = m['sg']

    vec = _row_spec(tr, RW)
    return _pcall(
        body, name=name, grid=(T // tr,),
        in_specs=[_row_spec(tr, RWKV_PAD), _row_spec(tr, RWKV_PAD), _full_spec((1, RWKV_PAD))] + [_full_spec((1, RW))] * 4
        + [_full_spec((128, RW))] * 3,
        out_specs=[vec] * 6 + [_row_spec(tr, 128)],
        out_shape=[jax.ShapeDtypeStruct((T, RW), F32)] * 6 + [jax.ShapeDtypeStruct((T, 128), F32)],
        compiler_params=_params(("parallel",)),
    )(p, pshift, mu, w0, a0, k_k, k_a, wlw, wla, wlg)


def _group_norm(y, bd):
    mean = _segsum(y, bd) * (1.0 / HEAD_DIM)
    yc = y - mean
    rstd = lax.rsqrt(_segsum(yc * yc, bd) * (1.0 / HEAD_DIM) + GN_EPS)
    return yc * rstd, rstd


def rwkv_post_fwd(y, r, k2, v, sg, wlg, ln_w, ln_b, r_k, *, name):
    T = y.shape[0]
    tr = _tile(T, 256, 8)

    def body(y_ref, r_ref, k_ref, v_ref, sg_ref, wlg_ref, lw_ref, lb_ref, rk_ref, o_ref):
        bd = _seg_matrix(RW, 6)
        yn, _ = _group_norm(y_ref[...], bd)
        s = _segsum(r_ref[...] * k_ref[...] * rk_ref[...], bd)
        g = _bdot(sg_ref[...], wlg_ref[...])
        o_ref[...] = ((yn * lw_ref[...] + lb_ref[...] + s * v_ref[...]) * g).astype(BF16)

    vec = _row_spec(tr, RW)
    return _pcall(body, name=name, grid=(T // tr,),
                  in_specs=[vec] * 4 + [_row_spec(tr, 128), _full_spec((128, RW))] + [_full_spec((1, RW))] * 3, out_specs=vec,
                  out_shape=jax.ShapeDtypeStruct((T, RW), BF16), compiler_params=_params(("parallel",)))(
                      y, r, k2, v, sg, wlg, ln_w, ln_b, r_k)


def rwkv_post_bwd(dyr, y, r, k2, v, sg, wlg, ln_w, ln_b, r_k, *, name):
    T = y.shape[0]
    tr = _tile(T, 256, 8)

    def body(dyr_ref, y_ref, r_ref, k_ref, v_ref, sg_ref, wlg_ref, lw_ref, lb_ref, rk_ref,
             dy_ref, dz_ref, dg_ref, dlw_ref, dlb_ref):
        i = pl.program_id(0)
        bd = _seg_matrix(RW, 6)
        yn, rstd = _group_norm(y_ref[...], bd)
        s = _segsum(r_ref[...] * k_ref[...] * rk_ref[...], bd)
        dyrv = dyr_ref[...]
        dg_ref[...] = dyrv * (yn * lw_ref[...] + lb_ref[...] + s * v_ref[...])
        dz = dyrv * _bdot(sg_ref[...], wlg_ref[...])
        dz_ref[...] = dz
        dyn = dz * lw_ref[...]
        inv = 1.0 / HEAD_DIM
        dy_ref[...] = rstd * (dyn - _segsum(dyn, bd) * inv - yn * (_segsum(dyn * yn, bd) * inv))
        _acc_rows(dlw_ref, jnp.sum(dz * yn, axis=0, keepdims=True), i)
        _acc_rows(dlb_ref, jnp.sum(dz, axis=0, keepdims=True), i)

    vec = _row_spec(tr, RW)
    one = _full_spec((1, RW))
    return _pcall(body, name=name, grid=(T // tr,),
                  in_specs=[vec] * 5 + [_row_spec(tr, 128), _full_spec((128, RW))] + [one] * 3, out_specs=[vec] * 3 + [one] * 2,
                  out_shape=[jax.ShapeDtypeStruct((T, RW), F32)] * 3 + [jax.ShapeDtypeStruct((1, RW), F32)] * 2,
                  compiler_params=_params(("arbitrary",)))(dyr, y, r, k2, v, sg, wlg, ln_w, ln_b, r_k)


def rwkv_pre_bwd(p, pshift, dr_w, dw_w, dk_w, dv_w, da_w, db_w, dz, dg, mu, w0, a0, k_k, k_a, r_k, wlw, wla, wlg, *, name):
    T = p.shape[0]
    tr = _tile(T, 256, 8)
    n = T // tr

    def body(p_ref, ps_ref, dr_ref, dw_ref, dk_ref, dv_ref, da_ref, db_ref, dz_ref, dg_ref,
             mu_ref, w0_ref, a0_ref, kk_ref, ka_ref, rk_ref, wlw_ref, wla_ref, wlg_ref,
             dp_ref, dmu_ref, dw0_ref, da0_ref, dkk_ref, dka_ref, drk_ref, dwlw_ref, dwla_ref, dwlg_ref,
             carry, dpp, acc_w, acc_a, acc_g):
        i = pl.program_id(0)

        @pl.when(i == 0)
        def _():
            carry[...] = jnp.zeros_like(carry)

        pv, prev, mu = p_ref[...], ps_ref[...], mu_ref[...]
        bd = _seg_matrix(RW, 6)
        k_k, k_a, r_k = kk_ref[...], ka_ref[...], rk_ref[...]
        m = _rwkv_mix(pv, prev, mu, w0_ref[...], a0_ref[...], k_k, k_a, wlw_ref[...], wla_ref[...], wlg_ref[...], bd)
        r, k, v, a, kk, k2 = m['r'], m['k'], m['v'], m['a'], m['kk'], m['k2']
        dzv, dgv = dz_ref[...], dg_ref[...]
        s = _segsum(r * k2 * r_k, bd)
        ds = _segsum(dzv * v, bd)
        dr = dr_ref[...] + ds * k2 * r_k
        dk2 = dk_ref[...] + ds * r * r_k
        dv = dv_ref[...] + dzv * s
        dbv = db_ref[...]
        dkk = dbv * a - da_ref[...]
        da = dbv * kk + dk2 * k * k_a
        dk = dk2 * (1.0 + (a - 1.0) * k_a)
        nmax = jnp.maximum(m['n'], 1e-12)
        dkkr = jnp.where(m['n'] > 1e-12, dkk - kk * _segsum(dkk * kk, bd), dkk) / nmax
        dk = dk + dkkr * k_k
        dapre = da * a * (1.0 - a)
        dwpre = dw_ref[...] * m['decay'] * (-m['e']) * _sigmoid(m['z'])
        dth = _bdot(dwpre, wlw_ref[...], _NT)
        dxa = _bdot(dapre, wla_ref[...], _NT)
        dsg = _bdot(dgv, wlg_ref[...], _NT)
        dpp[:, 0:RW] = dr
        dpp[:, RW:2 * RW] = dk
        dpp[:, 2 * RW:3 * RW] = dv
        dpp[:, 3 * RW:3 * RW + 128] = dth * (1.0 - m['th'] * m['th'])
        dpp[:, 3 * RW + 128:3 * RW + 256] = dxa
        dpp[:, 3 * RW + 256:3 * RW + 384] = dsg * m['sg'] * (1.0 - m['sg'])
        d = dpp[...]
        zed = d * mu
        last = lax.broadcasted_iota(jnp.int32, pv.shape, 0) == tr - 1
        dp_ref[...] = d * (1.0 - mu) + jnp.where(last, carry[0:1, :], pltpu.roll(zed, tr - 1, 0))
        carry[...] = zed[0:8, :]

        def colsum(x):
            return jnp.sum(x, axis=0, keepdims=True)

        _acc_rows(dmu_ref, colsum(d * (prev - pv)), i)
        _acc_rows(dw0_ref, colsum(dwpre), i)
        _acc_rows(da0_ref, colsum(dapre), i)
        _acc_rows(dkk_ref, colsum(dkkr * k), i)
        _acc_rows(dka_ref, colsum(dk2 * k * (a - 1.0)), i)
        _acc_rows(drk_ref, colsum(ds * r * k2), i)
        _acc_rows(acc_w, _bdot(m['th'], dwpre, _TN), i)
        _acc_rows(acc_a, _bdot(m['xa'], dapre, _TN), i)
        _acc_rows(acc_g, _bdot(m['sg'], dgv, _TN), i)

        @pl.when(i == n - 1)
        def _():
            dwlw_ref[...] = acc_w[...]
            dwla_ref[...] = acc_a[...]
            dwlg_ref[...] = acc_g[...]

    rev = lambda c: pl.BlockSpec((tr, c), lambda i: (n - 1 - i, 0))
    one, lora = _full_spec((1, RW)), _full_spec((128, RW))
    return _pcall(
        body, name=name, grid=(n,),
        in_specs=[rev(RWKV_PAD), rev(RWKV_PAD)] + [rev(RW)] * 8 + [_full_spec((1, RWKV_PAD))] + [one] * 5 + [lora] * 3,
        out_specs=[rev(RWKV_PAD), _full_spec((1, RWKV_PAD))] + [one] * 5 + [lora] * 3,
        out_shape=[jax.ShapeDtypeStruct((T, RWKV_PAD), F32), jax.ShapeDtypeStruct((1, RWKV_PAD), F32)]
        + [jax.ShapeDtypeStruct((1, RW), F32)] * 5 + [jax.ShapeDtypeStruct((128, RW), F32)] * 3,
        scratch_shapes=[pltpu.VMEM((8, RWKV_PAD), F32), pltpu.VMEM((tr, RWKV_PAD), F32)] + [pltpu.VMEM((128, RW), F32)] * 3,
        compiler_params=_params(("arbitrary",)),
    )(p, pshift, dr_w, dw_w, dk_w, dv_w, da_w, db_w, dz, dg, mu, w0, a0, k_k, k_a, r_k, wlw, wla, wlg)


def _qk_norm(x, g, bd):
    r = lax.rsqrt(_segsum(x * x, bd) * (1.0 / HEAD_DIM) + RMS_EPS)
    return x * r * g, r


def _att_mask(i):
    qi = lax.broadcasted_iota(jnp.int32, (BLOCK, 2 * BLOCK), 0)
    kj = lax.broadcasted_iota(jnp.int32, (BLOCK, 2 * BLOCK), 1)
    band = (kj <= qi + BLOCK) & (kj > qi + BLOCK - WINDOW)
    return band & ((kj >= BLOCK) | (i > 0))


_HQK = (((2,), (2,)), ((0,), (0,)))
_HPV = (((2,), (1,)), ((0,), (0,)))
_HTN = (((1,), (1,)), ((0,), (0,)))


def _heads(x, n):
    return jnp.stack([x[:, h * HEAD_DIM:(h + 1) * HEAD_DIM] for h in range(n)])


def _unheads(x3):
    return jnp.concatenate([x3[h] for h in range(x3.shape[0])], axis=1)


def _kv_heads(x):
    x2 = _heads(x, KVW // HEAD_DIM)
    return jnp.concatenate([x2[g:g + 1] for g in range(KVW // HEAD_DIM) for _ in range(ATT_GROUP)], axis=0)


def _sinks3(sk):
    return jnp.stack([sk[0:1, h:h + 1] for h in range(N_HEADS)])


def _att_probs(q3, k3, mask, sink):
    s = _bdot(q3, k3, _HQK) * (HEAD_DIM ** -0.5)
    s = jnp.where(mask[None], s, NEG_BIG)
    m = jnp.maximum(jnp.max(s, axis=-1, keepdims=True), sink)
    pexp = jnp.exp(s - m)
    psink = jnp.exp(sink - m)
    inv = 1.0 / (jnp.sum(pexp, axis=-1, keepdims=True) + psink)
    return pexp * inv, psink * inv


def _att_blocks(n):
    cur = pl.BlockSpec((BLOCK, ATT_COLS), lambda i: (i, 0))
    prev = pl.BlockSpec((BLOCK, ATT_COLS), lambda i: (jnp.maximum(i - 1, 0), 0))
    return cur, prev


def _att_qkv(cur, prev, qn_g, kn_g):
    bq, bk = _seg_matrix(RW, 6), _seg_matrix(KVW, 6)
    qn, rq = _qk_norm(cur[:, 0:RW], qn_g, bq)
    kcur, rkc = _qk_norm(cur[:, RW:RW + KVW], kn_g, bk)
    kprev, _ = _qk_norm(prev[:, RW:RW + KVW], kn_g, bk)
    kc = jnp.concatenate([kprev, kcur], axis=0)
    vc = jnp.concatenate([prev[:, RW + KVW:], cur[:, RW + KVW:]], axis=0)
    return qn, rq, kc, vc, rkc


def att_fwd(pa, qn_g, kn_g, sinks, *, name):
    T = pa.shape[0]
    n = T // BLOCK

    def body(cur_ref, prev_ref, qg_ref, kg_ref, sk_ref, o_ref):
        i = pl.program_id(0)
        qn, _, kc, vc, _ = _att_qkv(cur_ref[...], prev_ref[...], qg_ref[...], kg_ref[...])
        probs, _ = _att_probs(_heads(qn, N_HEADS), _kv_heads(kc), _att_mask(i), _sinks3(sk_ref[...]))
        o_ref[...] = _unheads(_bdot(probs, _kv_heads(vc), _HPV))

    cur, prev = _att_blocks(n)
    return _pcall(body, name=name, grid=(n,),
                  in_specs=[cur, prev, _full_spec((1, RW)), _full_spec((1, KVW)), _full_spec((1, LANE))],
                  out_specs=pl.BlockSpec((BLOCK, RW), lambda i: (i, 0)), out_shape=jax.ShapeDtypeStruct((T, RW), F32),
                  compiler_params=_params(("parallel",)))(pa, pa, qn_g, kn_g, sinks)


def att_bwd(pa, do, qn_g, kn_g, sinks, *, name):
    T = pa.shape[0]
    n = T // BLOCK

    def body(cur_ref, prev_ref, do_ref, qg_ref, kg_ref, sk_ref,
             dq_ref, dko_ref, dkn_ref, dvo_ref, dvn_ref, dqg_ref, dsk_ref):
        i = pl.program_id(0)
        cur = cur_ref[...]
        qn, rq, kc, vc, _ = _att_qkv(cur, prev_ref[...], qg_ref[...], kg_ref[...])
        q3, k3, v3, do3 = _heads(qn, N_HEADS), _kv_heads(kc), _kv_heads(vc), _heads(do_ref[...], N_HEADS)
        probs, psink = _att_probs(q3, k3, _att_mask(i), _sinks3(sk_ref[...]))
        dprobs = _bdot(do3, v3, _HQK)
        delta = jnp.sum(probs * dprobs, axis=-1, keepdims=True)
        ds = probs * (dprobs - delta) * (HEAD_DIM ** -0.5)
        dsink3 = -jnp.sum(psink * delta, axis=1, keepdims=True)
        lane = lax.broadcasted_iota(jnp.int32, (1, LANE), 1)
        dsink = jnp.zeros((1, LANE), F32)
        for h in range(N_HEADS):
            dsink = dsink + jnp.where(lane == h, dsink3[h], 0.0)
        dqn = _unheads(_bdot(ds, k3, _HPV))

        def per_kv_head(x3):
            groups = [sum(x3[g * ATT_GROUP + j] for j in range(ATT_GROUP)) for g in range(KVW // HEAD_DIM)]
            return jnp.concatenate(groups, axis=1)

        dk, dv = per_kv_head(_bdot(ds, q3, _HTN)), per_kv_head(_bdot(probs, do3, _HTN))
        dkn_ref[...], dko_ref[...] = dk[0:BLOCK], dk[BLOCK:]
        dvn_ref[...], dvo_ref[...] = dv[0:BLOCK], dv[BLOCK:]
        qhat = cur[:, 0:RW] * rq
        dqh = dqn * qg_ref[...]
        dq_ref[...] = rq * (dqh - qhat * (_segsum(dqh * qhat, _seg_matrix(RW, 6)) * (1.0 / HEAD_DIM)))
        prod = dqn * qhat
        fold = prod[:, 0:HEAD_DIM]
        for h in range(1, N_HEADS):
            fold = fold + prod[:, h * HEAD_DIM:(h + 1) * HEAD_DIM]
        _acc_rows(dqg_ref, jnp.sum(fold, axis=0, keepdims=True), i)
        _acc_rows(dsk_ref, dsink, i)

    cur, prev = _att_blocks(n)
    kvb = pl.BlockSpec((BLOCK, KVW), lambda i: (i, 0))
    qb = pl.BlockSpec((BLOCK, RW), lambda i: (i, 0))
    return _pcall(body, name=name, grid=(n,),
                  in_specs=[cur, prev, qb, _full_spec((1, RW)), _full_spec((1, KVW)), _full_spec((1, LANE))],
                  out_specs=[qb, kvb, kvb, kvb, kvb, _full_spec((1, HEAD_DIM)), _full_spec((1, LANE))],
                  out_shape=[jax.ShapeDtypeStruct((T, RW), F32)] + [jax.ShapeDtypeStruct((T, KVW), F32)] * 4
                  + [jax.ShapeDtypeStruct((1, HEAD_DIM), F32), jax.ShapeDtypeStruct((1, LANE), F32)],
                  compiler_params=_params(("arbitrary",)))(pa, pa, do, qn_g, kn_g, sinks)


def att_kv_bwd(pa, dq, dko, dkn, dvo, dvn, kn_g, *, name):
    T = pa.shape[0]
    n = T // BLOCK

    def body(pa_ref, dq_ref, dko_ref, dkn_ref, dvo_ref, dvn_ref, kg_ref, dpa_ref, dkg_ref):
        i = pl.program_id(0)
        more = i < n - 1
        dkn_tot = dko_ref[...] + jnp.where(more, dkn_ref[...], 0.0)
        dv_tot = dvo_ref[...] + jnp.where(more, dvn_ref[...], 0.0)
        kraw = pa_ref[:, RW:RW + KVW]
        bk = _seg_matrix(KVW, 6)
        _, rk = _qk_norm(kraw, kg_ref[...], bk)
        khat = kraw * rk
        dkh = dkn_tot * kg_ref[...]
        dpa_ref[:, 0:RW] = dq_ref[...]
        dpa_ref[:, RW:RW + KVW] = rk * (dkh - khat * (_segsum(dkh * khat, bk) * (1.0 / HEAD_DIM)))
        dpa_ref[:, RW + KVW:] = dv_tot
        prod = dkn_tot * khat
        _acc_rows(dkg_ref, jnp.sum(prod[:, 0:HEAD_DIM] + prod[:, HEAD_DIM:], axis=0, keepdims=True), i)

    kvb = pl.BlockSpec((BLOCK, KVW), lambda i: (i, 0))
    nxt = pl.BlockSpec((BLOCK, KVW), lambda i: (jnp.minimum(i + 1, n - 1), 0))
    return _pcall(body, name=name, grid=(n,),
                  in_specs=[pl.BlockSpec((BLOCK, ATT_COLS), lambda i: (i, 0)), pl.BlockSpec((BLOCK, RW), lambda i: (i, 0)),
                            kvb, nxt, kvb, nxt, _full_spec((1, KVW))],
                  out_specs=[pl.BlockSpec((BLOCK, ATT_COLS), lambda i: (i, 0)), _full_spec((1, HEAD_DIM))],
                  out_shape=[jax.ShapeDtypeStruct((T, ATT_COLS), F32), jax.ShapeDtypeStruct((1, HEAD_DIM), F32)],
                  compiler_params=_params(("arbitrary",)))(pa, dq, dko, dkn, dvo, dvn, kn_g)


WKV_CHUNK = 64
WKV_GROUP = 8


def _diag_mask():
    i = lax.broadcasted_iota(jnp.int32, (HEAD_DIM, RW), 0)
    j = lax.broadcasted_iota(jnp.int32, (HEAD_DIM, RW), 1) & (HEAD_DIM - 1)
    return i == j


def _heads_matrix():
    head = jnp.arange(RW // 2) // HEAD_DIM
    bd = (head[:, None] == head[None, :]).astype(BF16)
    return jnp.concatenate([bd, bd], axis=0)


def _headsums(xs, pieces, bd2):
    half = RW // 2
    bd = bd2[:pieces * half]
    rows = []
    for x in xs:
        parts, rest = [], x
        for n in range(pieces):
            p = rest.astype(BF16)
            parts.append(p)
            if n + 1 < pieces:
                rest = rest - p.astype(F32)
        for sl in (slice(0, half), slice(half, RW)):
            rows.append(jnp.concatenate([p[:, sl] for p in parts], axis=1))
    out = lax.dot_general(jnp.concatenate(rows, axis=0), bd, (((1,), (0,)), ((), ())), preferred_element_type=F32)
    return [jnp.concatenate([out[2 * n * HEAD_DIM:(2 * n + 1) * HEAD_DIM], out[(2 * n + 1) * HEAD_DIM:(2 * n + 2) * HEAD_DIM]],
                            axis=1) for n in range(len(xs))]


def _headsum(x):
    low = lax.broadcasted_iota(jnp.int32, (HEAD_DIM, LANE), 1) < HEAD_DIM
    tiles = []
    for c in range(RW // LANE):
        xt = x[:, c * LANE:(c + 1) * LANE]
        s_lo = jnp.sum(jnp.where(low, xt, 0.0), axis=1, keepdims=True)
        s_hi = jnp.sum(jnp.where(low, 0.0, xt), axis=1, keepdims=True)
        tiles.append(jnp.where(low, s_lo, s_hi))
    return jnp.concatenate(tiles, axis=1)


def _cols(rows, diag, bd2):
    return _headsums([jnp.where(diag, r, 0.0) for r in rows], 2, bd2)


def _row(x, diag):
    return jnp.sum(jnp.where(diag, x, 0.0), axis=0, keepdims=True)


def wkv_fwd(r, w, k, v, a, b, *, name, gather=()):
    T = r.shape[0]
    ch = min(WKV_CHUNK, T)
    ngroups = ch // WKV_GROUP
    nchunks = T // ch
    ng = len(gather)

    def body(*refs):
        r_ref, w_ref, k_ref, v_ref, a_ref, b_ref, bd_ref = refs[:7]
        y_ref, st_ref = refs[7 + ng:9 + ng]
        s_scr = refs[9 + 2 * ng]
        step = pl.program_id(0)
        if ng:
            plan = _GatherPlan(refs[7:7 + ng], refs[9 + ng:9 + 2 * ng], refs[10 + 2 * ng:])
            pl.when(step == 0)(plan.start)
            pl.when(step == nchunks // 2)(plan.relay)

        @pl.when(step == 0)
        def _():
            s_scr[...] = jnp.zeros_like(s_scr)

        diag, bd2 = _diag_mask(), bd_ref[...]

        def group(gi, S):
            t0 = pl.multiple_of(gi * WKV_GROUP, WKV_GROUP)
            rows = pl.ds(t0, WKV_GROUP)
            R, W, K, V, A, B = (ref[rows, :] for ref in (r_ref, w_ref, k_ref, v_ref, a_ref, b_ref))
            vcols = _cols([V[s:s + 1] for s in range(WKV_GROUP)], diag, bd2)
            yrows = []
            for s in range(WKV_GROUP):
                sa = _headsum(S * A[s:s + 1])
                S = S * W[s:s + 1] + sa * B[s:s + 1] + vcols[s] * K[s:s + 1]
                st_ref[t0 + s] = S
                yrows.append(_row(_headsums([S * R[s:s + 1]], 2, bd2)[0], diag))
            y_ref[rows, :] = jnp.concatenate(yrows, axis=0)
            return S

        s_scr[...] = lax.fori_loop(0, ngroups, group, s_scr[...])
        if ng:
            pl.when(step == nchunks - 1)(plan.finish_relayed)

    vec = pl.BlockSpec((ch, RW), lambda c: (c, 0))
    return _pcall(
        body, name=name, grid=(nchunks,), in_specs=[vec] * 6 + [_full_spec((RW, RW // 2))] + [_HBM] * ng,
        out_specs=[vec, pl.BlockSpec((ch, HEAD_DIM, RW), lambda c: (c, 0, 0))] + [_HBM] * ng,
        out_shape=[jax.ShapeDtypeStruct((T, RW), F32), jax.ShapeDtypeStruct((T, HEAD_DIM, RW), F32)] + _gathered_shapes(gather),
        scratch_shapes=[pltpu.VMEM((HEAD_DIM, RW), F32)] + (_GatherPlan.sems(ng) if ng else []),
        compiler_params=_params(("arbitrary",)),
    )(r, w, k, v, a, b, _heads_matrix(), *gather)


def wkv_bwd(r, w, k, v, a, b, dy, states, *, name, exchange=()):
    T = r.shape[0]
    ch = min(WKV_CHUNK, T)
    nchunks = T // ch
    ngroups = ch // WKV_GROUP
    ne = len(exchange)

    def body(*refs):
        r_ref, w_ref, k_ref, v_ref, a_ref, b_ref, dy_ref, st_ref, stp_ref, bd_ref = refs[:10]
        dr_ref, dw_ref, dk_ref, dv_ref, da_ref, db_ref = refs[10 + ne:16 + ne]
        ds_scr = refs[16 + 2 * ne]
        step = pl.program_id(0)
        if ne:
            plan = _ExchangePlan(refs[10:10 + ne], refs[16 + ne:16 + 2 * ne], refs[17 + 2 * ne:])
            pl.when(step == 0)(plan.start)

        @pl.when(step == 0)
        def _():
            ds_scr[...] = jnp.zeros_like(ds_scr)

        has_prev_chunk = step < nchunks - 1
        diag, bd2 = _diag_mask(), bd_ref[...]
        colsum = lambda x: jnp.sum(x, axis=0, keepdims=True)

        def group(gj, dS):
            gi = ngroups - 1 - gj
            t0 = pl.multiple_of(gi * WKV_GROUP, WKV_GROUP)
            rows = pl.ds(t0, WKV_GROUP)
            R, W, K, V, A, B, DY = (ref[rows, :] for ref in (r_ref, w_ref, k_ref, v_ref, a_ref, b_ref, dy_ref))
            before = jnp.where(gi > 0, st_ref[jnp.maximum(t0 - 1, 0)], jnp.where(has_prev_chunk, stp_ref[0], 0.0))
            prev_state = lambda s: st_ref[t0 + s - 1] if s > 0 else before
            steps = range(WKV_GROUP)
            dycols = _cols([DY[s:s + 1] for s in steps], diag, bd2)
            vcols = _cols([V[s:s + 1] for s in steps], diag, bd2)
            sas = _headsums([prev_state(s) * A[s:s + 1] for s in steps], 1, bd2)
            got = [[None] * WKV_GROUP for _ in range(6)]
            for s in reversed(steps):
                Sp = prev_state(s)
                dS = dS + dycols[s] * R[s:s + 1]
                got[0][s] = colsum(st_ref[t0 + s] * dycols[s])
                got[3][s] = _row(_headsums([dS * K[s:s + 1]], 2, bd2)[0], diag)
                got[2][s] = colsum(dS * vcols[s])
                dsa = _headsum(dS * B[s:s + 1])
                got[5][s] = colsum(dS * sas[s])
                got[1][s] = colsum(dS * Sp)
                got[4][s] = colsum(Sp * dsa)
                dS = dS * W[s:s + 1] + dsa * A[s:s + 1]
            for q, ref in enumerate((dr_ref, dw_ref, dk_ref, dv_ref, da_ref, db_ref)):
                ref[rows, :] = jnp.concatenate(got[q], axis=0)
            return dS

        ds_scr[...] = lax.fori_loop(0, ngroups, group, ds_scr[...])
        if ne:
            pl.when(step == nchunks - 1)(plan.finish)

    vec = pl.BlockSpec((ch, RW), lambda c: (nchunks - 1 - c, 0))
    st_spec = pl.BlockSpec((ch, HEAD_DIM, RW), lambda c: (nchunks - 1 - c, 0, 0))
    stp_spec = pl.BlockSpec((1, HEAD_DIM, RW), lambda c: (jnp.maximum((nchunks - 1 - c) * ch - 1, 0), 0, 0))
    return _pcall(
        body, name=name, grid=(nchunks,), in_specs=[vec] * 7 + [st_spec, stp_spec, _full_spec((RW, RW // 2))] + [_HBM] * ne,
        out_specs=[vec] * 6 + [_HBM] * ne,
        out_shape=[jax.ShapeDtypeStruct((T, RW), F32)] * 6 + [jax.ShapeDtypeStruct(e.shape, e.dtype) for e in exchange],
        scratch_shapes=[pltpu.VMEM((HEAD_DIM, RW), F32)] + (_ExchangePlan.sems(ne) if ne else []),
        compiler_params=_params(("arbitrary",)),
    )(r, w, k, v, a, b, dy, states, states, _heads_matrix(), *exchange)


_HBM = pl.BlockSpec(memory_space=pltpu.HBM)
_MESH = pl.DeviceIdType.MESH


def _place():
    x, y, c = lax.axis_index("x"), lax.axis_index("y"), lax.axis_index("c")
    return x, y, c, [(1 - x, y), (x, 1 - y), (1 - x, 1 - y)]


def _remote(src, dst, send_sem, recv_sem, to):
    return pltpu.make_async_remote_copy(src_ref=src, dst_ref=dst, send_sem=send_sem, recv_sem=recv_sem, device_id=to,
                                        device_id_type=_MESH)


def _dma_sems(*counts):
    return [pltpu.SemaphoreType.DMA((n,)) for n in counts]


class _GatherPlan:
    def __init__(self, ins, outs, sems):
        self.ins, self.outs, self.n = ins, outs, len(ins)
        self.ici_send, self.ici_recv, self.d2d_send, self.d2d_recv, self.local_sems = sems
        x, y, c, chips = _place()
        self.c, self.me, self.sibling = c, 2 * x + y, (x, y, 1 - c)
        self.peers = [(2 * qx + qy, (qx, qy, c)) for qx, qy in chips]

    @staticmethod
    def sems(n):
        return _dma_sems(3 * n, 3 * n, 3 * n, 3 * n, n)

    def _half(self, i, which):
        rh = self.ins[i].shape[0] // 2
        return pl.ds(which * rh, rh)

    def _local(self, i):
        return pltpu.make_async_copy(self.ins[i], self.outs[i].at[self.me], self.local_sems.at[i])

    def _send(self, i, j):
        k, mine = 3 * i + j, self._half(i, self.c)
        return _remote(self.ins[i].at[mine], self.outs[i].at[self.me, mine], self.ici_send.at[k], self.ici_recv.at[k],
                       self.peers[j][1])

    def _landed(self, i, j):
        k, piece = 3 * i + j, self.outs[i].at[self.peers[j][0], self._half(i, self.c)]
        return _remote(piece, piece, self.ici_send.at[k], self.ici_recv.at[k], self.peers[j][1])

    def _pass(self, i, j, which):
        k, piece = 3 * i + j, self.outs[i].at[self.peers[j][0], self._half(i, which)]
        return _remote(piece, piece, self.d2d_send.at[k], self.d2d_recv.at[k], self.sibling)

    def _all(self):
        return [(i, j) for i in range(self.n) for j in range(3)]

    def start(self):
        for i in range(self.n):
            self._local(i).start()
        for i, j in self._all():
            self._send(i, j).start()

    def relay(self):
        for i, j in self._all():
            self._landed(i, j).wait_recv()
            self._pass(i, j, self.c).start()

    def finish_relayed(self):
        for i, j in self._all():
            self._pass(i, j, 1 - self.c).wait_recv()
        for i, j in self._all():
            self._send(i, j).wait_send()
            self._pass(i, j, self.c).wait_send()
        for i in range(self.n):
            self._local(i).wait()

    def finish(self):
        self.relay()
        self.finish_relayed()

    @staticmethod
    def out_shapes(shards):
        return _gathered_shapes(shards)


def _gathered_shapes(shards):
    return [jax.ShapeDtypeStruct((N_CHIPS,) + s.shape, s.dtype) for s in shards]


def gather_weights(shards, *, name):
    n = len(shards)

    def body(*refs):
        plan = _GatherPlan(refs[:n], refs[n:2 * n], refs[2 * n:])
        plan.start()
        plan.finish()

    return _pcall(body, name=name, in_specs=[_HBM] * n, out_specs=[_HBM] * n, out_shape=_gathered_shapes(shards),
                  scratch_shapes=_GatherPlan.sems(n), compiler_params=_params())(*shards)


def to_sibling(arrays, take_other_half, *, name):
    n = len(arrays)

    def body(*refs):
        ins, outs = refs[:n], refs[n:2 * n]
        send_sems, recv_sems = refs[2 * n:]
        x, y, c, _ = _place()
        cps = []
        for i in range(n):
            src = ins[i]
            if take_other_half:
                rh = src.shape[1] // 2
                src = src.at[:, pl.ds((1 - c) * rh, rh)]
            cps.append(_remote(src, outs[i], send_sems.at[i], recv_sems.at[i], (x, y, 1 - c)))
        for cp in cps:
            cp.start()
        for cp in cps:
            cp.wait_recv()
        for cp in cps:
            cp.wait_send()

    def out_of(a):
        shape = (a.shape[0], a.shape[1] // 2, a.shape[2]) if take_other_half else a.shape
        return jax.ShapeDtypeStruct(shape, a.dtype)

    return _pcall(body, name=name, in_specs=[_HBM] * n, out_specs=[_HBM] * n, out_shape=[out_of(a) for a in arrays],
                  scratch_shapes=_dma_sems(n, n), compiler_params=_params())(*arrays)


def exchange_chips(arrays, *, name):
    n = len(arrays)

    def body(*refs):
        plan = _ExchangePlan(refs[:n], refs[n:2 * n], refs[2 * n:])
        plan.start()
        plan.finish()

    return _pcall(body, name=name, in_specs=[_HBM] * n, out_specs=[_HBM] * n,
                  out_shape=[jax.ShapeDtypeStruct(a.shape, a.dtype) for a in arrays],
                  scratch_shapes=_ExchangePlan.sems(n), compiler_params=_params())(*arrays)


class _ExchangePlan:
    def __init__(self, ins, outs, sems):
        self.ins, self.outs, self.n = ins, outs, len(ins)
        self.send_sems, self.recv_sems, self.local_sems = sems
        x, y, c, chips = _place()
        self.me = 2 * x + y
        self.peers = [(2 * qx + qy, (qx, qy, c)) for qx, qy in chips]

    @staticmethod
    def sems(n):
        return _dma_sems(3 * n, 3 * n, n)

    @staticmethod
    def out_shapes(arrays):
        return [jax.ShapeDtypeStruct(a.shape, a.dtype) for a in arrays]

    def _local(self, i):
        return pltpu.make_async_copy(self.ins[i].at[self.me], self.outs[i].at[self.me], self.local_sems.at[i])

    def _send(self, i, j):
        k = 3 * i + j
        return _remote(self.ins[i].at[self.peers[j][0]], self.outs[i].at[self.me], self.send_sems.at[k], self.recv_sems.at[k],
                       self.peers[j][1])

    def _landed(self, i, j):
        k, piece = 3 * i + j, self.outs[i].at[self.peers[j][0]]
        return _remote(piece, piece, self.send_sems.at[k], self.recv_sems.at[k], self.peers[j][1])

    def start(self):
        for i in range(self.n):
            self._local(i).start()
            for j in range(3):
                self._send(i, j).start()

    def finish(self):
        for i in range(self.n):
            for j in range(3):
                self._landed(i, j).wait_recv()
        for i in range(self.n):
            for j in range(3):
                self._send(i, j).wait_send()
            self._local(i).wait()


def _core_index():
    return lax.axis_index("c").astype(jnp.int32).reshape(1)


def pair_sum(g, theirs, wire_dtype, *, name):
    _, R, C = g.shape
    rh = R // 2
    tr = _tile(rh, 256, 16)
    nt = rh // tr

    def body(c_ref, g_ref, t_ref, q_ref, qw_ref):
        q = g_ref[...] + t_ref[...]
        q_ref[...] = q
        qw_ref[...] = q.astype(wire_dtype)

    blk = pl.BlockSpec((1, tr, C), lambda b, i, c_ref: (b, i, 0))
    mine = pl.BlockSpec((1, tr, C), lambda b, i, c_ref: (b, c_ref[0] * nt + i, 0))
    grid_spec = pltpu.PrefetchScalarGridSpec(num_scalar_prefetch=1, grid=(N_CHIPS, nt), in_specs=[mine, blk], out_specs=[blk, blk])
    return _pcall(body, name=name, grid_spec=grid_spec,
                  out_shape=[jax.ShapeDtypeStruct((N_CHIPS, rh, C), F32), jax.ShapeDtypeStruct((N_CHIPS, rh, C), wire_dtype)],
                  compiler_params=_params(("parallel", "parallel")))(_core_index(), g, theirs)


def half_sum(own, landed, *, name):
    _, rh, C = own.shape
    tr = _tile(rh, 256, 16)

    def body(me_ref, own_ref, land_ref, o_ref):
        total = None
        for p in range(N_CHIPS):
            term = jnp.where(me_ref[0] == p, own_ref[p], land_ref[p].astype(F32))
            total = term if total is None else total + term
        o_ref[...] = total

    blk = pl.BlockSpec((N_CHIPS, tr, C), lambda i, me_ref: (0, i, 0))
    grid_spec = pltpu.PrefetchScalarGridSpec(num_scalar_prefetch=1, grid=(rh // tr,), in_specs=[blk, blk],
                                             out_specs=pl.BlockSpec((tr, C), lambda i, me_ref: (i, 0)))
    me = (2 * lax.axis_index("x") + lax.axis_index("y")).astype(jnp.int32).reshape(1)
    return _pcall(body, name=name, grid_spec=grid_spec, out_shape=jax.ShapeDtypeStruct((rh, C), F32),
                  compiler_params=_params(("parallel",)))(me, own, landed)


def adamw(w, m, v, mine, theirs, *, name):
    _, R, C = w.shape
    rh = R // 2
    tr = _tile(rh, 256, 8)
    nt = rh // tr

    def body(c_ref, w_ref, m_ref, v_ref, a_ref, b_ref, g_ref, d_ref, nm_ref, nv_ref):
        is_mine = (pl.program_id(0) // nt) == c_ref[0]
        g = jnp.where(is_mine, a_ref[...], b_ref[...])
        g_ref[...] = g
        nm = ADAM_B1 * m_ref[...] + (1.0 - ADAM_B1) * g
        nv = ADAM_B2 * v_ref[...] + (1.0 - ADAM_B2) * (g * g)
        nm_ref[...] = nm
        nv_ref[...] = nv
        m_hat = nm / (1.0 - ADAM_B1 ** ADAM_STEP)
        v_hat = nv / (1.0 - ADAM_B2 ** ADAM_STEP)
        d_ref[...] = -ADAM_LR * (m_hat / (jnp.sqrt(v_hat) + ADAM_EPS) + ADAM_WD * w_ref[...])

    full = pl.BlockSpec((None, tr, C), lambda i, c_ref: (0, i, 0))
    a_spec = pl.BlockSpec((tr, C), lambda i, c_ref: (jnp.clip(i - c_ref[0] * nt, 0, nt - 1), 0))
    b_spec = pl.BlockSpec((tr, C), lambda i, c_ref: (jnp.clip(i - (1 - c_ref[0]) * nt, 0, nt - 1), 0))
    grid_spec = pltpu.PrefetchScalarGridSpec(num_scalar_prefetch=1, grid=(2 * nt,), in_specs=[full] * 3 + [a_spec, b_spec],
                                             out_specs=[full] * 4)
    return _pcall(body, name=name, grid_spec=grid_spec, out_shape=[jax.ShapeDtypeStruct((1, R, C), F32)] * 4,
                  compiler_params=_params(("arbitrary",)))(_core_index(), w, m, v, mine, theirs)


def _to_blocks(full, axis):
    r, c = full.shape
    if axis == 1:
        return full.reshape(r, N_CHIPS, c // N_CHIPS).transpose(1, 0, 2)
    return full.reshape(N_CHIPS, r // N_CHIPS, c)


def _from_blocks(blocks, axis):
    _, r, c = blocks.shape
    if axis == 1:
        return blocks.transpose(1, 0, 2).reshape(r, N_CHIPS * c)
    return blocks.reshape(N_CHIPS * r, c)


def _ffn_fwd(x, norm, wg_t, wu_t, wd, tag, carry=None):
    h = rms_fwd(x, norm, name=tag + "_norm")
    gate, up, act, *carried = mm_fused(h, [wg_t, wu_t], _swiglu, [BF16] * 3, tb=True, name=tag + "_gate_up", carry=carry)
    out = mm(act, wd, scale=0.5, res=x, name=tag + "_down")
    return out, (h, gate, up, act), carried


def _ffn_bwd(dout, x, saved, norm, wg_t, wu_t, wd, tag, carry=None):
    h, gate, up, act = saved
    dgate, dup, *carried = mm_fused(dout, [wd], _swiglu_bwd, [BF16] * 2, tb=True, extras=[gate, up], name=tag + "_dact",
                                    carry=carry)
    dwd = mm(act, dout, ta=True, scale=0.5, name=tag + "_dwd")
    dwg_t = mm(dgate, h, ta=True, name=tag + "_dwg")
    dwu_t = mm(dup, h, ta=True, name=tag + "_dwu")
    dh = mm(dgate, wg_t, name=tag + "_dh_gate")
    dh = mm(dup, wu_t, res=dh, name=tag + "_dh_up")
    dx, dnorm = rms_bwd(dh, x, norm, dout, name=tag + "_dnorm")
    return dx, dnorm, dwg_t, dwu_t, dwd, carried


TRANSPOSED = ('ffn1_w_gate', 'ffn1_w_up', 'ffn2_w_gate', 'ffn2_w_up')
FIRST_WEIGHTS = ['ffn1_w_gate', 'ffn1_w_up', 'ffn1_w_down']
MID_WEIGHTS = ['w_in', 'rwkv_w_lora_up', 'rwkv_a_lora_up', 'rwkv_g_lora_up']
LATE_WEIGHTS = ['w_branch_rwkv', 'w_branch_attn', 'w_out', 'ffn2_w_gate', 'ffn2_w_up', 'ffn2_w_down']


def _pair_sums(names, blocks, tag):
    from_sibling = to_sibling(blocks, True, name=tag + "_grads_to_sibling")
    return [pair_sum(g, t, F32 if n == 'small' else BF16, name="pair_sum_" + n)
            for g, t, n in zip(blocks, from_sibling, names)]


def _step(A):
    x, tgt = A['x'][0], A['loss_target'][0]
    T = x.shape[0]
    w = {n: A[n][0] for n in WEIGHT_NAMES}
    row = lambda a: a.reshape(1, -1)

    axis_of = {n: (0 if n in TRANSPOSED else axis) for n, axis in BIG}
    natural = lambda n, a: jnp.swapaxes(a, 1, 2) if n in TRANSPOSED else a
    shard = lambda n: natural(n, A[n])[0].astype(BF16)
    n1, nmix, n2, nfin = (row(w[n]) for n in ('ffn1_norm', 'mix_norm', 'ffn2_norm', 'final_norm'))
    gathered = gather_weights([shard(n) for n in FIRST_WEIGHTS], name="gather_weights")
    full = {n: _from_blocks(b, axis_of[n]) for n, b in zip(FIRST_WEIGHTS, gathered)}
    x1, ffn1, gathered = _ffn_fwd(x, n1, full['ffn1_w_gate'], full['ffn1_w_up'], full['ffn1_w_down'], "ffn1",
                                  carry=(_GatherPlan, [shard(n) for n in MID_WEIGHTS]))
    full.update({n: _from_blocks(b, axis_of[n]) for n, b in zip(MID_WEIGHTS, gathered)})
    w_in_r = _pad_rwkv_cols(full['w_in'][:, :RWKV_COLS])
    w_in_a = full['w_in'][:, RWKV_COLS:RWKV_COLS + ATT_COLS]
    w_in_g = full['w_in'][:, RWKV_COLS + ATT_COLS:]
    wlw, wla, wlg = (_pad_rows(full[n], 128).astype(F32) for n in ('rwkv_w_lora_up', 'rwkv_a_lora_up', 'rwkv_g_lora_up'))
    mu = _pad_rwkv_cols(row(w['rwkv_mu']))
    w0, a0, k_k, k_a, r_k, ln_w, ln_b = (row(w[n]) for n in ('rwkv_w0', 'rwkv_a0', 'rwkv_k_k', 'rwkv_k_a', 'rwkv_r_k',
                                                               'rwkv_ln_w', 'rwkv_ln_b'))
    qg = jnp.tile(row(w['attn_q_norm']), (1, N_HEADS))
    kg = jnp.tile(row(w['attn_k_norm']), (1, KVW // HEAD_DIM))
    sinks = jnp.pad(row(w['attn_sinks']), ((0, 0), (0, LANE - N_HEADS)))

    h2 = rms_fwd(x1, nmix, name="mix_norm")
    pr = mm(h2, w_in_r, name="proj_rwkv")
    pa = mm(h2, w_in_a, name="proj_att")
    pg = mm(h2, w_in_g, name="proj_gate")
    pr_shift = jnp.pad(pr, ((1, 0), (0, 0)))[:-1]
    r, dec, k2, v, a, b, sg = rwkv_pre_fwd(pr, pr_shift, mu, w0, a0, k_k, k_a, wlw, wla, wlg, name="rwkv_pre")
    y, states, *gathered = wkv_fwd(r, dec, k2, v, a, b, name="wkv_fwd", gather=[shard(n) for n in LATE_WEIGHTS])
    full.update({n: _from_blocks(b, axis_of[n]) for n, b in zip(LATE_WEIGHTS, gathered)})
    yr = rwkv_post_fwd(y, r, k2, v, sg, wlg, ln_w, ln_b, r_k, name="rwkv_post")
    ya = att_fwd(pa, qg, kg, sinks, name="att_fwd")
    br = mm(yr, full['w_branch_rwkv'], name="branch_rwkv")
    ba = mm(ya, full['w_branch_attn'], name="branch_att")
    mg = merge_fwd(br, ba, pg, name="merge")
    x2 = mm(mg, full['w_out'], res=x1, name="mix_out")
    x3, ffn2, _ = _ffn_fwd(x2, n2, full['ffn2_w_gate'], full['ffn2_w_up'], full['ffn2_w_down'], "ffn2")
    dx3, d_nfin, loss = final_loss(x3, tgt, nfin, name="final_loss")

    G = {'final_norm': d_nfin}
    dx2, G['ffn2_norm'], G['ffn2_w_gate'], G['ffn2_w_up'], G['ffn2_w_down'], _ = _ffn_bwd(
        dx3, x2, ffn2, n2, full['ffn2_w_gate'], full['ffn2_w_up'], full['ffn2_w_down'], "ffn2")
    dmg = mm(dx2, full['w_out'], tb=True, name="d_merge")
    G['w_out'] = mm(mg, dx2, ta=True, name="d_w_out")
    dbr, dba, dpg = merge_bwd(dmg, br, ba, pg, name="merge_bwd")
    dyr = mm(dbr, full['w_branch_rwkv'], tb=True, name="d_y_rwkv")
    G['w_branch_rwkv'] = mm(yr, dbr, ta=True, name="d_w_branch_rwkv")
    dya = mm(dba, full['w_branch_attn'], tb=True, name="d_y_att")
    G['w_branch_attn'] = mm(ya, dba, ta=True, name="d_w_branch_att")
    dy, dz, dg, G['rwkv_ln_w'], G['rwkv_ln_b'] = rwkv_post_bwd(dyr, y, r, k2, v, sg, wlg, ln_w, ln_b, r_k, name="rwkv_post_bwd")
    late_pair = _pair_sums(LATE_WEIGHTS, [_to_blocks(G[n], axis_of[n]) for n in LATE_WEIGHTS], "late")
    res = wkv_bwd(r, dec, k2, v, a, b, dy, states, name="wkv_bwd", exchange=[q for _, q in late_pair])
    wkv_grads, late_landed = res[:6], res[6:]
    (dpr, d_mu, G['rwkv_w0'], G['rwkv_a0'], G['rwkv_k_k'], G['rwkv_k_a'], G['rwkv_r_k'], d_wlw, d_wla, d_wlg) = rwkv_pre_bwd(
        pr, pr_shift, *wkv_grads, dz, dg, mu, w0, a0, k_k, k_a, r_k, wlw, wla, wlg, name="rwkv_pre_bwd")
    G['rwkv_mu'] = _unpad_rwkv_cols(d_mu)
    G['rwkv_w_lora_up'], G['rwkv_a_lora_up'], G['rwkv_g_lora_up'] = d_wlw[:DECAY_LORA], d_wla[:ICLR_LORA], d_wlg[:GATE_LORA]
    dq, dko, dkn, dvo, dvn, G['attn_q_norm'], d_sinks = att_bwd(pa, dya, qg, kg, sinks, name="att_bwd")
    G['attn_sinks'] = d_sinks[:, :N_HEADS]
    dpa, G['attn_k_norm'] = att_kv_bwd(pa, dq, dko, dkn, dvo, dvn, kg, name="att_kv_bwd")
    d_w_in_r = mm(h2, dpr, ta=True, name="d_w_in_rwkv")
    d_w_in_a = mm(h2, dpa, ta=True, name="d_w_in_att")
    d_w_in_g = mm(h2, dpg, ta=True, name="d_w_in_gate")
    G['w_in'] = jnp.concatenate([_unpad_rwkv_cols(d_w_in_r), d_w_in_a, d_w_in_g], axis=1)
    dh2 = mm(dpr, w_in_r, tb=True, name="d_h2_rwkv")
    dh2 = mm(dpa, w_in_a, tb=True, res=dh2, name="d_h2_att")
    dh2 = mm(dpg, w_in_g, tb=True, res=dh2, name="d_h2_gate")
    dx1, G['mix_norm'] = rms_bwd(dh2, x1, nmix, dx2, name="d_mix_norm")
    mid_pair = _pair_sums(MID_WEIGHTS, [_to_blocks(G[n], axis_of[n]) for n in MID_WEIGHTS], "mid")
    dx0, G['ffn1_norm'], G['ffn1_w_gate'], G['ffn1_w_up'], G['ffn1_w_down'], mid_landed = _ffn_bwd(
        dx1, x, ffn1, n1, full['ffn1_w_gate'], full['ffn1_w_up'], full['ffn1_w_down'], "ffn1",
        carry=(_ExchangePlan, [q for _, q in mid_pair]))

    small_shapes = [(w[n].size,) for n in SMALL] + [(1,)]

    def small_rows(parts):
        vec = jnp.concatenate([p.reshape(-1) for p in parts])
        return jnp.pad(vec, (0, SMALL_ROWS * FLAT_W - vec.shape[0])).reshape(SMALL_ROWS, FLAT_W)

    small = small_rows([G[n] for n in SMALL] + [loss[0, :1]])
    first_names = FIRST_WEIGHTS + ['small']
    first_blocks = [_to_blocks(G[n], axis_of[n]) for n in FIRST_WEIGHTS] + [jnp.broadcast_to(small[None], (N_CHIPS,) + small.shape)]
    first_pair = _pair_sums(first_names, first_blocks, "first")
    first_landed = exchange_chips([q for _, q in first_pair], name="exchange_grads")
    names = first_names + MID_WEIGHTS + LATE_WEIGHTS
    pair, landed = first_pair + mid_pair + late_pair, list(first_landed) + list(mid_landed) + list(late_landed)
    halves = [half_sum(own, l, name="half_sum_" + n) for (own, _), l, n in zip(pair, landed, names)]
    other_halves = to_sibling(halves, False, name="halves_to_sibling")

    def local(prefix, n):
        if n != 'small':
            return natural(n, A[prefix + n])
        return small_rows([A[prefix + s] for s in SMALL] + [jnp.zeros((1,), F32)])[None]

    result = {}
    for n, mine, theirs in zip(names, halves, other_halves):
        outs4 = adamw(local('', n), local('m_', n), local('v_', n), mine, theirs, name="adamw_" + n)
        for kind, o in zip(('grad_', 'delta_', 'new_m_', 'new_v_'), outs4):
            if n != 'small':
                result[kind + n] = natural(n, o)
            else:
                for s, part in zip(SMALL + ['loss'], _unpack_vec(o.reshape(-1), small_shapes)):
                    result[kind + s] = part.reshape(A[s].shape) if s != 'loss' else part.reshape(())
    outs = [result['grad_loss'], dx0[None]]
    for kind in ('grad_', 'delta_', 'new_m_', 'new_v_'):
        outs += [result[kind + n] for n in WEIGHT_NAMES]
    return tuple(outs)


def _unpack_vec(vec, shapes):
    out, off = [], 0
    for (n,) in shapes:
        out.append(vec[off:off + n])
        off += n
    return out


def kernel(x, ffn1_norm, ffn1_w_gate, ffn1_w_up, ffn1_w_down, mix_norm, w_in, rwkv_mu, rwkv_w0, rwkv_w_lora_up, rwkv_a0, rwkv_a_lora_up, rwkv_g_lora_up, rwkv_k_k, rwkv_k_a, rwkv_r_k, rwkv_ln_w, rwkv_ln_b, attn_q_norm, attn_k_norm, attn_sinks, w_branch_rwkv, w_branch_attn, w_out, ffn2_norm, ffn2_w_gate, ffn2_w_up, ffn2_w_down, final_norm, loss_target, m_ffn1_norm, m_ffn1_w_gate, m_ffn1_w_up, m_ffn1_w_down, m_mix_norm, m_w_in, m_rwkv_mu, m_rwkv_w0, m_rwkv_w_lora_up, m_rwkv_a0, m_rwkv_a_lora_up, m_rwkv_g_lora_up, m_rwkv_k_k, m_rwkv_k_a, m_rwkv_r_k, m_rwkv_ln_w, m_rwkv_ln_b, m_attn_q_norm, m_attn_k_norm, m_attn_sinks, m_w_branch_rwkv, m_w_branch_attn, m_w_out, m_ffn2_norm, m_ffn2_w_gate, m_ffn2_w_up, m_ffn2_w_down, m_final_norm, v_ffn1_norm, v_ffn1_w_gate, v_ffn1_w_up, v_ffn1_w_down, v_mix_norm, v_w_in, v_rwkv_mu, v_rwkv_w0, v_rwkv_w_lora_up, v_rwkv_a0, v_rwkv_a_lora_up, v_rwkv_g_lora_up, v_rwkv_k_k, v_rwkv_k_a, v_rwkv_r_k, v_rwkv_ln_w, v_rwkv_ln_b, v_attn_q_norm, v_attn_k_norm, v_attn_sinks, v_w_branch_rwkv, v_w_branch_attn, v_w_out, v_ffn2_norm, v_ffn2_w_gate, v_ffn2_w_up, v_ffn2_w_down, v_final_norm):
    return _step(dict(locals()))
```

```python
import functools

import jax
import jax.numpy as jnp
from jax import lax
from jax.experimental import pallas as pl
from jax.experimental.pallas import tpu as pltpu

F32 = jnp.float32
BF16 = jnp.bfloat16

D_MODEL = 1024
D_FF = 2816
HEAD_DIM = 64
N_HEADS = 8
RW = 512
KVW = 128
ATT_GROUP = 4
WINDOW = 128
BLOCK = 128
DECAY_LORA, ICLR_LORA, GATE_LORA = 32, 32, 96
RWKV_COLS = 3 * RW + DECAY_LORA + ICLR_LORA + GATE_LORA
ATT_COLS = RW + 2 * KVW
GATE_COLS = 2 * D_MODEL
RWKV_PAD = 3 * RW + 3 * 128
RMS_EPS = 1e-6
GN_EPS = 64e-5
N_CHIPS = 4
LANE = 128
FLAT_W = 1024
SMALL_ROWS = 32
NEG_BIG = -1e30

ADAM_LR, ADAM_B1, ADAM_B2, ADAM_EPS, ADAM_WD, ADAM_STEP = 0.001, 0.9, 0.999, 1e-08, 0.01, 10

VMEM_LIMIT = 56 * 1024 * 1024

WEIGHT_NAMES = ['ffn1_norm', 'ffn1_w_gate', 'ffn1_w_up', 'ffn1_w_down', 'mix_norm', 'w_in', 'rwkv_mu', 'rwkv_w0',
                'rwkv_w_lora_up', 'rwkv_a0', 'rwkv_a_lora_up', 'rwkv_g_lora_up', 'rwkv_k_k', 'rwkv_k_a', 'rwkv_r_k',
                'rwkv_ln_w', 'rwkv_ln_b', 'attn_q_norm', 'attn_k_norm', 'attn_sinks', 'w_branch_rwkv',
                'w_branch_attn', 'w_out', 'ffn2_norm', 'ffn2_w_gate', 'ffn2_w_up', 'ffn2_w_down', 'final_norm']
BIG = [('ffn1_w_gate', 1), ('ffn1_w_up', 1), ('ffn1_w_down', 0), ('w_in', 1), ('rwkv_w_lora_up', 1),
       ('rwkv_a_lora_up', 1), ('rwkv_g_lora_up', 1), ('w_branch_rwkv', 1), ('w_branch_attn', 1), ('w_out', 0),
       ('ffn2_w_gate', 1), ('ffn2_w_up', 1), ('ffn2_w_down', 0)]
SMALL = ['ffn1_norm', 'mix_norm', 'rwkv_mu', 'rwkv_w0', 'rwkv_a0', 'rwkv_k_k', 'rwkv_k_a', 'rwkv_r_k', 'rwkv_ln_w',
         'rwkv_ln_b', 'attn_q_norm', 'attn_k_norm', 'attn_sinks', 'ffn2_norm', 'final_norm']


def _pcall(body, **kw):
    return pl.pallas_call(body, **kw)


def _params(sem=None, **kw):
    if sem is not None:
        kw['dimension_semantics'] = sem
    return pltpu.CompilerParams(vmem_limit_bytes=VMEM_LIMIT, **kw)


def _tile(n, cap, mult):
    best = None
    for t in range(mult, min(n, cap) + 1, mult):
        if n % t == 0:
            best = t
    return best or n


def _sigmoid(z):
    return 1.0 / (1.0 + jnp.exp(-z))


def _softplus(z):
    return jnp.maximum(z, 0.0) + jnp.log(1.0 + jnp.exp(-jnp.abs(z)))


def _bdot(a, b, dims=(((1,), (0,)), ((), ()))):
    return lax.dot_general(a.astype(BF16), b.astype(BF16), dims, preferred_element_type=F32)


_NT = (((1,), (1,)), ((), ()))
_TN = (((0,), (0,)), ((), ()))


def _segsum(x, bd):
    hi = x.astype(BF16)
    r1 = x - hi.astype(F32)
    mid = r1.astype(BF16)
    lo = (r1 - mid.astype(F32)).astype(BF16)
    dot = functools.partial(lax.dot_general, dimension_numbers=(((1,), (0,)), ((), ())), preferred_element_type=F32)
    return dot(hi, bd) + dot(mid, bd) + dot(lo, bd)


_LORA_EDGES = (3 * RW, 3 * RW + DECAY_LORA, 3 * RW + DECAY_LORA + ICLR_LORA, RWKV_COLS)


def _pad_rwkv_cols(x):
    parts = [x[..., :3 * RW]]
    for lo, hi in zip(_LORA_EDGES[:-1], _LORA_EDGES[1:]):
        parts.append(jnp.pad(x[..., lo:hi], [(0, 0)] * (x.ndim - 1) + [(0, 128 - (hi - lo))]))
    return jnp.concatenate(parts, axis=-1)


def _unpad_rwkv_cols(x):
    parts = [x[..., :3 * RW]]
    for j, (lo, hi) in enumerate(zip(_LORA_EDGES[:-1], _LORA_EDGES[1:])):
        parts.append(x[..., 3 * RW + 128 * j:3 * RW + 128 * j + (hi - lo)])
    return jnp.concatenate(parts, axis=-1)


def _pad_rows(x, rows):
    return jnp.pad(x, [(0, rows - x.shape[0])] + [(0, 0)] * (x.ndim - 1))


def mm(a, b, *, name, ta=False, tb=False, scale=None, res=None, out_dtype=F32, carry=None):
    M, K = (a.shape[1], a.shape[0]) if ta else a.shape
    N = b.shape[0] if tb else b.shape[1]
    assert (b.shape[1] if tb else b.shape[0]) == K
    tm, tn, tk = _tile(M, 1408 if ta else 512, 128), _tile(N, 1408, 128), _tile(K, 1408, 128)
    nk = K // tk
    grid = (M // tm, N // tn, nk)
    dims = (((0 if ta else 1,), (1 if tb else 0,)), ((), ()))
    plan_cls, carried = carry if carry else (None, ())
    nc, nin = len(carried), 2 + (res is not None)

    def body(*refs):
        a_ref, b_ref = refs[:2]
        r_ref = refs[2] if res is not None else None
        o_ref, acc_ref = refs[nin + nc], refs[nin + 2 * nc + 1]
        k = pl.program_id(2)
        if nc:
            plan = plan_cls(refs[nin:nin + nc], refs[nin + nc + 1:nin + 2 * nc + 1], refs[nin + 2 * nc + 2:])
            at = lambda which: functools.reduce(jnp.logical_and, [pl.program_id(d) == (0 if which == 0 else grid[d] - 1)
                                                                 for d in range(3)])
            pl.when(at(0))(plan.start)
        part = _bdot(a_ref[...], b_ref[...], dims)

        @pl.when(k == 0)
        def _():
            acc_ref[...] = part

        @pl.when(k > 0)
        def _():
            acc_ref[...] += part

        @pl.when(k == nk - 1)
        def _():
            o = acc_ref[...]
            if scale is not None:
                o = o * scale
            if r_ref is not None:
                o = o + r_ref[...].astype(F32)
            o_ref[...] = o.astype(out_dtype)

        if nc:
            pl.when(at(1))(plan.finish)

    a_spec = pl.BlockSpec((tk, tm), lambda i, j, k: (k, i)) if ta else pl.BlockSpec((tm, tk), lambda i, j, k: (i, k))
    b_spec = pl.BlockSpec((tn, tk), lambda i, j, k: (j, k)) if tb else pl.BlockSpec((tk, tn), lambda i, j, k: (k, j))
    o_spec = pl.BlockSpec((tm, tn), lambda i, j, k: (i, j))
    in_specs = [a_spec, b_spec] + ([o_spec] if res is not None else [])
    args = (a, b) + ((res,) if res is not None else ())
    out_shape = jax.ShapeDtypeStruct((M, N), out_dtype)
    if not nc:
        return _pcall(
            body, name=name, grid=grid, in_specs=in_specs, out_specs=o_spec, out_shape=out_shape,
            scratch_shapes=[pltpu.VMEM((tm, tn), F32)], compiler_params=_params(("parallel", "parallel", "arbitrary")),
        )(*args)
    return _pcall(
        body, name=name, grid=grid, in_specs=in_specs + [_HBM] * nc, out_specs=[o_spec] + [_HBM] * nc,
        out_shape=[out_shape] + plan_cls.out_shapes(carried), scratch_shapes=[pltpu.VMEM((tm, tn), F32)] + plan_cls.sems(nc),
        compiler_params=_params(("arbitrary", "arbitrary", "arbitrary")),
    )(*args, *carried)


def mm_fused(a, bs, finish, out_dtypes, *, name, tb=False, extras=(), carry=None):
    M, K = a.shape
    N = bs[0].shape[0] if tb else bs[0].shape[1]
    tm, tn = _tile(M, 512, 128), _tile(N, 1408, 128)
    grid = (M // tm, N // tn)
    dims = (((1,), (1 if tb else 0,)), ((), ()))
    plan_cls, carried = carry if carry else (None, ())
    nc, nb, nx, no = len(carried), len(bs), len(extras), len(out_dtypes)
    nin = 1 + nb + nx

    def body(*refs):
        a_ref, b_refs, x_refs = refs[0], refs[1:1 + nb], refs[1 + nb:nin]
        o_refs = refs[nin + nc:nin + nc + no]
        if nc:
            plan = plan_cls(refs[nin:nin + nc], refs[nin + nc + no:nin + 2 * nc + no], refs[nin + 2 * nc + no:])
            at = lambda which: jnp.logical_and(*[pl.program_id(d) == (0 if which == 0 else grid[d] - 1) for d in range(2)])
            pl.when(at(0))(plan.start)
        av = a_ref[...]
        outs = finish([_bdot(av, b_ref[...], dims) for b_ref in b_refs], [x_ref[...] for x_ref in x_refs])
        for o_ref, o in zip(o_refs, outs):
            o_ref[...] = o.astype(o_ref.dtype)
        if nc:
            pl.when(at(1))(plan.finish)

    a_spec = pl.BlockSpec((tm, K), lambda i, j: (i, 0))
    b_spec = pl.BlockSpec((tn, K), lambda i, j: (j, 0)) if tb else pl.BlockSpec((K, tn), lambda i, j: (0, j))
    o_spec = pl.BlockSpec((tm, tn), lambda i, j: (i, j))
    out_shape = [jax.ShapeDtypeStruct((M, N), d) for d in out_dtypes]
    if not nc:
        return _pcall(body, name=name, grid=grid, in_specs=[a_spec] + [b_spec] * nb + [o_spec] * nx, out_specs=[o_spec] * no,
                      out_shape=out_shape, compiler_params=_params(("parallel", "parallel")))(a, *bs, *extras)
    return _pcall(body, name=name, grid=grid, in_specs=[a_spec] + [b_spec] * nb + [o_spec] * nx + [_HBM] * nc,
                  out_specs=[o_spec] * no + [_HBM] * nc, out_shape=out_shape + plan_cls.out_shapes(carried),
                  scratch_shapes=plan_cls.sems(nc), compiler_params=_params(("arbitrary", "arbitrary")))(a, *bs, *extras, *carried)


def _swiglu(products, _):
    g, u = products
    return g, u, g * _sigmoid(g) * u


def _swiglu_bwd(products, extras):
    (da,), (gate, up) = products, extras
    gv = gate.astype(F32)
    s = _sigmoid(gv)
    return da * 0.5 * up.astype(F32) * s * (1.0 + gv * (1.0 - s)), da * 0.5 * gv * s


def _row_spec(tr, c):
    return pl.BlockSpec((tr, c), lambda i: (i, 0))


def _full_spec(shape):
    return pl.BlockSpec(shape, lambda i: (0,) * len(shape))


def _acc_rows(ref, val, i):
    @pl.when(i == 0)
    def _():
        ref[...] = val

    @pl.when(i > 0)
    def _():
        ref[...] += val


def rms_fwd(x, g, *, name):
    T, D = x.shape
    tr = _tile(T, 512, 8)

    def body(x_ref, g_ref, h_ref):
        xv = x_ref[...]
        r = lax.rsqrt(jnp.mean(xv * xv, axis=-1, keepdims=True) + RMS_EPS)
        h_ref[...] = (xv * r * g_ref[...]).astype(BF16)

    return _pcall(body, name=name, grid=(T // tr,), in_specs=[_row_spec(tr, D), _full_spec((1, D))],
                  out_specs=_row_spec(tr, D), out_shape=jax.ShapeDtypeStruct((T, D), BF16),
                  compiler_params=_params(("parallel",)))(x, g)


def rms_bwd(dh, x, g, res, *, name):
    T, D = x.shape
    tr = _tile(T, 256, 8)

    def body(dh_ref, x_ref, g_ref, res_ref, dx_ref, dg_ref):
        i = pl.program_id(0)
        xv, dhv = x_ref[...], dh_ref[...].astype(F32)
        r = lax.rsqrt(jnp.mean(xv * xv, axis=-1, keepdims=True) + RMS_EPS)
        xh = xv * r
        dxh = dhv * g_ref[...]
        dx_ref[...] = res_ref[...] + r * (dxh - xh * jnp.mean(dxh * xh, axis=-1, keepdims=True))
        _acc_rows(dg_ref, jnp.sum(dhv * xh, axis=0, keepdims=True), i)

    return _pcall(body, name=name, grid=(T // tr,),
                  in_specs=[_row_spec(tr, D), _row_spec(tr, D), _full_spec((1, D)), _row_spec(tr, D)],
                  out_specs=[_row_spec(tr, D), _full_spec((1, D))],
                  out_shape=[jax.ShapeDtypeStruct((T, D), F32), jax.ShapeDtypeStruct((1, D), F32)],
                  compiler_params=_params(("arbitrary",)))(dh, x, g, res)


def final_loss(x, tgt, g, *, name):
    T, D = x.shape
    tr = _tile(T, 256, 8)

    def body(x_ref, t_ref, g_ref, dx_ref, dg_ref, loss_ref):
        i = pl.program_id(0)
        xv = x_ref[...]
        r = lax.rsqrt(jnp.mean(xv * xv, axis=-1, keepdims=True) + RMS_EPS)
        xh = xv * r
        e = xh * g_ref[...] - t_ref[...]
        part = 0.5 * jnp.sum(jnp.mean(e * e, axis=-1, keepdims=True), axis=0, keepdims=True)
        dy = e * (1.0 / D)
        dxh = dy * g_ref[...]
        dx_ref[...] = r * (dxh - xh * jnp.mean(dxh * xh, axis=-1, keepdims=True))
        _acc_rows(dg_ref, jnp.sum(dy * xh, axis=0, keepdims=True), i)
        _acc_rows(loss_ref, jnp.broadcast_to(part, (1, LANE)), i)

    return _pcall(body, name=name, grid=(T // tr,),
                  in_specs=[_row_spec(tr, D), _row_spec(tr, D), _full_spec((1, D))],
                  out_specs=[_row_spec(tr, D), _full_spec((1, D)), _full_spec((1, LANE))],
                  out_shape=[jax.ShapeDtypeStruct((T, D), F32), jax.ShapeDtypeStruct((1, D), F32),
                             jax.ShapeDtypeStruct((1, LANE), F32)],
                  compiler_params=_params(("arbitrary",)))(x, tgt, g)


def merge_fwd(br, ba, pg, *, name):
    T, D = br.shape
    tr = _tile(T, 256, 8)

    def body(br_ref, ba_ref, pg_ref, o_ref):
        pgv = pg_ref[...]
        o_ref[...] = (_sigmoid(pgv[:, :D]) * br_ref[...] + _sigmoid(pgv[:, D:]) * ba_ref[...]).astype(BF16)

    return _pcall(body, name=name, grid=(T // tr,), in_specs=[_row_spec(tr, D), _row_spec(tr, D), _row_spec(tr, 2 * D)],
                  out_specs=_row_spec(tr, D), out_shape=jax.ShapeDtypeStruct((T, D), BF16),
                  compiler_params=_params(("parallel",)))(br, ba, pg)


def merge_bwd(dm, br, ba, pg, *, name):
    T, D = br.shape
    tr = _tile(T, 256, 8)

    def body(dm_ref, br_ref, ba_ref, pg_ref, dbr_ref, dba_ref, dpg_ref):
        pgv, dmv = pg_ref[...], dm_ref[...]
        sr, sa = _sigmoid(pgv[:, :D]), _sigmoid(pgv[:, D:])
        dbr_ref[...] = (dmv * sr).astype(BF16)
        dba_ref[...] = (dmv * sa).astype(BF16)
        dpg_ref[:, :D] = dmv * br_ref[...] * sr * (1.0 - sr)
        dpg_ref[:, D:] = dmv * ba_ref[...] * sa * (1.0 - sa)

    return _pcall(body, name=name, grid=(T // tr,),
                  in_specs=[_row_spec(tr, D), _row_spec(tr, D), _row_spec(tr, D), _row_spec(tr, 2 * D)],
                  out_specs=[_row_spec(tr, D), _row_spec(tr, D), _row_spec(tr, 2 * D)],
                  out_shape=[jax.ShapeDtypeStruct((T, D), BF16), jax.ShapeDtypeStruct((T, D), BF16),
                             jax.ShapeDtypeStruct((T, 2 * D), F32)],
                  compiler_params=_params(("parallel",)))(dm, br, ba, pg)


def _rwkv_mix(p, prev, mu, w0, a0, k_k, k_a, wlw, wla, wlg, bd):
    pp = p + (prev - p) * mu
    r, k, v = pp[:, 0:RW], pp[:, RW:2 * RW], pp[:, 2 * RW:3 * RW]
    xw, xa, xg = pp[:, 3 * RW:3 * RW + 128], pp[:, 3 * RW + 128:3 * RW + 256], pp[:, 3 * RW + 256:3 * RW + 384]
    th = jnp.tanh(xw)
    z = -(w0 + _bdot(th, wlw))
    e = jnp.exp(-_softplus(z) - 0.5)
    decay = jnp.exp(-e)
    a = _sigmoid(a0 + _bdot(xa, wla))
    sg = _sigmoid(xg)
    kkr = k * k_k
    n = jnp.sqrt(_segsum(kkr * kkr, bd))
    kk = kkr / jnp.maximum(n, 1e-12)
    k2 = k * (1.0 + (a - 1.0) * k_a)
    return dict(r=r, k=k, v=v, xa=xa, th=th, z=z, e=e, decay=decay, a=a, sg=sg, n=n, kk=kk, k2=k2)


def _seg_matrix(n, shift):
    r = lax.shift_right_logical(lax.broadcasted_iota(jnp.int32, (n, n), 0), shift)
    c = lax.shift_right_logical(lax.broadcasted_iota(jnp.int32, (n, n), 1), shift)
    return jnp.where(r == c, 1.0, 0.0).astype(BF16)


def rwkv_pre_fwd(p, pshift, mu, w0, a0, k_k, k_a, wlw, wla, wlg, *, name):
    T = p.shape[0]
    tr = _tile(T, 256, 8)

    def body(p_ref, ps_ref, mu_ref, w0_ref, a0_ref, kk_ref, ka_ref, wlw_ref, wla_ref, wlg_ref,
             r_ref, w_ref, k_ref, v_ref, a_ref, b_ref, g_ref):
        pv, prev = p_ref[...], ps_ref[...]
        m = _rwkv_mix(pv, prev, mu_ref[...], w0_ref[...], a0_ref[...], kk_ref[...], ka_ref[...],
                      wlw_ref[...], wla_ref[...], wlg_ref[...], _seg_matrix(RW, 6))
        r_ref[...] = m['r']
        w_ref[...] = m['decay']
        k_ref[...] = m['k2']
        v_ref[...] = m['v']
        a_ref[...] = -m['kk']
        b_ref[...] = m['kk'] * m['a']
        g_ref[...] = m['sg']

    vec = _row_spec(tr, RW)
    return _pcall(
        body, name=name, grid=(T // tr,),
        in_specs=[_row_spec(tr, RWKV_PAD), _row_spec(tr, RWKV_PAD), _full_spec((1, RWKV_PAD))] + [_full_spec((1, RW))] * 4
        + [_full_spec((128, RW))] * 3,
        out_specs=[vec] * 6 + [_row_spec(tr, 128)],
        out_shape=[jax.ShapeDtypeStruct((T, RW), F32)] * 6 + [jax.ShapeDtypeStruct((T, 128), F32)],
        compiler_params=_params(("parallel",)),
    )(p, pshift, mu, w0, a0, k_k, k_a, wlw, wla, wlg)


def _group_norm(y, bd):
    mean = _segsum(y, bd) * (1.0 / HEAD_DIM)
    yc = y - mean
    rstd = lax.rsqrt(_segsum(yc * yc, bd) * (1.0 / HEAD_DIM) + GN_EPS)
    return yc * rstd, rstd


def rwkv_post_fwd(y, r, k2, v, sg, wlg, ln_w, ln_b, r_k, *, name):
    T = y.shape[0]
    tr = _tile(T, 256, 8)

    def body(y_ref, r_ref, k_ref, v_ref, sg_ref, wlg_ref, lw_ref, lb_ref, rk_ref, o_ref):
        bd = _seg_matrix(RW, 6)
        yn, _ = _group_norm(y_ref[...], bd)
        s = _segsum(r_ref[...] * k_ref[...] * rk_ref[...], bd)
        g = _bdot(sg_ref[...], wlg_ref[...])
        o_ref[...] = ((yn * lw_ref[...] + lb_ref[...] + s * v_ref[...]) * g).astype(BF16)

    vec = _row_spec(tr, RW)
    return _pcall(body, name=name, grid=(T // tr,),
                  in_specs=[vec] * 4 + [_row_spec(tr, 128), _full_spec((128, RW))] + [_full_spec((1, RW))] * 3, out_specs=vec,
                  out_shape=jax.ShapeDtypeStruct((T, RW), BF16), compiler_params=_params(("parallel",)))(
                      y, r, k2, v, sg, wlg, ln_w, ln_b, r_k)


def rwkv_post_bwd(dyr, y, r, k2, v, sg, wlg, ln_w, ln_b, r_k, *, name, carry=None):
    T = y.shape[0]
    tr = _tile(T, 256, 8)
    nt = T // tr
    plan_cls, carried = carry if carry else (None, ())
    nc = len(carried)

    def body(*refs):
        dyr_ref, y_ref, r_ref, k_ref, v_ref, sg_ref, wlg_ref, lw_ref, lb_ref, rk_ref = refs[:10]
        dy_ref, dz_ref, dg_ref, dlw_ref, dlb_ref = refs[10 + nc:15 + nc]
        i = pl.program_id(0)
        if nc:
            plan = plan_cls(refs[10:10 + nc], refs[15 + nc:15 + 2 * nc], refs[15 + 2 * nc:])
            pl.when(i == 0)(plan.start)
        bd = _seg_matrix(RW, 6)
        yn, rstd = _group_norm(y_ref[...], bd)
        s = _segsum(r_ref[...] * k_ref[...] * rk_ref[...], bd)
        dyrv = dyr_ref[...]
        dg_ref[...] = dyrv * (yn * lw_ref[...] + lb_ref[...] + s * v_ref[...])
        dz = dyrv * _bdot(sg_ref[...], wlg_ref[...])
        dz_ref[...] = dz
        dyn = dz * lw_ref[...]
        inv = 1.0 / HEAD_DIM
        dy_ref[...] = rstd * (dyn - _segsum(dyn, bd) * inv - yn * (_segsum(dyn * yn, bd) * inv))
        _acc_rows(dlw_ref, jnp.sum(dz * yn, axis=0, keepdims=True), i)
        _acc_rows(dlb_ref, jnp.sum(dz, axis=0, keepdims=True), i)
        if nc:
            pl.when(i == nt - 1)(plan.finish)

    vec = _row_spec(tr, RW)
    one = _full_spec((1, RW))
    return _pcall(body, name=name, grid=(nt,),
                  in_specs=[vec] * 5 + [_row_spec(tr, 128), _full_spec((128, RW))] + [one] * 3 + [_HBM] * nc,
                  out_specs=[vec] * 3 + [one] * 2 + [_HBM] * nc,
                  out_shape=[jax.ShapeDtypeStruct((T, RW), F32)] * 3 + [jax.ShapeDtypeStruct((1, RW), F32)] * 2
                  + (plan_cls.out_shapes(carried) if nc else []),
                  scratch_shapes=plan_cls.sems(nc) if nc else [],
                  compiler_params=_params(("arbitrary",)))(dyr, y, r, k2, v, sg, wlg, ln_w, ln_b, r_k, *carried)


def rwkv_pre_bwd(p, pshift, dr_w, dw_w, dk_w, dv_w, da_w, db_w, dz, dg, mu, w0, a0, k_k, k_a, r_k, wlw, wla, wlg, *, name):
    T = p.shape[0]
    tr = _tile(T, 256, 8)
    n = T // tr

    def body(p_ref, ps_ref, dr_ref, dw_ref, dk_ref, dv_ref, da_ref, db_ref, dz_ref, dg_ref,
             mu_ref, w0_ref, a0_ref, kk_ref, ka_ref, rk_ref, wlw_ref, wla_ref, wlg_ref,
             dp_ref, dmu_ref, dw0_ref, da0_ref, dkk_ref, dka_ref, drk_ref, dwlw_ref, dwla_ref, dwlg_ref,
             carry, dpp, acc_w, acc_a, acc_g):
        i = pl.program_id(0)

        @pl.when(i == 0)
        def _():
            carry[...] = jnp.zeros_like(carry)

        pv, prev, mu = p_ref[...], ps_ref[...], mu_ref[...]
        bd = _seg_matrix(RW, 6)
        k_k, k_a, r_k = kk_ref[...], ka_ref[...], rk_ref[...]
        m = _rwkv_mix(pv, prev, mu, w0_ref[...], a0_ref[...], k_k, k_a, wlw_ref[...], wla_ref[...], wlg_ref[...], bd)
        r, k, v, a, kk, k2 = m['r'], m['k'], m['v'], m['a'], m['kk'], m['k2']
        dzv, dgv = dz_ref[...], dg_ref[...]
        s = _segsum(r * k2 * r_k, bd)
        ds = _segsum(dzv * v, bd)
        dr = dr_ref[...] + ds * k2 * r_k
        dk2 = dk_ref[...] + ds * r * r_k
        dv = dv_ref[...] + dzv * s
        dbv = db_ref[...]
        dkk = dbv * a - da_ref[...]
        da = dbv * kk + dk2 * k * k_a
        dk = dk2 * (1.0 + (a - 1.0) * k_a)
        nmax = jnp.maximum(m['n'], 1e-12)
        dkkr = jnp.where(m['n'] > 1e-12, dkk - kk * _segsum(dkk * kk, bd), dkk) / nmax
        dk = dk + dkkr * k_k
        dapre = da * a * (1.0 - a)
        dwpre = dw_ref[...] * m['decay'] * (-m['e']) * _sigmoid(m['z'])
        dth = _bdot(dwpre, wlw_ref[...], _NT)
        dxa = _bdot(dapre, wla_ref[...], _NT)
        dsg = _bdot(dgv, wlg_ref[...], _NT)
        dpp[:, 0:RW] = dr
        dpp[:, RW:2 * RW] = dk
        dpp[:, 2 * RW:3 * RW] = dv
        dpp[:, 3 * RW:3 * RW + 128] = dth * (1.0 - m['th'] * m['th'])
        dpp[:, 3 * RW + 128:3 * RW + 256] = dxa
        dpp[:, 3 * RW + 256:3 * RW + 384] = dsg * m['sg'] * (1.0 - m['sg'])
        d = dpp[...]
        zed = d * mu
        last = lax.broadcasted_iota(jnp.int32, pv.shape, 0) == tr - 1
        dp_ref[...] = d * (1.0 - mu) + jnp.where(last, carry[0:1, :], pltpu.roll(zed, tr - 1, 0))
        carry[...] = zed[0:8, :]

        def colsum(x):
            return jnp.sum(x, axis=0, keepdims=True)

        _acc_rows(dmu_ref, colsum(d * (prev - pv)), i)
        _acc_rows(dw0_ref, colsum(dwpre), i)
        _acc_rows(da0_ref, colsum(dapre), i)
        _acc_rows(dkk_ref, colsum(dkkr * k), i)
        _acc_rows(dka_ref, colsum(dk2 * k * (a - 1.0)), i)
        _acc_rows(drk_ref, colsum(ds * r * k2), i)
        _acc_rows(acc_w, _bdot(m['th'], dwpre, _TN), i)
        _acc_rows(acc_a, _bdot(m['xa'], dapre, _TN), i)
        _acc_rows(acc_g, _bdot(m['sg'], dgv, _TN), i)

        @pl.when(i == n - 1)
        def _():
            dwlw_ref[...] = acc_w[...]
            dwla_ref[...] = acc_a[...]
            dwlg_ref[...] = acc_g[...]

    rev = lambda c: pl.BlockSpec((tr, c), lambda i: (n - 1 - i, 0))
    one, lora = _full_spec((1, RW)), _full_spec((128, RW))
    return _pcall(
        body, name=name, grid=(n,),
        in_specs=[rev(RWKV_PAD), rev(RWKV_PAD)] + [rev(RW)] * 8 + [_full_spec((1, RWKV_PAD))] + [one] * 5 + [lora] * 3,
        out_specs=[rev(RWKV_PAD), _full_spec((1, RWKV_PAD))] + [one] * 5 + [lora] * 3,
        out_shape=[jax.ShapeDtypeStruct((T, RWKV_PAD), F32), jax.ShapeDtypeStruct((1, RWKV_PAD), F32)]
        + [jax.ShapeDtypeStruct((1, RW), F32)] * 5 + [jax.ShapeDtypeStruct((128, RW), F32)] * 3,
        scratch_shapes=[pltpu.VMEM((8, RWKV_PAD), F32), pltpu.VMEM((tr, RWKV_PAD), F32)] + [pltpu.VMEM((128, RW), F32)] * 3,
        compiler_params=_params(("arbitrary",)),
    )(p, pshift, dr_w, dw_w, dk_w, dv_w, da_w, db_w, dz, dg, mu, w0, a0, k_k, k_a, r_k, wlw, wla, wlg)


def _qk_norm(x, g, bd):
    r = lax.rsqrt(_segsum(x * x, bd) * (1.0 / HEAD_DIM) + RMS_EPS)
    return x * r * g, r


def _att_mask(i):
    qi = lax.broadcasted_iota(jnp.int32, (BLOCK, 2 * BLOCK), 0)
    kj = lax.broadcasted_iota(jnp.int32, (BLOCK, 2 * BLOCK), 1)
    band = (kj <= qi + BLOCK) & (kj > qi + BLOCK - WINDOW)
    return band & ((kj >= BLOCK) | (i > 0))


_HQK = (((2,), (2,)), ((0,), (0,)))
_HPV = (((2,), (1,)), ((0,), (0,)))
_HTN = (((1,), (1,)), ((0,), (0,)))


def _heads(x, n):
    return jnp.stack([x[:, h * HEAD_DIM:(h + 1) * HEAD_DIM] for h in range(n)])


def _unheads(x3):
    return jnp.concatenate([x3[h] for h in range(x3.shape[0])], axis=1)


def _kv_heads(x):
    x2 = _heads(x, KVW // HEAD_DIM)
    return jnp.concatenate([x2[g:g + 1] for g in range(KVW // HEAD_DIM) for _ in range(ATT_GROUP)], axis=0)


def _sinks3(sk):
    return jnp.stack([sk[0:1, h:h + 1] for h in range(N_HEADS)])


def _att_probs(q3, k3, mask, sink):
    s = _bdot(q3, k3, _HQK) * (HEAD_DIM ** -0.5)
    s = jnp.where(mask[None], s, NEG_BIG)
    m = jnp.maximum(jnp.max(s, axis=-1, keepdims=True), sink)
    pexp = jnp.exp(s - m)
    psink = jnp.exp(sink - m)
    inv = 1.0 / (jnp.sum(pexp, axis=-1, keepdims=True) + psink)
    return pexp * inv, psink * inv


def _att_blocks(n):
    cur = pl.BlockSpec((BLOCK, ATT_COLS), lambda i: (i, 0))
    prev = pl.BlockSpec((BLOCK, ATT_COLS), lambda i: (jnp.maximum(i - 1, 0), 0))
    return cur, prev


def _att_qkv(cur, prev, qn_g, kn_g):
    bq, bk = _seg_matrix(RW, 6), _seg_matrix(KVW, 6)
    qn, rq = _qk_norm(cur[:, 0:RW], qn_g, bq)
    kcur, rkc = _qk_norm(cur[:, RW:RW + KVW], kn_g, bk)
    kprev, _ = _qk_norm(prev[:, RW:RW + KVW], kn_g, bk)
    kc = jnp.concatenate([kprev, kcur], axis=0)
    vc = jnp.concatenate([prev[:, RW + KVW:], cur[:, RW + KVW:]], axis=0)
    return qn, rq, kc, vc, rkc


def att_fwd(pa, qn_g, kn_g, sinks, *, name):
    T = pa.shape[0]
    n = T // BLOCK

    def body(cur_ref, prev_ref, qg_ref, kg_ref, sk_ref, o_ref):
        i = pl.program_id(0)
        qn, _, kc, vc, _ = _att_qkv(cur_ref[...], prev_ref[...], qg_ref[...], kg_ref[...])
        probs, _ = _att_probs(_heads(qn, N_HEADS), _kv_heads(kc), _att_mask(i), _sinks3(sk_ref[...]))
        o_ref[...] = _unheads(_bdot(probs, _kv_heads(vc), _HPV))

    cur, prev = _att_blocks(n)
    return _pcall(body, name=name, grid=(n,),
                  in_specs=[cur, prev, _full_spec((1, RW)), _full_spec((1, KVW)), _full_spec((1, LANE))],
                  out_specs=pl.BlockSpec((BLOCK, RW), lambda i: (i, 0)), out_shape=jax.ShapeDtypeStruct((T, RW), F32),
                  compiler_params=_params(("parallel",)))(pa, pa, qn_g, kn_g, sinks)


def att_bwd(pa, do, qn_g, kn_g, sinks, *, name):
    T = pa.shape[0]
    n = T // BLOCK

    def body(cur_ref, prev_ref, do_ref, qg_ref, kg_ref, sk_ref,
             dq_ref, dko_ref, dkn_ref, dvo_ref, dvn_ref, dqg_ref, dsk_ref):
        i = pl.program_id(0)
        cur = cur_ref[...]
        qn, rq, kc, vc, _ = _att_qkv(cur, prev_ref[...], qg_ref[...], kg_ref[...])
        q3, k3, v3, do3 = _heads(qn, N_HEADS), _kv_heads(kc), _kv_heads(vc), _heads(do_ref[...], N_HEADS)
        probs, psink = _att_probs(q3, k3, _att_mask(i), _sinks3(sk_ref[...]))
        dprobs = _bdot(do3, v3, _HQK)
        delta = jnp.sum(probs * dprobs, axis=-1, keepdims=True)
        ds = probs * (dprobs - delta) * (HEAD_DIM ** -0.5)
        dsink3 = -jnp.sum(psink * delta, axis=1, keepdims=True)
        lane = lax.broadcasted_iota(jnp.int32, (1, LANE), 1)
        dsink = jnp.zeros((1, LANE), F32)
        for h in range(N_HEADS):
            dsink = dsink + jnp.where(lane == h, dsink3[h], 0.0)
        dqn = _unheads(_bdot(ds, k3, _HPV))

        def per_kv_head(x3):
            groups = [sum(x3[g * ATT_GROUP + j] for j in range(ATT_GROUP)) for g in range(KVW // HEAD_DIM)]
            return jnp.concatenate(groups, axis=1)

        dk, dv = per_kv_head(_bdot(ds, q3, _HTN)), per_kv_head(_bdot(probs, do3, _HTN))
        dkn_ref[...], dko_ref[...] = dk[0:BLOCK], dk[BLOCK:]
        dvn_ref[...], dvo_ref[...] = dv[0:BLOCK], dv[BLOCK:]
        qhat = cur[:, 0:RW] * rq
        dqh = dqn * qg_ref[...]
        dq_ref[...] = rq * (dqh - qhat * (_segsum(dqh * qhat, _seg_matrix(RW, 6)) * (1.0 / HEAD_DIM)))
        prod = dqn * qhat
        fold = prod[:, 0:HEAD_DIM]
        for h in range(1, N_HEADS):
            fold = fold + prod[:, h * HEAD_DIM:(h + 1) * HEAD_DIM]
        _acc_rows(dqg_ref, jnp.sum(fold, axis=0, keepdims=True), i)
        _acc_rows(dsk_ref, dsink, i)

    cur, prev = _att_blocks(n)
    kvb = pl.BlockSpec((BLOCK, KVW), lambda i: (i, 0))
    qb = pl.BlockSpec((BLOCK, RW), lambda i: (i, 0))
    return _pcall(body, name=name, grid=(n,),
                  in_specs=[cur, prev, qb, _full_spec((1, RW)), _full_spec((1, KVW)), _full_spec((1, LANE))],
                  out_specs=[qb, kvb, kvb, kvb, kvb, _full_spec((1, HEAD_DIM)), _full_spec((1, LANE))],
                  out_shape=[jax.ShapeDtypeStruct((T, RW), F32)] + [jax.ShapeDtypeStruct((T, KVW), F32)] * 4
                  + [jax.ShapeDtypeStruct((1, HEAD_DIM), F32), jax.ShapeDtypeStruct((1, LANE), F32)],
                  compiler_params=_params(("arbitrary",)))(pa, pa, do, qn_g, kn_g, sinks)


def att_kv_bwd(pa, dq, dko, dkn, dvo, dvn, kn_g, *, name):
    T = pa.shape[0]
    n = T // BLOCK

    def body(pa_ref, dq_ref, dko_ref, dkn_ref, dvo_ref, dvn_ref, kg_ref, dpa_ref, dkg_ref):
        i = pl.program_id(0)
        more = i < n - 1
        dkn_tot = dko_ref[...] + jnp.where(more, dkn_ref[...], 0.0)
        dv_tot = dvo_ref[...] + jnp.where(more, dvn_ref[...], 0.0)
        kraw = pa_ref[:, RW:RW + KVW]
        bk = _seg_matrix(KVW, 6)
        _, rk = _qk_norm(kraw, kg_ref[...], bk)
        khat = kraw * rk
        dkh = dkn_tot * kg_ref[...]
        dpa_ref[:, 0:RW] = dq_ref[...]
        dpa_ref[:, RW:RW + KVW] = rk * (dkh - khat * (_segsum(dkh * khat, bk) * (1.0 / HEAD_DIM)))
        dpa_ref[:, RW + KVW:] = dv_tot
        prod = dkn_tot * khat
        _acc_rows(dkg_ref, jnp.sum(prod[:, 0:HEAD_DIM] + prod[:, HEAD_DIM:], axis=0, keepdims=True), i)

    kvb = pl.BlockSpec((BLOCK, KVW), lambda i: (i, 0))
    nxt = pl.BlockSpec((BLOCK, KVW), lambda i: (jnp.minimum(i + 1, n - 1), 0))
    return _pcall(body, name=name, grid=(n,),
                  in_specs=[pl.BlockSpec((BLOCK, ATT_COLS), lambda i: (i, 0)), pl.BlockSpec((BLOCK, RW), lambda i: (i, 0)),
                            kvb, nxt, kvb, nxt, _full_spec((1, KVW))],
                  out_specs=[pl.BlockSpec((BLOCK, ATT_COLS), lambda i: (i, 0)), _full_spec((1, HEAD_DIM))],
                  out_shape=[jax.ShapeDtypeStruct((T, ATT_COLS), F32), jax.ShapeDtypeStruct((1, HEAD_DIM), F32)],
                  compiler_params=_params(("arbitrary",)))(pa, dq, dko, dkn, dvo, dvn, kn_g)


WKV_CHUNK = 64
WKV_GROUP = 8


def _diag_mask():
    i = lax.broadcasted_iota(jnp.int32, (HEAD_DIM, RW), 0)
    j = lax.broadcasted_iota(jnp.int32, (HEAD_DIM, RW), 1) & (HEAD_DIM - 1)
    return i == j


def _heads_matrix():
    head = jnp.arange(RW // 2) // HEAD_DIM
    bd = (head[:, None] == head[None, :]).astype(BF16)
    return jnp.concatenate([bd, bd], axis=0)


def _headsums(xs, pieces, bd2):
    half = RW // 2
    bd = bd2[:pieces * half]
    rows = []
    for x in xs:
        parts, rest = [], x
        for n in range(pieces):
            p = rest.astype(BF16)
            parts.append(p)
            if n + 1 < pieces:
                rest = rest - p.astype(F32)
        for sl in (slice(0, half), slice(half, RW)):
            rows.append(jnp.concatenate([p[:, sl] for p in parts], axis=1))
    out = lax.dot_general(jnp.concatenate(rows, axis=0), bd, (((1,), (0,)), ((), ())), preferred_element_type=F32)
    return [jnp.concatenate([out[2 * n * HEAD_DIM:(2 * n + 1) * HEAD_DIM], out[(2 * n + 1) * HEAD_DIM:(2 * n + 2) * HEAD_DIM]],
                            axis=1) for n in range(len(xs))]


def _headsum(x):
    low = lax.broadcasted_iota(jnp.int32, (HEAD_DIM, LANE), 1) < HEAD_DIM
    tiles = []
    for c in range(RW // LANE):
        xt = x[:, c * LANE:(c + 1) * LANE]
        s_lo = jnp.sum(jnp.where(low, xt, 0.0), axis=1, keepdims=True)
        s_hi = jnp.sum(jnp.where(low, 0.0, xt), axis=1, keepdims=True)
        tiles.append(jnp.where(low, s_lo, s_hi))
    return jnp.concatenate(tiles, axis=1)


def _cols(rows, diag, bd2):
    return _headsums([jnp.where(diag, r, 0.0) for r in rows], 2, bd2)


def _row(x, diag):
    return jnp.sum(jnp.where(diag, x, 0.0), axis=0, keepdims=True)


def wkv_fwd(r, w, k, v, a, b, *, name, gather=()):
    T = r.shape[0]
    ch = min(WKV_CHUNK, T)
    ngroups = ch // WKV_GROUP
    nchunks = T // ch
    ng = len(gather)

    def body(*refs):
        r_ref, w_ref, k_ref, v_ref, a_ref, b_ref, bd_ref = refs[:7]
        y_ref, st_ref = refs[7 + ng:9 + ng]
        s_scr = refs[9 + 2 * ng]
        step = pl.program_id(0)
        if ng:
            plan = _GatherPlan(refs[7:7 + ng], refs[9 + ng:9 + 2 * ng], refs[10 + 2 * ng:])
            pl.when(step == 0)(plan.start)
            pl.when(step == nchunks // 2)(plan.relay)

        @pl.when(step == 0)
        def _():
            s_scr[...] = jnp.zeros_like(s_scr)

        diag, bd2 = _diag_mask(), bd_ref[...]

        def group(gi, S):
            t0 = pl.multiple_of(gi * WKV_GROUP, WKV_GROUP)
            rows = pl.ds(t0, WKV_GROUP)
            R, W, K, V, A, B = (ref[rows, :] for ref in (r_ref, w_ref, k_ref, v_ref, a_ref, b_ref))
            vcols = _cols([V[s:s + 1] for s in range(WKV_GROUP)], diag, bd2)
            yrows = []
            for s in range(WKV_GROUP):
                sa = _headsum(S * A[s:s + 1])
                S = S * W[s:s + 1] + sa * B[s:s + 1] + vcols[s] * K[s:s + 1]
                st_ref[t0 + s] = S
                yrows.append(_row(_headsums([S * R[s:s + 1]], 2, bd2)[0], diag))
            y_ref[rows, :] = jnp.concatenate(yrows, axis=0)
            return S

        s_scr[...] = lax.fori_loop(0, ngroups, group, s_scr[...])
        if ng:
            pl.when(step == nchunks - 1)(plan.finish_relayed)

    vec = pl.BlockSpec((ch, RW), lambda c: (c, 0))
    return _pcall(
        body, name=name, grid=(nchunks,), in_specs=[vec] * 6 + [_full_spec((RW, RW // 2))] + [_HBM] * ng,
        out_specs=[vec, pl.BlockSpec((ch, HEAD_DIM, RW), lambda c: (c, 0, 0))] + [_HBM] * ng,
        out_shape=[jax.ShapeDtypeStruct((T, RW), F32), jax.ShapeDtypeStruct((T, HEAD_DIM, RW), F32)] + _gathered_shapes(gather),
        scratch_shapes=[pltpu.VMEM((HEAD_DIM, RW), F32)] + (_GatherPlan.sems(ng) if ng else []),
        compiler_params=_params(("arbitrary",)),
    )(r, w, k, v, a, b, _heads_matrix(), *gather)


def wkv_bwd(r, w, k, v, a, b, dy, states, *, name, exchange=()):
    T = r.shape[0]
    ch = min(WKV_CHUNK, T)
    nchunks = T // ch
    ngroups = ch // WKV_GROUP
    ne = len(exchange)

    def body(*refs):
        r_ref, w_ref, k_ref, v_ref, a_ref, b_ref, dy_ref, st_ref, stp_ref, bd_ref = refs[:10]
        dr_ref, dw_ref, dk_ref, dv_ref, da_ref, db_ref = refs[10 + ne:16 + ne]
        ds_scr = refs[16 + 2 * ne]
        step = pl.program_id(0)
        if ne:
            plan = _ExchangePlan(refs[10:10 + ne], refs[16 + ne:16 + 2 * ne], refs[17 + 2 * ne:])
            pl.when(step == 0)(plan.start)

        @pl.when(step == 0)
        def _():
            ds_scr[...] = jnp.zeros_like(ds_scr)

        has_prev_chunk = step < nchunks - 1
        diag, bd2 = _diag_mask(), bd_ref[...]
        colsum = lambda x: jnp.sum(x, axis=0, keepdims=True)

        def group(gj, dS):
            gi = ngroups - 1 - gj
            t0 = pl.multiple_of(gi * WKV_GROUP, WKV_GROUP)
            rows = pl.ds(t0, WKV_GROUP)
            R, W, K, V, A, B, DY = (ref[rows, :] for ref in (r_ref, w_ref, k_ref, v_ref, a_ref, b_ref, dy_ref))
            before = jnp.where(gi > 0, st_ref[jnp.maximum(t0 - 1, 0)], jnp.where(has_prev_chunk, stp_ref[0], 0.0))
            prev_state = lambda s: st_ref[t0 + s - 1] if s > 0 else before
            steps = range(WKV_GROUP)
            dycols = _cols([DY[s:s + 1] for s in steps], diag, bd2)
            vcols = _cols([V[s:s + 1] for s in steps], diag, bd2)
            sas = _headsums([prev_state(s) * A[s:s + 1] for s in steps], 1, bd2)
            got = [[None] * WKV_GROUP for _ in range(6)]
            for s in reversed(steps):
                Sp = prev_state(s)
                dS = dS + dycols[s] * R[s:s + 1]
                got[0][s] = colsum(st_ref[t0 + s] * dycols[s])
                got[3][s] = _row(_headsums([dS * K[s:s + 1]], 2, bd2)[0], diag)
                got[2][s] = colsum(dS * vcols[s])
                dsa = _headsum(dS * B[s:s + 1])
                got[5][s] = colsum(dS * sas[s])
                got[1][s] = colsum(dS * Sp)
                got[4][s] = colsum(Sp * dsa)
                dS = dS * W[s:s + 1] + dsa * A[s:s + 1]
            for q, ref in enumerate((dr_ref, dw_ref, dk_ref, dv_ref, da_ref, db_ref)):
                ref[rows, :] = jnp.concatenate(got[q], axis=0)
            return dS

        ds_scr[...] = lax.fori_loop(0, ngroups, group, ds_scr[...])
        if ne:
            pl.when(step == nchunks - 1)(plan.finish)

    vec = pl.BlockSpec((ch, RW), lambda c: (nchunks - 1 - c, 0))
    st_spec = pl.BlockSpec((ch, HEAD_DIM, RW), lambda c: (nchunks - 1 - c, 0, 0))
    stp_spec = pl.BlockSpec((1, HEAD_DIM, RW), lambda c: (jnp.maximum((nchunks - 1 - c) * ch - 1, 0), 0, 0))
    return _pcall(
        body, name=name, grid=(nchunks,), in_specs=[vec] * 7 + [st_spec, stp_spec, _full_spec((RW, RW // 2))] + [_HBM] * ne,
        out_specs=[vec] * 6 + [_HBM] * ne,
        out_shape=[jax.ShapeDtypeStruct((T, RW), F32)] * 6 + [jax.ShapeDtypeStruct(e.shape, e.dtype) for e in exchange],
        scratch_shapes=[pltpu.VMEM((HEAD_DIM, RW), F32)] + (_ExchangePlan.sems(ne) if ne else []),
        compiler_params=_params(("arbitrary",)),
    )(r, w, k, v, a, b, dy, states, states, _heads_matrix(), *exchange)


_HBM = pl.BlockSpec(memory_space=pltpu.HBM)
_MESH = pl.DeviceIdType.MESH


def _place():
    x, y, c = lax.axis_index("x"), lax.axis_index("y"), lax.axis_index("c")
    return x, y, c, [(1 - x, y), (x, 1 - y), (1 - x, 1 - y)]


def _remote(src, dst, send_sem, recv_sem, to):
    return pltpu.make_async_remote_copy(src_ref=src, dst_ref=dst, send_sem=send_sem, recv_sem=recv_sem, device_id=to,
                                        device_id_type=_MESH)


def _dma_sems(*counts):
    return [pltpu.SemaphoreType.DMA((n,)) for n in counts]


class _GatherPlan:
    def __init__(self, ins, outs, sems):
        self.ins, self.outs, self.n = ins, outs, len(ins)
        self.ici_send, self.ici_recv, self.d2d_send, self.d2d_recv, self.local_sems = sems
        x, y, c, chips = _place()
        self.c, self.me, self.sibling = c, 2 * x + y, (x, y, 1 - c)
        self.peers = [(2 * qx + qy, (qx, qy, c)) for qx, qy in chips]

    @staticmethod
    def sems(n):
        return _dma_sems(3 * n, 3 * n, 3 * n, 3 * n, n)

    def _half(self, i, which):
        rh = self.ins[i].shape[0] // 2
        return pl.ds(which * rh, rh)

    def _local(self, i):
        return pltpu.make_async_copy(self.ins[i], self.outs[i].at[self.me], self.local_sems.at[i])

    def _send(self, i, j):
        k, mine = 3 * i + j, self._half(i, self.c)
        return _remote(self.ins[i].at[mine], self.outs[i].at[self.me, mine], self.ici_send.at[k], self.ici_recv.at[k],
                       self.peers[j][1])

    def _landed(self, i, j):
        k, piece = 3 * i + j, self.outs[i].at[self.peers[j][0], self._half(i, self.c)]
        return _remote(piece, piece, self.ici_send.at[k], self.ici_recv.at[k], self.peers[j][1])

    def _pass(self, i, j, which):
        k, piece = 3 * i + j, self.outs[i].at[self.peers[j][0], self._half(i, which)]
        return _remote(piece, piece, self.d2d_send.at[k], self.d2d_recv.at[k], self.sibling)

    def _all(self):
        return [(i, j) for i in range(self.n) for j in range(3)]

    def start(self):
        for i in range(self.n):
            self._local(i).start()
        for i, j in self._all():
            self._send(i, j).start()

    def relay(self):
        for i, j in self._all():
            self._landed(i, j).wait_recv()
            self._pass(i, j, self.c).start()

    def finish_relayed(self):
        for i, j in self._all():
            self._pass(i, j, 1 - self.c).wait_recv()
        for i, j in self._all():
            self._send(i, j).wait_send()
            self._pass(i, j, self.c).wait_send()
        for i in range(self.n):
            self._local(i).wait()

    def finish(self):
        self.relay()
        self.finish_relayed()

    @staticmethod
    def out_shapes(shards):
        return _gathered_shapes(shards)


def _gathered_shapes(shards):
    return [jax.ShapeDtypeStruct((N_CHIPS,) + s.shape, s.dtype) for s in shards]


def gather_weights(shards, *, name):
    n = len(shards)

    def body(*refs):
        plan = _GatherPlan(refs[:n], refs[n:2 * n], refs[2 * n:])
        plan.start()
        plan.finish()

    return _pcall(body, name=name, in_specs=[_HBM] * n, out_specs=[_HBM] * n, out_shape=_gathered_shapes(shards),
                  scratch_shapes=_GatherPlan.sems(n), compiler_params=_params())(*shards)


class _SiblingPlan:
    halves = True

    def __init__(self, ins, outs, sems):
        self.ins, self.outs, self.n = ins, outs, len(ins)
        self.send_sems, self.recv_sems = sems
        x, y, c, _ = _place()
        self.c, self.sibling = c, (x, y, 1 - c)

    @staticmethod
    def sems(n):
        return _dma_sems(n, n)

    @classmethod
    def out_shapes(cls, arrays):
        if not cls.halves:
            return [jax.ShapeDtypeStruct(a.shape, a.dtype) for a in arrays]
        return [jax.ShapeDtypeStruct((a.shape[0], a.shape[1] // 2, a.shape[2]), a.dtype) for a in arrays]

    def _copy(self, i):
        src = self.ins[i]
        if self.halves:
            rh = src.shape[1] // 2
            src = src.at[:, pl.ds((1 - self.c) * rh, rh)]
        return _remote(src, self.outs[i], self.send_sems.at[i], self.recv_sems.at[i], self.sibling)

    def start(self):
        for i in range(self.n):
            self._copy(i).start()

    def finish(self):
        for i in range(self.n):
            self._copy(i).wait_recv()
        for i in range(self.n):
            self._copy(i).wait_send()


class _SiblingWhole(_SiblingPlan):
    halves = False


def to_sibling(arrays, take_other_half, *, name):
    n = len(arrays)
    plan_cls = _SiblingPlan if take_other_half else _SiblingWhole

    def body(*refs):
        plan = plan_cls(refs[:n], refs[n:2 * n], refs[2 * n:])
        plan.start()
        plan.finish()

    return _pcall(body, name=name, in_specs=[_HBM] * n, out_specs=[_HBM] * n, out_shape=plan_cls.out_shapes(arrays),
                  scratch_shapes=plan_cls.sems(n), compiler_params=_params())(*arrays)


def exchange_chips(arrays, *, name):
    n = len(arrays)

    def body(*refs):
        plan = _ExchangePlan(refs[:n], refs[n:2 * n], refs[2 * n:])
        plan.start()
        plan.finish()

    return _pcall(body, name=name, in_specs=[_HBM] * n, out_specs=[_HBM] * n,
                  out_shape=[jax.ShapeDtypeStruct(a.shape, a.dtype) for a in arrays],
                  scratch_shapes=_ExchangePlan.sems(n), compiler_params=_params())(*arrays)


class _ExchangePlan:
    def __init__(self, ins, outs, sems):
        self.ins, self.outs, self.n = ins, outs, len(ins)
        self.send_sems, self.recv_sems, self.local_sems = sems
        x, y, c, chips = _place()
        self.me = 2 * x + y
        self.peers = [(2 * qx + qy, (qx, qy, c)) for qx, qy in chips]

    @staticmethod
    def sems(n):
        return _dma_sems(3 * n, 3 * n, n)

    @staticmethod
    def out_shapes(arrays):
        return [jax.ShapeDtypeStruct(a.shape, a.dtype) for a in arrays]

    def _local(self, i):
        return pltpu.make_async_copy(self.ins[i].at[self.me], self.outs[i].at[self.me], self.local_sems.at[i])

    def _send(self, i, j):
        k = 3 * i + j
        return _remote(self.ins[i].at[self.peers[j][0]], self.outs[i].at[self.me], self.send_sems.at[k], self.recv_sems.at[k],
                       self.peers[j][1])

    def _landed(self, i, j):
        k, piece = 3 * i + j, self.outs[i].at[self.peers[j][0]]
        return _remote(piece, piece, self.send_sems.at[k], self.recv_sems.at[k], self.peers[j][1])

    def start(self):
        for i in range(self.n):
            self._local(i).start()
            for j in range(3):
                self._send(i, j).start()

    def finish(self):
        for i in range(self.n):
            for j in range(3):
                self._landed(i, j).wait_recv()
        for i in range(self.n):
            for j in range(3):
                self._send(i, j).wait_send()
            self._local(i).wait()


def _core_index():
    return lax.axis_index("c").astype(jnp.int32).reshape(1)


def pair_sum(g, theirs, wire_dtype, *, name):
    _, R, C = g.shape
    rh = R // 2
    tr = _tile(rh, 256, 16)
    nt = rh // tr

    def body(c_ref, g_ref, t_ref, q_ref, qw_ref):
        q = g_ref[...] + t_ref[...]
        q_ref[...] = q
        qw_ref[...] = q.astype(wire_dtype)

    blk = pl.BlockSpec((1, tr, C), lambda b, i, c_ref: (b, i, 0))
    mine = pl.BlockSpec((1, tr, C), lambda b, i, c_ref: (b, c_ref[0] * nt + i, 0))
    grid_spec = pltpu.PrefetchScalarGridSpec(num_scalar_prefetch=1, grid=(N_CHIPS, nt), in_specs=[mine, blk], out_specs=[blk, blk])
    return _pcall(body, name=name, grid_spec=grid_spec,
                  out_shape=[jax.ShapeDtypeStruct((N_CHIPS, rh, C), F32), jax.ShapeDtypeStruct((N_CHIPS, rh, C), wire_dtype)],
                  compiler_params=_params(("parallel", "parallel")))(_core_index(), g, theirs)


def half_sum(own, landed, *, name):
    _, rh, C = own.shape
    tr = _tile(rh, 256, 16)

    def body(me_ref, own_ref, land_ref, o_ref):
        total = None
        for p in range(N_CHIPS):
            term = jnp.where(me_ref[0] == p, own_ref[p], land_ref[p].astype(F32))
            total = term if total is None else total + term
        o_ref[...] = total

    blk = pl.BlockSpec((N_CHIPS, tr, C), lambda i, me_ref: (0, i, 0))
    grid_spec = pltpu.PrefetchScalarGridSpec(num_scalar_prefetch=1, grid=(rh // tr,), in_specs=[blk, blk],
                                             out_specs=pl.BlockSpec((tr, C), lambda i, me_ref: (i, 0)))
    me = (2 * lax.axis_index("x") + lax.axis_index("y")).astype(jnp.int32).reshape(1)
    return _pcall(body, name=name, grid_spec=grid_spec, out_shape=jax.ShapeDtypeStruct((rh, C), F32),
                  compiler_params=_params(("parallel",)))(me, own, landed)


def adamw(w, m, v, mine, theirs, *, name):
    _, R, C = w.shape
    rh = R // 2
    tr = _tile(rh, 256, 8)
    nt = rh // tr

    def body(c_ref, w_ref, m_ref, v_ref, a_ref, b_ref, g_ref, d_ref, nm_ref, nv_ref):
        is_mine = (pl.program_id(0) // nt) == c_ref[0]
        g = jnp.where(is_mine, a_ref[...], b_ref[...])
        g_ref[...] = g
        nm = ADAM_B1 * m_ref[...] + (1.0 - ADAM_B1) * g
        nv = ADAM_B2 * v_ref[...] + (1.0 - ADAM_B2) * (g * g)
        nm_ref[...] = nm
        nv_ref[...] = nv
        m_hat = nm / (1.0 - ADAM_B1 ** ADAM_STEP)
        v_hat = nv / (1.0 - ADAM_B2 ** ADAM_STEP)
        d_ref[...] = -ADAM_LR * (m_hat / (jnp.sqrt(v_hat) + ADAM_EPS) + ADAM_WD * w_ref[...])

    full = pl.BlockSpec((None, tr, C), lambda i, c_ref: (0, i, 0))
    a_spec = pl.BlockSpec((tr, C), lambda i, c_ref: (jnp.clip(i - c_ref[0] * nt, 0, nt - 1), 0))
    b_spec = pl.BlockSpec((tr, C), lambda i, c_ref: (jnp.clip(i - (1 - c_ref[0]) * nt, 0, nt - 1), 0))
    grid_spec = pltpu.PrefetchScalarGridSpec(num_scalar_prefetch=1, grid=(2 * nt,), in_specs=[full] * 3 + [a_spec, b_spec],
                                             out_specs=[full] * 4)
    return _pcall(body, name=name, grid_spec=grid_spec, out_shape=[jax.ShapeDtypeStruct((1, R, C), F32)] * 4,
                  compiler_params=_params(("arbitrary",)))(_core_index(), w, m, v, mine, theirs)


def _to_blocks(full, axis):
    r, c = full.shape
    if axis == 1:
        return full.reshape(r, N_CHIPS, c // N_CHIPS).transpose(1, 0, 2)
    return full.reshape(N_CHIPS, r // N_CHIPS, c)


def _from_blocks(blocks, axis):
    _, r, c = blocks.shape
    if axis == 1:
        return blocks.transpose(1, 0, 2).reshape(r, N_CHIPS * c)
    return blocks.reshape(N_CHIPS * r, c)


def _ffn_fwd(x, norm, wg_t, wu_t, wd, tag, carry=None):
    h = rms_fwd(x, norm, name=tag + "_norm")
    gate, up, act, *carried = mm_fused(h, [wg_t, wu_t], _swiglu, [BF16] * 3, tb=True, name=tag + "_gate_up", carry=carry)
    out = mm(act, wd, scale=0.5, res=x, name=tag + "_down")
    return out, (h, gate, up, act), carried


def _ffn_bwd(dout, x, saved, norm, wg_t, wu_t, wd, tag, carry=None, reduce=None):
    h, gate, up, act = saved
    dgate, dup, *carried = mm_fused(dout, [wd], _swiglu_bwd, [BF16] * 2, tb=True, extras=[gate, up], name=tag + "_dact",
                                    carry=carry)
    dwd = mm(act, dout, ta=True, scale=0.5, name=tag + "_dwd")
    dwg_t = mm(dgate, h, ta=True, name=tag + "_dwg")
    dwu_t = mm(dup, h, ta=True, name=tag + "_dwu")
    reduced = None
    if reduce:
        blocks = [_to_blocks(g, 0) for g in (dwg_t, dwu_t, dwd)]
        dh, *from_sibling = mm(dgate, wg_t, name=tag + "_dh_gate", carry=(_SiblingPlan, blocks))
        pair = [pair_sum(g, t, BF16, name="pair_sum_" + n) for g, t, n in zip(blocks, from_sibling, reduce)]
        dh, *landed = mm(dup, wu_t, res=dh, name=tag + "_dh_up", carry=(_ExchangePlan, [q for _, q in pair]))
        reduced = (pair, landed)
    else:
        dh = mm(dgate, wg_t, name=tag + "_dh_gate")
        dh = mm(dup, wu_t, res=dh, name=tag + "_dh_up")
    dx, dnorm = rms_bwd(dh, x, norm, dout, name=tag + "_dnorm")
    return dx, dnorm, dwg_t, dwu_t, dwd, carried, reduced


TRANSPOSED = ('ffn1_w_gate', 'ffn1_w_up', 'ffn2_w_gate', 'ffn2_w_up')
FIRST_WEIGHTS = ['ffn1_w_gate', 'ffn1_w_up', 'ffn1_w_down']
MID_WEIGHTS = ['w_in', 'rwkv_w_lora_up', 'rwkv_a_lora_up', 'rwkv_g_lora_up']
LATE_WEIGHTS = ['w_branch_rwkv', 'w_branch_attn', 'w_out', 'ffn2_w_gate', 'ffn2_w_up', 'ffn2_w_down']


def _pair_sums(names, blocks, tag):
    from_sibling = to_sibling(blocks, True, name=tag + "_grads_to_sibling")
    return [pair_sum(g, t, F32 if n == 'small' else BF16, name="pair_sum_" + n)
            for g, t, n in zip(blocks, from_sibling, names)]


def _step(A):
    x, tgt = A['x'][0], A['loss_target'][0]
    T = x.shape[0]
    w = {n: A[n][0] for n in WEIGHT_NAMES}
    row = lambda a: a.reshape(1, -1)

    axis_of = {n: (0 if n in TRANSPOSED else axis) for n, axis in BIG}
    natural = lambda n, a: jnp.swapaxes(a, 1, 2) if n in TRANSPOSED else a
    shard = lambda n: natural(n, A[n])[0].astype(BF16)
    n1, nmix, n2, nfin = (row(w[n]) for n in ('ffn1_norm', 'mix_norm', 'ffn2_norm', 'final_norm'))
    gathered = gather_weights([shard(n) for n in FIRST_WEIGHTS], name="gather_weights")
    full = {n: _from_blocks(b, axis_of[n]) for n, b in zip(FIRST_WEIGHTS, gathered)}
    x1, ffn1, gathered = _ffn_fwd(x, n1, full['ffn1_w_gate'], full['ffn1_w_up'], full['ffn1_w_down'], "ffn1",
                                  carry=(_GatherPlan, [shard(n) for n in MID_WEIGHTS]))
    full.update({n: _from_blocks(b, axis_of[n]) for n, b in zip(MID_WEIGHTS, gathered)})
    w_in_r = _pad_rwkv_cols(full['w_in'][:, :RWKV_COLS])
    w_in_a = full['w_in'][:, RWKV_COLS:RWKV_COLS + ATT_COLS]
    w_in_g = full['w_in'][:, RWKV_COLS + ATT_COLS:]
    wlw, wla, wlg = (_pad_rows(full[n], 128).astype(F32) for n in ('rwkv_w_lora_up', 'rwkv_a_lora_up', 'rwkv_g_lora_up'))
    mu = _pad_rwkv_cols(row(w['rwkv_mu']))
    w0, a0, k_k, k_a, r_k, ln_w, ln_b = (row(w[n]) for n in ('rwkv_w0', 'rwkv_a0', 'rwkv_k_k', 'rwkv_k_a', 'rwkv_r_k',
                                                               'rwkv_ln_w', 'rwkv_ln_b'))
    qg = jnp.tile(row(w['attn_q_norm']), (1, N_HEADS))
    kg = jnp.tile(row(w['attn_k_norm']), (1, KVW // HEAD_DIM))
    sinks = jnp.pad(row(w['attn_sinks']), ((0, 0), (0, LANE - N_HEADS)))

    h2 = rms_fwd(x1, nmix, name="mix_norm")
    pr = mm(h2, w_in_r, name="proj_rwkv")
    pa = mm(h2, w_in_a, name="proj_att")
    pg = mm(h2, w_in_g, name="proj_gate")
    pr_shift = jnp.pad(pr, ((1, 0), (0, 0)))[:-1]
    r, dec, k2, v, a, b, sg = rwkv_pre_fwd(pr, pr_shift, mu, w0, a0, k_k, k_a, wlw, wla, wlg, name="rwkv_pre")
    y, states, *gathered = wkv_fwd(r, dec, k2, v, a, b, name="wkv_fwd", gather=[shard(n) for n in LATE_WEIGHTS])
    full.update({n: _from_blocks(b, axis_of[n]) for n, b in zip(LATE_WEIGHTS, gathered)})
    yr = rwkv_post_fwd(y, r, k2, v, sg, wlg, ln_w, ln_b, r_k, name="rwkv_post")
    ya = att_fwd(pa, qg, kg, sinks, name="att_fwd")
    br = mm(yr, full['w_branch_rwkv'], name="branch_rwkv")
    ba = mm(ya, full['w_branch_attn'], name="branch_att")
    mg = merge_fwd(br, ba, pg, name="merge")
    x2 = mm(mg, full['w_out'], res=x1, name="mix_out")
    x3, ffn2, _ = _ffn_fwd(x2, n2, full['ffn2_w_gate'], full['ffn2_w_up'], full['ffn2_w_down'], "ffn2")
    dx3, d_nfin, loss = final_loss(x3, tgt, nfin, name="final_loss")

    G = {'final_norm': d_nfin}
    dx2, G['ffn2_norm'], G['ffn2_w_gate'], G['ffn2_w_up'], G['ffn2_w_down'], _, _ = _ffn_bwd(
        dx3, x2, ffn2, n2, full['ffn2_w_gate'], full['ffn2_w_up'], full['ffn2_w_down'], "ffn2")
    dmg = mm(dx2, full['w_out'], tb=True, name="d_merge")
    G['w_out'] = mm(mg, dx2, ta=True, name="d_w_out")
    dbr, dba, dpg = merge_bwd(dmg, br, ba, pg, name="merge_bwd")
    dyr = mm(dbr, full['w_branch_rwkv'], tb=True, name="d_y_rwkv")
    G['w_branch_rwkv'] = mm(yr, dbr, ta=True, name="d_w_branch_rwkv")
    dya = mm(dba, full['w_branch_attn'], tb=True, name="d_y_att")
    G['w_branch_attn'] = mm(ya, dba, ta=True, name="d_w_branch_att")
    late_blocks = [_to_blocks(G[n], axis_of[n]) for n in LATE_WEIGHTS]
    dy, dz, dg, G['rwkv_ln_w'], G['rwkv_ln_b'], *late_from_sibling = rwkv_post_bwd(
        dyr, y, r, k2, v, sg, wlg, ln_w, ln_b, r_k, name="rwkv_post_bwd", carry=(_SiblingPlan, late_blocks))
    late_pair = [pair_sum(g, t, BF16, name="pair_sum_" + n) for g, t, n in zip(late_blocks, late_from_sibling, LATE_WEIGHTS)]
    res = wkv_bwd(r, dec, k2, v, a, b, dy, states, name="wkv_bwd", exchange=[q for _, q in late_pair])
    wkv_grads, late_landed = res[:6], res[6:]
    (dpr, d_mu, G['rwkv_w0'], G['rwkv_a0'], G['rwkv_k_k'], G['rwkv_k_a'], G['rwkv_r_k'], d_wlw, d_wla, d_wlg) = rwkv_pre_bwd(
        pr, pr_shift, *wkv_grads, dz, dg, mu, w0, a0, k_k, k_a, r_k, wlw, wla, wlg, name="rwkv_pre_bwd")
    G['rwkv_mu'] = _unpad_rwkv_cols(d_mu)
    G['rwkv_w_lora_up'], G['rwkv_a_lora_up'], G['rwkv_g_lora_up'] = d_wlw[:DECAY_LORA], d_wla[:ICLR_LORA], d_wlg[:GATE_LORA]
    dq, dko, dkn, dvo, dvn, G['attn_q_norm'], d_sinks = att_bwd(pa, dya, qg, kg, sinks, name="att_bwd")
    G['attn_sinks'] = d_sinks[:, :N_HEADS]
    dpa, G['attn_k_norm'] = att_kv_bwd(pa, dq, dko, dkn, dvo, dvn, kg, name="att_kv_bwd")
    d_w_in_r = mm(h2, dpr, ta=True, name="d_w_in_rwkv")
    d_w_in_a = mm(h2, dpa, ta=True, name="d_w_in_att")
    d_w_in_g = mm(h2, dpg, ta=True, name="d_w_in_gate")
    G['w_in'] = jnp.concatenate([_unpad_rwkv_cols(d_w_in_r), d_w_in_a, d_w_in_g], axis=1)
    dh2 = mm(dpr, w_in_r, tb=True, name="d_h2_rwkv")
    dh2 = mm(dpa, w_in_a, tb=True, res=dh2, name="d_h2_att")
    dh2 = mm(dpg, w_in_g, tb=True, res=dh2, name="d_h2_gate")
    dx1, G['mix_norm'] = rms_bwd(dh2, x1, nmix, dx2, name="d_mix_norm")
    mid_pair = _pair_sums(MID_WEIGHTS, [_to_blocks(G[n], axis_of[n]) for n in MID_WEIGHTS], "mid")
    dx0, G['ffn1_norm'], _, _, _, mid_landed, (first_pair, first_landed) = _ffn_bwd(
        dx1, x, ffn1, n1, full['ffn1_w_gate'], full['ffn1_w_up'], full['ffn1_w_down'], "ffn1",
        carry=(_ExchangePlan, [q for _, q in mid_pair]), reduce=FIRST_WEIGHTS)

    small_shapes = [(w[n].size,) for n in SMALL] + [(1,)]

    def small_rows(parts):
        vec = jnp.concatenate([p.reshape(-1) for p in parts])
        return jnp.pad(vec, (0, SMALL_ROWS * FLAT_W - vec.shape[0])).reshape(SMALL_ROWS, FLAT_W)

    small = small_rows([G[n] for n in SMALL] + [loss[0, :1]])
    small_pair = _pair_sums(['small'], [jnp.broadcast_to(small[None], (N_CHIPS,) + small.shape)], "small")
    small_landed = exchange_chips([q for _, q in small_pair], name="exchange_small")
    names = FIRST_WEIGHTS + ['small'] + MID_WEIGHTS + LATE_WEIGHTS
    pair = first_pair + small_pair + mid_pair + late_pair
    landed = list(first_landed) + list(small_landed) + list(mid_landed) + list(late_landed)
    halves = [half_sum(own, l, name="half_sum_" + n) for (own, _), l, n in zip(pair, landed, names)]
    other_halves = to_sibling(halves, False, name="halves_to_sibling")

    def local(prefix, n):
        if n != 'small':
            return natural(n, A[prefix + n])
        return small_rows([A[prefix + s] for s in SMALL] + [jnp.zeros((1,), F32)])[None]

    result = {}
    for n, mine, theirs in zip(names, halves, other_halves):
        outs4 = adamw(local('', n), local('m_', n), local('v_', n), mine, theirs, name="adamw_" + n)
        for kind, o in zip(('grad_', 'delta_', 'new_m_', 'new_v_'), outs4):
            if n != 'small':
                result[kind + n] = natural(n, o)
            else:
                for s, part in zip(SMALL + ['loss'], _unpack_vec(o.reshape(-1), small_shapes)):
                    result[kind + s] = part.reshape(A[s].shape) if s != 'loss' else part.reshape(())
    outs = [result['grad_loss'], dx0[None]]
    for kind in ('grad_', 'delta_', 'new_m_', 'new_v_'):
        outs += [result[kind + n] for n in WEIGHT_NAMES]
    return tuple(outs)


def _unpack_vec(vec, shapes):
    out, off = [], 0
    for (n,) in shapes:
        out.append(vec[off:off + n])
        off += n
    return out


def kernel(x, ffn1_norm, ffn1_w_gate, ffn1_w_up, ffn1_w_down, mix_norm, w_in, rwkv_mu, rwkv_w0, rwkv_w_lora_up, rwkv_a0, rwkv_a_lora_up, rwkv_g_lora_up, rwkv_k_k, rwkv_k_a, rwkv_r_k, rwkv_ln_w, rwkv_ln_b, attn_q_norm, attn_k_norm, attn_sinks, w_branch_rwkv, w_branch_attn, w_out, ffn2_norm, ffn2_w_gate, ffn2_w_up, ffn2_w_down, final_norm, loss_target, m_ffn1_norm, m_ffn1_w_gate, m_ffn1_w_up, m_ffn1_w_down, m_mix_norm, m_w_in, m_rwkv_mu, m_rwkv_w0, m_rwkv_w_lora_up, m_rwkv_a0, m_rwkv_a_lora_up, m_rwkv_g_lora_up, m_rwkv_k_k, m_rwkv_k_a, m_rwkv_r_k, m_rwkv_ln_w, m_rwkv_ln_b, m_attn_q_norm, m_attn_k_norm, m_attn_sinks, m_w_branch_rwkv, m_w_branch_attn, m_w_out, m_ffn2_norm, m_ffn2_w_gate, m_ffn2_w_up, m_ffn2_w_down, m_final_norm, v_ffn1_norm, v_ffn1_w_gate, v_ffn1_w_up, v_ffn1_w_down, v_mix_norm, v_w_in, v_rwkv_mu, v_rwkv_w0, v_rwkv_w_lora_up, v_rwkv_a0, v_rwkv_a_lora_up, v_rwkv_g_lora_up, v_rwkv_k_k, v_rwkv_k_a, v_rwkv_r_k, v_rwkv_ln_w, v_rwkv_ln_b, v_attn_q_norm, v_attn_k_norm, v_attn_sinks, v_w_branch_rwkv, v_w_branch_attn, v_w_out, v_ffn2_norm, v_ffn2_w_gate, v_ffn2_w_up, v_ffn2_w_down, v_final_norm):
    return _step(dict(locals()))
```

```python
import functools

import jax
import jax.numpy as jnp
from jax import lax
from jax.experimental import pallas as pl
from jax.experimental.pallas import tpu as pltpu

F32 = jnp.float32
BF16 = jnp.bfloat16

D_MODEL = 1024
D_FF = 2816
HEAD_DIM = 64
N_HEADS = 8
RW = 512
KVW = 128
ATT_GROUP = 4
WINDOW = 128
BLOCK = 128
DECAY_LORA, ICLR_LORA, GATE_LORA = 32, 32, 96
RWKV_COLS = 3 * RW + DECAY_LORA + ICLR_LORA + GATE_LORA
ATT_COLS = RW + 2 * KVW
GATE_COLS = 2 * D_MODEL
RWKV_PAD = 3 * RW + 3 * 128
RMS_EPS = 1e-6
GN_EPS = 64e-5
N_CHIPS = 4
LANE = 128
FLAT_W = 1024
SMALL_ROWS = 32
NEG_BIG = -1e30

ADAM_LR, ADAM_B1, ADAM_B2, ADAM_EPS, ADAM_WD, ADAM_STEP = 0.001, 0.9, 0.999, 1e-08, 0.01, 10

VMEM_LIMIT = 56 * 1024 * 1024

WEIGHT_NAMES = ['ffn1_norm', 'ffn1_w_gate', 'ffn1_w_up', 'ffn1_w_down', 'mix_norm', 'w_in', 'rwkv_mu', 'rwkv_w0',
                'rwkv_w_lora_up', 'rwkv_a0', 'rwkv_a_lora_up', 'rwkv_g_lora_up', 'rwkv_k_k', 'rwkv_k_a', 'rwkv_r_k',
                'rwkv_ln_w', 'rwkv_ln_b', 'attn_q_norm', 'attn_k_norm', 'attn_sinks', 'w_branch_rwkv',
                'w_branch_attn', 'w_out', 'ffn2_norm', 'ffn2_w_gate', 'ffn2_w_up', 'ffn2_w_down', 'final_norm']
BIG = [('ffn1_w_gate', 1), ('ffn1_w_up', 1), ('ffn1_w_down', 0), ('w_in', 1), ('rwkv_w_lora_up', 1),
       ('rwkv_a_lora_up', 1), ('rwkv_g_lora_up', 1), ('w_branch_rwkv', 1), ('w_branch_attn', 1), ('w_out', 0),
       ('ffn2_w_gate', 1), ('ffn2_w_up', 1), ('ffn2_w_down', 0)]
SMALL = ['ffn1_norm', 'mix_norm', 'rwkv_mu', 'rwkv_w0', 'rwkv_a0', 'rwkv_k_k', 'rwkv_k_a', 'rwkv_r_k', 'rwkv_ln_w',
         'rwkv_ln_b', 'attn_q_norm', 'attn_k_norm', 'attn_sinks', 'ffn2_norm', 'final_norm']


def _pcall(body, **kw):
    return pl.pallas_call(body, **kw)


def _params(sem=None, **kw):
    if sem is not None:
        kw['dimension_semantics'] = sem
    return pltpu.CompilerParams(vmem_limit_bytes=VMEM_LIMIT, **kw)


def _tile(n, cap, mult):
    best = None
    for t in range(mult, min(n, cap) + 1, mult):
        if n % t == 0:
            best = t
    return best or n


def _sigmoid(z):
    return 1.0 / (1.0 + jnp.exp(-z))


def _softplus(z):
    return jnp.maximum(z, 0.0) + jnp.log(1.0 + jnp.exp(-jnp.abs(z)))


def _bdot(a, b, dims=(((1,), (0,)), ((), ()))):
    return lax.dot_general(a.astype(BF16), b.astype(BF16), dims, preferred_element_type=F32)


_NT = (((1,), (1,)), ((), ()))
_TN = (((0,), (0,)), ((), ()))


def _segsum(x, bd):
    hi = x.astype(BF16)
    r1 = x - hi.astype(F32)
    mid = r1.astype(BF16)
    lo = (r1 - mid.astype(F32)).astype(BF16)
    dot = functools.partial(lax.dot_general, dimension_numbers=(((1,), (0,)), ((), ())), preferred_element_type=F32)
    return dot(hi, bd) + dot(mid, bd) + dot(lo, bd)


_LORA_EDGES = (3 * RW, 3 * RW + DECAY_LORA, 3 * RW + DECAY_LORA + ICLR_LORA, RWKV_COLS)


def _pad_rwkv_cols(x):
    parts = [x[..., :3 * RW]]
    for lo, hi in zip(_LORA_EDGES[:-1], _LORA_EDGES[1:]):
        parts.append(jnp.pad(x[..., lo:hi], [(0, 0)] * (x.ndim - 1) + [(0, 128 - (hi - lo))]))
    return jnp.concatenate(parts, axis=-1)


def _unpad_rwkv_cols(x):
    parts = [x[..., :3 * RW]]
    for j, (lo, hi) in enumerate(zip(_LORA_EDGES[:-1], _LORA_EDGES[1:])):
        parts.append(x[..., 3 * RW + 128 * j:3 * RW + 128 * j + (hi - lo)])
    return jnp.concatenate(parts, axis=-1)


def _pad_rows(x, rows):
    return jnp.pad(x, [(0, rows - x.shape[0])] + [(0, 0)] * (x.ndim - 1))


def mm(a, b, *, name, ta=False, tb=False, scale=None, res=None, out_dtype=F32, carry=None):
    M, K = (a.shape[1], a.shape[0]) if ta else a.shape
    N = b.shape[0] if tb else b.shape[1]
    assert (b.shape[1] if tb else b.shape[0]) == K
    tm, tn, tk = _tile(M, 1408 if ta else 512, 128), _tile(N, 1408, 128), _tile(K, 1408, 128)
    nk = K // tk
    grid = (M // tm, N // tn, nk)
    dims = (((0 if ta else 1,), (1 if tb else 0,)), ((), ()))
    plan_cls, carried = carry if carry else (None, ())
    nc, nin = len(carried), 2 + (res is not None)

    def body(*refs):
        a_ref, b_ref = refs[:2]
        r_ref = refs[2] if res is not None else None
        o_ref, acc_ref = refs[nin + nc], refs[nin + 2 * nc + 1]
        k = pl.program_id(2)
        if nc:
            plan = plan_cls(refs[nin:nin + nc], refs[nin + nc + 1:nin + 2 * nc + 1], refs[nin + 2 * nc + 2:])
            at = lambda which: functools.reduce(jnp.logical_and, [pl.program_id(d) == (0 if which == 0 else grid[d] - 1)
                                                                 for d in range(3)])
            pl.when(at(0))(plan.start)
        part = _bdot(a_ref[...], b_ref[...], dims)

        @pl.when(k == 0)
        def _():
            acc_ref[...] = part

        @pl.when(k > 0)
        def _():
            acc_ref[...] += part

        @pl.when(k == nk - 1)
        def _():
            o = acc_ref[...]
            if scale is not None:
                o = o * scale
            if r_ref is not None:
                o = o + r_ref[...].astype(F32)
            o_ref[...] = o.astype(out_dtype)

        if nc:
            pl.when(at(1))(plan.finish)

    a_spec = pl.BlockSpec((tk, tm), lambda i, j, k: (k, i)) if ta else pl.BlockSpec((tm, tk), lambda i, j, k: (i, k))
    b_spec = pl.BlockSpec((tn, tk), lambda i, j, k: (j, k)) if tb else pl.BlockSpec((tk, tn), lambda i, j, k: (k, j))
    o_spec = pl.BlockSpec((tm, tn), lambda i, j, k: (i, j))
    in_specs = [a_spec, b_spec] + ([o_spec] if res is not None else [])
    args = (a, b) + ((res,) if res is not None else ())
    out_shape = jax.ShapeDtypeStruct((M, N), out_dtype)
    if not nc:
        return _pcall(
            body, name=name, grid=grid, in_specs=in_specs, out_specs=o_spec, out_shape=out_shape,
            scratch_shapes=[pltpu.VMEM((tm, tn), F32)], compiler_params=_params(("parallel", "parallel", "arbitrary")),
        )(*args)
    return _pcall(
        body, name=name, grid=grid, in_specs=in_specs + [_HBM] * nc, out_specs=[o_spec] + [_HBM] * nc,
        out_shape=[out_shape] + plan_cls.out_shapes(carried), scratch_shapes=[pltpu.VMEM((tm, tn), F32)] + plan_cls.sems(nc),
        compiler_params=_params(("arbitrary", "arbitrary", "arbitrary")),
    )(*args, *carried)


def mm_fused(a, bs, finish, out_dtypes, *, name, tb=False, extras=(), carry=None):
    M, K = a.shape
    N = bs[0].shape[0] if tb else bs[0].shape[1]
    tm, tn = _tile(M, 512, 128), _tile(N, 1408, 128)
    grid = (M // tm, N // tn)
    dims = (((1,), (1 if tb else 0,)), ((), ()))
    plan_cls, carried = carry if carry else (None, ())
    nc, nb, nx, no = len(carried), len(bs), len(extras), len(out_dtypes)
    nin = 1 + nb + nx

    def body(*refs):
        a_ref, b_refs, x_refs = refs[0], refs[1:1 + nb], refs[1 + nb:nin]
        o_refs = refs[nin + nc:nin + nc + no]
        if nc:
            plan = plan_cls(refs[nin:nin + nc], refs[nin + nc + no:nin + 2 * nc + no], refs[nin + 2 * nc + no:])
            at = lambda which: jnp.logical_and(*[pl.program_id(d) == (0 if which == 0 else grid[d] - 1) for d in range(2)])
            pl.when(at(0))(plan.start)
        av = a_ref[...]
        outs = finish([_bdot(av, b_ref[...], dims) for b_ref in b_refs], [x_ref[...] for x_ref in x_refs])
        for o_ref, o in zip(o_refs, outs):
            o_ref[...] = o.astype(o_ref.dtype)
        if nc:
            pl.when(at(1))(plan.finish)

    a_spec = pl.BlockSpec((tm, K), lambda i, j: (i, 0))
    b_spec = pl.BlockSpec((tn, K), lambda i, j: (j, 0)) if tb else pl.BlockSpec((K, tn), lambda i, j: (0, j))
    o_spec = pl.BlockSpec((tm, tn), lambda i, j: (i, j))
    out_shape = [jax.ShapeDtypeStruct((M, N), d) for d in out_dtypes]
    if not nc:
        return _pcall(body, name=name, grid=grid, in_specs=[a_spec] + [b_spec] * nb + [o_spec] * nx, out_specs=[o_spec] * no,
                      out_shape=out_shape, compiler_params=_params(("parallel", "parallel")))(a, *bs, *extras)
    return _pcall(body, name=name, grid=grid, in_specs=[a_spec] + [b_spec] * nb + [o_spec] * nx + [_HBM] * nc,
                  out_specs=[o_spec] * no + [_HBM] * nc, out_shape=out_shape + plan_cls.out_shapes(carried),
                  scratch_shapes=plan_cls.sems(nc), compiler_params=_params(("arbitrary", "arbitrary")))(a, *bs, *extras, *carried)


def _swiglu(products, _):
    g, u = products
    return g, u, g * _sigmoid(g) * u


def _swiglu_bwd(products, extras):
    (da,), (gate, up) = products, extras
    gv = gate.astype(F32)
    s = _sigmoid(gv)
    return da * 0.5 * up.astype(F32) * s * (1.0 + gv * (1.0 - s)), da * 0.5 * gv * s


def _row_spec(tr, c):
    return pl.BlockSpec((tr, c), lambda i: (i, 0))


def _full_spec(shape):
    return pl.BlockSpec(shape, lambda i: (0,) * len(shape))


def _acc_rows(ref, val, i):
    @pl.when(i == 0)
    def _():
        ref[...] = val

    @pl.when(i > 0)
    def _():
        ref[...] += val


def rms_fwd(x, g, *, name):
    T, D = x.shape
    tr = _tile(T, 512, 8)

    def body(x_ref, g_ref, h_ref):
        xv = x_ref[...]
        r = lax.rsqrt(jnp.mean(xv * xv, axis=-1, keepdims=True) + RMS_EPS)
        h_ref[...] = (xv * r * g_ref[...]).astype(BF16)

    return _pcall(body, name=name, grid=(T // tr,), in_specs=[_row_spec(tr, D), _full_spec((1, D))],
                  out_specs=_row_spec(tr, D), out_shape=jax.ShapeDtypeStruct((T, D), BF16),
                  compiler_params=_params(("parallel",)))(x, g)


def rms_bwd(dh, x, g, res, *, name):
    T, D = x.shape
    tr = _tile(T, 256, 8)

    def body(dh_ref, x_ref, g_ref, res_ref, dx_ref, dg_ref):
        i = pl.program_id(0)
        xv, dhv = x_ref[...], dh_ref[...].astype(F32)
        r = lax.rsqrt(jnp.mean(xv * xv, axis=-1, keepdims=True) + RMS_EPS)
        xh = xv * r
        dxh = dhv * g_ref[...]
        dx_ref[...] = res_ref[...] + r * (dxh - xh * jnp.mean(dxh * xh, axis=-1, keepdims=True))
        _acc_rows(dg_ref, jnp.sum(dhv * xh, axis=0, keepdims=True), i)

    return _pcall(body, name=name, grid=(T // tr,),
                  in_specs=[_row_spec(tr, D), _row_spec(tr, D), _full_spec((1, D)), _row_spec(tr, D)],
                  out_specs=[_row_spec(tr, D), _full_spec((1, D))],
                  out_shape=[jax.ShapeDtypeStruct((T, D), F32), jax.ShapeDtypeStruct((1, D), F32)],
                  compiler_params=_params(("arbitrary",)))(dh, x, g, res)


def final_loss(x, tgt, g, *, name):
    T, D = x.shape
    tr = _tile(T, 256, 8)

    def body(x_ref, t_ref, g_ref, dx_ref, dg_ref, loss_ref):
        i = pl.program_id(0)
        xv = x_ref[...]
        r = lax.rsqrt(jnp.mean(xv * xv, axis=-1, keepdims=True) + RMS_EPS)
        xh = xv * r
        e = xh * g_ref[...] - t_ref[...]
        part = 0.5 * jnp.sum(jnp.mean(e * e, axis=-1, keepdims=True), axis=0, keepdims=True)
        dy = e * (1.0 / D)
        dxh = dy * g_ref[...]
        dx_ref[...] = r * (dxh - xh * jnp.mean(dxh * xh, axis=-1, keepdims=True))
        _acc_rows(dg_ref, jnp.sum(dy * xh, axis=0, keepdims=True), i)
        _acc_rows(loss_ref, jnp.broadcast_to(part, (1, LANE)), i)

    return _pcall(body, name=name, grid=(T // tr,),
                  in_specs=[_row_spec(tr, D), _row_spec(tr, D), _full_spec((1, D))],
                  out_specs=[_row_spec(tr, D), _full_spec((1, D)), _full_spec((1, LANE))],
                  out_shape=[jax.ShapeDtypeStruct((T, D), F32), jax.ShapeDtypeStruct((1, D), F32),
                             jax.ShapeDtypeStruct((1, LANE), F32)],
                  compiler_params=_params(("arbitrary",)))(x, tgt, g)


def merge_fwd(br, ba, pg, *, name):
    T, D = br.shape
    tr = _tile(T, 256, 8)

    def body(br_ref, ba_ref, pg_ref, o_ref):
        pgv = pg_ref[...]
        o_ref[...] = (_sigmoid(pgv[:, :D]) * br_ref[...] + _sigmoid(pgv[:, D:]) * ba_ref[...]).astype(BF16)

    return _pcall(body, name=name, grid=(T // tr,), in_specs=[_row_spec(tr, D), _row_spec(tr, D), _row_spec(tr, 2 * D)],
                  out_specs=_row_spec(tr, D), out_shape=jax.ShapeDtypeStruct((T, D), BF16),
                  compiler_params=_params(("parallel",)))(br, ba, pg)


def merge_bwd(dm, br, ba, pg, *, name):
    T, D = br.shape
    tr = _tile(T, 256, 8)

    def body(dm_ref, br_ref, ba_ref, pg_ref, dbr_ref, dba_ref, dpg_ref):
        pgv, dmv = pg_ref[...], dm_ref[...]
        sr, sa = _sigmoid(pgv[:, :D]), _sigmoid(pgv[:, D:])
        dbr_ref[...] = (dmv * sr).astype(BF16)
        dba_ref[...] = (dmv * sa).astype(BF16)
        dpg_ref[:, :D] = dmv * br_ref[...] * sr * (1.0 - sr)
        dpg_ref[:, D:] = dmv * ba_ref[...] * sa * (1.0 - sa)

    return _pcall(body, name=name, grid=(T // tr,),
                  in_specs=[_row_spec(tr, D), _row_spec(tr, D), _row_spec(tr, D), _row_spec(tr, 2 * D)],
                  out_specs=[_row_spec(tr, D), _row_spec(tr, D), _row_spec(tr, 2 * D)],
                  out_shape=[jax.ShapeDtypeStruct((T, D), BF16), jax.ShapeDtypeStruct((T, D), BF16),
                             jax.ShapeDtypeStruct((T, 2 * D), F32)],
                  compiler_params=_params(("parallel",)))(dm, br, ba, pg)


def _rwkv_mix(p, prev, mu, w0, a0, k_k, k_a, wlw, wla, wlg, bd):
    pp = p + (prev - p) * mu
    r, k, v = pp[:, 0:RW], pp[:, RW:2 * RW], pp[:, 2 * RW:3 * RW]
    xw, xa, xg = pp[:, 3 * RW:3 * RW + 128], pp[:, 3 * RW + 128:3 * RW + 256], pp[:, 3 * RW + 256:3 * RW + 384]
    th = jnp.tanh(xw)
    z = -(w0 + _bdot(th, wlw))
    e = jnp.exp(-_softplus(z) - 0.5)
    decay = jnp.exp(-e)
    a = _sigmoid(a0 + _bdot(xa, wla))
    sg = _sigmoid(xg)
    kkr = k * k_k
    n = jnp.sqrt(_segsum(kkr * kkr, bd))
    kk = kkr / jnp.maximum(n, 1e-12)
    k2 = k * (1.0 + (a - 1.0) * k_a)
    return dict(r=r, k=k, v=v, xa=xa, th=th, z=z, e=e, decay=decay, a=a, sg=sg, n=n, kk=kk, k2=k2)


def _seg_matrix(n, shift):
    r = lax.shift_right_logical(lax.broadcasted_iota(jnp.int32, (n, n), 0), shift)
    c = lax.shift_right_logical(lax.broadcasted_iota(jnp.int32, (n, n), 1), shift)
    return jnp.where(r == c, 1.0, 0.0).astype(BF16)


def rwkv_pre_fwd(p, pshift, mu, w0, a0, k_k, k_a, wlw, wla, wlg, *, name):
    T = p.shape[0]
    tr = _tile(T, 256, 8)

    def body(p_ref, ps_ref, mu_ref, w0_ref, a0_ref, kk_ref, ka_ref, wlw_ref, wla_ref, wlg_ref,
             r_ref, w_ref, k_ref, v_ref, a_ref, b_ref, g_ref):
        pv, prev = p_ref[...], ps_ref[...]
        m = _rwkv_mix(pv, prev, mu_ref[...], w0_ref[...], a0_ref[...], kk_ref[...], ka_ref[...],
                      wlw_ref[...], wla_ref[...], wlg_ref[...], _seg_matrix(RW, 6))
        r_ref[...] = m['r']
        w_ref[...] = m['decay']
        k_ref[...] = m['k2']
        v_ref[...] = m['v']
        a_ref[...] = -m['kk']
        b_ref[...] = m['kk'] * m['a']
        g_ref[...] = m['sg']

    vec = _row_spec(tr, RW)
    return _pcall(
        body, name=name, grid=(T // tr,),
        in_specs=[_row_spec(tr, RWKV_PAD), _row_spec(tr, RWKV_PAD), _full_spec((1, RWKV_PAD))] + [_full_spec((1, RW))] * 4
        + [_full_spec((128, RW))] * 3,
        out_specs=[vec] * 6 + [_row_spec(tr, 128)],
        out_shape=[jax.ShapeDtypeStruct((T, RW), F32)] * 6 + [jax.ShapeDtypeStruct((T, 128), F32)],
        compiler_params=_params(("parallel",)),
    )(p, pshift, mu, w0, a0, k_k, k_a, wlw, wla, wlg)


def _group_norm(y, bd):
    mean = _segsum(y, bd) * (1.0 / HEAD_DIM)
    yc = y - mean
    rstd = lax.rsqrt(_segsum(yc * yc, bd) * (1.0 / HEAD_DIM) + GN_EPS)
    return yc * rstd, rstd


def rwkv_post_fwd(y, r, k2, v, sg, wlg, ln_w, ln_b, r_k, *, name):
    T = y.shape[0]
    tr = _tile(T, 256, 8)

    def body(y_ref, r_ref, k_ref, v_ref, sg_ref, wlg_ref, lw_ref, lb_ref, rk_ref, o_ref):
        bd = _seg_matrix(RW, 6)
        yn, _ = _group_norm(y_ref[...], bd)
        s = _segsum(r_ref[...] * k_ref[...] * rk_ref[...], bd)
        g = _bdot(sg_ref[...], wlg_ref[...])
        o_ref[...] = ((yn * lw_ref[...] + lb_ref[...] + s * v_ref[...]) * g).astype(BF16)

    vec = _row_spec(tr, RW)
    return _pcall(body, name=name, grid=(T // tr,),
                  in_specs=[vec] * 4 + [_row_spec(tr, 128), _full_spec((128, RW))] + [_full_spec((1, RW))] * 3, out_specs=vec,
                  out_shape=jax.ShapeDtypeStruct((T, RW), BF16), compiler_params=_params(("parallel",)))(
                      y, r, k2, v, sg, wlg, ln_w, ln_b, r_k)


def rwkv_post_bwd(dyr, y, r, k2, v, sg, wlg, ln_w, ln_b, r_k, *, name, carry=None):
    T = y.shape[0]
    tr = _tile(T, 256, 8)
    nt = T // tr
    plan_cls, carried = carry if carry else (None, ())
    nc = len(carried)

    def body(*refs):
        dyr_ref, y_ref, r_ref, k_ref, v_ref, sg_ref, wlg_ref, lw_ref, lb_ref, rk_ref = refs[:10]
        dy_ref, dz_ref, dg_ref, dlw_ref, dlb_ref = refs[10 + nc:15 + nc]
        i = pl.program_id(0)
        if nc:
            plan = plan_cls(refs[10:10 + nc], refs[15 + nc:15 + 2 * nc], refs[15 + 2 * nc:])
            pl.when(i == 0)(plan.start)
        bd = _seg_matrix(RW, 6)
        yn, rstd = _group_norm(y_ref[...], bd)
        s = _segsum(r_ref[...] * k_ref[...] * rk_ref[...], bd)
        dyrv = dyr_ref[...]
        dg_ref[...] = dyrv * (yn * lw_ref[...] + lb_ref[...] + s * v_ref[...])
        dz = dyrv * _bdot(sg_ref[...], wlg_ref[...])
        dz_ref[...] = dz
        dyn = dz * lw_ref[...]
        inv = 1.0 / HEAD_DIM
        dy_ref[...] = rstd * (dyn - _segsum(dyn, bd) * inv - yn * (_segsum(dyn * yn, bd) * inv))
        _acc_rows(dlw_ref, jnp.sum(dz * yn, axis=0, keepdims=True), i)
        _acc_rows(dlb_ref, jnp.sum(dz, axis=0, keepdims=True), i)
        if nc:
            pl.when(i == nt - 1)(plan.finish)

    vec = _row_spec(tr, RW)
    one = _full_spec((1, RW))
    return _pcall(body, name=name, grid=(nt,),
                  in_specs=[vec] * 5 + [_row_spec(tr, 128), _full_spec((128, RW))] + [one] * 3 + [_HBM] * nc,
                  out_specs=[vec] * 3 + [one] * 2 + [_HBM] * nc,
                  out_shape=[jax.ShapeDtypeStruct((T, RW), F32)] * 3 + [jax.ShapeDtypeStruct((1, RW), F32)] * 2
                  + (plan_cls.out_shapes(carried) if nc else []),
                  scratch_shapes=plan_cls.sems(nc) if nc else [],
                  compiler_params=_params(("arbitrary",)))(dyr, y, r, k2, v, sg, wlg, ln_w, ln_b, r_k, *carried)


def rwkv_pre_bwd(p, pshift, dr_w, dw_w, dk_w, dv_w, da_w, db_w, dz, dg, mu, w0, a0, k_k, k_a, r_k, wlw, wla, wlg, *, name):
    T = p.shape[0]
    tr = _tile(T, 256, 8)
    n = T // tr

    def body(p_ref, ps_ref, dr_ref, dw_ref, dk_ref, dv_ref, da_ref, db_ref, dz_ref, dg_ref,
             mu_ref, w0_ref, a0_ref, kk_ref, ka_ref, rk_ref, wlw_ref, wla_ref, wlg_ref,
             dp_ref, dmu_ref, dw0_ref, da0_ref, dkk_ref, dka_ref, drk_ref, dwlw_ref, dwla_ref, dwlg_ref,
             carry, dpp, acc_w, acc_a, acc_g):
        i = pl.program_id(0)

        @pl.when(i == 0)
        def _():
            carry[...] = jnp.zeros_like(carry)

        pv, prev, mu = p_ref[...], ps_ref[...], mu_ref[...]
        bd = _seg_matrix(RW, 6)
        k_k, k_a, r_k = kk_ref[...], ka_ref[...], rk_ref[...]
        m = _rwkv_mix(pv, prev, mu, w0_ref[...], a0_ref[...], k_k, k_a, wlw_ref[...], wla_ref[...], wlg_ref[...], bd)
        r, k, v, a, kk, k2 = m['r'], m['k'], m['v'], m['a'], m['kk'], m['k2']
        dzv, dgv = dz_ref[...], dg_ref[...]
        s = _segsum(r * k2 * r_k, bd)
        ds = _segsum(dzv * v, bd)
        dr = dr_ref[...] + ds * k2 * r_k
        dk2 = dk_ref[...] + ds * r * r_k
        dv = dv_ref[...] + dzv * s
        dbv = db_ref[...]
        dkk = dbv * a - da_ref[...]
        da = dbv * kk + dk2 * k * k_a
        dk = dk2 * (1.0 + (a - 1.0) * k_a)
        nmax = jnp.maximum(m['n'], 1e-12)
        dkkr = jnp.where(m['n'] > 1e-12, dkk - kk * _segsum(dkk * kk, bd), dkk) / nmax
        dk = dk + dkkr * k_k
        dapre = da * a * (1.0 - a)
        dwpre = dw_ref[...] * m['decay'] * (-m['e']) * _sigmoid(m['z'])
        dth = _bdot(dwpre, wlw_ref[...], _NT)
        dxa = _bdot(dapre, wla_ref[...], _NT)
        dsg = _bdot(dgv, wlg_ref[...], _NT)
        dpp[:, 0:RW] = dr
        dpp[:, RW:2 * RW] = dk
        dpp[:, 2 * RW:3 * RW] = dv
        dpp[:, 3 * RW:3 * RW + 128] = dth * (1.0 - m['th'] * m['th'])
        dpp[:, 3 * RW + 128:3 * RW + 256] = dxa
        dpp[:, 3 * RW + 256:3 * RW + 384] = dsg * m['sg'] * (1.0 - m['sg'])
        d = dpp[...]
        zed = d * mu
        last = lax.broadcasted_iota(jnp.int32, pv.shape, 0) == tr - 1
        dp_ref[...] = d * (1.0 - mu) + jnp.where(last, carry[0:1, :], pltpu.roll(zed, tr - 1, 0))
        carry[...] = zed[0:8, :]

        def colsum(x):
            return jnp.sum(x, axis=0, keepdims=True)

        _acc_rows(dmu_ref, colsum(d * (prev - pv)), i)
        _acc_rows(dw0_ref, colsum(dwpre), i)
        _acc_rows(da0_ref, colsum(dapre), i)
        _acc_rows(dkk_ref, colsum(dkkr * k), i)
        _acc_rows(dka_ref, colsum(dk2 * k * (a - 1.0)), i)
        _acc_rows(drk_ref, colsum(ds * r * k2), i)
        _acc_rows(acc_w, _bdot(m['th'], dwpre, _TN), i)
        _acc_rows(acc_a, _bdot(m['xa'], dapre, _TN), i)
        _acc_rows(acc_g, _bdot(m['sg'], dgv, _TN), i)

        @pl.when(i == n - 1)
        def _():
            dwlw_ref[...] = acc_w[...]
            dwla_ref[...] = acc_a[...]
            dwlg_ref[...] = acc_g[...]

    rev = lambda c: pl.BlockSpec((tr, c), lambda i: (n - 1 - i, 0))
    one, lora = _full_spec((1, RW)), _full_spec((128, RW))
    return _pcall(
        body, name=name, grid=(n,),
        in_specs=[rev(RWKV_PAD), rev(RWKV_PAD)] + [rev(RW)] * 8 + [_full_spec((1, RWKV_PAD))] + [one] * 5 + [lora] * 3,
        out_specs=[rev(RWKV_PAD), _full_spec((1, RWKV_PAD))] + [one] * 5 + [lora] * 3,
        out_shape=[jax.ShapeDtypeStruct((T, RWKV_PAD), F32), jax.ShapeDtypeStruct((1, RWKV_PAD), F32)]
        + [jax.ShapeDtypeStruct((1, RW), F32)] * 5 + [jax.ShapeDtypeStruct((128, RW), F32)] * 3,
        scratch_shapes=[pltpu.VMEM((8, RWKV_PAD), F32), pltpu.VMEM((tr, RWKV_PAD), F32)] + [pltpu.VMEM((128, RW), F32)] * 3,
        compiler_params=_params(("arbitrary",)),
    )(p, pshift, dr_w, dw_w, dk_w, dv_w, da_w, db_w, dz, dg, mu, w0, a0, k_k, k_a, r_k, wlw, wla, wlg)


def _qk_norm(x, g, bd):
    r = lax.rsqrt(_segsum(x * x, bd) * (1.0 / HEAD_DIM) + RMS_EPS)
    return x * r * g, r


def _att_mask(i):
    qi = lax.broadcasted_iota(jnp.int32, (BLOCK, 2 * BLOCK), 0)
    kj = lax.broadcasted_iota(jnp.int32, (BLOCK, 2 * BLOCK), 1)
    band = (kj <= qi + BLOCK) & (kj > qi + BLOCK - WINDOW)
    return band & ((kj >= BLOCK) | (i > 0))


_HQK = (((2,), (2,)), ((0,), (0,)))
_HPV = (((2,), (1,)), ((0,), (0,)))
_HTN = (((1,), (1,)), ((0,), (0,)))


def _heads(x, n):
    return jnp.stack([x[:, h * HEAD_DIM:(h + 1) * HEAD_DIM] for h in range(n)])


def _unheads(x3):
    return jnp.concatenate([x3[h] for h in range(x3.shape[0])], axis=1)


def _kv_heads(x):
    x2 = _heads(x, KVW // HEAD_DIM)
    return jnp.concatenate([x2[g:g + 1] for g in range(KVW // HEAD_DIM) for _ in range(ATT_GROUP)], axis=0)


def _sinks3(sk):
    return jnp.stack([sk[0:1, h:h + 1] for h in range(N_HEADS)])


def _att_probs(q3, k3, mask, sink):
    s = _bdot(q3, k3, _HQK) * (HEAD_DIM ** -0.5)
    s = jnp.where(mask[None], s, NEG_BIG)
    m = jnp.maximum(jnp.max(s, axis=-1, keepdims=True), sink)
    pexp = jnp.exp(s - m)
    psink = jnp.exp(sink - m)
    inv = 1.0 / (jnp.sum(pexp, axis=-1, keepdims=True) + psink)
    return pexp * inv, psink * inv


def _att_blocks(n):
    cur = pl.BlockSpec((BLOCK, ATT_COLS), lambda i: (i, 0))
    prev = pl.BlockSpec((BLOCK, ATT_COLS), lambda i: (jnp.maximum(i - 1, 0), 0))
    return cur, prev


def _att_qkv(cur, prev, qn_g, kn_g):
    bq, bk = _seg_matrix(RW, 6), _seg_matrix(KVW, 6)
    qn, rq = _qk_norm(cur[:, 0:RW], qn_g, bq)
    kcur, rkc = _qk_norm(cur[:, RW:RW + KVW], kn_g, bk)
    kprev, _ = _qk_norm(prev[:, RW:RW + KVW], kn_g, bk)
    kc = jnp.concatenate([kprev, kcur], axis=0)
    vc = jnp.concatenate([prev[:, RW + KVW:], cur[:, RW + KVW:]], axis=0)
    return qn, rq, kc, vc, rkc


def att_fwd(pa, qn_g, kn_g, sinks, *, name):
    T = pa.shape[0]
    n = T // BLOCK

    def body(cur_ref, prev_ref, qg_ref, kg_ref, sk_ref, o_ref):
        i = pl.program_id(0)
        qn, _, kc, vc, _ = _att_qkv(cur_ref[...], prev_ref[...], qg_ref[...], kg_ref[...])
        probs, _ = _att_probs(_heads(qn, N_HEADS), _kv_heads(kc), _att_mask(i), _sinks3(sk_ref[...]))
        o_ref[...] = _unheads(_bdot(probs, _kv_heads(vc), _HPV))

    cur, prev = _att_blocks(n)
    return _pcall(body, name=name, grid=(n,),
                  in_specs=[cur, prev, _full_spec((1, RW)), _full_spec((1, KVW)), _full_spec((1, LANE))],
                  out_specs=pl.BlockSpec((BLOCK, RW), lambda i: (i, 0)), out_shape=jax.ShapeDtypeStruct((T, RW), F32),
                  compiler_params=_params(("parallel",)))(pa, pa, qn_g, kn_g, sinks)


def att_bwd(pa, do, qn_g, kn_g, sinks, *, name):
    T = pa.shape[0]
    n = T // BLOCK

    def body(cur_ref, prev_ref, do_ref, qg_ref, kg_ref, sk_ref,
             dq_ref, dko_ref, dkn_ref, dvo_ref, dvn_ref, dqg_ref, dsk_ref):
        i = pl.program_id(0)
        cur = cur_ref[...]
        qn, rq, kc, vc, _ = _att_qkv(cur, prev_ref[...], qg_ref[...], kg_ref[...])
        q3, k3, v3, do3 = _heads(qn, N_HEADS), _kv_heads(kc), _kv_heads(vc), _heads(do_ref[...], N_HEADS)
        probs, psink = _att_probs(q3, k3, _att_mask(i), _sinks3(sk_ref[...]))
        dprobs = _bdot(do3, v3, _HQK)
        delta = jnp.sum(probs * dprobs, axis=-1, keepdims=True)
        ds = probs * (dprobs - delta) * (HEAD_DIM ** -0.5)
        dsink3 = -jnp.sum(psink * delta, axis=1, keepdims=True)
        lane = lax.broadcasted_iota(jnp.int32, (1, LANE), 1)
        dsink = jnp.zeros((1, LANE), F32)
        for h in range(N_HEADS):
            dsink = dsink + jnp.where(lane == h, dsink3[h], 0.0)
        dqn = _unheads(_bdot(ds, k3, _HPV))

        def per_kv_head(x3):
            groups = [sum(x3[g * ATT_GROUP + j] for j in range(ATT_GROUP)) for g in range(KVW // HEAD_DIM)]
            return jnp.concatenate(groups, axis=1)

        dk, dv = per_kv_head(_bdot(ds, q3, _HTN)), per_kv_head(_bdot(probs, do3, _HTN))
        dkn_ref[...], dko_ref[...] = dk[0:BLOCK], dk[BLOCK:]
        dvn_ref[...], dvo_ref[...] = dv[0:BLOCK], dv[BLOCK:]
        qhat = cur[:, 0:RW] * rq
        dqh = dqn * qg_ref[...]
        dq_ref[...] = rq * (dqh - qhat * (_segsum(dqh * qhat, _seg_matrix(RW, 6)) * (1.0 / HEAD_DIM)))
        prod = dqn * qhat
        fold = prod[:, 0:HEAD_DIM]
        for h in range(1, N_HEADS):
            fold = fold + prod[:, h * HEAD_DIM:(h + 1) * HEAD_DIM]
        _acc_rows(dqg_ref, jnp.sum(fold, axis=0, keepdims=True), i)
        _acc_rows(dsk_ref, dsink, i)

    cur, prev = _att_blocks(n)
    kvb = pl.BlockSpec((BLOCK, KVW), lambda i: (i, 0))
    qb = pl.BlockSpec((BLOCK, RW), lambda i: (i, 0))
    return _pcall(body, name=name, grid=(n,),
                  in_specs=[cur, prev, qb, _full_spec((1, RW)), _full_spec((1, KVW)), _full_spec((1, LANE))],
                  out_specs=[qb, kvb, kvb, kvb, kvb, _full_spec((1, HEAD_DIM)), _full_spec((1, LANE))],
                  out_shape=[jax.ShapeDtypeStruct((T, RW), F32)] + [jax.ShapeDtypeStruct((T, KVW), F32)] * 4
                  + [jax.ShapeDtypeStruct((1, HEAD_DIM), F32), jax.ShapeDtypeStruct((1, LANE), F32)],
                  compiler_params=_params(("arbitrary",)))(pa, pa, do, qn_g, kn_g, sinks)


def att_kv_bwd(pa, dq, dko, dkn, dvo, dvn, kn_g, *, name):
    T = pa.shape[0]
    n = T // BLOCK

    def body(pa_ref, dq_ref, dko_ref, dkn_ref, dvo_ref, dvn_ref, kg_ref, dpa_ref, dkg_ref):
        i = pl.program_id(0)
        more = i < n - 1
        dkn_tot = dko_ref[...] + jnp.where(more, dkn_ref[...], 0.0)
        dv_tot = dvo_ref[...] + jnp.where(more, dvn_ref[...], 0.0)
        kraw = pa_ref[:, RW:RW + KVW]
        bk = _seg_matrix(KVW, 6)
        _, rk = _qk_norm(kraw, kg_ref[...], bk)
        khat = kraw * rk
        dkh = dkn_tot * kg_ref[...]
        dpa_ref[:, 0:RW] = dq_ref[...]
        dpa_ref[:, RW:RW + KVW] = rk * (dkh - khat * (_segsum(dkh * khat, bk) * (1.0 / HEAD_DIM)))
        dpa_ref[:, RW + KVW:] = dv_tot
        prod = dkn_tot * khat
        _acc_rows(dkg_ref, jnp.sum(prod[:, 0:HEAD_DIM] + prod[:, HEAD_DIM:], axis=0, keepdims=True), i)

    kvb = pl.BlockSpec((BLOCK, KVW), lambda i: (i, 0))
    nxt = pl.BlockSpec((BLOCK, KVW), lambda i: (jnp.minimum(i + 1, n - 1), 0))
    return _pcall(body, name=name, grid=(n,),
                  in_specs=[pl.BlockSpec((BLOCK, ATT_COLS), lambda i: (i, 0)), pl.BlockSpec((BLOCK, RW), lambda i: (i, 0)),
                            kvb, nxt, kvb, nxt, _full_spec((1, KVW))],
                  out_specs=[pl.BlockSpec((BLOCK, ATT_COLS), lambda i: (i, 0)), _full_spec((1, HEAD_DIM))],
                  out_shape=[jax.ShapeDtypeStruct((T, ATT_COLS), F32), jax.ShapeDtypeStruct((1, HEAD_DIM), F32)],
                  compiler_params=_params(("arbitrary",)))(pa, dq, dko, dkn, dvo, dvn, kn_g)


WKV_CHUNK = 64
WKV_GROUP = 8


def _diag_mask():
    i = lax.broadcasted_iota(jnp.int32, (HEAD_DIM, RW), 0)
    j = lax.broadcasted_iota(jnp.int32, (HEAD_DIM, RW), 1) & (HEAD_DIM - 1)
    return i == j


def _heads_matrix():
    head = jnp.arange(RW // 2) // HEAD_DIM
    bd = (head[:, None] == head[None, :]).astype(BF16)
    return jnp.concatenate([bd, bd], axis=0)


def _headsums(xs, pieces, bd2):
    half = RW // 2
    bd = bd2[:pieces * half]
    rows = []
    for x in xs:
        parts, rest = [], x
        for n in range(pieces):
            p = rest.astype(BF16)
            parts.append(p)
            if n + 1 < pieces:
                rest = rest - p.astype(F32)
        for sl in (slice(0, half), slice(half, RW)):
            rows.append(jnp.concatenate([p[:, sl] for p in parts], axis=1))
    out = lax.dot_general(jnp.concatenate(rows, axis=0), bd, (((1,), (0,)), ((), ())), preferred_element_type=F32)
    return [jnp.concatenate([out[2 * n * HEAD_DIM:(2 * n + 1) * HEAD_DIM], out[(2 * n + 1) * HEAD_DIM:(2 * n + 2) * HEAD_DIM]],
                            axis=1) for n in range(len(xs))]


def _headsum(x):
    low = lax.broadcasted_iota(jnp.int32, (HEAD_DIM, LANE), 1) < HEAD_DIM
    tiles = []
    for c in range(RW // LANE):
        xt = x[:, c * LANE:(c + 1) * LANE]
        s_lo = jnp.sum(jnp.where(low, xt, 0.0), axis=1, keepdims=True)
        s_hi = jnp.sum(jnp.where(low, 0.0, xt), axis=1, keepdims=True)
        tiles.append(jnp.where(low, s_lo, s_hi))
    return jnp.concatenate(tiles, axis=1)


def _cols(rows, diag, bd2, pieces=2):
    return _headsums([jnp.where(diag, r, 0.0) for r in rows], pieces, bd2)


def _row(x, diag):
    return jnp.sum(jnp.where(diag, x, 0.0), axis=0, keepdims=True)


def wkv_fwd(r, w, k, v, a, b, *, name, gather=()):
    T = r.shape[0]
    ch = min(WKV_CHUNK, T)
    ngroups = ch // WKV_GROUP
    nchunks = T // ch
    ng = len(gather)

    def body(*refs):
        r_ref, w_ref, k_ref, v_ref, a_ref, b_ref, bd_ref = refs[:7]
        y_ref, st_ref = refs[7 + ng:9 + ng]
        s_scr = refs[9 + 2 * ng]
        step = pl.program_id(0)
        if ng:
            plan = _GatherPlan(refs[7:7 + ng], refs[9 + ng:9 + 2 * ng], refs[10 + 2 * ng:])
            pl.when(step == 0)(plan.start)
            pl.when(step == nchunks // 2)(plan.relay)

        @pl.when(step == 0)
        def _():
            s_scr[...] = jnp.zeros_like(s_scr)

        diag, bd2 = _diag_mask(), bd_ref[...]

        def group(gi, S):
            t0 = pl.multiple_of(gi * WKV_GROUP, WKV_GROUP)
            rows = pl.ds(t0, WKV_GROUP)
            R, W, K, V, A, B = (ref[rows, :] for ref in (r_ref, w_ref, k_ref, v_ref, a_ref, b_ref))
            vcols = _cols([V[s:s + 1] for s in range(WKV_GROUP)], diag, bd2)
            yrows = []
            for s in range(WKV_GROUP):
                sa = _headsum(S * A[s:s + 1])
                S = S * W[s:s + 1] + sa * B[s:s + 1] + vcols[s] * K[s:s + 1]
                st_ref[t0 + s] = S
                yrows.append(_row(_headsums([S * R[s:s + 1]], 2, bd2)[0], diag))
            y_ref[rows, :] = jnp.concatenate(yrows, axis=0)
            return S

        s_scr[...] = lax.fori_loop(0, ngroups, group, s_scr[...])
        if ng:
            pl.when(step == nchunks - 1)(plan.finish_relayed)

    vec = pl.BlockSpec((ch, RW), lambda c: (c, 0))
    return _pcall(
        body, name=name, grid=(nchunks,), in_specs=[vec] * 6 + [_full_spec((RW, RW // 2))] + [_HBM] * ng,
        out_specs=[vec, pl.BlockSpec((ch, HEAD_DIM, RW), lambda c: (c, 0, 0))] + [_HBM] * ng,
        out_shape=[jax.ShapeDtypeStruct((T, RW), F32), jax.ShapeDtypeStruct((T, HEAD_DIM, RW), F32)] + _gathered_shapes(gather),
        scratch_shapes=[pltpu.VMEM((HEAD_DIM, RW), F32)] + (_GatherPlan.sems(ng) if ng else []),
        compiler_params=_params(("arbitrary",)),
    )(r, w, k, v, a, b, _heads_matrix(), *gather)


def wkv_bwd(r, w, k, v, a, b, dy, states, *, name, exchange=()):
    T = r.shape[0]
    ch = min(WKV_CHUNK, T)
    nchunks = T // ch
    ngroups = ch // WKV_GROUP
    ne = len(exchange)

    def body(*refs):
        r_ref, w_ref, k_ref, v_ref, a_ref, b_ref, dy_ref, st_ref, stp_ref, bd_ref = refs[:10]
        dr_ref, dw_ref, dk_ref, dv_ref, da_ref, db_ref = refs[10 + ne:16 + ne]
        ds_scr = refs[16 + 2 * ne]
        step = pl.program_id(0)
        if ne:
            plan = _ExchangePlan(refs[10:10 + ne], refs[16 + ne:16 + 2 * ne], refs[17 + 2 * ne:])
            pl.when(step == 0)(plan.start)

        @pl.when(step == 0)
        def _():
            ds_scr[...] = jnp.zeros_like(ds_scr)

        has_prev_chunk = step < nchunks - 1
        diag, bd2 = _diag_mask(), bd_ref[...]
        colsum = lambda x: jnp.sum(x, axis=0, keepdims=True)

        def group(gj, dS):
            gi = ngroups - 1 - gj
            t0 = pl.multiple_of(gi * WKV_GROUP, WKV_GROUP)
            rows = pl.ds(t0, WKV_GROUP)
            R, W, K, V, A, B, DY = (ref[rows, :] for ref in (r_ref, w_ref, k_ref, v_ref, a_ref, b_ref, dy_ref))
            before = jnp.where(gi > 0, st_ref[jnp.maximum(t0 - 1, 0)], jnp.where(has_prev_chunk, stp_ref[0], 0.0))
            prev_state = lambda s: st_ref[t0 + s - 1] if s > 0 else before
            steps = range(WKV_GROUP)
            dycols = _cols([DY[s:s + 1] for s in steps], diag, bd2, 1)
            vcols = _cols([V[s:s + 1] for s in steps], diag, bd2, 1)
            sas = _headsums([prev_state(s) * A[s:s + 1] for s in steps], 1, bd2)
            got = [[None] * WKV_GROUP for _ in range(6)]
            for s in reversed(steps):
                Sp = prev_state(s)
                dS = dS + dycols[s] * R[s:s + 1]
                got[0][s] = colsum(st_ref[t0 + s] * dycols[s])
                got[3][s] = _row(_headsums([dS * K[s:s + 1]], 2, bd2)[0], diag)
                got[2][s] = colsum(dS * vcols[s])
                dsa = _headsum(dS * B[s:s + 1])
                got[5][s] = colsum(dS * sas[s])
                got[1][s] = colsum(dS * Sp)
                got[4][s] = colsum(Sp * dsa)
                dS = dS * W[s:s + 1] + dsa * A[s:s + 1]
            for q, ref in enumerate((dr_ref, dw_ref, dk_ref, dv_ref, da_ref, db_ref)):
                ref[rows, :] = jnp.concatenate(got[q], axis=0)
            return dS

        ds_scr[...] = lax.fori_loop(0, ngroups, group, ds_scr[...])
        if ne:
            pl.when(step == nchunks - 1)(plan.finish)

    vec = pl.BlockSpec((ch, RW), lambda c: (nchunks - 1 - c, 0))
    st_spec = pl.BlockSpec((ch, HEAD_DIM, RW), lambda c: (nchunks - 1 - c, 0, 0))
    stp_spec = pl.BlockSpec((1, HEAD_DIM, RW), lambda c: (jnp.maximum((nchunks - 1 - c) * ch - 1, 0), 0, 0))
    return _pcall(
        body, name=name, grid=(nchunks,), in_specs=[vec] * 7 + [st_spec, stp_spec, _full_spec((RW, RW // 2))] + [_HBM] * ne,
        out_specs=[vec] * 6 + [_HBM] * ne,
        out_shape=[jax.ShapeDtypeStruct((T, RW), F32)] * 6 + [jax.ShapeDtypeStruct(e.shape, e.dtype) for e in exchange],
        scratch_shapes=[pltpu.VMEM((HEAD_DIM, RW), F32)] + (_ExchangePlan.sems(ne) if ne else []),
        compiler_params=_params(("arbitrary",)),
    )(r, w, k, v, a, b, dy, states, states, _heads_matrix(), *exchange)


_HBM = pl.BlockSpec(memory_space=pltpu.HBM)
_MESH = pl.DeviceIdType.MESH


def _place():
    x, y, c = lax.axis_index("x"), lax.axis_index("y"), lax.axis_index("c")
    return x, y, c, [(1 - x, y), (x, 1 - y), (1 - x, 1 - y)]


def _remote(src, dst, send_sem, recv_sem, to):
    return pltpu.make_async_remote_copy(src_ref=src, dst_ref=dst, send_sem=send_sem, recv_sem=recv_sem, device_id=to,
                                        device_id_type=_MESH)


def _dma_sems(*counts):
    return [pltpu.SemaphoreType.DMA((n,)) for n in counts]


class _GatherPlan:
    def __init__(self, ins, outs, sems):
        self.ins, self.outs, self.n = ins, outs, len(ins)
        self.ici_send, self.ici_recv, self.d2d_send, self.d2d_recv, self.local_sems = sems
        x, y, c, chips = _place()
        self.c, self.me, self.sibling = c, 2 * x + y, (x, y, 1 - c)
        self.peers = [(2 * qx + qy, (qx, qy, c)) for qx, qy in chips]

    @staticmethod
    def sems(n):
        return _dma_sems(3 * n, 3 * n, 3 * n, 3 * n, n)

    def _half(self, i, which):
        rh = self.ins[i].shape[0] // 2
        return pl.ds(which * rh, rh)

    def _local(self, i):
        return pltpu.make_async_copy(self.ins[i], self.outs[i].at[self.me], self.local_sems.at[i])

    def _send(self, i, j):
        k, mine = 3 * i + j, self._half(i, self.c)
        return _remote(self.ins[i].at[mine], self.outs[i].at[self.me, mine], self.ici_send.at[k], self.ici_recv.at[k],
                       self.peers[j][1])

    def _landed(self, i, j):
        k, piece = 3 * i + j, self.outs[i].at[self.peers[j][0], self._half(i, self.c)]
        return _remote(piece, piece, self.ici_send.at[k], self.ici_recv.at[k], self.peers[j][1])

    def _pass(self, i, j, which):
        k, piece = 3 * i + j, self.outs[i].at[self.peers[j][0], self._half(i, which)]
        return _remote(piece, piece, self.d2d_send.at[k], self.d2d_recv.at[k], self.sibling)

    def _all(self):
        return [(i, j) for i in range(self.n) for j in range(3)]

    def start(self):
        for i in range(self.n):
            self._local(i).start()
        for i, j in self._all():
            self._send(i, j).start()

    def relay(self):
        for i, j in self._all():
            self._landed(i, j).wait_recv()
            self._pass(i, j, self.c).start()

    def finish_relayed(self):
        for i, j in self._all():
            self._pass(i, j, 1 - self.c).wait_recv()
        for i, j in self._all():
            self._send(i, j).wait_send()
            self._pass(i, j, self.c).wait_send()
        for i in range(self.n):
            self._local(i).wait()

    def finish(self):
        self.relay()
        self.finish_relayed()

    @staticmethod
    def out_shapes(shards):
        return _gathered_shapes(shards)


def _gathered_shapes(shards):
    return [jax.ShapeDtypeStruct((N_CHIPS,) + s.shape, s.dtype) for s in shards]


def gather_weights(shards, *, name):
    n = len(shards)

    def body(*refs):
        plan = _GatherPlan(refs[:n], refs[n:2 * n], refs[2 * n:])
        plan.start()
        plan.finish()

    return _pcall(body, name=name, in_specs=[_HBM] * n, out_specs=[_HBM] * n, out_shape=_gathered_shapes(shards),
                  scratch_shapes=_GatherPlan.sems(n), compiler_params=_params())(*shards)


class _SiblingPlan:
    halves = True

    def __init__(self, ins, outs, sems):
        self.ins, self.outs, self.n = ins, outs, len(ins)
        self.send_sems, self.recv_sems = sems
        x, y, c, _ = _place()
        self.c, self.sibling = c, (x, y, 1 - c)

    @staticmethod
    def sems(n):
        return _dma_sems(n, n)

    @classmethod
    def out_shapes(cls, arrays):
        if not cls.halves:
            return [jax.ShapeDtypeStruct(a.shape, a.dtype) for a in arrays]
        return [jax.ShapeDtypeStruct((a.shape[0], a.shape[1] // 2, a.shape[2]), a.dtype) for a in arrays]

    def _copy(self, i):
        src = self.ins[i]
        if self.halves:
            rh = src.shape[1] // 2
            src = src.at[:, pl.ds((1 - self.c) * rh, rh)]
        return _remote(src, self.outs[i], self.send_sems.at[i], self.recv_sems.at[i], self.sibling)

    def start(self):
        for i in range(self.n):
            self._copy(i).start()

    def finish(self):
        for i in range(self.n):
            self._copy(i).wait_recv()
        for i in range(self.n):
            self._copy(i).wait_send()


class _SiblingWhole(_SiblingPlan):
    halves = False


def to_sibling(arrays, take_other_half, *, name):
    n = len(arrays)
    plan_cls = _SiblingPlan if take_other_half else _SiblingWhole

    def body(*refs):
        plan = plan_cls(refs[:n], refs[n:2 * n], refs[2 * n:])
        plan.start()
        plan.finish()

    return _pcall(body, name=name, in_specs=[_HBM] * n, out_specs=[_HBM] * n, out_shape=plan_cls.out_shapes(arrays),
                  scratch_shapes=plan_cls.sems(n), compiler_params=_params())(*arrays)


def exchange_chips(arrays, *, name):
    n = len(arrays)

    def body(*refs):
        plan = _ExchangePlan(refs[:n], refs[n:2 * n], refs[2 * n:])
        plan.start()
        plan.finish()

    return _pcall(body, name=name, in_specs=[_HBM] * n, out_specs=[_HBM] * n,
                  out_shape=[jax.ShapeDtypeStruct(a.shape, a.dtype) for a in arrays],
                  scratch_shapes=_ExchangePlan.sems(n), compiler_params=_params())(*arrays)


class _ExchangePlan:
    def __init__(self, ins, outs, sems):
        self.ins, self.outs, self.n = ins, outs, len(ins)
        self.send_sems, self.recv_sems, self.local_sems = sems
        x, y, c, chips = _place()
        self.me = 2 * x + y
        self.peers = [(2 * qx + qy, (qx, qy, c)) for qx, qy in chips]

    @staticmethod
    def sems(n):
        return _dma_sems(3 * n, 3 * n, n)

    @staticmethod
    def out_shapes(arrays):
        return [jax.ShapeDtypeStruct(a.shape, a.dtype) for a in arrays]

    def _local(self, i):
        return pltpu.make_async_copy(self.ins[i].at[self.me], self.outs[i].at[self.me], self.local_sems.at[i])

    def _send(self, i, j):
        k = 3 * i + j
        return _remote(self.ins[i].at[self.peers[j][0]], self.outs[i].at[self.me], self.send_sems.at[k], self.recv_sems.at[k],
                       self.peers[j][1])

    def _landed(self, i, j):
        k, piece = 3 * i + j, self.outs[i].at[self.peers[j][0]]
        return _remote(piece, piece, self.send_sems.at[k], self.recv_sems.at[k], self.peers[j][1])

    def start(self):
        for i in range(self.n):
            self._local(i).start()
            for j in range(3):
                self._send(i, j).start()

    def finish(self):
        for i in range(self.n):
            for j in range(3):
                self._landed(i, j).wait_recv()
        for i in range(self.n):
            for j in range(3):
                self._send(i, j).wait_send()
            self._local(i).wait()


def _core_index():
    return lax.axis_index("c").astype(jnp.int32).reshape(1)


def pair_sum(g, theirs, wire_dtype, *, name):
    _, R, C = g.shape
    rh = R // 2
    tr = _tile(rh, 256, 16)
    nt = rh // tr

    def body(c_ref, g_ref, t_ref, q_ref, qw_ref):
        q = g_ref[...] + t_ref[...]
        q_ref[...] = q
        qw_ref[...] = q.astype(wire_dtype)

    blk = pl.BlockSpec((1, tr, C), lambda b, i, c_ref: (b, i, 0))
    mine = pl.BlockSpec((1, tr, C), lambda b, i, c_ref: (b, c_ref[0] * nt + i, 0))
    grid_spec = pltpu.PrefetchScalarGridSpec(num_scalar_prefetch=1, grid=(N_CHIPS, nt), in_specs=[mine, blk], out_specs=[blk, blk])
    return _pcall(body, name=name, grid_spec=grid_spec,
                  out_shape=[jax.ShapeDtypeStruct((N_CHIPS, rh, C), F32), jax.ShapeDtypeStruct((N_CHIPS, rh, C), wire_dtype)],
                  compiler_params=_params(("parallel", "parallel")))(_core_index(), g, theirs)


def half_sum(own, landed, *, name):
    _, rh, C = own.shape
    tr = _tile(rh, 256, 16)

    def body(me_ref, own_ref, land_ref, o_ref):
        total = None
        for p in range(N_CHIPS):
            term = jnp.where(me_ref[0] == p, own_ref[p], land_ref[p].astype(F32))
            total = term if total is None else total + term
        o_ref[...] = total

    blk = pl.BlockSpec((N_CHIPS, tr, C), lambda i, me_ref: (0, i, 0))
    grid_spec = pltpu.PrefetchScalarGridSpec(num_scalar_prefetch=1, grid=(rh // tr,), in_specs=[blk, blk],
                                             out_specs=pl.BlockSpec((tr, C), lambda i, me_ref: (i, 0)))
    me = (2 * lax.axis_index("x") + lax.axis_index("y")).astype(jnp.int32).reshape(1)
    return _pcall(body, name=name, grid_spec=grid_spec, out_shape=jax.ShapeDtypeStruct((rh, C), F32),
                  compiler_params=_params(("parallel",)))(me, own, landed)


def adamw(w, m, v, mine, theirs, *, name):
    _, R, C = w.shape
    rh = R // 2
    tr = _tile(rh, 256, 8)
    nt = rh // tr

    def body(c_ref, w_ref, m_ref, v_ref, a_ref, b_ref, g_ref, d_ref, nm_ref, nv_ref):
        is_mine = (pl.program_id(0) // nt) == c_ref[0]
        g = jnp.where(is_mine, a_ref[...], b_ref[...])
        g_ref[...] = g
        nm = ADAM_B1 * m_ref[...] + (1.0 - ADAM_B1) * g
        nv = ADAM_B2 * v_ref[...] + (1.0 - ADAM_B2) * (g * g)
        nm_ref[...] = nm
        nv_ref[...] = nv
        m_hat = nm / (1.0 - ADAM_B1 ** ADAM_STEP)
        v_hat = nv / (1.0 - ADAM_B2 ** ADAM_STEP)
        d_ref[...] = -ADAM_LR * (m_hat / (jnp.sqrt(v_hat) + ADAM_EPS) + ADAM_WD * w_ref[...])

    full = pl.BlockSpec((None, tr, C), lambda i, c_ref: (0, i, 0))
    a_spec = pl.BlockSpec((tr, C), lambda i, c_ref: (jnp.clip(i - c_ref[0] * nt, 0, nt - 1), 0))
    b_spec = pl.BlockSpec((tr, C), lambda i, c_ref: (jnp.clip(i - (1 - c_ref[0]) * nt, 0, nt - 1), 0))
    grid_spec = pltpu.PrefetchScalarGridSpec(num_scalar_prefetch=1, grid=(2 * nt,), in_specs=[full] * 3 + [a_spec, b_spec],
                                             out_specs=[full] * 4)
    return _pcall(body, name=name, grid_spec=grid_spec, out_shape=[jax.ShapeDtypeStruct((1, R, C), F32)] * 4,
                  compiler_params=_params(("arbitrary",)))(_core_index(), w, m, v, mine, theirs)


def _to_blocks(full, axis):
    r, c = full.shape
    if axis == 1:
        return full.reshape(r, N_CHIPS, c // N_CHIPS).transpose(1, 0, 2)
    return full.reshape(N_CHIPS, r // N_CHIPS, c)


def _from_blocks(blocks, axis):
    _, r, c = blocks.shape
    if axis == 1:
        return blocks.transpose(1, 0, 2).reshape(r, N_CHIPS * c)
    return blocks.reshape(N_CHIPS * r, c)


def _ffn_fwd(x, norm, wg_t, wu_t, wd, tag):
    h = rms_fwd(x, norm, name=tag + "_norm")
    gate, up, act = mm_fused(h, [wg_t, wu_t], _swiglu, [BF16] * 3, tb=True, name=tag + "_gate_up")
    out = mm(act, wd, scale=0.5, res=x, name=tag + "_down")
    return out, (h, gate, up, act)


def _ffn_bwd(dout, x, saved, norm, wg_t, wu_t, wd, tag, carry=None, reduce=None):
    h, gate, up, act = saved
    dgate, dup, *carried = mm_fused(dout, [wd], _swiglu_bwd, [BF16] * 2, tb=True, extras=[gate, up], name=tag + "_dact",
                                    carry=carry)
    dwd = mm(act, dout, ta=True, scale=0.5, name=tag + "_dwd")
    dwg_t = mm(dgate, h, ta=True, name=tag + "_dwg")
    dwu_t = mm(dup, h, ta=True, name=tag + "_dwu")
    reduced = None
    if reduce:
        blocks = [_to_blocks(g, 0) for g in (dwg_t, dwu_t, dwd)]
        dh, *from_sibling = mm(dgate, wg_t, name=tag + "_dh_gate", carry=(_SiblingPlan, blocks))
        pair = [pair_sum(g, t, BF16, name="pair_sum_" + n) for g, t, n in zip(blocks, from_sibling, reduce)]
        dh, *landed = mm(dup, wu_t, res=dh, name=tag + "_dh_up", carry=(_ExchangePlan, [q for _, q in pair]))
        reduced = (pair, landed)
    else:
        dh = mm(dgate, wg_t, name=tag + "_dh_gate")
        dh = mm(dup, wu_t, res=dh, name=tag + "_dh_up")
    dx, dnorm = rms_bwd(dh, x, norm, dout, name=tag + "_dnorm")
    return dx, dnorm, dwg_t, dwu_t, dwd, carried, reduced


TRANSPOSED = ('ffn1_w_gate', 'ffn1_w_up', 'ffn2_w_gate', 'ffn2_w_up')
FIRST_WEIGHTS = ['ffn1_w_gate', 'ffn1_w_up', 'ffn1_w_down']
MID_WEIGHTS = ['w_in', 'rwkv_w_lora_up', 'rwkv_a_lora_up', 'rwkv_g_lora_up']
LATE_WEIGHTS = ['w_branch_rwkv', 'w_branch_attn', 'w_out', 'ffn2_w_gate', 'ffn2_w_up', 'ffn2_w_down']


def _pair_sums(names, blocks, tag):
    from_sibling = to_sibling(blocks, True, name=tag + "_grads_to_sibling")
    return [pair_sum(g, t, F32 if n == 'small' else BF16, name="pair_sum_" + n)
            for g, t, n in zip(blocks, from_sibling, names)]


def _step(A):
    x, tgt = A['x'][0], A['loss_target'][0]
    T = x.shape[0]
    w = {n: A[n][0] for n in WEIGHT_NAMES}
    row = lambda a: a.reshape(1, -1)

    axis_of = {n: (0 if n in TRANSPOSED else axis) for n, axis in BIG}
    natural = lambda n, a: jnp.swapaxes(a, 1, 2) if n in TRANSPOSED else a
    shard = lambda n: natural(n, A[n])[0].astype(BF16)
    n1, nmix, n2, nfin = (row(w[n]) for n in ('ffn1_norm', 'mix_norm', 'ffn2_norm', 'final_norm'))
    gathered = gather_weights([shard(n) for n in FIRST_WEIGHTS[:2]], name="gather_weights")
    full = {n: _from_blocks(b, axis_of[n]) for n, b in zip(FIRST_WEIGHTS[:2], gathered)}
    h1 = rms_fwd(x, n1, name="ffn1_norm")
    gate1, up1, act1, down_blocks = mm_fused(h1, [full['ffn1_w_gate'], full['ffn1_w_up']], _swiglu, [BF16] * 3, tb=True,
                                             name="ffn1_gate_up", carry=(_GatherPlan, [shard('ffn1_w_down')]))
    full['ffn1_w_down'] = _from_blocks(down_blocks, axis_of['ffn1_w_down'])
    x1, *gathered = mm(act1, full['ffn1_w_down'], scale=0.5, res=x, name="ffn1_down",
                       carry=(_GatherPlan, [shard(n) for n in MID_WEIGHTS]))
    ffn1 = (h1, gate1, up1, act1)
    full.update({n: _from_blocks(b, axis_of[n]) for n, b in zip(MID_WEIGHTS, gathered)})
    w_in_r = _pad_rwkv_cols(full['w_in'][:, :RWKV_COLS])
    w_in_a = full['w_in'][:, RWKV_COLS:RWKV_COLS + ATT_COLS]
    w_in_g = full['w_in'][:, RWKV_COLS + ATT_COLS:]
    wlw, wla, wlg = (_pad_rows(full[n], 128).astype(F32) for n in ('rwkv_w_lora_up', 'rwkv_a_lora_up', 'rwkv_g_lora_up'))
    mu = _pad_rwkv_cols(row(w['rwkv_mu']))
    w0, a0, k_k, k_a, r_k, ln_w, ln_b = (row(w[n]) for n in ('rwkv_w0', 'rwkv_a0', 'rwkv_k_k', 'rwkv_k_a', 'rwkv_r_k',
                                                               'rwkv_ln_w', 'rwkv_ln_b'))
    qg = jnp.tile(row(w['attn_q_norm']), (1, N_HEADS))
    kg = jnp.tile(row(w['attn_k_norm']), (1, KVW // HEAD_DIM))
    sinks = jnp.pad(row(w['attn_sinks']), ((0, 0), (0, LANE - N_HEADS)))

    h2 = rms_fwd(x1, nmix, name="mix_norm")
    pr = mm(h2, w_in_r, name="proj_rwkv")
    pa = mm(h2, w_in_a, name="proj_att")
    pg = mm(h2, w_in_g, name="proj_gate")
    pr_shift = jnp.pad(pr, ((1, 0), (0, 0)))[:-1]
    r, dec, k2, v, a, b, sg = rwkv_pre_fwd(pr, pr_shift, mu, w0, a0, k_k, k_a, wlw, wla, wlg, name="rwkv_pre")
    y, states, *gathered = wkv_fwd(r, dec, k2, v, a, b, name="wkv_fwd", gather=[shard(n) for n in LATE_WEIGHTS])
    full.update({n: _from_blocks(b, axis_of[n]) for n, b in zip(LATE_WEIGHTS, gathered)})
    yr = rwkv_post_fwd(y, r, k2, v, sg, wlg, ln_w, ln_b, r_k, name="rwkv_post")
    ya = att_fwd(pa, qg, kg, sinks, name="att_fwd")
    br = mm(yr, full['w_branch_rwkv'], name="branch_rwkv")
    ba = mm(ya, full['w_branch_attn'], name="branch_att")
    mg = merge_fwd(br, ba, pg, name="merge")
    x2 = mm(mg, full['w_out'], res=x1, name="mix_out")
    x3, ffn2 = _ffn_fwd(x2, n2, full['ffn2_w_gate'], full['ffn2_w_up'], full['ffn2_w_down'], "ffn2")
    dx3, d_nfin, loss = final_loss(x3, tgt, nfin, name="final_loss")

    G = {'final_norm': d_nfin}
    dx2, G['ffn2_norm'], G['ffn2_w_gate'], G['ffn2_w_up'], G['ffn2_w_down'], _, _ = _ffn_bwd(
        dx3, x2, ffn2, n2, full['ffn2_w_gate'], full['ffn2_w_up'], full['ffn2_w_down'], "ffn2")
    dmg = mm(dx2, full['w_out'], tb=True, name="d_merge")
    G['w_out'] = mm(mg, dx2, ta=True, name="d_w_out")
    dbr, dba, dpg = merge_bwd(dmg, br, ba, pg, name="merge_bwd")
    dyr = mm(dbr, full['w_branch_rwkv'], tb=True, name="d_y_rwkv")
    G['w_branch_rwkv'] = mm(yr, dbr, ta=True, name="d_w_branch_rwkv")
    dya = mm(dba, full['w_branch_attn'], tb=True, name="d_y_att")
    G['w_branch_attn'] = mm(ya, dba, ta=True, name="d_w_branch_att")
    late_blocks = [_to_blocks(G[n], axis_of[n]) for n in LATE_WEIGHTS]
    dy, dz, dg, G['rwkv_ln_w'], G['rwkv_ln_b'], *late_from_sibling = rwkv_post_bwd(
        dyr, y, r, k2, v, sg, wlg, ln_w, ln_b, r_k, name="rwkv_post_bwd", carry=(_SiblingPlan, late_blocks))
    late_pair = [pair_sum(g, t, BF16, name="pair_sum_" + n) for g, t, n in zip(late_blocks, late_from_sibling, LATE_WEIGHTS)]
    res = wkv_bwd(r, dec, k2, v, a, b, dy, states, name="wkv_bwd", exchange=[q for _, q in late_pair])
    wkv_grads, late_landed = res[:6], res[6:]
    (dpr, d_mu, G['rwkv_w0'], G['rwkv_a0'], G['rwkv_k_k'], G['rwkv_k_a'], G['rwkv_r_k'], d_wlw, d_wla, d_wlg) = rwkv_pre_bwd(
        pr, pr_shift, *wkv_grads, dz, dg, mu, w0, a0, k_k, k_a, r_k, wlw, wla, wlg, name="rwkv_pre_bwd")
    G['rwkv_mu'] = _unpad_rwkv_cols(d_mu)
    G['rwkv_w_lora_up'], G['rwkv_a_lora_up'], G['rwkv_g_lora_up'] = d_wlw[:DECAY_LORA], d_wla[:ICLR_LORA], d_wlg[:GATE_LORA]
    dq, dko, dkn, dvo, dvn, G['attn_q_norm'], d_sinks = att_bwd(pa, dya, qg, kg, sinks, name="att_bwd")
    G['attn_sinks'] = d_sinks[:, :N_HEADS]
    dpa, G['attn_k_norm'] = att_kv_bwd(pa, dq, dko, dkn, dvo, dvn, kg, name="att_kv_bwd")
    d_w_in_r = mm(h2, dpr, ta=True, name="d_w_in_rwkv")
    d_w_in_a = mm(h2, dpa, ta=True, name="d_w_in_att")
    d_w_in_g = mm(h2, dpg, ta=True, name="d_w_in_gate")
    G['w_in'] = jnp.concatenate([_unpad_rwkv_cols(d_w_in_r), d_w_in_a, d_w_in_g], axis=1)
    mid_blocks = [_to_blocks(G[n], axis_of[n]) for n in MID_WEIGHTS]
    dh2, *mid_from_sibling = mm(dpr, w_in_r, tb=True, name="d_h2_rwkv", carry=(_SiblingPlan, mid_blocks))
    dh2 = mm(dpa, w_in_a, tb=True, res=dh2, name="d_h2_att")
    dh2 = mm(dpg, w_in_g, tb=True, res=dh2, name="d_h2_gate")
    dx1, G['mix_norm'] = rms_bwd(dh2, x1, nmix, dx2, name="d_mix_norm")
    mid_pair = [pair_sum(g, t, BF16, name="pair_sum_" + n) for g, t, n in zip(mid_blocks, mid_from_sibling, MID_WEIGHTS)]
    dx0, G['ffn1_norm'], _, _, _, mid_landed, (first_pair, first_landed) = _ffn_bwd(
        dx1, x, ffn1, n1, full['ffn1_w_gate'], full['ffn1_w_up'], full['ffn1_w_down'], "ffn1",
        carry=(_ExchangePlan, [q for _, q in mid_pair]), reduce=FIRST_WEIGHTS)

    small_shapes = [(w[n].size,) for n in SMALL] + [(1,)]

    def small_rows(parts):
        vec = jnp.concatenate([p.reshape(-1) for p in parts])
        return jnp.pad(vec, (0, SMALL_ROWS * FLAT_W - vec.shape[0])).reshape(SMALL_ROWS, FLAT_W)

    small = small_rows([G[n] for n in SMALL] + [loss[0, :1]])
    small_pair = _pair_sums(['small'], [jnp.broadcast_to(small[None], (N_CHIPS,) + small.shape)], "small")
    small_landed = exchange_chips([q for _, q in small_pair], name="exchange_small")
    names = FIRST_WEIGHTS + ['small'] + MID_WEIGHTS + LATE_WEIGHTS
    pair = first_pair + small_pair + mid_pair + late_pair
    landed = list(first_landed) + list(small_landed) + list(mid_landed) + list(late_landed)
    halves = [half_sum(own, l, name="half_sum_" + n) for (own, _), l, n in zip(pair, landed, names)]
    other_halves = to_sibling(halves, False, name="halves_to_sibling")

    def local(prefix, n):
        if n != 'small':
            return natural(n, A[prefix + n])
        return small_rows([A[prefix + s] for s in SMALL] + [jnp.zeros((1,), F32)])[None]

    result = {}
    for n, mine, theirs in zip(names, halves, other_halves):
        outs4 = adamw(local('', n), local('m_', n), local('v_', n), mine, theirs, name="adamw_" + n)
        for kind, o in zip(('grad_', 'delta_', 'new_m_', 'new_v_'), outs4):
            if n != 'small':
                result[kind + n] = natural(n, o)
            else:
                for s, part in zip(SMALL + ['loss'], _unpack_vec(o.reshape(-1), small_shapes)):
                    result[kind + s] = part.reshape(A[s].shape) if s != 'loss' else part.reshape(())
    outs = [result['grad_loss'], dx0[None]]
    for kind in ('grad_', 'delta_', 'new_m_', 'new_v_'):
        outs += [result[kind + n] for n in WEIGHT_NAMES]
    return tuple(outs)


def _unpack_vec(vec, shapes):
    out, off = [], 0
    for (n,) in shapes:
        out.append(vec[off:off + n])
        off += n
    return out


def kernel(x, ffn1_norm, ffn1_w_gate, ffn1_w_up, ffn1_w_down, mix_norm, w_in, rwkv_mu, rwkv_w0, rwkv_w_lora_up, rwkv_a0, rwkv_a_lora_up, rwkv_g_lora_up, rwkv_k_k, rwkv_k_a, rwkv_r_k, rwkv_ln_w, rwkv_ln_b, attn_q_norm, attn_k_norm, attn_sinks, w_branch_rwkv, w_branch_attn, w_out, ffn2_norm, ffn2_w_gate, ffn2_w_up, ffn2_w_down, final_norm, loss_target, m_ffn1_norm, m_ffn1_w_gate, m_ffn1_w_up, m_ffn1_w_down, m_mix_norm, m_w_in, m_rwkv_mu, m_rwkv_w0, m_rwkv_w_lora_up, m_rwkv_a0, m_rwkv_a_lora_up, m_rwkv_g_lora_up, m_rwkv_k_k, m_rwkv_k_a, m_rwkv_r_k, m_rwkv_ln_w, m_rwkv_ln_b, m_attn_q_norm, m_attn_k_norm, m_attn_sinks, m_w_branch_rwkv, m_w_branch_attn, m_w_out, m_ffn2_norm, m_ffn2_w_gate, m_ffn2_w_up, m_ffn2_w_down, m_final_norm, v_ffn1_norm, v_ffn1_w_gate, v_ffn1_w_up, v_ffn1_w_down, v_mix_norm, v_w_in, v_rwkv_mu, v_rwkv_w0, v_rwkv_w_lora_up, v_rwkv_a0, v_rwkv_a_lora_up, v_rwkv_g_lora_up, v_rwkv_k_k, v_rwkv_k_a, v_rwkv_r_k, v_rwkv_ln_w, v_rwkv_ln_b, v_attn_q_norm, v_attn_k_norm, v_attn_sinks, v_w_branch_rwkv, v_w_branch_attn, v_w_out, v_ffn2_norm, v_ffn2_w_gate, v_ffn2_w_up, v_ffn2_w_down, v_final_norm):
    return _step(dict(locals()))
```

```python
import functools

import jax
import jax.numpy as jnp
from jax import lax
from jax.experimental import pallas as pl
from jax.experimental.pallas import tpu as pltpu

F32 = jnp.float32
BF16 = jnp.bfloat16

D_MODEL = 1024
D_FF = 2816
HEAD_DIM = 64
N_HEADS = 8
RW = 512
KVW = 128
ATT_GROUP = 4
WINDOW = 128
BLOCK = 128
DECAY_LORA, ICLR_LORA, GATE_LORA = 32, 32, 96
RWKV_COLS = 3 * RW + DECAY_LORA + ICLR_LORA + GATE_LORA
ATT_COLS = RW + 2 * KVW
GATE_COLS = 2 * D_MODEL
RWKV_PAD = 3 * RW + 3 * 128
RMS_EPS = 1e-6
GN_EPS = 64e-5
N_CHIPS = 4
LANE = 128
FLAT_W = 1024
SMALL_ROWS = 32
NEG_BIG = -1e30

ADAM_LR, ADAM_B1, ADAM_B2, ADAM_EPS, ADAM_WD, ADAM_STEP = 0.001, 0.9, 0.999, 1e-08, 0.01, 10

VMEM_LIMIT = 56 * 1024 * 1024

WEIGHT_NAMES = ['ffn1_norm', 'ffn1_w_gate', 'ffn1_w_up', 'ffn1_w_down', 'mix_norm', 'w_in', 'rwkv_mu', 'rwkv_w0',
                'rwkv_w_lora_up', 'rwkv_a0', 'rwkv_a_lora_up', 'rwkv_g_lora_up', 'rwkv_k_k', 'rwkv_k_a', 'rwkv_r_k',
                'rwkv_ln_w', 'rwkv_ln_b', 'attn_q_norm', 'attn_k_norm', 'attn_sinks', 'w_branch_rwkv',
                'w_branch_attn', 'w_out', 'ffn2_norm', 'ffn2_w_gate', 'ffn2_w_up', 'ffn2_w_down', 'final_norm']
BIG = [('ffn1_w_gate', 1), ('ffn1_w_up', 1), ('ffn1_w_down', 0), ('w_in', 1), ('rwkv_w_lora_up', 1),
       ('rwkv_a_lora_up', 1), ('rwkv_g_lora_up', 1), ('w_branch_rwkv', 1), ('w_branch_attn', 1), ('w_out', 0),
       ('ffn2_w_gate', 1), ('ffn2_w_up', 1), ('ffn2_w_down', 0)]
SMALL = ['ffn1_norm', 'mix_norm', 'rwkv_mu', 'rwkv_w0', 'rwkv_a0', 'rwkv_k_k', 'rwkv_k_a', 'rwkv_r_k', 'rwkv_ln_w',
         'rwkv_ln_b', 'attn_q_norm', 'attn_k_norm', 'attn_sinks', 'ffn2_norm', 'final_norm']


def _pcall(body, **kw):
    return pl.pallas_call(body, **kw)


def _params(sem=None, **kw):
    if sem is not None:
        kw['dimension_semantics'] = sem
    return pltpu.CompilerParams(vmem_limit_bytes=VMEM_LIMIT, **kw)


def _tile(n, cap, mult):
    best = None
    for t in range(mult, min(n, cap) + 1, mult):
        if n % t == 0:
            best = t
    return best or n


def _sigmoid(z):
    return 1.0 / (1.0 + jnp.exp(-z))


def _softplus(z):
    return jnp.maximum(z, 0.0) + jnp.log(1.0 + jnp.exp(-jnp.abs(z)))


def _bdot(a, b, dims=(((1,), (0,)), ((), ()))):
    return lax.dot_general(a.astype(BF16), b.astype(BF16), dims, preferred_element_type=F32)


_NT = (((1,), (1,)), ((), ()))
_TN = (((0,), (0,)), ((), ()))


def _segsum(x, bd):
    hi = x.astype(BF16)
    r1 = x - hi.astype(F32)
    mid = r1.astype(BF16)
    lo = (r1 - mid.astype(F32)).astype(BF16)
    dot = functools.partial(lax.dot_general, dimension_numbers=(((1,), (0,)), ((), ())), preferred_element_type=F32)
    return dot(hi, bd) + dot(mid, bd) + dot(lo, bd)


_LORA_EDGES = (3 * RW, 3 * RW + DECAY_LORA, 3 * RW + DECAY_LORA + ICLR_LORA, RWKV_COLS)


def _pad_rwkv_cols(x):
    parts = [x[..., :3 * RW]]
    for lo, hi in zip(_LORA_EDGES[:-1], _LORA_EDGES[1:]):
        parts.append(jnp.pad(x[..., lo:hi], [(0, 0)] * (x.ndim - 1) + [(0, 128 - (hi - lo))]))
    return jnp.concatenate(parts, axis=-1)


def _unpad_rwkv_cols(x):
    parts = [x[..., :3 * RW]]
    for j, (lo, hi) in enumerate(zip(_LORA_EDGES[:-1], _LORA_EDGES[1:])):
        parts.append(x[..., 3 * RW + 128 * j:3 * RW + 128 * j + (hi - lo)])
    return jnp.concatenate(parts, axis=-1)


def _pad_rows(x, rows):
    return jnp.pad(x, [(0, rows - x.shape[0])] + [(0, 0)] * (x.ndim - 1))


def mm(a, b, *, name, ta=False, tb=False, scale=None, res=None, out_dtype=F32, carry=None):
    M, K = (a.shape[1], a.shape[0]) if ta else a.shape
    N = b.shape[0] if tb else b.shape[1]
    assert (b.shape[1] if tb else b.shape[0]) == K
    tm, tn, tk = _tile(M, 1408 if ta else 512, 128), _tile(N, 1408, 128), _tile(K, 1408, 128)
    nk = K // tk
    grid = (M // tm, N // tn, nk)
    dims = (((0 if ta else 1,), (1 if tb else 0,)), ((), ()))
    plan_cls, carried = carry if carry else (None, ())
    nc, nin = len(carried), 2 + (res is not None)

    def body(*refs):
        a_ref, b_ref = refs[:2]
        r_ref = refs[2] if res is not None else None
        o_ref, acc_ref = refs[nin + nc], refs[nin + 2 * nc + 1]
        k = pl.program_id(2)
        if nc:
            plan = plan_cls(refs[nin:nin + nc], refs[nin + nc + 1:nin + 2 * nc + 1], refs[nin + 2 * nc + 2:])
            at = lambda which: functools.reduce(jnp.logical_and, [pl.program_id(d) == (0 if which == 0 else grid[d] - 1)
                                                                 for d in range(3)])
            pl.when(at(0))(plan.start)
        part = _bdot(a_ref[...], b_ref[...], dims)

        @pl.when(k == 0)
        def _():
            acc_ref[...] = part

        @pl.when(k > 0)
        def _():
            acc_ref[...] += part

        @pl.when(k == nk - 1)
        def _():
            o = acc_ref[...]
            if scale is not None:
                o = o * scale
            if r_ref is not None:
                o = o + r_ref[...].astype(F32)
            o_ref[...] = o.astype(out_dtype)

        if nc:
            pl.when(at(1))(plan.finish)

    a_spec = pl.BlockSpec((tk, tm), lambda i, j, k: (k, i)) if ta else pl.BlockSpec((tm, tk), lambda i, j, k: (i, k))
    b_spec = pl.BlockSpec((tn, tk), lambda i, j, k: (j, k)) if tb else pl.BlockSpec((tk, tn), lambda i, j, k: (k, j))
    o_spec = pl.BlockSpec((tm, tn), lambda i, j, k: (i, j))
    in_specs = [a_spec, b_spec] + ([o_spec] if res is not None else [])
    args = (a, b) + ((res,) if res is not None else ())
    out_shape = jax.ShapeDtypeStruct((M, N), out_dtype)
    if not nc:
        return _pcall(
            body, name=name, grid=grid, in_specs=in_specs, out_specs=o_spec, out_shape=out_shape,
            scratch_shapes=[pltpu.VMEM((tm, tn), F32)], compiler_params=_params(("parallel", "parallel", "arbitrary")),
        )(*args)
    return _pcall(
        body, name=name, grid=grid, in_specs=in_specs + [_HBM] * nc, out_specs=[o_spec] + [_HBM] * nc,
        out_shape=[out_shape] + plan_cls.out_shapes(carried), scratch_shapes=[pltpu.VMEM((tm, tn), F32)] + plan_cls.sems(nc),
        compiler_params=_params(("arbitrary", "arbitrary", "arbitrary")),
    )(*args, *carried)


def mm_fused(a, bs, finish, out_dtypes, *, name, tb=False, extras=(), carry=None):
    M, K = a.shape
    N = bs[0].shape[0] if tb else bs[0].shape[1]
    tm, tn = _tile(M, 512, 128), _tile(N, 1408, 128)
    grid = (M // tm, N // tn)
    dims = (((1,), (1 if tb else 0,)), ((), ()))
    plan_cls, carried = carry if carry else (None, ())
    nc, nb, nx, no = len(carried), len(bs), len(extras), len(out_dtypes)
    nin = 1 + nb + nx

    def body(*refs):
        a_ref, b_refs, x_refs = refs[0], refs[1:1 + nb], refs[1 + nb:nin]
        o_refs = refs[nin + nc:nin + nc + no]
        if nc:
            plan = plan_cls(refs[nin:nin + nc], refs[nin + nc + no:nin + 2 * nc + no], refs[nin + 2 * nc + no:])
            at = lambda which: jnp.logical_and(*[pl.program_id(d) == (0 if which == 0 else grid[d] - 1) for d in range(2)])
            pl.when(at(0))(plan.start)
        av = a_ref[...]
        outs = finish([_bdot(av, b_ref[...], dims) for b_ref in b_refs], [x_ref[...] for x_ref in x_refs])
        for o_ref, o in zip(o_refs, outs):
            o_ref[...] = o.astype(o_ref.dtype)
        if nc:
            pl.when(at(1))(plan.finish)

    a_spec = pl.BlockSpec((tm, K), lambda i, j: (i, 0))
    b_spec = pl.BlockSpec((tn, K), lambda i, j: (j, 0)) if tb else pl.BlockSpec((K, tn), lambda i, j: (0, j))
    o_spec = pl.BlockSpec((tm, tn), lambda i, j: (i, j))
    out_shape = [jax.ShapeDtypeStruct((M, N), d) for d in out_dtypes]
    if not nc:
        return _pcall(body, name=name, grid=grid, in_specs=[a_spec] + [b_spec] * nb + [o_spec] * nx, out_specs=[o_spec] * no,
                      out_shape=out_shape, compiler_params=_params(("parallel", "parallel")))(a, *bs, *extras)
    return _pcall(body, name=name, grid=grid, in_specs=[a_spec] + [b_spec] * nb + [o_spec] * nx + [_HBM] * nc,
                  out_specs=[o_spec] * no + [_HBM] * nc, out_shape=out_shape + plan_cls.out_shapes(carried),
                  scratch_shapes=plan_cls.sems(nc), compiler_params=_params(("arbitrary", "arbitrary")))(a, *bs, *extras, *carried)


def _swiglu(products, _):
    g, u = products
    return g, u, g * _sigmoid(g) * u


def _swiglu_bwd(products, extras):
    (da,), (gate, up) = products, extras
    gv = gate.astype(F32)
    s = _sigmoid(gv)
    return da * 0.5 * up.astype(F32) * s * (1.0 + gv * (1.0 - s)), da * 0.5 * gv * s


def _row_spec(tr, c):
    return pl.BlockSpec((tr, c), lambda i: (i, 0))


def _full_spec(shape):
    return pl.BlockSpec(shape, lambda i: (0,) * len(shape))


def _acc_rows(ref, val, i):
    @pl.when(i == 0)
    def _():
        ref[...] = val

    @pl.when(i > 0)
    def _():
        ref[...] += val


def rms_fwd(x, g, *, name):
    T, D = x.shape
    tr = _tile(T, 512, 8)

    def body(x_ref, g_ref, h_ref):
        xv = x_ref[...]
        r = lax.rsqrt(jnp.mean(xv * xv, axis=-1, keepdims=True) + RMS_EPS)
        h_ref[...] = (xv * r * g_ref[...]).astype(BF16)

    return _pcall(body, name=name, grid=(T // tr,), in_specs=[_row_spec(tr, D), _full_spec((1, D))],
                  out_specs=_row_spec(tr, D), out_shape=jax.ShapeDtypeStruct((T, D), BF16),
                  compiler_params=_params(("parallel",)))(x, g)


def rms_bwd(dh, x, g, res, *, name):
    T, D = x.shape
    tr = _tile(T, 256, 8)

    def body(dh_ref, x_ref, g_ref, res_ref, dx_ref, dg_ref):
        i = pl.program_id(0)
        xv, dhv = x_ref[...], dh_ref[...].astype(F32)
        r = lax.rsqrt(jnp.mean(xv * xv, axis=-1, keepdims=True) + RMS_EPS)
        xh = xv * r
        dxh = dhv * g_ref[...]
        dx_ref[...] = res_ref[...] + r * (dxh - xh * jnp.mean(dxh * xh, axis=-1, keepdims=True))
        _acc_rows(dg_ref, jnp.sum(dhv * xh, axis=0, keepdims=True), i)

    return _pcall(body, name=name, grid=(T // tr,),
                  in_specs=[_row_spec(tr, D), _row_spec(tr, D), _full_spec((1, D)), _row_spec(tr, D)],
                  out_specs=[_row_spec(tr, D), _full_spec((1, D))],
                  out_shape=[jax.ShapeDtypeStruct((T, D), F32), jax.ShapeDtypeStruct((1, D), F32)],
                  compiler_params=_params(("arbitrary",)))(dh, x, g, res)


def final_loss(x, tgt, g, *, name):
    T, D = x.shape
    tr = _tile(T, 256, 8)

    def body(x_ref, t_ref, g_ref, dx_ref, dg_ref, loss_ref):
        i = pl.program_id(0)
        xv = x_ref[...]
        r = lax.rsqrt(jnp.mean(xv * xv, axis=-1, keepdims=True) + RMS_EPS)
        xh = xv * r
        e = xh * g_ref[...] - t_ref[...]
        part = 0.5 * jnp.sum(jnp.mean(e * e, axis=-1, keepdims=True), axis=0, keepdims=True)
        dy = e * (1.0 / D)
        dxh = dy * g_ref[...]
        dx_ref[...] = r * (dxh - xh * jnp.mean(dxh * xh, axis=-1, keepdims=True))
        _acc_rows(dg_ref, jnp.sum(dy * xh, axis=0, keepdims=True), i)
        _acc_rows(loss_ref, jnp.broadcast_to(part, (1, LANE)), i)

    return _pcall(body, name=name, grid=(T // tr,),
                  in_specs=[_row_spec(tr, D), _row_spec(tr, D), _full_spec((1, D))],
                  out_specs=[_row_spec(tr, D), _full_spec((1, D)), _full_spec((1, LANE))],
                  out_shape=[jax.ShapeDtypeStruct((T, D), F32), jax.ShapeDtypeStruct((1, D), F32),
                             jax.ShapeDtypeStruct((1, LANE), F32)],
                  compiler_params=_params(("arbitrary",)))(x, tgt, g)


def merge_fwd(br, ba, pg, *, name):
    T, D = br.shape
    tr = _tile(T, 256, 8)

    def body(br_ref, ba_ref, pg_ref, o_ref):
        pgv = pg_ref[...]
        o_ref[...] = (_sigmoid(pgv[:, :D]) * br_ref[...] + _sigmoid(pgv[:, D:]) * ba_ref[...]).astype(BF16)

    return _pcall(body, name=name, grid=(T // tr,), in_specs=[_row_spec(tr, D), _row_spec(tr, D), _row_spec(tr, 2 * D)],
                  out_specs=_row_spec(tr, D), out_shape=jax.ShapeDtypeStruct((T, D), BF16),
                  compiler_params=_params(("parallel",)))(br, ba, pg)


def merge_bwd(dm, br, ba, pg, *, name):
    T, D = br.shape
    tr = _tile(T, 256, 8)

    def body(dm_ref, br_ref, ba_ref, pg_ref, dbr_ref, dba_ref, dpg_ref):
        pgv, dmv = pg_ref[...], dm_ref[...]
        sr, sa = _sigmoid(pgv[:, :D]), _sigmoid(pgv[:, D:])
        dbr_ref[...] = (dmv * sr).astype(BF16)
        dba_ref[...] = (dmv * sa).astype(BF16)
        dpg_ref[:, :D] = dmv * br_ref[...] * sr * (1.0 - sr)
        dpg_ref[:, D:] = dmv * ba_ref[...] * sa * (1.0 - sa)

    return _pcall(body, name=name, grid=(T // tr,),
                  in_specs=[_row_spec(tr, D), _row_spec(tr, D), _row_spec(tr, D), _row_spec(tr, 2 * D)],
                  out_specs=[_row_spec(tr, D), _row_spec(tr, D), _row_spec(tr, 2 * D)],
                  out_shape=[jax.ShapeDtypeStruct((T, D), BF16), jax.ShapeDtypeStruct((T, D), BF16),
                             jax.ShapeDtypeStruct((T, 2 * D), F32)],
                  compiler_params=_params(("parallel",)))(dm, br, ba, pg)


def _rwkv_mix(p, prev, mu, w0, a0, k_k, k_a, wlw, wla, wlg, bd):
    pp = p + (prev - p) * mu
    r, k, v = pp[:, 0:RW], pp[:, RW:2 * RW], pp[:, 2 * RW:3 * RW]
    xw, xa, xg = pp[:, 3 * RW:3 * RW + 128], pp[:, 3 * RW + 128:3 * RW + 256], pp[:, 3 * RW + 256:3 * RW + 384]
    th = jnp.tanh(xw)
    z = -(w0 + _bdot(th, wlw))
    e = jnp.exp(-_softplus(z) - 0.5)
    decay = jnp.exp(-e)
    a = _sigmoid(a0 + _bdot(xa, wla))
    sg = _sigmoid(xg)
    kkr = k * k_k
    n = jnp.sqrt(_segsum(kkr * kkr, bd))
    kk = kkr / jnp.maximum(n, 1e-12)
    k2 = k * (1.0 + (a - 1.0) * k_a)
    return dict(r=r, k=k, v=v, xa=xa, th=th, z=z, e=e, decay=decay, a=a, sg=sg, n=n, kk=kk, k2=k2)


def _seg_matrix(n, shift):
    r = lax.shift_right_logical(lax.broadcasted_iota(jnp.int32, (n, n), 0), shift)
    c = lax.shift_right_logical(lax.broadcasted_iota(jnp.int32, (n, n), 1), shift)
    return jnp.where(r == c, 1.0, 0.0).astype(BF16)


def rwkv_pre_fwd(p, pshift, mu, w0, a0, k_k, k_a, wlw, wla, wlg, *, name):
    T = p.shape[0]
    tr = _tile(T, 256, 8)

    def body(p_ref, ps_ref, mu_ref, w0_ref, a0_ref, kk_ref, ka_ref, wlw_ref, wla_ref, wlg_ref,
             r_ref, w_ref, k_ref, v_ref, a_ref, b_ref, g_ref):
        pv, prev = p_ref[...], ps_ref[...]
        m = _rwkv_mix(pv, prev, mu_ref[...], w0_ref[...], a0_ref[...], kk_ref[...], ka_ref[...],
                      wlw_ref[...], wla_ref[...], wlg_ref[...], _seg_matrix(RW, 6))
        r_ref[...] = m['r']
        w_ref[...] = m['decay']
        k_ref[...] = m['k2']
        v_ref[...] = m['v']
        a_ref[...] = -m['kk']
        b_ref[...] = m['kk'] * m['a']
        g_ref[...] = m['sg']

    vec = _row_spec(tr, RW)
    return _pcall(
        body, name=name, grid=(T // tr,),
        in_specs=[_row_spec(tr, RWKV_PAD), _row_spec(tr, RWKV_PAD), _full_spec((1, RWKV_PAD))] + [_full_spec((1, RW))] * 4
        + [_full_spec((128, RW))] * 3,
        out_specs=[vec] * 6 + [_row_spec(tr, 128)],
        out_shape=[jax.ShapeDtypeStruct((T, RW), F32)] * 6 + [jax.ShapeDtypeStruct((T, 128), F32)],
        compiler_params=_params(("parallel",)),
    )(p, pshift, mu, w0, a0, k_k, k_a, wlw, wla, wlg)


def _group_norm(y, bd):
    mean = _segsum(y, bd) * (1.0 / HEAD_DIM)
    yc = y - mean
    rstd = lax.rsqrt(_segsum(yc * yc, bd) * (1.0 / HEAD_DIM) + GN_EPS)
    return yc * rstd, rstd


def rwkv_post_fwd(y, r, k2, v, sg, wlg, ln_w, ln_b, r_k, *, name):
    T = y.shape[0]
    tr = _tile(T, 256, 8)

    def body(y_ref, r_ref, k_ref, v_ref, sg_ref, wlg_ref, lw_ref, lb_ref, rk_ref, o_ref):
        bd = _seg_matrix(RW, 6)
        yn, _ = _group_norm(y_ref[...], bd)
        s = _segsum(r_ref[...] * k_ref[...] * rk_ref[...], bd)
        g = _bdot(sg_ref[...], wlg_ref[...])
        o_ref[...] = ((yn * lw_ref[...] + lb_ref[...] + s * v_ref[...]) * g).astype(BF16)

    vec = _row_spec(tr, RW)
    return _pcall(body, name=name, grid=(T // tr,),
                  in_specs=[vec] * 4 + [_row_spec(tr, 128), _full_spec((128, RW))] + [_full_spec((1, RW))] * 3, out_specs=vec,
                  out_shape=jax.ShapeDtypeStruct((T, RW), BF16), compiler_params=_params(("parallel",)))(
                      y, r, k2, v, sg, wlg, ln_w, ln_b, r_k)


def rwkv_post_bwd(dyr, y, r, k2, v, sg, wlg, ln_w, ln_b, r_k, *, name, carry=None):
    T = y.shape[0]
    tr = _tile(T, 256, 8)
    nt = T // tr
    plan_cls, carried = carry if carry else (None, ())
    nc = len(carried)

    def body(*refs):
        dyr_ref, y_ref, r_ref, k_ref, v_ref, sg_ref, wlg_ref, lw_ref, lb_ref, rk_ref = refs[:10]
        dy_ref, dz_ref, dg_ref, dlw_ref, dlb_ref = refs[10 + nc:15 + nc]
        i = pl.program_id(0)
        if nc:
            plan = plan_cls(refs[10:10 + nc], refs[15 + nc:15 + 2 * nc], refs[15 + 2 * nc:])
            pl.when(i == 0)(plan.start)
        bd = _seg_matrix(RW, 6)
        yn, rstd = _group_norm(y_ref[...], bd)
        s = _segsum(r_ref[...] * k_ref[...] * rk_ref[...], bd)
        dyrv = dyr_ref[...]
        dg_ref[...] = dyrv * (yn * lw_ref[...] + lb_ref[...] + s * v_ref[...])
        dz = dyrv * _bdot(sg_ref[...], wlg_ref[...])
        dz_ref[...] = dz
        dyn = dz * lw_ref[...]
        inv = 1.0 / HEAD_DIM
        dy_ref[...] = rstd * (dyn - _segsum(dyn, bd) * inv - yn * (_segsum(dyn * yn, bd) * inv))
        _acc_rows(dlw_ref, jnp.sum(dz * yn, axis=0, keepdims=True), i)
        _acc_rows(dlb_ref, jnp.sum(dz, axis=0, keepdims=True), i)
        if nc:
            pl.when(i == nt - 1)(plan.finish)

    vec = _row_spec(tr, RW)
    one = _full_spec((1, RW))
    return _pcall(body, name=name, grid=(nt,),
                  in_specs=[vec] * 5 + [_row_spec(tr, 128), _full_spec((128, RW))] + [one] * 3 + [_HBM] * nc,
                  out_specs=[vec] * 3 + [one] * 2 + [_HBM] * nc,
                  out_shape=[jax.ShapeDtypeStruct((T, RW), F32)] * 3 + [jax.ShapeDtypeStruct((1, RW), F32)] * 2
                  + (plan_cls.out_shapes(carried) if nc else []),
                  scratch_shapes=plan_cls.sems(nc) if nc else [],
                  compiler_params=_params(("arbitrary",)))(dyr, y, r, k2, v, sg, wlg, ln_w, ln_b, r_k, *carried)


def rwkv_pre_bwd(p, pshift, dr_w, dw_w, dk_w, dv_w, da_w, db_w, dz, dg, mu, w0, a0, k_k, k_a, r_k, wlw, wla, wlg, *, name):
    T = p.shape[0]
    tr = _tile(T, 256, 8)
    n = T // tr

    def body(p_ref, ps_ref, dr_ref, dw_ref, dk_ref, dv_ref, da_ref, db_ref, dz_ref, dg_ref,
             mu_ref, w0_ref, a0_ref, kk_ref, ka_ref, rk_ref, wlw_ref, wla_ref, wlg_ref,
             dp_ref, dmu_ref, dw0_ref, da0_ref, dkk_ref, dka_ref, drk_ref, dwlw_ref, dwla_ref, dwlg_ref,
             carry, dpp, acc_w, acc_a, acc_g):
        i = pl.program_id(0)

        @pl.when(i == 0)
        def _():
            carry[...] = jnp.zeros_like(carry)

        pv, prev, mu = p_ref[...], ps_ref[...], mu_ref[...]
        bd = _seg_matrix(RW, 6)
        k_k, k_a, r_k = kk_ref[...], ka_ref[...], rk_ref[...]
        m = _rwkv_mix(pv, prev, mu, w0_ref[...], a0_ref[...], k_k, k_a, wlw_ref[...], wla_ref[...], wlg_ref[...], bd)
        r, k, v, a, kk, k2 = m['r'], m['k'], m['v'], m['a'], m['kk'], m['k2']
        dzv, dgv = dz_ref[...], dg_ref[...]
        s = _segsum(r * k2 * r_k, bd)
        ds = _segsum(dzv * v, bd)
        dr = dr_ref[...] + ds * k2 * r_k
        dk2 = dk_ref[...] + ds * r * r_k
        dv = dv_ref[...] + dzv * s
        dbv = db_ref[...]
        dkk = dbv * a - da_ref[...]
        da = dbv * kk + dk2 * k * k_a
        dk = dk2 * (1.0 + (a - 1.0) * k_a)
        nmax = jnp.maximum(m['n'], 1e-12)
        dkkr = jnp.where(m['n'] > 1e-12, dkk - kk * _segsum(dkk * kk, bd), dkk) / nmax
        dk = dk + dkkr * k_k
        dapre = da * a * (1.0 - a)
        dwpre = dw_ref[...] * m['decay'] * (-m['e']) * _sigmoid(m['z'])
        dth = _bdot(dwpre, wlw_ref[...], _NT)
        dxa = _bdot(dapre, wla_ref[...], _NT)
        dsg = _bdot(dgv, wlg_ref[...], _NT)
        dpp[:, 0:RW] = dr
        dpp[:, RW:2 * RW] = dk
        dpp[:, 2 * RW:3 * RW] = dv
        dpp[:, 3 * RW:3 * RW + 128] = dth * (1.0 - m['th'] * m['th'])
        dpp[:, 3 * RW + 128:3 * RW + 256] = dxa
        dpp[:, 3 * RW + 256:3 * RW + 384] = dsg * m['sg'] * (1.0 - m['sg'])
        d = dpp[...]
        zed = d * mu
        last = lax.broadcasted_iota(jnp.int32, pv.shape, 0) == tr - 1
        dp_ref[...] = d * (1.0 - mu) + jnp.where(last, carry[0:1, :], pltpu.roll(zed, tr - 1, 0))
        carry[...] = zed[0:8, :]

        def colsum(x):
            return jnp.sum(x, axis=0, keepdims=True)

        _acc_rows(dmu_ref, colsum(d * (prev - pv)), i)
        _acc_rows(dw0_ref, colsum(dwpre), i)
        _acc_rows(da0_ref, colsum(dapre), i)
        _acc_rows(dkk_ref, colsum(dkkr * k), i)
        _acc_rows(dka_ref, colsum(dk2 * k * (a - 1.0)), i)
        _acc_rows(drk_ref, colsum(ds * r * k2), i)
        _acc_rows(acc_w, _bdot(m['th'], dwpre, _TN), i)
        _acc_rows(acc_a, _bdot(m['xa'], dapre, _TN), i)
        _acc_rows(acc_g, _bdot(m['sg'], dgv, _TN), i)

        @pl.when(i == n - 1)
        def _():
            dwlw_ref[...] = acc_w[...]
            dwla_ref[...] = acc_a[...]
            dwlg_ref[...] = acc_g[...]

    rev = lambda c: pl.BlockSpec((tr, c), lambda i: (n - 1 - i, 0))
    one, lora = _full_spec((1, RW)), _full_spec((128, RW))
    return _pcall(
        body, name=name, grid=(n,),
        in_specs=[rev(RWKV_PAD), rev(RWKV_PAD)] + [rev(RW)] * 8 + [_full_spec((1, RWKV_PAD))] + [one] * 5 + [lora] * 3,
        out_specs=[rev(RWKV_PAD), _full_spec((1, RWKV_PAD))] + [one] * 5 + [lora] * 3,
        out_shape=[jax.ShapeDtypeStruct((T, RWKV_PAD), F32), jax.ShapeDtypeStruct((1, RWKV_PAD), F32)]
        + [jax.ShapeDtypeStruct((1, RW), F32)] * 5 + [jax.ShapeDtypeStruct((128, RW), F32)] * 3,
        scratch_shapes=[pltpu.VMEM((8, RWKV_PAD), F32), pltpu.VMEM((tr, RWKV_PAD), F32)] + [pltpu.VMEM((128, RW), F32)] * 3,
        compiler_params=_params(("arbitrary",)),
    )(p, pshift, dr_w, dw_w, dk_w, dv_w, da_w, db_w, dz, dg, mu, w0, a0, k_k, k_a, r_k, wlw, wla, wlg)


def _qk_norm(x, g, bd):
    r = lax.rsqrt(_segsum(x * x, bd) * (1.0 / HEAD_DIM) + RMS_EPS)
    return x * r * g, r


def _att_mask(i):
    qi = lax.broadcasted_iota(jnp.int32, (BLOCK, 2 * BLOCK), 0)
    kj = lax.broadcasted_iota(jnp.int32, (BLOCK, 2 * BLOCK), 1)
    band = (kj <= qi + BLOCK) & (kj > qi + BLOCK - WINDOW)
    return band & ((kj >= BLOCK) | (i > 0))


_HQK = (((2,), (2,)), ((0,), (0,)))
_HPV = (((2,), (1,)), ((0,), (0,)))
_HTN = (((1,), (1,)), ((0,), (0,)))


def _heads(x, n):
    return jnp.stack([x[:, h * HEAD_DIM:(h + 1) * HEAD_DIM] for h in range(n)])


def _unheads(x3):
    return jnp.concatenate([x3[h] for h in range(x3.shape[0])], axis=1)


def _kv_heads(x):
    x2 = _heads(x, KVW // HEAD_DIM)
    return jnp.concatenate([x2[g:g + 1] for g in range(KVW // HEAD_DIM) for _ in range(ATT_GROUP)], axis=0)


def _sinks3(sk):
    return jnp.stack([sk[0:1, h:h + 1] for h in range(N_HEADS)])


def _att_probs(q3, k3, mask, sink):
    s = _bdot(q3, k3, _HQK) * (HEAD_DIM ** -0.5)
    s = jnp.where(mask[None], s, NEG_BIG)
    m = jnp.maximum(jnp.max(s, axis=-1, keepdims=True), sink)
    pexp = jnp.exp(s - m)
    psink = jnp.exp(sink - m)
    inv = 1.0 / (jnp.sum(pexp, axis=-1, keepdims=True) + psink)
    return pexp * inv, psink * inv


def _att_blocks(n):
    cur = pl.BlockSpec((BLOCK, ATT_COLS), lambda i: (i, 0))
    prev = pl.BlockSpec((BLOCK, ATT_COLS), lambda i: (jnp.maximum(i - 1, 0), 0))
    return cur, prev


def _att_qkv(cur, prev, qn_g, kn_g):
    bq, bk = _seg_matrix(RW, 6), _seg_matrix(KVW, 6)
    qn, rq = _qk_norm(cur[:, 0:RW], qn_g, bq)
    kcur, rkc = _qk_norm(cur[:, RW:RW + KVW], kn_g, bk)
    kprev, _ = _qk_norm(prev[:, RW:RW + KVW], kn_g, bk)
    kc = jnp.concatenate([kprev, kcur], axis=0)
    vc = jnp.concatenate([prev[:, RW + KVW:], cur[:, RW + KVW:]], axis=0)
    return qn, rq, kc, vc, rkc


def att_fwd(pa, qn_g, kn_g, sinks, *, name):
    T = pa.shape[0]
    n = T // BLOCK

    def body(cur_ref, prev_ref, qg_ref, kg_ref, sk_ref, o_ref):
        i = pl.program_id(0)
        qn, _, kc, vc, _ = _att_qkv(cur_ref[...], prev_ref[...], qg_ref[...], kg_ref[...])
        probs, _ = _att_probs(_heads(qn, N_HEADS), _kv_heads(kc), _att_mask(i), _sinks3(sk_ref[...]))
        o_ref[...] = _unheads(_bdot(probs, _kv_heads(vc), _HPV))

    cur, prev = _att_blocks(n)
    return _pcall(body, name=name, grid=(n,),
                  in_specs=[cur, prev, _full_spec((1, RW)), _full_spec((1, KVW)), _full_spec((1, LANE))],
                  out_specs=pl.BlockSpec((BLOCK, RW), lambda i: (i, 0)), out_shape=jax.ShapeDtypeStruct((T, RW), F32),
                  compiler_params=_params(("parallel",)))(pa, pa, qn_g, kn_g, sinks)


def att_bwd(pa, do, qn_g, kn_g, sinks, *, name):
    T = pa.shape[0]
    n = T // BLOCK

    def body(cur_ref, prev_ref, do_ref, qg_ref, kg_ref, sk_ref,
             dq_ref, dko_ref, dkn_ref, dvo_ref, dvn_ref, dqg_ref, dsk_ref):
        i = pl.program_id(0)
        cur = cur_ref[...]
        qn, rq, kc, vc, _ = _att_qkv(cur, prev_ref[...], qg_ref[...], kg_ref[...])
        q3, k3, v3, do3 = _heads(qn, N_HEADS), _kv_heads(kc), _kv_heads(vc), _heads(do_ref[...], N_HEADS)
        probs, psink = _att_probs(q3, k3, _att_mask(i), _sinks3(sk_ref[...]))
        dprobs = _bdot(do3, v3, _HQK)
        delta = jnp.sum(probs * dprobs, axis=-1, keepdims=True)
        ds = probs * (dprobs - delta) * (HEAD_DIM ** -0.5)
        dsink3 = -jnp.sum(psink * delta, axis=1, keepdims=True)
        lane = lax.broadcasted_iota(jnp.int32, (1, LANE), 1)
        dsink = jnp.zeros((1, LANE), F32)
        for h in range(N_HEADS):
            dsink = dsink + jnp.where(lane == h, dsink3[h], 0.0)
        dqn = _unheads(_bdot(ds, k3, _HPV))

        def per_kv_head(x3):
            groups = [sum(x3[g * ATT_GROUP + j] for j in range(ATT_GROUP)) for g in range(KVW // HEAD_DIM)]
            return jnp.concatenate(groups, axis=1)

        dk, dv = per_kv_head(_bdot(ds, q3, _HTN)), per_kv_head(_bdot(probs, do3, _HTN))
        dkn_ref[...], dko_ref[...] = dk[0:BLOCK], dk[BLOCK:]
        dvn_ref[...], dvo_ref[...] = dv[0:BLOCK], dv[BLOCK:]
        qhat = cur[:, 0:RW] * rq
        dqh = dqn * qg_ref[...]
        dq_ref[...] = rq * (dqh - qhat * (_segsum(dqh * qhat, _seg_matrix(RW, 6)) * (1.0 / HEAD_DIM)))
        prod = dqn * qhat
        fold = prod[:, 0:HEAD_DIM]
        for h in range(1, N_HEADS):
            fold = fold + prod[:, h * HEAD_DIM:(h + 1) * HEAD_DIM]
        _acc_rows(dqg_ref, jnp.sum(fold, axis=0, keepdims=True), i)
        _acc_rows(dsk_ref, dsink, i)

    cur, prev = _att_blocks(n)
    kvb = pl.BlockSpec((BLOCK, KVW), lambda i: (i, 0))
    qb = pl.BlockSpec((BLOCK, RW), lambda i: (i, 0))
    return _pcall(body, name=name, grid=(n,),
                  in_specs=[cur, prev, qb, _full_spec((1, RW)), _full_spec((1, KVW)), _full_spec((1, LANE))],
                  out_specs=[qb, kvb, kvb, kvb, kvb, _full_spec((1, HEAD_DIM)), _full_spec((1, LANE))],
                  out_shape=[jax.ShapeDtypeStruct((T, RW), F32)] + [jax.ShapeDtypeStruct((T, KVW), F32)] * 4
                  + [jax.ShapeDtypeStruct((1, HEAD_DIM), F32), jax.ShapeDtypeStruct((1, LANE), F32)],
                  compiler_params=_params(("arbitrary",)))(pa, pa, do, qn_g, kn_g, sinks)


def att_kv_bwd(pa, dq, dko, dkn, dvo, dvn, kn_g, *, name):
    T = pa.shape[0]
    n = T // BLOCK

    def body(pa_ref, dq_ref, dko_ref, dkn_ref, dvo_ref, dvn_ref, kg_ref, dpa_ref, dkg_ref):
        i = pl.program_id(0)
        more = i < n - 1
        dkn_tot = dko_ref[...] + jnp.where(more, dkn_ref[...], 0.0)
        dv_tot = dvo_ref[...] + jnp.where(more, dvn_ref[...], 0.0)
        kraw = pa_ref[:, RW:RW + KVW]
        bk = _seg_matrix(KVW, 6)
        _, rk = _qk_norm(kraw, kg_ref[...], bk)
        khat = kraw * rk
        dkh = dkn_tot * kg_ref[...]
        dpa_ref[:, 0:RW] = dq_ref[...]
        dpa_ref[:, RW:RW + KVW] = rk * (dkh - khat * (_segsum(dkh * khat, bk) * (1.0 / HEAD_DIM)))
        dpa_ref[:, RW + KVW:] = dv_tot
        prod = dkn_tot * khat
        _acc_rows(dkg_ref, jnp.sum(prod[:, 0:HEAD_DIM] + prod[:, HEAD_DIM:], axis=0, keepdims=True), i)

    kvb = pl.BlockSpec((BLOCK, KVW), lambda i: (i, 0))
    nxt = pl.BlockSpec((BLOCK, KVW), lambda i: (jnp.minimum(i + 1, n - 1), 0))
    return _pcall(body, name=name, grid=(n,),
                  in_specs=[pl.BlockSpec((BLOCK, ATT_COLS), lambda i: (i, 0)), pl.BlockSpec((BLOCK, RW), lambda i: (i, 0)),
                            kvb, nxt, kvb, nxt, _full_spec((1, KVW))],
                  out_specs=[pl.BlockSpec((BLOCK, ATT_COLS), lambda i: (i, 0)), _full_spec((1, HEAD_DIM))],
                  out_shape=[jax.ShapeDtypeStruct((T, ATT_COLS), F32), jax.ShapeDtypeStruct((1, HEAD_DIM), F32)],
                  compiler_params=_params(("arbitrary",)))(pa, dq, dko, dkn, dvo, dvn, kn_g)


WKV_CHUNK = 64
WKV_GROUP = 8


def _diag_mask():
    i = lax.broadcasted_iota(jnp.int32, (HEAD_DIM, RW), 0)
    j = lax.broadcasted_iota(jnp.int32, (HEAD_DIM, RW), 1) & (HEAD_DIM - 1)
    return i == j


def _heads_matrix():
    head = jnp.arange(RW // 2) // HEAD_DIM
    bd = (head[:, None] == head[None, :]).astype(BF16)
    return jnp.concatenate([bd, bd], axis=0)


def _headsums(xs, pieces, bd2):
    half = RW // 2
    bd = bd2[:pieces * half]
    rows = []
    for x in xs:
        parts, rest = [], x
        for n in range(pieces):
            p = rest.astype(BF16)
            parts.append(p)
            if n + 1 < pieces:
                rest = rest - p.astype(F32)
        for sl in (slice(0, half), slice(half, RW)):
            rows.append(jnp.concatenate([p[:, sl] for p in parts], axis=1))
    out = lax.dot_general(jnp.concatenate(rows, axis=0), bd, (((1,), (0,)), ((), ())), preferred_element_type=F32)
    return [jnp.concatenate([out[2 * n * HEAD_DIM:(2 * n + 1) * HEAD_DIM], out[(2 * n + 1) * HEAD_DIM:(2 * n + 2) * HEAD_DIM]],
                            axis=1) for n in range(len(xs))]


def _headsum(x):
    low = lax.broadcasted_iota(jnp.int32, (HEAD_DIM, LANE), 1) < HEAD_DIM
    tiles = []
    for c in range(RW // LANE):
        xt = x[:, c * LANE:(c + 1) * LANE]
        s_lo = jnp.sum(jnp.where(low, xt, 0.0), axis=1, keepdims=True)
        s_hi = jnp.sum(jnp.where(low, 0.0, xt), axis=1, keepdims=True)
        tiles.append(jnp.where(low, s_lo, s_hi))
    return jnp.concatenate(tiles, axis=1)


def _cols(rows, diag, bd2, pieces=2):
    return _headsums([jnp.where(diag, r, 0.0) for r in rows], pieces, bd2)


def _row(x, diag):
    return jnp.sum(jnp.where(diag, x, 0.0), axis=0, keepdims=True)


def wkv_fwd(r, w, k, v, a, b, *, name, gather=()):
    T = r.shape[0]
    ch = min(WKV_CHUNK, T)
    ngroups = ch // WKV_GROUP
    nchunks = T // ch
    ng = len(gather)

    def body(*refs):
        r_ref, w_ref, k_ref, v_ref, a_ref, b_ref, bd_ref = refs[:7]
        y_ref, st_ref = refs[7 + ng:9 + ng]
        s_scr = refs[9 + 2 * ng]
        step = pl.program_id(0)
        if ng:
            plan = _GatherPlan(refs[7:7 + ng], refs[9 + ng:9 + 2 * ng], refs[10 + 2 * ng:])
            pl.when(step == 0)(plan.start)
            pl.when(step == nchunks // 2)(plan.relay)

        @pl.when(step == 0)
        def _():
            s_scr[...] = jnp.zeros_like(s_scr)

        diag, bd2 = _diag_mask(), bd_ref[...]

        def group(gi, S):
            t0 = pl.multiple_of(gi * WKV_GROUP, WKV_GROUP)
            rows = pl.ds(t0, WKV_GROUP)
            R, W, K, V, A, B = (ref[rows, :] for ref in (r_ref, w_ref, k_ref, v_ref, a_ref, b_ref))
            vcols = _cols([V[s:s + 1] for s in range(WKV_GROUP)], diag, bd2, 1)
            yrows = []
            for s in range(WKV_GROUP):
                sa = _headsum(S * A[s:s + 1])
                S = S * W[s:s + 1] + sa * B[s:s + 1] + vcols[s] * K[s:s + 1]
                st_ref[t0 + s] = S
                yrows.append(_row(_headsums([S * R[s:s + 1]], 1, bd2)[0], diag))
            y_ref[rows, :] = jnp.concatenate(yrows, axis=0)
            return S

        s_scr[...] = lax.fori_loop(0, ngroups, group, s_scr[...])
        if ng:
            pl.when(step == nchunks - 1)(plan.finish_relayed)

    vec = pl.BlockSpec((ch, RW), lambda c: (c, 0))
    return _pcall(
        body, name=name, grid=(nchunks,), in_specs=[vec] * 6 + [_full_spec((RW, RW // 2))] + [_HBM] * ng,
        out_specs=[vec, pl.BlockSpec((ch, HEAD_DIM, RW), lambda c: (c, 0, 0))] + [_HBM] * ng,
        out_shape=[jax.ShapeDtypeStruct((T, RW), F32), jax.ShapeDtypeStruct((T, HEAD_DIM, RW), F32)] + _gathered_shapes(gather),
        scratch_shapes=[pltpu.VMEM((HEAD_DIM, RW), F32)] + (_GatherPlan.sems(ng) if ng else []),
        compiler_params=_params(("arbitrary",)),
    )(r, w, k, v, a, b, _heads_matrix(), *gather)


def wkv_bwd(r, w, k, v, a, b, dy, states, *, name, exchange=()):
    T = r.shape[0]
    ch = min(WKV_CHUNK, T)
    nchunks = T // ch
    ngroups = ch // WKV_GROUP
    ne = len(exchange)

    def body(*refs):
        r_ref, w_ref, k_ref, v_ref, a_ref, b_ref, dy_ref, st_ref, stp_ref, bd_ref = refs[:10]
        dr_ref, dw_ref, dk_ref, dv_ref, da_ref, db_ref = refs[10 + ne:16 + ne]
        ds_scr = refs[16 + 2 * ne]
        step = pl.program_id(0)
        if ne:
            plan = _ExchangePlan(refs[10:10 + ne], refs[16 + ne:16 + 2 * ne], refs[17 + 2 * ne:])
            pl.when(step == 0)(plan.start)

        @pl.when(step == 0)
        def _():
            ds_scr[...] = jnp.zeros_like(ds_scr)

        has_prev_chunk = step < nchunks - 1
        diag, bd2 = _diag_mask(), bd_ref[...]
        colsum = lambda x: jnp.sum(x, axis=0, keepdims=True)

        def group(gj, dS):
            gi = ngroups - 1 - gj
            t0 = pl.multiple_of(gi * WKV_GROUP, WKV_GROUP)
            rows = pl.ds(t0, WKV_GROUP)
            R, W, K, V, A, B, DY = (ref[rows, :] for ref in (r_ref, w_ref, k_ref, v_ref, a_ref, b_ref, dy_ref))
            before = jnp.where(gi > 0, st_ref[jnp.maximum(t0 - 1, 0)], jnp.where(has_prev_chunk, stp_ref[0], 0.0))
            prev_state = lambda s: st_ref[t0 + s - 1] if s > 0 else before
            steps = range(WKV_GROUP)
            dycols = _cols([DY[s:s + 1] for s in steps], diag, bd2, 1)
            vcols = _cols([V[s:s + 1] for s in steps], diag, bd2, 1)
            sas = _headsums([prev_state(s) * A[s:s + 1] for s in steps], 1, bd2)
            got = [[None] * WKV_GROUP for _ in range(6)]
            for s in reversed(steps):
                Sp = prev_state(s)
                dS = dS + dycols[s] * R[s:s + 1]
                got[0][s] = colsum(st_ref[t0 + s] * dycols[s])
                got[3][s] = _row(_headsums([dS * K[s:s + 1]], 1, bd2)[0], diag)
                got[2][s] = colsum(dS * vcols[s])
                dsa = _headsum(dS * B[s:s + 1])
                got[5][s] = colsum(dS * sas[s])
                got[1][s] = colsum(dS * Sp)
                got[4][s] = colsum(Sp * dsa)
                dS = dS * W[s:s + 1] + dsa * A[s:s + 1]
            for q, ref in enumerate((dr_ref, dw_ref, dk_ref, dv_ref, da_ref, db_ref)):
                ref[rows, :] = jnp.concatenate(got[q], axis=0)
            return dS

        ds_scr[...] = lax.fori_loop(0, ngroups, group, ds_scr[...])
        if ne:
            pl.when(step == nchunks - 1)(plan.finish)

    vec = pl.BlockSpec((ch, RW), lambda c: (nchunks - 1 - c, 0))
    st_spec = pl.BlockSpec((ch, HEAD_DIM, RW), lambda c: (nchunks - 1 - c, 0, 0))
    stp_spec = pl.BlockSpec((1, HEAD_DIM, RW), lambda c: (jnp.maximum((nchunks - 1 - c) * ch - 1, 0), 0, 0))
    return _pcall(
        body, name=name, grid=(nchunks,), in_specs=[vec] * 7 + [st_spec, stp_spec, _full_spec((RW, RW // 2))] + [_HBM] * ne,
        out_specs=[vec] * 6 + [_HBM] * ne,
        out_shape=[jax.ShapeDtypeStruct((T, RW), F32)] * 6 + [jax.ShapeDtypeStruct(e.shape, e.dtype) for e in exchange],
        scratch_shapes=[pltpu.VMEM((HEAD_DIM, RW), F32)] + (_ExchangePlan.sems(ne) if ne else []),
        compiler_params=_params(("arbitrary",)),
    )(r, w, k, v, a, b, dy, states, states, _heads_matrix(), *exchange)


_HBM = pl.BlockSpec(memory_space=pltpu.HBM)
_MESH = pl.DeviceIdType.MESH


def _place():
    x, y, c = lax.axis_index("x"), lax.axis_index("y"), lax.axis_index("c")
    return x, y, c, [(1 - x, y), (x, 1 - y), (1 - x, 1 - y)]


def _remote(src, dst, send_sem, recv_sem, to):
    return pltpu.make_async_remote_copy(src_ref=src, dst_ref=dst, send_sem=send_sem, recv_sem=recv_sem, device_id=to,
                                        device_id_type=_MESH)


def _dma_sems(*counts):
    return [pltpu.SemaphoreType.DMA((n,)) for n in counts]


class _GatherPlan:
    def __init__(self, ins, outs, sems):
        self.ins, self.outs, self.n = ins, outs, len(ins)
        self.ici_send, self.ici_recv, self.d2d_send, self.d2d_recv, self.local_sems = sems
        x, y, c, chips = _place()
        self.c, self.me, self.sibling = c, 2 * x + y, (x, y, 1 - c)
        self.peers = [(2 * qx + qy, (qx, qy, c)) for qx, qy in chips]

    @staticmethod
    def sems(n):
        return _dma_sems(3 * n, 3 * n, 3 * n, 3 * n, n)

    def _half(self, i, which):
        rh = self.ins[i].shape[0] // 2
        return pl.ds(which * rh, rh)

    def _local(self, i):
        return pltpu.make_async_copy(self.ins[i], self.outs[i].at[self.me], self.local_sems.at[i])

    def _send(self, i, j):
        k, mine = 3 * i + j, self._half(i, self.c)
        return _remote(self.ins[i].at[mine], self.outs[i].at[self.me, mine], self.ici_send.at[k], self.ici_recv.at[k],
                       self.peers[j][1])

    def _landed(self, i, j):
        k, piece = 3 * i + j, self.outs[i].at[self.peers[j][0], self._half(i, self.c)]
        return _remote(piece, piece, self.ici_send.at[k], self.ici_recv.at[k], self.peers[j][1])

    def _pass(self, i, j, which):
        k, piece = 3 * i + j, self.outs[i].at[self.peers[j][0], self._half(i, which)]
        return _remote(piece, piece, self.d2d_send.at[k], self.d2d_recv.at[k], self.sibling)

    def _all(self):
        return [(i, j) for i in range(self.n) for j in range(3)]

    def start(self):
        for i in range(self.n):
            self._local(i).start()
        for i, j in self._all():
            self._send(i, j).start()

    def relay(self):
        for i, j in self._all():
            self._landed(i, j).wait_recv()
            self._pass(i, j, self.c).start()

    def finish_relayed(self):
        for i, j in self._all():
            self._pass(i, j, 1 - self.c).wait_recv()
        for i, j in self._all():
            self._send(i, j).wait_send()
            self._pass(i, j, self.c).wait_send()
        for i in range(self.n):
            self._local(i).wait()

    def finish(self):
        self.relay()
        self.finish_relayed()

    @staticmethod
    def out_shapes(shards):
        return _gathered_shapes(shards)


def _gathered_shapes(shards):
    return [jax.ShapeDtypeStruct((N_CHIPS,) + s.shape, s.dtype) for s in shards]


def gather_weights(shards, *, name):
    n = len(shards)

    def body(*refs):
        plan = _GatherPlan(refs[:n], refs[n:2 * n], refs[2 * n:])
        plan.start()
        plan.finish()

    return _pcall(body, name=name, in_specs=[_HBM] * n, out_specs=[_HBM] * n, out_shape=_gathered_shapes(shards),
                  scratch_shapes=_GatherPlan.sems(n), compiler_params=_params())(*shards)


class _SiblingPlan:
    halves = True

    def __init__(self, ins, outs, sems):
        self.ins, self.outs, self.n = ins, outs, len(ins)
        self.send_sems, self.recv_sems = sems
        x, y, c, _ = _place()
        self.c, self.sibling = c, (x, y, 1 - c)

    @staticmethod
    def sems(n):
        return _dma_sems(n, n)

    @classmethod
    def out_shapes(cls, arrays):
        if not cls.halves:
            return [jax.ShapeDtypeStruct(a.shape, a.dtype) for a in arrays]
        return [jax.ShapeDtypeStruct((a.shape[0], a.shape[1] // 2, a.shape[2]), a.dtype) for a in arrays]

    def _copy(self, i):
        src = self.ins[i]
        if self.halves:
            rh = src.shape[1] // 2
            src = src.at[:, pl.ds((1 - self.c) * rh, rh)]
        return _remote(src, self.outs[i], self.send_sems.at[i], self.recv_sems.at[i], self.sibling)

    def start(self):
        for i in range(self.n):
            self._copy(i).start()

    def finish(self):
        for i in range(self.n):
            self._copy(i).wait_recv()
        for i in range(self.n):
            self._copy(i).wait_send()


class _SiblingWhole(_SiblingPlan):
    halves = False


def to_sibling(arrays, take_other_half, *, name):
    n = len(arrays)
    plan_cls = _SiblingPlan if take_other_half else _SiblingWhole

    def body(*refs):
        plan = plan_cls(refs[:n], refs[n:2 * n], refs[2 * n:])
        plan.start()
        plan.finish()

    return _pcall(body, name=name, in_specs=[_HBM] * n, out_specs=[_HBM] * n, out_shape=plan_cls.out_shapes(arrays),
                  scratch_shapes=plan_cls.sems(n), compiler_params=_params())(*arrays)


def exchange_chips(arrays, *, name):
    n = len(arrays)

    def body(*refs):
        plan = _ExchangePlan(refs[:n], refs[n:2 * n], refs[2 * n:])
        plan.start()
        plan.finish()

    return _pcall(body, name=name, in_specs=[_HBM] * n, out_specs=[_HBM] * n,
                  out_shape=[jax.ShapeDtypeStruct(a.shape, a.dtype) for a in arrays],
                  scratch_shapes=_ExchangePlan.sems(n), compiler_params=_params())(*arrays)


class _ExchangePlan:
    def __init__(self, ins, outs, sems):
        self.ins, self.outs, self.n = ins, outs, len(ins)
        self.send_sems, self.recv_sems, self.local_sems = sems
        x, y, c, chips = _place()
        self.me = 2 * x + y
        self.peers = [(2 * qx + qy, (qx, qy, c)) for qx, qy in chips]

    @staticmethod
    def sems(n):
        return _dma_sems(3 * n, 3 * n, n)

    @staticmethod
    def out_shapes(arrays):
        return [jax.ShapeDtypeStruct(a.shape, a.dtype) for a in arrays]

    def _local(self, i):
        return pltpu.make_async_copy(self.ins[i].at[self.me], self.outs[i].at[self.me], self.local_sems.at[i])

    def _send(self, i, j):
        k = 3 * i + j
        return _remote(self.ins[i].at[self.peers[j][0]], self.outs[i].at[self.me], self.send_sems.at[k], self.recv_sems.at[k],
                       self.peers[j][1])

    def _landed(self, i, j):
        k, piece = 3 * i + j, self.outs[i].at[self.peers[j][0]]
        return _remote(piece, piece, self.send_sems.at[k], self.recv_sems.at[k], self.peers[j][1])

    def start(self):
        for i in range(self.n):
            self._local(i).start()
            for j in range(3):
                self._send(i, j).start()

    def finish(self):
        for i in range(self.n):
            for j in range(3):
                self._landed(i, j).wait_recv()
        for i in range(self.n):
            for j in range(3):
                self._send(i, j).wait_send()
            self._local(i).wait()


def _core_index():
    return lax.axis_index("c").astype(jnp.int32).reshape(1)


def pair_sum(g, theirs, wire_dtype, *, name):
    _, R, C = g.shape
    rh = R // 2
    tr = _tile(rh, 256, 16)
    nt = rh // tr

    def body(c_ref, g_ref, t_ref, q_ref, qw_ref):
        q = g_ref[...] + t_ref[...]
        q_ref[...] = q
        qw_ref[...] = q.astype(wire_dtype)

    blk = pl.BlockSpec((1, tr, C), lambda b, i, c_ref: (b, i, 0))
    mine = pl.BlockSpec((1, tr, C), lambda b, i, c_ref: (b, c_ref[0] * nt + i, 0))
    grid_spec = pltpu.PrefetchScalarGridSpec(num_scalar_prefetch=1, grid=(N_CHIPS, nt), in_specs=[mine, blk], out_specs=[blk, blk])
    return _pcall(body, name=name, grid_spec=grid_spec,
                  out_shape=[jax.ShapeDtypeStruct((N_CHIPS, rh, C), F32), jax.ShapeDtypeStruct((N_CHIPS, rh, C), wire_dtype)],
                  compiler_params=_params(("parallel", "parallel")))(_core_index(), g, theirs)


def half_sum(own, landed, *, name):
    _, rh, C = own.shape
    tr = _tile(rh, 256, 16)

    def body(me_ref, own_ref, land_ref, o_ref):
        total = None
        for p in range(N_CHIPS):
            term = jnp.where(me_ref[0] == p, own_ref[p], land_ref[p].astype(F32))
            total = term if total is None else total + term
        o_ref[...] = total

    blk = pl.BlockSpec((N_CHIPS, tr, C), lambda i, me_ref: (0, i, 0))
    grid_spec = pltpu.PrefetchScalarGridSpec(num_scalar_prefetch=1, grid=(rh // tr,), in_specs=[blk, blk],
                                             out_specs=pl.BlockSpec((tr, C), lambda i, me_ref: (i, 0)))
    me = (2 * lax.axis_index("x") + lax.axis_index("y")).astype(jnp.int32).reshape(1)
    return _pcall(body, name=name, grid_spec=grid_spec, out_shape=jax.ShapeDtypeStruct((rh, C), F32),
                  compiler_params=_params(("parallel",)))(me, own, landed)


def adamw(w, m, v, mine, theirs, *, name):
    _, R, C = w.shape
    rh = R // 2
    tr = _tile(rh, 256, 8)
    nt = rh // tr

    def body(c_ref, w_ref, m_ref, v_ref, a_ref, b_ref, g_ref, d_ref, nm_ref, nv_ref):
        is_mine = (pl.program_id(0) // nt) == c_ref[0]
        g = jnp.where(is_mine, a_ref[...], b_ref[...])
        g_ref[...] = g
        nm = ADAM_B1 * m_ref[...] + (1.0 - ADAM_B1) * g
        nv = ADAM_B2 * v_ref[...] + (1.0 - ADAM_B2) * (g * g)
        nm_ref[...] = nm
        nv_ref[...] = nv
        m_hat = nm / (1.0 - ADAM_B1 ** ADAM_STEP)
        v_hat = nv / (1.0 - ADAM_B2 ** ADAM_STEP)
        d_ref[...] = -ADAM_LR * (m_hat / (jnp.sqrt(v_hat) + ADAM_EPS) + ADAM_WD * w_ref[...])

    full = pl.BlockSpec((None, tr, C), lambda i, c_ref: (0, i, 0))
    a_spec = pl.BlockSpec((tr, C), lambda i, c_ref: (jnp.clip(i - c_ref[0] * nt, 0, nt - 1), 0))
    b_spec = pl.BlockSpec((tr, C), lambda i, c_ref: (jnp.clip(i - (1 - c_ref[0]) * nt, 0, nt - 1), 0))
    grid_spec = pltpu.PrefetchScalarGridSpec(num_scalar_prefetch=1, grid=(2 * nt,), in_specs=[full] * 3 + [a_spec, b_spec],
                                             out_specs=[full] * 4)
    return _pcall(body, name=name, grid_spec=grid_spec, out_shape=[jax.ShapeDtypeStruct((1, R, C), F32)] * 4,
                  compiler_params=_params(("arbitrary",)))(_core_index(), w, m, v, mine, theirs)


def _to_blocks(full, axis):
    r, c = full.shape
    if axis == 1:
        return full.reshape(r, N_CHIPS, c // N_CHIPS).transpose(1, 0, 2)
    return full.reshape(N_CHIPS, r // N_CHIPS, c)


def _from_blocks(blocks, axis):
    _, r, c = blocks.shape
    if axis == 1:
        return blocks.transpose(1, 0, 2).reshape(r, N_CHIPS * c)
    return blocks.reshape(N_CHIPS * r, c)


def _ffn_fwd(x, norm, wg_t, wu_t, wd, tag):
    h = rms_fwd(x, norm, name=tag + "_norm")
    gate, up, act = mm_fused(h, [wg_t, wu_t], _swiglu, [BF16] * 3, tb=True, name=tag + "_gate_up")
    out = mm(act, wd, scale=0.5, res=x, name=tag + "_down")
    return out, (h, gate, up, act)


def _ffn_bwd(dout, x, saved, norm, wg_t, wu_t, wd, tag, carry=None, reduce=None):
    h, gate, up, act = saved
    dgate, dup, *carried = mm_fused(dout, [wd], _swiglu_bwd, [BF16] * 2, tb=True, extras=[gate, up], name=tag + "_dact",
                                    carry=carry)
    dwd = mm(act, dout, ta=True, scale=0.5, name=tag + "_dwd")
    dwg_t = mm(dgate, h, ta=True, name=tag + "_dwg")
    dwu_t = mm(dup, h, ta=True, name=tag + "_dwu")
    reduced = None
    if reduce:
        blocks = [_to_blocks(g, 0) for g in (dwg_t, dwu_t, dwd)]
        dh, *from_sibling = mm(dgate, wg_t, name=tag + "_dh_gate", carry=(_SiblingPlan, blocks))
        pair = [pair_sum(g, t, BF16, name="pair_sum_" + n) for g, t, n in zip(blocks, from_sibling, reduce)]
        dh, *landed = mm(dup, wu_t, res=dh, name=tag + "_dh_up", carry=(_ExchangePlan, [q for _, q in pair]))
        reduced = (pair, landed)
    else:
        dh = mm(dgate, wg_t, name=tag + "_dh_gate")
        dh = mm(dup, wu_t, res=dh, name=tag + "_dh_up")
    dx, dnorm = rms_bwd(dh, x, norm, dout, name=tag + "_dnorm")
    return dx, dnorm, dwg_t, dwu_t, dwd, carried, reduced


TRANSPOSED = ('ffn1_w_gate', 'ffn1_w_up', 'ffn2_w_gate', 'ffn2_w_up')
FIRST_WEIGHTS = ['ffn1_w_gate', 'ffn1_w_up', 'ffn1_w_down']
MID_WEIGHTS = ['w_in', 'rwkv_w_lora_up', 'rwkv_a_lora_up', 'rwkv_g_lora_up']
LATE_WEIGHTS = ['w_branch_rwkv', 'w_branch_attn', 'w_out', 'ffn2_w_gate', 'ffn2_w_up', 'ffn2_w_down']


def _pair_sums(names, blocks, tag):
    from_sibling = to_sibling(blocks, True, name=tag + "_grads_to_sibling")
    return [pair_sum(g, t, F32 if n == 'small' else BF16, name="pair_sum_" + n)
            for g, t, n in zip(blocks, from_sibling, names)]


def _step(A):
    x, tgt = A['x'][0], A['loss_target'][0]
    T = x.shape[0]
    w = {n: A[n][0] for n in WEIGHT_NAMES}
    row = lambda a: a.reshape(1, -1)

    axis_of = {n: (0 if n in TRANSPOSED else axis) for n, axis in BIG}
    natural = lambda n, a: jnp.swapaxes(a, 1, 2) if n in TRANSPOSED else a
    shard = lambda n: natural(n, A[n])[0].astype(BF16)
    n1, nmix, n2, nfin = (row(w[n]) for n in ('ffn1_norm', 'mix_norm', 'ffn2_norm', 'final_norm'))
    gathered = gather_weights([shard(n) for n in FIRST_WEIGHTS[:2]], name="gather_weights")
    full = {n: _from_blocks(b, axis_of[n]) for n, b in zip(FIRST_WEIGHTS[:2], gathered)}
    h1 = rms_fwd(x, n1, name="ffn1_norm")
    gate1, up1, act1, down_blocks = mm_fused(h1, [full['ffn1_w_gate'], full['ffn1_w_up']], _swiglu, [BF16] * 3, tb=True,
                                             name="ffn1_gate_up", carry=(_GatherPlan, [shard('ffn1_w_down')]))
    full['ffn1_w_down'] = _from_blocks(down_blocks, axis_of['ffn1_w_down'])
    x1, *gathered = mm(act1, full['ffn1_w_down'], scale=0.5, res=x, name="ffn1_down",
                       carry=(_GatherPlan, [shard(n) for n in MID_WEIGHTS]))
    ffn1 = (h1, gate1, up1, act1)
    full.update({n: _from_blocks(b, axis_of[n]) for n, b in zip(MID_WEIGHTS, gathered)})
    w_in_r = _pad_rwkv_cols(full['w_in'][:, :RWKV_COLS])
    w_in_a = full['w_in'][:, RWKV_COLS:RWKV_COLS + ATT_COLS]
    w_in_g = full['w_in'][:, RWKV_COLS + ATT_COLS:]
    wlw, wla, wlg = (_pad_rows(full[n], 128).astype(F32) for n in ('rwkv_w_lora_up', 'rwkv_a_lora_up', 'rwkv_g_lora_up'))
    mu = _pad_rwkv_cols(row(w['rwkv_mu']))
    w0, a0, k_k, k_a, r_k, ln_w, ln_b = (row(w[n]) for n in ('rwkv_w0', 'rwkv_a0', 'rwkv_k_k', 'rwkv_k_a', 'rwkv_r_k',
                                                               'rwkv_ln_w', 'rwkv_ln_b'))
    qg = jnp.tile(row(w['attn_q_norm']), (1, N_HEADS))
    kg = jnp.tile(row(w['attn_k_norm']), (1, KVW // HEAD_DIM))
    sinks = jnp.pad(row(w['attn_sinks']), ((0, 0), (0, LANE - N_HEADS)))

    h2 = rms_fwd(x1, nmix, name="mix_norm")
    pr = mm(h2, w_in_r, name="proj_rwkv")
    pa = mm(h2, w_in_a, name="proj_att")
    pg = mm(h2, w_in_g, name="proj_gate")
    pr_shift = jnp.pad(pr, ((1, 0), (0, 0)))[:-1]
    r, dec, k2, v, a, b, sg = rwkv_pre_fwd(pr, pr_shift, mu, w0, a0, k_k, k_a, wlw, wla, wlg, name="rwkv_pre")
    y, states, *gathered = wkv_fwd(r, dec, k2, v, a, b, name="wkv_fwd", gather=[shard(n) for n in LATE_WEIGHTS])
    full.update({n: _from_blocks(b, axis_of[n]) for n, b in zip(LATE_WEIGHTS, gathered)})
    yr = rwkv_post_fwd(y, r, k2, v, sg, wlg, ln_w, ln_b, r_k, name="rwkv_post")
    ya = att_fwd(pa, qg, kg, sinks, name="att_fwd")
    br = mm(yr, full['w_branch_rwkv'], name="branch_rwkv")
    ba = mm(ya, full['w_branch_attn'], name="branch_att")
    mg = merge_fwd(br, ba, pg, name="merge")
    x2 = mm(mg, full['w_out'], res=x1, name="mix_out")
    x3, ffn2 = _ffn_fwd(x2, n2, full['ffn2_w_gate'], full['ffn2_w_up'], full['ffn2_w_down'], "ffn2")
    dx3, d_nfin, loss = final_loss(x3, tgt, nfin, name="final_loss")

    G = {'final_norm': d_nfin}
    dx2, G['ffn2_norm'], G['ffn2_w_gate'], G['ffn2_w_up'], G['ffn2_w_down'], _, _ = _ffn_bwd(
        dx3, x2, ffn2, n2, full['ffn2_w_gate'], full['ffn2_w_up'], full['ffn2_w_down'], "ffn2")
    dmg = mm(dx2, full['w_out'], tb=True, name="d_merge")
    G['w_out'] = mm(mg, dx2, ta=True, name="d_w_out")
    dbr, dba, dpg = merge_bwd(dmg, br, ba, pg, name="merge_bwd")
    dyr = mm(dbr, full['w_branch_rwkv'], tb=True, name="d_y_rwkv")
    G['w_branch_rwkv'] = mm(yr, dbr, ta=True, name="d_w_branch_rwkv")
    dya = mm(dba, full['w_branch_attn'], tb=True, name="d_y_att")
    G['w_branch_attn'] = mm(ya, dba, ta=True, name="d_w_branch_att")
    late_blocks = [_to_blocks(G[n], axis_of[n]) for n in LATE_WEIGHTS]
    dy, dz, dg, G['rwkv_ln_w'], G['rwkv_ln_b'], *late_from_sibling = rwkv_post_bwd(
        dyr, y, r, k2, v, sg, wlg, ln_w, ln_b, r_k, name="rwkv_post_bwd", carry=(_SiblingPlan, late_blocks))
    late_pair = [pair_sum(g, t, BF16, name="pair_sum_" + n) for g, t, n in zip(late_blocks, late_from_sibling, LATE_WEIGHTS)]
    res = wkv_bwd(r, dec, k2, v, a, b, dy, states, name="wkv_bwd", exchange=[q for _, q in late_pair])
    wkv_grads, late_landed = res[:6], res[6:]
    (dpr, d_mu, G['rwkv_w0'], G['rwkv_a0'], G['rwkv_k_k'], G['rwkv_k_a'], G['rwkv_r_k'], d_wlw, d_wla, d_wlg) = rwkv_pre_bwd(
        pr, pr_shift, *wkv_grads, dz, dg, mu, w0, a0, k_k, k_a, r_k, wlw, wla, wlg, name="rwkv_pre_bwd")
    G['rwkv_mu'] = _unpad_rwkv_cols(d_mu)
    G['rwkv_w_lora_up'], G['rwkv_a_lora_up'], G['rwkv_g_lora_up'] = d_wlw[:DECAY_LORA], d_wla[:ICLR_LORA], d_wlg[:GATE_LORA]
    dq, dko, dkn, dvo, dvn, G['attn_q_norm'], d_sinks = att_bwd(pa, dya, qg, kg, sinks, name="att_bwd")
    G['attn_sinks'] = d_sinks[:, :N_HEADS]
    dpa, G['attn_k_norm'] = att_kv_bwd(pa, dq, dko, dkn, dvo, dvn, kg, name="att_kv_bwd")
    d_w_in_r = mm(h2, dpr, ta=True, name="d_w_in_rwkv")
    d_w_in_a = mm(h2, dpa, ta=True, name="d_w_in_att")
    d_w_in_g = mm(h2, dpg, ta=True, name="d_w_in_gate")
    G['w_in'] = jnp.concatenate([_unpad_rwkv_cols(d_w_in_r), d_w_in_a, d_w_in_g], axis=1)
    mid_blocks = [_to_blocks(G[n], axis_of[n]) for n in MID_WEIGHTS]
    dh2, *mid_from_sibling = mm(dpr, w_in_r, tb=True, name="d_h2_rwkv", carry=(_SiblingPlan, mid_blocks))
    dh2 = mm(dpa, w_in_a, tb=True, res=dh2, name="d_h2_att")
    dh2 = mm(dpg, w_in_g, tb=True, res=dh2, name="d_h2_gate")
    dx1, G['mix_norm'] = rms_bwd(dh2, x1, nmix, dx2, name="d_mix_norm")
    mid_pair = [pair_sum(g, t, BF16, name="pair_sum_" + n) for g, t, n in zip(mid_blocks, mid_from_sibling, MID_WEIGHTS)]
    dx0, G['ffn1_norm'], _, _, _, mid_landed, (first_pair, first_landed) = _ffn_bwd(
        dx1, x, ffn1, n1, full['ffn1_w_gate'], full['ffn1_w_up'], full['ffn1_w_down'], "ffn1",
        carry=(_ExchangePlan, [q for _, q in mid_pair]), reduce=FIRST_WEIGHTS)

    small_shapes = [(w[n].size,) for n in SMALL] + [(1,)]

    def small_rows(parts):
        vec = jnp.concatenate([p.reshape(-1) for p in parts])
        return jnp.pad(vec, (0, SMALL_ROWS * FLAT_W - vec.shape[0])).reshape(SMALL_ROWS, FLAT_W)

    small = small_rows([G[n] for n in SMALL] + [loss[0, :1]])
    small_pair = _pair_sums(['small'], [jnp.broadcast_to(small[None], (N_CHIPS,) + small.shape)], "small")
    small_landed = exchange_chips([q for _, q in small_pair], name="exchange_small")
    names = FIRST_WEIGHTS + ['small'] + MID_WEIGHTS + LATE_WEIGHTS
    pair = first_pair + small_pair + mid_pair + late_pair
    landed = list(first_landed) + list(small_landed) + list(mid_landed) + list(late_landed)
    halves = [half_sum(own, l, name="half_sum_" + n) for (own, _), l, n in zip(pair, landed, names)]
    other_halves = to_sibling(halves, False, name="halves_to_sibling")

    def local(prefix, n):
        if n != 'small':
            return natural(n, A[prefix + n])
        return small_rows([A[prefix + s] for s in SMALL] + [jnp.zeros((1,), F32)])[None]

    result = {}
    for n, mine, theirs in zip(names, halves, other_halves):
        outs4 = adamw(local('', n), local('m_', n), local('v_', n), mine, theirs, name="adamw_" + n)
        for kind, o in zip(('grad_', 'delta_', 'new_m_', 'new_v_'), outs4):
            if n != 'small':
                result[kind + n] = natural(n, o)
            else:
                for s, part in zip(SMALL + ['loss'], _unpack_vec(o.reshape(-1), small_shapes)):
                    result[kind + s] = part.reshape(A[s].shape) if s != 'loss' else part.reshape(())
    outs = [result['grad_loss'], dx0[None]]
    for kind in ('grad_', 'delta_', 'new_m_', 'new_v_'):
        outs += [result[kind + n] for n in WEIGHT_NAMES]
    return tuple(outs)


def _unpack_vec(vec, shapes):
    out, off = [], 0
    for (n,) in shapes:
        out.append(vec[off:off + n])
        off += n
    return out


def kernel(x, ffn1_norm, ffn1_w_gate, ffn1_w_up, ffn1_w_down, mix_norm, w_in, rwkv_mu, rwkv_w0, rwkv_w_lora_up, rwkv_a0, rwkv_a_lora_up, rwkv_g_lora_up, rwkv_k_k, rwkv_k_a, rwkv_r_k, rwkv_ln_w, rwkv_ln_b, attn_q_norm, attn_k_norm, attn_sinks, w_branch_rwkv, w_branch_attn, w_out, ffn2_norm, ffn2_w_gate, ffn2_w_up, ffn2_w_down, final_norm, loss_target, m_ffn1_norm, m_ffn1_w_gate, m_ffn1_w_up, m_ffn1_w_down, m_mix_norm, m_w_in, m_rwkv_mu, m_rwkv_w0, m_rwkv_w_lora_up, m_rwkv_a0, m_rwkv_a_lora_up, m_rwkv_g_lora_up, m_rwkv_k_k, m_rwkv_k_a, m_rwkv_r_k, m_rwkv_ln_w, m_rwkv_ln_b, m_attn_q_norm, m_attn_k_norm, m_attn_sinks, m_w_branch_rwkv, m_w_branch_attn, m_w_out, m_ffn2_norm, m_ffn2_w_gate, m_ffn2_w_up, m_ffn2_w_down, m_final_norm, v_ffn1_norm, v_ffn1_w_gate, v_ffn1_w_up, v_ffn1_w_down, v_mix_norm, v_w_in, v_rwkv_mu, v_rwkv_w0, v_rwkv_w_lora_up, v_rwkv_a0, v_rwkv_a_lora_up, v_rwkv_g_lora_up, v_rwkv_k_k, v_rwkv_k_a, v_rwkv_r_k, v_rwkv_ln_w, v_rwkv_ln_b, v_attn_q_norm, v_attn_k_norm, v_attn_sinks, v_w_branch_rwkv, v_w_branch_attn, v_w_out, v_ffn2_norm, v_ffn2_w_gate, v_ffn2_w_up, v_ffn2_w_down, v_final_norm):
    return _step(dict(locals()))
```

```python
import functools

import jax
import jax.numpy as jnp
from jax import lax
from jax.experimental import pallas as pl
from jax.experimental.pallas import tpu as pltpu

F32 = jnp.float32
BF16 = jnp.bfloat16

D_MODEL = 1024
D_FF = 2816
HEAD_DIM = 64
N_HEADS = 8
RW = 512
KVW = 128
ATT_GROUP = 4
WINDOW = 128
BLOCK = 128
DECAY_LORA, ICLR_LORA, GATE_LORA = 32, 32, 96
RWKV_COLS = 3 * RW + DECAY_LORA + ICLR_LORA + GATE_LORA
ATT_COLS = RW + 2 * KVW
GATE_COLS = 2 * D_MODEL
RWKV_PAD = 3 * RW + 3 * 128
RMS_EPS = 1e-6
GN_EPS = 64e-5
N_CHIPS = 4
LANE = 128
FLAT_W = 1024
SMALL_ROWS = 32
NEG_BIG = -1e30

ADAM_LR, ADAM_B1, ADAM_B2, ADAM_EPS, ADAM_WD, ADAM_STEP = 0.001, 0.9, 0.999, 1e-08, 0.01, 10

VMEM_LIMIT = 56 * 1024 * 1024

WEIGHT_NAMES = ['ffn1_norm', 'ffn1_w_gate', 'ffn1_w_up', 'ffn1_w_down', 'mix_norm', 'w_in', 'rwkv_mu', 'rwkv_w0',
                'rwkv_w_lora_up', 'rwkv_a0', 'rwkv_a_lora_up', 'rwkv_g_lora_up', 'rwkv_k_k', 'rwkv_k_a', 'rwkv_r_k',
                'rwkv_ln_w', 'rwkv_ln_b', 'attn_q_norm', 'attn_k_norm', 'attn_sinks', 'w_branch_rwkv',
                'w_branch_attn', 'w_out', 'ffn2_norm', 'ffn2_w_gate', 'ffn2_w_up', 'ffn2_w_down', 'final_norm']
BIG = [('ffn1_w_gate', 1), ('ffn1_w_up', 1), ('ffn1_w_down', 0), ('w_in', 1), ('rwkv_w_lora_up', 1),
       ('rwkv_a_lora_up', 1), ('rwkv_g_lora_up', 1), ('w_branch_rwkv', 1), ('w_branch_attn', 1), ('w_out', 0),
       ('ffn2_w_gate', 1), ('ffn2_w_up', 1), ('ffn2_w_down', 0)]
SMALL = ['ffn1_norm', 'mix_norm', 'rwkv_mu', 'rwkv_w0', 'rwkv_a0', 'rwkv_k_k', 'rwkv_k_a', 'rwkv_r_k', 'rwkv_ln_w',
         'rwkv_ln_b', 'attn_q_norm', 'attn_k_norm', 'attn_sinks', 'ffn2_norm', 'final_norm']


def _pcall(body, **kw):
    return pl.pallas_call(body, **kw)


def _params(sem=None, **kw):
    if sem is not None:
        kw['dimension_semantics'] = sem
    return pltpu.CompilerParams(vmem_limit_bytes=VMEM_LIMIT, **kw)


def _tile(n, cap, mult):
    best = None
    for t in range(mult, min(n, cap) + 1, mult):
        if n % t == 0:
            best = t
    return best or n


def _sigmoid(z):
    return 1.0 / (1.0 + jnp.exp(-z))


def _softplus(z):
    return jnp.maximum(z, 0.0) + jnp.log(1.0 + jnp.exp(-jnp.abs(z)))


def _bdot(a, b, dims=(((1,), (0,)), ((), ()))):
    return lax.dot_general(a.astype(BF16), b.astype(BF16), dims, preferred_element_type=F32)


_NT = (((1,), (1,)), ((), ()))
_TN = (((0,), (0,)), ((), ()))


def _segsum(x, bd):
    hi = x.astype(BF16)
    lo = (x - hi.astype(F32)).astype(BF16)
    dot = functools.partial(lax.dot_general, dimension_numbers=(((1,), (0,)), ((), ())), preferred_element_type=F32)
    return dot(hi, bd) + dot(lo, bd)


_LORA_EDGES = (3 * RW, 3 * RW + DECAY_LORA, 3 * RW + DECAY_LORA + ICLR_LORA, RWKV_COLS)


def _pad_rwkv_cols(x):
    parts = [x[..., :3 * RW]]
    for lo, hi in zip(_LORA_EDGES[:-1], _LORA_EDGES[1:]):
        parts.append(jnp.pad(x[..., lo:hi], [(0, 0)] * (x.ndim - 1) + [(0, 128 - (hi - lo))]))
    return jnp.concatenate(parts, axis=-1)


def _unpad_rwkv_cols(x):
    parts = [x[..., :3 * RW]]
    for j, (lo, hi) in enumerate(zip(_LORA_EDGES[:-1], _LORA_EDGES[1:])):
        parts.append(x[..., 3 * RW + 128 * j:3 * RW + 128 * j + (hi - lo)])
    return jnp.concatenate(parts, axis=-1)


def _pad_rows(x, rows):
    return jnp.pad(x, [(0, rows - x.shape[0])] + [(0, 0)] * (x.ndim - 1))


def mm(a, b, *, name, ta=False, tb=False, scale=None, res=None, out_dtype=F32, carry=None):
    M, K = (a.shape[1], a.shape[0]) if ta else a.shape
    N = b.shape[0] if tb else b.shape[1]
    assert (b.shape[1] if tb else b.shape[0]) == K
    tm, tn, tk = _tile(M, 1408 if ta else 512, 128), _tile(N, 1408, 128), _tile(K, 1408, 128)
    nk = K // tk
    grid = (M // tm, N // tn, nk)
    dims = (((0 if ta else 1,), (1 if tb else 0,)), ((), ()))
    plan_cls, carried = carry if carry else (None, ())
    nc, nin = len(carried), 2 + (res is not None)

    def body(*refs):
        a_ref, b_ref = refs[:2]
        r_ref = refs[2] if res is not None else None
        o_ref, acc_ref = refs[nin + nc], refs[nin + 2 * nc + 1]
        k = pl.program_id(2)
        if nc:
            plan = plan_cls(refs[nin:nin + nc], refs[nin + nc + 1:nin + 2 * nc + 1], refs[nin + 2 * nc + 2:])
            at = lambda which: functools.reduce(jnp.logical_and, [pl.program_id(d) == (0 if which == 0 else grid[d] - 1)
                                                                 for d in range(3)])
            pl.when(at(0))(plan.start)
        part = _bdot(a_ref[...], b_ref[...], dims)

        @pl.when(k == 0)
        def _():
            acc_ref[...] = part

        @pl.when(k > 0)
        def _():
            acc_ref[...] += part

        @pl.when(k == nk - 1)
        def _():
            o = acc_ref[...]
            if scale is not None:
                o = o * scale
            if r_ref is not None:
                o = o + r_ref[...].astype(F32)
            o_ref[...] = o.astype(out_dtype)

        if nc:
            pl.when(at(1))(plan.finish)

    a_spec = pl.BlockSpec((tk, tm), lambda i, j, k: (k, i)) if ta else pl.BlockSpec((tm, tk), lambda i, j, k: (i, k))
    b_spec = pl.BlockSpec((tn, tk), lambda i, j, k: (j, k)) if tb else pl.BlockSpec((tk, tn), lambda i, j, k: (k, j))
    o_spec = pl.BlockSpec((tm, tn), lambda i, j, k: (i, j))
    in_specs = [a_spec, b_spec] + ([o_spec] if res is not None else [])
    args = (a, b) + ((res,) if res is not None else ())
    out_shape = jax.ShapeDtypeStruct((M, N), out_dtype)
    if not nc:
        return _pcall(
            body, name=name, grid=grid, in_specs=in_specs, out_specs=o_spec, out_shape=out_shape,
            scratch_shapes=[pltpu.VMEM((tm, tn), F32)], compiler_params=_params(("parallel", "parallel", "arbitrary")),
        )(*args)
    return _pcall(
        body, name=name, grid=grid, in_specs=in_specs + [_HBM] * nc, out_specs=[o_spec] + [_HBM] * nc,
        out_shape=[out_shape] + plan_cls.out_shapes(carried), scratch_shapes=[pltpu.VMEM((tm, tn), F32)] + plan_cls.sems(nc),
        compiler_params=_params(("arbitrary", "arbitrary", "arbitrary")),
    )(*args, *carried)


def mm_fused(a, bs, finish, out_dtypes, *, name, tb=False, extras=(), carry=None):
    M, K = a.shape
    N = bs[0].shape[0] if tb else bs[0].shape[1]
    tm, tn = _tile(M, 512, 128), _tile(N, 1408, 128)
    grid = (M // tm, N // tn)
    dims = (((1,), (1 if tb else 0,)), ((), ()))
    plan_cls, carried = carry if carry else (None, ())
    nc, nb, nx, no = len(carried), len(bs), len(extras), len(out_dtypes)
    nin = 1 + nb + nx

    def body(*refs):
        a_ref, b_refs, x_refs = refs[0], refs[1:1 + nb], refs[1 + nb:nin]
        o_refs = refs[nin + nc:nin + nc + no]
        if nc:
            plan = plan_cls(refs[nin:nin + nc], refs[nin + nc + no:nin + 2 * nc + no], refs[nin + 2 * nc + no:])
            at = lambda which: jnp.logical_and(*[pl.program_id(d) == (0 if which == 0 else grid[d] - 1) for d in range(2)])
            pl.when(at(0))(plan.start)
        av = a_ref[...]
        outs = finish([_bdot(av, b_ref[...], dims) for b_ref in b_refs], [x_ref[...] for x_ref in x_refs])
        for o_ref, o in zip(o_refs, outs):
            o_ref[...] = o.astype(o_ref.dtype)
        if nc:
            pl.when(at(1))(plan.finish)

    a_spec = pl.BlockSpec((tm, K), lambda i, j: (i, 0))
    b_spec = pl.BlockSpec((tn, K), lambda i, j: (j, 0)) if tb else pl.BlockSpec((K, tn), lambda i, j: (0, j))
    o_spec = pl.BlockSpec((tm, tn), lambda i, j: (i, j))
    out_shape = [jax.ShapeDtypeStruct((M, N), d) for d in out_dtypes]
    if not nc:
        return _pcall(body, name=name, grid=grid, in_specs=[a_spec] + [b_spec] * nb + [o_spec] * nx, out_specs=[o_spec] * no,
                      out_shape=out_shape, compiler_params=_params(("parallel", "parallel")))(a, *bs, *extras)
    return _pcall(body, name=name, grid=grid, in_specs=[a_spec] + [b_spec] * nb + [o_spec] * nx + [_HBM] * nc,
                  out_specs=[o_spec] * no + [_HBM] * nc, out_shape=out_shape + plan_cls.out_shapes(carried),
                  scratch_shapes=plan_cls.sems(nc), compiler_params=_params(("arbitrary", "arbitrary")))(a, *bs, *extras, *carried)


def _swiglu(products, _):
    g, u = products
    return g, u, g * _sigmoid(g) * u


def _swiglu_bwd(products, extras):
    (da,), (gate, up) = products, extras
    gv = gate.astype(F32)
    s = _sigmoid(gv)
    return da * 0.5 * up.astype(F32) * s * (1.0 + gv * (1.0 - s)), da * 0.5 * gv * s


def _row_spec(tr, c):
    return pl.BlockSpec((tr, c), lambda i: (i, 0))


def _full_spec(shape):
    return pl.BlockSpec(shape, lambda i: (0,) * len(shape))


def _acc_rows(ref, val, i):
    @pl.when(i == 0)
    def _():
        ref[...] = val

    @pl.when(i > 0)
    def _():
        ref[...] += val


def rms_fwd(x, g, *, name):
    T, D = x.shape
    tr = _tile(T, 512, 8)

    def body(x_ref, g_ref, h_ref):
        xv = x_ref[...]
        r = lax.rsqrt(jnp.mean(xv * xv, axis=-1, keepdims=True) + RMS_EPS)
        h_ref[...] = (xv * r * g_ref[...]).astype(BF16)

    return _pcall(body, name=name, grid=(T // tr,), in_specs=[_row_spec(tr, D), _full_spec((1, D))],
                  out_specs=_row_spec(tr, D), out_shape=jax.ShapeDtypeStruct((T, D), BF16),
                  compiler_params=_params(("parallel",)))(x, g)


def rms_bwd(dh, x, g, res, *, name):
    T, D = x.shape
    tr = _tile(T, 256, 8)

    def body(dh_ref, x_ref, g_ref, res_ref, dx_ref, dg_ref):
        i = pl.program_id(0)
        xv, dhv = x_ref[...], dh_ref[...].astype(F32)
        r = lax.rsqrt(jnp.mean(xv * xv, axis=-1, keepdims=True) + RMS_EPS)
        xh = xv * r
        dxh = dhv * g_ref[...]
        dx_ref[...] = res_ref[...] + r * (dxh - xh * jnp.mean(dxh * xh, axis=-1, keepdims=True))
        _acc_rows(dg_ref, jnp.sum(dhv * xh, axis=0, keepdims=True), i)

    return _pcall(body, name=name, grid=(T // tr,),
                  in_specs=[_row_spec(tr, D), _row_spec(tr, D), _full_spec((1, D)), _row_spec(tr, D)],
                  out_specs=[_row_spec(tr, D), _full_spec((1, D))],
                  out_shape=[jax.ShapeDtypeStruct((T, D), F32), jax.ShapeDtypeStruct((1, D), F32)],
                  compiler_params=_params(("arbitrary",)))(dh, x, g, res)


def final_loss(x, tgt, g, *, name):
    T, D = x.shape
    tr = _tile(T, 256, 8)

    def body(x_ref, t_ref, g_ref, dx_ref, dg_ref, loss_ref):
        i = pl.program_id(0)
        xv = x_ref[...]
        r = lax.rsqrt(jnp.mean(xv * xv, axis=-1, keepdims=True) + RMS_EPS)
        xh = xv * r
        e = xh * g_ref[...] - t_ref[...]
        part = 0.5 * jnp.sum(jnp.mean(e * e, axis=-1, keepdims=True), axis=0, keepdims=True)
        dy = e * (1.0 / D)
        dxh = dy * g_ref[...]
        dx_ref[...] = r * (dxh - xh * jnp.mean(dxh * xh, axis=-1, keepdims=True))
        _acc_rows(dg_ref, jnp.sum(dy * xh, axis=0, keepdims=True), i)
        _acc_rows(loss_ref, jnp.broadcast_to(part, (1, LANE)), i)

    return _pcall(body, name=name, grid=(T // tr,),
                  in_specs=[_row_spec(tr, D), _row_spec(tr, D), _full_spec((1, D))],
                  out_specs=[_row_spec(tr, D), _full_spec((1, D)), _full_spec((1, LANE))],
                  out_shape=[jax.ShapeDtypeStruct((T, D), F32), jax.ShapeDtypeStruct((1, D), F32),
                             jax.ShapeDtypeStruct((1, LANE), F32)],
                  compiler_params=_params(("arbitrary",)))(x, tgt, g)


def merge_fwd(br, ba, pg, *, name):
    T, D = br.shape
    tr = _tile(T, 256, 8)

    def body(br_ref, ba_ref, pg_ref, o_ref):
        pgv = pg_ref[...]
        o_ref[...] = (_sigmoid(pgv[:, :D]) * br_ref[...] + _sigmoid(pgv[:, D:]) * ba_ref[...]).astype(BF16)

    return _pcall(body, name=name, grid=(T // tr,), in_specs=[_row_spec(tr, D), _row_spec(tr, D), _row_spec(tr, 2 * D)],
                  out_specs=_row_spec(tr, D), out_shape=jax.ShapeDtypeStruct((T, D), BF16),
                  compiler_params=_params(("parallel",)))(br, ba, pg)


def merge_bwd(dm, br, ba, pg, *, name):
    T, D = br.shape
    tr = _tile(T, 256, 8)

    def body(dm_ref, br_ref, ba_ref, pg_ref, dbr_ref, dba_ref, dpg_ref):
        pgv, dmv = pg_ref[...], dm_ref[...]
        sr, sa = _sigmoid(pgv[:, :D]), _sigmoid(pgv[:, D:])
        dbr_ref[...] = (dmv * sr).astype(BF16)
        dba_ref[...] = (dmv * sa).astype(BF16)
        dpg_ref[:, :D] = dmv * br_ref[...] * sr * (1.0 - sr)
        dpg_ref[:, D:] = dmv * ba_ref[...] * sa * (1.0 - sa)

    return _pcall(body, name=name, grid=(T // tr,),
                  in_specs=[_row_spec(tr, D), _row_spec(tr, D), _row_spec(tr, D), _row_spec(tr, 2 * D)],
                  out_specs=[_row_spec(tr, D), _row_spec(tr, D), _row_spec(tr, 2 * D)],
                  out_shape=[jax.ShapeDtypeStruct((T, D), BF16), jax.ShapeDtypeStruct((T, D), BF16),
                             jax.ShapeDtypeStruct((T, 2 * D), F32)],
                  compiler_params=_params(("parallel",)))(dm, br, ba, pg)


def _rwkv_mix(p, prev, mu, w0, a0, k_k, k_a, wlw, wla, wlg, bd):
    pp = p + (prev - p) * mu
    r, k, v = pp[:, 0:RW], pp[:, RW:2 * RW], pp[:, 2 * RW:3 * RW]
    xw, xa, xg = pp[:, 3 * RW:3 * RW + 128], pp[:, 3 * RW + 128:3 * RW + 256], pp[:, 3 * RW + 256:3 * RW + 384]
    th = jnp.tanh(xw)
    z = -(w0 + _bdot(th, wlw))
    e = jnp.exp(-_softplus(z) - 0.5)
    decay = jnp.exp(-e)
    a = _sigmoid(a0 + _bdot(xa, wla))
    sg = _sigmoid(xg)
    kkr = k * k_k
    n = jnp.sqrt(_segsum(kkr * kkr, bd))
    kk = kkr / jnp.maximum(n, 1e-12)
    k2 = k * (1.0 + (a - 1.0) * k_a)
    return dict(r=r, k=k, v=v, xa=xa, th=th, z=z, e=e, decay=decay, a=a, sg=sg, n=n, kk=kk, k2=k2)


def _seg_matrix(n, shift):
    r = lax.shift_right_logical(lax.broadcasted_iota(jnp.int32, (n, n), 0), shift)
    c = lax.shift_right_logical(lax.broadcasted_iota(jnp.int32, (n, n), 1), shift)
    return jnp.where(r == c, 1.0, 0.0).astype(BF16)


def rwkv_pre_fwd(p, pshift, mu, w0, a0, k_k, k_a, wlw, wla, wlg, *, name):
    T = p.shape[0]
    tr = _tile(T, 256, 8)

    def body(p_ref, ps_ref, mu_ref, w0_ref, a0_ref, kk_ref, ka_ref, wlw_ref, wla_ref, wlg_ref,
             r_ref, w_ref, k_ref, v_ref, a_ref, b_ref, g_ref):
        pv, prev = p_ref[...], ps_ref[...]
        m = _rwkv_mix(pv, prev, mu_ref[...], w0_ref[...], a0_ref[...], kk_ref[...], ka_ref[...],
                      wlw_ref[...], wla_ref[...], wlg_ref[...], _seg_matrix(RW, 6))
        r_ref[...] = m['r']
        w_ref[...] = m['decay']
        k_ref[...] = m['k2']
        v_ref[...] = m['v']
        a_ref[...] = -m['kk']
        b_ref[...] = m['kk'] * m['a']
        g_ref[...] = m['sg']

    vec = _row_spec(tr, RW)
    return _pcall(
        body, name=name, grid=(T // tr,),
        in_specs=[_row_spec(tr, RWKV_PAD), _row_spec(tr, RWKV_PAD), _full_spec((1, RWKV_PAD))] + [_full_spec((1, RW))] * 4
        + [_full_spec((128, RW))] * 3,
        out_specs=[vec] * 6 + [_row_spec(tr, 128)],
        out_shape=[jax.ShapeDtypeStruct((T, RW), F32)] * 6 + [jax.ShapeDtypeStruct((T, 128), F32)],
        compiler_params=_params(("parallel",)),
    )(p, pshift, mu, w0, a0, k_k, k_a, wlw, wla, wlg)


def _group_norm(y, bd):
    mean = _segsum(y, bd) * (1.0 / HEAD_DIM)
    yc = y - mean
    rstd = lax.rsqrt(_segsum(yc * yc, bd) * (1.0 / HEAD_DIM) + GN_EPS)
    return yc * rstd, rstd


def rwkv_post_fwd(y, r, k2, v, sg, wlg, ln_w, ln_b, r_k, *, name):
    T = y.shape[0]
    tr = _tile(T, 256, 8)

    def body(y_ref, r_ref, k_ref, v_ref, sg_ref, wlg_ref, lw_ref, lb_ref, rk_ref, o_ref):
        bd = _seg_matrix(RW, 6)
        yn, _ = _group_norm(y_ref[...], bd)
        s = _segsum(r_ref[...] * k_ref[...] * rk_ref[...], bd)
        g = _bdot(sg_ref[...], wlg_ref[...])
        o_ref[...] = ((yn * lw_ref[...] + lb_ref[...] + s * v_ref[...]) * g).astype(BF16)

    vec = _row_spec(tr, RW)
    return _pcall(body, name=name, grid=(T // tr,),
                  in_specs=[vec] * 4 + [_row_spec(tr, 128), _full_spec((128, RW))] + [_full_spec((1, RW))] * 3, out_specs=vec,
                  out_shape=jax.ShapeDtypeStruct((T, RW), BF16), compiler_params=_params(("parallel",)))(
                      y, r, k2, v, sg, wlg, ln_w, ln_b, r_k)


def rwkv_post_bwd(dyr, y, r, k2, v, sg, wlg, ln_w, ln_b, r_k, *, name, carry=None):
    T = y.shape[0]
    tr = _tile(T, 256, 8)
    nt = T // tr
    plan_cls, carried = carry if carry else (None, ())
    nc = len(carried)

    def body(*refs):
        dyr_ref, y_ref, r_ref, k_ref, v_ref, sg_ref, wlg_ref, lw_ref, lb_ref, rk_ref = refs[:10]
        dy_ref, dz_ref, dg_ref, dlw_ref, dlb_ref = refs[10 + nc:15 + nc]
        i = pl.program_id(0)
        if nc:
            plan = plan_cls(refs[10:10 + nc], refs[15 + nc:15 + 2 * nc], refs[15 + 2 * nc:])
            pl.when(i == 0)(plan.start)
        bd = _seg_matrix(RW, 6)
        yn, rstd = _group_norm(y_ref[...], bd)
        s = _segsum(r_ref[...] * k_ref[...] * rk_ref[...], bd)
        dyrv = dyr_ref[...]
        dg_ref[...] = dyrv * (yn * lw_ref[...] + lb_ref[...] + s * v_ref[...])
        dz = dyrv * _bdot(sg_ref[...], wlg_ref[...])
        dz_ref[...] = dz
        dyn = dz * lw_ref[...]
        inv = 1.0 / HEAD_DIM
        dy_ref[...] = rstd * (dyn - _segsum(dyn, bd) * inv - yn * (_segsum(dyn * yn, bd) * inv))
        _acc_rows(dlw_ref, jnp.sum(dz * yn, axis=0, keepdims=True), i)
        _acc_rows(dlb_ref, jnp.sum(dz, axis=0, keepdims=True), i)
        if nc:
            pl.when(i == nt - 1)(plan.finish)

    vec = _row_spec(tr, RW)
    one = _full_spec((1, RW))
    return _pcall(body, name=name, grid=(nt,),
                  in_specs=[vec] * 5 + [_row_spec(tr, 128), _full_spec((128, RW))] + [one] * 3 + [_HBM] * nc,
                  out_specs=[vec] * 3 + [one] * 2 + [_HBM] * nc,
                  out_shape=[jax.ShapeDtypeStruct((T, RW), F32)] * 3 + [jax.ShapeDtypeStruct((1, RW), F32)] * 2
                  + (plan_cls.out_shapes(carried) if nc else []),
                  scratch_shapes=plan_cls.sems(nc) if nc else [],
                  compiler_params=_params(("arbitrary",)))(dyr, y, r, k2, v, sg, wlg, ln_w, ln_b, r_k, *carried)


def rwkv_pre_bwd(p, pshift, dr_w, dw_w, dk_w, dv_w, da_w, db_w, dz, dg, mu, w0, a0, k_k, k_a, r_k, wlw, wla, wlg, *, name):
    T = p.shape[0]
    tr = _tile(T, 256, 8)
    n = T // tr

    def body(p_ref, ps_ref, dr_ref, dw_ref, dk_ref, dv_ref, da_ref, db_ref, dz_ref, dg_ref,
             mu_ref, w0_ref, a0_ref, kk_ref, ka_ref, rk_ref, wlw_ref, wla_ref, wlg_ref,
             dp_ref, dmu_ref, dw0_ref, da0_ref, dkk_ref, dka_ref, drk_ref, dwlw_ref, dwla_ref, dwlg_ref,
             carry, dpp, acc_w, acc_a, acc_g):
        i = pl.program_id(0)

        @pl.when(i == 0)
        def _():
            carry[...] = jnp.zeros_like(carry)

        pv, prev, mu = p_ref[...], ps_ref[...], mu_ref[...]
        bd = _seg_matrix(RW, 6)
        k_k, k_a, r_k = kk_ref[...], ka_ref[...], rk_ref[...]
        m = _rwkv_mix(pv, prev, mu, w0_ref[...], a0_ref[...], k_k, k_a, wlw_ref[...], wla_ref[...], wlg_ref[...], bd)
        r, k, v, a, kk, k2 = m['r'], m['k'], m['v'], m['a'], m['kk'], m['k2']
        dzv, dgv = dz_ref[...], dg_ref[...]
        s = _segsum(r * k2 * r_k, bd)
        ds = _segsum(dzv * v, bd)
        dr = dr_ref[...] + ds * k2 * r_k
        dk2 = dk_ref[...] + ds * r * r_k
        dv = dv_ref[...] + dzv * s
        dbv = db_ref[...]
        dkk = dbv * a - da_ref[...]
        da = dbv * kk + dk2 * k * k_a
        dk = dk2 * (1.0 + (a - 1.0) * k_a)
        nmax = jnp.maximum(m['n'], 1e-12)
        dkkr = jnp.where(m['n'] > 1e-12, dkk - kk * _segsum(dkk * kk, bd), dkk) / nmax
        dk = dk + dkkr * k_k
        dapre = da * a * (1.0 - a)
        dwpre = dw_ref[...] * m['decay'] * (-m['e']) * _sigmoid(m['z'])
        dth = _bdot(dwpre, wlw_ref[...], _NT)
        dxa = _bdot(dapre, wla_ref[...], _NT)
        dsg = _bdot(dgv, wlg_ref[...], _NT)
        dpp[:, 0:RW] = dr
        dpp[:, RW:2 * RW] = dk
        dpp[:, 2 * RW:3 * RW] = dv
        dpp[:, 3 * RW:3 * RW + 128] = dth * (1.0 - m['th'] * m['th'])
        dpp[:, 3 * RW + 128:3 * RW + 256] = dxa
        dpp[:, 3 * RW + 256:3 * RW + 384] = dsg * m['sg'] * (1.0 - m['sg'])
        d = dpp[...]
        zed = d * mu
        last = lax.broadcasted_iota(jnp.int32, pv.shape, 0) == tr - 1
        dp_ref[...] = d * (1.0 - mu) + jnp.where(last, carry[0:1, :], pltpu.roll(zed, tr - 1, 0))
        carry[...] = zed[0:8, :]

        def colsum(x):
            return jnp.sum(x, axis=0, keepdims=True)

        _acc_rows(dmu_ref, colsum(d * (prev - pv)), i)
        _acc_rows(dw0_ref, colsum(dwpre), i)
        _acc_rows(da0_ref, colsum(dapre), i)
        _acc_rows(dkk_ref, colsum(dkkr * k), i)
        _acc_rows(dka_ref, colsum(dk2 * k * (a - 1.0)), i)
        _acc_rows(drk_ref, colsum(ds * r * k2), i)
        _acc_rows(acc_w, _bdot(m['th'], dwpre, _TN), i)
        _acc_rows(acc_a, _bdot(m['xa'], dapre, _TN), i)
        _acc_rows(acc_g, _bdot(m['sg'], dgv, _TN), i)

        @pl.when(i == n - 1)
        def _():
            dwlw_ref[...] = acc_w[...]
            dwla_ref[...] = acc_a[...]
            dwlg_ref[...] = acc_g[...]

    rev = lambda c: pl.BlockSpec((tr, c), lambda i: (n - 1 - i, 0))
    one, lora = _full_spec((1, RW)), _full_spec((128, RW))
    return _pcall(
        body, name=name, grid=(n,),
        in_specs=[rev(RWKV_PAD), rev(RWKV_PAD)] + [rev(RW)] * 8 + [_full_spec((1, RWKV_PAD))] + [one] * 5 + [lora] * 3,
        out_specs=[rev(RWKV_PAD), _full_spec((1, RWKV_PAD))] + [one] * 5 + [lora] * 3,
        out_shape=[jax.ShapeDtypeStruct((T, RWKV_PAD), F32), jax.ShapeDtypeStruct((1, RWKV_PAD), F32)]
        + [jax.ShapeDtypeStruct((1, RW), F32)] * 5 + [jax.ShapeDtypeStruct((128, RW), F32)] * 3,
        scratch_shapes=[pltpu.VMEM((8, RWKV_PAD), F32), pltpu.VMEM((tr, RWKV_PAD), F32)] + [pltpu.VMEM((128, RW), F32)] * 3,
        compiler_params=_params(("arbitrary",)),
    )(p, pshift, dr_w, dw_w, dk_w, dv_w, da_w, db_w, dz, dg, mu, w0, a0, k_k, k_a, r_k, wlw, wla, wlg)


def _qk_norm(x, g, bd):
    r = lax.rsqrt(_segsum(x * x, bd) * (1.0 / HEAD_DIM) + RMS_EPS)
    return x * r * g, r


def _att_mask(i):
    qi = lax.broadcasted_iota(jnp.int32, (BLOCK, 2 * BLOCK), 0)
    kj = lax.broadcasted_iota(jnp.int32, (BLOCK, 2 * BLOCK), 1)
    band = (kj <= qi + BLOCK) & (kj > qi + BLOCK - WINDOW)
    return band & ((kj >= BLOCK) | (i > 0))


_HQK = (((2,), (2,)), ((0,), (0,)))
_HPV = (((2,), (1,)), ((0,), (0,)))
_HTN = (((1,), (1,)), ((0,), (0,)))


def _heads(x, n):
    return jnp.stack([x[:, h * HEAD_DIM:(h + 1) * HEAD_DIM] for h in range(n)])


def _unheads(x3):
    return jnp.concatenate([x3[h] for h in range(x3.shape[0])], axis=1)


def _kv_heads(x):
    x2 = _heads(x, KVW // HEAD_DIM)
    return jnp.concatenate([x2[g:g + 1] for g in range(KVW // HEAD_DIM) for _ in range(ATT_GROUP)], axis=0)


def _sinks3(sk):
    return jnp.stack([sk[0:1, h:h + 1] for h in range(N_HEADS)])


def _att_probs(q3, k3, mask, sink):
    s = _bdot(q3, k3, _HQK) * (HEAD_DIM ** -0.5)
    s = jnp.where(mask[None], s, NEG_BIG)
    m = jnp.maximum(jnp.max(s, axis=-1, keepdims=True), sink)
    pexp = jnp.exp(s - m)
    psink = jnp.exp(sink - m)
    inv = 1.0 / (jnp.sum(pexp, axis=-1, keepdims=True) + psink)
    return pexp * inv, psink * inv


ATT_SUB = 2


def _att_blocks(sub):
    cur = pl.BlockSpec((sub * BLOCK, ATT_COLS), lambda i: (i, 0))
    prev = pl.BlockSpec((BLOCK, ATT_COLS), lambda i: (jnp.maximum(i * sub - 1, 0), 0))
    return cur, prev


def _att_sub(cur_all, prev_first, j):
    cur = cur_all[j * BLOCK:(j + 1) * BLOCK]
    return cur, (prev_first if j == 0 else cur_all[(j - 1) * BLOCK:j * BLOCK])


def _att_qkv(cur, prev, qn_g, kn_g):
    bq, bk = _seg_matrix(RW, 6), _seg_matrix(KVW, 6)
    qn, rq = _qk_norm(cur[:, 0:RW], qn_g, bq)
    kcur, rkc = _qk_norm(cur[:, RW:RW + KVW], kn_g, bk)
    kprev, _ = _qk_norm(prev[:, RW:RW + KVW], kn_g, bk)
    kc = jnp.concatenate([kprev, kcur], axis=0)
    vc = jnp.concatenate([prev[:, RW + KVW:], cur[:, RW + KVW:]], axis=0)
    return qn, rq, kc, vc, rkc


def att_fwd(pa, qn_g, kn_g, sinks, *, name):
    T = pa.shape[0]
    sub = ATT_SUB if (T // BLOCK) % ATT_SUB == 0 else 1
    n = T // (sub * BLOCK)

    def body(cur_ref, prev_ref, qg_ref, kg_ref, sk_ref, o_ref):
        i = pl.program_id(0)
        cur_all, prev_first = cur_ref[...], prev_ref[...]
        for j in range(sub):
            cur, prev = _att_sub(cur_all, prev_first, j)
            qn, _, kc, vc, _ = _att_qkv(cur, prev, qg_ref[...], kg_ref[...])
            probs, _ = _att_probs(_heads(qn, N_HEADS), _kv_heads(kc), _att_mask(i * sub + j), _sinks3(sk_ref[...]))
            o_ref[j * BLOCK:(j + 1) * BLOCK, :] = _unheads(_bdot(probs, _kv_heads(vc), _HPV))

    cur, prev = _att_blocks(sub)
    return _pcall(body, name=name, grid=(n,),
                  in_specs=[cur, prev, _full_spec((1, RW)), _full_spec((1, KVW)), _full_spec((1, LANE))],
                  out_specs=pl.BlockSpec((sub * BLOCK, RW), lambda i: (i, 0)), out_shape=jax.ShapeDtypeStruct((T, RW), F32),
                  compiler_params=_params(("parallel",)))(pa, pa, qn_g, kn_g, sinks)


def att_bwd(pa, do, qn_g, kn_g, sinks, *, name):
    T = pa.shape[0]
    sub = ATT_SUB if (T // BLOCK) % ATT_SUB == 0 else 1
    n = T // (sub * BLOCK)

    def one_block(cur, prev, do, blk, qg_ref, kg_ref, sk_ref, rows, dq_ref, dko_ref, dkn_ref, dvo_ref, dvn_ref):
        qn, rq, kc, vc, _ = _att_qkv(cur, prev, qg_ref[...], kg_ref[...])
        q3, k3, v3, do3 = _heads(qn, N_HEADS), _kv_heads(kc), _kv_heads(vc), _heads(do, N_HEADS)
        probs, psink = _att_probs(q3, k3, _att_mask(blk), _sinks3(sk_ref[...]))
        dprobs = _bdot(do3, v3, _HQK)
        delta = jnp.sum(probs * dprobs, axis=-1, keepdims=True)
        ds = probs * (dprobs - delta) * (HEAD_DIM ** -0.5)
        dsink3 = -jnp.sum(psink * delta, axis=1, keepdims=True)
        lane = lax.broadcasted_iota(jnp.int32, (1, LANE), 1)
        dsink = jnp.zeros((1, LANE), F32)
        for h in range(N_HEADS):
            dsink = dsink + jnp.where(lane == h, dsink3[h], 0.0)
        dqn = _unheads(_bdot(ds, k3, _HPV))

        def per_kv_head(x3):
            groups = [sum(x3[g * ATT_GROUP + j] for j in range(ATT_GROUP)) for g in range(KVW // HEAD_DIM)]
            return jnp.concatenate(groups, axis=1)

        dk, dv = per_kv_head(_bdot(ds, q3, _HTN)), per_kv_head(_bdot(probs, do3, _HTN))
        dkn_ref[rows, :], dko_ref[rows, :] = dk[0:BLOCK], dk[BLOCK:]
        dvn_ref[rows, :], dvo_ref[rows, :] = dv[0:BLOCK], dv[BLOCK:]
        qhat = cur[:, 0:RW] * rq
        dqh = dqn * qg_ref[...]
        dq_ref[rows, :] = rq * (dqh - qhat * (_segsum(dqh * qhat, _seg_matrix(RW, 6)) * (1.0 / HEAD_DIM)))
        prod = dqn * qhat
        fold = prod[:, 0:HEAD_DIM]
        for h in range(1, N_HEADS):
            fold = fold + prod[:, h * HEAD_DIM:(h + 1) * HEAD_DIM]
        return jnp.sum(fold, axis=0, keepdims=True), dsink

    def body(cur_ref, prev_ref, do_ref, qg_ref, kg_ref, sk_ref,
             dq_ref, dko_ref, dkn_ref, dvo_ref, dvn_ref, dqg_ref, dsk_ref):
        i = pl.program_id(0)
        cur_all, prev_first, do_all = cur_ref[...], prev_ref[...], do_ref[...]
        dqg, dsk = None, None
        for j in range(sub):
            cur, prev = _att_sub(cur_all, prev_first, j)
            rows = slice(j * BLOCK, (j + 1) * BLOCK)
            g, s = one_block(cur, prev, do_all[rows], i * sub + j, qg_ref, kg_ref, sk_ref, rows,
                             dq_ref, dko_ref, dkn_ref, dvo_ref, dvn_ref)
            dqg, dsk = (g, s) if dqg is None else (dqg + g, dsk + s)
        _acc_rows(dqg_ref, dqg, i)
        _acc_rows(dsk_ref, dsk, i)

    cur, prev = _att_blocks(sub)
    kvb = pl.BlockSpec((sub * BLOCK, KVW), lambda i: (i, 0))
    qb = pl.BlockSpec((sub * BLOCK, RW), lambda i: (i, 0))
    return _pcall(body, name=name, grid=(n,),
                  in_specs=[cur, prev, qb, _full_spec((1, RW)), _full_spec((1, KVW)), _full_spec((1, LANE))],
                  out_specs=[qb, kvb, kvb, kvb, kvb, _full_spec((1, HEAD_DIM)), _full_spec((1, LANE))],
                  out_shape=[jax.ShapeDtypeStruct((T, RW), F32)] + [jax.ShapeDtypeStruct((T, KVW), F32)] * 4
                  + [jax.ShapeDtypeStruct((1, HEAD_DIM), F32), jax.ShapeDtypeStruct((1, LANE), F32)],
                  compiler_params=_params(("arbitrary",)))(pa, pa, do, qn_g, kn_g, sinks)


def att_kv_bwd(pa, dq, dko, dkn, dvo, dvn, kn_g, *, name):
    T = pa.shape[0]
    n = T // BLOCK

    def body(pa_ref, dq_ref, dko_ref, dkn_ref, dvo_ref, dvn_ref, kg_ref, dpa_ref, dkg_ref):
        i = pl.program_id(0)
        more = i < n - 1
        dkn_tot = dko_ref[...] + jnp.where(more, dkn_ref[...], 0.0)
        dv_tot = dvo_ref[...] + jnp.where(more, dvn_ref[...], 0.0)
        kraw = pa_ref[:, RW:RW + KVW]
        bk = _seg_matrix(KVW, 6)
        _, rk = _qk_norm(kraw, kg_ref[...], bk)
        khat = kraw * rk
        dkh = dkn_tot * kg_ref[...]
        dpa_ref[:, 0:RW] = dq_ref[...]
        dpa_ref[:, RW:RW + KVW] = rk * (dkh - khat * (_segsum(dkh * khat, bk) * (1.0 / HEAD_DIM)))
        dpa_ref[:, RW + KVW:] = dv_tot
        prod = dkn_tot * khat
        _acc_rows(dkg_ref, jnp.sum(prod[:, 0:HEAD_DIM] + prod[:, HEAD_DIM:], axis=0, keepdims=True), i)

    kvb = pl.BlockSpec((BLOCK, KVW), lambda i: (i, 0))
    nxt = pl.BlockSpec((BLOCK, KVW), lambda i: (jnp.minimum(i + 1, n - 1), 0))
    return _pcall(body, name=name, grid=(n,),
                  in_specs=[pl.BlockSpec((BLOCK, ATT_COLS), lambda i: (i, 0)), pl.BlockSpec((BLOCK, RW), lambda i: (i, 0)),
                            kvb, nxt, kvb, nxt, _full_spec((1, KVW))],
                  out_specs=[pl.BlockSpec((BLOCK, ATT_COLS), lambda i: (i, 0)), _full_spec((1, HEAD_DIM))],
                  out_shape=[jax.ShapeDtypeStruct((T, ATT_COLS), F32), jax.ShapeDtypeStruct((1, HEAD_DIM), F32)],
                  compiler_params=_params(("arbitrary",)))(pa, dq, dko, dkn, dvo, dvn, kn_g)


WKV_CHUNK = 64
WKV_GROUP = 8


def _diag_mask():
    i = lax.broadcasted_iota(jnp.int32, (HEAD_DIM, RW), 0)
    j = lax.broadcasted_iota(jnp.int32, (HEAD_DIM, RW), 1) & (HEAD_DIM - 1)
    return i == j


def _heads_matrix():
    head = jnp.arange(RW // 2) // HEAD_DIM
    bd = (head[:, None] == head[None, :]).astype(BF16)
    return jnp.concatenate([bd, bd], axis=0)


def _headsums(xs, pieces, bd2):
    half = RW // 2
    bd = bd2[:pieces * half]
    rows = []
    for x in xs:
        parts, rest = [], x
        for n in range(pieces):
            p = rest.astype(BF16)
            parts.append(p)
            if n + 1 < pieces:
                rest = rest - p.astype(F32)
        for sl in (slice(0, half), slice(half, RW)):
            rows.append(jnp.concatenate([p[:, sl] for p in parts], axis=1))
    out = lax.dot_general(jnp.concatenate(rows, axis=0), bd, (((1,), (0,)), ((), ())), preferred_element_type=F32)
    return [jnp.concatenate([out[2 * n * HEAD_DIM:(2 * n + 1) * HEAD_DIM], out[(2 * n + 1) * HEAD_DIM:(2 * n + 2) * HEAD_DIM]],
                            axis=1) for n in range(len(xs))]


def _headsum(x):
    low = lax.broadcasted_iota(jnp.int32, (HEAD_DIM, LANE), 1) < HEAD_DIM
    tiles = []
    for c in range(RW // LANE):
        xt = x[:, c * LANE:(c + 1) * LANE]
        s_lo = jnp.sum(jnp.where(low, xt, 0.0), axis=1, keepdims=True)
        s_hi = jnp.sum(jnp.where(low, 0.0, xt), axis=1, keepdims=True)
        tiles.append(jnp.where(low, s_lo, s_hi))
    return jnp.concatenate(tiles, axis=1)


def _cols(rows, diag, bd2, pieces=2):
    return _headsums([jnp.where(diag, r, 0.0) for r in rows], pieces, bd2)


def _row(x, diag):
    return jnp.sum(jnp.where(diag, x, 0.0), axis=0, keepdims=True)


def wkv_fwd(r, w, k, v, a, b, *, name, gather=()):
    T = r.shape[0]
    ch = min(WKV_CHUNK, T)
    ngroups = ch // WKV_GROUP
    nchunks = T // ch
    ng = len(gather)

    def body(*refs):
        r_ref, w_ref, k_ref, v_ref, a_ref, b_ref, bd_ref = refs[:7]
        y_ref, st_ref = refs[7 + ng:9 + ng]
        s_scr = refs[9 + 2 * ng]
        step = pl.program_id(0)
        if ng:
            plan = _GatherPlan(refs[7:7 + ng], refs[9 + ng:9 + 2 * ng], refs[10 + 2 * ng:])
            pl.when(step == 0)(plan.start)
            pl.when(step == nchunks // 2)(plan.relay)

        @pl.when(step == 0)
        def _():
            s_scr[...] = jnp.zeros_like(s_scr)

        diag, bd2 = _diag_mask(), bd_ref[...]

        def group(gi, S):
            t0 = pl.multiple_of(gi * WKV_GROUP, WKV_GROUP)
            rows = pl.ds(t0, WKV_GROUP)
            R, W, K, V, A, B = (ref[rows, :] for ref in (r_ref, w_ref, k_ref, v_ref, a_ref, b_ref))
            vcols = _cols([V[s:s + 1] for s in range(WKV_GROUP)], diag, bd2, 1)
            yrows = []
            for s in range(WKV_GROUP):
                sa = _headsum(S * A[s:s + 1])
                S = S * W[s:s + 1] + sa * B[s:s + 1] + vcols[s] * K[s:s + 1]
                st_ref[t0 + s] = S
                yrows.append(_row(_headsums([S * R[s:s + 1]], 1, bd2)[0], diag))
            y_ref[rows, :] = jnp.concatenate(yrows, axis=0)
            return S

        s_scr[...] = lax.fori_loop(0, ngroups, group, s_scr[...])
        if ng:
            pl.when(step == nchunks - 1)(plan.finish_relayed)

    vec = pl.BlockSpec((ch, RW), lambda c: (c, 0))
    return _pcall(
        body, name=name, grid=(nchunks,), in_specs=[vec] * 6 + [_full_spec((RW, RW // 2))] + [_HBM] * ng,
        out_specs=[vec, pl.BlockSpec((ch, HEAD_DIM, RW), lambda c: (c, 0, 0))] + [_HBM] * ng,
        out_shape=[jax.ShapeDtypeStruct((T, RW), F32), jax.ShapeDtypeStruct((T, HEAD_DIM, RW), F32)] + _gathered_shapes(gather),
        scratch_shapes=[pltpu.VMEM((HEAD_DIM, RW), F32)] + (_GatherPlan.sems(ng) if ng else []),
        compiler_params=_params(("arbitrary",)),
    )(r, w, k, v, a, b, _heads_matrix(), *gather)


def wkv_bwd(r, w, k, v, a, b, dy, states, *, name, exchange=()):
    T = r.shape[0]
    ch = min(WKV_CHUNK, T)
    nchunks = T // ch
    ngroups = ch // WKV_GROUP
    ne = len(exchange)

    def body(*refs):
        r_ref, w_ref, k_ref, v_ref, a_ref, b_ref, dy_ref, st_ref, stp_ref, bd_ref = refs[:10]
        dr_ref, dw_ref, dk_ref, dv_ref, da_ref, db_ref = refs[10 + ne:16 + ne]
        ds_scr = refs[16 + 2 * ne]
        step = pl.program_id(0)
        if ne:
            plan = _ExchangePlan(refs[10:10 + ne], refs[16 + ne:16 + 2 * ne], refs[17 + 2 * ne:])
            pl.when(step == 0)(plan.start)

        @pl.when(step == 0)
        def _():
            ds_scr[...] = jnp.zeros_like(ds_scr)

        has_prev_chunk = step < nchunks - 1
        diag, bd2 = _diag_mask(), bd_ref[...]
        colsum = lambda x: jnp.sum(x, axis=0, keepdims=True)

        def group(gj, dS):
            gi = ngroups - 1 - gj
            t0 = pl.multiple_of(gi * WKV_GROUP, WKV_GROUP)
            rows = pl.ds(t0, WKV_GROUP)
            R, W, K, V, A, B, DY = (ref[rows, :] for ref in (r_ref, w_ref, k_ref, v_ref, a_ref, b_ref, dy_ref))
            before = jnp.where(gi > 0, st_ref[jnp.maximum(t0 - 1, 0)], jnp.where(has_prev_chunk, stp_ref[0], 0.0))
            prev_state = lambda s: st_ref[t0 + s - 1] if s > 0 else before
            steps = range(WKV_GROUP)
            dycols = _cols([DY[s:s + 1] for s in steps], diag, bd2, 1)
            vcols = _cols([V[s:s + 1] for s in steps], diag, bd2, 1)
            sas = _headsums([prev_state(s) * A[s:s + 1] for s in steps], 1, bd2)
            got = [[None] * WKV_GROUP for _ in range(6)]
            for s in reversed(steps):
                Sp = prev_state(s)
                dS = dS + dycols[s] * R[s:s + 1]
                got[0][s] = colsum(st_ref[t0 + s] * dycols[s])
                got[3][s] = _row(_headsums([dS * K[s:s + 1]], 1, bd2)[0], diag)
                got[2][s] = colsum(dS * vcols[s])
                dsa = _headsum(dS * B[s:s + 1])
                got[5][s] = colsum(dS * sas[s])
                got[1][s] = colsum(dS * Sp)
                got[4][s] = colsum(Sp * dsa)
                dS = dS * W[s:s + 1] + dsa * A[s:s + 1]
            for q, ref in enumerate((dr_ref, dw_ref, dk_ref, dv_ref, da_ref, db_ref)):
                ref[rows, :] = jnp.concatenate(got[q], axis=0)
            return dS

        ds_scr[...] = lax.fori_loop(0, ngroups, group, ds_scr[...])
        if ne:
            pl.when(step == nchunks - 1)(plan.finish)

    vec = pl.BlockSpec((ch, RW), lambda c: (nchunks - 1 - c, 0))
    st_spec = pl.BlockSpec((ch, HEAD_DIM, RW), lambda c: (nchunks - 1 - c, 0, 0))
    stp_spec = pl.BlockSpec((1, HEAD_DIM, RW), lambda c: (jnp.maximum((nchunks - 1 - c) * ch - 1, 0), 0, 0))
    return _pcall(
        body, name=name, grid=(nchunks,), in_specs=[vec] * 7 + [st_spec, stp_spec, _full_spec((RW, RW // 2))] + [_HBM] * ne,
        out_specs=[vec] * 6 + [_HBM] * ne,
        out_shape=[jax.ShapeDtypeStruct((T, RW), F32)] * 6 + [jax.ShapeDtypeStruct(e.shape, e.dtype) for e in exchange],
        scratch_shapes=[pltpu.VMEM((HEAD_DIM, RW), F32)] + (_ExchangePlan.sems(ne) if ne else []),
        compiler_params=_params(("arbitrary",)),
    )(r, w, k, v, a, b, dy, states, states, _heads_matrix(), *exchange)


_HBM = pl.BlockSpec(memory_space=pltpu.HBM)
_MESH = pl.DeviceIdType.MESH


def _place():
    x, y, c = lax.axis_index("x"), lax.axis_index("y"), lax.axis_index("c")
    return x, y, c, [(1 - x, y), (x, 1 - y), (1 - x, 1 - y)]


def _remote(src, dst, send_sem, recv_sem, to):
    return pltpu.make_async_remote_copy(src_ref=src, dst_ref=dst, send_sem=send_sem, recv_sem=recv_sem, device_id=to,
                                        device_id_type=_MESH)


def _dma_sems(*counts):
    return [pltpu.SemaphoreType.DMA((n,)) for n in counts]


class _GatherPlan:
    def __init__(self, ins, outs, sems):
        self.ins, self.outs, self.n = ins, outs, len(ins)
        self.ici_send, self.ici_recv, self.d2d_send, self.d2d_recv, self.local_sems = sems
        x, y, c, chips = _place()
        self.c, self.me, self.sibling = c, 2 * x + y, (x, y, 1 - c)
        self.peers = [(2 * qx + qy, (qx, qy, c)) for qx, qy in chips]

    @staticmethod
    def sems(n):
        return _dma_sems(3 * n, 3 * n, 3 * n, 3 * n, n)

    def _half(self, i, which):
        rh = self.ins[i].shape[0] // 2
        return pl.ds(which * rh, rh)

    def _local(self, i):
        return pltpu.make_async_copy(self.ins[i], self.outs[i].at[self.me], self.local_sems.at[i])

    def _send(self, i, j):
        k, mine = 3 * i + j, self._half(i, self.c)
        return _remote(self.ins[i].at[mine], self.outs[i].at[self.me, mine], self.ici_send.at[k], self.ici_recv.at[k],
                       self.peers[j][1])

    def _landed(self, i, j):
        k, piece = 3 * i + j, self.outs[i].at[self.peers[j][0], self._half(i, self.c)]
        return _remote(piece, piece, self.ici_send.at[k], self.ici_recv.at[k], self.peers[j][1])

    def _pass(self, i, j, which):
        k, piece = 3 * i + j, self.outs[i].at[self.peers[j][0], self._half(i, which)]
        return _remote(piece, piece, self.d2d_send.at[k], self.d2d_recv.at[k], self.sibling)

    def _all(self):
        return [(i, j) for i in range(self.n) for j in range(3)]

    def start(self):
        for i in range(self.n):
            self._local(i).start()
        for i, j in self._all():
            self._send(i, j).start()

    def relay(self):
        for i, j in self._all():
            self._landed(i, j).wait_recv()
            self._pass(i, j, self.c).start()

    def finish_relayed(self):
        for i, j in self._all():
            self._pass(i, j, 1 - self.c).wait_recv()
        for i, j in self._all():
            self._send(i, j).wait_send()
            self._pass(i, j, self.c).wait_send()
        for i in range(self.n):
            self._local(i).wait()

    def finish(self):
        self.relay()
        self.finish_relayed()

    @staticmethod
    def out_shapes(shards):
        return _gathered_shapes(shards)


def _gathered_shapes(shards):
    return [jax.ShapeDtypeStruct((N_CHIPS,) + s.shape, s.dtype) for s in shards]


def gather_weights(shards, *, name):
    n = len(shards)

    def body(*refs):
        plan = _GatherPlan(refs[:n], refs[n:2 * n], refs[2 * n:])
        plan.start()
        plan.finish()

    return _pcall(body, name=name, in_specs=[_HBM] * n, out_specs=[_HBM] * n, out_shape=_gathered_shapes(shards),
                  scratch_shapes=_GatherPlan.sems(n), compiler_params=_params())(*shards)


class _SiblingPlan:
    halves = True

    def __init__(self, ins, outs, sems):
        self.ins, self.outs, self.n = ins, outs, len(ins)
        self.send_sems, self.recv_sems = sems
        x, y, c, _ = _place()
        self.c, self.sibling = c, (x, y, 1 - c)

    @staticmethod
    def sems(n):
        return _dma_sems(n, n)

    @classmethod
    def out_shapes(cls, arrays):
        if not cls.halves:
            return [jax.ShapeDtypeStruct(a.shape, a.dtype) for a in arrays]
        return [jax.ShapeDtypeStruct((a.shape[0], a.shape[1] // 2, a.shape[2]), a.dtype) for a in arrays]

    def _copy(self, i):
        src = self.ins[i]
        if self.halves:
            rh = src.shape[1] // 2
            src = src.at[:, pl.ds((1 - self.c) * rh, rh)]
        return _remote(src, self.outs[i], self.send_sems.at[i], self.recv_sems.at[i], self.sibling)

    def start(self):
        for i in range(self.n):
            self._copy(i).start()

    def finish(self):
        for i in range(self.n):
            self._copy(i).wait_recv()
        for i in range(self.n):
            self._copy(i).wait_send()


class _SiblingWhole(_SiblingPlan):
    halves = False


def to_sibling(arrays, take_other_half, *, name):
    n = len(arrays)
    plan_cls = _SiblingPlan if take_other_half else _SiblingWhole

    def body(*refs):
        plan = plan_cls(refs[:n], refs[n:2 * n], refs[2 * n:])
        plan.start()
        plan.finish()

    return _pcall(body, name=name, in_specs=[_HBM] * n, out_specs=[_HBM] * n, out_shape=plan_cls.out_shapes(arrays),
                  scratch_shapes=plan_cls.sems(n), compiler_params=_params())(*arrays)


def exchange_chips(arrays, *, name):
    n = len(arrays)

    def body(*refs):
        plan = _ExchangePlan(refs[:n], refs[n:2 * n], refs[2 * n:])
        plan.start()
        plan.finish()

    return _pcall(body, name=name, in_specs=[_HBM] * n, out_specs=[_HBM] * n,
                  out_shape=[jax.ShapeDtypeStruct(a.shape, a.dtype) for a in arrays],
                  scratch_shapes=_ExchangePlan.sems(n), compiler_params=_params())(*arrays)


class _ExchangePlan:
    def __init__(self, ins, outs, sems):
        self.ins, self.outs, self.n = ins, outs, len(ins)
        self.send_sems, self.recv_sems, self.local_sems = sems
        x, y, c, chips = _place()
        self.me = 2 * x + y
        self.peers = [(2 * qx + qy, (qx, qy, c)) for qx, qy in chips]

    @staticmethod
    def sems(n):
        return _dma_sems(3 * n, 3 * n, n)

    @staticmethod
    def out_shapes(arrays):
        return [jax.ShapeDtypeStruct(a.shape, a.dtype) for a in arrays]

    def _local(self, i):
        return pltpu.make_async_copy(self.ins[i].at[self.me], self.outs[i].at[self.me], self.local_sems.at[i])

    def _send(self, i, j):
        k = 3 * i + j
        return _remote(self.ins[i].at[self.peers[j][0]], self.outs[i].at[self.me], self.send_sems.at[k], self.recv_sems.at[k],
                       self.peers[j][1])

    def _landed(self, i, j):
        k, piece = 3 * i + j, self.outs[i].at[self.peers[j][0]]
        return _remote(piece, piece, self.send_sems.at[k], self.recv_sems.at[k], self.peers[j][1])

    def start(self):
        for i in range(self.n):
            self._local(i).start()
            for j in range(3):
                self._send(i, j).start()

    def finish(self):
        for i in range(self.n):
            for j in range(3):
                self._landed(i, j).wait_recv()
        for i in range(self.n):
            for j in range(3):
                self._send(i, j).wait_send()
            self._local(i).wait()


def _core_index():
    return lax.axis_index("c").astype(jnp.int32).reshape(1)


def pair_sum(g, theirs, wire_dtype, *, name):
    _, R, C = g.shape
    rh = R // 2
    tr = _tile(rh, 256, 16)
    nt = rh // tr

    def body(c_ref, g_ref, t_ref, q_ref, qw_ref):
        q = g_ref[...] + t_ref[...]
        q_ref[...] = q
        qw_ref[...] = q.astype(wire_dtype)

    blk = pl.BlockSpec((1, tr, C), lambda b, i, c_ref: (b, i, 0))
    mine = pl.BlockSpec((1, tr, C), lambda b, i, c_ref: (b, c_ref[0] * nt + i, 0))
    grid_spec = pltpu.PrefetchScalarGridSpec(num_scalar_prefetch=1, grid=(N_CHIPS, nt), in_specs=[mine, blk], out_specs=[blk, blk])
    return _pcall(body, name=name, grid_spec=grid_spec,
                  out_shape=[jax.ShapeDtypeStruct((N_CHIPS, rh, C), F32), jax.ShapeDtypeStruct((N_CHIPS, rh, C), wire_dtype)],
                  compiler_params=_params(("parallel", "parallel")))(_core_index(), g, theirs)


def half_sum(own, landed, *, name):
    _, rh, C = own.shape
    tr = _tile(rh, 256, 16)

    def body(me_ref, own_ref, land_ref, o_ref):
        total = None
        for p in range(N_CHIPS):
            term = jnp.where(me_ref[0] == p, own_ref[p], land_ref[p].astype(F32))
            total = term if total is None else total + term
        o_ref[...] = total

    blk = pl.BlockSpec((N_CHIPS, tr, C), lambda i, me_ref: (0, i, 0))
    grid_spec = pltpu.PrefetchScalarGridSpec(num_scalar_prefetch=1, grid=(rh // tr,), in_specs=[blk, blk],
                                             out_specs=pl.BlockSpec((tr, C), lambda i, me_ref: (i, 0)))
    me = (2 * lax.axis_index("x") + lax.axis_index("y")).astype(jnp.int32).reshape(1)
    return _pcall(body, name=name, grid_spec=grid_spec, out_shape=jax.ShapeDtypeStruct((rh, C), F32),
                  compiler_params=_params(("parallel",)))(me, own, landed)


def adamw(w, m, v, mine, theirs, *, name):
    _, R, C = w.shape
    rh = R // 2
    tr = _tile(rh, 256, 8)
    nt = rh // tr

    def body(c_ref, w_ref, m_ref, v_ref, a_ref, b_ref, g_ref, d_ref, nm_ref, nv_ref):
        is_mine = (pl.program_id(0) // nt) == c_ref[0]
        g = jnp.where(is_mine, a_ref[...], b_ref[...])
        g_ref[...] = g
        nm = ADAM_B1 * m_ref[...] + (1.0 - ADAM_B1) * g
        nv = ADAM_B2 * v_ref[...] + (1.0 - ADAM_B2) * (g * g)
        nm_ref[...] = nm
        nv_ref[...] = nv
        m_hat = nm / (1.0 - ADAM_B1 ** ADAM_STEP)
        v_hat = nv / (1.0 - ADAM_B2 ** ADAM_STEP)
        d_ref[...] = -ADAM_LR * (m_hat / (jnp.sqrt(v_hat) + ADAM_EPS) + ADAM_WD * w_ref[...])

    full = pl.BlockSpec((None, tr, C), lambda i, c_ref: (0, i, 0))
    a_spec = pl.BlockSpec((tr, C), lambda i, c_ref: (jnp.clip(i - c_ref[0] * nt, 0, nt - 1), 0))
    b_spec = pl.BlockSpec((tr, C), lambda i, c_ref: (jnp.clip(i - (1 - c_ref[0]) * nt, 0, nt - 1), 0))
    grid_spec = pltpu.PrefetchScalarGridSpec(num_scalar_prefetch=1, grid=(2 * nt,), in_specs=[full] * 3 + [a_spec, b_spec],
                                             out_specs=[full] * 4)
    return _pcall(body, name=name, grid_spec=grid_spec, out_shape=[jax.ShapeDtypeStruct((1, R, C), F32)] * 4,
                  compiler_params=_params(("arbitrary",)))(_core_index(), w, m, v, mine, theirs)


def _to_blocks(full, axis):
    r, c = full.shape
    if axis == 1:
        return full.reshape(r, N_CHIPS, c // N_CHIPS).transpose(1, 0, 2)
    return full.reshape(N_CHIPS, r // N_CHIPS, c)


def _from_blocks(blocks, axis):
    _, r, c = blocks.shape
    if axis == 1:
        return blocks.transpose(1, 0, 2).reshape(r, N_CHIPS * c)
    return blocks.reshape(N_CHIPS * r, c)


def _ffn_fwd(x, norm, wg_t, wu_t, wd, tag):
    h = rms_fwd(x, norm, name=tag + "_norm")
    gate, up, act = mm_fused(h, [wg_t, wu_t], _swiglu, [BF16] * 3, tb=True, name=tag + "_gate_up")
    out = mm(act, wd, scale=0.5, res=x, name=tag + "_down")
    return out, (h, gate, up, act)


def _ffn_bwd(dout, x, saved, norm, wg_t, wu_t, wd, tag, carry=None, reduce=None):
    h, gate, up, act = saved
    dgate, dup, *carried = mm_fused(dout, [wd], _swiglu_bwd, [BF16] * 2, tb=True, extras=[gate, up], name=tag + "_dact",
                                    carry=carry)
    dwd = mm(act, dout, ta=True, scale=0.5, name=tag + "_dwd")
    dwg_t = mm(dgate, h, ta=True, name=tag + "_dwg")
    dwu_t = mm(dup, h, ta=True, name=tag + "_dwu")
    reduced = None
    if reduce:
        blocks = [_to_blocks(g, 0) for g in (dwg_t, dwu_t, dwd)]
        dh, *from_sibling = mm(dgate, wg_t, name=tag + "_dh_gate", carry=(_SiblingPlan, blocks))
        pair = [pair_sum(g, t, BF16, name="pair_sum_" + n) for g, t, n in zip(blocks, from_sibling, reduce)]
        dh, *landed = mm(dup, wu_t, res=dh, name=tag + "_dh_up", carry=(_ExchangePlan, [q for _, q in pair]))
        reduced = (pair, landed)
    else:
        dh = mm(dgate, wg_t, name=tag + "_dh_gate")
        dh = mm(dup, wu_t, res=dh, name=tag + "_dh_up")
    dx, dnorm = rms_bwd(dh, x, norm, dout, name=tag + "_dnorm")
    return dx, dnorm, dwg_t, dwu_t, dwd, carried, reduced


TRANSPOSED = ('ffn1_w_gate', 'ffn1_w_up', 'ffn2_w_gate', 'ffn2_w_up')
FIRST_WEIGHTS = ['ffn1_w_gate', 'ffn1_w_up', 'ffn1_w_down']
MID_WEIGHTS = ['w_in', 'rwkv_w_lora_up', 'rwkv_a_lora_up', 'rwkv_g_lora_up']
LATE_WEIGHTS = ['w_branch_rwkv', 'w_branch_attn', 'w_out', 'ffn2_w_gate', 'ffn2_w_up', 'ffn2_w_down']


def _pair_sums(names, blocks, tag):
    from_sibling = to_sibling(blocks, True, name=tag + "_grads_to_sibling")
    return [pair_sum(g, t, F32 if n == 'small' else BF16, name="pair_sum_" + n)
            for g, t, n in zip(blocks, from_sibling, names)]


def _step(A):
    x, tgt = A['x'][0], A['loss_target'][0]
    T = x.shape[0]
    w = {n: A[n][0] for n in WEIGHT_NAMES}
    row = lambda a: a.reshape(1, -1)

    axis_of = {n: (0 if n in TRANSPOSED else axis) for n, axis in BIG}
    natural = lambda n, a: jnp.swapaxes(a, 1, 2) if n in TRANSPOSED else a
    shard = lambda n: natural(n, A[n])[0].astype(BF16)
    n1, nmix, n2, nfin = (row(w[n]) for n in ('ffn1_norm', 'mix_norm', 'ffn2_norm', 'final_norm'))
    gathered = gather_weights([shard(n) for n in FIRST_WEIGHTS[:2]], name="gather_weights")
    full = {n: _from_blocks(b, axis_of[n]) for n, b in zip(FIRST_WEIGHTS[:2], gathered)}
    h1 = rms_fwd(x, n1, name="ffn1_norm")
    gate1, up1, act1, down_blocks = mm_fused(h1, [full['ffn1_w_gate'], full['ffn1_w_up']], _swiglu, [BF16] * 3, tb=True,
                                             name="ffn1_gate_up", carry=(_GatherPlan, [shard('ffn1_w_down')]))
    full['ffn1_w_down'] = _from_blocks(down_blocks, axis_of['ffn1_w_down'])
    x1, *gathered = mm(act1, full['ffn1_w_down'], scale=0.5, res=x, name="ffn1_down",
                       carry=(_GatherPlan, [shard(n) for n in MID_WEIGHTS]))
    ffn1 = (h1, gate1, up1, act1)
    full.update({n: _from_blocks(b, axis_of[n]) for n, b in zip(MID_WEIGHTS, gathered)})
    w_in_r = _pad_rwkv_cols(full['w_in'][:, :RWKV_COLS])
    w_in_a = full['w_in'][:, RWKV_COLS:RWKV_COLS + ATT_COLS]
    w_in_g = full['w_in'][:, RWKV_COLS + ATT_COLS:]
    wlw, wla, wlg = (_pad_rows(full[n], 128).astype(F32) for n in ('rwkv_w_lora_up', 'rwkv_a_lora_up', 'rwkv_g_lora_up'))
    mu = _pad_rwkv_cols(row(w['rwkv_mu']))
    w0, a0, k_k, k_a, r_k, ln_w, ln_b = (row(w[n]) for n in ('rwkv_w0', 'rwkv_a0', 'rwkv_k_k', 'rwkv_k_a', 'rwkv_r_k',
                                                               'rwkv_ln_w', 'rwkv_ln_b'))
    qg = jnp.tile(row(w['attn_q_norm']), (1, N_HEADS))
    kg = jnp.tile(row(w['attn_k_norm']), (1, KVW // HEAD_DIM))
    sinks = jnp.pad(row(w['attn_sinks']), ((0, 0), (0, LANE - N_HEADS)))

    h2 = rms_fwd(x1, nmix, name="mix_norm")
    pr = mm(h2, w_in_r, name="proj_rwkv")
    pa = mm(h2, w_in_a, name="proj_att")
    pg = mm(h2, w_in_g, name="proj_gate")
    pr_shift = jnp.pad(pr, ((1, 0), (0, 0)))[:-1]
    r, dec, k2, v, a, b, sg = rwkv_pre_fwd(pr, pr_shift, mu, w0, a0, k_k, k_a, wlw, wla, wlg, name="rwkv_pre")
    y, states, *gathered = wkv_fwd(r, dec, k2, v, a, b, name="wkv_fwd", gather=[shard(n) for n in LATE_WEIGHTS])
    full.update({n: _from_blocks(b, axis_of[n]) for n, b in zip(LATE_WEIGHTS, gathered)})
    yr = rwkv_post_fwd(y, r, k2, v, sg, wlg, ln_w, ln_b, r_k, name="rwkv_post")
    ya = att_fwd(pa, qg, kg, sinks, name="att_fwd")
    br = mm(yr, full['w_branch_rwkv'], name="branch_rwkv")
    ba = mm(ya, full['w_branch_attn'], name="branch_att")
    mg = merge_fwd(br, ba, pg, name="merge")
    x2 = mm(mg, full['w_out'], res=x1, name="mix_out")
    x3, ffn2 = _ffn_fwd(x2, n2, full['ffn2_w_gate'], full['ffn2_w_up'], full['ffn2_w_down'], "ffn2")
    dx3, d_nfin, loss = final_loss(x3, tgt, nfin, name="final_loss")

    G = {'final_norm': d_nfin}
    dx2, G['ffn2_norm'], G['ffn2_w_gate'], G['ffn2_w_up'], G['ffn2_w_down'], _, _ = _ffn_bwd(
        dx3, x2, ffn2, n2, full['ffn2_w_gate'], full['ffn2_w_up'], full['ffn2_w_down'], "ffn2")
    dmg = mm(dx2, full['w_out'], tb=True, name="d_merge")
    G['w_out'] = mm(mg, dx2, ta=True, name="d_w_out")
    dbr, dba, dpg = merge_bwd(dmg, br, ba, pg, name="merge_bwd")
    dyr = mm(dbr, full['w_branch_rwkv'], tb=True, name="d_y_rwkv")
    G['w_branch_rwkv'] = mm(yr, dbr, ta=True, name="d_w_branch_rwkv")
    dya = mm(dba, full['w_branch_attn'], tb=True, name="d_y_att")
    G['w_branch_attn'] = mm(ya, dba, ta=True, name="d_w_branch_att")
    late_blocks = [_to_blocks(G[n], axis_of[n]) for n in LATE_WEIGHTS]
    dy, dz, dg, G['rwkv_ln_w'], G['rwkv_ln_b'], *late_from_sibling = rwkv_post_bwd(
        dyr, y, r, k2, v, sg, wlg, ln_w, ln_b, r_k, name="rwkv_post_bwd", carry=(_SiblingPlan, late_blocks))
    late_pair = [pair_sum(g, t, BF16, name="pair_sum_" + n) for g, t, n in zip(late_blocks, late_from_sibling, LATE_WEIGHTS)]
    res = wkv_bwd(r, dec, k2, v, a, b, dy, states, name="wkv_bwd", exchange=[q for _, q in late_pair])
    wkv_grads, late_landed = res[:6], res[6:]
    (dpr, d_mu, G['rwkv_w0'], G['rwkv_a0'], G['rwkv_k_k'], G['rwkv_k_a'], G['rwkv_r_k'], d_wlw, d_wla, d_wlg) = rwkv_pre_bwd(
        pr, pr_shift, *wkv_grads, dz, dg, mu, w0, a0, k_k, k_a, r_k, wlw, wla, wlg, name="rwkv_pre_bwd")
    G['rwkv_mu'] = _unpad_rwkv_cols(d_mu)
    G['rwkv_w_lora_up'], G['rwkv_a_lora_up'], G['rwkv_g_lora_up'] = d_wlw[:DECAY_LORA], d_wla[:ICLR_LORA], d_wlg[:GATE_LORA]
    dq, dko, dkn, dvo, dvn, G['attn_q_norm'], d_sinks = att_bwd(pa, dya, qg, kg, sinks, name="att_bwd")
    G['attn_sinks'] = d_sinks[:, :N_HEADS]
    dpa, G['attn_k_norm'] = att_kv_bwd(pa, dq, dko, dkn, dvo, dvn, kg, name="att_kv_bwd")
    d_w_in_r = mm(h2, dpr, ta=True, name="d_w_in_rwkv")
    d_w_in_a = mm(h2, dpa, ta=True, name="d_w_in_att")
    d_w_in_g = mm(h2, dpg, ta=True, name="d_w_in_gate")
    G['w_in'] = jnp.concatenate([_unpad_rwkv_cols(d_w_in_r), d_w_in_a, d_w_in_g], axis=1)
    mid_blocks = [_to_blocks(G[n], axis_of[n]) for n in MID_WEIGHTS]
    dh2, *mid_from_sibling = mm(dpr, w_in_r, tb=True, name="d_h2_rwkv", carry=(_SiblingPlan, mid_blocks))
    dh2 = mm(dpa, w_in_a, tb=True, res=dh2, name="d_h2_att")
    dh2 = mm(dpg, w_in_g, tb=True, res=dh2, name="d_h2_gate")
    dx1, G['mix_norm'] = rms_bwd(dh2, x1, nmix, dx2, name="d_mix_norm")
    mid_pair = [pair_sum(g, t, BF16, name="pair_sum_" + n) for g, t, n in zip(mid_blocks, mid_from_sibling, MID_WEIGHTS)]
    dx0, G['ffn1_norm'], _, _, _, mid_landed, (first_pair, first_landed) = _ffn_bwd(
        dx1, x, ffn1, n1, full['ffn1_w_gate'], full['ffn1_w_up'], full['ffn1_w_down'], "ffn1",
        carry=(_ExchangePlan, [q for _, q in mid_pair]), reduce=FIRST_WEIGHTS)

    small_shapes = [(w[n].size,) for n in SMALL] + [(1,)]

    def small_rows(parts):
        vec = jnp.concatenate([p.reshape(-1) for p in parts])
        return jnp.pad(vec, (0, SMALL_ROWS * FLAT_W - vec.shape[0])).reshape(SMALL_ROWS, FLAT_W)

    small = small_rows([G[n] for n in SMALL] + [loss[0, :1]])
    small_pair = _pair_sums(['small'], [jnp.broadcast_to(small[None], (N_CHIPS,) + small.shape)], "small")
    small_landed = exchange_chips([q for _, q in small_pair], name="exchange_small")
    names = FIRST_WEIGHTS + ['small'] + MID_WEIGHTS + LATE_WEIGHTS
    pair = first_pair + small_pair + mid_pair + late_pair
    landed = list(first_landed) + list(small_landed) + list(mid_landed) + list(late_landed)
    halves = [half_sum(own, l, name="half_sum_" + n) for (own, _), l, n in zip(pair, landed, names)]
    other_halves = to_sibling(halves, False, name="halves_to_sibling")

    def local(prefix, n):
        if n != 'small':
            return natural(n, A[prefix + n])
        return small_rows([A[prefix + s] for s in SMALL] + [jnp.zeros((1,), F32)])[None]

    result = {}
    for n, mine, theirs in zip(names, halves, other_halves):
        outs4 = adamw(local('', n), local('m_', n), local('v_', n), mine, theirs, name="adamw_" + n)
        for kind, o in zip(('grad_', 'delta_', 'new_m_', 'new_v_'), outs4):
            if n != 'small':
                result[kind + n] = natural(n, o)
            else:
                for s, part in zip(SMALL + ['loss'], _unpack_vec(o.reshape(-1), small_shapes)):
                    result[kind + s] = part.reshape(A[s].shape) if s != 'loss' else part.reshape(())
    outs = [result['grad_loss'], dx0[None]]
    for kind in ('grad_', 'delta_', 'new_m_', 'new_v_'):
        outs += [result[kind + n] for n in WEIGHT_NAMES]
    return tuple(outs)


def _unpack_vec(vec, shapes):
    out, off = [], 0
    for (n,) in shapes:
        out.append(vec[off:off + n])
        off += n
    return out


def kernel(x, ffn1_norm, ffn1_w_gate, ffn1_w_up, ffn1_w_down, mix_norm, w_in, rwkv_mu, rwkv_w0, rwkv_w_lora_up, rwkv_a0, rwkv_a_lora_up, rwkv_g_lora_up, rwkv_k_k, rwkv_k_a, rwkv_r_k, rwkv_ln_w, rwkv_ln_b, attn_q_norm, attn_k_norm, attn_sinks, w_branch_rwkv, w_branch_attn, w_out, ffn2_norm, ffn2_w_gate, ffn2_w_up, ffn2_w_down, final_norm, loss_target, m_ffn1_norm, m_ffn1_w_gate, m_ffn1_w_up, m_ffn1_w_down, m_mix_norm, m_w_in, m_rwkv_mu, m_rwkv_w0, m_rwkv_w_lora_up, m_rwkv_a0, m_rwkv_a_lora_up, m_rwkv_g_lora_up, m_rwkv_k_k, m_rwkv_k_a, m_rwkv_r_k, m_rwkv_ln_w, m_rwkv_ln_b, m_attn_q_norm, m_attn_k_norm, m_attn_sinks, m_w_branch_rwkv, m_w_branch_attn, m_w_out, m_ffn2_norm, m_ffn2_w_gate, m_ffn2_w_up, m_ffn2_w_down, m_final_norm, v_ffn1_norm, v_ffn1_w_gate, v_ffn1_w_up, v_ffn1_w_down, v_mix_norm, v_w_in, v_rwkv_mu, v_rwkv_w0, v_rwkv_w_lora_up, v_rwkv_a0, v_rwkv_a_lora_up, v_rwkv_g_lora_up, v_rwkv_k_k, v_rwkv_k_a, v_rwkv_r_k, v_rwkv_ln_w, v_rwkv_ln_b, v_attn_q_norm, v_attn_k_norm, v_attn_sinks, v_w_branch_rwkv, v_w_branch_attn, v_w_out, v_ffn2_norm, v_ffn2_w_gate, v_ffn2_w_up, v_ffn2_w_down, v_final_norm):
    return _step(dict(locals()))
```

```python
import functools

import jax
import jax.numpy as jnp
from jax import lax
from jax.experimental import pallas as pl
from jax.experimental.pallas import tpu as pltpu

F32 = jnp.float32
BF16 = jnp.bfloat16

D_MODEL = 1024
D_FF = 2816
HEAD_DIM = 64
N_HEADS = 8
RW = 512
KVW = 128
ATT_GROUP = 4
WINDOW = 128
BLOCK = 128
DECAY_LORA, ICLR_LORA, GATE_LORA = 32, 32, 96
RWKV_COLS = 3 * RW + DECAY_LORA + ICLR_LORA + GATE_LORA
ATT_COLS = RW + 2 * KVW
GATE_COLS = 2 * D_MODEL
RWKV_PAD = 3 * RW + 3 * 128
RMS_EPS = 1e-6
GN_EPS = 64e-5
N_CHIPS = 4
LANE = 128
FLAT_W = 1024
SMALL_ROWS = 32
NEG_BIG = -1e30

ADAM_LR, ADAM_B1, ADAM_B2, ADAM_EPS, ADAM_WD, ADAM_STEP = 0.001, 0.9, 0.999, 1e-08, 0.01, 10

VMEM_LIMIT = 56 * 1024 * 1024

WEIGHT_NAMES = ['ffn1_norm', 'ffn1_w_gate', 'ffn1_w_up', 'ffn1_w_down', 'mix_norm', 'w_in', 'rwkv_mu', 'rwkv_w0',
                'rwkv_w_lora_up', 'rwkv_a0', 'rwkv_a_lora_up', 'rwkv_g_lora_up', 'rwkv_k_k', 'rwkv_k_a', 'rwkv_r_k',
                'rwkv_ln_w', 'rwkv_ln_b', 'attn_q_norm', 'attn_k_norm', 'attn_sinks', 'w_branch_rwkv',
                'w_branch_attn', 'w_out', 'ffn2_norm', 'ffn2_w_gate', 'ffn2_w_up', 'ffn2_w_down', 'final_norm']
BIG = [('ffn1_w_gate', 1), ('ffn1_w_up', 1), ('ffn1_w_down', 0), ('w_in', 1), ('rwkv_w_lora_up', 1),
       ('rwkv_a_lora_up', 1), ('rwkv_g_lora_up', 1), ('w_branch_rwkv', 1), ('w_branch_attn', 1), ('w_out', 0),
       ('ffn2_w_gate', 1), ('ffn2_w_up', 1), ('ffn2_w_down', 0)]
SMALL = ['ffn1_norm', 'mix_norm', 'rwkv_mu', 'rwkv_w0', 'rwkv_a0', 'rwkv_k_k', 'rwkv_k_a', 'rwkv_r_k', 'rwkv_ln_w',
         'rwkv_ln_b', 'attn_q_norm', 'attn_k_norm', 'attn_sinks', 'ffn2_norm', 'final_norm']


def _pcall(body, **kw):
    return pl.pallas_call(body, **kw)


def _params(sem=None, **kw):
    if sem is not None:
        kw['dimension_semantics'] = sem
    return pltpu.CompilerParams(vmem_limit_bytes=VMEM_LIMIT, **kw)


def _tile(n, cap, mult):
    best = None
    for t in range(mult, min(n, cap) + 1, mult):
        if n % t == 0:
            best = t
    return best or n


def _sigmoid(z):
    return 1.0 / (1.0 + jnp.exp(-z))


def _softplus(z):
    return jnp.maximum(z, 0.0) + jnp.log(1.0 + jnp.exp(-jnp.abs(z)))


def _bdot(a, b, dims=(((1,), (0,)), ((), ()))):
    return lax.dot_general(a.astype(BF16), b.astype(BF16), dims, preferred_element_type=F32)


_NT = (((1,), (1,)), ((), ()))
_TN = (((0,), (0,)), ((), ()))


def _segsum(x, bd):
    hi = x.astype(BF16)
    lo = (x - hi.astype(F32)).astype(BF16)
    dot = functools.partial(lax.dot_general, dimension_numbers=(((1,), (0,)), ((), ())), preferred_element_type=F32)
    return dot(hi, bd) + dot(lo, bd)


_LORA_EDGES = (3 * RW, 3 * RW + DECAY_LORA, 3 * RW + DECAY_LORA + ICLR_LORA, RWKV_COLS)


def _pad_rwkv_cols(x):
    parts = [x[..., :3 * RW]]
    for lo, hi in zip(_LORA_EDGES[:-1], _LORA_EDGES[1:]):
        parts.append(jnp.pad(x[..., lo:hi], [(0, 0)] * (x.ndim - 1) + [(0, 128 - (hi - lo))]))
    return jnp.concatenate(parts, axis=-1)


def _unpad_rwkv_cols(x):
    parts = [x[..., :3 * RW]]
    for j, (lo, hi) in enumerate(zip(_LORA_EDGES[:-1], _LORA_EDGES[1:])):
        parts.append(x[..., 3 * RW + 128 * j:3 * RW + 128 * j + (hi - lo)])
    return jnp.concatenate(parts, axis=-1)


def _pad_rows(x, rows):
    return jnp.pad(x, [(0, rows - x.shape[0])] + [(0, 0)] * (x.ndim - 1))


def mm(a, b, *, name, ta=False, tb=False, scale=None, res=None, out_dtype=F32, carry=None, norm_bwd=None):
    M, K = (a.shape[1], a.shape[0]) if ta else a.shape
    N = b.shape[0] if tb else b.shape[1]
    assert (b.shape[1] if tb else b.shape[0]) == K
    tm, tn, tk = _tile(M, 1408 if ta else 512, 128), _tile(N, 1408, 128), _tile(K, 1408, 128)
    nk = K // tk
    grid = (M // tm, N // tn, nk)
    dims = (((0 if ta else 1,), (1 if tb else 0,)), ((), ()))
    plan_cls, carried = carry if carry else (None, ())
    nn = 3 if norm_bwd else 0
    nd = 1 if norm_bwd else 0
    nc, nin = len(carried), 2 + (res is not None) + nn
    assert not norm_bwd or (tn == N and out_dtype == F32)

    def body(*refs):
        a_ref, b_ref = refs[:2]
        r_ref = refs[2] if res is not None else None
        o_ref, acc_ref = refs[nin + nc], refs[nin + 2 * nc + 1 + nd]
        row_tile, k = pl.program_id(0), pl.program_id(2)
        if nc:
            plan = plan_cls(refs[nin:nin + nc], refs[nin + nc + 1 + nd:nin + 2 * nc + 1 + nd], refs[nin + 2 * nc + 2 + nd:])
            at = lambda which: functools.reduce(jnp.logical_and, [pl.program_id(d) == (0 if which == 0 else grid[d] - 1)
                                                                 for d in range(3)])
            pl.when(at(0))(plan.start)
        part = _bdot(a_ref[...], b_ref[...], dims)

        @pl.when(k == 0)
        def _():
            acc_ref[...] = part

        @pl.when(k > 0)
        def _():
            acc_ref[...] += part

        @pl.when(k == nk - 1)
        def _():
            o = acc_ref[...]
            if scale is not None:
                o = o * scale
            if r_ref is not None:
                o = o + r_ref[...].astype(F32)
            if norm_bwd:
                x_ref, g_ref, dres_ref = refs[nin - 3:nin]
                xv = x_ref[...]
                r = lax.rsqrt(jnp.mean(xv * xv, axis=-1, keepdims=True) + RMS_EPS)
                xh = xv * r
                dxh = o * g_ref[...]
                _acc_rows(refs[nin + nc + 1], jnp.sum(o * xh, axis=0, keepdims=True), row_tile)
                o = dres_ref[...] + r * (dxh - xh * jnp.mean(dxh * xh, axis=-1, keepdims=True))
            o_ref[...] = o.astype(out_dtype)

        if nc:
            pl.when(at(1))(plan.finish)

    a_spec = pl.BlockSpec((tk, tm), lambda i, j, k: (k, i)) if ta else pl.BlockSpec((tm, tk), lambda i, j, k: (i, k))
    b_spec = pl.BlockSpec((tn, tk), lambda i, j, k: (j, k)) if tb else pl.BlockSpec((tk, tn), lambda i, j, k: (k, j))
    o_spec = pl.BlockSpec((tm, tn), lambda i, j, k: (i, j))
    g_spec = pl.BlockSpec((1, N), lambda i, j, k: (0, 0))
    in_specs = [a_spec, b_spec] + ([o_spec] if res is not None else []) + ([o_spec, g_spec, o_spec] if norm_bwd else [])
    args = (a, b) + ((res,) if res is not None else ()) + (tuple(norm_bwd) if norm_bwd else ())
    out_shape = [jax.ShapeDtypeStruct((M, N), out_dtype)] + [jax.ShapeDtypeStruct((1, N), F32)] * nd
    out_specs = [o_spec] + [g_spec] * nd
    if not nc and not nd:
        return _pcall(
            body, name=name, grid=grid, in_specs=in_specs, out_specs=o_spec, out_shape=out_shape[0],
            scratch_shapes=[pltpu.VMEM((tm, tn), F32)], compiler_params=_params(("parallel", "parallel", "arbitrary")),
        )(*args)
    return _pcall(
        body, name=name, grid=grid, in_specs=in_specs + [_HBM] * nc, out_specs=out_specs + [_HBM] * nc,
        out_shape=out_shape + (plan_cls.out_shapes(carried) if nc else []),
        scratch_shapes=[pltpu.VMEM((tm, tn), F32)] + (plan_cls.sems(nc) if nc else []),
        compiler_params=_params(("arbitrary", "arbitrary", "arbitrary")),
    )(*args, *carried)


def mm_fused(a, bs, finish, out_dtypes, *, name, tb=False, extras=(), carry=None):
    M, K = a.shape
    N = bs[0].shape[0] if tb else bs[0].shape[1]
    tm, tn = _tile(M, 512, 128), _tile(N, 1408, 128)
    grid = (M // tm, N // tn)
    dims = (((1,), (1 if tb else 0,)), ((), ()))
    plan_cls, carried = carry if carry else (None, ())
    nc, nb, nx, no = len(carried), len(bs), len(extras), len(out_dtypes)
    nin = 1 + nb + nx

    def body(*refs):
        a_ref, b_refs, x_refs = refs[0], refs[1:1 + nb], refs[1 + nb:nin]
        o_refs = refs[nin + nc:nin + nc + no]
        if nc:
            plan = plan_cls(refs[nin:nin + nc], refs[nin + nc + no:nin + 2 * nc + no], refs[nin + 2 * nc + no:])
            at = lambda which: jnp.logical_and(*[pl.program_id(d) == (0 if which == 0 else grid[d] - 1) for d in range(2)])
            pl.when(at(0))(plan.start)
        av = a_ref[...]
        outs = finish([_bdot(av, b_ref[...], dims) for b_ref in b_refs], [x_ref[...] for x_ref in x_refs])
        for o_ref, o in zip(o_refs, outs):
            o_ref[...] = o.astype(o_ref.dtype)
        if nc:
            pl.when(at(1))(plan.finish)

    a_spec = pl.BlockSpec((tm, K), lambda i, j: (i, 0))
    b_spec = pl.BlockSpec((tn, K), lambda i, j: (j, 0)) if tb else pl.BlockSpec((K, tn), lambda i, j: (0, j))
    o_spec = pl.BlockSpec((tm, tn), lambda i, j: (i, j))
    out_shape = [jax.ShapeDtypeStruct((M, N), d) for d in out_dtypes]
    if not nc:
        return _pcall(body, name=name, grid=grid, in_specs=[a_spec] + [b_spec] * nb + [o_spec] * nx, out_specs=[o_spec] * no,
                      out_shape=out_shape, compiler_params=_params(("parallel", "parallel")))(a, *bs, *extras)
    return _pcall(body, name=name, grid=grid, in_specs=[a_spec] + [b_spec] * nb + [o_spec] * nx + [_HBM] * nc,
                  out_specs=[o_spec] * no + [_HBM] * nc, out_shape=out_shape + plan_cls.out_shapes(carried),
                  scratch_shapes=plan_cls.sems(nc), compiler_params=_params(("arbitrary", "arbitrary")))(a, *bs, *extras, *carried)


def _swiglu(products, _):
    g, u = products
    return g, u, g * _sigmoid(g) * u


def _swiglu_bwd(products, extras):
    (da,), (gate, up) = products, extras
    gv = gate.astype(F32)
    s = _sigmoid(gv)
    return da * 0.5 * up.astype(F32) * s * (1.0 + gv * (1.0 - s)), da * 0.5 * gv * s


def _row_spec(tr, c):
    return pl.BlockSpec((tr, c), lambda i: (i, 0))


def _full_spec(shape):
    return pl.BlockSpec(shape, lambda i: (0,) * len(shape))


def _acc_rows(ref, val, i):
    @pl.when(i == 0)
    def _():
        ref[...] = val

    @pl.when(i > 0)
    def _():
        ref[...] += val


def rms_fwd(x, g, *, name):
    T, D = x.shape
    tr = _tile(T, 512, 8)

    def body(x_ref, g_ref, h_ref):
        xv = x_ref[...]
        r = lax.rsqrt(jnp.mean(xv * xv, axis=-1, keepdims=True) + RMS_EPS)
        h_ref[...] = (xv * r * g_ref[...]).astype(BF16)

    return _pcall(body, name=name, grid=(T // tr,), in_specs=[_row_spec(tr, D), _full_spec((1, D))],
                  out_specs=_row_spec(tr, D), out_shape=jax.ShapeDtypeStruct((T, D), BF16),
                  compiler_params=_params(("parallel",)))(x, g)


def final_loss(x, tgt, g, *, name):
    T, D = x.shape
    tr = _tile(T, 256, 8)

    def body(x_ref, t_ref, g_ref, dx_ref, dg_ref, loss_ref):
        i = pl.program_id(0)
        xv = x_ref[...]
        r = lax.rsqrt(jnp.mean(xv * xv, axis=-1, keepdims=True) + RMS_EPS)
        xh = xv * r
        e = xh * g_ref[...] - t_ref[...]
        part = 0.5 * jnp.sum(jnp.mean(e * e, axis=-1, keepdims=True), axis=0, keepdims=True)
        dy = e * (1.0 / D)
        dxh = dy * g_ref[...]
        dx_ref[...] = r * (dxh - xh * jnp.mean(dxh * xh, axis=-1, keepdims=True))
        _acc_rows(dg_ref, jnp.sum(dy * xh, axis=0, keepdims=True), i)
        _acc_rows(loss_ref, jnp.broadcast_to(part, (1, LANE)), i)

    return _pcall(body, name=name, grid=(T // tr,),
                  in_specs=[_row_spec(tr, D), _row_spec(tr, D), _full_spec((1, D))],
                  out_specs=[_row_spec(tr, D), _full_spec((1, D)), _full_spec((1, LANE))],
                  out_shape=[jax.ShapeDtypeStruct((T, D), F32), jax.ShapeDtypeStruct((1, D), F32),
                             jax.ShapeDtypeStruct((1, LANE), F32)],
                  compiler_params=_params(("arbitrary",)))(x, tgt, g)


def merge_fwd(br, ba, pg, *, name):
    T, D = br.shape
    tr = _tile(T, 256, 8)

    def body(br_ref, ba_ref, pg_ref, o_ref):
        pgv = pg_ref[...]
        o_ref[...] = (_sigmoid(pgv[:, :D]) * br_ref[...] + _sigmoid(pgv[:, D:]) * ba_ref[...]).astype(BF16)

    return _pcall(body, name=name, grid=(T // tr,), in_specs=[_row_spec(tr, D), _row_spec(tr, D), _row_spec(tr, 2 * D)],
                  out_specs=_row_spec(tr, D), out_shape=jax.ShapeDtypeStruct((T, D), BF16),
                  compiler_params=_params(("parallel",)))(br, ba, pg)


def merge_bwd(dm, br, ba, pg, *, name):
    T, D = br.shape
    tr = _tile(T, 256, 8)

    def body(dm_ref, br_ref, ba_ref, pg_ref, dbr_ref, dba_ref, dpg_ref):
        pgv, dmv = pg_ref[...], dm_ref[...]
        sr, sa = _sigmoid(pgv[:, :D]), _sigmoid(pgv[:, D:])
        dbr_ref[...] = (dmv * sr).astype(BF16)
        dba_ref[...] = (dmv * sa).astype(BF16)
        dpg_ref[:, :D] = dmv * br_ref[...] * sr * (1.0 - sr)
        dpg_ref[:, D:] = dmv * ba_ref[...] * sa * (1.0 - sa)

    return _pcall(body, name=name, grid=(T // tr,),
                  in_specs=[_row_spec(tr, D), _row_spec(tr, D), _row_spec(tr, D), _row_spec(tr, 2 * D)],
                  out_specs=[_row_spec(tr, D), _row_spec(tr, D), _row_spec(tr, 2 * D)],
                  out_shape=[jax.ShapeDtypeStruct((T, D), BF16), jax.ShapeDtypeStruct((T, D), BF16),
                             jax.ShapeDtypeStruct((T, 2 * D), F32)],
                  compiler_params=_params(("parallel",)))(dm, br, ba, pg)


def _rwkv_mix(p, prev, mu, w0, a0, k_k, k_a, wlw, wla, wlg, bd):
    pp = p + (prev - p) * mu
    r, k, v = pp[:, 0:RW], pp[:, RW:2 * RW], pp[:, 2 * RW:3 * RW]
    xw, xa, xg = pp[:, 3 * RW:3 * RW + 128], pp[:, 3 * RW + 128:3 * RW + 256], pp[:, 3 * RW + 256:3 * RW + 384]
    th = jnp.tanh(xw)
    z = -(w0 + _bdot(th, wlw))
    e = jnp.exp(-_softplus(z) - 0.5)
    decay = jnp.exp(-e)
    a = _sigmoid(a0 + _bdot(xa, wla))
    sg = _sigmoid(xg)
    kkr = k * k_k
    n = jnp.sqrt(_segsum(kkr * kkr, bd))
    kk = kkr / jnp.maximum(n, 1e-12)
    k2 = k * (1.0 + (a - 1.0) * k_a)
    return dict(r=r, k=k, v=v, xa=xa, th=th, z=z, e=e, decay=decay, a=a, sg=sg, n=n, kk=kk, k2=k2)


def _seg_matrix(n, shift):
    r = lax.shift_right_logical(lax.broadcasted_iota(jnp.int32, (n, n), 0), shift)
    c = lax.shift_right_logical(lax.broadcasted_iota(jnp.int32, (n, n), 1), shift)
    return jnp.where(r == c, 1.0, 0.0).astype(BF16)


def rwkv_pre_fwd(p, pshift, mu, w0, a0, k_k, k_a, wlw, wla, wlg, *, name):
    T = p.shape[0]
    tr = _tile(T, 256, 8)

    def body(p_ref, ps_ref, mu_ref, w0_ref, a0_ref, kk_ref, ka_ref, wlw_ref, wla_ref, wlg_ref,
             r_ref, w_ref, k_ref, v_ref, a_ref, b_ref, g_ref):
        pv, prev = p_ref[...], ps_ref[...]
        m = _rwkv_mix(pv, prev, mu_ref[...], w0_ref[...], a0_ref[...], kk_ref[...], ka_ref[...],
                      wlw_ref[...], wla_ref[...], wlg_ref[...], _seg_matrix(RW, 6))
        r_ref[...] = m['r']
        w_ref[...] = m['decay']
        k_ref[...] = m['k2']
        v_ref[...] = m['v']
        a_ref[...] = -m['kk']
        b_ref[...] = m['kk'] * m['a']
        g_ref[...] = m['sg']

    vec = _row_spec(tr, RW)
    return _pcall(
        body, name=name, grid=(T // tr,),
        in_specs=[_row_spec(tr, RWKV_PAD), _row_spec(tr, RWKV_PAD), _full_spec((1, RWKV_PAD))] + [_full_spec((1, RW))] * 4
        + [_full_spec((128, RW))] * 3,
        out_specs=[vec] * 6 + [_row_spec(tr, 128)],
        out_shape=[jax.ShapeDtypeStruct((T, RW), F32)] * 6 + [jax.ShapeDtypeStruct((T, 128), F32)],
        compiler_params=_params(("parallel",)),
    )(p, pshift, mu, w0, a0, k_k, k_a, wlw, wla, wlg)


def _group_norm(y, bd):
    mean = _segsum(y, bd) * (1.0 / HEAD_DIM)
    yc = y - mean
    rstd = lax.rsqrt(_segsum(yc * yc, bd) * (1.0 / HEAD_DIM) + GN_EPS)
    return yc * rstd, rstd


def rwkv_post_fwd(y, r, k2, v, sg, wlg, ln_w, ln_b, r_k, *, name):
    T = y.shape[0]
    tr = _tile(T, 256, 8)

    def body(y_ref, r_ref, k_ref, v_ref, sg_ref, wlg_ref, lw_ref, lb_ref, rk_ref, o_ref):
        bd = _seg_matrix(RW, 6)
        yn, _ = _group_norm(y_ref[...], bd)
        s = _segsum(r_ref[...] * k_ref[...] * rk_ref[...], bd)
        g = _bdot(sg_ref[...], wlg_ref[...])
        o_ref[...] = ((yn * lw_ref[...] + lb_ref[...] + s * v_ref[...]) * g).astype(BF16)

    vec = _row_spec(tr, RW)
    return _pcall(body, name=name, grid=(T // tr,),
                  in_specs=[vec] * 4 + [_row_spec(tr, 128), _full_spec((128, RW))] + [_full_spec((1, RW))] * 3, out_specs=vec,
                  out_shape=jax.ShapeDtypeStruct((T, RW), BF16), compiler_params=_params(("parallel",)))(
                      y, r, k2, v, sg, wlg, ln_w, ln_b, r_k)


def rwkv_post_bwd(dyr, y, r, k2, v, sg, wlg, ln_w, ln_b, r_k, *, name, carry=None):
    T = y.shape[0]
    tr = _tile(T, 256, 8)
    nt = T // tr
    plan_cls, carried = carry if carry else (None, ())
    nc = len(carried)

    def body(*refs):
        dyr_ref, y_ref, r_ref, k_ref, v_ref, sg_ref, wlg_ref, lw_ref, lb_ref, rk_ref = refs[:10]
        dy_ref, dz_ref, dg_ref, dlw_ref, dlb_ref = refs[10 + nc:15 + nc]
        i = pl.program_id(0)
        if nc:
            plan = plan_cls(refs[10:10 + nc], refs[15 + nc:15 + 2 * nc], refs[15 + 2 * nc:])
            pl.when(i == 0)(plan.start)
        bd = _seg_matrix(RW, 6)
        yn, rstd = _group_norm(y_ref[...], bd)
        s = _segsum(r_ref[...] * k_ref[...] * rk_ref[...], bd)
        dyrv = dyr_ref[...]
        dg_ref[...] = dyrv * (yn * lw_ref[...] + lb_ref[...] + s * v_ref[...])
        dz = dyrv * _bdot(sg_ref[...], wlg_ref[...])
        dz_ref[...] = dz
        dyn = dz * lw_ref[...]
        inv = 1.0 / HEAD_DIM
        dy_ref[...] = rstd * (dyn - _segsum(dyn, bd) * inv - yn * (_segsum(dyn * yn, bd) * inv))
        _acc_rows(dlw_ref, jnp.sum(dz * yn, axis=0, keepdims=True), i)
        _acc_rows(dlb_ref, jnp.sum(dz, axis=0, keepdims=True), i)
        if nc:
            pl.when(i == nt - 1)(plan.finish)

    vec = _row_spec(tr, RW)
    one = _full_spec((1, RW))
    return _pcall(body, name=name, grid=(nt,),
                  in_specs=[vec] * 5 + [_row_spec(tr, 128), _full_spec((128, RW))] + [one] * 3 + [_HBM] * nc,
                  out_specs=[vec] * 3 + [one] * 2 + [_HBM] * nc,
                  out_shape=[jax.ShapeDtypeStruct((T, RW), F32)] * 3 + [jax.ShapeDtypeStruct((1, RW), F32)] * 2
                  + (plan_cls.out_shapes(carried) if nc else []),
                  scratch_shapes=plan_cls.sems(nc) if nc else [],
                  compiler_params=_params(("arbitrary",)))(dyr, y, r, k2, v, sg, wlg, ln_w, ln_b, r_k, *carried)


def rwkv_pre_bwd(p, pshift, dr_w, dw_w, dk_w, dv_w, da_w, db_w, dz, dg, mu, w0, a0, k_k, k_a, r_k, wlw, wla, wlg, *, name):
    T = p.shape[0]
    tr = _tile(T, 256, 8)
    n = T // tr

    def body(p_ref, ps_ref, dr_ref, dw_ref, dk_ref, dv_ref, da_ref, db_ref, dz_ref, dg_ref,
             mu_ref, w0_ref, a0_ref, kk_ref, ka_ref, rk_ref, wlw_ref, wla_ref, wlg_ref,
             dp_ref, dmu_ref, dw0_ref, da0_ref, dkk_ref, dka_ref, drk_ref, dwlw_ref, dwla_ref, dwlg_ref,
             carry, dpp, acc_w, acc_a, acc_g):
        i = pl.program_id(0)

        @pl.when(i == 0)
        def _():
            carry[...] = jnp.zeros_like(carry)

        pv, prev, mu = p_ref[...], ps_ref[...], mu_ref[...]
        bd = _seg_matrix(RW, 6)
        k_k, k_a, r_k = kk_ref[...], ka_ref[...], rk_ref[...]
        m = _rwkv_mix(pv, prev, mu, w0_ref[...], a0_ref[...], k_k, k_a, wlw_ref[...], wla_ref[...], wlg_ref[...], bd)
        r, k, v, a, kk, k2 = m['r'], m['k'], m['v'], m['a'], m['kk'], m['k2']
        dzv, dgv = dz_ref[...], dg_ref[...]
        s = _segsum(r * k2 * r_k, bd)
        ds = _segsum(dzv * v, bd)
        dr = dr_ref[...] + ds * k2 * r_k
        dk2 = dk_ref[...] + ds * r * r_k
        dv = dv_ref[...] + dzv * s
        dbv = db_ref[...]
        dkk = dbv * a - da_ref[...]
        da = dbv * kk + dk2 * k * k_a
        dk = dk2 * (1.0 + (a - 1.0) * k_a)
        nmax = jnp.maximum(m['n'], 1e-12)
        dkkr = jnp.where(m['n'] > 1e-12, dkk - kk * _segsum(dkk * kk, bd), dkk) / nmax
        dk = dk + dkkr * k_k
        dapre = da * a * (1.0 - a)
        dwpre = dw_ref[...] * m['decay'] * (-m['e']) * _sigmoid(m['z'])
        dth = _bdot(dwpre, wlw_ref[...], _NT)
        dxa = _bdot(dapre, wla_ref[...], _NT)
        dsg = _bdot(dgv, wlg_ref[...], _NT)
        dpp[:, 0:RW] = dr
        dpp[:, RW:2 * RW] = dk
        dpp[:, 2 * RW:3 * RW] = dv
        dpp[:, 3 * RW:3 * RW + 128] = dth * (1.0 - m['th'] * m['th'])
        dpp[:, 3 * RW + 128:3 * RW + 256] = dxa
        dpp[:, 3 * RW + 256:3 * RW + 384] = dsg * m['sg'] * (1.0 - m['sg'])
        d = dpp[...]
        zed = d * mu
        last = lax.broadcasted_iota(jnp.int32, pv.shape, 0) == tr - 1
        dp_ref[...] = d * (1.0 - mu) + jnp.where(last, carry[0:1, :], pltpu.roll(zed, tr - 1, 0))
        carry[...] = zed[0:8, :]

        def colsum(x):
            return jnp.sum(x, axis=0, keepdims=True)

        _acc_rows(dmu_ref, colsum(d * (prev - pv)), i)
        _acc_rows(dw0_ref, colsum(dwpre), i)
        _acc_rows(da0_ref, colsum(dapre), i)
        _acc_rows(dkk_ref, colsum(dkkr * k), i)
        _acc_rows(dka_ref, colsum(dk2 * k * (a - 1.0)), i)
        _acc_rows(drk_ref, colsum(ds * r * k2), i)
        _acc_rows(acc_w, _bdot(m['th'], dwpre, _TN), i)
        _acc_rows(acc_a, _bdot(m['xa'], dapre, _TN), i)
        _acc_rows(acc_g, _bdot(m['sg'], dgv, _TN), i)

        @pl.when(i == n - 1)
        def _():
            dwlw_ref[...] = acc_w[...]
            dwla_ref[...] = acc_a[...]
            dwlg_ref[...] = acc_g[...]

    rev = lambda c: pl.BlockSpec((tr, c), lambda i: (n - 1 - i, 0))
    one, lora = _full_spec((1, RW)), _full_spec((128, RW))
    return _pcall(
        body, name=name, grid=(n,),
        in_specs=[rev(RWKV_PAD), rev(RWKV_PAD)] + [rev(RW)] * 8 + [_full_spec((1, RWKV_PAD))] + [one] * 5 + [lora] * 3,
        out_specs=[rev(RWKV_PAD), _full_spec((1, RWKV_PAD))] + [one] * 5 + [lora] * 3,
        out_shape=[jax.ShapeDtypeStruct((T, RWKV_PAD), F32), jax.ShapeDtypeStruct((1, RWKV_PAD), F32)]
        + [jax.ShapeDtypeStruct((1, RW), F32)] * 5 + [jax.ShapeDtypeStruct((128, RW), F32)] * 3,
        scratch_shapes=[pltpu.VMEM((8, RWKV_PAD), F32), pltpu.VMEM((tr, RWKV_PAD), F32)] + [pltpu.VMEM((128, RW), F32)] * 3,
        compiler_params=_params(("arbitrary",)),
    )(p, pshift, dr_w, dw_w, dk_w, dv_w, da_w, db_w, dz, dg, mu, w0, a0, k_k, k_a, r_k, wlw, wla, wlg)


def _qk_norm(x, g, bd):
    r = lax.rsqrt(_segsum(x * x, bd) * (1.0 / HEAD_DIM) + RMS_EPS)
    return x * r * g, r


def _att_mask(i):
    qi = lax.broadcasted_iota(jnp.int32, (BLOCK, 2 * BLOCK), 0)
    kj = lax.broadcasted_iota(jnp.int32, (BLOCK, 2 * BLOCK), 1)
    band = (kj <= qi + BLOCK) & (kj > qi + BLOCK - WINDOW)
    return band & ((kj >= BLOCK) | (i > 0))


_HQK = (((2,), (2,)), ((0,), (0,)))
_HPV = (((2,), (1,)), ((0,), (0,)))
_HTN = (((1,), (1,)), ((0,), (0,)))


def _heads(x, n):
    return jnp.stack([x[:, h * HEAD_DIM:(h + 1) * HEAD_DIM] for h in range(n)])


def _unheads(x3):
    return jnp.concatenate([x3[h] for h in range(x3.shape[0])], axis=1)


def _kv_heads(x):
    x2 = _heads(x, KVW // HEAD_DIM)
    return jnp.concatenate([x2[g:g + 1] for g in range(KVW // HEAD_DIM) for _ in range(ATT_GROUP)], axis=0)


def _sinks3(sk):
    return jnp.stack([sk[0:1, h:h + 1] for h in range(N_HEADS)])


def _att_probs(q3, k3, mask, sink):
    s = _bdot(q3, k3, _HQK) * (HEAD_DIM ** -0.5)
    s = jnp.where(mask[None], s, NEG_BIG)
    m = jnp.maximum(jnp.max(s, axis=-1, keepdims=True), sink)
    pexp = jnp.exp(s - m)
    psink = jnp.exp(sink - m)
    inv = 1.0 / (jnp.sum(pexp, axis=-1, keepdims=True) + psink)
    return pexp * inv, psink * inv


ATT_SUB = 2


def _att_blocks(sub):
    cur = pl.BlockSpec((sub * BLOCK, ATT_COLS), lambda i: (i, 0))
    prev = pl.BlockSpec((BLOCK, ATT_COLS), lambda i: (jnp.maximum(i * sub - 1, 0), 0))
    return cur, prev


def _att_sub(cur_all, prev_first, j):
    cur = cur_all[j * BLOCK:(j + 1) * BLOCK]
    return cur, (prev_first if j == 0 else cur_all[(j - 1) * BLOCK:j * BLOCK])


def _att_qkv(cur, prev, qn_g, kn_g):
    bq, bk = _seg_matrix(RW, 6), _seg_matrix(KVW, 6)
    qn, rq = _qk_norm(cur[:, 0:RW], qn_g, bq)
    kcur, rkc = _qk_norm(cur[:, RW:RW + KVW], kn_g, bk)
    kprev, _ = _qk_norm(prev[:, RW:RW + KVW], kn_g, bk)
    kc = jnp.concatenate([kprev, kcur], axis=0)
    vc = jnp.concatenate([prev[:, RW + KVW:], cur[:, RW + KVW:]], axis=0)
    return qn, rq, kc, vc, rkc


def att_fwd(pa, qn_g, kn_g, sinks, *, name):
    T = pa.shape[0]
    sub = ATT_SUB if (T // BLOCK) % ATT_SUB == 0 else 1
    n = T // (sub * BLOCK)

    def body(cur_ref, prev_ref, qg_ref, kg_ref, sk_ref, o_ref):
        i = pl.program_id(0)
        cur_all, prev_first = cur_ref[...], prev_ref[...]
        for j in range(sub):
            cur, prev = _att_sub(cur_all, prev_first, j)
            qn, _, kc, vc, _ = _att_qkv(cur, prev, qg_ref[...], kg_ref[...])
            probs, _ = _att_probs(_heads(qn, N_HEADS), _kv_heads(kc), _att_mask(i * sub + j), _sinks3(sk_ref[...]))
            o_ref[j * BLOCK:(j + 1) * BLOCK, :] = _unheads(_bdot(probs, _kv_heads(vc), _HPV))

    cur, prev = _att_blocks(sub)
    return _pcall(body, name=name, grid=(n,),
                  in_specs=[cur, prev, _full_spec((1, RW)), _full_spec((1, KVW)), _full_spec((1, LANE))],
                  out_specs=pl.BlockSpec((sub * BLOCK, RW), lambda i: (i, 0)), out_shape=jax.ShapeDtypeStruct((T, RW), F32),
                  compiler_params=_params(("parallel",)))(pa, pa, qn_g, kn_g, sinks)


def att_bwd(pa, do, qn_g, kn_g, sinks, *, name):
    T = pa.shape[0]
    sub = ATT_SUB if (T // BLOCK) % ATT_SUB == 0 else 1
    n = T // (sub * BLOCK)

    def one_block(cur, prev, do, blk, qg_ref, kg_ref, sk_ref, rows, dq_ref, dko_ref, dkn_ref, dvo_ref, dvn_ref):
        qn, rq, kc, vc, _ = _att_qkv(cur, prev, qg_ref[...], kg_ref[...])
        q3, k3, v3, do3 = _heads(qn, N_HEADS), _kv_heads(kc), _kv_heads(vc), _heads(do, N_HEADS)
        probs, psink = _att_probs(q3, k3, _att_mask(blk), _sinks3(sk_ref[...]))
        dprobs = _bdot(do3, v3, _HQK)
        delta = jnp.sum(probs * dprobs, axis=-1, keepdims=True)
        ds = probs * (dprobs - delta) * (HEAD_DIM ** -0.5)
        dsink3 = -jnp.sum(psink * delta, axis=1, keepdims=True)
        lane = lax.broadcasted_iota(jnp.int32, (1, LANE), 1)
        dsink = jnp.zeros((1, LANE), F32)
        for h in range(N_HEADS):
            dsink = dsink + jnp.where(lane == h, dsink3[h], 0.0)
        dqn = _unheads(_bdot(ds, k3, _HPV))

        def per_kv_head(x3):
            groups = [sum(x3[g * ATT_GROUP + j] for j in range(ATT_GROUP)) for g in range(KVW // HEAD_DIM)]
            return jnp.concatenate(groups, axis=1)

        dk, dv = per_kv_head(_bdot(ds, q3, _HTN)), per_kv_head(_bdot(probs, do3, _HTN))
        dkn_ref[rows, :], dko_ref[rows, :] = dk[0:BLOCK], dk[BLOCK:]
        dvn_ref[rows, :], dvo_ref[rows, :] = dv[0:BLOCK], dv[BLOCK:]
        qhat = cur[:, 0:RW] * rq
        dqh = dqn * qg_ref[...]
        dq_ref[rows, :] = rq * (dqh - qhat * (_segsum(dqh * qhat, _seg_matrix(RW, 6)) * (1.0 / HEAD_DIM)))
        prod = dqn * qhat
        fold = prod[:, 0:HEAD_DIM]
        for h in range(1, N_HEADS):
            fold = fold + prod[:, h * HEAD_DIM:(h + 1) * HEAD_DIM]
        return jnp.sum(fold, axis=0, keepdims=True), dsink

    def body(cur_ref, prev_ref, do_ref, qg_ref, kg_ref, sk_ref,
             dq_ref, dko_ref, dkn_ref, dvo_ref, dvn_ref, dqg_ref, dsk_ref):
        i = pl.program_id(0)
        cur_all, prev_first, do_all = cur_ref[...], prev_ref[...], do_ref[...]
        dqg, dsk = None, None
        for j in range(sub):
            cur, prev = _att_sub(cur_all, prev_first, j)
            rows = slice(j * BLOCK, (j + 1) * BLOCK)
            g, s = one_block(cur, prev, do_all[rows], i * sub + j, qg_ref, kg_ref, sk_ref, rows,
                             dq_ref, dko_ref, dkn_ref, dvo_ref, dvn_ref)
            dqg, dsk = (g, s) if dqg is None else (dqg + g, dsk + s)
        _acc_rows(dqg_ref, dqg, i)
        _acc_rows(dsk_ref, dsk, i)

    cur, prev = _att_blocks(sub)
    kvb = pl.BlockSpec((sub * BLOCK, KVW), lambda i: (i, 0))
    qb = pl.BlockSpec((sub * BLOCK, RW), lambda i: (i, 0))
    return _pcall(body, name=name, grid=(n,),
                  in_specs=[cur, prev, qb, _full_spec((1, RW)), _full_spec((1, KVW)), _full_spec((1, LANE))],
                  out_specs=[qb, kvb, kvb, kvb, kvb, _full_spec((1, HEAD_DIM)), _full_spec((1, LANE))],
                  out_shape=[jax.ShapeDtypeStruct((T, RW), F32)] + [jax.ShapeDtypeStruct((T, KVW), F32)] * 4
                  + [jax.ShapeDtypeStruct((1, HEAD_DIM), F32), jax.ShapeDtypeStruct((1, LANE), F32)],
                  compiler_params=_params(("arbitrary",)))(pa, pa, do, qn_g, kn_g, sinks)


def att_kv_bwd(pa, dq, dko, dkn, dvo, dvn, kn_g, *, name):
    T = pa.shape[0]
    n = T // BLOCK

    def body(pa_ref, dq_ref, dko_ref, dkn_ref, dvo_ref, dvn_ref, kg_ref, dpa_ref, dkg_ref):
        i = pl.program_id(0)
        more = i < n - 1
        dkn_tot = dko_ref[...] + jnp.where(more, dkn_ref[...], 0.0)
        dv_tot = dvo_ref[...] + jnp.where(more, dvn_ref[...], 0.0)
        kraw = pa_ref[:, RW:RW + KVW]
        bk = _seg_matrix(KVW, 6)
        _, rk = _qk_norm(kraw, kg_ref[...], bk)
        khat = kraw * rk
        dkh = dkn_tot * kg_ref[...]
        dpa_ref[:, 0:RW] = dq_ref[...]
        dpa_ref[:, RW:RW + KVW] = rk * (dkh - khat * (_segsum(dkh * khat, bk) * (1.0 / HEAD_DIM)))
        dpa_ref[:, RW + KVW:] = dv_tot
        prod = dkn_tot * khat
        _acc_rows(dkg_ref, jnp.sum(prod[:, 0:HEAD_DIM] + prod[:, HEAD_DIM:], axis=0, keepdims=True), i)

    kvb = pl.BlockSpec((BLOCK, KVW), lambda i: (i, 0))
    nxt = pl.BlockSpec((BLOCK, KVW), lambda i: (jnp.minimum(i + 1, n - 1), 0))
    return _pcall(body, name=name, grid=(n,),
                  in_specs=[pl.BlockSpec((BLOCK, ATT_COLS), lambda i: (i, 0)), pl.BlockSpec((BLOCK, RW), lambda i: (i, 0)),
                            kvb, nxt, kvb, nxt, _full_spec((1, KVW))],
                  out_specs=[pl.BlockSpec((BLOCK, ATT_COLS), lambda i: (i, 0)), _full_spec((1, HEAD_DIM))],
                  out_shape=[jax.ShapeDtypeStruct((T, ATT_COLS), F32), jax.ShapeDtypeStruct((1, HEAD_DIM), F32)],
                  compiler_params=_params(("arbitrary",)))(pa, dq, dko, dkn, dvo, dvn, kn_g)


WKV_CHUNK = 128
WKV_GROUP = 8


def _diag_mask():
    i = lax.broadcasted_iota(jnp.int32, (HEAD_DIM, RW), 0)
    j = lax.broadcasted_iota(jnp.int32, (HEAD_DIM, RW), 1) & (HEAD_DIM - 1)
    return i == j


def _heads_matrix():
    head = jnp.arange(RW // 2) // HEAD_DIM
    bd = (head[:, None] == head[None, :]).astype(BF16)
    return jnp.concatenate([bd, bd], axis=0)


def _headsums(xs, pieces, bd2):
    half = RW // 2
    bd = bd2[:pieces * half]
    rows = []
    for x in xs:
        parts, rest = [], x
        for n in range(pieces):
            p = rest.astype(BF16)
            parts.append(p)
            if n + 1 < pieces:
                rest = rest - p.astype(F32)
        for sl in (slice(0, half), slice(half, RW)):
            rows.append(jnp.concatenate([p[:, sl] for p in parts], axis=1))
    out = lax.dot_general(jnp.concatenate(rows, axis=0), bd, (((1,), (0,)), ((), ())), preferred_element_type=F32)
    return [jnp.concatenate([out[2 * n * HEAD_DIM:(2 * n + 1) * HEAD_DIM], out[(2 * n + 1) * HEAD_DIM:(2 * n + 2) * HEAD_DIM]],
                            axis=1) for n in range(len(xs))]


def _headsum(x):
    low = lax.broadcasted_iota(jnp.int32, (HEAD_DIM, LANE), 1) < HEAD_DIM
    tiles = []
    for c in range(RW // LANE):
        xt = x[:, c * LANE:(c + 1) * LANE]
        s_lo = jnp.sum(jnp.where(low, xt, 0.0), axis=1, keepdims=True)
        s_hi = jnp.sum(jnp.where(low, 0.0, xt), axis=1, keepdims=True)
        tiles.append(jnp.where(low, s_lo, s_hi))
    return jnp.concatenate(tiles, axis=1)


def _cols(rows, diag, bd2, pieces=2):
    return _headsums([jnp.where(diag, r, 0.0) for r in rows], pieces, bd2)


def _row(x, diag):
    return jnp.sum(jnp.where(diag, x, 0.0), axis=0, keepdims=True)


def wkv_fwd(r, w, k, v, a, b, *, name, gather=()):
    T = r.shape[0]
    ch = min(WKV_CHUNK, T)
    ngroups = ch // WKV_GROUP
    nchunks = T // ch
    ng = len(gather)

    def body(*refs):
        r_ref, w_ref, k_ref, v_ref, a_ref, b_ref, bd_ref = refs[:7]
        y_ref, st_ref = refs[7 + ng:9 + ng]
        s_scr = refs[9 + 2 * ng]
        step = pl.program_id(0)
        if ng:
            plan = _GatherPlan(refs[7:7 + ng], refs[9 + ng:9 + 2 * ng], refs[10 + 2 * ng:])
            pl.when(step == 0)(plan.start)
            pl.when(step == nchunks // 2)(plan.relay)

        @pl.when(step == 0)
        def _():
            s_scr[...] = jnp.zeros_like(s_scr)

        diag, bd2 = _diag_mask(), bd_ref[...]

        def group(gi, S):
            t0 = pl.multiple_of(gi * WKV_GROUP, WKV_GROUP)
            rows = pl.ds(t0, WKV_GROUP)
            R, W, K, V, A, B = (ref[rows, :] for ref in (r_ref, w_ref, k_ref, v_ref, a_ref, b_ref))
            vcols = _cols([V[s:s + 1] for s in range(WKV_GROUP)], diag, bd2, 1)
            yrows = []
            for s in range(WKV_GROUP):
                sa = _headsum(S * A[s:s + 1])
                S = S * W[s:s + 1] + sa * B[s:s + 1] + vcols[s] * K[s:s + 1]
                st_ref[t0 + s] = S
                yrows.append(_row(_headsums([S * R[s:s + 1]], 1, bd2)[0], diag))
            y_ref[rows, :] = jnp.concatenate(yrows, axis=0)
            return S

        s_scr[...] = lax.fori_loop(0, ngroups, group, s_scr[...])
        if ng:
            pl.when(step == nchunks - 1)(plan.finish_relayed)

    vec = pl.BlockSpec((ch, RW), lambda c: (c, 0))
    return _pcall(
        body, name=name, grid=(nchunks,), in_specs=[vec] * 6 + [_full_spec((RW, RW // 2))] + [_HBM] * ng,
        out_specs=[vec, pl.BlockSpec((ch, HEAD_DIM, RW), lambda c: (c, 0, 0))] + [_HBM] * ng,
        out_shape=[jax.ShapeDtypeStruct((T, RW), F32), jax.ShapeDtypeStruct((T, HEAD_DIM, RW), F32)] + _gathered_shapes(gather),
        scratch_shapes=[pltpu.VMEM((HEAD_DIM, RW), F32)] + (_GatherPlan.sems(ng) if ng else []),
        compiler_params=_params(("arbitrary",)),
    )(r, w, k, v, a, b, _heads_matrix(), *gather)


def wkv_bwd(r, w, k, v, a, b, dy, states, *, name, exchange=()):
    T = r.shape[0]
    ch = min(WKV_CHUNK, T)
    nchunks = T // ch
    ngroups = ch // WKV_GROUP
    ne = len(exchange)

    def body(*refs):
        r_ref, w_ref, k_ref, v_ref, a_ref, b_ref, dy_ref, st_ref, stp_ref, bd_ref = refs[:10]
        dr_ref, dw_ref, dk_ref, dv_ref, da_ref, db_ref = refs[10 + ne:16 + ne]
        ds_scr = refs[16 + 2 * ne]
        step = pl.program_id(0)
        if ne:
            plan = _ExchangePlan(refs[10:10 + ne], refs[16 + ne:16 + 2 * ne], refs[17 + 2 * ne:])
            pl.when(step == 0)(plan.start)

        @pl.when(step == 0)
        def _():
            ds_scr[...] = jnp.zeros_like(ds_scr)

        has_prev_chunk = step < nchunks - 1
        diag, bd2 = _diag_mask(), bd_ref[...]
        colsum = lambda x: jnp.sum(x, axis=0, keepdims=True)

        def group(gj, dS):
            gi = ngroups - 1 - gj
            t0 = pl.multiple_of(gi * WKV_GROUP, WKV_GROUP)
            rows = pl.ds(t0, WKV_GROUP)
            R, W, K, V, A, B, DY = (ref[rows, :] for ref in (r_ref, w_ref, k_ref, v_ref, a_ref, b_ref, dy_ref))
            before = jnp.where(gi > 0, st_ref[jnp.maximum(t0 - 1, 0)], jnp.where(has_prev_chunk, stp_ref[0], 0.0))
            prev_state = lambda s: st_ref[t0 + s - 1] if s > 0 else before
            steps = range(WKV_GROUP)
            dycols = _cols([DY[s:s + 1] for s in steps], diag, bd2, 1)
            vcols = _cols([V[s:s + 1] for s in steps], diag, bd2, 1)
            sas = _headsums([prev_state(s) * A[s:s + 1] for s in steps], 1, bd2)
            got = [[None] * WKV_GROUP for _ in range(6)]
            for s in reversed(steps):
                Sp = prev_state(s)
                dS = dS + dycols[s] * R[s:s + 1]
                got[0][s] = colsum(st_ref[t0 + s] * dycols[s])
                got[3][s] = _row(_headsums([dS * K[s:s + 1]], 1, bd2)[0], diag)
                got[2][s] = colsum(dS * vcols[s])
                dsa = _headsum(dS * B[s:s + 1])
                got[5][s] = colsum(dS * sas[s])
                got[1][s] = colsum(dS * Sp)
                got[4][s] = colsum(Sp * dsa)
                dS = dS * W[s:s + 1] + dsa * A[s:s + 1]
            for q, ref in enumerate((dr_ref, dw_ref, dk_ref, dv_ref, da_ref, db_ref)):
                ref[rows, :] = jnp.concatenate(got[q], axis=0)
            return dS

        ds_scr[...] = lax.fori_loop(0, ngroups, group, ds_scr[...])
        if ne:
            pl.when(step == nchunks - 1)(plan.finish)

    vec = pl.BlockSpec((ch, RW), lambda c: (nchunks - 1 - c, 0))
    st_spec = pl.BlockSpec((ch, HEAD_DIM, RW), lambda c: (nchunks - 1 - c, 0, 0))
    stp_spec = pl.BlockSpec((1, HEAD_DIM, RW), lambda c: (jnp.maximum((nchunks - 1 - c) * ch - 1, 0), 0, 0))
    return _pcall(
        body, name=name, grid=(nchunks,), in_specs=[vec] * 7 + [st_spec, stp_spec, _full_spec((RW, RW // 2))] + [_HBM] * ne,
        out_specs=[vec] * 6 + [_HBM] * ne,
        out_shape=[jax.ShapeDtypeStruct((T, RW), F32)] * 6 + [jax.ShapeDtypeStruct(e.shape, e.dtype) for e in exchange],
        scratch_shapes=[pltpu.VMEM((HEAD_DIM, RW), F32)] + (_ExchangePlan.sems(ne) if ne else []),
        compiler_params=_params(("arbitrary",)),
    )(r, w, k, v, a, b, dy, states, states, _heads_matrix(), *exchange)


_HBM = pl.BlockSpec(memory_space=pltpu.HBM)
_MESH = pl.DeviceIdType.MESH


def _place():
    x, y, c = lax.axis_index("x"), lax.axis_index("y"), lax.axis_index("c")
    return x, y, c, [(1 - x, y), (x, 1 - y), (1 - x, 1 - y)]


def _remote(src, dst, send_sem, recv_sem, to):
    return pltpu.make_async_remote_copy(src_ref=src, dst_ref=dst, send_sem=send_sem, recv_sem=recv_sem, device_id=to,
                                        device_id_type=_MESH)


def _dma_sems(*counts):
    return [pltpu.SemaphoreType.DMA((n,)) for n in counts]


class _GatherPlan:
    def __init__(self, ins, outs, sems):
        self.ins, self.outs, self.n = ins, outs, len(ins)
        self.ici_send, self.ici_recv, self.d2d_send, self.d2d_recv, self.local_sems = sems
        x, y, c, chips = _place()
        self.c, self.me, self.sibling = c, 2 * x + y, (x, y, 1 - c)
        self.peers = [(2 * qx + qy, (qx, qy, c)) for qx, qy in chips]

    @staticmethod
    def sems(n):
        return _dma_sems(3 * n, 3 * n, 3 * n, 3 * n, n)

    def _half(self, i, which):
        rh = self.ins[i].shape[0] // 2
        return pl.ds(which * rh, rh)

    def _local(self, i):
        return pltpu.make_async_copy(self.ins[i], self.outs[i].at[self.me], self.local_sems.at[i])

    def _send(self, i, j):
        k, mine = 3 * i + j, self._half(i, self.c)
        return _remote(self.ins[i].at[mine], self.outs[i].at[self.me, mine], self.ici_send.at[k], self.ici_recv.at[k],
                       self.peers[j][1])

    def _landed(self, i, j):
        k, piece = 3 * i + j, self.outs[i].at[self.peers[j][0], self._half(i, self.c)]
        return _remote(piece, piece, self.ici_send.at[k], self.ici_recv.at[k], self.peers[j][1])

    def _pass(self, i, j, which):
        k, piece = 3 * i + j, self.outs[i].at[self.peers[j][0], self._half(i, which)]
        return _remote(piece, piece, self.d2d_send.at[k], self.d2d_recv.at[k], self.sibling)

    def _all(self):
        return [(i, j) for i in range(self.n) for j in range(3)]

    def start(self):
        for i in range(self.n):
            self._local(i).start()
        for i, j in self._all():
            self._send(i, j).start()

    def relay(self):
        for i, j in self._all():
            self._landed(i, j).wait_recv()
            self._pass(i, j, self.c).start()

    def finish_relayed(self):
        for i, j in self._all():
            self._pass(i, j, 1 - self.c).wait_recv()
        for i, j in self._all():
            self._send(i, j).wait_send()
            self._pass(i, j, self.c).wait_send()
        for i in range(self.n):
            self._local(i).wait()

    def finish(self):
        self.relay()
        self.finish_relayed()

    @staticmethod
    def out_shapes(shards):
        return _gathered_shapes(shards)


def _gathered_shapes(shards):
    return [jax.ShapeDtypeStruct((N_CHIPS,) + s.shape, s.dtype) for s in shards]


def gather_weights(shards, *, name):
    n = len(shards)

    def body(*refs):
        plan = _GatherPlan(refs[:n], refs[n:2 * n], refs[2 * n:])
        plan.start()
        plan.finish()

    return _pcall(body, name=name, in_specs=[_HBM] * n, out_specs=[_HBM] * n, out_shape=_gathered_shapes(shards),
                  scratch_shapes=_GatherPlan.sems(n), compiler_params=_params())(*shards)


class _SiblingPlan:
    halves = True

    def __init__(self, ins, outs, sems):
        self.ins, self.outs, self.n = ins, outs, len(ins)
        self.send_sems, self.recv_sems = sems
        x, y, c, _ = _place()
        self.c, self.sibling = c, (x, y, 1 - c)

    @staticmethod
    def sems(n):
        return _dma_sems(n, n)

    @classmethod
    def out_shapes(cls, arrays):
        if not cls.halves:
            return [jax.ShapeDtypeStruct(a.shape, a.dtype) for a in arrays]
        return [jax.ShapeDtypeStruct((a.shape[0], a.shape[1] // 2, a.shape[2]), a.dtype) for a in arrays]

    def _copy(self, i):
        src = self.ins[i]
        if self.halves:
            rh = src.shape[1] // 2
            src = src.at[:, pl.ds((1 - self.c) * rh, rh)]
        return _remote(src, self.outs[i], self.send_sems.at[i], self.recv_sems.at[i], self.sibling)

    def start(self):
        for i in range(self.n):
            self._copy(i).start()

    def finish(self):
        for i in range(self.n):
            self._copy(i).wait_recv()
        for i in range(self.n):
            self._copy(i).wait_send()


class _SiblingWhole(_SiblingPlan):
    halves = False


def to_sibling(arrays, take_other_half, *, name):
    n = len(arrays)
    plan_cls = _SiblingPlan if take_other_half else _SiblingWhole

    def body(*refs):
        plan = plan_cls(refs[:n], refs[n:2 * n], refs[2 * n:])
        plan.start()
        plan.finish()

    return _pcall(body, name=name, in_specs=[_HBM] * n, out_specs=[_HBM] * n, out_shape=plan_cls.out_shapes(arrays),
                  scratch_shapes=plan_cls.sems(n), compiler_params=_params())(*arrays)


def exchange_chips(arrays, *, name):
    n = len(arrays)

    def body(*refs):
        plan = _ExchangePlan(refs[:n], refs[n:2 * n], refs[2 * n:])
        plan.start()
        plan.finish()

    return _pcall(body, name=name, in_specs=[_HBM] * n, out_specs=[_HBM] * n,
                  out_shape=[jax.ShapeDtypeStruct(a.shape, a.dtype) for a in arrays],
                  scratch_shapes=_ExchangePlan.sems(n), compiler_params=_params())(*arrays)


class _ExchangePlan:
    def __init__(self, ins, outs, sems):
        self.ins, self.outs, self.n = ins, outs, len(ins)
        self.send_sems, self.recv_sems, self.local_sems = sems
        x, y, c, chips = _place()
        self.me = 2 * x + y
        self.peers = [(2 * qx + qy, (qx, qy, c)) for qx, qy in chips]

    @staticmethod
    def sems(n):
        return _dma_sems(3 * n, 3 * n, n)

    @staticmethod
    def out_shapes(arrays):
        return [jax.ShapeDtypeStruct(a.shape, a.dtype) for a in arrays]

    def _local(self, i):
        return pltpu.make_async_copy(self.ins[i].at[self.me], self.outs[i].at[self.me], self.local_sems.at[i])

    def _send(self, i, j):
        k = 3 * i + j
        return _remote(self.ins[i].at[self.peers[j][0]], self.outs[i].at[self.me], self.send_sems.at[k], self.recv_sems.at[k],
                       self.peers[j][1])

    def _landed(self, i, j):
        k, piece = 3 * i + j, self.outs[i].at[self.peers[j][0]]
        return _remote(piece, piece, self.send_sems.at[k], self.recv_sems.at[k], self.peers[j][1])

    def start(self):
        for i in range(self.n):
            self._local(i).start()
            for j in range(3):
                self._send(i, j).start()

    def finish(self):
        for i in range(self.n):
            for j in range(3):
                self._landed(i, j).wait_recv()
        for i in range(self.n):
            for j in range(3):
                self._send(i, j).wait_send()
            self._local(i).wait()


def _core_index():
    return lax.axis_index("c").astype(jnp.int32).reshape(1)


def pair_sum(g, theirs, wire_dtype, *, name):
    _, R, C = g.shape
    rh = R // 2
    tr = _tile(rh, 256, 16)
    nt = rh // tr

    def body(c_ref, g_ref, t_ref, q_ref, qw_ref):
        q = g_ref[...] + t_ref[...]
        q_ref[...] = q
        qw_ref[...] = q.astype(wire_dtype)

    blk = pl.BlockSpec((1, tr, C), lambda b, i, c_ref: (b, i, 0))
    mine = pl.BlockSpec((1, tr, C), lambda b, i, c_ref: (b, c_ref[0] * nt + i, 0))
    grid_spec = pltpu.PrefetchScalarGridSpec(num_scalar_prefetch=1, grid=(N_CHIPS, nt), in_specs=[mine, blk], out_specs=[blk, blk])
    return _pcall(body, name=name, grid_spec=grid_spec,
                  out_shape=[jax.ShapeDtypeStruct((N_CHIPS, rh, C), F32), jax.ShapeDtypeStruct((N_CHIPS, rh, C), wire_dtype)],
                  compiler_params=_params(("parallel", "parallel")))(_core_index(), g, theirs)


def half_sum(own, landed, *, name):
    _, rh, C = own.shape
    tr = _tile(rh, 256, 16)

    def body(me_ref, own_ref, land_ref, o_ref):
        total = None
        for p in range(N_CHIPS):
            term = jnp.where(me_ref[0] == p, own_ref[p], land_ref[p].astype(F32))
            total = term if total is None else total + term
        o_ref[...] = total

    blk = pl.BlockSpec((N_CHIPS, tr, C), lambda i, me_ref: (0, i, 0))
    grid_spec = pltpu.PrefetchScalarGridSpec(num_scalar_prefetch=1, grid=(rh // tr,), in_specs=[blk, blk],
                                             out_specs=pl.BlockSpec((tr, C), lambda i, me_ref: (i, 0)))
    me = (2 * lax.axis_index("x") + lax.axis_index("y")).astype(jnp.int32).reshape(1)
    return _pcall(body, name=name, grid_spec=grid_spec, out_shape=jax.ShapeDtypeStruct((rh, C), F32),
                  compiler_params=_params(("parallel",)))(me, own, landed)


def adamw(w, m, v, mine, theirs, *, name):
    _, R, C = w.shape
    rh = R // 2
    tr = _tile(rh, 256, 8)
    nt = rh // tr

    def body(c_ref, w_ref, m_ref, v_ref, a_ref, b_ref, g_ref, d_ref, nm_ref, nv_ref):
        is_mine = (pl.program_id(0) // nt) == c_ref[0]
        g = jnp.where(is_mine, a_ref[...], b_ref[...])
        g_ref[...] = g
        nm = ADAM_B1 * m_ref[...] + (1.0 - ADAM_B1) * g
        nv = ADAM_B2 * v_ref[...] + (1.0 - ADAM_B2) * (g * g)
        nm_ref[...] = nm
        nv_ref[...] = nv
        m_hat = nm / (1.0 - ADAM_B1 ** ADAM_STEP)
        v_hat = nv / (1.0 - ADAM_B2 ** ADAM_STEP)
        d_ref[...] = -ADAM_LR * (m_hat / (jnp.sqrt(v_hat) + ADAM_EPS) + ADAM_WD * w_ref[...])

    full = pl.BlockSpec((None, tr, C), lambda i, c_ref: (0, i, 0))
    a_spec = pl.BlockSpec((tr, C), lambda i, c_ref: (jnp.clip(i - c_ref[0] * nt, 0, nt - 1), 0))
    b_spec = pl.BlockSpec((tr, C), lambda i, c_ref: (jnp.clip(i - (1 - c_ref[0]) * nt, 0, nt - 1), 0))
    grid_spec = pltpu.PrefetchScalarGridSpec(num_scalar_prefetch=1, grid=(2 * nt,), in_specs=[full] * 3 + [a_spec, b_spec],
                                             out_specs=[full] * 4)
    return _pcall(body, name=name, grid_spec=grid_spec, out_shape=[jax.ShapeDtypeStruct((1, R, C), F32)] * 4,
                  compiler_params=_params(("arbitrary",)))(_core_index(), w, m, v, mine, theirs)


def _to_blocks(full, axis):
    r, c = full.shape
    if axis == 1:
        return full.reshape(r, N_CHIPS, c // N_CHIPS).transpose(1, 0, 2)
    return full.reshape(N_CHIPS, r // N_CHIPS, c)


def _from_blocks(blocks, axis):
    _, r, c = blocks.shape
    if axis == 1:
        return blocks.transpose(1, 0, 2).reshape(r, N_CHIPS * c)
    return blocks.reshape(N_CHIPS * r, c)


def _ffn_fwd(x, norm, wg_t, wu_t, wd, tag):
    h = rms_fwd(x, norm, name=tag + "_norm")
    gate, up, act = mm_fused(h, [wg_t, wu_t], _swiglu, [BF16] * 3, tb=True, name=tag + "_gate_up")
    out = mm(act, wd, scale=0.5, res=x, name=tag + "_down")
    return out, (h, gate, up, act)


def _ffn_bwd(dout, x, saved, norm, wg_t, wu_t, wd, tag, carry=None, reduce=None):
    h, gate, up, act = saved
    dgate, dup, *carried = mm_fused(dout, [wd], _swiglu_bwd, [BF16] * 2, tb=True, extras=[gate, up], name=tag + "_dact",
                                    carry=carry)
    dwd = mm(act, dout, ta=True, scale=0.5, name=tag + "_dwd")
    dwg_t = mm(dgate, h, ta=True, name=tag + "_dwg")
    dwu_t = mm(dup, h, ta=True, name=tag + "_dwu")
    reduced = None
    if reduce:
        blocks = [_to_blocks(g, 0) for g in (dwg_t, dwu_t, dwd)]
        dh, *from_sibling = mm(dgate, wg_t, name=tag + "_dh_gate", carry=(_SiblingPlan, blocks))
        pair = [pair_sum(g, t, BF16, name="pair_sum_" + n) for g, t, n in zip(blocks, from_sibling, reduce)]
        dx, dnorm, *landed = mm(dup, wu_t, res=dh, name=tag + "_dh_up", norm_bwd=(x, norm, dout),
                                carry=(_ExchangePlan, [q for _, q in pair]))
        reduced = (pair, landed)
    else:
        dh = mm(dgate, wg_t, name=tag + "_dh_gate")
        dx, dnorm = mm(dup, wu_t, res=dh, name=tag + "_dh_up", norm_bwd=(x, norm, dout))
    return dx, dnorm, dwg_t, dwu_t, dwd, carried, reduced


TRANSPOSED = ('ffn1_w_gate', 'ffn1_w_up', 'ffn2_w_gate', 'ffn2_w_up')
FIRST_WEIGHTS = ['ffn1_w_gate', 'ffn1_w_up', 'ffn1_w_down']
MID_WEIGHTS = ['w_in', 'rwkv_w_lora_up', 'rwkv_a_lora_up', 'rwkv_g_lora_up']
LATE_WEIGHTS = ['w_branch_rwkv', 'w_branch_attn', 'w_out', 'ffn2_w_gate', 'ffn2_w_up', 'ffn2_w_down']


def _pair_sums(names, blocks, tag):
    from_sibling = to_sibling(blocks, True, name=tag + "_grads_to_sibling")
    return [pair_sum(g, t, F32 if n == 'small' else BF16, name="pair_sum_" + n)
            for g, t, n in zip(blocks, from_sibling, names)]


def _step(A):
    x, tgt = A['x'][0], A['loss_target'][0]
    T = x.shape[0]
    w = {n: A[n][0] for n in WEIGHT_NAMES}
    row = lambda a: a.reshape(1, -1)

    axis_of = {n: (0 if n in TRANSPOSED else axis) for n, axis in BIG}
    natural = lambda n, a: jnp.swapaxes(a, 1, 2) if n in TRANSPOSED else a
    shard = lambda n: natural(n, A[n])[0].astype(BF16)
    n1, nmix, n2, nfin = (row(w[n]) for n in ('ffn1_norm', 'mix_norm', 'ffn2_norm', 'final_norm'))
    gathered = gather_weights([shard(n) for n in FIRST_WEIGHTS[:2]], name="gather_weights")
    full = {n: _from_blocks(b, axis_of[n]) for n, b in zip(FIRST_WEIGHTS[:2], gathered)}
    h1 = rms_fwd(x, n1, name="ffn1_norm")
    gate1, up1, act1, down_blocks = mm_fused(h1, [full['ffn1_w_gate'], full['ffn1_w_up']], _swiglu, [BF16] * 3, tb=True,
                                             name="ffn1_gate_up", carry=(_GatherPlan, [shard('ffn1_w_down')]))
    full['ffn1_w_down'] = _from_blocks(down_blocks, axis_of['ffn1_w_down'])
    x1, *gathered = mm(act1, full['ffn1_w_down'], scale=0.5, res=x, name="ffn1_down",
                       carry=(_GatherPlan, [shard(n) for n in MID_WEIGHTS]))
    ffn1 = (h1, gate1, up1, act1)
    full.update({n: _from_blocks(b, axis_of[n]) for n, b in zip(MID_WEIGHTS, gathered)})
    w_in_r = _pad_rwkv_cols(full['w_in'][:, :RWKV_COLS])
    w_in_a = full['w_in'][:, RWKV_COLS:RWKV_COLS + ATT_COLS]
    w_in_g = full['w_in'][:, RWKV_COLS + ATT_COLS:]
    wlw, wla, wlg = (_pad_rows(full[n], 128).astype(F32) for n in ('rwkv_w_lora_up', 'rwkv_a_lora_up', 'rwkv_g_lora_up'))
    mu = _pad_rwkv_cols(row(w['rwkv_mu']))
    w0, a0, k_k, k_a, r_k, ln_w, ln_b = (row(w[n]) for n in ('rwkv_w0', 'rwkv_a0', 'rwkv_k_k', 'rwkv_k_a', 'rwkv_r_k',
                                                               'rwkv_ln_w', 'rwkv_ln_b'))
    qg = jnp.tile(row(w['attn_q_norm']), (1, N_HEADS))
    kg = jnp.tile(row(w['attn_k_norm']), (1, KVW // HEAD_DIM))
    sinks = jnp.pad(row(w['attn_sinks']), ((0, 0), (0, LANE - N_HEADS)))

    h2 = rms_fwd(x1, nmix, name="mix_norm")
    pr = mm(h2, w_in_r, name="proj_rwkv")
    pa = mm(h2, w_in_a, name="proj_att")
    pg = mm(h2, w_in_g, name="proj_gate")
    pr_shift = jnp.pad(pr, ((1, 0), (0, 0)))[:-1]
    r, dec, k2, v, a, b, sg = rwkv_pre_fwd(pr, pr_shift, mu, w0, a0, k_k, k_a, wlw, wla, wlg, name="rwkv_pre")
    y, states, *gathered = wkv_fwd(r, dec, k2, v, a, b, name="wkv_fwd", gather=[shard(n) for n in LATE_WEIGHTS])
    full.update({n: _from_blocks(b, axis_of[n]) for n, b in zip(LATE_WEIGHTS, gathered)})
    yr = rwkv_post_fwd(y, r, k2, v, sg, wlg, ln_w, ln_b, r_k, name="rwkv_post")
    ya = att_fwd(pa, qg, kg, sinks, name="att_fwd")
    br = mm(yr, full['w_branch_rwkv'], name="branch_rwkv")
    ba = mm(ya, full['w_branch_attn'], name="branch_att")
    mg = merge_fwd(br, ba, pg, name="merge")
    x2 = mm(mg, full['w_out'], res=x1, name="mix_out")
    x3, ffn2 = _ffn_fwd(x2, n2, full['ffn2_w_gate'], full['ffn2_w_up'], full['ffn2_w_down'], "ffn2")
    dx3, d_nfin, loss = final_loss(x3, tgt, nfin, name="final_loss")

    G = {'final_norm': d_nfin}
    dx2, G['ffn2_norm'], G['ffn2_w_gate'], G['ffn2_w_up'], G['ffn2_w_down'], _, _ = _ffn_bwd(
        dx3, x2, ffn2, n2, full['ffn2_w_gate'], full['ffn2_w_up'], full['ffn2_w_down'], "ffn2")
    dmg = mm(dx2, full['w_out'], tb=True, name="d_merge")
    G['w_out'] = mm(mg, dx2, ta=True, name="d_w_out")
    dbr, dba, dpg = merge_bwd(dmg, br, ba, pg, name="merge_bwd")
    dyr = mm(dbr, full['w_branch_rwkv'], tb=True, name="d_y_rwkv")
    G['w_branch_rwkv'] = mm(yr, dbr, ta=True, name="d_w_branch_rwkv")
    dya = mm(dba, full['w_branch_attn'], tb=True, name="d_y_att")
    G['w_branch_attn'] = mm(ya, dba, ta=True, name="d_w_branch_att")
    late_blocks = [_to_blocks(G[n], axis_of[n]) for n in LATE_WEIGHTS]
    dy, dz, dg, G['rwkv_ln_w'], G['rwkv_ln_b'], *late_from_sibling = rwkv_post_bwd(
        dyr, y, r, k2, v, sg, wlg, ln_w, ln_b, r_k, name="rwkv_post_bwd", carry=(_SiblingPlan, late_blocks))
    late_pair = [pair_sum(g, t, BF16, name="pair_sum_" + n) for g, t, n in zip(late_blocks, late_from_sibling, LATE_WEIGHTS)]
    res = wkv_bwd(r, dec, k2, v, a, b, dy, states, name="wkv_bwd", exchange=[q for _, q in late_pair])
    wkv_grads, late_landed = res[:6], res[6:]
    (dpr, d_mu, G['rwkv_w0'], G['rwkv_a0'], G['rwkv_k_k'], G['rwkv_k_a'], G['rwkv_r_k'], d_wlw, d_wla, d_wlg) = rwkv_pre_bwd(
        pr, pr_shift, *wkv_grads, dz, dg, mu, w0, a0, k_k, k_a, r_k, wlw, wla, wlg, name="rwkv_pre_bwd")
    G['rwkv_mu'] = _unpad_rwkv_cols(d_mu)
    G['rwkv_w_lora_up'], G['rwkv_a_lora_up'], G['rwkv_g_lora_up'] = d_wlw[:DECAY_LORA], d_wla[:ICLR_LORA], d_wlg[:GATE_LORA]
    dq, dko, dkn, dvo, dvn, G['attn_q_norm'], d_sinks = att_bwd(pa, dya, qg, kg, sinks, name="att_bwd")
    G['attn_sinks'] = d_sinks[:, :N_HEADS]
    dpa, G['attn_k_norm'] = att_kv_bwd(pa, dq, dko, dkn, dvo, dvn, kg, name="att_kv_bwd")
    d_w_in_r = mm(h2, dpr, ta=True, name="d_w_in_rwkv")
    d_w_in_a = mm(h2, dpa, ta=True, name="d_w_in_att")
    d_w_in_g = mm(h2, dpg, ta=True, name="d_w_in_gate")
    G['w_in'] = jnp.concatenate([_unpad_rwkv_cols(d_w_in_r), d_w_in_a, d_w_in_g], axis=1)
    mid_blocks = [_to_blocks(G[n], axis_of[n]) for n in MID_WEIGHTS]
    dh2, *mid_from_sibling = mm(dpr, w_in_r, tb=True, name="d_h2_rwkv", carry=(_SiblingPlan, mid_blocks))
    dh2 = mm(dpa, w_in_a, tb=True, res=dh2, name="d_h2_att")
    dx1, G['mix_norm'] = mm(dpg, w_in_g, tb=True, res=dh2, name="d_h2_gate", norm_bwd=(x1, nmix, dx2))
    mid_pair = [pair_sum(g, t, BF16, name="pair_sum_" + n) for g, t, n in zip(mid_blocks, mid_from_sibling, MID_WEIGHTS)]
    dx0, G['ffn1_norm'], _, _, _, mid_landed, (first_pair, first_landed) = _ffn_bwd(
        dx1, x, ffn1, n1, full['ffn1_w_gate'], full['ffn1_w_up'], full['ffn1_w_down'], "ffn1",
        carry=(_ExchangePlan, [q for _, q in mid_pair]), reduce=FIRST_WEIGHTS)

    small_shapes = [(w[n].size,) for n in SMALL] + [(1,)]

    def small_rows(parts):
        vec = jnp.concatenate([p.reshape(-1) for p in parts])
        return jnp.pad(vec, (0, SMALL_ROWS * FLAT_W - vec.shape[0])).reshape(SMALL_ROWS, FLAT_W)

    small = small_rows([G[n] for n in SMALL] + [loss[0, :1]])
    small_pair = _pair_sums(['small'], [jnp.broadcast_to(small[None], (N_CHIPS,) + small.shape)], "small")
    small_landed = exchange_chips([q for _, q in small_pair], name="exchange_small")
    names = FIRST_WEIGHTS + ['small'] + MID_WEIGHTS + LATE_WEIGHTS
    pair = first_pair + small_pair + mid_pair + late_pair
    landed = list(first_landed) + list(small_landed) + list(mid_landed) + list(late_landed)
    halves = [half_sum(own, l, name="half_sum_" + n) for (own, _), l, n in zip(pair, landed, names)]
    other_halves = to_sibling(halves, False, name="halves_to_sibling")

    def local(prefix, n):
        if n != 'small':
            return natural(n, A[prefix + n])
        return small_rows([A[prefix + s] for s in SMALL] + [jnp.zeros((1,), F32)])[None]

    result = {}
    for n, mine, theirs in zip(names, halves, other_halves):
        outs4 = adamw(local('', n), local('m_', n), local('v_', n), mine, theirs, name="adamw_" + n)
        for kind, o in zip(('grad_', 'delta_', 'new_m_', 'new_v_'), outs4):
            if n != 'small':
                result[kind + n] = natural(n, o)
            else:
                for s, part in zip(SMALL + ['loss'], _unpack_vec(o.reshape(-1), small_shapes)):
                    result[kind + s] = part.reshape(A[s].shape) if s != 'loss' else part.reshape(())
    outs = [result['grad_loss'], dx0[None]]
    for kind in ('grad_', 'delta_', 'new_m_', 'new_v_'):
        outs += [result[kind + n] for n in WEIGHT_NAMES]
    return tuple(outs)


def _unpack_vec(vec, shapes):
    out, off = [], 0
    for (n,) in shapes:
        out.append(vec[off:off + n])
        off += n
    return out


def kernel(x, ffn1_norm, ffn1_w_gate, ffn1_w_up, ffn1_w_down, mix_norm, w_in, rwkv_mu, rwkv_w0, rwkv_w_lora_up, rwkv_a0, rwkv_a_lora_up, rwkv_g_lora_up, rwkv_k_k, rwkv_k_a, rwkv_r_k, rwkv_ln_w, rwkv_ln_b, attn_q_norm, attn_k_norm, attn_sinks, w_branch_rwkv, w_branch_attn, w_out, ffn2_norm, ffn2_w_gate, ffn2_w_up, ffn2_w_down, final_norm, loss_target, m_ffn1_norm, m_ffn1_w_gate, m_ffn1_w_up, m_ffn1_w_down, m_mix_norm, m_w_in, m_rwkv_mu, m_rwkv_w0, m_rwkv_w_lora_up, m_rwkv_a0, m_rwkv_a_lora_up, m_rwkv_g_lora_up, m_rwkv_k_k, m_rwkv_k_a, m_rwkv_r_k, m_rwkv_ln_w, m_rwkv_ln_b, m_attn_q_norm, m_attn_k_norm, m_attn_sinks, m_w_branch_rwkv, m_w_branch_attn, m_w_out, m_ffn2_norm, m_ffn2_w_gate, m_ffn2_w_up, m_ffn2_w_down, m_final_norm, v_ffn1_norm, v_ffn1_w_gate, v_ffn1_w_up, v_ffn1_w_down, v_mix_norm, v_w_in, v_rwkv_mu, v_rwkv_w0, v_rwkv_w_lora_up, v_rwkv_a0, v_rwkv_a_lora_up, v_rwkv_g_lora_up, v_rwkv_k_k, v_rwkv_k_a, v_rwkv_r_k, v_rwkv_ln_w, v_rwkv_ln_b, v_attn_q_norm, v_attn_k_norm, v_attn_sinks, v_w_branch_rwkv, v_w_branch_attn, v_w_out, v_ffn2_norm, v_ffn2_w_gate, v_ffn2_w_up, v_ffn2_w_down, v_final_norm):
    return _step(dict(locals()))
```

```python
import functools

import jax
import jax.numpy as jnp
from jax import lax
from jax.experimental import pallas as pl
from jax.experimental.pallas import tpu as pltpu

F32 = jnp.float32
BF16 = jnp.bfloat16

D_MODEL = 1024
D_FF = 2816
HEAD_DIM = 64
N_HEADS = 8
RW = 512
KVW = 128
ATT_GROUP = 4
WINDOW = 128
BLOCK = 128
DECAY_LORA, ICLR_LORA, GATE_LORA = 32, 32, 96
RWKV_COLS = 3 * RW + DECAY_LORA + ICLR_LORA + GATE_LORA
ATT_COLS = RW + 2 * KVW
GATE_COLS = 2 * D_MODEL
RWKV_PAD = 3 * RW + 3 * 128
RMS_EPS = 1e-6
GN_EPS = 64e-5
N_CHIPS = 4
LANE = 128
FLAT_W = 1024
SMALL_ROWS = 32
NEG_BIG = -1e30

ADAM_LR, ADAM_B1, ADAM_B2, ADAM_EPS, ADAM_WD, ADAM_STEP = 0.001, 0.9, 0.999, 1e-08, 0.01, 10

VMEM_LIMIT = 56 * 1024 * 1024

WEIGHT_NAMES = ['ffn1_norm', 'ffn1_w_gate', 'ffn1_w_up', 'ffn1_w_down', 'mix_norm', 'w_in', 'rwkv_mu', 'rwkv_w0',
                'rwkv_w_lora_up', 'rwkv_a0', 'rwkv_a_lora_up', 'rwkv_g_lora_up', 'rwkv_k_k', 'rwkv_k_a', 'rwkv_r_k',
                'rwkv_ln_w', 'rwkv_ln_b', 'attn_q_norm', 'attn_k_norm', 'attn_sinks', 'w_branch_rwkv',
                'w_branch_attn', 'w_out', 'ffn2_norm', 'ffn2_w_gate', 'ffn2_w_up', 'ffn2_w_down', 'final_norm']
BIG = [('ffn1_w_gate', 1), ('ffn1_w_up', 1), ('ffn1_w_down', 0), ('w_in', 1), ('rwkv_w_lora_up', 1),
       ('rwkv_a_lora_up', 1), ('rwkv_g_lora_up', 1), ('w_branch_rwkv', 1), ('w_branch_attn', 1), ('w_out', 0),
       ('ffn2_w_gate', 1), ('ffn2_w_up', 1), ('ffn2_w_down', 0)]
SMALL = ['ffn1_norm', 'mix_norm', 'rwkv_mu', 'rwkv_w0', 'rwkv_a0', 'rwkv_k_k', 'rwkv_k_a', 'rwkv_r_k', 'rwkv_ln_w',
         'rwkv_ln_b', 'attn_q_norm', 'attn_k_norm', 'attn_sinks', 'ffn2_norm', 'final_norm']


def _pcall(body, **kw):
    return pl.pallas_call(body, **kw)


def _params(sem=None, **kw):
    if sem is not None:
        kw['dimension_semantics'] = sem
    return pltpu.CompilerParams(vmem_limit_bytes=VMEM_LIMIT, **kw)


def _tile(n, cap, mult):
    best = None
    for t in range(mult, min(n, cap) + 1, mult):
        if n % t == 0:
            best = t
    return best or n


def _sigmoid(z):
    return 1.0 / (1.0 + jnp.exp(-z))


def _softplus(z):
    return jnp.maximum(z, 0.0) + jnp.log(1.0 + jnp.exp(-jnp.abs(z)))


def _bdot(a, b, dims=(((1,), (0,)), ((), ()))):
    return lax.dot_general(a.astype(BF16), b.astype(BF16), dims, preferred_element_type=F32)


_NT = (((1,), (1,)), ((), ()))
_TN = (((0,), (0,)), ((), ()))


def _segsum(x, bd):
    hi = x.astype(BF16)
    lo = (x - hi.astype(F32)).astype(BF16)
    dot = functools.partial(lax.dot_general, dimension_numbers=(((1,), (0,)), ((), ())), preferred_element_type=F32)
    return dot(hi, bd) + dot(lo, bd)


_LORA_EDGES = (3 * RW, 3 * RW + DECAY_LORA, 3 * RW + DECAY_LORA + ICLR_LORA, RWKV_COLS)


def _pad_rwkv_cols(x):
    parts = [x[..., :3 * RW]]
    for lo, hi in zip(_LORA_EDGES[:-1], _LORA_EDGES[1:]):
        parts.append(jnp.pad(x[..., lo:hi], [(0, 0)] * (x.ndim - 1) + [(0, 128 - (hi - lo))]))
    return jnp.concatenate(parts, axis=-1)


def _unpad_rwkv_cols(x):
    parts = [x[..., :3 * RW]]
    for j, (lo, hi) in enumerate(zip(_LORA_EDGES[:-1], _LORA_EDGES[1:])):
        parts.append(x[..., 3 * RW + 128 * j:3 * RW + 128 * j + (hi - lo)])
    return jnp.concatenate(parts, axis=-1)


def _pad_rows(x, rows):
    return jnp.pad(x, [(0, rows - x.shape[0])] + [(0, 0)] * (x.ndim - 1))


def mm(a, b, *, name, ta=False, tb=False, scale=None, res=None, out_dtype=F32, carry=None, norm_bwd=None):
    M, K = (a.shape[1], a.shape[0]) if ta else a.shape
    N = b.shape[0] if tb else b.shape[1]
    assert (b.shape[1] if tb else b.shape[0]) == K
    tm, tn, tk = _tile(M, 1408 if ta else 512, 128), _tile(N, 1408, 128), _tile(K, 1408, 128)
    nk = K // tk
    grid = (M // tm, N // tn, nk)
    dims = (((0 if ta else 1,), (1 if tb else 0,)), ((), ()))
    plan_cls, carried = carry if carry else (None, ())
    nn = 3 if norm_bwd else 0
    nd = 1 if norm_bwd else 0
    nc, nin = len(carried), 2 + (res is not None) + nn
    assert not norm_bwd or (tn == N and out_dtype == F32)

    def body(*refs):
        a_ref, b_ref = refs[:2]
        r_ref = refs[2] if res is not None else None
        o_ref, acc_ref = refs[nin + nc], refs[nin + 2 * nc + 1 + nd]
        row_tile, k = pl.program_id(0), pl.program_id(2)
        if nc:
            plan = plan_cls(refs[nin:nin + nc], refs[nin + nc + 1 + nd:nin + 2 * nc + 1 + nd], refs[nin + 2 * nc + 2 + nd:])
            at = lambda which: functools.reduce(jnp.logical_and, [pl.program_id(d) == (0 if which == 0 else grid[d] - 1)
                                                                 for d in range(3)])
            pl.when(at(0))(plan.start)
        part = _bdot(a_ref[...], b_ref[...], dims)

        @pl.when(k == 0)
        def _():
            acc_ref[...] = part

        @pl.when(k > 0)
        def _():
            acc_ref[...] += part

        @pl.when(k == nk - 1)
        def _():
            o = acc_ref[...]
            if scale is not None:
                o = o * scale
            if r_ref is not None:
                o = o + r_ref[...].astype(F32)
            if norm_bwd:
                x_ref, g_ref, dres_ref = refs[nin - 3:nin]
                xv = x_ref[...]
                r = lax.rsqrt(jnp.mean(xv * xv, axis=-1, keepdims=True) + RMS_EPS)
                xh = xv * r
                dxh = o * g_ref[...]
                _acc_rows(refs[nin + nc + 1], jnp.sum(o * xh, axis=0, keepdims=True), row_tile)
                o = dres_ref[...] + r * (dxh - xh * jnp.mean(dxh * xh, axis=-1, keepdims=True))
            o_ref[...] = o.astype(out_dtype)

        if nc:
            pl.when(at(1))(plan.finish)

    a_spec = pl.BlockSpec((tk, tm), lambda i, j, k: (k, i)) if ta else pl.BlockSpec((tm, tk), lambda i, j, k: (i, k))
    b_spec = pl.BlockSpec((tn, tk), lambda i, j, k: (j, k)) if tb else pl.BlockSpec((tk, tn), lambda i, j, k: (k, j))
    o_spec = pl.BlockSpec((tm, tn), lambda i, j, k: (i, j))
    g_spec = pl.BlockSpec((1, N), lambda i, j, k: (0, 0))
    in_specs = [a_spec, b_spec] + ([o_spec] if res is not None else []) + ([o_spec, g_spec, o_spec] if norm_bwd else [])
    args = (a, b) + ((res,) if res is not None else ()) + (tuple(norm_bwd) if norm_bwd else ())
    out_shape = [jax.ShapeDtypeStruct((M, N), out_dtype)] + [jax.ShapeDtypeStruct((1, N), F32)] * nd
    out_specs = [o_spec] + [g_spec] * nd
    if not nc and not nd:
        return _pcall(
            body, name=name, grid=grid, in_specs=in_specs, out_specs=o_spec, out_shape=out_shape[0],
            scratch_shapes=[pltpu.VMEM((tm, tn), F32)], compiler_params=_params(("parallel", "parallel", "arbitrary")),
        )(*args)
    return _pcall(
        body, name=name, grid=grid, in_specs=in_specs + [_HBM] * nc, out_specs=out_specs + [_HBM] * nc,
        out_shape=out_shape + (plan_cls.out_shapes(carried) if nc else []),
        scratch_shapes=[pltpu.VMEM((tm, tn), F32)] + (plan_cls.sems(nc) if nc else []),
        compiler_params=_params(("arbitrary", "arbitrary", "arbitrary")),
    )(*args, *carried)


def mm_fused(a, bs, finish, out_dtypes, *, name, tb=False, extras=(), carry=None):
    M, K = a.shape
    N = bs[0].shape[0] if tb else bs[0].shape[1]
    tm, tn = _tile(M, 512, 128), _tile(N, 1408, 128)
    grid = (M // tm, N // tn)
    dims = (((1,), (1 if tb else 0,)), ((), ()))
    plan_cls, carried = carry if carry else (None, ())
    nc, nb, nx, no = len(carried), len(bs), len(extras), len(out_dtypes)
    nin = 1 + nb + nx

    def body(*refs):
        a_ref, b_refs, x_refs = refs[0], refs[1:1 + nb], refs[1 + nb:nin]
        o_refs = refs[nin + nc:nin + nc + no]
        if nc:
            plan = plan_cls(refs[nin:nin + nc], refs[nin + nc + no:nin + 2 * nc + no], refs[nin + 2 * nc + no:])
            at = lambda which: jnp.logical_and(*[pl.program_id(d) == (0 if which == 0 else grid[d] - 1) for d in range(2)])
            pl.when(at(0))(plan.start)
        av = a_ref[...]
        outs = finish([_bdot(av, b_ref[...], dims) for b_ref in b_refs], [x_ref[...] for x_ref in x_refs])
        for o_ref, o in zip(o_refs, outs):
            o_ref[...] = o.astype(o_ref.dtype)
        if nc:
            pl.when(at(1))(plan.finish)

    a_spec = pl.BlockSpec((tm, K), lambda i, j: (i, 0))
    b_spec = pl.BlockSpec((tn, K), lambda i, j: (j, 0)) if tb else pl.BlockSpec((K, tn), lambda i, j: (0, j))
    o_spec = pl.BlockSpec((tm, tn), lambda i, j: (i, j))
    out_shape = [jax.ShapeDtypeStruct((M, N), d) for d in out_dtypes]
    if not nc:
        return _pcall(body, name=name, grid=grid, in_specs=[a_spec] + [b_spec] * nb + [o_spec] * nx, out_specs=[o_spec] * no,
                      out_shape=out_shape, compiler_params=_params(("parallel", "parallel")))(a, *bs, *extras)
    return _pcall(body, name=name, grid=grid, in_specs=[a_spec] + [b_spec] * nb + [o_spec] * nx + [_HBM] * nc,
                  out_specs=[o_spec] * no + [_HBM] * nc, out_shape=out_shape + plan_cls.out_shapes(carried),
                  scratch_shapes=plan_cls.sems(nc), compiler_params=_params(("arbitrary", "arbitrary")))(a, *bs, *extras, *carried)


def _swiglu(products, _):
    g, u = products
    return g, u, g * _sigmoid(g) * u


def _swiglu_bwd(products, extras):
    (da,), (gate, up) = products, extras
    gv = gate.astype(F32)
    s = _sigmoid(gv)
    return da * 0.5 * up.astype(F32) * s * (1.0 + gv * (1.0 - s)), da * 0.5 * gv * s


def _row_spec(tr, c):
    return pl.BlockSpec((tr, c), lambda i: (i, 0))


def _full_spec(shape):
    return pl.BlockSpec(shape, lambda i: (0,) * len(shape))


def _acc_rows(ref, val, i):
    @pl.when(i == 0)
    def _():
        ref[...] = val

    @pl.when(i > 0)
    def _():
        ref[...] += val


def rms_fwd(x, g, *, name):
    T, D = x.shape
    tr = _tile(T, 512, 8)

    def body(x_ref, g_ref, h_ref):
        xv = x_ref[...]
        r = lax.rsqrt(jnp.mean(xv * xv, axis=-1, keepdims=True) + RMS_EPS)
        h_ref[...] = (xv * r * g_ref[...]).astype(BF16)

    return _pcall(body, name=name, grid=(T // tr,), in_specs=[_row_spec(tr, D), _full_spec((1, D))],
                  out_specs=_row_spec(tr, D), out_shape=jax.ShapeDtypeStruct((T, D), BF16),
                  compiler_params=_params(("parallel",)))(x, g)


def final_loss(x, tgt, g, *, name):
    T, D = x.shape
    tr = _tile(T, 256, 8)

    def body(x_ref, t_ref, g_ref, dx_ref, dg_ref, loss_ref):
        i = pl.program_id(0)
        xv = x_ref[...]
        r = lax.rsqrt(jnp.mean(xv * xv, axis=-1, keepdims=True) + RMS_EPS)
        xh = xv * r
        e = xh * g_ref[...] - t_ref[...]
        part = 0.5 * jnp.sum(jnp.mean(e * e, axis=-1, keepdims=True), axis=0, keepdims=True)
        dy = e * (1.0 / D)
        dxh = dy * g_ref[...]
        dx_ref[...] = r * (dxh - xh * jnp.mean(dxh * xh, axis=-1, keepdims=True))
        _acc_rows(dg_ref, jnp.sum(dy * xh, axis=0, keepdims=True), i)
        _acc_rows(loss_ref, jnp.broadcast_to(part, (1, LANE)), i)

    return _pcall(body, name=name, grid=(T // tr,),
                  in_specs=[_row_spec(tr, D), _row_spec(tr, D), _full_spec((1, D))],
                  out_specs=[_row_spec(tr, D), _full_spec((1, D)), _full_spec((1, LANE))],
                  out_shape=[jax.ShapeDtypeStruct((T, D), F32), jax.ShapeDtypeStruct((1, D), F32),
                             jax.ShapeDtypeStruct((1, LANE), F32)],
                  compiler_params=_params(("arbitrary",)))(x, tgt, g)


def merge_fwd(br, ba, pg, *, name):
    T, D = br.shape
    tr = _tile(T, 256, 8)

    def body(br_ref, ba_ref, pg_ref, o_ref):
        pgv = pg_ref[...]
        o_ref[...] = (_sigmoid(pgv[:, :D]) * br_ref[...] + _sigmoid(pgv[:, D:]) * ba_ref[...]).astype(BF16)

    return _pcall(body, name=name, grid=(T // tr,), in_specs=[_row_spec(tr, D), _row_spec(tr, D), _row_spec(tr, 2 * D)],
                  out_specs=_row_spec(tr, D), out_shape=jax.ShapeDtypeStruct((T, D), BF16),
                  compiler_params=_params(("parallel",)))(br, ba, pg)


def merge_bwd(dm, br, ba, pg, *, name):
    T, D = br.shape
    tr = _tile(T, 256, 8)

    def body(dm_ref, br_ref, ba_ref, pg_ref, dbr_ref, dba_ref, dpg_ref):
        pgv, dmv = pg_ref[...], dm_ref[...]
        sr, sa = _sigmoid(pgv[:, :D]), _sigmoid(pgv[:, D:])
        dbr_ref[...] = (dmv * sr).astype(BF16)
        dba_ref[...] = (dmv * sa).astype(BF16)
        dpg_ref[:, :D] = dmv * br_ref[...] * sr * (1.0 - sr)
        dpg_ref[:, D:] = dmv * ba_ref[...] * sa * (1.0 - sa)

    return _pcall(body, name=name, grid=(T // tr,),
                  in_specs=[_row_spec(tr, D), _row_spec(tr, D), _row_spec(tr, D), _row_spec(tr, 2 * D)],
                  out_specs=[_row_spec(tr, D), _row_spec(tr, D), _row_spec(tr, 2 * D)],
                  out_shape=[jax.ShapeDtypeStruct((T, D), BF16), jax.ShapeDtypeStruct((T, D), BF16),
                             jax.ShapeDtypeStruct((T, 2 * D), F32)],
                  compiler_params=_params(("parallel",)))(dm, br, ba, pg)


def _rwkv_mix(p, prev, mu, w0, a0, k_k, k_a, wlw, wla, wlg, bd):
    pp = p + (prev - p) * mu
    r, k, v = pp[:, 0:RW], pp[:, RW:2 * RW], pp[:, 2 * RW:3 * RW]
    xw, xa, xg = pp[:, 3 * RW:3 * RW + 128], pp[:, 3 * RW + 128:3 * RW + 256], pp[:, 3 * RW + 256:3 * RW + 384]
    th = jnp.tanh(xw)
    z = -(w0 + _bdot(th, wlw))
    e = jnp.exp(-_softplus(z) - 0.5)
    decay = jnp.exp(-e)
    a = _sigmoid(a0 + _bdot(xa, wla))
    sg = _sigmoid(xg)
    kkr = k * k_k
    n = jnp.sqrt(_segsum(kkr * kkr, bd))
    kk = kkr / jnp.maximum(n, 1e-12)
    k2 = k * (1.0 + (a - 1.0) * k_a)
    return dict(r=r, k=k, v=v, xa=xa, th=th, z=z, e=e, decay=decay, a=a, sg=sg, n=n, kk=kk, k2=k2)


def _seg_matrix(n, shift):
    r = lax.shift_right_logical(lax.broadcasted_iota(jnp.int32, (n, n), 0), shift)
    c = lax.shift_right_logical(lax.broadcasted_iota(jnp.int32, (n, n), 1), shift)
    return jnp.where(r == c, 1.0, 0.0).astype(BF16)


def rwkv_pre_fwd(p, pshift, mu, w0, a0, k_k, k_a, wlw, wla, wlg, *, name):
    T = p.shape[0]
    tr = _tile(T, 256, 8)

    def body(p_ref, ps_ref, mu_ref, w0_ref, a0_ref, kk_ref, ka_ref, wlw_ref, wla_ref, wlg_ref,
             r_ref, w_ref, k_ref, v_ref, a_ref, b_ref, g_ref):
        pv, prev = p_ref[...], ps_ref[...]
        m = _rwkv_mix(pv, prev, mu_ref[...], w0_ref[...], a0_ref[...], kk_ref[...], ka_ref[...],
                      wlw_ref[...], wla_ref[...], wlg_ref[...], _seg_matrix(RW, 6))
        r_ref[...] = m['r']
        w_ref[...] = m['decay']
        k_ref[...] = m['k2']
        v_ref[...] = m['v']
        a_ref[...] = -m['kk']
        b_ref[...] = m['kk'] * m['a']
        g_ref[...] = m['sg']

    vec = _row_spec(tr, RW)
    return _pcall(
        body, name=name, grid=(T // tr,),
        in_specs=[_row_spec(tr, RWKV_PAD), _row_spec(tr, RWKV_PAD), _full_spec((1, RWKV_PAD))] + [_full_spec((1, RW))] * 4
        + [_full_spec((128, RW))] * 3,
        out_specs=[vec] * 6 + [_row_spec(tr, 128)],
        out_shape=[jax.ShapeDtypeStruct((T, RW), F32)] * 6 + [jax.ShapeDtypeStruct((T, 128), F32)],
        compiler_params=_params(("parallel",)),
    )(p, pshift, mu, w0, a0, k_k, k_a, wlw, wla, wlg)


def _group_norm(y, bd):
    mean = _segsum(y, bd) * (1.0 / HEAD_DIM)
    yc = y - mean
    rstd = lax.rsqrt(_segsum(yc * yc, bd) * (1.0 / HEAD_DIM) + GN_EPS)
    return yc * rstd, rstd


def rwkv_post_fwd(y, r, k2, v, sg, wlg, ln_w, ln_b, r_k, *, name):
    T = y.shape[0]
    tr = _tile(T, 256, 8)

    def body(y_ref, r_ref, k_ref, v_ref, sg_ref, wlg_ref, lw_ref, lb_ref, rk_ref, o_ref):
        bd = _seg_matrix(RW, 6)
        yn, _ = _group_norm(y_ref[...], bd)
        s = _segsum(r_ref[...] * k_ref[...] * rk_ref[...], bd)
        g = _bdot(sg_ref[...], wlg_ref[...])
        o_ref[...] = ((yn * lw_ref[...] + lb_ref[...] + s * v_ref[...]) * g).astype(BF16)

    vec = _row_spec(tr, RW)
    return _pcall(body, name=name, grid=(T // tr,),
                  in_specs=[vec] * 4 + [_row_spec(tr, 128), _full_spec((128, RW))] + [_full_spec((1, RW))] * 3, out_specs=vec,
                  out_shape=jax.ShapeDtypeStruct((T, RW), BF16), compiler_params=_params(("parallel",)))(
                      y, r, k2, v, sg, wlg, ln_w, ln_b, r_k)


def rwkv_post_bwd(dyr, y, r, k2, v, sg, wlg, ln_w, ln_b, r_k, *, name, carry=None):
    T = y.shape[0]
    tr = _tile(T, 256, 8)
    nt = T // tr
    plan_cls, carried = carry if carry else (None, ())
    nc = len(carried)

    def body(*refs):
        dyr_ref, y_ref, r_ref, k_ref, v_ref, sg_ref, wlg_ref, lw_ref, lb_ref, rk_ref = refs[:10]
        dy_ref, dz_ref, dg_ref, dlw_ref, dlb_ref = refs[10 + nc:15 + nc]
        i = pl.program_id(0)
        if nc:
            plan = plan_cls(refs[10:10 + nc], refs[15 + nc:15 + 2 * nc], refs[15 + 2 * nc:])
            pl.when(i == 0)(plan.start)
        bd = _seg_matrix(RW, 6)
        yn, rstd = _group_norm(y_ref[...], bd)
        s = _segsum(r_ref[...] * k_ref[...] * rk_ref[...], bd)
        dyrv = dyr_ref[...]
        dg_ref[...] = dyrv * (yn * lw_ref[...] + lb_ref[...] + s * v_ref[...])
        dz = dyrv * _bdot(sg_ref[...], wlg_ref[...])
        dz_ref[...] = dz
        dyn = dz * lw_ref[...]
        inv = 1.0 / HEAD_DIM
        dy_ref[...] = rstd * (dyn - _segsum(dyn, bd) * inv - yn * (_segsum(dyn * yn, bd) * inv))
        _acc_rows(dlw_ref, jnp.sum(dz * yn, axis=0, keepdims=True), i)
        _acc_rows(dlb_ref, jnp.sum(dz, axis=0, keepdims=True), i)
        if nc:
            pl.when(i == nt - 1)(plan.finish)

    vec = _row_spec(tr, RW)
    one = _full_spec((1, RW))
    return _pcall(body, name=name, grid=(nt,),
                  in_specs=[vec] * 5 + [_row_spec(tr, 128), _full_spec((128, RW))] + [one] * 3 + [_HBM] * nc,
                  out_specs=[vec] * 3 + [one] * 2 + [_HBM] * nc,
                  out_shape=[jax.ShapeDtypeStruct((T, RW), F32)] * 3 + [jax.ShapeDtypeStruct((1, RW), F32)] * 2
                  + (plan_cls.out_shapes(carried) if nc else []),
                  scratch_shapes=plan_cls.sems(nc) if nc else [],
                  compiler_params=_params(("arbitrary",)))(dyr, y, r, k2, v, sg, wlg, ln_w, ln_b, r_k, *carried)


def rwkv_pre_bwd(p, pshift, dr_w, dw_w, dk_w, dv_w, da_w, db_w, dz, dg, mu, w0, a0, k_k, k_a, r_k, wlw, wla, wlg, *, name):
    T = p.shape[0]
    tr = _tile(T, 256, 8)
    n = T // tr

    def body(p_ref, ps_ref, dr_ref, dw_ref, dk_ref, dv_ref, da_ref, db_ref, dz_ref, dg_ref,
             mu_ref, w0_ref, a0_ref, kk_ref, ka_ref, rk_ref, wlw_ref, wla_ref, wlg_ref,
             dp_ref, dmu_ref, dw0_ref, da0_ref, dkk_ref, dka_ref, drk_ref, dwlw_ref, dwla_ref, dwlg_ref,
             carry, dpp, acc_w, acc_a, acc_g):
        i = pl.program_id(0)

        @pl.when(i == 0)
        def _():
            carry[...] = jnp.zeros_like(carry)

        pv, prev, mu = p_ref[...], ps_ref[...], mu_ref[...]
        bd = _seg_matrix(RW, 6)
        k_k, k_a, r_k = kk_ref[...], ka_ref[...], rk_ref[...]
        m = _rwkv_mix(pv, prev, mu, w0_ref[...], a0_ref[...], k_k, k_a, wlw_ref[...], wla_ref[...], wlg_ref[...], bd)
        r, k, v, a, kk, k2 = m['r'], m['k'], m['v'], m['a'], m['kk'], m['k2']
        dzv, dgv = dz_ref[...], dg_ref[...]
        s = _segsum(r * k2 * r_k, bd)
        ds = _segsum(dzv * v, bd)
        dr = dr_ref[...] + ds * k2 * r_k
        dk2 = dk_ref[...] + ds * r * r_k
        dv = dv_ref[...] + dzv * s
        dbv = db_ref[...]
        dkk = dbv * a - da_ref[...]
        da = dbv * kk + dk2 * k * k_a
        dk = dk2 * (1.0 + (a - 1.0) * k_a)
        nmax = jnp.maximum(m['n'], 1e-12)
        dkkr = jnp.where(m['n'] > 1e-12, dkk - kk * _segsum(dkk * kk, bd), dkk) / nmax
        dk = dk + dkkr * k_k
        dapre = da * a * (1.0 - a)
        dwpre = dw_ref[...] * m['decay'] * (-m['e']) * _sigmoid(m['z'])
        dth = _bdot(dwpre, wlw_ref[...], _NT)
        dxa = _bdot(dapre, wla_ref[...], _NT)
        dsg = _bdot(dgv, wlg_ref[...], _NT)
        dpp[:, 0:RW] = dr
        dpp[:, RW:2 * RW] = dk
        dpp[:, 2 * RW:3 * RW] = dv
        dpp[:, 3 * RW:3 * RW + 128] = dth * (1.0 - m['th'] * m['th'])
        dpp[:, 3 * RW + 128:3 * RW + 256] = dxa
        dpp[:, 3 * RW + 256:3 * RW + 384] = dsg * m['sg'] * (1.0 - m['sg'])
        d = dpp[...]
        zed = d * mu
        last = lax.broadcasted_iota(jnp.int32, pv.shape, 0) == tr - 1
        dp_ref[...] = d * (1.0 - mu) + jnp.where(last, carry[0:1, :], pltpu.roll(zed, tr - 1, 0))
        carry[...] = zed[0:8, :]

        def colsum(x):
            return jnp.sum(x, axis=0, keepdims=True)

        _acc_rows(dmu_ref, colsum(d * (prev - pv)), i)
        _acc_rows(dw0_ref, colsum(dwpre), i)
        _acc_rows(da0_ref, colsum(dapre), i)
        _acc_rows(dkk_ref, colsum(dkkr * k), i)
        _acc_rows(dka_ref, colsum(dk2 * k * (a - 1.0)), i)
        _acc_rows(drk_ref, colsum(ds * r * k2), i)
        _acc_rows(acc_w, _bdot(m['th'], dwpre, _TN), i)
        _acc_rows(acc_a, _bdot(m['xa'], dapre, _TN), i)
        _acc_rows(acc_g, _bdot(m['sg'], dgv, _TN), i)

        @pl.when(i == n - 1)
        def _():
            dwlw_ref[...] = acc_w[...]
            dwla_ref[...] = acc_a[...]
            dwlg_ref[...] = acc_g[...]

    rev = lambda c: pl.BlockSpec((tr, c), lambda i: (n - 1 - i, 0))
    one, lora = _full_spec((1, RW)), _full_spec((128, RW))
    return _pcall(
        body, name=name, grid=(n,),
        in_specs=[rev(RWKV_PAD), rev(RWKV_PAD)] + [rev(RW)] * 8 + [_full_spec((1, RWKV_PAD))] + [one] * 5 + [lora] * 3,
        out_specs=[rev(RWKV_PAD), _full_spec((1, RWKV_PAD))] + [one] * 5 + [lora] * 3,
        out_shape=[jax.ShapeDtypeStruct((T, RWKV_PAD), F32), jax.ShapeDtypeStruct((1, RWKV_PAD), F32)]
        + [jax.ShapeDtypeStruct((1, RW), F32)] * 5 + [jax.ShapeDtypeStruct((128, RW), F32)] * 3,
        scratch_shapes=[pltpu.VMEM((8, RWKV_PAD), F32), pltpu.VMEM((tr, RWKV_PAD), F32)] + [pltpu.VMEM((128, RW), F32)] * 3,
        compiler_params=_params(("arbitrary",)),
    )(p, pshift, dr_w, dw_w, dk_w, dv_w, da_w, db_w, dz, dg, mu, w0, a0, k_k, k_a, r_k, wlw, wla, wlg)


def _qk_norm(x, g, bd):
    r = lax.rsqrt(_segsum(x * x, bd) * (1.0 / HEAD_DIM) + RMS_EPS)
    return x * r * g, r


def _att_mask(i):
    qi = lax.broadcasted_iota(jnp.int32, (BLOCK, 2 * BLOCK), 0)
    kj = lax.broadcasted_iota(jnp.int32, (BLOCK, 2 * BLOCK), 1)
    band = (kj <= qi + BLOCK) & (kj > qi + BLOCK - WINDOW)
    return band & ((kj >= BLOCK) | (i > 0))


_HQK = (((2,), (2,)), ((0,), (0,)))
_HPV = (((2,), (1,)), ((0,), (0,)))
_HTN = (((1,), (1,)), ((0,), (0,)))


def _heads(x, n):
    return jnp.stack([x[:, h * HEAD_DIM:(h + 1) * HEAD_DIM] for h in range(n)])


def _unheads(x3):
    return jnp.concatenate([x3[h] for h in range(x3.shape[0])], axis=1)


def _kv_heads(x):
    x2 = _heads(x, KVW // HEAD_DIM)
    return jnp.concatenate([x2[g:g + 1] for g in range(KVW // HEAD_DIM) for _ in range(ATT_GROUP)], axis=0)


def _sinks3(sk):
    return jnp.stack([sk[0:1, h:h + 1] for h in range(N_HEADS)])


def _att_probs(q3, k3, mask, sink):
    s = _bdot(q3, k3, _HQK) * (HEAD_DIM ** -0.5)
    s = jnp.where(mask[None], s, NEG_BIG)
    m = jnp.maximum(jnp.max(s, axis=-1, keepdims=True), sink)
    pexp = jnp.exp(s - m)
    psink = jnp.exp(sink - m)
    inv = 1.0 / (jnp.sum(pexp, axis=-1, keepdims=True) + psink)
    return pexp * inv, psink * inv


ATT_SUB = 2


def _att_blocks(sub):
    cur = pl.BlockSpec((sub * BLOCK, ATT_COLS), lambda i: (i, 0))
    prev = pl.BlockSpec((BLOCK, ATT_COLS), lambda i: (jnp.maximum(i * sub - 1, 0), 0))
    return cur, prev


def _att_sub(cur_all, prev_first, j):
    cur = cur_all[j * BLOCK:(j + 1) * BLOCK]
    return cur, (prev_first if j == 0 else cur_all[(j - 1) * BLOCK:j * BLOCK])


def _att_qkv(cur, prev, qn_g, kn_g):
    bq, bk = _seg_matrix(RW, 6), _seg_matrix(KVW, 6)
    qn, rq = _qk_norm(cur[:, 0:RW], qn_g, bq)
    kcur, rkc = _qk_norm(cur[:, RW:RW + KVW], kn_g, bk)
    kprev, _ = _qk_norm(prev[:, RW:RW + KVW], kn_g, bk)
    kc = jnp.concatenate([kprev, kcur], axis=0)
    vc = jnp.concatenate([prev[:, RW + KVW:], cur[:, RW + KVW:]], axis=0)
    return qn, rq, kc, vc, rkc


def att_fwd(pa, qn_g, kn_g, sinks, *, name):
    T = pa.shape[0]
    sub = ATT_SUB if (T // BLOCK) % ATT_SUB == 0 else 1
    n = T // (sub * BLOCK)

    def body(cur_ref, prev_ref, qg_ref, kg_ref, sk_ref, o_ref):
        i = pl.program_id(0)
        cur_all, prev_first = cur_ref[...], prev_ref[...]
        for j in range(sub):
            cur, prev = _att_sub(cur_all, prev_first, j)
            qn, _, kc, vc, _ = _att_qkv(cur, prev, qg_ref[...], kg_ref[...])
            probs, _ = _att_probs(_heads(qn, N_HEADS), _kv_heads(kc), _att_mask(i * sub + j), _sinks3(sk_ref[...]))
            o_ref[j * BLOCK:(j + 1) * BLOCK, :] = _unheads(_bdot(probs, _kv_heads(vc), _HPV))

    cur, prev = _att_blocks(sub)
    return _pcall(body, name=name, grid=(n,),
                  in_specs=[cur, prev, _full_spec((1, RW)), _full_spec((1, KVW)), _full_spec((1, LANE))],
                  out_specs=pl.BlockSpec((sub * BLOCK, RW), lambda i: (i, 0)), out_shape=jax.ShapeDtypeStruct((T, RW), F32),
                  compiler_params=_params(("parallel",)))(pa, pa, qn_g, kn_g, sinks)


def att_bwd(pa, do, qn_g, kn_g, sinks, *, name):
    T = pa.shape[0]
    sub = ATT_SUB if (T // BLOCK) % ATT_SUB == 0 else 1
    n = T // (sub * BLOCK)

    def one_block(cur, prev, do, blk, qg_ref, kg_ref, sk_ref, rows, dq_ref, dko_ref, dkn_ref, dvo_ref, dvn_ref):
        qn, rq, kc, vc, _ = _att_qkv(cur, prev, qg_ref[...], kg_ref[...])
        q3, k3, v3, do3 = _heads(qn, N_HEADS), _kv_heads(kc), _kv_heads(vc), _heads(do, N_HEADS)
        probs, psink = _att_probs(q3, k3, _att_mask(blk), _sinks3(sk_ref[...]))
        dprobs = _bdot(do3, v3, _HQK)
        delta = jnp.sum(probs * dprobs, axis=-1, keepdims=True)
        ds = probs * (dprobs - delta) * (HEAD_DIM ** -0.5)
        dsink3 = -jnp.sum(psink * delta, axis=1, keepdims=True)
        lane = lax.broadcasted_iota(jnp.int32, (1, LANE), 1)
        dsink = jnp.zeros((1, LANE), F32)
        for h in range(N_HEADS):
            dsink = dsink + jnp.where(lane == h, dsink3[h], 0.0)
        dqn = _unheads(_bdot(ds, k3, _HPV))

        def per_kv_head(x3):
            groups = [sum(x3[g * ATT_GROUP + j] for j in range(ATT_GROUP)) for g in range(KVW // HEAD_DIM)]
            return jnp.concatenate(groups, axis=1)

        dk, dv = per_kv_head(_bdot(ds, q3, _HTN)), per_kv_head(_bdot(probs, do3, _HTN))
        dkn_ref[rows, :], dko_ref[rows, :] = dk[0:BLOCK], dk[BLOCK:]
        dvn_ref[rows, :], dvo_ref[rows, :] = dv[0:BLOCK], dv[BLOCK:]
        qhat = cur[:, 0:RW] * rq
        dqh = dqn * qg_ref[...]
        dq_ref[rows, :] = rq * (dqh - qhat * (_segsum(dqh * qhat, _seg_matrix(RW, 6)) * (1.0 / HEAD_DIM)))
        prod = dqn * qhat
        fold = prod[:, 0:HEAD_DIM]
        for h in range(1, N_HEADS):
            fold = fold + prod[:, h * HEAD_DIM:(h + 1) * HEAD_DIM]
        return jnp.sum(fold, axis=0, keepdims=True), dsink

    def body(cur_ref, prev_ref, do_ref, qg_ref, kg_ref, sk_ref,
             dq_ref, dko_ref, dkn_ref, dvo_ref, dvn_ref, dqg_ref, dsk_ref):
        i = pl.program_id(0)
        cur_all, prev_first, do_all = cur_ref[...], prev_ref[...], do_ref[...]
        dqg, dsk = None, None
        for j in range(sub):
            cur, prev = _att_sub(cur_all, prev_first, j)
            rows = slice(j * BLOCK, (j + 1) * BLOCK)
            g, s = one_block(cur, prev, do_all[rows], i * sub + j, qg_ref, kg_ref, sk_ref, rows,
                             dq_ref, dko_ref, dkn_ref, dvo_ref, dvn_ref)
            dqg, dsk = (g, s) if dqg is None else (dqg + g, dsk + s)
        _acc_rows(dqg_ref, dqg, i)
        _acc_rows(dsk_ref, dsk, i)

    cur, prev = _att_blocks(sub)
    kvb = pl.BlockSpec((sub * BLOCK, KVW), lambda i: (i, 0))
    qb = pl.BlockSpec((sub * BLOCK, RW), lambda i: (i, 0))
    return _pcall(body, name=name, grid=(n,),
                  in_specs=[cur, prev, qb, _full_spec((1, RW)), _full_spec((1, KVW)), _full_spec((1, LANE))],
                  out_specs=[qb, kvb, kvb, kvb, kvb, _full_spec((1, HEAD_DIM)), _full_spec((1, LANE))],
                  out_shape=[jax.ShapeDtypeStruct((T, RW), F32)] + [jax.ShapeDtypeStruct((T, KVW), F32)] * 4
                  + [jax.ShapeDtypeStruct((1, HEAD_DIM), F32), jax.ShapeDtypeStruct((1, LANE), F32)],
                  compiler_params=_params(("arbitrary",)))(pa, pa, do, qn_g, kn_g, sinks)


def att_kv_bwd(pa, dq, dko, dkn, dvo, dvn, kn_g, *, name):
    T = pa.shape[0]
    n = T // BLOCK

    def body(pa_ref, dq_ref, dko_ref, dkn_ref, dvo_ref, dvn_ref, kg_ref, dpa_ref, dkg_ref):
        i = pl.program_id(0)
        more = i < n - 1
        dkn_tot = dko_ref[...] + jnp.where(more, dkn_ref[...], 0.0)
        dv_tot = dvo_ref[...] + jnp.where(more, dvn_ref[...], 0.0)
        kraw = pa_ref[:, RW:RW + KVW]
        bk = _seg_matrix(KVW, 6)
        _, rk = _qk_norm(kraw, kg_ref[...], bk)
        khat = kraw * rk
        dkh = dkn_tot * kg_ref[...]
        dpa_ref[:, 0:RW] = dq_ref[...]
        dpa_ref[:, RW:RW + KVW] = rk * (dkh - khat * (_segsum(dkh * khat, bk) * (1.0 / HEAD_DIM)))
        dpa_ref[:, RW + KVW:] = dv_tot
        prod = dkn_tot * khat
        _acc_rows(dkg_ref, jnp.sum(prod[:, 0:HEAD_DIM] + prod[:, HEAD_DIM:], axis=0, keepdims=True), i)

    kvb = pl.BlockSpec((BLOCK, KVW), lambda i: (i, 0))
    nxt = pl.BlockSpec((BLOCK, KVW), lambda i: (jnp.minimum(i + 1, n - 1), 0))
    return _pcall(body, name=name, grid=(n,),
                  in_specs=[pl.BlockSpec((BLOCK, ATT_COLS), lambda i: (i, 0)), pl.BlockSpec((BLOCK, RW), lambda i: (i, 0)),
                            kvb, nxt, kvb, nxt, _full_spec((1, KVW))],
                  out_specs=[pl.BlockSpec((BLOCK, ATT_COLS), lambda i: (i, 0)), _full_spec((1, HEAD_DIM))],
                  out_shape=[jax.ShapeDtypeStruct((T, ATT_COLS), F32), jax.ShapeDtypeStruct((1, HEAD_DIM), F32)],
                  compiler_params=_params(("arbitrary",)))(pa, dq, dko, dkn, dvo, dvn, kn_g)


WKV_CHUNK = 128
WKV_GROUP = 8


def _diag_mask():
    i = lax.broadcasted_iota(jnp.int32, (HEAD_DIM, RW), 0)
    j = lax.broadcasted_iota(jnp.int32, (HEAD_DIM, RW), 1) & (HEAD_DIM - 1)
    return i == j


def _heads_matrix():
    head = jnp.arange(RW // 2) // HEAD_DIM
    bd = (head[:, None] == head[None, :]).astype(BF16)
    return jnp.concatenate([bd, bd], axis=0)


def _headsums(xs, pieces, bd2):
    half = RW // 2
    bd = bd2[:pieces * half]
    rows = []
    for x in xs:
        parts, rest = [], x
        for n in range(pieces):
            p = rest.astype(BF16)
            parts.append(p)
            if n + 1 < pieces:
                rest = rest - p.astype(F32)
        for sl in (slice(0, half), slice(half, RW)):
            rows.append(jnp.concatenate([p[:, sl] for p in parts], axis=1))
    out = lax.dot_general(jnp.concatenate(rows, axis=0), bd, (((1,), (0,)), ((), ())), preferred_element_type=F32)
    return [jnp.concatenate([out[2 * n * HEAD_DIM:(2 * n + 1) * HEAD_DIM], out[(2 * n + 1) * HEAD_DIM:(2 * n + 2) * HEAD_DIM]],
                            axis=1) for n in range(len(xs))]


def _headsum(x):
    low = lax.broadcasted_iota(jnp.int32, (HEAD_DIM, LANE), 1) < HEAD_DIM
    tiles = []
    for c in range(RW // LANE):
        xt = x[:, c * LANE:(c + 1) * LANE]
        s_lo = jnp.sum(jnp.where(low, xt, 0.0), axis=1, keepdims=True)
        s_hi = jnp.sum(jnp.where(low, 0.0, xt), axis=1, keepdims=True)
        tiles.append(jnp.where(low, s_lo, s_hi))
    return jnp.concatenate(tiles, axis=1)


def _cols(rows, diag, bd2, pieces=2):
    return _headsums([jnp.where(diag, r, 0.0) for r in rows], pieces, bd2)


def _tilesum(x):
    return jnp.sum(x.reshape(HEAD_DIM // 8, 8, RW), axis=0)


def _finish_colsums(parts):
    q = jnp.concatenate(parts + [jnp.zeros((LANE - HEAD_DIM, RW), F32)], axis=0).astype(BF16)
    sel = (lax.broadcasted_iota(jnp.int32, (8, LANE), 1) // 8 == lax.broadcasted_iota(jnp.int32, (8, LANE), 0))
    return lax.dot_general(jnp.where(sel, 1.0, 0.0).astype(BF16), q, (((1,), (0,)), ((), ())), preferred_element_type=F32)


def wkv_fwd(r, w, k, v, a, b, *, name, gather=()):
    T = r.shape[0]
    ch = min(WKV_CHUNK, T)
    ngroups = ch // WKV_GROUP
    nchunks = T // ch
    ng = len(gather)

    def body(*refs):
        r_ref, w_ref, k_ref, v_ref, a_ref, b_ref, bd_ref = refs[:7]
        y_ref, st_ref = refs[7 + ng:9 + ng]
        s_scr = refs[9 + 2 * ng]
        step = pl.program_id(0)
        if ng:
            plan = _GatherPlan(refs[7:7 + ng], refs[9 + ng:9 + 2 * ng], refs[10 + 2 * ng:])
            pl.when(step == 0)(plan.start)
            pl.when(step == nchunks // 2)(plan.relay)

        @pl.when(step == 0)
        def _():
            s_scr[...] = jnp.zeros_like(s_scr)

        diag, bd2 = _diag_mask(), bd_ref[...]

        def group(gi, S):
            t0 = pl.multiple_of(gi * WKV_GROUP, WKV_GROUP)
            rows = pl.ds(t0, WKV_GROUP)
            R, W, K, V, A, B = (ref[rows, :] for ref in (r_ref, w_ref, k_ref, v_ref, a_ref, b_ref))
            vcols = _cols([V[s:s + 1] for s in range(WKV_GROUP)], diag, bd2, 1)
            yrows = []
            for s in range(WKV_GROUP):
                sa = _headsum(S * A[s:s + 1])
                S = S * W[s:s + 1] + sa * B[s:s + 1] + vcols[s] * K[s:s + 1]
                st_ref[t0 + s] = S
                yrows.append(_tilesum(jnp.where(diag, _headsums([S * R[s:s + 1]], 1, bd2)[0], 0.0)))
            y_ref[rows, :] = _finish_colsums(yrows)
            return S

        s_scr[...] = lax.fori_loop(0, ngroups, group, s_scr[...])
        if ng:
            pl.when(step == nchunks - 1)(plan.finish_relayed)

    vec = pl.BlockSpec((ch, RW), lambda c: (c, 0))
    return _pcall(
        body, name=name, grid=(nchunks,), in_specs=[vec] * 6 + [_full_spec((RW, RW // 2))] + [_HBM] * ng,
        out_specs=[vec, pl.BlockSpec((ch, HEAD_DIM, RW), lambda c: (c, 0, 0))] + [_HBM] * ng,
        out_shape=[jax.ShapeDtypeStruct((T, RW), F32), jax.ShapeDtypeStruct((T, HEAD_DIM, RW), F32)] + _gathered_shapes(gather),
        scratch_shapes=[pltpu.VMEM((HEAD_DIM, RW), F32)] + (_GatherPlan.sems(ng) if ng else []),
        compiler_params=_params(("arbitrary",)),
    )(r, w, k, v, a, b, _heads_matrix(), *gather)


def wkv_bwd(r, w, k, v, a, b, dy, states, *, name, exchange=()):
    T = r.shape[0]
    ch = min(WKV_CHUNK, T)
    nchunks = T // ch
    ngroups = ch // WKV_GROUP
    ne = len(exchange)

    def body(*refs):
        r_ref, w_ref, k_ref, v_ref, a_ref, b_ref, dy_ref, st_ref, stp_ref, bd_ref = refs[:10]
        dr_ref, dw_ref, dk_ref, dv_ref, da_ref, db_ref = refs[10 + ne:16 + ne]
        ds_scr = refs[16 + 2 * ne]
        step = pl.program_id(0)
        if ne:
            plan = _ExchangePlan(refs[10:10 + ne], refs[16 + ne:16 + 2 * ne], refs[17 + 2 * ne:])
            pl.when(step == 0)(plan.start)

        @pl.when(step == 0)
        def _():
            ds_scr[...] = jnp.zeros_like(ds_scr)

        has_prev_chunk = step < nchunks - 1
        diag, bd2 = _diag_mask(), bd_ref[...]
        colsum = _tilesum

        def group(gj, dS):
            gi = ngroups - 1 - gj
            t0 = pl.multiple_of(gi * WKV_GROUP, WKV_GROUP)
            rows = pl.ds(t0, WKV_GROUP)
            R, W, K, V, A, B, DY = (ref[rows, :] for ref in (r_ref, w_ref, k_ref, v_ref, a_ref, b_ref, dy_ref))
            before = jnp.where(gi > 0, st_ref[jnp.maximum(t0 - 1, 0)], jnp.where(has_prev_chunk, stp_ref[0], 0.0))
            prev_state = lambda s: st_ref[t0 + s - 1] if s > 0 else before
            steps = range(WKV_GROUP)
            dycols = _cols([DY[s:s + 1] for s in steps], diag, bd2, 1)
            vcols = _cols([V[s:s + 1] for s in steps], diag, bd2, 1)
            sas = _headsums([prev_state(s) * A[s:s + 1] for s in steps], 1, bd2)
            got = [[None] * WKV_GROUP for _ in range(6)]
            for s in reversed(steps):
                Sp = prev_state(s)
                dS = dS + dycols[s] * R[s:s + 1]
                got[0][s] = colsum(st_ref[t0 + s] * dycols[s])
                got[3][s] = _tilesum(jnp.where(diag, _headsums([dS * K[s:s + 1]], 1, bd2)[0], 0.0))
                got[2][s] = colsum(dS * vcols[s])
                dsa = _headsum(dS * B[s:s + 1])
                got[5][s] = colsum(dS * sas[s])
                got[1][s] = colsum(dS * Sp)
                got[4][s] = colsum(Sp * dsa)
                dS = dS * W[s:s + 1] + dsa * A[s:s + 1]
            for q, ref in enumerate((dr_ref, dw_ref, dk_ref, dv_ref, da_ref, db_ref)):
                ref[rows, :] = _finish_colsums(got[q])
            return dS

        ds_scr[...] = lax.fori_loop(0, ngroups, group, ds_scr[...])
        if ne:
            pl.when(step == nchunks - 1)(plan.finish)

    vec = pl.BlockSpec((ch, RW), lambda c: (nchunks - 1 - c, 0))
    st_spec = pl.BlockSpec((ch, HEAD_DIM, RW), lambda c: (nchunks - 1 - c, 0, 0))
    stp_spec = pl.BlockSpec((1, HEAD_DIM, RW), lambda c: (jnp.maximum((nchunks - 1 - c) * ch - 1, 0), 0, 0))
    return _pcall(
        body, name=name, grid=(nchunks,), in_specs=[vec] * 7 + [st_spec, stp_spec, _full_spec((RW, RW // 2))] + [_HBM] * ne,
        out_specs=[vec] * 6 + [_HBM] * ne,
        out_shape=[jax.ShapeDtypeStruct((T, RW), F32)] * 6 + [jax.ShapeDtypeStruct(e.shape, e.dtype) for e in exchange],
        scratch_shapes=[pltpu.VMEM((HEAD_DIM, RW), F32)] + (_ExchangePlan.sems(ne) if ne else []),
        compiler_params=_params(("arbitrary",)),
    )(r, w, k, v, a, b, dy, states, states, _heads_matrix(), *exchange)


_HBM = pl.BlockSpec(memory_space=pltpu.HBM)
_MESH = pl.DeviceIdType.MESH


def _place():
    x, y, c = lax.axis_index("x"), lax.axis_index("y"), lax.axis_index("c")
    return x, y, c, [(1 - x, y), (x, 1 - y), (1 - x, 1 - y)]


def _remote(src, dst, send_sem, recv_sem, to):
    return pltpu.make_async_remote_copy(src_ref=src, dst_ref=dst, send_sem=send_sem, recv_sem=recv_sem, device_id=to,
                                        device_id_type=_MESH)


def _dma_sems(*counts):
    return [pltpu.SemaphoreType.DMA((n,)) for n in counts]


class _GatherPlan:
    def __init__(self, ins, outs, sems):
        self.ins, self.outs, self.n = ins, outs, len(ins)
        self.ici_send, self.ici_recv, self.d2d_send, self.d2d_recv, self.local_sems = sems
        x, y, c, chips = _place()
        self.c, self.me, self.sibling = c, 2 * x + y, (x, y, 1 - c)
        self.peers = [(2 * qx + qy, (qx, qy, c)) for qx, qy in chips]

    @staticmethod
    def sems(n):
        return _dma_sems(3 * n, 3 * n, 3 * n, 3 * n, n)

    def _half(self, i, which):
        rh = self.ins[i].shape[0] // 2
        return pl.ds(which * rh, rh)

    def _local(self, i):
        return pltpu.make_async_copy(self.ins[i], self.outs[i].at[self.me], self.local_sems.at[i])

    def _send(self, i, j):
        k, mine = 3 * i + j, self._half(i, self.c)
        return _remote(self.ins[i].at[mine], self.outs[i].at[self.me, mine], self.ici_send.at[k], self.ici_recv.at[k],
                       self.peers[j][1])

    def _landed(self, i, j):
        k, piece = 3 * i + j, self.outs[i].at[self.peers[j][0], self._half(i, self.c)]
        return _remote(piece, piece, self.ici_send.at[k], self.ici_recv.at[k], self.peers[j][1])

    def _pass(self, i, j, which):
        k, piece = 3 * i + j, self.outs[i].at[self.peers[j][0], self._half(i, which)]
        return _remote(piece, piece, self.d2d_send.at[k], self.d2d_recv.at[k], self.sibling)

    def _all(self):
        return [(i, j) for i in range(self.n) for j in range(3)]

    def start(self):
        for i in range(self.n):
            self._local(i).start()
        for i, j in self._all():
            self._send(i, j).start()

    def relay(self):
        for i, j in self._all():
            self._landed(i, j).wait_recv()
            self._pass(i, j, self.c).start()

    def finish_relayed(self):
        for i, j in self._all():
            self._pass(i, j, 1 - self.c).wait_recv()
        for i, j in self._all():
            self._send(i, j).wait_send()
            self._pass(i, j, self.c).wait_send()
        for i in range(self.n):
            self._local(i).wait()

    def finish(self):
        self.relay()
        self.finish_relayed()

    @staticmethod
    def out_shapes(shards):
        return _gathered_shapes(shards)


def _gathered_shapes(shards):
    return [jax.ShapeDtypeStruct((N_CHIPS,) + s.shape, s.dtype) for s in shards]


def gather_weights(shards, *, name):
    n = len(shards)

    def body(*refs):
        plan = _GatherPlan(refs[:n], refs[n:2 * n], refs[2 * n:])
        plan.start()
        plan.finish()

    return _pcall(body, name=name, in_specs=[_HBM] * n, out_specs=[_HBM] * n, out_shape=_gathered_shapes(shards),
                  scratch_shapes=_GatherPlan.sems(n), compiler_params=_params())(*shards)


class _SiblingPlan:
    halves = True

    def __init__(self, ins, outs, sems):
        self.ins, self.outs, self.n = ins, outs, len(ins)
        self.send_sems, self.recv_sems = sems
        x, y, c, _ = _place()
        self.c, self.sibling = c, (x, y, 1 - c)

    @staticmethod
    def sems(n):
        return _dma_sems(n, n)

    @classmethod
    def out_shapes(cls, arrays):
        if not cls.halves:
            return [jax.ShapeDtypeStruct(a.shape, a.dtype) for a in arrays]
        return [jax.ShapeDtypeStruct((a.shape[0], a.shape[1] // 2, a.shape[2]), a.dtype) for a in arrays]

    def _copy(self, i):
        src = self.ins[i]
        if self.halves:
            rh = src.shape[1] // 2
            src = src.at[:, pl.ds((1 - self.c) * rh, rh)]
        return _remote(src, self.outs[i], self.send_sems.at[i], self.recv_sems.at[i], self.sibling)

    def start(self):
        for i in range(self.n):
            self._copy(i).start()

    def finish(self):
        for i in range(self.n):
            self._copy(i).wait_recv()
        for i in range(self.n):
            self._copy(i).wait_send()


class _SiblingWhole(_SiblingPlan):
    halves = False


def to_sibling(arrays, take_other_half, *, name):
    n = len(arrays)
    plan_cls = _SiblingPlan if take_other_half else _SiblingWhole

    def body(*refs):
        plan = plan_cls(refs[:n], refs[n:2 * n], refs[2 * n:])
        plan.start()
        plan.finish()

    return _pcall(body, name=name, in_specs=[_HBM] * n, out_specs=[_HBM] * n, out_shape=plan_cls.out_shapes(arrays),
                  scratch_shapes=plan_cls.sems(n), compiler_params=_params())(*arrays)


def exchange_chips(arrays, *, name):
    n = len(arrays)

    def body(*refs):
        plan = _ExchangePlan(refs[:n], refs[n:2 * n], refs[2 * n:])
        plan.start()
        plan.finish()

    return _pcall(body, name=name, in_specs=[_HBM] * n, out_specs=[_HBM] * n,
                  out_shape=[jax.ShapeDtypeStruct(a.shape, a.dtype) for a in arrays],
                  scratch_shapes=_ExchangePlan.sems(n), compiler_params=_params())(*arrays)


class _ExchangePlan:
    def __init__(self, ins, outs, sems):
        self.ins, self.outs, self.n = ins, outs, len(ins)
        self.send_sems, self.recv_sems, self.local_sems = sems
        x, y, c, chips = _place()
        self.me = 2 * x + y
        self.peers = [(2 * qx + qy, (qx, qy, c)) for qx, qy in chips]

    @staticmethod
    def sems(n):
        return _dma_sems(3 * n, 3 * n, n)

    @staticmethod
    def out_shapes(arrays):
        return [jax.ShapeDtypeStruct(a.shape, a.dtype) for a in arrays]

    def _local(self, i):
        return pltpu.make_async_copy(self.ins[i].at[self.me], self.outs[i].at[self.me], self.local_sems.at[i])

    def _send(self, i, j):
        k = 3 * i + j
        return _remote(self.ins[i].at[self.peers[j][0]], self.outs[i].at[self.me], self.send_sems.at[k], self.recv_sems.at[k],
                       self.peers[j][1])

    def _landed(self, i, j):
        k, piece = 3 * i + j, self.outs[i].at[self.peers[j][0]]
        return _remote(piece, piece, self.send_sems.at[k], self.recv_sems.at[k], self.peers[j][1])

    def start(self):
        for i in range(self.n):
            self._local(i).start()
            for j in range(3):
                self._send(i, j).start()

    def finish(self):
        for i in range(self.n):
            for j in range(3):
                self._landed(i, j).wait_recv()
        for i in range(self.n):
            for j in range(3):
                self._send(i, j).wait_send()
            self._local(i).wait()


def _core_index():
    return lax.axis_index("c").astype(jnp.int32).reshape(1)


def pair_sum(g, theirs, wire_dtype, *, name):
    _, R, C = g.shape
    rh = R // 2
    tr = _tile(rh, 256, 16)
    nt = rh // tr

    def body(c_ref, g_ref, t_ref, q_ref, qw_ref):
        q = g_ref[...] + t_ref[...]
        q_ref[...] = q
        qw_ref[...] = q.astype(wire_dtype)

    blk = pl.BlockSpec((1, tr, C), lambda b, i, c_ref: (b, i, 0))
    mine = pl.BlockSpec((1, tr, C), lambda b, i, c_ref: (b, c_ref[0] * nt + i, 0))
    grid_spec = pltpu.PrefetchScalarGridSpec(num_scalar_prefetch=1, grid=(N_CHIPS, nt), in_specs=[mine, blk], out_specs=[blk, blk])
    return _pcall(body, name=name, grid_spec=grid_spec,
                  out_shape=[jax.ShapeDtypeStruct((N_CHIPS, rh, C), F32), jax.ShapeDtypeStruct((N_CHIPS, rh, C), wire_dtype)],
                  compiler_params=_params(("parallel", "parallel")))(_core_index(), g, theirs)


def half_sum(own, landed, *, name):
    _, rh, C = own.shape
    tr = _tile(rh, 256, 16)

    def body(me_ref, own_ref, land_ref, o_ref):
        total = None
        for p in range(N_CHIPS):
            term = jnp.where(me_ref[0] == p, own_ref[p], land_ref[p].astype(F32))
            total = term if total is None else total + term
        o_ref[...] = total

    blk = pl.BlockSpec((N_CHIPS, tr, C), lambda i, me_ref: (0, i, 0))
    grid_spec = pltpu.PrefetchScalarGridSpec(num_scalar_prefetch=1, grid=(rh // tr,), in_specs=[blk, blk],
                                             out_specs=pl.BlockSpec((tr, C), lambda i, me_ref: (i, 0)))
    me = (2 * lax.axis_index("x") + lax.axis_index("y")).astype(jnp.int32).reshape(1)
    return _pcall(body, name=name, grid_spec=grid_spec, out_shape=jax.ShapeDtypeStruct((rh, C), F32),
                  compiler_params=_params(("parallel",)))(me, own, landed)


def adamw(w, m, v, mine, theirs, *, name):
    _, R, C = w.shape
    rh = R // 2
    tr = _tile(rh, 256, 8)
    nt = rh // tr

    def body(c_ref, w_ref, m_ref, v_ref, a_ref, b_ref, g_ref, d_ref, nm_ref, nv_ref):
        is_mine = (pl.program_id(0) // nt) == c_ref[0]
        g = jnp.where(is_mine, a_ref[...], b_ref[...])
        g_ref[...] = g
        nm = ADAM_B1 * m_ref[...] + (1.0 - ADAM_B1) * g
        nv = ADAM_B2 * v_ref[...] + (1.0 - ADAM_B2) * (g * g)
        nm_ref[...] = nm
        nv_ref[...] = nv
        m_hat = nm / (1.0 - ADAM_B1 ** ADAM_STEP)
        v_hat = nv / (1.0 - ADAM_B2 ** ADAM_STEP)
        d_ref[...] = -ADAM_LR * (m_hat / (jnp.sqrt(v_hat) + ADAM_EPS) + ADAM_WD * w_ref[...])

    full = pl.BlockSpec((None, tr, C), lambda i, c_ref: (0, i, 0))
    a_spec = pl.BlockSpec((tr, C), lambda i, c_ref: (jnp.clip(i - c_ref[0] * nt, 0, nt - 1), 0))
    b_spec = pl.BlockSpec((tr, C), lambda i, c_ref: (jnp.clip(i - (1 - c_ref[0]) * nt, 0, nt - 1), 0))
    grid_spec = pltpu.PrefetchScalarGridSpec(num_scalar_prefetch=1, grid=(2 * nt,), in_specs=[full] * 3 + [a_spec, b_spec],
                                             out_specs=[full] * 4)
    return _pcall(body, name=name, grid_spec=grid_spec, out_shape=[jax.ShapeDtypeStruct((1, R, C), F32)] * 4,
                  compiler_params=_params(("arbitrary",)))(_core_index(), w, m, v, mine, theirs)


def _to_blocks(full, axis):
    r, c = full.shape
    if axis == 1:
        return full.reshape(r, N_CHIPS, c // N_CHIPS).transpose(1, 0, 2)
    return full.reshape(N_CHIPS, r // N_CHIPS, c)


def _from_blocks(blocks, axis):
    _, r, c = blocks.shape
    if axis == 1:
        return blocks.transpose(1, 0, 2).reshape(r, N_CHIPS * c)
    return blocks.reshape(N_CHIPS * r, c)


def _ffn_fwd(x, norm, wg_t, wu_t, wd, tag):
    h = rms_fwd(x, norm, name=tag + "_norm")
    gate, up, act = mm_fused(h, [wg_t, wu_t], _swiglu, [BF16] * 3, tb=True, name=tag + "_gate_up")
    out = mm(act, wd, scale=0.5, res=x, name=tag + "_down")
    return out, (h, gate, up, act)


def _ffn_bwd(dout, x, saved, norm, wg_t, wu_t, wd, tag, carry=None, reduce=None):
    h, gate, up, act = saved
    dgate, dup, *carried = mm_fused(dout, [wd], _swiglu_bwd, [BF16] * 2, tb=True, extras=[gate, up], name=tag + "_dact",
                                    carry=carry)
    dwd = mm(act, dout, ta=True, scale=0.5, name=tag + "_dwd")
    dwg_t = mm(dgate, h, ta=True, name=tag + "_dwg")
    dwu_t = mm(dup, h, ta=True, name=tag + "_dwu")
    reduced = None
    if reduce:
        blocks = [_to_blocks(g, 0) for g in (dwg_t, dwu_t, dwd)]
        dh, *from_sibling = mm(dgate, wg_t, name=tag + "_dh_gate", carry=(_SiblingPlan, blocks))
        pair = [pair_sum(g, t, BF16, name="pair_sum_" + n) for g, t, n in zip(blocks, from_sibling, reduce)]
        dx, dnorm, *landed = mm(dup, wu_t, res=dh, name=tag + "_dh_up", norm_bwd=(x, norm, dout),
                                carry=(_ExchangePlan, [q for _, q in pair]))
        reduced = (pair, landed)
    else:
        dh = mm(dgate, wg_t, name=tag + "_dh_gate")
        dx, dnorm = mm(dup, wu_t, res=dh, name=tag + "_dh_up", norm_bwd=(x, norm, dout))
    return dx, dnorm, dwg_t, dwu_t, dwd, carried, reduced


TRANSPOSED = ('ffn1_w_gate', 'ffn1_w_up', 'ffn2_w_gate', 'ffn2_w_up')
FIRST_WEIGHTS = ['ffn1_w_gate', 'ffn1_w_up', 'ffn1_w_down']
MID_WEIGHTS = ['w_in', 'rwkv_w_lora_up', 'rwkv_a_lora_up', 'rwkv_g_lora_up']
LATE_WEIGHTS = ['w_branch_rwkv', 'w_branch_attn', 'w_out', 'ffn2_w_gate', 'ffn2_w_up', 'ffn2_w_down']


def _pair_sums(names, blocks, tag):
    from_sibling = to_sibling(blocks, True, name=tag + "_grads_to_sibling")
    return [pair_sum(g, t, F32 if n == 'small' else BF16, name="pair_sum_" + n)
            for g, t, n in zip(blocks, from_sibling, names)]


def _step(A):
    x, tgt = A['x'][0], A['loss_target'][0]
    T = x.shape[0]
    w = {n: A[n][0] for n in WEIGHT_NAMES}
    row = lambda a: a.reshape(1, -1)

    axis_of = {n: (0 if n in TRANSPOSED else axis) for n, axis in BIG}
    natural = lambda n, a: jnp.swapaxes(a, 1, 2) if n in TRANSPOSED else a
    shard = lambda n: natural(n, A[n])[0].astype(BF16)
    n1, nmix, n2, nfin = (row(w[n]) for n in ('ffn1_norm', 'mix_norm', 'ffn2_norm', 'final_norm'))
    gathered = gather_weights([shard(n) for n in FIRST_WEIGHTS[:2]], name="gather_weights")
    full = {n: _from_blocks(b, axis_of[n]) for n, b in zip(FIRST_WEIGHTS[:2], gathered)}
    h1 = rms_fwd(x, n1, name="ffn1_norm")
    gate1, up1, act1, down_blocks = mm_fused(h1, [full['ffn1_w_gate'], full['ffn1_w_up']], _swiglu, [BF16] * 3, tb=True,
                                             name="ffn1_gate_up", carry=(_GatherPlan, [shard('ffn1_w_down')]))
    full['ffn1_w_down'] = _from_blocks(down_blocks, axis_of['ffn1_w_down'])
    x1, *gathered = mm(act1, full['ffn1_w_down'], scale=0.5, res=x, name="ffn1_down",
                       carry=(_GatherPlan, [shard(n) for n in MID_WEIGHTS]))
    ffn1 = (h1, gate1, up1, act1)
    full.update({n: _from_blocks(b, axis_of[n]) for n, b in zip(MID_WEIGHTS, gathered)})
    w_in_r = _pad_rwkv_cols(full['w_in'][:, :RWKV_COLS])
    w_in_a = full['w_in'][:, RWKV_COLS:RWKV_COLS + ATT_COLS]
    w_in_g = full['w_in'][:, RWKV_COLS + ATT_COLS:]
    wlw, wla, wlg = (_pad_rows(full[n], 128).astype(F32) for n in ('rwkv_w_lora_up', 'rwkv_a_lora_up', 'rwkv_g_lora_up'))
    mu = _pad_rwkv_cols(row(w['rwkv_mu']))
    w0, a0, k_k, k_a, r_k, ln_w, ln_b = (row(w[n]) for n in ('rwkv_w0', 'rwkv_a0', 'rwkv_k_k', 'rwkv_k_a', 'rwkv_r_k',
                                                               'rwkv_ln_w', 'rwkv_ln_b'))
    qg = jnp.tile(row(w['attn_q_norm']), (1, N_HEADS))
    kg = jnp.tile(row(w['attn_k_norm']), (1, KVW // HEAD_DIM))
    sinks = jnp.pad(row(w['attn_sinks']), ((0, 0), (0, LANE - N_HEADS)))

    h2 = rms_fwd(x1, nmix, name="mix_norm")
    pr = mm(h2, w_in_r, name="proj_rwkv")
    pa = mm(h2, w_in_a, name="proj_att")
    pg = mm(h2, w_in_g, name="proj_gate")
    pr_shift = jnp.pad(pr, ((1, 0), (0, 0)))[:-1]
    r, dec, k2, v, a, b, sg = rwkv_pre_fwd(pr, pr_shift, mu, w0, a0, k_k, k_a, wlw, wla, wlg, name="rwkv_pre")
    y, states, *gathered = wkv_fwd(r, dec, k2, v, a, b, name="wkv_fwd", gather=[shard(n) for n in LATE_WEIGHTS])
    full.update({n: _from_blocks(b, axis_of[n]) for n, b in zip(LATE_WEIGHTS, gathered)})
    yr = rwkv_post_fwd(y, r, k2, v, sg, wlg, ln_w, ln_b, r_k, name="rwkv_post")
    ya = att_fwd(pa, qg, kg, sinks, name="att_fwd")
    br = mm(yr, full['w_branch_rwkv'], name="branch_rwkv")
    ba = mm(ya, full['w_branch_attn'], name="branch_att")
    mg = merge_fwd(br, ba, pg, name="merge")
    x2 = mm(mg, full['w_out'], res=x1, name="mix_out")
    x3, ffn2 = _ffn_fwd(x2, n2, full['ffn2_w_gate'], full['ffn2_w_up'], full['ffn2_w_down'], "ffn2")
    dx3, d_nfin, loss = final_loss(x3, tgt, nfin, name="final_loss")

    G = {'final_norm': d_nfin}
    dx2, G['ffn2_norm'], G['ffn2_w_gate'], G['ffn2_w_up'], G['ffn2_w_down'], _, _ = _ffn_bwd(
        dx3, x2, ffn2, n2, full['ffn2_w_gate'], full['ffn2_w_up'], full['ffn2_w_down'], "ffn2")
    dmg = mm(dx2, full['w_out'], tb=True, name="d_merge")
    G['w_out'] = mm(mg, dx2, ta=True, name="d_w_out")
    dbr, dba, dpg = merge_bwd(dmg, br, ba, pg, name="merge_bwd")
    dyr = mm(dbr, full['w_branch_rwkv'], tb=True, name="d_y_rwkv")
    G['w_branch_rwkv'] = mm(yr, dbr, ta=True, name="d_w_branch_rwkv")
    dya = mm(dba, full['w_branch_attn'], tb=True, name="d_y_att")
    G['w_branch_attn'] = mm(ya, dba, ta=True, name="d_w_branch_att")
    late_blocks = [_to_blocks(G[n], axis_of[n]) for n in LATE_WEIGHTS]
    dy, dz, dg, G['rwkv_ln_w'], G['rwkv_ln_b'], *late_from_sibling = rwkv_post_bwd(
        dyr, y, r, k2, v, sg, wlg, ln_w, ln_b, r_k, name="rwkv_post_bwd", carry=(_SiblingPlan, late_blocks))
    late_pair = [pair_sum(g, t, BF16, name="pair_sum_" + n) for g, t, n in zip(late_blocks, late_from_sibling, LATE_WEIGHTS)]
    res = wkv_bwd(r, dec, k2, v, a, b, dy, states, name="wkv_bwd", exchange=[q for _, q in late_pair])
    wkv_grads, late_landed = res[:6], res[6:]
    (dpr, d_mu, G['rwkv_w0'], G['rwkv_a0'], G['rwkv_k_k'], G['rwkv_k_a'], G['rwkv_r_k'], d_wlw, d_wla, d_wlg) = rwkv_pre_bwd(
        pr, pr_shift, *wkv_grads, dz, dg, mu, w0, a0, k_k, k_a, r_k, wlw, wla, wlg, name="rwkv_pre_bwd")
    G['rwkv_mu'] = _unpad_rwkv_cols(d_mu)
    G['rwkv_w_lora_up'], G['rwkv_a_lora_up'], G['rwkv_g_lora_up'] = d_wlw[:DECAY_LORA], d_wla[:ICLR_LORA], d_wlg[:GATE_LORA]
    dq, dko, dkn, dvo, dvn, G['attn_q_norm'], d_sinks = att_bwd(pa, dya, qg, kg, sinks, name="att_bwd")
    G['attn_sinks'] = d_sinks[:, :N_HEADS]
    dpa, G['attn_k_norm'] = att_kv_bwd(pa, dq, dko, dkn, dvo, dvn, kg, name="att_kv_bwd")
    d_w_in_r = mm(h2, dpr, ta=True, name="d_w_in_rwkv")
    d_w_in_a = mm(h2, dpa, ta=True, name="d_w_in_att")
    d_w_in_g = mm(h2, dpg, ta=True, name="d_w_in_gate")
    G['w_in'] = jnp.concatenate([_unpad_rwkv_cols(d_w_in_r), d_w_in_a, d_w_in_g], axis=1)
    mid_blocks = [_to_blocks(G[n], axis_of[n]) for n in MID_WEIGHTS]
    dh2, *mid_from_sibling = mm(dpr, w_in_r, tb=True, name="d_h2_rwkv", carry=(_SiblingPlan, mid_blocks))
    dh2 = mm(dpa, w_in_a, tb=True, res=dh2, name="d_h2_att")
    dx1, G['mix_norm'] = mm(dpg, w_in_g, tb=True, res=dh2, name="d_h2_gate", norm_bwd=(x1, nmix, dx2))
    mid_pair = [pair_sum(g, t, BF16, name="pair_sum_" + n) for g, t, n in zip(mid_blocks, mid_from_sibling, MID_WEIGHTS)]
    dx0, G['ffn1_norm'], _, _, _, mid_landed, (first_pair, first_landed) = _ffn_bwd(
        dx1, x, ffn1, n1, full['ffn1_w_gate'], full['ffn1_w_up'], full['ffn1_w_down'], "ffn1",
        carry=(_ExchangePlan, [q for _, q in mid_pair]), reduce=FIRST_WEIGHTS)

    small_shapes = [(w[n].size,) for n in SMALL] + [(1,)]

    def small_rows(parts):
        vec = jnp.concatenate([p.reshape(-1) for p in parts])
        return jnp.pad(vec, (0, SMALL_ROWS * FLAT_W - vec.shape[0])).reshape(SMALL_ROWS, FLAT_W)

    small = small_rows([G[n] for n in SMALL] + [loss[0, :1]])
    small_pair = _pair_sums(['small'], [jnp.broadcast_to(small[None], (N_CHIPS,) + small.shape)], "small")
    small_landed = exchange_chips([q for _, q in small_pair], name="exchange_small")
    names = FIRST_WEIGHTS + ['small'] + MID_WEIGHTS + LATE_WEIGHTS
    pair = first_pair + small_pair + mid_pair + late_pair
    landed = list(first_landed) + list(small_landed) + list(mid_landed) + list(late_landed)
    halves = [half_sum(own, l, name="half_sum_" + n) for (own, _), l, n in zip(pair, landed, names)]
    other_halves = to_sibling(halves, False, name="halves_to_sibling")

    def local(prefix, n):
        if n != 'small':
            return natural(n, A[prefix + n])
        return small_rows([A[prefix + s] for s in SMALL] + [jnp.zeros((1,), F32)])[None]

    result = {}
    for n, mine, theirs in zip(names, halves, other_halves):
        outs4 = adamw(local('', n), local('m_', n), local('v_', n), mine, theirs, name="adamw_" + n)
        for kind, o in zip(('grad_', 'delta_', 'new_m_', 'new_v_'), outs4):
            if n != 'small':
                result[kind + n] = natural(n, o)
            else:
                for s, part in zip(SMALL + ['loss'], _unpack_vec(o.reshape(-1), small_shapes)):
                    result[kind + s] = part.reshape(A[s].shape) if s != 'loss' else part.reshape(())
    outs = [result['grad_loss'], dx0[None]]
    for kind in ('grad_', 'delta_', 'new_m_', 'new_v_'):
        outs += [result[kind + n] for n in WEIGHT_NAMES]
    return tuple(outs)


def _unpack_vec(vec, shapes):
    out, off = [], 0
    for (n,) in shapes:
        out.append(vec[off:off + n])
        off += n
    return out


def kernel(x, ffn1_norm, ffn1_w_gate, ffn1_w_up, ffn1_w_down, mix_norm, w_in, rwkv_mu, rwkv_w0, rwkv_w_lora_up, rwkv_a0, rwkv_a_lora_up, rwkv_g_lora_up, rwkv_k_k, rwkv_k_a, rwkv_r_k, rwkv_ln_w, rwkv_ln_b, attn_q_norm, attn_k_norm, attn_sinks, w_branch_rwkv, w_branch_attn, w_out, ffn2_norm, ffn2_w_gate, ffn2_w_up, ffn2_w_down, final_norm, loss_target, m_ffn1_norm, m_ffn1_w_gate, m_ffn1_w_up, m_ffn1_w_down, m_mix_norm, m_w_in, m_rwkv_mu, m_rwkv_w0, m_rwkv_w_lora_up, m_rwkv_a0, m_rwkv_a_lora_up, m_rwkv_g_lora_up, m_rwkv_k_k, m_rwkv_k_a, m_rwkv_r_k, m_rwkv_ln_w, m_rwkv_ln_b, m_attn_q_norm, m_attn_k_norm, m_attn_sinks, m_w_branch_rwkv, m_w_branch_attn, m_w_out, m_ffn2_norm, m_ffn2_w_gate, m_ffn2_w_up, m_ffn2_w_down, m_final_norm, v_ffn1_norm, v_ffn1_w_gate, v_ffn1_w_up, v_ffn1_w_down, v_mix_norm, v_w_in, v_rwkv_mu, v_rwkv_w0, v_rwkv_w_lora_up, v_rwkv_a0, v_rwkv_a_lora_up, v_rwkv_g_lora_up, v_rwkv_k_k, v_rwkv_k_a, v_rwkv_r_k, v_rwkv_ln_w, v_rwkv_ln_b, v_attn_q_norm, v_attn_k_norm, v_attn_sinks, v_w_branch_rwkv, v_w_branch_attn, v_w_out, v_ffn2_norm, v_ffn2_w_gate, v_ffn2_w_up, v_ffn2_w_down, v_final_norm):
    return _step(dict(locals()))
```

```python
import functools

import jax
import jax.numpy as jnp
from jax import lax
from jax.experimental import pallas as pl
from jax.experimental.pallas import tpu as pltpu

F32 = jnp.float32
BF16 = jnp.bfloat16

D_MODEL = 1024
D_FF = 2816
HEAD_DIM = 64
N_HEADS = 8
RW = 512
KVW = 128
ATT_GROUP = 4
WINDOW = 128
BLOCK = 128
DECAY_LORA, ICLR_LORA, GATE_LORA = 32, 32, 96
RWKV_COLS = 3 * RW + DECAY_LORA + ICLR_LORA + GATE_LORA
ATT_COLS = RW + 2 * KVW
GATE_COLS = 2 * D_MODEL
RWKV_PAD = 3 * RW + 3 * 128
RMS_EPS = 1e-6
GN_EPS = 64e-5
N_CHIPS = 4
LANE = 128
FLAT_W = 1024
SMALL_ROWS = 32
NEG_BIG = -1e30

ADAM_LR, ADAM_B1, ADAM_B2, ADAM_EPS, ADAM_WD, ADAM_STEP = 0.001, 0.9, 0.999, 1e-08, 0.01, 10

VMEM_LIMIT = 56 * 1024 * 1024

WEIGHT_NAMES = ['ffn1_norm', 'ffn1_w_gate', 'ffn1_w_up', 'ffn1_w_down', 'mix_norm', 'w_in', 'rwkv_mu', 'rwkv_w0',
                'rwkv_w_lora_up', 'rwkv_a0', 'rwkv_a_lora_up', 'rwkv_g_lora_up', 'rwkv_k_k', 'rwkv_k_a', 'rwkv_r_k',
                'rwkv_ln_w', 'rwkv_ln_b', 'attn_q_norm', 'attn_k_norm', 'attn_sinks', 'w_branch_rwkv',
                'w_branch_attn', 'w_out', 'ffn2_norm', 'ffn2_w_gate', 'ffn2_w_up', 'ffn2_w_down', 'final_norm']
BIG = [('ffn1_w_gate', 1), ('ffn1_w_up', 1), ('ffn1_w_down', 0), ('w_in', 1), ('rwkv_w_lora_up', 1),
       ('rwkv_a_lora_up', 1), ('rwkv_g_lora_up', 1), ('w_branch_rwkv', 1), ('w_branch_attn', 1), ('w_out', 0),
       ('ffn2_w_gate', 1), ('ffn2_w_up', 1), ('ffn2_w_down', 0)]
SMALL = ['ffn1_norm', 'mix_norm', 'rwkv_mu', 'rwkv_w0', 'rwkv_a0', 'rwkv_k_k', 'rwkv_k_a', 'rwkv_r_k', 'rwkv_ln_w',
         'rwkv_ln_b', 'attn_q_norm', 'attn_k_norm', 'attn_sinks', 'ffn2_norm', 'final_norm']


def _pcall(body, **kw):
    return pl.pallas_call(body, **kw)


def _params(sem=None, **kw):
    if sem is not None:
        kw['dimension_semantics'] = sem
    return pltpu.CompilerParams(vmem_limit_bytes=VMEM_LIMIT, **kw)


def _tile(n, cap, mult):
    best = None
    for t in range(mult, min(n, cap) + 1, mult):
        if n % t == 0:
            best = t
    return best or n


def _sigmoid(z):
    return 1.0 / (1.0 + jnp.exp(-z))


def _softplus(z):
    return jnp.maximum(z, 0.0) + jnp.log(1.0 + jnp.exp(-jnp.abs(z)))


def _bdot(a, b, dims=(((1,), (0,)), ((), ()))):
    return lax.dot_general(a.astype(BF16), b.astype(BF16), dims, preferred_element_type=F32)


_NT = (((1,), (1,)), ((), ()))
_TN = (((0,), (0,)), ((), ()))


def _segsum(x, bd):
    hi = x.astype(BF16)
    lo = (x - hi.astype(F32)).astype(BF16)
    dot = functools.partial(lax.dot_general, dimension_numbers=(((1,), (0,)), ((), ())), preferred_element_type=F32)
    return dot(hi, bd) + dot(lo, bd)


_LORA_EDGES = (3 * RW, 3 * RW + DECAY_LORA, 3 * RW + DECAY_LORA + ICLR_LORA, RWKV_COLS)


def _pad_rwkv_cols(x):
    parts = [x[..., :3 * RW]]
    for lo, hi in zip(_LORA_EDGES[:-1], _LORA_EDGES[1:]):
        parts.append(jnp.pad(x[..., lo:hi], [(0, 0)] * (x.ndim - 1) + [(0, 128 - (hi - lo))]))
    return jnp.concatenate(parts, axis=-1)


def _unpad_rwkv_cols(x):
    parts = [x[..., :3 * RW]]
    for j, (lo, hi) in enumerate(zip(_LORA_EDGES[:-1], _LORA_EDGES[1:])):
        parts.append(x[..., 3 * RW + 128 * j:3 * RW + 128 * j + (hi - lo)])
    return jnp.concatenate(parts, axis=-1)


def _pad_rows(x, rows):
    return jnp.pad(x, [(0, rows - x.shape[0])] + [(0, 0)] * (x.ndim - 1))


def mm(a, b, *, name, ta=False, tb=False, scale=None, res=None, out_dtype=F32, carry=None, norm_bwd=None):
    M, K = (a.shape[1], a.shape[0]) if ta else a.shape
    N = b.shape[0] if tb else b.shape[1]
    assert (b.shape[1] if tb else b.shape[0]) == K
    tm, tn, tk = _tile(M, 1408 if ta else 512, 128), _tile(N, 1408, 128), _tile(K, 1408, 128)
    nk = K // tk
    grid = (M // tm, N // tn, nk)
    dims = (((0 if ta else 1,), (1 if tb else 0,)), ((), ()))
    plan_cls, carried = carry if carry else (None, ())
    nn = 3 if norm_bwd else 0
    nd = 1 if norm_bwd else 0
    nc, nin = len(carried), 2 + (res is not None) + nn
    assert not norm_bwd or (tn == N and out_dtype == F32)

    def body(*refs):
        a_ref, b_ref = refs[:2]
        r_ref = refs[2] if res is not None else None
        o_ref, acc_ref = refs[nin + nc], refs[nin + 2 * nc + 1 + nd]
        row_tile, k = pl.program_id(0), pl.program_id(2)
        if nc:
            plan = plan_cls(refs[nin:nin + nc], refs[nin + nc + 1 + nd:nin + 2 * nc + 1 + nd], refs[nin + 2 * nc + 2 + nd:])
            at = lambda which: functools.reduce(jnp.logical_and, [pl.program_id(d) == (0 if which == 0 else grid[d] - 1)
                                                                 for d in range(3)])
            pl.when(at(0))(plan.start)
        part = _bdot(a_ref[...], b_ref[...], dims)

        @pl.when(k == 0)
        def _():
            acc_ref[...] = part

        @pl.when(k > 0)
        def _():
            acc_ref[...] += part

        @pl.when(k == nk - 1)
        def _():
            o = acc_ref[...]
            if scale is not None:
                o = o * scale
            if r_ref is not None:
                o = o + r_ref[...].astype(F32)
            if norm_bwd:
                x_ref, g_ref, dres_ref = refs[nin - 3:nin]
                xv = x_ref[...]
                r = lax.rsqrt(jnp.mean(xv * xv, axis=-1, keepdims=True) + RMS_EPS)
                xh = xv * r
                dxh = o * g_ref[...]
                _acc_rows(refs[nin + nc + 1], jnp.sum(o * xh, axis=0, keepdims=True), row_tile)
                o = dres_ref[...] + r * (dxh - xh * jnp.mean(dxh * xh, axis=-1, keepdims=True))
            o_ref[...] = o.astype(out_dtype)

        if nc:
            pl.when(at(1))(plan.finish)

    a_spec = pl.BlockSpec((tk, tm), lambda i, j, k: (k, i)) if ta else pl.BlockSpec((tm, tk), lambda i, j, k: (i, k))
    b_spec = pl.BlockSpec((tn, tk), lambda i, j, k: (j, k)) if tb else pl.BlockSpec((tk, tn), lambda i, j, k: (k, j))
    o_spec = pl.BlockSpec((tm, tn), lambda i, j, k: (i, j))
    g_spec = pl.BlockSpec((1, N), lambda i, j, k: (0, 0))
    in_specs = [a_spec, b_spec] + ([o_spec] if res is not None else []) + ([o_spec, g_spec, o_spec] if norm_bwd else [])
    args = (a, b) + ((res,) if res is not None else ()) + (tuple(norm_bwd) if norm_bwd else ())
    out_shape = [jax.ShapeDtypeStruct((M, N), out_dtype)] + [jax.ShapeDtypeStruct((1, N), F32)] * nd
    out_specs = [o_spec] + [g_spec] * nd
    if not nc and not nd:
        return _pcall(
            body, name=name, grid=grid, in_specs=in_specs, out_specs=o_spec, out_shape=out_shape[0],
            scratch_shapes=[pltpu.VMEM((tm, tn), F32)], compiler_params=_params(("parallel", "parallel", "arbitrary")),
        )(*args)
    return _pcall(
        body, name=name, grid=grid, in_specs=in_specs + [_HBM] * nc, out_specs=out_specs + [_HBM] * nc,
        out_shape=out_shape + (plan_cls.out_shapes(carried) if nc else []),
        scratch_shapes=[pltpu.VMEM((tm, tn), F32)] + (plan_cls.sems(nc) if nc else []),
        compiler_params=_params(("arbitrary", "arbitrary", "arbitrary")),
    )(*args, *carried)


def mm_fused(a, bs, finish, out_dtypes, *, name, tb=False, extras=(), carry=None):
    M, K = a.shape
    N = bs[0].shape[0] if tb else bs[0].shape[1]
    tm, tn = _tile(M, 512, 128), _tile(N, 1408, 128)
    grid = (M // tm, N // tn)
    dims = (((1,), (1 if tb else 0,)), ((), ()))
    plan_cls, carried = carry if carry else (None, ())
    nc, nb, nx, no = len(carried), len(bs), len(extras), len(out_dtypes)
    nin = 1 + nb + nx

    def body(*refs):
        a_ref, b_refs, x_refs = refs[0], refs[1:1 + nb], refs[1 + nb:nin]
        o_refs = refs[nin + nc:nin + nc + no]
        if nc:
            plan = plan_cls(refs[nin:nin + nc], refs[nin + nc + no:nin + 2 * nc + no], refs[nin + 2 * nc + no:])
            at = lambda which: jnp.logical_and(*[pl.program_id(d) == (0 if which == 0 else grid[d] - 1) for d in range(2)])
            pl.when(at(0))(plan.start)
        av = a_ref[...]
        outs = finish([_bdot(av, b_ref[...], dims) for b_ref in b_refs], [x_ref[...] for x_ref in x_refs])
        for o_ref, o in zip(o_refs, outs):
            o_ref[...] = o.astype(o_ref.dtype)
        if nc:
            pl.when(at(1))(plan.finish)

    a_spec = pl.BlockSpec((tm, K), lambda i, j: (i, 0))
    b_spec = pl.BlockSpec((tn, K), lambda i, j: (j, 0)) if tb else pl.BlockSpec((K, tn), lambda i, j: (0, j))
    o_spec = pl.BlockSpec((tm, tn), lambda i, j: (i, j))
    out_shape = [jax.ShapeDtypeStruct((M, N), d) for d in out_dtypes]
    if not nc:
        return _pcall(body, name=name, grid=grid, in_specs=[a_spec] + [b_spec] * nb + [o_spec] * nx, out_specs=[o_spec] * no,
                      out_shape=out_shape, compiler_params=_params(("parallel", "parallel")))(a, *bs, *extras)
    return _pcall(body, name=name, grid=grid, in_specs=[a_spec] + [b_spec] * nb + [o_spec] * nx + [_HBM] * nc,
                  out_specs=[o_spec] * no + [_HBM] * nc, out_shape=out_shape + plan_cls.out_shapes(carried),
                  scratch_shapes=plan_cls.sems(nc), compiler_params=_params(("arbitrary", "arbitrary")))(a, *bs, *extras, *carried)


def _swiglu(products, _):
    g, u = products
    return g, u, g * _sigmoid(g) * u


def _swiglu_bwd(products, extras):
    (da,), (gate, up) = products, extras
    gv = gate.astype(F32)
    s = _sigmoid(gv)
    return da * 0.5 * up.astype(F32) * s * (1.0 + gv * (1.0 - s)), da * 0.5 * gv * s


def _row_spec(tr, c):
    return pl.BlockSpec((tr, c), lambda i: (i, 0))


def _full_spec(shape):
    return pl.BlockSpec(shape, lambda i: (0,) * len(shape))


def _acc_rows(ref, val, i):
    @pl.when(i == 0)
    def _():
        ref[...] = val

    @pl.when(i > 0)
    def _():
        ref[...] += val


def rms_fwd(x, g, *, name):
    T, D = x.shape
    tr = _tile(T, 512, 8)

    def body(x_ref, g_ref, h_ref):
        xv = x_ref[...]
        r = lax.rsqrt(jnp.mean(xv * xv, axis=-1, keepdims=True) + RMS_EPS)
        h_ref[...] = (xv * r * g_ref[...]).astype(BF16)

    return _pcall(body, name=name, grid=(T // tr,), in_specs=[_row_spec(tr, D), _full_spec((1, D))],
                  out_specs=_row_spec(tr, D), out_shape=jax.ShapeDtypeStruct((T, D), BF16),
                  compiler_params=_params(("parallel",)))(x, g)


def final_loss(x, tgt, g, *, name):
    T, D = x.shape
    tr = _tile(T, 256, 8)

    def body(x_ref, t_ref, g_ref, dx_ref, dg_ref, loss_ref):
        i = pl.program_id(0)
        xv = x_ref[...]
        r = lax.rsqrt(jnp.mean(xv * xv, axis=-1, keepdims=True) + RMS_EPS)
        xh = xv * r
        e = xh * g_ref[...] - t_ref[...]
        part = 0.5 * jnp.sum(jnp.mean(e * e, axis=-1, keepdims=True), axis=0, keepdims=True)
        dy = e * (1.0 / D)
        dxh = dy * g_ref[...]
        dx_ref[...] = r * (dxh - xh * jnp.mean(dxh * xh, axis=-1, keepdims=True))
        _acc_rows(dg_ref, jnp.sum(dy * xh, axis=0, keepdims=True), i)
        _acc_rows(loss_ref, jnp.broadcast_to(part, (1, LANE)), i)

    return _pcall(body, name=name, grid=(T // tr,),
                  in_specs=[_row_spec(tr, D), _row_spec(tr, D), _full_spec((1, D))],
                  out_specs=[_row_spec(tr, D), _full_spec((1, D)), _full_spec((1, LANE))],
                  out_shape=[jax.ShapeDtypeStruct((T, D), F32), jax.ShapeDtypeStruct((1, D), F32),
                             jax.ShapeDtypeStruct((1, LANE), F32)],
                  compiler_params=_params(("arbitrary",)))(x, tgt, g)


def merge_fwd(br, ba, pg, *, name):
    T, D = br.shape
    tr = _tile(T, 256, 8)

    def body(br_ref, ba_ref, pg_ref, o_ref):
        pgv = pg_ref[...]
        o_ref[...] = (_sigmoid(pgv[:, :D]) * br_ref[...] + _sigmoid(pgv[:, D:]) * ba_ref[...]).astype(BF16)

    return _pcall(body, name=name, grid=(T // tr,), in_specs=[_row_spec(tr, D), _row_spec(tr, D), _row_spec(tr, 2 * D)],
                  out_specs=_row_spec(tr, D), out_shape=jax.ShapeDtypeStruct((T, D), BF16),
                  compiler_params=_params(("parallel",)))(br, ba, pg)


def merge_bwd(dm, br, ba, pg, *, name):
    T, D = br.shape
    tr = _tile(T, 256, 8)

    def body(dm_ref, br_ref, ba_ref, pg_ref, dbr_ref, dba_ref, dpg_ref):
        pgv, dmv = pg_ref[...], dm_ref[...]
        sr, sa = _sigmoid(pgv[:, :D]), _sigmoid(pgv[:, D:])
        dbr_ref[...] = (dmv * sr).astype(BF16)
        dba_ref[...] = (dmv * sa).astype(BF16)
        dpg_ref[:, :D] = dmv * br_ref[...] * sr * (1.0 - sr)
        dpg_ref[:, D:] = dmv * ba_ref[...] * sa * (1.0 - sa)

    return _pcall(body, name=name, grid=(T // tr,),
                  in_specs=[_row_spec(tr, D), _row_spec(tr, D), _row_spec(tr, D), _row_spec(tr, 2 * D)],
                  out_specs=[_row_spec(tr, D), _row_spec(tr, D), _row_spec(tr, 2 * D)],
                  out_shape=[jax.ShapeDtypeStruct((T, D), BF16), jax.ShapeDtypeStruct((T, D), BF16),
                             jax.ShapeDtypeStruct((T, 2 * D), F32)],
                  compiler_params=_params(("parallel",)))(dm, br, ba, pg)


def _rwkv_mix(p, prev, mu, w0, a0, k_k, k_a, wlw, wla, wlg, bd):
    pp = p + (prev - p) * mu
    r, k, v = pp[:, 0:RW], pp[:, RW:2 * RW], pp[:, 2 * RW:3 * RW]
    xw, xa, xg = pp[:, 3 * RW:3 * RW + 128], pp[:, 3 * RW + 128:3 * RW + 256], pp[:, 3 * RW + 256:3 * RW + 384]
    th = jnp.tanh(xw)
    z = -(w0 + _bdot(th, wlw))
    e = jnp.exp(-_softplus(z) - 0.5)
    decay = jnp.exp(-e)
    a = _sigmoid(a0 + _bdot(xa, wla))
    sg = _sigmoid(xg)
    kkr = k * k_k
    n = jnp.sqrt(_segsum(kkr * kkr, bd))
    kk = kkr / jnp.maximum(n, 1e-12)
    k2 = k * (1.0 + (a - 1.0) * k_a)
    return dict(r=r, k=k, v=v, xa=xa, th=th, z=z, e=e, decay=decay, a=a, sg=sg, n=n, kk=kk, k2=k2)


def _seg_matrix(n, shift):
    r = lax.shift_right_logical(lax.broadcasted_iota(jnp.int32, (n, n), 0), shift)
    c = lax.shift_right_logical(lax.broadcasted_iota(jnp.int32, (n, n), 1), shift)
    return jnp.where(r == c, 1.0, 0.0).astype(BF16)


def rwkv_pre_fwd(p, pshift, mu, w0, a0, k_k, k_a, wlw, wla, wlg, *, name):
    T = p.shape[0]
    tr = _tile(T, 256, 8)

    def body(p_ref, ps_ref, mu_ref, w0_ref, a0_ref, kk_ref, ka_ref, wlw_ref, wla_ref, wlg_ref,
             r_ref, w_ref, k_ref, v_ref, a_ref, b_ref, g_ref):
        pv, prev = p_ref[...], ps_ref[...]
        m = _rwkv_mix(pv, prev, mu_ref[...], w0_ref[...], a0_ref[...], kk_ref[...], ka_ref[...],
                      wlw_ref[...], wla_ref[...], wlg_ref[...], _seg_matrix(RW, 6))
        r_ref[...] = m['r']
        w_ref[...] = m['decay']
        k_ref[...] = m['k2']
        v_ref[...] = m['v']
        a_ref[...] = -m['kk']
        b_ref[...] = m['kk'] * m['a']
        g_ref[...] = m['sg']

    vec = _row_spec(tr, RW)
    return _pcall(
        body, name=name, grid=(T // tr,),
        in_specs=[_row_spec(tr, RWKV_PAD), _row_spec(tr, RWKV_PAD), _full_spec((1, RWKV_PAD))] + [_full_spec((1, RW))] * 4
        + [_full_spec((128, RW))] * 3,
        out_specs=[vec] * 6 + [_row_spec(tr, 128)],
        out_shape=[jax.ShapeDtypeStruct((T, RW), F32)] * 6 + [jax.ShapeDtypeStruct((T, 128), F32)],
        compiler_params=_params(("parallel",)),
    )(p, pshift, mu, w0, a0, k_k, k_a, wlw, wla, wlg)


def _group_norm(y, bd):
    mean = _segsum(y, bd) * (1.0 / HEAD_DIM)
    yc = y - mean
    rstd = lax.rsqrt(_segsum(yc * yc, bd) * (1.0 / HEAD_DIM) + GN_EPS)
    return yc * rstd, rstd


def rwkv_post_fwd(y, r, k2, v, sg, wlg, ln_w, ln_b, r_k, *, name):
    T = y.shape[0]
    tr = _tile(T, 256, 8)

    def body(y_ref, r_ref, k_ref, v_ref, sg_ref, wlg_ref, lw_ref, lb_ref, rk_ref, o_ref):
        bd = _seg_matrix(RW, 6)
        yn, _ = _group_norm(y_ref[...], bd)
        s = _segsum(r_ref[...] * k_ref[...] * rk_ref[...], bd)
        g = _bdot(sg_ref[...], wlg_ref[...])
        o_ref[...] = ((yn * lw_ref[...] + lb_ref[...] + s * v_ref[...]) * g).astype(BF16)

    vec = _row_spec(tr, RW)
    return _pcall(body, name=name, grid=(T // tr,),
                  in_specs=[vec] * 4 + [_row_spec(tr, 128), _full_spec((128, RW))] + [_full_spec((1, RW))] * 3, out_specs=vec,
                  out_shape=jax.ShapeDtypeStruct((T, RW), BF16), compiler_params=_params(("parallel",)))(
                      y, r, k2, v, sg, wlg, ln_w, ln_b, r_k)


def rwkv_post_bwd(dyr, y, r, k2, v, sg, wlg, ln_w, ln_b, r_k, *, name, carry=None):
    T = y.shape[0]
    tr = _tile(T, 256, 8)
    nt = T // tr
    plan_cls, carried = carry if carry else (None, ())
    nc = len(carried)

    def body(*refs):
        dyr_ref, y_ref, r_ref, k_ref, v_ref, sg_ref, wlg_ref, lw_ref, lb_ref, rk_ref = refs[:10]
        dy_ref, dz_ref, dg_ref, dlw_ref, dlb_ref = refs[10 + nc:15 + nc]
        i = pl.program_id(0)
        if nc:
            plan = plan_cls(refs[10:10 + nc], refs[15 + nc:15 + 2 * nc], refs[15 + 2 * nc:])
            pl.when(i == 0)(plan.start)
        bd = _seg_matrix(RW, 6)
        yn, rstd = _group_norm(y_ref[...], bd)
        s = _segsum(r_ref[...] * k_ref[...] * rk_ref[...], bd)
        dyrv = dyr_ref[...]
        dg_ref[...] = dyrv * (yn * lw_ref[...] + lb_ref[...] + s * v_ref[...])
        dz = dyrv * _bdot(sg_ref[...], wlg_ref[...])
        dz_ref[...] = dz
        dyn = dz * lw_ref[...]
        inv = 1.0 / HEAD_DIM
        dy_ref[...] = rstd * (dyn - _segsum(dyn, bd) * inv - yn * (_segsum(dyn * yn, bd) * inv))
        _acc_rows(dlw_ref, jnp.sum(dz * yn, axis=0, keepdims=True), i)
        _acc_rows(dlb_ref, jnp.sum(dz, axis=0, keepdims=True), i)
        if nc:
            pl.when(i == nt - 1)(plan.finish)

    vec = _row_spec(tr, RW)
    one = _full_spec((1, RW))
    return _pcall(body, name=name, grid=(nt,),
                  in_specs=[vec] * 5 + [_row_spec(tr, 128), _full_spec((128, RW))] + [one] * 3 + [_HBM] * nc,
                  out_specs=[vec] * 3 + [one] * 2 + [_HBM] * nc,
                  out_shape=[jax.ShapeDtypeStruct((T, RW), F32)] * 3 + [jax.ShapeDtypeStruct((1, RW), F32)] * 2
                  + (plan_cls.out_shapes(carried) if nc else []),
                  scratch_shapes=plan_cls.sems(nc) if nc else [],
                  compiler_params=_params(("arbitrary",)))(dyr, y, r, k2, v, sg, wlg, ln_w, ln_b, r_k, *carried)


def rwkv_pre_bwd(p, pshift, dr_w, dw_w, dk_w, dv_w, da_w, db_w, dz, dg, mu, w0, a0, k_k, k_a, r_k, wlw, wla, wlg, *, name):
    T = p.shape[0]
    tr = _tile(T, 256, 8)
    n = T // tr

    def body(p_ref, ps_ref, dr_ref, dw_ref, dk_ref, dv_ref, da_ref, db_ref, dz_ref, dg_ref,
             mu_ref, w0_ref, a0_ref, kk_ref, ka_ref, rk_ref, wlw_ref, wla_ref, wlg_ref,
             dp_ref, dmu_ref, dw0_ref, da0_ref, dkk_ref, dka_ref, drk_ref, dwlw_ref, dwla_ref, dwlg_ref,
             carry, dpp, acc_w, acc_a, acc_g):
        i = pl.program_id(0)

        @pl.when(i == 0)
        def _():
            carry[...] = jnp.zeros_like(carry)

        pv, prev, mu = p_ref[...], ps_ref[...], mu_ref[...]
        bd = _seg_matrix(RW, 6)
        k_k, k_a, r_k = kk_ref[...], ka_ref[...], rk_ref[...]
        m = _rwkv_mix(pv, prev, mu, w0_ref[...], a0_ref[...], k_k, k_a, wlw_ref[...], wla_ref[...], wlg_ref[...], bd)
        r, k, v, a, kk, k2 = m['r'], m['k'], m['v'], m['a'], m['kk'], m['k2']
        dzv, dgv = dz_ref[...], dg_ref[...]
        s = _segsum(r * k2 * r_k, bd)
        ds = _segsum(dzv * v, bd)
        dr = dr_ref[...] + ds * k2 * r_k
        dk2 = dk_ref[...] + ds * r * r_k
        dv = dv_ref[...] + dzv * s
        dbv = db_ref[...]
        dkk = dbv * a - da_ref[...]
        da = dbv * kk + dk2 * k * k_a
        dk = dk2 * (1.0 + (a - 1.0) * k_a)
        nmax = jnp.maximum(m['n'], 1e-12)
        dkkr = jnp.where(m['n'] > 1e-12, dkk - kk * _segsum(dkk * kk, bd), dkk) / nmax
        dk = dk + dkkr * k_k
        dapre = da * a * (1.0 - a)
        dwpre = dw_ref[...] * m['decay'] * (-m['e']) * _sigmoid(m['z'])
        dth = _bdot(dwpre, wlw_ref[...], _NT)
        dxa = _bdot(dapre, wla_ref[...], _NT)
        dsg = _bdot(dgv, wlg_ref[...], _NT)
        dpp[:, 0:RW] = dr
        dpp[:, RW:2 * RW] = dk
        dpp[:, 2 * RW:3 * RW] = dv
        dpp[:, 3 * RW:3 * RW + 128] = dth * (1.0 - m['th'] * m['th'])
        dpp[:, 3 * RW + 128:3 * RW + 256] = dxa
        dpp[:, 3 * RW + 256:3 * RW + 384] = dsg * m['sg'] * (1.0 - m['sg'])
        d = dpp[...]
        zed = d * mu
        last = lax.broadcasted_iota(jnp.int32, pv.shape, 0) == tr - 1
        dp_ref[...] = d * (1.0 - mu) + jnp.where(last, carry[0:1, :], pltpu.roll(zed, tr - 1, 0))
        carry[...] = zed[0:8, :]

        def colsum(x):
            return jnp.sum(x, axis=0, keepdims=True)

        _acc_rows(dmu_ref, colsum(d * (prev - pv)), i)
        _acc_rows(dw0_ref, colsum(dwpre), i)
        _acc_rows(da0_ref, colsum(dapre), i)
        _acc_rows(dkk_ref, colsum(dkkr * k), i)
        _acc_rows(dka_ref, colsum(dk2 * k * (a - 1.0)), i)
        _acc_rows(drk_ref, colsum(ds * r * k2), i)
        _acc_rows(acc_w, _bdot(m['th'], dwpre, _TN), i)
        _acc_rows(acc_a, _bdot(m['xa'], dapre, _TN), i)
        _acc_rows(acc_g, _bdot(m['sg'], dgv, _TN), i)

        @pl.when(i == n - 1)
        def _():
            dwlw_ref[...] = acc_w[...]
            dwla_ref[...] = acc_a[...]
            dwlg_ref[...] = acc_g[...]

    rev = lambda c: pl.BlockSpec((tr, c), lambda i: (n - 1 - i, 0))
    one, lora = _full_spec((1, RW)), _full_spec((128, RW))
    return _pcall(
        body, name=name, grid=(n,),
        in_specs=[rev(RWKV_PAD), rev(RWKV_PAD)] + [rev(RW)] * 8 + [_full_spec((1, RWKV_PAD))] + [one] * 5 + [lora] * 3,
        out_specs=[rev(RWKV_PAD), _full_spec((1, RWKV_PAD))] + [one] * 5 + [lora] * 3,
        out_shape=[jax.ShapeDtypeStruct((T, RWKV_PAD), F32), jax.ShapeDtypeStruct((1, RWKV_PAD), F32)]
        + [jax.ShapeDtypeStruct((1, RW), F32)] * 5 + [jax.ShapeDtypeStruct((128, RW), F32)] * 3,
        scratch_shapes=[pltpu.VMEM((8, RWKV_PAD), F32), pltpu.VMEM((tr, RWKV_PAD), F32)] + [pltpu.VMEM((128, RW), F32)] * 3,
        compiler_params=_params(("arbitrary",)),
    )(p, pshift, dr_w, dw_w, dk_w, dv_w, da_w, db_w, dz, dg, mu, w0, a0, k_k, k_a, r_k, wlw, wla, wlg)


def _qk_norm(x, g, bd):
    r = lax.rsqrt(_segsum(x * x, bd) * (1.0 / HEAD_DIM) + RMS_EPS)
    return x * r * g, r


def _att_mask(i):
    qi = lax.broadcasted_iota(jnp.int32, (BLOCK, 2 * BLOCK), 0)
    kj = lax.broadcasted_iota(jnp.int32, (BLOCK, 2 * BLOCK), 1)
    band = (kj <= qi + BLOCK) & (kj > qi + BLOCK - WINDOW)
    return band & ((kj >= BLOCK) | (i > 0))


_HQK = (((2,), (2,)), ((0,), (0,)))
_HPV = (((2,), (1,)), ((0,), (0,)))
_HTN = (((1,), (1,)), ((0,), (0,)))


def _heads(x, n):
    return jnp.stack([x[:, h * HEAD_DIM:(h + 1) * HEAD_DIM] for h in range(n)])


def _unheads(x3):
    return jnp.concatenate([x3[h] for h in range(x3.shape[0])], axis=1)


def _kv_heads(x):
    x2 = _heads(x, KVW // HEAD_DIM)
    return jnp.concatenate([x2[g:g + 1] for g in range(KVW // HEAD_DIM) for _ in range(ATT_GROUP)], axis=0)


def _sinks3(sk):
    return jnp.stack([sk[0:1, h:h + 1] for h in range(N_HEADS)])


def _att_probs(q3, k3, mask, sink):
    s = _bdot(q3, k3, _HQK) * (HEAD_DIM ** -0.5)
    s = jnp.where(mask[None], s, NEG_BIG)
    m = jnp.maximum(jnp.max(s, axis=-1, keepdims=True), sink)
    pexp = jnp.exp(s - m)
    psink = jnp.exp(sink - m)
    inv = 1.0 / (jnp.sum(pexp, axis=-1, keepdims=True) + psink)
    return pexp * inv, psink * inv


ATT_SUB = 2


def _att_blocks(sub):
    cur = pl.BlockSpec((sub * BLOCK, ATT_COLS), lambda i: (i, 0))
    prev = pl.BlockSpec((BLOCK, ATT_COLS), lambda i: (jnp.maximum(i * sub - 1, 0), 0))
    return cur, prev


def _att_sub(cur_all, prev_first, j):
    cur = cur_all[j * BLOCK:(j + 1) * BLOCK]
    return cur, (prev_first if j == 0 else cur_all[(j - 1) * BLOCK:j * BLOCK])


def _att_qkv(cur, prev, qn_g, kn_g):
    bq, bk = _seg_matrix(RW, 6), _seg_matrix(KVW, 6)
    qn, rq = _qk_norm(cur[:, 0:RW], qn_g, bq)
    kcur, rkc = _qk_norm(cur[:, RW:RW + KVW], kn_g, bk)
    kprev, _ = _qk_norm(prev[:, RW:RW + KVW], kn_g, bk)
    kc = jnp.concatenate([kprev, kcur], axis=0)
    vc = jnp.concatenate([prev[:, RW + KVW:], cur[:, RW + KVW:]], axis=0)
    return qn, rq, kc, vc, rkc


def att_fwd(pa, qn_g, kn_g, sinks, *, name):
    T = pa.shape[0]
    sub = ATT_SUB if (T // BLOCK) % ATT_SUB == 0 else 1
    n = T // (sub * BLOCK)

    def body(cur_ref, prev_ref, qg_ref, kg_ref, sk_ref, o_ref):
        i = pl.program_id(0)
        cur_all, prev_first = cur_ref[...], prev_ref[...]
        for j in range(sub):
            cur, prev = _att_sub(cur_all, prev_first, j)
            qn, _, kc, vc, _ = _att_qkv(cur, prev, qg_ref[...], kg_ref[...])
            probs, _ = _att_probs(_heads(qn, N_HEADS), _kv_heads(kc), _att_mask(i * sub + j), _sinks3(sk_ref[...]))
            o_ref[j * BLOCK:(j + 1) * BLOCK, :] = _unheads(_bdot(probs, _kv_heads(vc), _HPV))

    cur, prev = _att_blocks(sub)
    return _pcall(body, name=name, grid=(n,),
                  in_specs=[cur, prev, _full_spec((1, RW)), _full_spec((1, KVW)), _full_spec((1, LANE))],
                  out_specs=pl.BlockSpec((sub * BLOCK, RW), lambda i: (i, 0)), out_shape=jax.ShapeDtypeStruct((T, RW), F32),
                  compiler_params=_params(("parallel",)))(pa, pa, qn_g, kn_g, sinks)


def att_bwd(pa, do, qn_g, kn_g, sinks, *, name):
    T = pa.shape[0]
    sub = ATT_SUB if (T // BLOCK) % ATT_SUB == 0 else 1
    n = T // (sub * BLOCK)

    def one_block(cur, prev, do, blk, qg_ref, kg_ref, sk_ref, rows, dq_ref, dko_ref, dkn_ref, dvo_ref, dvn_ref):
        qn, rq, kc, vc, _ = _att_qkv(cur, prev, qg_ref[...], kg_ref[...])
        q3, k3, v3, do3 = _heads(qn, N_HEADS), _kv_heads(kc), _kv_heads(vc), _heads(do, N_HEADS)
        probs, psink = _att_probs(q3, k3, _att_mask(blk), _sinks3(sk_ref[...]))
        dprobs = _bdot(do3, v3, _HQK)
        delta = jnp.sum(probs * dprobs, axis=-1, keepdims=True)
        ds = probs * (dprobs - delta) * (HEAD_DIM ** -0.5)
        dsink3 = -jnp.sum(psink * delta, axis=1, keepdims=True)
        lane = lax.broadcasted_iota(jnp.int32, (1, LANE), 1)
        dsink = jnp.zeros((1, LANE), F32)
        for h in range(N_HEADS):
            dsink = dsink + jnp.where(lane == h, dsink3[h], 0.0)
        dqn = _unheads(_bdot(ds, k3, _HPV))

        def per_kv_head(x3):
            groups = [sum(x3[g * ATT_GROUP + j] for j in range(ATT_GROUP)) for g in range(KVW // HEAD_DIM)]
            return jnp.concatenate(groups, axis=1)

        dk, dv = per_kv_head(_bdot(ds, q3, _HTN)), per_kv_head(_bdot(probs, do3, _HTN))
        dkn_ref[rows, :], dko_ref[rows, :] = dk[0:BLOCK], dk[BLOCK:]
        dvn_ref[rows, :], dvo_ref[rows, :] = dv[0:BLOCK], dv[BLOCK:]
        qhat = cur[:, 0:RW] * rq
        dqh = dqn * qg_ref[...]
        dq_ref[rows, :] = rq * (dqh - qhat * (_segsum(dqh * qhat, _seg_matrix(RW, 6)) * (1.0 / HEAD_DIM)))
        prod = dqn * qhat
        fold = prod[:, 0:HEAD_DIM]
        for h in range(1, N_HEADS):
            fold = fold + prod[:, h * HEAD_DIM:(h + 1) * HEAD_DIM]
        return jnp.sum(fold, axis=0, keepdims=True), dsink

    def body(cur_ref, prev_ref, do_ref, qg_ref, kg_ref, sk_ref,
             dq_ref, dko_ref, dkn_ref, dvo_ref, dvn_ref, dqg_ref, dsk_ref):
        i = pl.program_id(0)
        cur_all, prev_first, do_all = cur_ref[...], prev_ref[...], do_ref[...]
        dqg, dsk = None, None
        for j in range(sub):
            cur, prev = _att_sub(cur_all, prev_first, j)
            rows = slice(j * BLOCK, (j + 1) * BLOCK)
            g, s = one_block(cur, prev, do_all[rows], i * sub + j, qg_ref, kg_ref, sk_ref, rows,
                             dq_ref, dko_ref, dkn_ref, dvo_ref, dvn_ref)
            dqg, dsk = (g, s) if dqg is None else (dqg + g, dsk + s)
        _acc_rows(dqg_ref, dqg, i)
        _acc_rows(dsk_ref, dsk, i)

    cur, prev = _att_blocks(sub)
    kvb = pl.BlockSpec((sub * BLOCK, KVW), lambda i: (i, 0))
    qb = pl.BlockSpec((sub * BLOCK, RW), lambda i: (i, 0))
    return _pcall(body, name=name, grid=(n,),
                  in_specs=[cur, prev, qb, _full_spec((1, RW)), _full_spec((1, KVW)), _full_spec((1, LANE))],
                  out_specs=[qb, kvb, kvb, kvb, kvb, _full_spec((1, HEAD_DIM)), _full_spec((1, LANE))],
                  out_shape=[jax.ShapeDtypeStruct((T, RW), F32)] + [jax.ShapeDtypeStruct((T, KVW), F32)] * 4
                  + [jax.ShapeDtypeStruct((1, HEAD_DIM), F32), jax.ShapeDtypeStruct((1, LANE), F32)],
                  compiler_params=_params(("arbitrary",)))(pa, pa, do, qn_g, kn_g, sinks)


def att_kv_bwd(pa, dq, dko, dkn, dvo, dvn, kn_g, *, name):
    T = pa.shape[0]
    n = T // BLOCK

    def body(pa_ref, dq_ref, dko_ref, dkn_ref, dvo_ref, dvn_ref, kg_ref, dpa_ref, dkg_ref):
        i = pl.program_id(0)
        more = i < n - 1
        dkn_tot = dko_ref[...] + jnp.where(more, dkn_ref[...], 0.0)
        dv_tot = dvo_ref[...] + jnp.where(more, dvn_ref[...], 0.0)
        kraw = pa_ref[:, RW:RW + KVW]
        bk = _seg_matrix(KVW, 6)
        _, rk = _qk_norm(kraw, kg_ref[...], bk)
        khat = kraw * rk
        dkh = dkn_tot * kg_ref[...]
        dpa_ref[:, 0:RW] = dq_ref[...]
        dpa_ref[:, RW:RW + KVW] = rk * (dkh - khat * (_segsum(dkh * khat, bk) * (1.0 / HEAD_DIM)))
        dpa_ref[:, RW + KVW:] = dv_tot
        prod = dkn_tot * khat
        _acc_rows(dkg_ref, jnp.sum(prod[:, 0:HEAD_DIM] + prod[:, HEAD_DIM:], axis=0, keepdims=True), i)

    kvb = pl.BlockSpec((BLOCK, KVW), lambda i: (i, 0))
    nxt = pl.BlockSpec((BLOCK, KVW), lambda i: (jnp.minimum(i + 1, n - 1), 0))
    return _pcall(body, name=name, grid=(n,),
                  in_specs=[pl.BlockSpec((BLOCK, ATT_COLS), lambda i: (i, 0)), pl.BlockSpec((BLOCK, RW), lambda i: (i, 0)),
                            kvb, nxt, kvb, nxt, _full_spec((1, KVW))],
                  out_specs=[pl.BlockSpec((BLOCK, ATT_COLS), lambda i: (i, 0)), _full_spec((1, HEAD_DIM))],
                  out_shape=[jax.ShapeDtypeStruct((T, ATT_COLS), F32), jax.ShapeDtypeStruct((1, HEAD_DIM), F32)],
                  compiler_params=_params(("arbitrary",)))(pa, dq, dko, dkn, dvo, dvn, kn_g)


WKV_CHUNK = 128
WKV_GROUP = 8


def _diag_mask():
    i = lax.broadcasted_iota(jnp.int32, (HEAD_DIM, RW), 0)
    j = lax.broadcasted_iota(jnp.int32, (HEAD_DIM, RW), 1) & (HEAD_DIM - 1)
    return i == j


def _heads_matrix():
    head = jnp.arange(RW // 2) // HEAD_DIM
    bd = (head[:, None] == head[None, :]).astype(BF16)
    return jnp.concatenate([bd, bd], axis=0)


def _headsums(xs, pieces, bd2):
    half = RW // 2
    bd = bd2[:pieces * half]
    rows = []
    for x in xs:
        parts, rest = [], x
        for n in range(pieces):
            p = rest.astype(BF16)
            parts.append(p)
            if n + 1 < pieces:
                rest = rest - p.astype(F32)
        for sl in (slice(0, half), slice(half, RW)):
            rows.append(jnp.concatenate([p[:, sl] for p in parts], axis=1))
    out = lax.dot_general(jnp.concatenate(rows, axis=0), bd, (((1,), (0,)), ((), ())), preferred_element_type=F32)
    return [jnp.concatenate([out[2 * n * HEAD_DIM:(2 * n + 1) * HEAD_DIM], out[(2 * n + 1) * HEAD_DIM:(2 * n + 2) * HEAD_DIM]],
                            axis=1) for n in range(len(xs))]


def _headsum(x):
    low = lax.broadcasted_iota(jnp.int32, (HEAD_DIM, LANE), 1) < HEAD_DIM
    tiles = []
    for c in range(RW // LANE):
        xt = x[:, c * LANE:(c + 1) * LANE]
        s_lo = jnp.sum(jnp.where(low, xt, 0.0), axis=1, keepdims=True)
        s_hi = jnp.sum(jnp.where(low, 0.0, xt), axis=1, keepdims=True)
        tiles.append(jnp.where(low, s_lo, s_hi))
    return jnp.concatenate(tiles, axis=1)


def _cols(rows, diag, bd2, pieces=2):
    return _headsums([jnp.where(diag, r, 0.0) for r in rows], pieces, bd2)


def _row(x, diag):
    return jnp.sum(jnp.where(diag, x, 0.0), axis=0, keepdims=True)


def wkv_fwd(r, w, k, v, a, b, *, name, gather=()):
    T = r.shape[0]
    ch = min(WKV_CHUNK, T)
    ngroups = ch // WKV_GROUP
    nchunks = T // ch
    ng = len(gather)

    def body(*refs):
        r_ref, w_ref, k_ref, v_ref, a_ref, b_ref, bd_ref = refs[:7]
        y_ref, st_ref = refs[7 + ng:9 + ng]
        s_scr = refs[9 + 2 * ng]
        step = pl.program_id(0)
        if ng:
            plan = _GatherPlan(refs[7:7 + ng], refs[9 + ng:9 + 2 * ng], refs[10 + 2 * ng:])
            pl.when(step == 0)(plan.start)
            pl.when(step == nchunks // 2)(plan.relay)

        @pl.when(step == 0)
        def _():
            s_scr[...] = jnp.zeros_like(s_scr)

        diag, bd2 = _diag_mask(), bd_ref[...]

        def group(gi, S):
            t0 = pl.multiple_of(gi * WKV_GROUP, WKV_GROUP)
            rows = pl.ds(t0, WKV_GROUP)
            R, W, K, V, A, B = (ref[rows, :] for ref in (r_ref, w_ref, k_ref, v_ref, a_ref, b_ref))
            vcols = _cols([V[s:s + 1] for s in range(WKV_GROUP)], diag, bd2, 1)
            yrows = []
            for s in range(WKV_GROUP):
                sa = _headsum(S * A[s:s + 1])
                S = S * W[s:s + 1] + sa * B[s:s + 1] + vcols[s] * K[s:s + 1]
                st_ref[t0 + s] = S
                yrows.append(_row(_headsums([S * R[s:s + 1]], 1, bd2)[0], diag))
            y_ref[rows, :] = jnp.concatenate(yrows, axis=0)
            return S

        s_scr[...] = lax.fori_loop(0, ngroups, group, s_scr[...])
        if ng:
            pl.when(step == nchunks - 1)(plan.finish_relayed)

    vec = pl.BlockSpec((ch, RW), lambda c: (c, 0))
    return _pcall(
        body, name=name, grid=(nchunks,), in_specs=[vec] * 6 + [_full_spec((RW, RW // 2))] + [_HBM] * ng,
        out_specs=[vec, pl.BlockSpec((ch, HEAD_DIM, RW), lambda c: (c, 0, 0))] + [_HBM] * ng,
        out_shape=[jax.ShapeDtypeStruct((T, RW), F32), jax.ShapeDtypeStruct((T, HEAD_DIM, RW), F32)] + _gathered_shapes(gather),
        scratch_shapes=[pltpu.VMEM((HEAD_DIM, RW), F32)] + (_GatherPlan.sems(ng) if ng else []),
        compiler_params=_params(("arbitrary",)),
    )(r, w, k, v, a, b, _heads_matrix(), *gather)


def wkv_bwd(r, w, k, v, a, b, dy, states, *, name, exchange=()):
    T = r.shape[0]
    ch = min(WKV_CHUNK, T)
    nchunks = T // ch
    ngroups = ch // WKV_GROUP
    ne = len(exchange)

    def body(*refs):
        r_ref, w_ref, k_ref, v_ref, a_ref, b_ref, dy_ref, st_ref, stp_ref, bd_ref = refs[:10]
        dr_ref, dw_ref, dk_ref, dv_ref, da_ref, db_ref = refs[10 + ne:16 + ne]
        ds_scr = refs[16 + 2 * ne]
        step = pl.program_id(0)
        if ne:
            plan = _ExchangePlan(refs[10:10 + ne], refs[16 + ne:16 + 2 * ne], refs[17 + 2 * ne:])
            pl.when(step == 0)(plan.start)

        @pl.when(step == 0)
        def _():
            ds_scr[...] = jnp.zeros_like(ds_scr)

        has_prev_chunk = step < nchunks - 1
        diag, bd2 = _diag_mask(), bd_ref[...]
        colsum = lambda x: jnp.sum(x, axis=0, keepdims=True)

        def group(gj, dS):
            gi = ngroups - 1 - gj
            t0 = pl.multiple_of(gi * WKV_GROUP, WKV_GROUP)
            rows = pl.ds(t0, WKV_GROUP)
            R, W, K, V, A, B, DY = (ref[rows, :] for ref in (r_ref, w_ref, k_ref, v_ref, a_ref, b_ref, dy_ref))
            before = jnp.where(gi > 0, st_ref[jnp.maximum(t0 - 1, 0)], jnp.where(has_prev_chunk, stp_ref[0], 0.0))
            prev_state = lambda s: st_ref[t0 + s - 1] if s > 0 else before
            steps = range(WKV_GROUP)
            dycols = _cols([DY[s:s + 1] for s in steps], diag, bd2, 1)
            vcols = _cols([V[s:s + 1] for s in steps], diag, bd2, 1)
            sas = _headsums([prev_state(s) * A[s:s + 1] for s in steps], 1, bd2)
            got = [[None] * WKV_GROUP for _ in range(6)]
            for s in reversed(steps):
                Sp = prev_state(s)
                dS = dS + dycols[s] * R[s:s + 1]
                got[0][s] = colsum(st_ref[t0 + s] * dycols[s])
                got[3][s] = _row(_headsums([dS * K[s:s + 1]], 1, bd2)[0], diag)
                got[2][s] = colsum(dS * vcols[s])
                dsa = _headsum(dS * B[s:s + 1])
                got[5][s] = colsum(dS * sas[s])
                got[1][s] = colsum(dS * Sp)
                got[4][s] = colsum(Sp * dsa)
                dS = dS * W[s:s + 1] + dsa * A[s:s + 1]
            for q, ref in enumerate((dr_ref, dw_ref, dk_ref, dv_ref, da_ref, db_ref)):
                ref[rows, :] = jnp.concatenate(got[q], axis=0)
            return dS

        ds_scr[...] = lax.fori_loop(0, ngroups, group, ds_scr[...])
        if ne:
            pl.when(step == nchunks - 1)(plan.finish)

    vec = pl.BlockSpec((ch, RW), lambda c: (nchunks - 1 - c, 0))
    st_spec = pl.BlockSpec((ch, HEAD_DIM, RW), lambda c: (nchunks - 1 - c, 0, 0))
    stp_spec = pl.BlockSpec((1, HEAD_DIM, RW), lambda c: (jnp.maximum((nchunks - 1 - c) * ch - 1, 0), 0, 0))
    return _pcall(
        body, name=name, grid=(nchunks,), in_specs=[vec] * 7 + [st_spec, stp_spec, _full_spec((RW, RW // 2))] + [_HBM] * ne,
        out_specs=[vec] * 6 + [_HBM] * ne,
        out_shape=[jax.ShapeDtypeStruct((T, RW), F32)] * 6 + [jax.ShapeDtypeStruct(e.shape, e.dtype) for e in exchange],
        scratch_shapes=[pltpu.VMEM((HEAD_DIM, RW), F32)] + (_ExchangePlan.sems(ne) if ne else []),
        compiler_params=_params(("arbitrary",)),
    )(r, w, k, v, a, b, dy, states, states, _heads_matrix(), *exchange)


_HBM = pl.BlockSpec(memory_space=pltpu.HBM)
_MESH = pl.DeviceIdType.MESH


def _place():
    x, y, c = lax.axis_index("x"), lax.axis_index("y"), lax.axis_index("c")
    return x, y, c, [(1 - x, y), (x, 1 - y), (1 - x, 1 - y)]


def _remote(src, dst, send_sem, recv_sem, to):
    return pltpu.make_async_remote_copy(src_ref=src, dst_ref=dst, send_sem=send_sem, recv_sem=recv_sem, device_id=to,
                                        device_id_type=_MESH)


def _dma_sems(*counts):
    return [pltpu.SemaphoreType.DMA((n,)) for n in counts]


class _GatherPlan:
    def __init__(self, ins, outs, sems):
        self.ins, self.outs, self.n = ins, outs, len(ins)
        self.ici_send, self.ici_recv, self.d2d_send, self.d2d_recv, self.local_sems = sems
        x, y, c, chips = _place()
        self.c, self.me, self.sibling = c, 2 * x + y, (x, y, 1 - c)
        self.peers = [(2 * qx + qy, (qx, qy, c)) for qx, qy in chips]

    @staticmethod
    def sems(n):
        return _dma_sems(3 * n, 3 * n, 3 * n, 3 * n, n)

    def _half(self, i, which):
        rh = self.ins[i].shape[0] // 2
        return pl.ds(which * rh, rh)

    def _local(self, i):
        return pltpu.make_async_copy(self.ins[i], self.outs[i].at[self.me], self.local_sems.at[i])

    def _send(self, i, j):
        k, mine = 3 * i + j, self._half(i, self.c)
        return _remote(self.ins[i].at[mine], self.outs[i].at[self.me, mine], self.ici_send.at[k], self.ici_recv.at[k],
                       self.peers[j][1])

    def _landed(self, i, j):
        k, piece = 3 * i + j, self.outs[i].at[self.peers[j][0], self._half(i, self.c)]
        return _remote(piece, piece, self.ici_send.at[k], self.ici_recv.at[k], self.peers[j][1])

    def _pass(self, i, j, which):
        k, piece = 3 * i + j, self.outs[i].at[self.peers[j][0], self._half(i, which)]
        return _remote(piece, piece, self.d2d_send.at[k], self.d2d_recv.at[k], self.sibling)

    def _all(self):
        return [(i, j) for i in range(self.n) for j in range(3)]

    def start(self):
        for i in range(self.n):
            self._local(i).start()
        for i, j in self._all():
            self._send(i, j).start()

    def relay(self):
        for i, j in self._all():
            self._landed(i, j).wait_recv()
            self._pass(i, j, self.c).start()

    def finish_relayed(self):
        for i, j in self._all():
            self._pass(i, j, 1 - self.c).wait_recv()
        for i, j in self._all():
            self._send(i, j).wait_send()
            self._pass(i, j, self.c).wait_send()
        for i in range(self.n):
            self._local(i).wait()

    def finish(self):
        self.relay()
        self.finish_relayed()

    @staticmethod
    def out_shapes(shards):
        return _gathered_shapes(shards)


def _gathered_shapes(shards):
    return [jax.ShapeDtypeStruct((N_CHIPS,) + s.shape, s.dtype) for s in shards]


def gather_weights(shards, *, name):
    n = len(shards)

    def body(*refs):
        plan = _GatherPlan(refs[:n], refs[n:2 * n], refs[2 * n:])
        plan.start()
        plan.finish()

    return _pcall(body, name=name, in_specs=[_HBM] * n, out_specs=[_HBM] * n, out_shape=_gathered_shapes(shards),
                  scratch_shapes=_GatherPlan.sems(n), compiler_params=_params())(*shards)


class _SiblingPlan:
    halves = True

    def __init__(self, ins, outs, sems):
        self.ins, self.outs, self.n = ins, outs, len(ins)
        self.send_sems, self.recv_sems = sems
        x, y, c, _ = _place()
        self.c, self.sibling = c, (x, y, 1 - c)

    @staticmethod
    def sems(n):
        return _dma_sems(n, n)

    @classmethod
    def out_shapes(cls, arrays):
        if not cls.halves:
            return [jax.ShapeDtypeStruct(a.shape, a.dtype) for a in arrays]
        return [jax.ShapeDtypeStruct((a.shape[0], a.shape[1] // 2, a.shape[2]), a.dtype) for a in arrays]

    def _copy(self, i):
        src = self.ins[i]
        if self.halves:
            rh = src.shape[1] // 2
            src = src.at[:, pl.ds((1 - self.c) * rh, rh)]
        return _remote(src, self.outs[i], self.send_sems.at[i], self.recv_sems.at[i], self.sibling)

    def start(self):
        for i in range(self.n):
            self._copy(i).start()

    def finish(self):
        for i in range(self.n):
            self._copy(i).wait_recv()
        for i in range(self.n):
            self._copy(i).wait_send()


class _SiblingWhole(_SiblingPlan):
    halves = False


def to_sibling(arrays, take_other_half, *, name):
    n = len(arrays)
    plan_cls = _SiblingPlan if take_other_half else _SiblingWhole

    def body(*refs):
        plan = plan_cls(refs[:n], refs[n:2 * n], refs[2 * n:])
        plan.start()
        plan.finish()

    return _pcall(body, name=name, in_specs=[_HBM] * n, out_specs=[_HBM] * n, out_shape=plan_cls.out_shapes(arrays),
                  scratch_shapes=plan_cls.sems(n), compiler_params=_params())(*arrays)


def exchange_chips(arrays, *, name):
    n = len(arrays)

    def body(*refs):
        plan = _ExchangePlan(refs[:n], refs[n:2 * n], refs[2 * n:])
        plan.start()
        plan.finish()

    return _pcall(body, name=name, in_specs=[_HBM] * n, out_specs=[_HBM] * n,
                  out_shape=[jax.ShapeDtypeStruct(a.shape, a.dtype) for a in arrays],
                  scratch_shapes=_ExchangePlan.sems(n), compiler_params=_params())(*arrays)


class _ExchangePlan:
    def __init__(self, ins, outs, sems):
        self.ins, self.outs, self.n = ins, outs, len(ins)
        self.send_sems, self.recv_sems, self.local_sems = sems
        x, y, c, chips = _place()
        self.me = 2 * x + y
        self.peers = [(2 * qx + qy, (qx, qy, c)) for qx, qy in chips]

    @staticmethod
    def sems(n):
        return _dma_sems(3 * n, 3 * n, n)

    @staticmethod
    def out_shapes(arrays):
        return [jax.ShapeDtypeStruct(a.shape, a.dtype) for a in arrays]

    def _local(self, i):
        return pltpu.make_async_copy(self.ins[i].at[self.me], self.outs[i].at[self.me], self.local_sems.at[i])

    def _send(self, i, j):
        k = 3 * i + j
        return _remote(self.ins[i].at[self.peers[j][0]], self.outs[i].at[self.me], self.send_sems.at[k], self.recv_sems.at[k],
                       self.peers[j][1])

    def _landed(self, i, j):
        k, piece = 3 * i + j, self.outs[i].at[self.peers[j][0]]
        return _remote(piece, piece, self.send_sems.at[k], self.recv_sems.at[k], self.peers[j][1])

    def start(self):
        for i in range(self.n):
            self._local(i).start()
            for j in range(3):
                self._send(i, j).start()

    def finish(self):
        for i in range(self.n):
            for j in range(3):
                self._landed(i, j).wait_recv()
        for i in range(self.n):
            for j in range(3):
                self._send(i, j).wait_send()
            self._local(i).wait()


def _core_index():
    return lax.axis_index("c").astype(jnp.int32).reshape(1)


def pair_sum(g, theirs, wire_dtype, *, name):
    _, R, C = g.shape
    rh = R // 2
    tr = _tile(rh, 256, 16)
    nt = rh // tr

    def body(c_ref, g_ref, t_ref, q_ref, qw_ref):
        q = g_ref[...] + t_ref[...]
        q_ref[...] = q
        qw_ref[...] = q.astype(wire_dtype)

    blk = pl.BlockSpec((1, tr, C), lambda b, i, c_ref: (b, i, 0))
    mine = pl.BlockSpec((1, tr, C), lambda b, i, c_ref: (b, c_ref[0] * nt + i, 0))
    grid_spec = pltpu.PrefetchScalarGridSpec(num_scalar_prefetch=1, grid=(N_CHIPS, nt), in_specs=[mine, blk], out_specs=[blk, blk])
    return _pcall(body, name=name, grid_spec=grid_spec,
                  out_shape=[jax.ShapeDtypeStruct((N_CHIPS, rh, C), F32), jax.ShapeDtypeStruct((N_CHIPS, rh, C), wire_dtype)],
                  compiler_params=_params(("parallel", "parallel")))(_core_index(), g, theirs)


def half_sum(own, landed, *, name):
    _, rh, C = own.shape
    tr = _tile(rh, 256, 16)

    def body(me_ref, own_ref, land_ref, o_ref):
        total = None
        for p in range(N_CHIPS):
            term = jnp.where(me_ref[0] == p, own_ref[p], land_ref[p].astype(F32))
            total = term if total is None else total + term
        o_ref[...] = total

    blk = pl.BlockSpec((N_CHIPS, tr, C), lambda i, me_ref: (0, i, 0))
    grid_spec = pltpu.PrefetchScalarGridSpec(num_scalar_prefetch=1, grid=(rh // tr,), in_specs=[blk, blk],
                                             out_specs=pl.BlockSpec((tr, C), lambda i, me_ref: (i, 0)))
    me = (2 * lax.axis_index("x") + lax.axis_index("y")).astype(jnp.int32).reshape(1)
    return _pcall(body, name=name, grid_spec=grid_spec, out_shape=jax.ShapeDtypeStruct((rh, C), F32),
                  compiler_params=_params(("parallel",)))(me, own, landed)


def adamw(w, m, v, mine, theirs, *, name):
    _, R, C = w.shape
    rh = R // 2
    tr = _tile(rh, 256, 8)
    nt = rh // tr

    def body(c_ref, w_ref, m_ref, v_ref, a_ref, b_ref, g_ref, d_ref, nm_ref, nv_ref):
        is_mine = (pl.program_id(0) // nt) == c_ref[0]
        g = jnp.where(is_mine, a_ref[...], b_ref[...])
        g_ref[...] = g
        nm = ADAM_B1 * m_ref[...] + (1.0 - ADAM_B1) * g
        nv = ADAM_B2 * v_ref[...] + (1.0 - ADAM_B2) * (g * g)
        nm_ref[...] = nm
        nv_ref[...] = nv
        m_hat = nm / (1.0 - ADAM_B1 ** ADAM_STEP)
        v_hat = nv / (1.0 - ADAM_B2 ** ADAM_STEP)
        d_ref[...] = -ADAM_LR * (m_hat / (jnp.sqrt(v_hat) + ADAM_EPS) + ADAM_WD * w_ref[...])

    full = pl.BlockSpec((None, tr, C), lambda i, c_ref: (0, i, 0))
    a_spec = pl.BlockSpec((tr, C), lambda i, c_ref: (jnp.clip(i - c_ref[0] * nt, 0, nt - 1), 0))
    b_spec = pl.BlockSpec((tr, C), lambda i, c_ref: (jnp.clip(i - (1 - c_ref[0]) * nt, 0, nt - 1), 0))
    grid_spec = pltpu.PrefetchScalarGridSpec(num_scalar_prefetch=1, grid=(2 * nt,), in_specs=[full] * 3 + [a_spec, b_spec],
                                             out_specs=[full] * 4)
    return _pcall(body, name=name, grid_spec=grid_spec, out_shape=[jax.ShapeDtypeStruct((1, R, C), F32)] * 4,
                  compiler_params=_params(("arbitrary",)))(_core_index(), w, m, v, mine, theirs)


def _to_blocks(full, axis):
    r, c = full.shape
    if axis == 1:
        return full.reshape(r, N_CHIPS, c // N_CHIPS).transpose(1, 0, 2)
    return full.reshape(N_CHIPS, r // N_CHIPS, c)


def _from_blocks(blocks, axis):
    _, r, c = blocks.shape
    if axis == 1:
        return blocks.transpose(1, 0, 2).reshape(r, N_CHIPS * c)
    return blocks.reshape(N_CHIPS * r, c)


def _ffn_fwd(x, norm, wg_t, wu_t, wd, tag):
    h = rms_fwd(x, norm, name=tag + "_norm")
    gate, up, act = mm_fused(h, [wg_t, wu_t], _swiglu, [BF16] * 3, tb=True, name=tag + "_gate_up")
    out = mm(act, wd, scale=0.5, res=x, name=tag + "_down")
    return out, (h, gate, up, act)


def _ffn_bwd(dout, x, saved, norm, wg_t, wu_t, wd, tag, carry=None, reduce=None):
    h, gate, up, act = saved
    dgate, dup, *carried = mm_fused(dout, [wd], _swiglu_bwd, [BF16] * 2, tb=True, extras=[gate, up], name=tag + "_dact",
                                    carry=carry)
    dwd = mm(act, dout, ta=True, scale=0.5, name=tag + "_dwd")
    if reduce:
        down_blocks = _to_blocks(dwd, 0)
        dwg_t, down_from_sibling = mm(dgate, h, ta=True, name=tag + "_dwg", carry=(_SiblingPlan, [down_blocks]))
        down_pair = pair_sum(down_blocks, down_from_sibling, BF16, name="pair_sum_" + reduce[2])
        dwu_t, down_landed = mm(dup, h, ta=True, name=tag + "_dwu", carry=(_ExchangePlan, [down_pair[1]]))
        blocks = [_to_blocks(g, 0) for g in (dwg_t, dwu_t)]
        dh, *from_sibling = mm(dgate, wg_t, name=tag + "_dh_gate", carry=(_SiblingPlan, blocks))
        pair = [pair_sum(g, t, BF16, name="pair_sum_" + n) for g, t, n in zip(blocks, from_sibling, reduce[:2])]
        dx, dnorm, *landed = mm(dup, wu_t, res=dh, name=tag + "_dh_up", norm_bwd=(x, norm, dout),
                                carry=(_ExchangePlan, [q for _, q in pair]))
        return dx, dnorm, dwg_t, dwu_t, dwd, carried, (pair + [down_pair], landed + [down_landed])
    dwg_t = mm(dgate, h, ta=True, name=tag + "_dwg")
    dwu_t = mm(dup, h, ta=True, name=tag + "_dwu")
    dh = mm(dgate, wg_t, name=tag + "_dh_gate")
    dx, dnorm = mm(dup, wu_t, res=dh, name=tag + "_dh_up", norm_bwd=(x, norm, dout))
    return dx, dnorm, dwg_t, dwu_t, dwd, carried, None


TRANSPOSED = ('ffn1_w_gate', 'ffn1_w_up', 'ffn2_w_gate', 'ffn2_w_up')
FIRST_WEIGHTS = ['ffn1_w_gate', 'ffn1_w_up', 'ffn1_w_down']
MID_WEIGHTS = ['w_in', 'rwkv_w_lora_up', 'rwkv_a_lora_up', 'rwkv_g_lora_up']
LATE_WEIGHTS = ['w_branch_rwkv', 'w_branch_attn', 'w_out', 'ffn2_w_gate', 'ffn2_w_up', 'ffn2_w_down']


def _pair_sums(names, blocks, tag):
    from_sibling = to_sibling(blocks, True, name=tag + "_grads_to_sibling")
    return [pair_sum(g, t, F32 if n == 'small' else BF16, name="pair_sum_" + n)
            for g, t, n in zip(blocks, from_sibling, names)]


def _step(A):
    x, tgt = A['x'][0], A['loss_target'][0]
    T = x.shape[0]
    w = {n: A[n][0] for n in WEIGHT_NAMES}
    row = lambda a: a.reshape(1, -1)

    axis_of = {n: (0 if n in TRANSPOSED else axis) for n, axis in BIG}
    natural = lambda n, a: jnp.swapaxes(a, 1, 2) if n in TRANSPOSED else a
    shard = lambda n: natural(n, A[n])[0].astype(BF16)
    n1, nmix, n2, nfin = (row(w[n]) for n in ('ffn1_norm', 'mix_norm', 'ffn2_norm', 'final_norm'))
    gathered = gather_weights([shard(n) for n in FIRST_WEIGHTS[:2]], name="gather_weights")
    full = {n: _from_blocks(b, axis_of[n]) for n, b in zip(FIRST_WEIGHTS[:2], gathered)}
    h1 = rms_fwd(x, n1, name="ffn1_norm")
    next_weights = FIRST_WEIGHTS[2:] + MID_WEIGHTS
    gate1, up1, act1, *gathered = mm_fused(h1, [full['ffn1_w_gate'], full['ffn1_w_up']], _swiglu, [BF16] * 3, tb=True,
                                           name="ffn1_gate_up", carry=(_GatherPlan, [shard(n) for n in next_weights]))
    full.update({n: _from_blocks(b, axis_of[n]) for n, b in zip(next_weights, gathered)})
    x1 = mm(act1, full['ffn1_w_down'], scale=0.5, res=x, name="ffn1_down")
    ffn1 = (h1, gate1, up1, act1)
    w_in_r = _pad_rwkv_cols(full['w_in'][:, :RWKV_COLS])
    w_in_a = full['w_in'][:, RWKV_COLS:RWKV_COLS + ATT_COLS]
    w_in_g = full['w_in'][:, RWKV_COLS + ATT_COLS:]
    wlw, wla, wlg = (_pad_rows(full[n], 128).astype(F32) for n in ('rwkv_w_lora_up', 'rwkv_a_lora_up', 'rwkv_g_lora_up'))
    mu = _pad_rwkv_cols(row(w['rwkv_mu']))
    w0, a0, k_k, k_a, r_k, ln_w, ln_b = (row(w[n]) for n in ('rwkv_w0', 'rwkv_a0', 'rwkv_k_k', 'rwkv_k_a', 'rwkv_r_k',
                                                               'rwkv_ln_w', 'rwkv_ln_b'))
    qg = jnp.tile(row(w['attn_q_norm']), (1, N_HEADS))
    kg = jnp.tile(row(w['attn_k_norm']), (1, KVW // HEAD_DIM))
    sinks = jnp.pad(row(w['attn_sinks']), ((0, 0), (0, LANE - N_HEADS)))

    h2 = rms_fwd(x1, nmix, name="mix_norm")
    pr = mm(h2, w_in_r, name="proj_rwkv")
    pa = mm(h2, w_in_a, name="proj_att")
    pg = mm(h2, w_in_g, name="proj_gate")
    pr_shift = jnp.pad(pr, ((1, 0), (0, 0)))[:-1]
    r, dec, k2, v, a, b, sg = rwkv_pre_fwd(pr, pr_shift, mu, w0, a0, k_k, k_a, wlw, wla, wlg, name="rwkv_pre")
    y, states, *gathered = wkv_fwd(r, dec, k2, v, a, b, name="wkv_fwd", gather=[shard(n) for n in LATE_WEIGHTS])
    full.update({n: _from_blocks(b, axis_of[n]) for n, b in zip(LATE_WEIGHTS, gathered)})
    yr = rwkv_post_fwd(y, r, k2, v, sg, wlg, ln_w, ln_b, r_k, name="rwkv_post")
    ya = att_fwd(pa, qg, kg, sinks, name="att_fwd")
    br = mm(yr, full['w_branch_rwkv'], name="branch_rwkv")
    ba = mm(ya, full['w_branch_attn'], name="branch_att")
    mg = merge_fwd(br, ba, pg, name="merge")
    x2 = mm(mg, full['w_out'], res=x1, name="mix_out")
    x3, ffn2 = _ffn_fwd(x2, n2, full['ffn2_w_gate'], full['ffn2_w_up'], full['ffn2_w_down'], "ffn2")
    dx3, d_nfin, loss = final_loss(x3, tgt, nfin, name="final_loss")

    G = {'final_norm': d_nfin}
    dx2, G['ffn2_norm'], G['ffn2_w_gate'], G['ffn2_w_up'], G['ffn2_w_down'], _, _ = _ffn_bwd(
        dx3, x2, ffn2, n2, full['ffn2_w_gate'], full['ffn2_w_up'], full['ffn2_w_down'], "ffn2")
    dmg = mm(dx2, full['w_out'], tb=True, name="d_merge")
    G['w_out'] = mm(mg, dx2, ta=True, name="d_w_out")
    dbr, dba, dpg = merge_bwd(dmg, br, ba, pg, name="merge_bwd")
    dyr = mm(dbr, full['w_branch_rwkv'], tb=True, name="d_y_rwkv")
    G['w_branch_rwkv'] = mm(yr, dbr, ta=True, name="d_w_branch_rwkv")
    dya = mm(dba, full['w_branch_attn'], tb=True, name="d_y_att")
    G['w_branch_attn'] = mm(ya, dba, ta=True, name="d_w_branch_att")
    late_blocks = [_to_blocks(G[n], axis_of[n]) for n in LATE_WEIGHTS]
    dy, dz, dg, G['rwkv_ln_w'], G['rwkv_ln_b'], *late_from_sibling = rwkv_post_bwd(
        dyr, y, r, k2, v, sg, wlg, ln_w, ln_b, r_k, name="rwkv_post_bwd", carry=(_SiblingPlan, late_blocks))
    late_pair = [pair_sum(g, t, BF16, name="pair_sum_" + n) for g, t, n in zip(late_blocks, late_from_sibling, LATE_WEIGHTS)]
    res = wkv_bwd(r, dec, k2, v, a, b, dy, states, name="wkv_bwd", exchange=[q for _, q in late_pair])
    wkv_grads, late_landed = res[:6], res[6:]
    (dpr, d_mu, G['rwkv_w0'], G['rwkv_a0'], G['rwkv_k_k'], G['rwkv_k_a'], G['rwkv_r_k'], d_wlw, d_wla, d_wlg) = rwkv_pre_bwd(
        pr, pr_shift, *wkv_grads, dz, dg, mu, w0, a0, k_k, k_a, r_k, wlw, wla, wlg, name="rwkv_pre_bwd")
    G['rwkv_mu'] = _unpad_rwkv_cols(d_mu)
    G['rwkv_w_lora_up'], G['rwkv_a_lora_up'], G['rwkv_g_lora_up'] = d_wlw[:DECAY_LORA], d_wla[:ICLR_LORA], d_wlg[:GATE_LORA]
    dq, dko, dkn, dvo, dvn, G['attn_q_norm'], d_sinks = att_bwd(pa, dya, qg, kg, sinks, name="att_bwd")
    G['attn_sinks'] = d_sinks[:, :N_HEADS]
    dpa, G['attn_k_norm'] = att_kv_bwd(pa, dq, dko, dkn, dvo, dvn, kg, name="att_kv_bwd")
    d_w_in_r = mm(h2, dpr, ta=True, name="d_w_in_rwkv")
    d_w_in_a = mm(h2, dpa, ta=True, name="d_w_in_att")
    d_w_in_g = mm(h2, dpg, ta=True, name="d_w_in_gate")
    G['w_in'] = jnp.concatenate([_unpad_rwkv_cols(d_w_in_r), d_w_in_a, d_w_in_g], axis=1)
    mid_blocks = [_to_blocks(G[n], axis_of[n]) for n in MID_WEIGHTS]
    dh2, *mid_from_sibling = mm(dpr, w_in_r, tb=True, name="d_h2_rwkv", carry=(_SiblingPlan, mid_blocks))
    dh2 = mm(dpa, w_in_a, tb=True, res=dh2, name="d_h2_att")
    dx1, G['mix_norm'] = mm(dpg, w_in_g, tb=True, res=dh2, name="d_h2_gate", norm_bwd=(x1, nmix, dx2))
    mid_pair = [pair_sum(g, t, BF16, name="pair_sum_" + n) for g, t, n in zip(mid_blocks, mid_from_sibling, MID_WEIGHTS)]
    dx0, G['ffn1_norm'], _, _, _, mid_landed, (first_pair, first_landed) = _ffn_bwd(
        dx1, x, ffn1, n1, full['ffn1_w_gate'], full['ffn1_w_up'], full['ffn1_w_down'], "ffn1",
        carry=(_ExchangePlan, [q for _, q in mid_pair]), reduce=FIRST_WEIGHTS)

    small_shapes = [(w[n].size,) for n in SMALL] + [(1,)]

    def small_rows(parts):
        vec = jnp.concatenate([p.reshape(-1) for p in parts])
        return jnp.pad(vec, (0, SMALL_ROWS * FLAT_W - vec.shape[0])).reshape(SMALL_ROWS, FLAT_W)

    small = small_rows([G[n] for n in SMALL] + [loss[0, :1]])
    small_pair = _pair_sums(['small'], [jnp.broadcast_to(small[None], (N_CHIPS,) + small.shape)], "small")
    small_landed = exchange_chips([q for _, q in small_pair], name="exchange_small")
    names = FIRST_WEIGHTS + ['small'] + MID_WEIGHTS + LATE_WEIGHTS
    pair = first_pair + small_pair + mid_pair + late_pair
    landed = list(first_landed) + list(small_landed) + list(mid_landed) + list(late_landed)
    halves = [half_sum(own, l, name="half_sum_" + n) for (own, _), l, n in zip(pair, landed, names)]
    other_halves = to_sibling(halves, False, name="halves_to_sibling")

    def local(prefix, n):
        if n != 'small':
            return natural(n, A[prefix + n])
        return small_rows([A[prefix + s] for s in SMALL] + [jnp.zeros((1,), F32)])[None]

    result = {}
    for n, mine, theirs in zip(names, halves, other_halves):
        outs4 = adamw(local('', n), local('m_', n), local('v_', n), mine, theirs, name="adamw_" + n)
        for kind, o in zip(('grad_', 'delta_', 'new_m_', 'new_v_'), outs4):
            if n != 'small':
                result[kind + n] = natural(n, o)
            else:
                for s, part in zip(SMALL + ['loss'], _unpack_vec(o.reshape(-1), small_shapes)):
                    result[kind + s] = part.reshape(A[s].shape) if s != 'loss' else part.reshape(())
    outs = [result['grad_loss'], dx0[None]]
    for kind in ('grad_', 'delta_', 'new_m_', 'new_v_'):
        outs += [result[kind + n] for n in WEIGHT_NAMES]
    return tuple(outs)


def _unpack_vec(vec, shapes):
    out, off = [], 0
    for (n,) in shapes:
        out.append(vec[off:off + n])
        off += n
    return out


def kernel(x, ffn1_norm, ffn1_w_gate, ffn1_w_up, ffn1_w_down, mix_norm, w_in, rwkv_mu, rwkv_w0, rwkv_w_lora_up, rwkv_a0, rwkv_a_lora_up, rwkv_g_lora_up, rwkv_k_k, rwkv_k_a, rwkv_r_k, rwkv_ln_w, rwkv_ln_b, attn_q_norm, attn_k_norm, attn_sinks, w_branch_rwkv, w_branch_attn, w_out, ffn2_norm, ffn2_w_gate, ffn2_w_up, ffn2_w_down, final_norm, loss_target, m_ffn1_norm, m_ffn1_w_gate, m_ffn1_w_up, m_ffn1_w_down, m_mix_norm, m_w_in, m_rwkv_mu, m_rwkv_w0, m_rwkv_w_lora_up, m_rwkv_a0, m_rwkv_a_lora_up, m_rwkv_g_lora_up, m_rwkv_k_k, m_rwkv_k_a, m_rwkv_r_k, m_rwkv_ln_w, m_rwkv_ln_b, m_attn_q_norm, m_attn_k_norm, m_attn_sinks, m_w_branch_rwkv, m_w_branch_attn, m_w_out, m_ffn2_norm, m_ffn2_w_gate, m_ffn2_w_up, m_ffn2_w_down, m_final_norm, v_ffn1_norm, v_ffn1_w_gate, v_ffn1_w_up, v_ffn1_w_down, v_mix_norm, v_w_in, v_rwkv_mu, v_rwkv_w0, v_rwkv_w_lora_up, v_rwkv_a0, v_rwkv_a_lora_up, v_rwkv_g_lora_up, v_rwkv_k_k, v_rwkv_k_a, v_rwkv_r_k, v_rwkv_ln_w, v_rwkv_ln_b, v_attn_q_norm, v_attn_k_norm, v_attn_sinks, v_w_branch_rwkv, v_w_branch_attn, v_w_out, v_ffn2_norm, v_ffn2_w_gate, v_ffn2_w_up, v_ffn2_w_down, v_final_norm):
    return _step(dict(locals()))
```

```python
import functools

import jax
import jax.numpy as jnp
from jax import lax
from jax.experimental import pallas as pl
from jax.experimental.pallas import tpu as pltpu

F32 = jnp.float32
BF16 = jnp.bfloat16

D_MODEL = 1024
D_FF = 2816
HEAD_DIM = 64
N_HEADS = 8
RW = 512
KVW = 128
ATT_GROUP = 4
WINDOW = 128
BLOCK = 128
DECAY_LORA, ICLR_LORA, GATE_LORA = 32, 32, 96
RWKV_COLS = 3 * RW + DECAY_LORA + ICLR_LORA + GATE_LORA
ATT_COLS = RW + 2 * KVW
GATE_COLS = 2 * D_MODEL
RWKV_PAD = 3 * RW + 3 * 128
RMS_EPS = 1e-6
GN_EPS = 64e-5
N_CHIPS = 4
LANE = 128
FLAT_W = 1024
SMALL_ROWS = 32
NEG_BIG = -1e30

ADAM_LR, ADAM_B1, ADAM_B2, ADAM_EPS, ADAM_WD, ADAM_STEP = 0.001, 0.9, 0.999, 1e-08, 0.01, 10

VMEM_LIMIT = 56 * 1024 * 1024

WEIGHT_NAMES = ['ffn1_norm', 'ffn1_w_gate', 'ffn1_w_up', 'ffn1_w_down', 'mix_norm', 'w_in', 'rwkv_mu', 'rwkv_w0',
                'rwkv_w_lora_up', 'rwkv_a0', 'rwkv_a_lora_up', 'rwkv_g_lora_up', 'rwkv_k_k', 'rwkv_k_a', 'rwkv_r_k',
                'rwkv_ln_w', 'rwkv_ln_b', 'attn_q_norm', 'attn_k_norm', 'attn_sinks', 'w_branch_rwkv',
                'w_branch_attn', 'w_out', 'ffn2_norm', 'ffn2_w_gate', 'ffn2_w_up', 'ffn2_w_down', 'final_norm']
BIG = [('ffn1_w_gate', 1), ('ffn1_w_up', 1), ('ffn1_w_down', 0), ('w_in', 1), ('rwkv_w_lora_up', 1),
       ('rwkv_a_lora_up', 1), ('rwkv_g_lora_up', 1), ('w_branch_rwkv', 1), ('w_branch_attn', 1), ('w_out', 0),
       ('ffn2_w_gate', 1), ('ffn2_w_up', 1), ('ffn2_w_down', 0)]
SMALL = ['ffn1_norm', 'mix_norm', 'rwkv_mu', 'rwkv_w0', 'rwkv_a0', 'rwkv_k_k', 'rwkv_k_a', 'rwkv_r_k', 'rwkv_ln_w',
         'rwkv_ln_b', 'attn_q_norm', 'attn_k_norm', 'attn_sinks', 'ffn2_norm', 'final_norm']


def _pcall(body, **kw):
    return pl.pallas_call(body, **kw)


def _params(sem=None, **kw):
    if sem is not None:
        kw['dimension_semantics'] = sem
    return pltpu.CompilerParams(vmem_limit_bytes=VMEM_LIMIT, **kw)


def _tile(n, cap, mult):
    best = None
    for t in range(mult, min(n, cap) + 1, mult):
        if n % t == 0:
            best = t
    return best or n


def _sigmoid(z):
    return 1.0 / (1.0 + jnp.exp(-z))


def _softplus(z):
    return jnp.maximum(z, 0.0) + jnp.log(1.0 + jnp.exp(-jnp.abs(z)))


def _bdot(a, b, dims=(((1,), (0,)), ((), ()))):
    return lax.dot_general(a.astype(BF16), b.astype(BF16), dims, preferred_element_type=F32)


_NT = (((1,), (1,)), ((), ()))
_TN = (((0,), (0,)), ((), ()))


def _segsum(x, bd):
    hi = x.astype(BF16)
    lo = (x - hi.astype(F32)).astype(BF16)
    dot = functools.partial(lax.dot_general, dimension_numbers=(((1,), (0,)), ((), ())), preferred_element_type=F32)
    return dot(hi, bd) + dot(lo, bd)


_LORA_EDGES = (3 * RW, 3 * RW + DECAY_LORA, 3 * RW + DECAY_LORA + ICLR_LORA, RWKV_COLS)


def _pad_rwkv_cols(x):
    parts = [x[..., :3 * RW]]
    for lo, hi in zip(_LORA_EDGES[:-1], _LORA_EDGES[1:]):
        parts.append(jnp.pad(x[..., lo:hi], [(0, 0)] * (x.ndim - 1) + [(0, 128 - (hi - lo))]))
    return jnp.concatenate(parts, axis=-1)


def _unpad_rwkv_cols(x):
    parts = [x[..., :3 * RW]]
    for j, (lo, hi) in enumerate(zip(_LORA_EDGES[:-1], _LORA_EDGES[1:])):
        parts.append(x[..., 3 * RW + 128 * j:3 * RW + 128 * j + (hi - lo)])
    return jnp.concatenate(parts, axis=-1)


def _pad_rows(x, rows):
    return jnp.pad(x, [(0, rows - x.shape[0])] + [(0, 0)] * (x.ndim - 1))


def mm(a, b, *, name, ta=False, tb=False, scale=None, res=None, out_dtype=F32, carry=None, norm_bwd=None):
    M, K = (a.shape[1], a.shape[0]) if ta else a.shape
    N = b.shape[0] if tb else b.shape[1]
    assert (b.shape[1] if tb else b.shape[0]) == K
    tm, tn, tk = _tile(M, 1408 if ta else 512, 128), _tile(N, 1408, 128), _tile(K, 1408, 128)
    nk = K // tk
    grid = (M // tm, N // tn, nk)
    dims = (((0 if ta else 1,), (1 if tb else 0,)), ((), ()))
    plan_cls, carried = carry if carry else (None, ())
    nn = 3 if norm_bwd else 0
    nd = 1 if norm_bwd else 0
    nc, nin = len(carried), 2 + (res is not None) + nn
    assert not norm_bwd or (tn == N and out_dtype == F32)

    def body(*refs):
        a_ref, b_ref = refs[:2]
        r_ref = refs[2] if res is not None else None
        o_ref, acc_ref = refs[nin + nc], refs[nin + 2 * nc + 1 + nd]
        row_tile, k = pl.program_id(0), pl.program_id(2)
        if nc:
            plan = plan_cls(refs[nin:nin + nc], refs[nin + nc + 1 + nd:nin + 2 * nc + 1 + nd], refs[nin + 2 * nc + 2 + nd:])
            at = lambda which: functools.reduce(jnp.logical_and, [pl.program_id(d) == (0 if which == 0 else grid[d] - 1)
                                                                 for d in range(3)])
            pl.when(at(0))(plan.start)
        part = _bdot(a_ref[...], b_ref[...], dims)

        @pl.when(k == 0)
        def _():
            acc_ref[...] = part

        @pl.when(k > 0)
        def _():
            acc_ref[...] += part

        @pl.when(k == nk - 1)
        def _():
            o = acc_ref[...]
            if scale is not None:
                o = o * scale
            if r_ref is not None:
                o = o + r_ref[...].astype(F32)
            if norm_bwd:
                x_ref, g_ref, dres_ref = refs[nin - 3:nin]
                xv = x_ref[...]
                r = lax.rsqrt(jnp.mean(xv * xv, axis=-1, keepdims=True) + RMS_EPS)
                xh = xv * r
                dxh = o * g_ref[...]
                _acc_rows(refs[nin + nc + 1], jnp.sum(o * xh, axis=0, keepdims=True), row_tile)
                o = dres_ref[...] + r * (dxh - xh * jnp.mean(dxh * xh, axis=-1, keepdims=True))
            o_ref[...] = o.astype(out_dtype)

        if nc:
            pl.when(at(1))(plan.finish)

    a_spec = pl.BlockSpec((tk, tm), lambda i, j, k: (k, i)) if ta else pl.BlockSpec((tm, tk), lambda i, j, k: (i, k))
    b_spec = pl.BlockSpec((tn, tk), lambda i, j, k: (j, k)) if tb else pl.BlockSpec((tk, tn), lambda i, j, k: (k, j))
    o_spec = pl.BlockSpec((tm, tn), lambda i, j, k: (i, j))
    g_spec = pl.BlockSpec((1, N), lambda i, j, k: (0, 0))
    in_specs = [a_spec, b_spec] + ([o_spec] if res is not None else []) + ([o_spec, g_spec, o_spec] if norm_bwd else [])
    args = (a, b) + ((res,) if res is not None else ()) + (tuple(norm_bwd) if norm_bwd else ())
    out_shape = [jax.ShapeDtypeStruct((M, N), out_dtype)] + [jax.ShapeDtypeStruct((1, N), F32)] * nd
    out_specs = [o_spec] + [g_spec] * nd
    if not nc and not nd:
        return _pcall(
            body, name=name, grid=grid, in_specs=in_specs, out_specs=o_spec, out_shape=out_shape[0],
            scratch_shapes=[pltpu.VMEM((tm, tn), F32)], compiler_params=_params(("parallel", "parallel", "arbitrary")),
        )(*args)
    return _pcall(
        body, name=name, grid=grid, in_specs=in_specs + [_HBM] * nc, out_specs=out_specs + [_HBM] * nc,
        out_shape=out_shape + (plan_cls.out_shapes(carried) if nc else []),
        scratch_shapes=[pltpu.VMEM((tm, tn), F32)] + (plan_cls.sems(nc) if nc else []),
        compiler_params=_params(("arbitrary", "arbitrary", "arbitrary")),
    )(*args, *carried)


def mm_fused(a, bs, finish, out_dtypes, *, name, tb=False, extras=(), carry=None):
    M, K = a.shape
    N = bs[0].shape[0] if tb else bs[0].shape[1]
    tm, tn = _tile(M, 512, 128), _tile(N, 1408, 128)
    grid = (M // tm, N // tn)
    dims = (((1,), (1 if tb else 0,)), ((), ()))
    plan_cls, carried = carry if carry else (None, ())
    nc, nb, nx, no = len(carried), len(bs), len(extras), len(out_dtypes)
    nin = 1 + nb + nx

    def body(*refs):
        a_ref, b_refs, x_refs = refs[0], refs[1:1 + nb], refs[1 + nb:nin]
        o_refs = refs[nin + nc:nin + nc + no]
        if nc:
            plan = plan_cls(refs[nin:nin + nc], refs[nin + nc + no:nin + 2 * nc + no], refs[nin + 2 * nc + no:])
            at = lambda which: jnp.logical_and(*[pl.program_id(d) == (0 if which == 0 else grid[d] - 1) for d in range(2)])
            pl.when(at(0))(plan.start)
        av = a_ref[...]
        outs = finish([_bdot(av, b_ref[...], dims) for b_ref in b_refs], [x_ref[...] for x_ref in x_refs])
        for o_ref, o in zip(o_refs, outs):
            o_ref[...] = o.astype(o_ref.dtype)
        if nc:
            pl.when(at(1))(plan.finish)

    a_spec = pl.BlockSpec((tm, K), lambda i, j: (i, 0))
    b_spec = pl.BlockSpec((tn, K), lambda i, j: (j, 0)) if tb else pl.BlockSpec((K, tn), lambda i, j: (0, j))
    o_spec = pl.BlockSpec((tm, tn), lambda i, j: (i, j))
    out_shape = [jax.ShapeDtypeStruct((M, N), d) for d in out_dtypes]
    if not nc:
        return _pcall(body, name=name, grid=grid, in_specs=[a_spec] + [b_spec] * nb + [o_spec] * nx, out_specs=[o_spec] * no,
                      out_shape=out_shape, compiler_params=_params(("parallel", "parallel")))(a, *bs, *extras)
    return _pcall(body, name=name, grid=grid, in_specs=[a_spec] + [b_spec] * nb + [o_spec] * nx + [_HBM] * nc,
                  out_specs=[o_spec] * no + [_HBM] * nc, out_shape=out_shape + plan_cls.out_shapes(carried),
                  scratch_shapes=plan_cls.sems(nc), compiler_params=_params(("arbitrary", "arbitrary")))(a, *bs, *extras, *carried)


def _swiglu(products, _):
    g, u = products
    return g, u, g * _sigmoid(g) * u


def _swiglu_bwd(products, extras):
    (da,), (gate, up) = products, extras
    gv = gate.astype(F32)
    s = _sigmoid(gv)
    return da * 0.5 * up.astype(F32) * s * (1.0 + gv * (1.0 - s)), da * 0.5 * gv * s


def _row_spec(tr, c):
    return pl.BlockSpec((tr, c), lambda i: (i, 0))


def _full_spec(shape):
    return pl.BlockSpec(shape, lambda i: (0,) * len(shape))


def _acc_rows(ref, val, i):
    @pl.when(i == 0)
    def _():
        ref[...] = val

    @pl.when(i > 0)
    def _():
        ref[...] += val


def rms_fwd(x, g, *, name):
    T, D = x.shape
    tr = _tile(T, 512, 8)

    def body(x_ref, g_ref, h_ref):
        xv = x_ref[...]
        r = lax.rsqrt(jnp.mean(xv * xv, axis=-1, keepdims=True) + RMS_EPS)
        h_ref[...] = (xv * r * g_ref[...]).astype(BF16)

    return _pcall(body, name=name, grid=(T // tr,), in_specs=[_row_spec(tr, D), _full_spec((1, D))],
                  out_specs=_row_spec(tr, D), out_shape=jax.ShapeDtypeStruct((T, D), BF16),
                  compiler_params=_params(("parallel",)))(x, g)


def final_loss(x, tgt, g, *, name):
    T, D = x.shape
    tr = _tile(T, 256, 8)

    def body(x_ref, t_ref, g_ref, dx_ref, dg_ref, loss_ref):
        i = pl.program_id(0)
        xv = x_ref[...]
        r = lax.rsqrt(jnp.mean(xv * xv, axis=-1, keepdims=True) + RMS_EPS)
        xh = xv * r
        e = xh * g_ref[...] - t_ref[...]
        part = 0.5 * jnp.sum(jnp.mean(e * e, axis=-1, keepdims=True), axis=0, keepdims=True)
        dy = e * (1.0 / D)
        dxh = dy * g_ref[...]
        dx_ref[...] = r * (dxh - xh * jnp.mean(dxh * xh, axis=-1, keepdims=True))
        _acc_rows(dg_ref, jnp.sum(dy * xh, axis=0, keepdims=True), i)
        _acc_rows(loss_ref, jnp.broadcast_to(part, (1, LANE)), i)

    return _pcall(body, name=name, grid=(T // tr,),
                  in_specs=[_row_spec(tr, D), _row_spec(tr, D), _full_spec((1, D))],
                  out_specs=[_row_spec(tr, D), _full_spec((1, D)), _full_spec((1, LANE))],
                  out_shape=[jax.ShapeDtypeStruct((T, D), F32), jax.ShapeDtypeStruct((1, D), F32),
                             jax.ShapeDtypeStruct((1, LANE), F32)],
                  compiler_params=_params(("arbitrary",)))(x, tgt, g)


def merge_fwd(br, ba, pg, *, name):
    T, D = br.shape
    tr = _tile(T, 256, 8)

    def body(br_ref, ba_ref, pg_ref, o_ref):
        pgv = pg_ref[...]
        o_ref[...] = (_sigmoid(pgv[:, :D]) * br_ref[...] + _sigmoid(pgv[:, D:]) * ba_ref[...]).astype(BF16)

    return _pcall(body, name=name, grid=(T // tr,), in_specs=[_row_spec(tr, D), _row_spec(tr, D), _row_spec(tr, 2 * D)],
                  out_specs=_row_spec(tr, D), out_shape=jax.ShapeDtypeStruct((T, D), BF16),
                  compiler_params=_params(("parallel",)))(br, ba, pg)


def merge_bwd(dm, br, ba, pg, *, name):
    T, D = br.shape
    tr = _tile(T, 256, 8)

    def body(dm_ref, br_ref, ba_ref, pg_ref, dbr_ref, dba_ref, dpg_ref):
        pgv, dmv = pg_ref[...], dm_ref[...]
        sr, sa = _sigmoid(pgv[:, :D]), _sigmoid(pgv[:, D:])
        dbr_ref[...] = (dmv * sr).astype(BF16)
        dba_ref[...] = (dmv * sa).astype(BF16)
        dpg_ref[:, :D] = dmv * br_ref[...] * sr * (1.0 - sr)
        dpg_ref[:, D:] = dmv * ba_ref[...] * sa * (1.0 - sa)

    return _pcall(body, name=name, grid=(T // tr,),
                  in_specs=[_row_spec(tr, D), _row_spec(tr, D), _row_spec(tr, D), _row_spec(tr, 2 * D)],
                  out_specs=[_row_spec(tr, D), _row_spec(tr, D), _row_spec(tr, 2 * D)],
                  out_shape=[jax.ShapeDtypeStruct((T, D), BF16), jax.ShapeDtypeStruct((T, D), BF16),
                             jax.ShapeDtypeStruct((T, 2 * D), F32)],
                  compiler_params=_params(("parallel",)))(dm, br, ba, pg)


def _rwkv_mix(p, prev, mu, w0, a0, k_k, k_a, wlw, wla, wlg, bd):
    pp = p + (prev - p) * mu
    r, k, v = pp[:, 0:RW], pp[:, RW:2 * RW], pp[:, 2 * RW:3 * RW]
    xw, xa, xg = pp[:, 3 * RW:3 * RW + 128], pp[:, 3 * RW + 128:3 * RW + 256], pp[:, 3 * RW + 256:3 * RW + 384]
    th = jnp.tanh(xw)
    z = -(w0 + _bdot(th, wlw))
    e = jnp.exp(-_softplus(z) - 0.5)
    decay = jnp.exp(-e)
    a = _sigmoid(a0 + _bdot(xa, wla))
    sg = _sigmoid(xg)
    kkr = k * k_k
    n = jnp.sqrt(_segsum(kkr * kkr, bd))
    kk = kkr / jnp.maximum(n, 1e-12)
    k2 = k * (1.0 + (a - 1.0) * k_a)
    return dict(r=r, k=k, v=v, xa=xa, th=th, z=z, e=e, decay=decay, a=a, sg=sg, n=n, kk=kk, k2=k2)


def _seg_matrix(n, shift):
    r = lax.shift_right_logical(lax.broadcasted_iota(jnp.int32, (n, n), 0), shift)
    c = lax.shift_right_logical(lax.broadcasted_iota(jnp.int32, (n, n), 1), shift)
    return jnp.where(r == c, 1.0, 0.0).astype(BF16)


def rwkv_pre_fwd(p, pshift, mu, w0, a0, k_k, k_a, wlw, wla, wlg, *, name):
    T = p.shape[0]
    tr = _tile(T, 256, 8)

    def body(p_ref, ps_ref, mu_ref, w0_ref, a0_ref, kk_ref, ka_ref, wlw_ref, wla_ref, wlg_ref,
             r_ref, w_ref, k_ref, v_ref, a_ref, b_ref, g_ref):
        pv, prev = p_ref[...], ps_ref[...]
        m = _rwkv_mix(pv, prev, mu_ref[...], w0_ref[...], a0_ref[...], kk_ref[...], ka_ref[...],
                      wlw_ref[...], wla_ref[...], wlg_ref[...], _seg_matrix(RW, 6))
        r_ref[...] = m['r']
        w_ref[...] = m['decay']
        k_ref[...] = m['k2']
        v_ref[...] = m['v']
        a_ref[...] = -m['kk']
        b_ref[...] = m['kk'] * m['a']
        g_ref[...] = m['sg']

    vec = _row_spec(tr, RW)
    return _pcall(
        body, name=name, grid=(T // tr,),
        in_specs=[_row_spec(tr, RWKV_PAD), _row_spec(tr, RWKV_PAD), _full_spec((1, RWKV_PAD))] + [_full_spec((1, RW))] * 4
        + [_full_spec((128, RW))] * 3,
        out_specs=[vec] * 6 + [_row_spec(tr, 128)],
        out_shape=[jax.ShapeDtypeStruct((T, RW), F32)] * 6 + [jax.ShapeDtypeStruct((T, 128), F32)],
        compiler_params=_params(("parallel",)),
    )(p, pshift, mu, w0, a0, k_k, k_a, wlw, wla, wlg)


def _group_norm(y, bd):
    mean = _segsum(y, bd) * (1.0 / HEAD_DIM)
    yc = y - mean
    rstd = lax.rsqrt(_segsum(yc * yc, bd) * (1.0 / HEAD_DIM) + GN_EPS)
    return yc * rstd, rstd


def rwkv_post_fwd(y, r, k2, v, sg, wlg, ln_w, ln_b, r_k, *, name):
    T = y.shape[0]
    tr = _tile(T, 256, 8)

    def body(y_ref, r_ref, k_ref, v_ref, sg_ref, wlg_ref, lw_ref, lb_ref, rk_ref, o_ref):
        bd = _seg_matrix(RW, 6)
        yn, _ = _group_norm(y_ref[...], bd)
        s = _segsum(r_ref[...] * k_ref[...] * rk_ref[...], bd)
        g = _bdot(sg_ref[...], wlg_ref[...])
        o_ref[...] = ((yn * lw_ref[...] + lb_ref[...] + s * v_ref[...]) * g).astype(BF16)

    vec = _row_spec(tr, RW)
    return _pcall(body, name=name, grid=(T // tr,),
                  in_specs=[vec] * 4 + [_row_spec(tr, 128), _full_spec((128, RW))] + [_full_spec((1, RW))] * 3, out_specs=vec,
                  out_shape=jax.ShapeDtypeStruct((T, RW), BF16), compiler_params=_params(("parallel",)))(
                      y, r, k2, v, sg, wlg, ln_w, ln_b, r_k)


def rwkv_post_bwd(dyr, y, r, k2, v, sg, wlg, ln_w, ln_b, r_k, *, name, carry=None):
    T = y.shape[0]
    tr = _tile(T, 256, 8)
    nt = T // tr
    plan_cls, carried = carry if carry else (None, ())
    nc = len(carried)

    def body(*refs):
        dyr_ref, y_ref, r_ref, k_ref, v_ref, sg_ref, wlg_ref, lw_ref, lb_ref, rk_ref = refs[:10]
        dy_ref, dz_ref, dg_ref, dlw_ref, dlb_ref = refs[10 + nc:15 + nc]
        i = pl.program_id(0)
        if nc:
            plan = plan_cls(refs[10:10 + nc], refs[15 + nc:15 + 2 * nc], refs[15 + 2 * nc:])
            pl.when(i == 0)(plan.start)
        bd = _seg_matrix(RW, 6)
        yn, rstd = _group_norm(y_ref[...], bd)
        s = _segsum(r_ref[...] * k_ref[...] * rk_ref[...], bd)
        dyrv = dyr_ref[...]
        dg_ref[...] = dyrv * (yn * lw_ref[...] + lb_ref[...] + s * v_ref[...])
        dz = dyrv * _bdot(sg_ref[...], wlg_ref[...])
        dz_ref[...] = dz
        dyn = dz * lw_ref[...]
        inv = 1.0 / HEAD_DIM
        dy_ref[...] = rstd * (dyn - _segsum(dyn, bd) * inv - yn * (_segsum(dyn * yn, bd) * inv))
        _acc_rows(dlw_ref, jnp.sum(dz * yn, axis=0, keepdims=True), i)
        _acc_rows(dlb_ref, jnp.sum(dz, axis=0, keepdims=True), i)
        if nc:
            pl.when(i == nt - 1)(plan.finish)

    vec = _row_spec(tr, RW)
    one = _full_spec((1, RW))
    return _pcall(body, name=name, grid=(nt,),
                  in_specs=[vec] * 5 + [_row_spec(tr, 128), _full_spec((128, RW))] + [one] * 3 + [_HBM] * nc,
                  out_specs=[vec] * 3 + [one] * 2 + [_HBM] * nc,
                  out_shape=[jax.ShapeDtypeStruct((T, RW), F32)] * 3 + [jax.ShapeDtypeStruct((1, RW), F32)] * 2
                  + (plan_cls.out_shapes(carried) if nc else []),
                  scratch_shapes=plan_cls.sems(nc) if nc else [],
                  compiler_params=_params(("arbitrary",)))(dyr, y, r, k2, v, sg, wlg, ln_w, ln_b, r_k, *carried)


def rwkv_pre_bwd(p, pshift, dr_w, dw_w, dk_w, dv_w, da_w, db_w, dz, dg, mu, w0, a0, k_k, k_a, r_k, wlw, wla, wlg, *, name):
    T = p.shape[0]
    tr = _tile(T, 256, 8)
    n = T // tr

    def body(p_ref, ps_ref, dr_ref, dw_ref, dk_ref, dv_ref, da_ref, db_ref, dz_ref, dg_ref,
             mu_ref, w0_ref, a0_ref, kk_ref, ka_ref, rk_ref, wlw_ref, wla_ref, wlg_ref,
             dp_ref, dmu_ref, dw0_ref, da0_ref, dkk_ref, dka_ref, drk_ref, dwlw_ref, dwla_ref, dwlg_ref,
             carry, dpp, acc_w, acc_a, acc_g):
        i = pl.program_id(0)

        @pl.when(i == 0)
        def _():
            carry[...] = jnp.zeros_like(carry)

        pv, prev, mu = p_ref[...], ps_ref[...], mu_ref[...]
        bd = _seg_matrix(RW, 6)
        k_k, k_a, r_k = kk_ref[...], ka_ref[...], rk_ref[...]
        m = _rwkv_mix(pv, prev, mu, w0_ref[...], a0_ref[...], k_k, k_a, wlw_ref[...], wla_ref[...], wlg_ref[...], bd)
        r, k, v, a, kk, k2 = m['r'], m['k'], m['v'], m['a'], m['kk'], m['k2']
        dzv, dgv = dz_ref[...], dg_ref[...]
        s = _segsum(r * k2 * r_k, bd)
        ds = _segsum(dzv * v, bd)
        dr = dr_ref[...] + ds * k2 * r_k
        dk2 = dk_ref[...] + ds * r * r_k
        dv = dv_ref[...] + dzv * s
        dbv = db_ref[...]
        dkk = dbv * a - da_ref[...]
        da = dbv * kk + dk2 * k * k_a
        dk = dk2 * (1.0 + (a - 1.0) * k_a)
        nmax = jnp.maximum(m['n'], 1e-12)
        dkkr = jnp.where(m['n'] > 1e-12, dkk - kk * _segsum(dkk * kk, bd), dkk) / nmax
        dk = dk + dkkr * k_k
        dapre = da * a * (1.0 - a)
        dwpre = dw_ref[...] * m['decay'] * (-m['e']) * _sigmoid(m['z'])
        dth = _bdot(dwpre, wlw_ref[...], _NT)
        dxa = _bdot(dapre, wla_ref[...], _NT)
        dsg = _bdot(dgv, wlg_ref[...], _NT)
        dpp[:, 0:RW] = dr
        dpp[:, RW:2 * RW] = dk
        dpp[:, 2 * RW:3 * RW] = dv
        dpp[:, 3 * RW:3 * RW + 128] = dth * (1.0 - m['th'] * m['th'])
        dpp[:, 3 * RW + 128:3 * RW + 256] = dxa
        dpp[:, 3 * RW + 256:3 * RW + 384] = dsg * m['sg'] * (1.0 - m['sg'])
        d = dpp[...]
        zed = d * mu
        last = lax.broadcasted_iota(jnp.int32, pv.shape, 0) == tr - 1
        dp_ref[...] = d * (1.0 - mu) + jnp.where(last, carry[0:1, :], pltpu.roll(zed, tr - 1, 0))
        carry[...] = zed[0:8, :]

        def colsum(x):
            return jnp.sum(x, axis=0, keepdims=True)

        _acc_rows(dmu_ref, colsum(d * (prev - pv)), i)
        _acc_rows(dw0_ref, colsum(dwpre), i)
        _acc_rows(da0_ref, colsum(dapre), i)
        _acc_rows(dkk_ref, colsum(dkkr * k), i)
        _acc_rows(dka_ref, colsum(dk2 * k * (a - 1.0)), i)
        _acc_rows(drk_ref, colsum(ds * r * k2), i)
        _acc_rows(acc_w, _bdot(m['th'], dwpre, _TN), i)
        _acc_rows(acc_a, _bdot(m['xa'], dapre, _TN), i)
        _acc_rows(acc_g, _bdot(m['sg'], dgv, _TN), i)

        @pl.when(i == n - 1)
        def _():
            dwlw_ref[...] = acc_w[...]
            dwla_ref[...] = acc_a[...]
            dwlg_ref[...] = acc_g[...]

    rev = lambda c: pl.BlockSpec((tr, c), lambda i: (n - 1 - i, 0))
    one, lora = _full_spec((1, RW)), _full_spec((128, RW))
    return _pcall(
        body, name=name, grid=(n,),
        in_specs=[rev(RWKV_PAD), rev(RWKV_PAD)] + [rev(RW)] * 8 + [_full_spec((1, RWKV_PAD))] + [one] * 5 + [lora] * 3,
        out_specs=[rev(RWKV_PAD), _full_spec((1, RWKV_PAD))] + [one] * 5 + [lora] * 3,
        out_shape=[jax.ShapeDtypeStruct((T, RWKV_PAD), F32), jax.ShapeDtypeStruct((1, RWKV_PAD), F32)]
        + [jax.ShapeDtypeStruct((1, RW), F32)] * 5 + [jax.ShapeDtypeStruct((128, RW), F32)] * 3,
        scratch_shapes=[pltpu.VMEM((8, RWKV_PAD), F32), pltpu.VMEM((tr, RWKV_PAD), F32)] + [pltpu.VMEM((128, RW), F32)] * 3,
        compiler_params=_params(("arbitrary",)),
    )(p, pshift, dr_w, dw_w, dk_w, dv_w, da_w, db_w, dz, dg, mu, w0, a0, k_k, k_a, r_k, wlw, wla, wlg)


def _qk_norm(x, g, bd):
    r = lax.rsqrt(_segsum(x * x, bd) * (1.0 / HEAD_DIM) + RMS_EPS)
    return x * r * g, r


def _att_mask(i):
    qi = lax.broadcasted_iota(jnp.int32, (BLOCK, 2 * BLOCK), 0)
    kj = lax.broadcasted_iota(jnp.int32, (BLOCK, 2 * BLOCK), 1)
    band = (kj <= qi + BLOCK) & (kj > qi + BLOCK - WINDOW)
    return band & ((kj >= BLOCK) | (i > 0))


_HQK = (((2,), (2,)), ((0,), (0,)))
_HPV = (((2,), (1,)), ((0,), (0,)))
_HTN = (((1,), (1,)), ((0,), (0,)))


def _heads(x, n):
    return jnp.stack([x[:, h * HEAD_DIM:(h + 1) * HEAD_DIM] for h in range(n)])


def _unheads(x3):
    return jnp.concatenate([x3[h] for h in range(x3.shape[0])], axis=1)


def _kv_heads(x):
    x2 = _heads(x, KVW // HEAD_DIM)
    return jnp.concatenate([x2[g:g + 1] for g in range(KVW // HEAD_DIM) for _ in range(ATT_GROUP)], axis=0)


def _sinks3(sk):
    return jnp.stack([sk[0:1, h:h + 1] for h in range(N_HEADS)])


def _att_probs(q3, k3, mask, sink):
    s = _bdot(q3, k3, _HQK) * (HEAD_DIM ** -0.5)
    s = jnp.where(mask[None], s, NEG_BIG)
    m = jnp.maximum(jnp.max(s, axis=-1, keepdims=True), sink)
    pexp = jnp.exp(s - m)
    psink = jnp.exp(sink - m)
    inv = 1.0 / (jnp.sum(pexp, axis=-1, keepdims=True) + psink)
    return pexp * inv, psink * inv


ATT_SUB = 2


def _att_blocks(sub):
    cur = pl.BlockSpec((sub * BLOCK, ATT_COLS), lambda i: (i, 0))
    prev = pl.BlockSpec((BLOCK, ATT_COLS), lambda i: (jnp.maximum(i * sub - 1, 0), 0))
    return cur, prev


def _att_sub(cur_all, prev_first, j):
    cur = cur_all[j * BLOCK:(j + 1) * BLOCK]
    return cur, (prev_first if j == 0 else cur_all[(j - 1) * BLOCK:j * BLOCK])


def _att_qkv(cur, prev, qn_g, kn_g):
    bq, bk = _seg_matrix(RW, 6), _seg_matrix(KVW, 6)
    qn, rq = _qk_norm(cur[:, 0:RW], qn_g, bq)
    kcur, rkc = _qk_norm(cur[:, RW:RW + KVW], kn_g, bk)
    kprev, _ = _qk_norm(prev[:, RW:RW + KVW], kn_g, bk)
    kc = jnp.concatenate([kprev, kcur], axis=0)
    vc = jnp.concatenate([prev[:, RW + KVW:], cur[:, RW + KVW:]], axis=0)
    return qn, rq, kc, vc, rkc


def att_fwd(pa, qn_g, kn_g, sinks, *, name):
    T = pa.shape[0]
    sub = ATT_SUB if (T // BLOCK) % ATT_SUB == 0 else 1
    n = T // (sub * BLOCK)

    def body(cur_ref, prev_ref, qg_ref, kg_ref, sk_ref, o_ref):
        i = pl.program_id(0)
        cur_all, prev_first = cur_ref[...], prev_ref[...]
        for j in range(sub):
            cur, prev = _att_sub(cur_all, prev_first, j)
            qn, _, kc, vc, _ = _att_qkv(cur, prev, qg_ref[...], kg_ref[...])
            probs, _ = _att_probs(_heads(qn, N_HEADS), _kv_heads(kc), _att_mask(i * sub + j), _sinks3(sk_ref[...]))
            o_ref[j * BLOCK:(j + 1) * BLOCK, :] = _unheads(_bdot(probs, _kv_heads(vc), _HPV))

    cur, prev = _att_blocks(sub)
    return _pcall(body, name=name, grid=(n,),
                  in_specs=[cur, prev, _full_spec((1, RW)), _full_spec((1, KVW)), _full_spec((1, LANE))],
                  out_specs=pl.BlockSpec((sub * BLOCK, RW), lambda i: (i, 0)), out_shape=jax.ShapeDtypeStruct((T, RW), F32),
                  compiler_params=_params(("parallel",)))(pa, pa, qn_g, kn_g, sinks)


def att_bwd(pa, do, qn_g, kn_g, sinks, *, name):
    T = pa.shape[0]
    sub = ATT_SUB if (T // BLOCK) % ATT_SUB == 0 else 1
    n = T // (sub * BLOCK)

    def one_block(cur, prev, do, blk, qg_ref, kg_ref, sk_ref, rows, dq_ref, dko_ref, dkn_ref, dvo_ref, dvn_ref):
        qn, rq, kc, vc, _ = _att_qkv(cur, prev, qg_ref[...], kg_ref[...])
        q3, k3, v3, do3 = _heads(qn, N_HEADS), _kv_heads(kc), _kv_heads(vc), _heads(do, N_HEADS)
        probs, psink = _att_probs(q3, k3, _att_mask(blk), _sinks3(sk_ref[...]))
        dprobs = _bdot(do3, v3, _HQK)
        delta = jnp.sum(probs * dprobs, axis=-1, keepdims=True)
        ds = probs * (dprobs - delta) * (HEAD_DIM ** -0.5)
        dsink3 = -jnp.sum(psink * delta, axis=1, keepdims=True)
        lane = lax.broadcasted_iota(jnp.int32, (1, LANE), 1)
        dsink = jnp.zeros((1, LANE), F32)
        for h in range(N_HEADS):
            dsink = dsink + jnp.where(lane == h, dsink3[h], 0.0)
        dqn = _unheads(_bdot(ds, k3, _HPV))

        def per_kv_head(x3):
            groups = [sum(x3[g * ATT_GROUP + j] for j in range(ATT_GROUP)) for g in range(KVW // HEAD_DIM)]
            return jnp.concatenate(groups, axis=1)

        dk, dv = per_kv_head(_bdot(ds, q3, _HTN)), per_kv_head(_bdot(probs, do3, _HTN))
        dkn_ref[rows, :], dko_ref[rows, :] = dk[0:BLOCK], dk[BLOCK:]
        dvn_ref[rows, :], dvo_ref[rows, :] = dv[0:BLOCK], dv[BLOCK:]
        qhat = cur[:, 0:RW] * rq
        dqh = dqn * qg_ref[...]
        dq_ref[rows, :] = rq * (dqh - qhat * (_segsum(dqh * qhat, _seg_matrix(RW, 6)) * (1.0 / HEAD_DIM)))
        prod = dqn * qhat
        fold = prod[:, 0:HEAD_DIM]
        for h in range(1, N_HEADS):
            fold = fold + prod[:, h * HEAD_DIM:(h + 1) * HEAD_DIM]
        return jnp.sum(fold, axis=0, keepdims=True), dsink

    def body(cur_ref, prev_ref, do_ref, qg_ref, kg_ref, sk_ref,
             dq_ref, dko_ref, dkn_ref, dvo_ref, dvn_ref, dqg_ref, dsk_ref):
        i = pl.program_id(0)
        cur_all, prev_first, do_all = cur_ref[...], prev_ref[...], do_ref[...]
        dqg, dsk = None, None
        for j in range(sub):
            cur, prev = _att_sub(cur_all, prev_first, j)
            rows = slice(j * BLOCK, (j + 1) * BLOCK)
            g, s = one_block(cur, prev, do_all[rows], i * sub + j, qg_ref, kg_ref, sk_ref, rows,
                             dq_ref, dko_ref, dkn_ref, dvo_ref, dvn_ref)
            dqg, dsk = (g, s) if dqg is None else (dqg + g, dsk + s)
        _acc_rows(dqg_ref, dqg, i)
        _acc_rows(dsk_ref, dsk, i)

    cur, prev = _att_blocks(sub)
    kvb = pl.BlockSpec((sub * BLOCK, KVW), lambda i: (i, 0))
    qb = pl.BlockSpec((sub * BLOCK, RW), lambda i: (i, 0))
    return _pcall(body, name=name, grid=(n,),
                  in_specs=[cur, prev, qb, _full_spec((1, RW)), _full_spec((1, KVW)), _full_spec((1, LANE))],
                  out_specs=[qb, kvb, kvb, kvb, kvb, _full_spec((1, HEAD_DIM)), _full_spec((1, LANE))],
                  out_shape=[jax.ShapeDtypeStruct((T, RW), F32)] + [jax.ShapeDtypeStruct((T, KVW), F32)] * 4
                  + [jax.ShapeDtypeStruct((1, HEAD_DIM), F32), jax.ShapeDtypeStruct((1, LANE), F32)],
                  compiler_params=_params(("arbitrary",)))(pa, pa, do, qn_g, kn_g, sinks)


def att_kv_bwd(pa, dq, dko, dkn, dvo, dvn, kn_g, *, name):
    T = pa.shape[0]
    n = T // BLOCK

    def body(pa_ref, dq_ref, dko_ref, dkn_ref, dvo_ref, dvn_ref, kg_ref, dpa_ref, dkg_ref):
        i = pl.program_id(0)
        more = i < n - 1
        dkn_tot = dko_ref[...] + jnp.where(more, dkn_ref[...], 0.0)
        dv_tot = dvo_ref[...] + jnp.where(more, dvn_ref[...], 0.0)
        kraw = pa_ref[:, RW:RW + KVW]
        bk = _seg_matrix(KVW, 6)
        _, rk = _qk_norm(kraw, kg_ref[...], bk)
        khat = kraw * rk
        dkh = dkn_tot * kg_ref[...]
        dpa_ref[:, 0:RW] = dq_ref[...]
        dpa_ref[:, RW:RW + KVW] = rk * (dkh - khat * (_segsum(dkh * khat, bk) * (1.0 / HEAD_DIM)))
        dpa_ref[:, RW + KVW:] = dv_tot
        prod = dkn_tot * khat
        _acc_rows(dkg_ref, jnp.sum(prod[:, 0:HEAD_DIM] + prod[:, HEAD_DIM:], axis=0, keepdims=True), i)

    kvb = pl.BlockSpec((BLOCK, KVW), lambda i: (i, 0))
    nxt = pl.BlockSpec((BLOCK, KVW), lambda i: (jnp.minimum(i + 1, n - 1), 0))
    return _pcall(body, name=name, grid=(n,),
                  in_specs=[pl.BlockSpec((BLOCK, ATT_COLS), lambda i: (i, 0)), pl.BlockSpec((BLOCK, RW), lambda i: (i, 0)),
                            kvb, nxt, kvb, nxt, _full_spec((1, KVW))],
                  out_specs=[pl.BlockSpec((BLOCK, ATT_COLS), lambda i: (i, 0)), _full_spec((1, HEAD_DIM))],
                  out_shape=[jax.ShapeDtypeStruct((T, ATT_COLS), F32), jax.ShapeDtypeStruct((1, HEAD_DIM), F32)],
                  compiler_params=_params(("arbitrary",)))(pa, dq, dko, dkn, dvo, dvn, kn_g)


WKV_CHUNK = 128
WKV_GROUP = 8


def _diag_mask():
    i = lax.broadcasted_iota(jnp.int32, (HEAD_DIM, RW), 0)
    j = lax.broadcasted_iota(jnp.int32, (HEAD_DIM, RW), 1) & (HEAD_DIM - 1)
    return i == j


def _heads_matrix():
    head = jnp.arange(RW // 2) // HEAD_DIM
    bd = (head[:, None] == head[None, :]).astype(BF16)
    return jnp.concatenate([bd, bd], axis=0)


def _headsums(xs, pieces, bd2):
    half = RW // 2
    bd = bd2[:pieces * half]
    rows = []
    for x in xs:
        parts, rest = [], x
        for n in range(pieces):
            p = rest.astype(BF16)
            parts.append(p)
            if n + 1 < pieces:
                rest = rest - p.astype(F32)
        for sl in (slice(0, half), slice(half, RW)):
            rows.append(jnp.concatenate([p[:, sl] for p in parts], axis=1))
    out = lax.dot_general(jnp.concatenate(rows, axis=0), bd, (((1,), (0,)), ((), ())), preferred_element_type=F32)
    return [jnp.concatenate([out[2 * n * HEAD_DIM:(2 * n + 1) * HEAD_DIM], out[(2 * n + 1) * HEAD_DIM:(2 * n + 2) * HEAD_DIM]],
                            axis=1) for n in range(len(xs))]


def _headsum(x):
    low = lax.broadcasted_iota(jnp.int32, (HEAD_DIM, LANE), 1) < HEAD_DIM
    tiles = []
    for c in range(RW // LANE):
        xt = x[:, c * LANE:(c + 1) * LANE]
        s_lo = jnp.sum(jnp.where(low, xt, 0.0), axis=1, keepdims=True)
        s_hi = jnp.sum(jnp.where(low, 0.0, xt), axis=1, keepdims=True)
        tiles.append(jnp.where(low, s_lo, s_hi))
    return jnp.concatenate(tiles, axis=1)


def _cols(rows, diag, bd2, pieces=2):
    return _headsums([jnp.where(diag, r, 0.0) for r in rows], pieces, bd2)


def _row(x, diag):
    return jnp.sum(jnp.where(diag, x, 0.0), axis=0, keepdims=True)


def wkv_fwd(r, w, k, v, a, b, *, name, gather=()):
    T = r.shape[0]
    ch = min(WKV_CHUNK, T)
    ngroups = ch // WKV_GROUP
    nchunks = T // ch
    ng = len(gather)

    def body(*refs):
        r_ref, w_ref, k_ref, v_ref, a_ref, b_ref, bd_ref = refs[:7]
        y_ref, st_ref = refs[7 + ng:9 + ng]
        s_scr = refs[9 + 2 * ng]
        step = pl.program_id(0)
        if ng:
            plan = _GatherPlan(refs[7:7 + ng], refs[9 + ng:9 + 2 * ng], refs[10 + 2 * ng:])
            pl.when(step == 0)(plan.start)
            pl.when(step == nchunks // 2)(plan.relay)

        @pl.when(step == 0)
        def _():
            s_scr[...] = jnp.zeros_like(s_scr)

        diag, bd2 = _diag_mask(), bd_ref[...]

        def group(gi, S):
            t0 = pl.multiple_of(gi * WKV_GROUP, WKV_GROUP)
            rows = pl.ds(t0, WKV_GROUP)
            R, W, K, V, A, B = (ref[rows, :] for ref in (r_ref, w_ref, k_ref, v_ref, a_ref, b_ref))
            vcols = _cols([V[s:s + 1] for s in range(WKV_GROUP)], diag, bd2, 1)
            yrows = []
            for s in range(WKV_GROUP):
                sa = _headsum(S * A[s:s + 1])
                S = S * W[s:s + 1] + sa * B[s:s + 1] + vcols[s] * K[s:s + 1]
                st_ref[t0 + s] = S
                yrows.append(_row(_headsums([S * R[s:s + 1]], 1, bd2)[0], diag))
            y_ref[rows, :] = jnp.concatenate(yrows, axis=0)
            return S

        s_scr[...] = lax.fori_loop(0, ngroups, group, s_scr[...])
        if ng:
            pl.when(step == nchunks - 1)(plan.finish_relayed)

    vec = pl.BlockSpec((ch, RW), lambda c: (c, 0))
    return _pcall(
        body, name=name, grid=(nchunks,), in_specs=[vec] * 6 + [_full_spec((RW, RW // 2))] + [_HBM] * ng,
        out_specs=[vec, pl.BlockSpec((ch, HEAD_DIM, RW), lambda c: (c, 0, 0))] + [_HBM] * ng,
        out_shape=[jax.ShapeDtypeStruct((T, RW), F32), jax.ShapeDtypeStruct((T, HEAD_DIM, RW), F32)] + _gathered_shapes(gather),
        scratch_shapes=[pltpu.VMEM((HEAD_DIM, RW), F32)] + (_GatherPlan.sems(ng) if ng else []),
        compiler_params=_params(("arbitrary",)),
    )(r, w, k, v, a, b, _heads_matrix(), *gather)


def wkv_bwd(r, w, k, v, a, b, dy, states, *, name, exchange=()):
    T = r.shape[0]
    ch = min(WKV_CHUNK, T)
    nchunks = T // ch
    ngroups = ch // WKV_GROUP
    ne = len(exchange)

    def body(*refs):
        r_ref, w_ref, k_ref, v_ref, a_ref, b_ref, dy_ref, st_ref, stp_ref, bd_ref = refs[:10]
        dr_ref, dw_ref, dk_ref, dv_ref, da_ref, db_ref = refs[10 + ne:16 + ne]
        ds_scr = refs[16 + 2 * ne]
        step = pl.program_id(0)
        if ne:
            plan = _ExchangePlan(refs[10:10 + ne], refs[16 + ne:16 + 2 * ne], refs[17 + 2 * ne:])
            pl.when(step == 0)(plan.start)

        @pl.when(step == 0)
        def _():
            ds_scr[...] = jnp.zeros_like(ds_scr)

        has_prev_chunk = step < nchunks - 1
        diag, bd2 = _diag_mask(), bd_ref[...]
        colsum = lambda x: jnp.sum(x, axis=0, keepdims=True)

        def group(gj, dS):
            gi = ngroups - 1 - gj
            t0 = pl.multiple_of(gi * WKV_GROUP, WKV_GROUP)
            rows = pl.ds(t0, WKV_GROUP)
            R, W, K, V, A, B, DY = (ref[rows, :] for ref in (r_ref, w_ref, k_ref, v_ref, a_ref, b_ref, dy_ref))
            before = jnp.where(gi > 0, st_ref[jnp.maximum(t0 - 1, 0)], jnp.where(has_prev_chunk, stp_ref[0], 0.0))
            prev_state = lambda s: st_ref[t0 + s - 1] if s > 0 else before
            steps = range(WKV_GROUP)
            dycols = _cols([DY[s:s + 1] for s in steps], diag, bd2, 1)
            vcols = _cols([V[s:s + 1] for s in steps], diag, bd2, 1)
            sas = _headsums([prev_state(s) * A[s:s + 1] for s in steps], 1, bd2)
            got = [[None] * WKV_GROUP for _ in range(6)]
            for s in reversed(steps):
                Sp = prev_state(s)
                dS = dS + dycols[s] * R[s:s + 1]
                got[0][s] = colsum(st_ref[t0 + s] * dycols[s])
                got[3][s] = _row(_headsums([dS * K[s:s + 1]], 1, bd2)[0], diag)
                got[2][s] = colsum(dS * vcols[s])
                dsa = _headsum(dS * B[s:s + 1])
                got[5][s] = colsum(dS * sas[s])
                got[1][s] = colsum(dS * Sp)
                got[4][s] = colsum(Sp * dsa)
                dS = dS * W[s:s + 1] + dsa * A[s:s + 1]
            for q, ref in enumerate((dr_ref, dw_ref, dk_ref, dv_ref, da_ref, db_ref)):
                ref[rows, :] = jnp.concatenate(got[q], axis=0)
            return dS

        ds_scr[...] = lax.fori_loop(0, ngroups, group, ds_scr[...])
        if ne:
            pl.when(step == nchunks - 1)(plan.finish)

    vec = pl.BlockSpec((ch, RW), lambda c: (nchunks - 1 - c, 0))
    st_spec = pl.BlockSpec((ch, HEAD_DIM, RW), lambda c: (nchunks - 1 - c, 0, 0))
    stp_spec = pl.BlockSpec((1, HEAD_DIM, RW), lambda c: (jnp.maximum((nchunks - 1 - c) * ch - 1, 0), 0, 0))
    return _pcall(
        body, name=name, grid=(nchunks,), in_specs=[vec] * 7 + [st_spec, stp_spec, _full_spec((RW, RW // 2))] + [_HBM] * ne,
        out_specs=[vec] * 6 + [_HBM] * ne,
        out_shape=[jax.ShapeDtypeStruct((T, RW), F32)] * 6 + [jax.ShapeDtypeStruct(e.shape, e.dtype) for e in exchange],
        scratch_shapes=[pltpu.VMEM((HEAD_DIM, RW), F32)] + (_ExchangePlan.sems(ne) if ne else []),
        compiler_params=_params(("arbitrary",)),
    )(r, w, k, v, a, b, dy, states, states, _heads_matrix(), *exchange)


_HBM = pl.BlockSpec(memory_space=pltpu.HBM)
_MESH = pl.DeviceIdType.MESH


def _place():
    x, y, c = lax.axis_index("x"), lax.axis_index("y"), lax.axis_index("c")
    return x, y, c, [(1 - x, y), (x, 1 - y), (1 - x, 1 - y)]


def _remote(src, dst, send_sem, recv_sem, to):
    return pltpu.make_async_remote_copy(src_ref=src, dst_ref=dst, send_sem=send_sem, recv_sem=recv_sem, device_id=to,
                                        device_id_type=_MESH)


def _dma_sems(*counts):
    return [pltpu.SemaphoreType.DMA((n,)) for n in counts]


class _GatherPlan:
    def __init__(self, ins, outs, sems):
        self.ins, self.outs, self.n = ins, outs, len(ins)
        self.ici_send, self.ici_recv, self.d2d_send, self.d2d_recv, self.local_sems = sems
        x, y, c, chips = _place()
        self.c, self.me, self.sibling = c, 2 * x + y, (x, y, 1 - c)
        self.peers = [(2 * qx + qy, (qx, qy, c)) for qx, qy in chips]

    @staticmethod
    def sems(n):
        return _dma_sems(3 * n, 3 * n, 3 * n, 3 * n, n)

    def _half(self, i, which):
        rh = self.ins[i].shape[0] // 2
        return pl.ds(which * rh, rh)

    def _local(self, i):
        return pltpu.make_async_copy(self.ins[i], self.outs[i].at[self.me], self.local_sems.at[i])

    def _send(self, i, j):
        k, mine = 3 * i + j, self._half(i, self.c)
        return _remote(self.ins[i].at[mine], self.outs[i].at[self.me, mine], self.ici_send.at[k], self.ici_recv.at[k],
                       self.peers[j][1])

    def _landed(self, i, j):
        k, piece = 3 * i + j, self.outs[i].at[self.peers[j][0], self._half(i, self.c)]
        return _remote(piece, piece, self.ici_send.at[k], self.ici_recv.at[k], self.peers[j][1])

    def _pass(self, i, j, which):
        k, piece = 3 * i + j, self.outs[i].at[self.peers[j][0], self._half(i, which)]
        return _remote(piece, piece, self.d2d_send.at[k], self.d2d_recv.at[k], self.sibling)

    def _all(self):
        return [(i, j) for i in range(self.n) for j in range(3)]

    def start(self):
        for i in range(self.n):
            self._local(i).start()
        for i, j in self._all():
            self._send(i, j).start()

    def relay(self):
        for i, j in self._all():
            self._landed(i, j).wait_recv()
            self._pass(i, j, self.c).start()

    def finish_relayed(self):
        for i, j in self._all():
            self._pass(i, j, 1 - self.c).wait_recv()
        for i, j in self._all():
            self._send(i, j).wait_send()
            self._pass(i, j, self.c).wait_send()
        for i in range(self.n):
            self._local(i).wait()

    def finish(self):
        self.relay()
        self.finish_relayed()

    @staticmethod
    def out_shapes(shards):
        return _gathered_shapes(shards)


def _gathered_shapes(shards):
    return [jax.ShapeDtypeStruct((N_CHIPS,) + s.shape, s.dtype) for s in shards]


def gather_weights(shards, *, name):
    n = len(shards)

    def body(*refs):
        plan = _GatherPlan(refs[:n], refs[n:2 * n], refs[2 * n:])
        plan.start()
        plan.finish()

    return _pcall(body, name=name, in_specs=[_HBM] * n, out_specs=[_HBM] * n, out_shape=_gathered_shapes(shards),
                  scratch_shapes=_GatherPlan.sems(n), compiler_params=_params())(*shards)


class _SiblingPlan:
    halves = True

    def __init__(self, ins, outs, sems):
        self.ins, self.outs, self.n = ins, outs, len(ins)
        self.send_sems, self.recv_sems = sems
        x, y, c, _ = _place()
        self.c, self.sibling = c, (x, y, 1 - c)

    @staticmethod
    def sems(n):
        return _dma_sems(n, n)

    @classmethod
    def out_shapes(cls, arrays):
        if not cls.halves:
            return [jax.ShapeDtypeStruct(a.shape, a.dtype) for a in arrays]
        return [jax.ShapeDtypeStruct((a.shape[0], a.shape[1] // 2, a.shape[2]), a.dtype) for a in arrays]

    def _copy(self, i):
        src = self.ins[i]
        if self.halves:
            rh = src.shape[1] // 2
            src = src.at[:, pl.ds((1 - self.c) * rh, rh)]
        return _remote(src, self.outs[i], self.send_sems.at[i], self.recv_sems.at[i], self.sibling)

    def start(self):
        for i in range(self.n):
            self._copy(i).start()

    def finish(self):
        for i in range(self.n):
            self._copy(i).wait_recv()
        for i in range(self.n):
            self._copy(i).wait_send()


class _SiblingWhole(_SiblingPlan):
    halves = False


def to_sibling(arrays, take_other_half, *, name):
    n = len(arrays)
    plan_cls = _SiblingPlan if take_other_half else _SiblingWhole

    def body(*refs):
        plan = plan_cls(refs[:n], refs[n:2 * n], refs[2 * n:])
        plan.start()
        plan.finish()

    return _pcall(body, name=name, in_specs=[_HBM] * n, out_specs=[_HBM] * n, out_shape=plan_cls.out_shapes(arrays),
                  scratch_shapes=plan_cls.sems(n), compiler_params=_params())(*arrays)


def exchange_chips(arrays, *, name):
    n = len(arrays)

    def body(*refs):
        plan = _ExchangePlan(refs[:n], refs[n:2 * n], refs[2 * n:])
        plan.start()
        plan.finish()

    return _pcall(body, name=name, in_specs=[_HBM] * n, out_specs=[_HBM] * n,
                  out_shape=[jax.ShapeDtypeStruct(a.shape, a.dtype) for a in arrays],
                  scratch_shapes=_ExchangePlan.sems(n), compiler_params=_params())(*arrays)


class _ExchangePlan:
    def __init__(self, ins, outs, sems):
        self.ins, self.outs, self.n = ins, outs, len(ins)
        self.send_sems, self.recv_sems, self.local_sems = sems
        x, y, c, chips = _place()
        self.me = 2 * x + y
        self.peers = [(2 * qx + qy, (qx, qy, c)) for qx, qy in chips]

    @staticmethod
    def sems(n):
        return _dma_sems(3 * n, 3 * n, n)

    @staticmethod
    def out_shapes(arrays):
        return [jax.ShapeDtypeStruct(a.shape, a.dtype) for a in arrays]

    def _local(self, i):
        return pltpu.make_async_copy(self.ins[i].at[self.me], self.outs[i].at[self.me], self.local_sems.at[i])

    def _send(self, i, j):
        k = 3 * i + j
        return _remote(self.ins[i].at[self.peers[j][0]], self.outs[i].at[self.me], self.send_sems.at[k], self.recv_sems.at[k],
                       self.peers[j][1])

    def _landed(self, i, j):
        k, piece = 3 * i + j, self.outs[i].at[self.peers[j][0]]
        return _remote(piece, piece, self.send_sems.at[k], self.recv_sems.at[k], self.peers[j][1])

    def start(self):
        for i in range(self.n):
            self._local(i).start()
            for j in range(3):
                self._send(i, j).start()

    def finish(self):
        for i in range(self.n):
            for j in range(3):
                self._landed(i, j).wait_recv()
        for i in range(self.n):
            for j in range(3):
                self._send(i, j).wait_send()
            self._local(i).wait()


def _core_index():
    return lax.axis_index("c").astype(jnp.int32).reshape(1)


def pair_sum(g, theirs, wire_dtype, *, name):
    _, R, C = g.shape
    rh = R // 2
    tr = _tile(rh, 256, 16)
    nt = rh // tr

    def body(c_ref, g_ref, t_ref, q_ref, qw_ref):
        q = g_ref[...] + t_ref[...]
        q_ref[...] = q
        qw_ref[...] = q.astype(wire_dtype)

    blk = pl.BlockSpec((1, tr, C), lambda b, i, c_ref: (b, i, 0))
    mine = pl.BlockSpec((1, tr, C), lambda b, i, c_ref: (b, c_ref[0] * nt + i, 0))
    grid_spec = pltpu.PrefetchScalarGridSpec(num_scalar_prefetch=1, grid=(N_CHIPS, nt), in_specs=[mine, blk], out_specs=[blk, blk])
    return _pcall(body, name=name, grid_spec=grid_spec,
                  out_shape=[jax.ShapeDtypeStruct((N_CHIPS, rh, C), F32), jax.ShapeDtypeStruct((N_CHIPS, rh, C), wire_dtype)],
                  compiler_params=_params(("parallel", "parallel")))(_core_index(), g, theirs)


def half_sum(own, landed, *, name):
    _, rh, C = own.shape
    tr = _tile(rh, 256, 16)

    def body(me_ref, own_ref, land_ref, o_ref):
        total = None
        for p in range(N_CHIPS):
            term = jnp.where(me_ref[0] == p, own_ref[p], land_ref[p].astype(F32))
            total = term if total is None else total + term
        o_ref[...] = total

    blk = pl.BlockSpec((N_CHIPS, tr, C), lambda i, me_ref: (0, i, 0))
    grid_spec = pltpu.PrefetchScalarGridSpec(num_scalar_prefetch=1, grid=(rh // tr,), in_specs=[blk, blk],
                                             out_specs=pl.BlockSpec((tr, C), lambda i, me_ref: (i, 0)))
    me = (2 * lax.axis_index("x") + lax.axis_index("y")).astype(jnp.int32).reshape(1)
    return _pcall(body, name=name, grid_spec=grid_spec, out_shape=jax.ShapeDtypeStruct((rh, C), F32),
                  compiler_params=_params(("parallel",)))(me, own, landed)


def adamw(w, m, v, mine, theirs, *, name):
    _, R, C = w.shape
    rh = R // 2
    tr = _tile(rh, 256, 8)
    nt = rh // tr

    def body(c_ref, w_ref, m_ref, v_ref, a_ref, b_ref, g_ref, d_ref, nm_ref, nv_ref):
        is_mine = (pl.program_id(0) // nt) == c_ref[0]
        g = jnp.where(is_mine, a_ref[...], b_ref[...])
        g_ref[...] = g
        nm = ADAM_B1 * m_ref[...] + (1.0 - ADAM_B1) * g
        nv = ADAM_B2 * v_ref[...] + (1.0 - ADAM_B2) * (g * g)
        nm_ref[...] = nm
        nv_ref[...] = nv
        m_hat = nm / (1.0 - ADAM_B1 ** ADAM_STEP)
        v_hat = nv / (1.0 - ADAM_B2 ** ADAM_STEP)
        d_ref[...] = -ADAM_LR * (m_hat / (jnp.sqrt(v_hat) + ADAM_EPS) + ADAM_WD * w_ref[...])

    full = pl.BlockSpec((None, tr, C), lambda i, c_ref: (0, i, 0))
    a_spec = pl.BlockSpec((tr, C), lambda i, c_ref: (jnp.clip(i - c_ref[0] * nt, 0, nt - 1), 0))
    b_spec = pl.BlockSpec((tr, C), lambda i, c_ref: (jnp.clip(i - (1 - c_ref[0]) * nt, 0, nt - 1), 0))
    grid_spec = pltpu.PrefetchScalarGridSpec(num_scalar_prefetch=1, grid=(2 * nt,), in_specs=[full] * 3 + [a_spec, b_spec],
                                             out_specs=[full] * 4)
    return _pcall(body, name=name, grid_spec=grid_spec, out_shape=[jax.ShapeDtypeStruct((1, R, C), F32)] * 4,
                  compiler_params=_params(("arbitrary",)))(_core_index(), w, m, v, mine, theirs)


def _to_blocks(full, axis):
    r, c = full.shape
    if axis == 1:
        return full.reshape(r, N_CHIPS, c // N_CHIPS).transpose(1, 0, 2)
    return full.reshape(N_CHIPS, r // N_CHIPS, c)


def _from_blocks(blocks, axis):
    _, r, c = blocks.shape
    if axis == 1:
        return blocks.transpose(1, 0, 2).reshape(r, N_CHIPS * c)
    return blocks.reshape(N_CHIPS * r, c)


def _ffn_fwd(x, norm, wg_t, wu_t, wd, tag):
    h = rms_fwd(x, norm, name=tag + "_norm")
    gate, up, act = mm_fused(h, [wg_t, wu_t], _swiglu, [BF16] * 3, tb=True, name=tag + "_gate_up")
    out = mm(act, wd, scale=0.5, res=x, name=tag + "_down")
    return out, (h, gate, up, act)


def _ffn_bwd(dout, x, saved, norm, wg_t, wu_t, wd, tag, carry=None, reduce=None):
    h, gate, up, act = saved
    dgate, dup, *carried = mm_fused(dout, [wd], _swiglu_bwd, [BF16] * 2, tb=True, extras=[gate, up], name=tag + "_dact",
                                    carry=carry)
    dwd = mm(act, dout, ta=True, scale=0.5, name=tag + "_dwd")
    if reduce:
        down_blocks = _to_blocks(dwd, 0)
        dwg_t, down_from_sibling = mm(dgate, h, ta=True, name=tag + "_dwg", carry=(_SiblingPlan, [down_blocks]))
        down_pair = pair_sum(down_blocks, down_from_sibling, BF16, name="pair_sum_" + reduce[2])
        dwu_t, down_landed = mm(dup, h, ta=True, name=tag + "_dwu", carry=(_ExchangePlan, [down_pair[1]]))
        blocks = [_to_blocks(g, 0) for g in (dwg_t, dwu_t)]
        dh, *from_sibling = mm(dgate, wg_t, name=tag + "_dh_gate", carry=(_SiblingPlan, blocks))
        pair = [pair_sum(g, t, BF16, name="pair_sum_" + n) for g, t, n in zip(blocks, from_sibling, reduce[:2])]
        dx, dnorm, *landed = mm(dup, wu_t, res=dh, name=tag + "_dh_up", norm_bwd=(x, norm, dout),
                                carry=(_ExchangePlan, [q for _, q in pair]))
        return dx, dnorm, dwg_t, dwu_t, dwd, carried, (pair + [down_pair], landed + [down_landed])
    dwg_t = mm(dgate, h, ta=True, name=tag + "_dwg")
    dwu_t = mm(dup, h, ta=True, name=tag + "_dwu")
    dh = mm(dgate, wg_t, name=tag + "_dh_gate")
    dx, dnorm = mm(dup, wu_t, res=dh, name=tag + "_dh_up", norm_bwd=(x, norm, dout))
    return dx, dnorm, dwg_t, dwu_t, dwd, carried, None


TRANSPOSED = ('ffn1_w_gate', 'ffn1_w_up', 'ffn2_w_gate', 'ffn2_w_up')
FIRST_WEIGHTS = ['ffn1_w_gate', 'ffn1_w_up', 'ffn1_w_down']
MID_WEIGHTS = ['w_in', 'rwkv_w_lora_up', 'rwkv_a_lora_up', 'rwkv_g_lora_up']
LATE_WEIGHTS = ['w_branch_rwkv', 'w_branch_attn', 'w_out', 'ffn2_w_gate', 'ffn2_w_up', 'ffn2_w_down']


def _pair_sums(names, blocks, tag):
    from_sibling = to_sibling(blocks, True, name=tag + "_grads_to_sibling")
    return [pair_sum(g, t, F32 if n == 'small' else BF16, name="pair_sum_" + n)
            for g, t, n in zip(blocks, from_sibling, names)]


def _step(A):
    x, tgt = A['x'][0], A['loss_target'][0]
    T = x.shape[0]
    w = {n: A[n][0] for n in WEIGHT_NAMES}
    row = lambda a: a.reshape(1, -1)

    axis_of = {n: (0 if n in TRANSPOSED else axis) for n, axis in BIG}
    natural = lambda n, a: jnp.swapaxes(a, 1, 2) if n in TRANSPOSED else a
    shard = lambda n: natural(n, A[n])[0].astype(BF16)
    n1, nmix, n2, nfin = (row(w[n]) for n in ('ffn1_norm', 'mix_norm', 'ffn2_norm', 'final_norm'))
    gathered = gather_weights([shard(n) for n in FIRST_WEIGHTS[:2]], name="gather_weights")
    full = {n: _from_blocks(b, axis_of[n]) for n, b in zip(FIRST_WEIGHTS[:2], gathered)}
    h1 = rms_fwd(x, n1, name="ffn1_norm")
    w_in_shard = shard('w_in')
    upper, lower = w_in_shard[:D_MODEL // 2], w_in_shard[D_MODEL // 2:]
    gate1, up1, act1, down_blocks, upper_blocks = mm_fused(
        h1, [full['ffn1_w_gate'], full['ffn1_w_up']], _swiglu, [BF16] * 3, tb=True, name="ffn1_gate_up",
        carry=(_GatherPlan, [shard('ffn1_w_down'), upper]))
    full['ffn1_w_down'] = _from_blocks(down_blocks, axis_of['ffn1_w_down'])
    x1, lower_blocks, *gathered = mm(act1, full['ffn1_w_down'], scale=0.5, res=x, name="ffn1_down",
                                     carry=(_GatherPlan, [lower] + [shard(n) for n in MID_WEIGHTS[1:]]))
    full['w_in'] = jnp.concatenate([_from_blocks(upper_blocks, 1), _from_blocks(lower_blocks, 1)], axis=0)
    full.update({n: _from_blocks(b, axis_of[n]) for n, b in zip(MID_WEIGHTS[1:], gathered)})
    ffn1 = (h1, gate1, up1, act1)
    w_in_r = _pad_rwkv_cols(full['w_in'][:, :RWKV_COLS])
    w_in_a = full['w_in'][:, RWKV_COLS:RWKV_COLS + ATT_COLS]
    w_in_g = full['w_in'][:, RWKV_COLS + ATT_COLS:]
    wlw, wla, wlg = (_pad_rows(full[n], 128).astype(F32) for n in ('rwkv_w_lora_up', 'rwkv_a_lora_up', 'rwkv_g_lora_up'))
    mu = _pad_rwkv_cols(row(w['rwkv_mu']))
    w0, a0, k_k, k_a, r_k, ln_w, ln_b = (row(w[n]) for n in ('rwkv_w0', 'rwkv_a0', 'rwkv_k_k', 'rwkv_k_a', 'rwkv_r_k',
                                                               'rwkv_ln_w', 'rwkv_ln_b'))
    qg = jnp.tile(row(w['attn_q_norm']), (1, N_HEADS))
    kg = jnp.tile(row(w['attn_k_norm']), (1, KVW // HEAD_DIM))
    sinks = jnp.pad(row(w['attn_sinks']), ((0, 0), (0, LANE - N_HEADS)))

    h2 = rms_fwd(x1, nmix, name="mix_norm")
    pr = mm(h2, w_in_r, name="proj_rwkv")
    pa = mm(h2, w_in_a, name="proj_att")
    pg = mm(h2, w_in_g, name="proj_gate")
    pr_shift = jnp.pad(pr, ((1, 0), (0, 0)))[:-1]
    r, dec, k2, v, a, b, sg = rwkv_pre_fwd(pr, pr_shift, mu, w0, a0, k_k, k_a, wlw, wla, wlg, name="rwkv_pre")
    y, states, *gathered = wkv_fwd(r, dec, k2, v, a, b, name="wkv_fwd", gather=[shard(n) for n in LATE_WEIGHTS])
    full.update({n: _from_blocks(b, axis_of[n]) for n, b in zip(LATE_WEIGHTS, gathered)})
    yr = rwkv_post_fwd(y, r, k2, v, sg, wlg, ln_w, ln_b, r_k, name="rwkv_post")
    ya = att_fwd(pa, qg, kg, sinks, name="att_fwd")
    br = mm(yr, full['w_branch_rwkv'], name="branch_rwkv")
    ba = mm(ya, full['w_branch_attn'], name="branch_att")
    mg = merge_fwd(br, ba, pg, name="merge")
    x2 = mm(mg, full['w_out'], res=x1, name="mix_out")
    x3, ffn2 = _ffn_fwd(x2, n2, full['ffn2_w_gate'], full['ffn2_w_up'], full['ffn2_w_down'], "ffn2")
    dx3, d_nfin, loss = final_loss(x3, tgt, nfin, name="final_loss")

    G = {'final_norm': d_nfin}
    dx2, G['ffn2_norm'], G['ffn2_w_gate'], G['ffn2_w_up'], G['ffn2_w_down'], _, _ = _ffn_bwd(
        dx3, x2, ffn2, n2, full['ffn2_w_gate'], full['ffn2_w_up'], full['ffn2_w_down'], "ffn2")
    dmg = mm(dx2, full['w_out'], tb=True, name="d_merge")
    G['w_out'] = mm(mg, dx2, ta=True, name="d_w_out")
    dbr, dba, dpg = merge_bwd(dmg, br, ba, pg, name="merge_bwd")
    dyr = mm(dbr, full['w_branch_rwkv'], tb=True, name="d_y_rwkv")
    G['w_branch_rwkv'] = mm(yr, dbr, ta=True, name="d_w_branch_rwkv")
    dya = mm(dba, full['w_branch_attn'], tb=True, name="d_y_att")
    G['w_branch_attn'] = mm(ya, dba, ta=True, name="d_w_branch_att")
    late_blocks = [_to_blocks(G[n], axis_of[n]) for n in LATE_WEIGHTS]
    dy, dz, dg, G['rwkv_ln_w'], G['rwkv_ln_b'], *late_from_sibling = rwkv_post_bwd(
        dyr, y, r, k2, v, sg, wlg, ln_w, ln_b, r_k, name="rwkv_post_bwd", carry=(_SiblingPlan, late_blocks))
    late_pair = [pair_sum(g, t, BF16, name="pair_sum_" + n) for g, t, n in zip(late_blocks, late_from_sibling, LATE_WEIGHTS)]
    res = wkv_bwd(r, dec, k2, v, a, b, dy, states, name="wkv_bwd", exchange=[q for _, q in late_pair])
    wkv_grads, late_landed = res[:6], res[6:]
    (dpr, d_mu, G['rwkv_w0'], G['rwkv_a0'], G['rwkv_k_k'], G['rwkv_k_a'], G['rwkv_r_k'], d_wlw, d_wla, d_wlg) = rwkv_pre_bwd(
        pr, pr_shift, *wkv_grads, dz, dg, mu, w0, a0, k_k, k_a, r_k, wlw, wla, wlg, name="rwkv_pre_bwd")
    G['rwkv_mu'] = _unpad_rwkv_cols(d_mu)
    G['rwkv_w_lora_up'], G['rwkv_a_lora_up'], G['rwkv_g_lora_up'] = d_wlw[:DECAY_LORA], d_wla[:ICLR_LORA], d_wlg[:GATE_LORA]
    dq, dko, dkn, dvo, dvn, G['attn_q_norm'], d_sinks = att_bwd(pa, dya, qg, kg, sinks, name="att_bwd")
    G['attn_sinks'] = d_sinks[:, :N_HEADS]
    dpa, G['attn_k_norm'] = att_kv_bwd(pa, dq, dko, dkn, dvo, dvn, kg, name="att_kv_bwd")
    d_w_in_r = mm(h2, dpr, ta=True, name="d_w_in_rwkv")
    d_w_in_a = mm(h2, dpa, ta=True, name="d_w_in_att")
    d_w_in_g = mm(h2, dpg, ta=True, name="d_w_in_gate")
    G['w_in'] = jnp.concatenate([_unpad_rwkv_cols(d_w_in_r), d_w_in_a, d_w_in_g], axis=1)
    mid_blocks = [_to_blocks(G[n], axis_of[n]) for n in MID_WEIGHTS]
    dh2, *mid_from_sibling = mm(dpr, w_in_r, tb=True, name="d_h2_rwkv", carry=(_SiblingPlan, mid_blocks))
    dh2 = mm(dpa, w_in_a, tb=True, res=dh2, name="d_h2_att")
    dx1, G['mix_norm'] = mm(dpg, w_in_g, tb=True, res=dh2, name="d_h2_gate", norm_bwd=(x1, nmix, dx2))
    mid_pair = [pair_sum(g, t, BF16, name="pair_sum_" + n) for g, t, n in zip(mid_blocks, mid_from_sibling, MID_WEIGHTS)]
    dx0, G['ffn1_norm'], _, _, _, mid_landed, (first_pair, first_landed) = _ffn_bwd(
        dx1, x, ffn1, n1, full['ffn1_w_gate'], full['ffn1_w_up'], full['ffn1_w_down'], "ffn1",
        carry=(_ExchangePlan, [q for _, q in mid_pair]), reduce=FIRST_WEIGHTS)

    small_shapes = [(w[n].size,) for n in SMALL] + [(1,)]

    def small_rows(parts):
        vec = jnp.concatenate([p.reshape(-1) for p in parts])
        return jnp.pad(vec, (0, SMALL_ROWS * FLAT_W - vec.shape[0])).reshape(SMALL_ROWS, FLAT_W)

    small = small_rows([G[n] for n in SMALL] + [loss[0, :1]])
    small_pair = _pair_sums(['small'], [jnp.broadcast_to(small[None], (N_CHIPS,) + small.shape)], "small")
    small_landed = exchange_chips([q for _, q in small_pair], name="exchange_small")
    names = FIRST_WEIGHTS + ['small'] + MID_WEIGHTS + LATE_WEIGHTS
    pair = first_pair + small_pair + mid_pair + late_pair
    landed = list(first_landed) + list(small_landed) + list(mid_landed) + list(late_landed)
    halves = [half_sum(own, l, name="half_sum_" + n) for (own, _), l, n in zip(pair, landed, names)]
    other_halves = to_sibling(halves, False, name="halves_to_sibling")

    def local(prefix, n):
        if n != 'small':
            return natural(n, A[prefix + n])
        return small_rows([A[prefix + s] for s in SMALL] + [jnp.zeros((1,), F32)])[None]

    result = {}
    for n, mine, theirs in zip(names, halves, other_halves):
        outs4 = adamw(local('', n), local('m_', n), local('v_', n), mine, theirs, name="adamw_" + n)
        for kind, o in zip(('grad_', 'delta_', 'new_m_', 'new_v_'), outs4):
            if n != 'small':
                result[kind + n] = natural(n, o)
            else:
                for s, part in zip(SMALL + ['loss'], _unpack_vec(o.reshape(-1), small_shapes)):
                    result[kind + s] = part.reshape(A[s].shape) if s != 'loss' else part.reshape(())
    outs = [result['grad_loss'], dx0[None]]
    for kind in ('grad_', 'delta_', 'new_m_', 'new_v_'):
        outs += [result[kind + n] for n in WEIGHT_NAMES]
    return tuple(outs)


def _unpack_vec(vec, shapes):
    out, off = [], 0
    for (n,) in shapes:
        out.append(vec[off:off + n])
        off += n
    return out


def kernel(x, ffn1_norm, ffn1_w_gate, ffn1_w_up, ffn1_w_down, mix_norm, w_in, rwkv_mu, rwkv_w0, rwkv_w_lora_up, rwkv_a0, rwkv_a_lora_up, rwkv_g_lora_up, rwkv_k_k, rwkv_k_a, rwkv_r_k, rwkv_ln_w, rwkv_ln_b, attn_q_norm, attn_k_norm, attn_sinks, w_branch_rwkv, w_branch_attn, w_out, ffn2_norm, ffn2_w_gate, ffn2_w_up, ffn2_w_down, final_norm, loss_target, m_ffn1_norm, m_ffn1_w_gate, m_ffn1_w_up, m_ffn1_w_down, m_mix_norm, m_w_in, m_rwkv_mu, m_rwkv_w0, m_rwkv_w_lora_up, m_rwkv_a0, m_rwkv_a_lora_up, m_rwkv_g_lora_up, m_rwkv_k_k, m_rwkv_k_a, m_rwkv_r_k, m_rwkv_ln_w, m_rwkv_ln_b, m_attn_q_norm, m_attn_k_norm, m_attn_sinks, m_w_branch_rwkv, m_w_branch_attn, m_w_out, m_ffn2_norm, m_ffn2_w_gate, m_ffn2_w_up, m_ffn2_w_down, m_final_norm, v_ffn1_norm, v_ffn1_w_gate, v_ffn1_w_up, v_ffn1_w_down, v_mix_norm, v_w_in, v_rwkv_mu, v_rwkv_w0, v_rwkv_w_lora_up, v_rwkv_a0, v_rwkv_a_lora_up, v_rwkv_g_lora_up, v_rwkv_k_k, v_rwkv_k_a, v_rwkv_r_k, v_rwkv_ln_w, v_rwkv_ln_b, v_attn_q_norm, v_attn_k_norm, v_attn_sinks, v_w_branch_rwkv, v_w_branch_attn, v_w_out, v_ffn2_norm, v_ffn2_w_gate, v_ffn2_w_up, v_ffn2_w_down, v_final_norm):
    return _step(dict(locals()))
```

```python
import functools

import jax
import jax.numpy as jnp
from jax import lax
from jax.experimental import pallas as pl
from jax.experimental.pallas import tpu as pltpu

F32 = jnp.float32
BF16 = jnp.bfloat16

D_MODEL = 1024
D_FF = 2816
HEAD_DIM = 64
N_HEADS = 8
RW = 512
KVW = 128
ATT_GROUP = 4
WINDOW = 128
BLOCK = 128
DECAY_LORA, ICLR_LORA, GATE_LORA = 32, 32, 96
RWKV_COLS = 3 * RW + DECAY_LORA + ICLR_LORA + GATE_LORA
ATT_COLS = RW + 2 * KVW
GATE_COLS = 2 * D_MODEL
RWKV_PAD = 3 * RW + 3 * 128
RMS_EPS = 1e-6
GN_EPS = 64e-5
N_CHIPS = 4
LANE = 128
FLAT_W = 1024
SMALL_ROWS = 32
NEG_BIG = -1e30

ADAM_LR, ADAM_B1, ADAM_B2, ADAM_EPS, ADAM_WD, ADAM_STEP = 0.001, 0.9, 0.999, 1e-08, 0.01, 10

VMEM_LIMIT = 56 * 1024 * 1024

WEIGHT_NAMES = ['ffn1_norm', 'ffn1_w_gate', 'ffn1_w_up', 'ffn1_w_down', 'mix_norm', 'w_in', 'rwkv_mu', 'rwkv_w0',
                'rwkv_w_lora_up', 'rwkv_a0', 'rwkv_a_lora_up', 'rwkv_g_lora_up', 'rwkv_k_k', 'rwkv_k_a', 'rwkv_r_k',
                'rwkv_ln_w', 'rwkv_ln_b', 'attn_q_norm', 'attn_k_norm', 'attn_sinks', 'w_branch_rwkv',
                'w_branch_attn', 'w_out', 'ffn2_norm', 'ffn2_w_gate', 'ffn2_w_up', 'ffn2_w_down', 'final_norm']
BIG = [('ffn1_w_gate', 1), ('ffn1_w_up', 1), ('ffn1_w_down', 0), ('w_in', 1), ('rwkv_w_lora_up', 1),
       ('rwkv_a_lora_up', 1), ('rwkv_g_lora_up', 1), ('w_branch_rwkv', 1), ('w_branch_attn', 1), ('w_out', 0),
       ('ffn2_w_gate', 1), ('ffn2_w_up', 1), ('ffn2_w_down', 0)]
SMALL = ['ffn1_norm', 'mix_norm', 'rwkv_mu', 'rwkv_w0', 'rwkv_a0', 'rwkv_k_k', 'rwkv_k_a', 'rwkv_r_k', 'rwkv_ln_w',
         'rwkv_ln_b', 'attn_q_norm', 'attn_k_norm', 'attn_sinks', 'ffn2_norm', 'final_norm']


def _pcall(body, **kw):
    return pl.pallas_call(body, **kw)


def _params(sem=None, **kw):
    if sem is not None:
        kw['dimension_semantics'] = sem
    return pltpu.CompilerParams(vmem_limit_bytes=VMEM_LIMIT, **kw)


def _tile(n, cap, mult):
    best = None
    for t in range(mult, min(n, cap) + 1, mult):
        if n % t == 0:
            best = t
    return best or n


def _sigmoid(z):
    return 1.0 / (1.0 + jnp.exp(-z))


def _softplus(z):
    return jnp.maximum(z, 0.0) + jnp.log(1.0 + jnp.exp(-jnp.abs(z)))


def _bdot(a, b, dims=(((1,), (0,)), ((), ()))):
    return lax.dot_general(a.astype(BF16), b.astype(BF16), dims, preferred_element_type=F32)


_NT = (((1,), (1,)), ((), ()))
_TN = (((0,), (0,)), ((), ()))


def _segsum(x, bd):
    hi = x.astype(BF16)
    lo = (x - hi.astype(F32)).astype(BF16)
    dot = functools.partial(lax.dot_general, dimension_numbers=(((1,), (0,)), ((), ())), preferred_element_type=F32)
    return dot(hi, bd) + dot(lo, bd)


_LORA_EDGES = (3 * RW, 3 * RW + DECAY_LORA, 3 * RW + DECAY_LORA + ICLR_LORA, RWKV_COLS)


def _pad_rwkv_cols(x):
    parts = [x[..., :3 * RW]]
    for lo, hi in zip(_LORA_EDGES[:-1], _LORA_EDGES[1:]):
        parts.append(jnp.pad(x[..., lo:hi], [(0, 0)] * (x.ndim - 1) + [(0, 128 - (hi - lo))]))
    return jnp.concatenate(parts, axis=-1)


def _unpad_rwkv_cols(x):
    parts = [x[..., :3 * RW]]
    for j, (lo, hi) in enumerate(zip(_LORA_EDGES[:-1], _LORA_EDGES[1:])):
        parts.append(x[..., 3 * RW + 128 * j:3 * RW + 128 * j + (hi - lo)])
    return jnp.concatenate(parts, axis=-1)


def _pad_rows(x, rows):
    return jnp.pad(x, [(0, rows - x.shape[0])] + [(0, 0)] * (x.ndim - 1))


def mm(a, b, *, name, ta=False, tb=False, scale=None, res=None, out_dtype=F32, carry=None, norm_bwd=None):
    M, K = (a.shape[1], a.shape[0]) if ta else a.shape
    N = b.shape[0] if tb else b.shape[1]
    assert (b.shape[1] if tb else b.shape[0]) == K
    tm, tn, tk = _tile(M, 1408 if ta else 512, 128), _tile(N, 1408, 128), _tile(K, 1408, 128)
    nk = K // tk
    grid = (M // tm, N // tn, nk)
    dims = (((0 if ta else 1,), (1 if tb else 0,)), ((), ()))
    plan_cls, carried = carry if carry else (None, ())
    nn = 3 if norm_bwd else 0
    nd = 1 if norm_bwd else 0
    nc, nin = len(carried), 2 + (res is not None) + nn
    assert not norm_bwd or (tn == N and out_dtype == F32)

    def body(*refs):
        a_ref, b_ref = refs[:2]
        r_ref = refs[2] if res is not None else None
        o_ref, acc_ref = refs[nin + nc], refs[nin + 2 * nc + 1 + nd]
        row_tile, k = pl.program_id(0), pl.program_id(2)
        if nc:
            plan = plan_cls(refs[nin:nin + nc], refs[nin + nc + 1 + nd:nin + 2 * nc + 1 + nd], refs[nin + 2 * nc + 2 + nd:])
            at = lambda which: functools.reduce(jnp.logical_and, [pl.program_id(d) == (0 if which == 0 else grid[d] - 1)
                                                                 for d in range(3)])
            pl.when(at(0))(plan.start)
        part = _bdot(a_ref[...], b_ref[...], dims)

        @pl.when(k == 0)
        def _():
            acc_ref[...] = part

        @pl.when(k > 0)
        def _():
            acc_ref[...] += part

        @pl.when(k == nk - 1)
        def _():
            o = acc_ref[...]
            if scale is not None:
                o = o * scale
            if r_ref is not None:
                o = o + r_ref[...].astype(F32)
            if norm_bwd:
                x_ref, g_ref, dres_ref = refs[nin - 3:nin]
                xv = x_ref[...]
                r = lax.rsqrt(jnp.mean(xv * xv, axis=-1, keepdims=True) + RMS_EPS)
                xh = xv * r
                dxh = o * g_ref[...]
                _acc_rows(refs[nin + nc + 1], jnp.sum(o * xh, axis=0, keepdims=True), row_tile)
                o = dres_ref[...] + r * (dxh - xh * jnp.mean(dxh * xh, axis=-1, keepdims=True))
            o_ref[...] = o.astype(out_dtype)

        if nc:
            pl.when(at(1))(plan.finish)

    a_spec = pl.BlockSpec((tk, tm), lambda i, j, k: (k, i)) if ta else pl.BlockSpec((tm, tk), lambda i, j, k: (i, k))
    b_spec = pl.BlockSpec((tn, tk), lambda i, j, k: (j, k)) if tb else pl.BlockSpec((tk, tn), lambda i, j, k: (k, j))
    o_spec = pl.BlockSpec((tm, tn), lambda i, j, k: (i, j))
    g_spec = pl.BlockSpec((1, N), lambda i, j, k: (0, 0))
    in_specs = [a_spec, b_spec] + ([o_spec] if res is not None else []) + ([o_spec, g_spec, o_spec] if norm_bwd else [])
    args = (a, b) + ((res,) if res is not None else ()) + (tuple(norm_bwd) if norm_bwd else ())
    out_shape = [jax.ShapeDtypeStruct((M, N), out_dtype)] + [jax.ShapeDtypeStruct((1, N), F32)] * nd
    out_specs = [o_spec] + [g_spec] * nd
    if not nc and not nd:
        return _pcall(
            body, name=name, grid=grid, in_specs=in_specs, out_specs=o_spec, out_shape=out_shape[0],
            scratch_shapes=[pltpu.VMEM((tm, tn), F32)], compiler_params=_params(("parallel", "parallel", "arbitrary")),
        )(*args)
    return _pcall(
        body, name=name, grid=grid, in_specs=in_specs + [_HBM] * nc, out_specs=out_specs + [_HBM] * nc,
        out_shape=out_shape + (plan_cls.out_shapes(carried) if nc else []),
        scratch_shapes=[pltpu.VMEM((tm, tn), F32)] + (plan_cls.sems(nc) if nc else []),
        compiler_params=_params(("arbitrary", "arbitrary", "arbitrary")),
    )(*args, *carried)


def mm_fused(a, bs, finish, out_dtypes, *, name, tb=False, extras=(), carry=None):
    M, K = a.shape
    N = bs[0].shape[0] if tb else bs[0].shape[1]
    tm, tn = _tile(M, 512, 128), _tile(N, 1408, 128)
    grid = (M // tm, N // tn)
    dims = (((1,), (1 if tb else 0,)), ((), ()))
    plan_cls, carried = carry if carry else (None, ())
    nc, nb, nx, no = len(carried), len(bs), len(extras), len(out_dtypes)
    nin = 1 + nb + nx

    def body(*refs):
        a_ref, b_refs, x_refs = refs[0], refs[1:1 + nb], refs[1 + nb:nin]
        o_refs = refs[nin + nc:nin + nc + no]
        if nc:
            plan = plan_cls(refs[nin:nin + nc], refs[nin + nc + no:nin + 2 * nc + no], refs[nin + 2 * nc + no:])
            at = lambda which: jnp.logical_and(*[pl.program_id(d) == (0 if which == 0 else grid[d] - 1) for d in range(2)])
            pl.when(at(0))(plan.start)
        av = a_ref[...]
        outs = finish([_bdot(av, b_ref[...], dims) for b_ref in b_refs], [x_ref[...] for x_ref in x_refs])
        for o_ref, o in zip(o_refs, outs):
            o_ref[...] = o.astype(o_ref.dtype)
        if nc:
            pl.when(at(1))(plan.finish)

    a_spec = pl.BlockSpec((tm, K), lambda i, j: (i, 0))
    b_spec = pl.BlockSpec((tn, K), lambda i, j: (j, 0)) if tb else pl.BlockSpec((K, tn), lambda i, j: (0, j))
    o_spec = pl.BlockSpec((tm, tn), lambda i, j: (i, j))
    out_shape = [jax.ShapeDtypeStruct((M, N), d) for d in out_dtypes]
    if not nc:
        return _pcall(body, name=name, grid=grid, in_specs=[a_spec] + [b_spec] * nb + [o_spec] * nx, out_specs=[o_spec] * no,
                      out_shape=out_shape, compiler_params=_params(("parallel", "parallel")))(a, *bs, *extras)
    return _pcall(body, name=name, grid=grid, in_specs=[a_spec] + [b_spec] * nb + [o_spec] * nx + [_HBM] * nc,
                  out_specs=[o_spec] * no + [_HBM] * nc, out_shape=out_shape + plan_cls.out_shapes(carried),
                  scratch_shapes=plan_cls.sems(nc), compiler_params=_params(("arbitrary", "arbitrary")))(a, *bs, *extras, *carried)


def _swiglu(products, _):
    g, u = products
    return g, u, g * _sigmoid(g) * u


def _swiglu_bwd(products, extras):
    (da,), (gate, up) = products, extras
    gv = gate.astype(F32)
    s = _sigmoid(gv)
    return da * 0.5 * up.astype(F32) * s * (1.0 + gv * (1.0 - s)), da * 0.5 * gv * s


def _row_spec(tr, c):
    return pl.BlockSpec((tr, c), lambda i: (i, 0))


def _full_spec(shape):
    return pl.BlockSpec(shape, lambda i: (0,) * len(shape))


def _acc_rows(ref, val, i):
    @pl.when(i == 0)
    def _():
        ref[...] = val

    @pl.when(i > 0)
    def _():
        ref[...] += val


def rms_fwd(x, g, *, name):
    T, D = x.shape
    tr = _tile(T, 512, 8)

    def body(x_ref, g_ref, h_ref):
        xv = x_ref[...]
        r = lax.rsqrt(jnp.mean(xv * xv, axis=-1, keepdims=True) + RMS_EPS)
        h_ref[...] = (xv * r * g_ref[...]).astype(BF16)

    return _pcall(body, name=name, grid=(T // tr,), in_specs=[_row_spec(tr, D), _full_spec((1, D))],
                  out_specs=_row_spec(tr, D), out_shape=jax.ShapeDtypeStruct((T, D), BF16),
                  compiler_params=_params(("parallel",)))(x, g)


def final_loss(x, tgt, g, *, name):
    T, D = x.shape
    tr = _tile(T, 256, 8)

    def body(x_ref, t_ref, g_ref, dx_ref, dg_ref, loss_ref):
        i = pl.program_id(0)
        xv = x_ref[...]
        r = lax.rsqrt(jnp.mean(xv * xv, axis=-1, keepdims=True) + RMS_EPS)
        xh = xv * r
        e = xh * g_ref[...] - t_ref[...]
        part = 0.5 * jnp.sum(jnp.mean(e * e, axis=-1, keepdims=True), axis=0, keepdims=True)
        dy = e * (1.0 / D)
        dxh = dy * g_ref[...]
        dx_ref[...] = r * (dxh - xh * jnp.mean(dxh * xh, axis=-1, keepdims=True))
        _acc_rows(dg_ref, jnp.sum(dy * xh, axis=0, keepdims=True), i)
        _acc_rows(loss_ref, jnp.broadcast_to(part, (1, LANE)), i)

    return _pcall(body, name=name, grid=(T // tr,),
                  in_specs=[_row_spec(tr, D), _row_spec(tr, D), _full_spec((1, D))],
                  out_specs=[_row_spec(tr, D), _full_spec((1, D)), _full_spec((1, LANE))],
                  out_shape=[jax.ShapeDtypeStruct((T, D), F32), jax.ShapeDtypeStruct((1, D), F32),
                             jax.ShapeDtypeStruct((1, LANE), F32)],
                  compiler_params=_params(("arbitrary",)))(x, tgt, g)


def merge_fwd(br, ba, pg, *, name):
    T, D = br.shape
    tr = _tile(T, 256, 8)

    def body(br_ref, ba_ref, pg_ref, o_ref):
        pgv = pg_ref[...]
        o_ref[...] = (_sigmoid(pgv[:, :D]) * br_ref[...] + _sigmoid(pgv[:, D:]) * ba_ref[...]).astype(BF16)

    return _pcall(body, name=name, grid=(T // tr,), in_specs=[_row_spec(tr, D), _row_spec(tr, D), _row_spec(tr, 2 * D)],
                  out_specs=_row_spec(tr, D), out_shape=jax.ShapeDtypeStruct((T, D), BF16),
                  compiler_params=_params(("parallel",)))(br, ba, pg)


def merge_bwd(dm, br, ba, pg, *, name):
    T, D = br.shape
    tr = _tile(T, 256, 8)

    def body(dm_ref, br_ref, ba_ref, pg_ref, dbr_ref, dba_ref, dpg_ref):
        pgv, dmv = pg_ref[...], dm_ref[...]
        sr, sa = _sigmoid(pgv[:, :D]), _sigmoid(pgv[:, D:])
        dbr_ref[...] = (dmv * sr).astype(BF16)
        dba_ref[...] = (dmv * sa).astype(BF16)
        dpg_ref[:, :D] = dmv * br_ref[...] * sr * (1.0 - sr)
        dpg_ref[:, D:] = dmv * ba_ref[...] * sa * (1.0 - sa)

    return _pcall(body, name=name, grid=(T // tr,),
                  in_specs=[_row_spec(tr, D), _row_spec(tr, D), _row_spec(tr, D), _row_spec(tr, 2 * D)],
                  out_specs=[_row_spec(tr, D), _row_spec(tr, D), _row_spec(tr, 2 * D)],
                  out_shape=[jax.ShapeDtypeStruct((T, D), BF16), jax.ShapeDtypeStruct((T, D), BF16),
                             jax.ShapeDtypeStruct((T, 2 * D), F32)],
                  compiler_params=_params(("parallel",)))(dm, br, ba, pg)


def _rwkv_mix(p, prev, mu, w0, a0, k_k, k_a, wlw, wla, wlg, bd):
    pp = p + (prev - p) * mu
    r, k, v = pp[:, 0:RW], pp[:, RW:2 * RW], pp[:, 2 * RW:3 * RW]
    xw, xa, xg = pp[:, 3 * RW:3 * RW + 128], pp[:, 3 * RW + 128:3 * RW + 256], pp[:, 3 * RW + 256:3 * RW + 384]
    th = jnp.tanh(xw)
    z = -(w0 + _bdot(th, wlw))
    e = jnp.exp(-_softplus(z) - 0.5)
    decay = jnp.exp(-e)
    a = _sigmoid(a0 + _bdot(xa, wla))
    sg = _sigmoid(xg)
    kkr = k * k_k
    n = jnp.sqrt(_segsum(kkr * kkr, bd))
    kk = kkr / jnp.maximum(n, 1e-12)
    k2 = k * (1.0 + (a - 1.0) * k_a)
    return dict(r=r, k=k, v=v, xa=xa, th=th, z=z, e=e, decay=decay, a=a, sg=sg, n=n, kk=kk, k2=k2)


def _seg_matrix(n, shift):
    r = lax.shift_right_logical(lax.broadcasted_iota(jnp.int32, (n, n), 0), shift)
    c = lax.shift_right_logical(lax.broadcasted_iota(jnp.int32, (n, n), 1), shift)
    return jnp.where(r == c, 1.0, 0.0).astype(BF16)


def rwkv_pre_fwd(p, pshift, mu, w0, a0, k_k, k_a, wlw, wla, wlg, *, name):
    T = p.shape[0]
    tr = _tile(T, 256, 8)

    def body(p_ref, ps_ref, mu_ref, w0_ref, a0_ref, kk_ref, ka_ref, wlw_ref, wla_ref, wlg_ref,
             r_ref, w_ref, k_ref, v_ref, a_ref, b_ref, g_ref):
        pv, prev = p_ref[...], ps_ref[...]
        m = _rwkv_mix(pv, prev, mu_ref[...], w0_ref[...], a0_ref[...], kk_ref[...], ka_ref[...],
                      wlw_ref[...], wla_ref[...], wlg_ref[...], _seg_matrix(RW, 6))
        r_ref[...] = m['r']
        w_ref[...] = m['decay']
        k_ref[...] = m['k2']
        v_ref[...] = m['v']
        a_ref[...] = -m['kk']
        b_ref[...] = m['kk'] * m['a']
        g_ref[...] = m['sg']

    vec = _row_spec(tr, RW)
    return _pcall(
        body, name=name, grid=(T // tr,),
        in_specs=[_row_spec(tr, RWKV_PAD), _row_spec(tr, RWKV_PAD), _full_spec((1, RWKV_PAD))] + [_full_spec((1, RW))] * 4
        + [_full_spec((128, RW))] * 3,
        out_specs=[vec] * 6 + [_row_spec(tr, 128)],
        out_shape=[jax.ShapeDtypeStruct((T, RW), F32)] * 6 + [jax.ShapeDtypeStruct((T, 128), F32)],
        compiler_params=_params(("parallel",)),
    )(p, pshift, mu, w0, a0, k_k, k_a, wlw, wla, wlg)


def _group_norm(y, bd):
    mean = _segsum(y, bd) * (1.0 / HEAD_DIM)
    yc = y - mean
    rstd = lax.rsqrt(_segsum(yc * yc, bd) * (1.0 / HEAD_DIM) + GN_EPS)
    return yc * rstd, rstd


def rwkv_post_fwd(y, r, k2, v, sg, wlg, ln_w, ln_b, r_k, *, name):
    T = y.shape[0]
    tr = _tile(T, 256, 8)

    def body(y_ref, r_ref, k_ref, v_ref, sg_ref, wlg_ref, lw_ref, lb_ref, rk_ref, o_ref):
        bd = _seg_matrix(RW, 6)
        yn, _ = _group_norm(y_ref[...], bd)
        s = _segsum(r_ref[...] * k_ref[...] * rk_ref[...], bd)
        g = _bdot(sg_ref[...], wlg_ref[...])
        o_ref[...] = ((yn * lw_ref[...] + lb_ref[...] + s * v_ref[...]) * g).astype(BF16)

    vec = _row_spec(tr, RW)
    return _pcall(body, name=name, grid=(T // tr,),
                  in_specs=[vec] * 4 + [_row_spec(tr, 128), _full_spec((128, RW))] + [_full_spec((1, RW))] * 3, out_specs=vec,
                  out_shape=jax.ShapeDtypeStruct((T, RW), BF16), compiler_params=_params(("parallel",)))(
                      y, r, k2, v, sg, wlg, ln_w, ln_b, r_k)


def rwkv_post_bwd(dyr, y, r, k2, v, sg, wlg, ln_w, ln_b, r_k, *, name, carry=None):
    T = y.shape[0]
    tr = _tile(T, 256, 8)
    nt = T // tr
    plan_cls, carried = carry if carry else (None, ())
    nc = len(carried)

    def body(*refs):
        dyr_ref, y_ref, r_ref, k_ref, v_ref, sg_ref, wlg_ref, lw_ref, lb_ref, rk_ref = refs[:10]
        dy_ref, dz_ref, dg_ref, dlw_ref, dlb_ref = refs[10 + nc:15 + nc]
        i = pl.program_id(0)
        if nc:
            plan = plan_cls(refs[10:10 + nc], refs[15 + nc:15 + 2 * nc], refs[15 + 2 * nc:])
            pl.when(i == 0)(plan.start)
        bd = _seg_matrix(RW, 6)
        yn, rstd = _group_norm(y_ref[...], bd)
        s = _segsum(r_ref[...] * k_ref[...] * rk_ref[...], bd)
        dyrv = dyr_ref[...]
        dg_ref[...] = dyrv * (yn * lw_ref[...] + lb_ref[...] + s * v_ref[...])
        dz = dyrv * _bdot(sg_ref[...], wlg_ref[...])
        dz_ref[...] = dz
        dyn = dz * lw_ref[...]
        inv = 1.0 / HEAD_DIM
        dy_ref[...] = rstd * (dyn - _segsum(dyn, bd) * inv - yn * (_segsum(dyn * yn, bd) * inv))
        _acc_rows(dlw_ref, jnp.sum(dz * yn, axis=0, keepdims=True), i)
        _acc_rows(dlb_ref, jnp.sum(dz, axis=0, keepdims=True), i)
        if nc:
            pl.when(i == nt - 1)(plan.finish)

    vec = _row_spec(tr, RW)
    one = _full_spec((1, RW))
    return _pcall(body, name=name, grid=(nt,),
                  in_specs=[vec] * 5 + [_row_spec(tr, 128), _full_spec((128, RW))] + [one] * 3 + [_HBM] * nc,
                  out_specs=[vec] * 3 + [one] * 2 + [_HBM] * nc,
                  out_shape=[jax.ShapeDtypeStruct((T, RW), F32)] * 3 + [jax.ShapeDtypeStruct((1, RW), F32)] * 2
                  + (plan_cls.out_shapes(carried) if nc else []),
                  scratch_shapes=plan_cls.sems(nc) if nc else [],
                  compiler_params=_params(("arbitrary",)))(dyr, y, r, k2, v, sg, wlg, ln_w, ln_b, r_k, *carried)


def rwkv_pre_bwd(p, pshift, dr_w, dw_w, dk_w, dv_w, da_w, db_w, dz, dg, mu, w0, a0, k_k, k_a, r_k, wlw, wla, wlg, *, name):
    T = p.shape[0]
    tr = _tile(T, 256, 8)
    n = T // tr

    def body(p_ref, ps_ref, dr_ref, dw_ref, dk_ref, dv_ref, da_ref, db_ref, dz_ref, dg_ref,
             mu_ref, w0_ref, a0_ref, kk_ref, ka_ref, rk_ref, wlw_ref, wla_ref, wlg_ref,
             dp_ref, dmu_ref, dw0_ref, da0_ref, dkk_ref, dka_ref, drk_ref, dwlw_ref, dwla_ref, dwlg_ref,
             carry, dpp, acc_w, acc_a, acc_g):
        i = pl.program_id(0)

        @pl.when(i == 0)
        def _():
            carry[...] = jnp.zeros_like(carry)

        pv, prev, mu = p_ref[...], ps_ref[...], mu_ref[...]
        bd = _seg_matrix(RW, 6)
        k_k, k_a, r_k = kk_ref[...], ka_ref[...], rk_ref[...]
        m = _rwkv_mix(pv, prev, mu, w0_ref[...], a0_ref[...], k_k, k_a, wlw_ref[...], wla_ref[...], wlg_ref[...], bd)
        r, k, v, a, kk, k2 = m['r'], m['k'], m['v'], m['a'], m['kk'], m['k2']
        dzv, dgv = dz_ref[...], dg_ref[...]
        s = _segsum(r * k2 * r_k, bd)
        ds = _segsum(dzv * v, bd)
        dr = dr_ref[...] + ds * k2 * r_k
        dk2 = dk_ref[...] + ds * r * r_k
        dv = dv_ref[...] + dzv * s
        dbv = db_ref[...]
        dkk = dbv * a - da_ref[...]
        da = dbv * kk + dk2 * k * k_a
        dk = dk2 * (1.0 + (a - 1.0) * k_a)
        nmax = jnp.maximum(m['n'], 1e-12)
        dkkr = jnp.where(m['n'] > 1e-12, dkk - kk * _segsum(dkk * kk, bd), dkk) / nmax
        dk = dk + dkkr * k_k
        dapre = da * a * (1.0 - a)
        dwpre = dw_ref[...] * m['decay'] * (-m['e']) * _sigmoid(m['z'])
        dth = _bdot(dwpre, wlw_ref[...], _NT)
        dxa = _bdot(dapre, wla_ref[...], _NT)
        dsg = _bdot(dgv, wlg_ref[...], _NT)
        dpp[:, 0:RW] = dr
        dpp[:, RW:2 * RW] = dk
        dpp[:, 2 * RW:3 * RW] = dv
        dpp[:, 3 * RW:3 * RW + 128] = dth * (1.0 - m['th'] * m['th'])
        dpp[:, 3 * RW + 128:3 * RW + 256] = dxa
        dpp[:, 3 * RW + 256:3 * RW + 384] = dsg * m['sg'] * (1.0 - m['sg'])
        d = dpp[...]
        zed = d * mu
        last = lax.broadcasted_iota(jnp.int32, pv.shape, 0) == tr - 1
        dp_ref[...] = d * (1.0 - mu) + jnp.where(last, carry[0:1, :], pltpu.roll(zed, tr - 1, 0))
        carry[...] = zed[0:8, :]

        def colsum(x):
            return jnp.sum(x, axis=0, keepdims=True)

        _acc_rows(dmu_ref, colsum(d * (prev - pv)), i)
        _acc_rows(dw0_ref, colsum(dwpre), i)
        _acc_rows(da0_ref, colsum(dapre), i)
        _acc_rows(dkk_ref, colsum(dkkr * k), i)
        _acc_rows(dka_ref, colsum(dk2 * k * (a - 1.0)), i)
        _acc_rows(drk_ref, colsum(ds * r * k2), i)
        _acc_rows(acc_w, _bdot(m['th'], dwpre, _TN), i)
        _acc_rows(acc_a, _bdot(m['xa'], dapre, _TN), i)
        _acc_rows(acc_g, _bdot(m['sg'], dgv, _TN), i)

        @pl.when(i == n - 1)
        def _():
            dwlw_ref[...] = acc_w[...]
            dwla_ref[...] = acc_a[...]
            dwlg_ref[...] = acc_g[...]

    rev = lambda c: pl.BlockSpec((tr, c), lambda i: (n - 1 - i, 0))
    one, lora = _full_spec((1, RW)), _full_spec((128, RW))
    return _pcall(
        body, name=name, grid=(n,),
        in_specs=[rev(RWKV_PAD), rev(RWKV_PAD)] + [rev(RW)] * 8 + [_full_spec((1, RWKV_PAD))] + [one] * 5 + [lora] * 3,
        out_specs=[rev(RWKV_PAD), _full_spec((1, RWKV_PAD))] + [one] * 5 + [lora] * 3,
        out_shape=[jax.ShapeDtypeStruct((T, RWKV_PAD), F32), jax.ShapeDtypeStruct((1, RWKV_PAD), F32)]
        + [jax.ShapeDtypeStruct((1, RW), F32)] * 5 + [jax.ShapeDtypeStruct((128, RW), F32)] * 3,
        scratch_shapes=[pltpu.VMEM((8, RWKV_PAD), F32), pltpu.VMEM((tr, RWKV_PAD), F32)] + [pltpu.VMEM((128, RW), F32)] * 3,
        compiler_params=_params(("arbitrary",)),
    )(p, pshift, dr_w, dw_w, dk_w, dv_w, da_w, db_w, dz, dg, mu, w0, a0, k_k, k_a, r_k, wlw, wla, wlg)


def _qk_norm(x, g, bd):
    r = lax.rsqrt(_segsum(x * x, bd) * (1.0 / HEAD_DIM) + RMS_EPS)
    return x * r * g, r


def _att_mask(i):
    qi = lax.broadcasted_iota(jnp.int32, (BLOCK, 2 * BLOCK), 0)
    kj = lax.broadcasted_iota(jnp.int32, (BLOCK, 2 * BLOCK), 1)
    band = (kj <= qi + BLOCK) & (kj > qi + BLOCK - WINDOW)
    return band & ((kj >= BLOCK) | (i > 0))


_HQK = (((2,), (2,)), ((0,), (0,)))
_HPV = (((2,), (1,)), ((0,), (0,)))
_HTN = (((1,), (1,)), ((0,), (0,)))


def _heads(x, n):
    return jnp.stack([x[:, h * HEAD_DIM:(h + 1) * HEAD_DIM] for h in range(n)])


def _unheads(x3):
    return jnp.concatenate([x3[h] for h in range(x3.shape[0])], axis=1)


def _kv_heads(x):
    x2 = _heads(x, KVW // HEAD_DIM)
    return jnp.concatenate([x2[g:g + 1] for g in range(KVW // HEAD_DIM) for _ in range(ATT_GROUP)], axis=0)


def _sinks3(sk):
    return jnp.stack([sk[0:1, h:h + 1] for h in range(N_HEADS)])


def _att_probs(q3, k3, mask, sink):
    s = _bdot(q3, k3, _HQK) * (HEAD_DIM ** -0.5)
    s = jnp.where(mask[None], s, NEG_BIG)
    m = jnp.maximum(jnp.max(s, axis=-1, keepdims=True), sink)
    pexp = jnp.exp(s - m)
    psink = jnp.exp(sink - m)
    inv = 1.0 / (jnp.sum(pexp, axis=-1, keepdims=True) + psink)
    return pexp * inv, psink * inv


ATT_SUB = 2


def _att_blocks(sub):
    cur = pl.BlockSpec((sub * BLOCK, ATT_COLS), lambda i: (i, 0))
    prev = pl.BlockSpec((BLOCK, ATT_COLS), lambda i: (jnp.maximum(i * sub - 1, 0), 0))
    return cur, prev


def _att_sub(cur_all, prev_first, j):
    cur = cur_all[j * BLOCK:(j + 1) * BLOCK]
    return cur, (prev_first if j == 0 else cur_all[(j - 1) * BLOCK:j * BLOCK])


def _att_qkv(cur, prev, qn_g, kn_g):
    bq, bk = _seg_matrix(RW, 6), _seg_matrix(KVW, 6)
    qn, rq = _qk_norm(cur[:, 0:RW], qn_g, bq)
    kcur, rkc = _qk_norm(cur[:, RW:RW + KVW], kn_g, bk)
    kprev, _ = _qk_norm(prev[:, RW:RW + KVW], kn_g, bk)
    kc = jnp.concatenate([kprev, kcur], axis=0)
    vc = jnp.concatenate([prev[:, RW + KVW:], cur[:, RW + KVW:]], axis=0)
    return qn, rq, kc, vc, rkc


def att_fwd(pa, qn_g, kn_g, sinks, *, name):
    T = pa.shape[0]
    sub = ATT_SUB if (T // BLOCK) % ATT_SUB == 0 else 1
    n = T // (sub * BLOCK)

    def body(cur_ref, prev_ref, qg_ref, kg_ref, sk_ref, o_ref):
        i = pl.program_id(0)
        cur_all, prev_first = cur_ref[...], prev_ref[...]
        for j in range(sub):
            cur, prev = _att_sub(cur_all, prev_first, j)
            qn, _, kc, vc, _ = _att_qkv(cur, prev, qg_ref[...], kg_ref[...])
            probs, _ = _att_probs(_heads(qn, N_HEADS), _kv_heads(kc), _att_mask(i * sub + j), _sinks3(sk_ref[...]))
            o_ref[j * BLOCK:(j + 1) * BLOCK, :] = _unheads(_bdot(probs, _kv_heads(vc), _HPV))

    cur, prev = _att_blocks(sub)
    return _pcall(body, name=name, grid=(n,),
                  in_specs=[cur, prev, _full_spec((1, RW)), _full_spec((1, KVW)), _full_spec((1, LANE))],
                  out_specs=pl.BlockSpec((sub * BLOCK, RW), lambda i: (i, 0)), out_shape=jax.ShapeDtypeStruct((T, RW), F32),
                  compiler_params=_params(("parallel",)))(pa, pa, qn_g, kn_g, sinks)


def att_bwd(pa, do, qn_g, kn_g, sinks, *, name):
    T = pa.shape[0]
    sub = ATT_SUB if (T // BLOCK) % ATT_SUB == 0 else 1
    n = T // (sub * BLOCK)

    def one_block(cur, prev, do, blk, qg_ref, kg_ref, sk_ref, rows, dq_ref, dko_ref, dkn_ref, dvo_ref, dvn_ref):
        qn, rq, kc, vc, _ = _att_qkv(cur, prev, qg_ref[...], kg_ref[...])
        q3, k3, v3, do3 = _heads(qn, N_HEADS), _kv_heads(kc), _kv_heads(vc), _heads(do, N_HEADS)
        probs, psink = _att_probs(q3, k3, _att_mask(blk), _sinks3(sk_ref[...]))
        dprobs = _bdot(do3, v3, _HQK)
        delta = jnp.sum(probs * dprobs, axis=-1, keepdims=True)
        ds = probs * (dprobs - delta) * (HEAD_DIM ** -0.5)
        dsink3 = -jnp.sum(psink * delta, axis=1, keepdims=True)
        lane = lax.broadcasted_iota(jnp.int32, (1, LANE), 1)
        dsink = jnp.zeros((1, LANE), F32)
        for h in range(N_HEADS):
            dsink = dsink + jnp.where(lane == h, dsink3[h], 0.0)
        dqn = _unheads(_bdot(ds, k3, _HPV))

        def per_kv_head(x3):
            groups = [sum(x3[g * ATT_GROUP + j] for j in range(ATT_GROUP)) for g in range(KVW // HEAD_DIM)]
            return jnp.concatenate(groups, axis=1)

        dk, dv = per_kv_head(_bdot(ds, q3, _HTN)), per_kv_head(_bdot(probs, do3, _HTN))
        dkn_ref[rows, :], dko_ref[rows, :] = dk[0:BLOCK], dk[BLOCK:]
        dvn_ref[rows, :], dvo_ref[rows, :] = dv[0:BLOCK], dv[BLOCK:]
        qhat = cur[:, 0:RW] * rq
        dqh = dqn * qg_ref[...]
        dq_ref[rows, :] = rq * (dqh - qhat * (_segsum(dqh * qhat, _seg_matrix(RW, 6)) * (1.0 / HEAD_DIM)))
        prod = dqn * qhat
        fold = prod[:, 0:HEAD_DIM]
        for h in range(1, N_HEADS):
            fold = fold + prod[:, h * HEAD_DIM:(h + 1) * HEAD_DIM]
        return jnp.sum(fold, axis=0, keepdims=True), dsink

    def body(cur_ref, prev_ref, do_ref, qg_ref, kg_ref, sk_ref,
             dq_ref, dko_ref, dkn_ref, dvo_ref, dvn_ref, dqg_ref, dsk_ref):
        i = pl.program_id(0)
        cur_all, prev_first, do_all = cur_ref[...], prev_ref[...], do_ref[...]
        dqg, dsk = None, None
        for j in range(sub):
            cur, prev = _att_sub(cur_all, prev_first, j)
            rows = slice(j * BLOCK, (j + 1) * BLOCK)
            g, s = one_block(cur, prev, do_all[rows], i * sub + j, qg_ref, kg_ref, sk_ref, rows,
                             dq_ref, dko_ref, dkn_ref, dvo_ref, dvn_ref)
            dqg, dsk = (g, s) if dqg is None else (dqg + g, dsk + s)
        _acc_rows(dqg_ref, dqg, i)
        _acc_rows(dsk_ref, dsk, i)

    cur, prev = _att_blocks(sub)
    kvb = pl.BlockSpec((sub * BLOCK, KVW), lambda i: (i, 0))
    qb = pl.BlockSpec((sub * BLOCK, RW), lambda i: (i, 0))
    return _pcall(body, name=name, grid=(n,),
                  in_specs=[cur, prev, qb, _full_spec((1, RW)), _full_spec((1, KVW)), _full_spec((1, LANE))],
                  out_specs=[qb, kvb, kvb, kvb, kvb, _full_spec((1, HEAD_DIM)), _full_spec((1, LANE))],
                  out_shape=[jax.ShapeDtypeStruct((T, RW), F32)] + [jax.ShapeDtypeStruct((T, KVW), F32)] * 4
                  + [jax.ShapeDtypeStruct((1, HEAD_DIM), F32), jax.ShapeDtypeStruct((1, LANE), F32)],
                  compiler_params=_params(("arbitrary",)))(pa, pa, do, qn_g, kn_g, sinks)


def att_kv_bwd(pa, dq, dko, dkn, dvo, dvn, kn_g, *, name):
    T = pa.shape[0]
    n = T // BLOCK

    def body(pa_ref, dq_ref, dko_ref, dkn_ref, dvo_ref, dvn_ref, kg_ref, dpa_ref, dkg_ref):
        i = pl.program_id(0)
        more = i < n - 1
        dkn_tot = dko_ref[...] + jnp.where(more, dkn_ref[...], 0.0)
        dv_tot = dvo_ref[...] + jnp.where(more, dvn_ref[...], 0.0)
        kraw = pa_ref[:, RW:RW + KVW]
        bk = _seg_matrix(KVW, 6)
        _, rk = _qk_norm(kraw, kg_ref[...], bk)
        khat = kraw * rk
        dkh = dkn_tot * kg_ref[...]
        dpa_ref[:, 0:RW] = dq_ref[...]
        dpa_ref[:, RW:RW + KVW] = rk * (dkh - khat * (_segsum(dkh * khat, bk) * (1.0 / HEAD_DIM)))
        dpa_ref[:, RW + KVW:] = dv_tot
        prod = dkn_tot * khat
        _acc_rows(dkg_ref, jnp.sum(prod[:, 0:HEAD_DIM] + prod[:, HEAD_DIM:], axis=0, keepdims=True), i)

    kvb = pl.BlockSpec((BLOCK, KVW), lambda i: (i, 0))
    nxt = pl.BlockSpec((BLOCK, KVW), lambda i: (jnp.minimum(i + 1, n - 1), 0))
    return _pcall(body, name=name, grid=(n,),
                  in_specs=[pl.BlockSpec((BLOCK, ATT_COLS), lambda i: (i, 0)), pl.BlockSpec((BLOCK, RW), lambda i: (i, 0)),
                            kvb, nxt, kvb, nxt, _full_spec((1, KVW))],
                  out_specs=[pl.BlockSpec((BLOCK, ATT_COLS), lambda i: (i, 0)), _full_spec((1, HEAD_DIM))],
                  out_shape=[jax.ShapeDtypeStruct((T, ATT_COLS), F32), jax.ShapeDtypeStruct((1, HEAD_DIM), F32)],
                  compiler_params=_params(("arbitrary",)))(pa, dq, dko, dkn, dvo, dvn, kn_g)


WKV_CHUNK = 128
WKV_GROUP = 16


def _diag_mask():
    i = lax.broadcasted_iota(jnp.int32, (HEAD_DIM, RW), 0)
    j = lax.broadcasted_iota(jnp.int32, (HEAD_DIM, RW), 1) & (HEAD_DIM - 1)
    return i == j


def _heads_matrix():
    head = jnp.arange(RW // 2) // HEAD_DIM
    bd = (head[:, None] == head[None, :]).astype(BF16)
    return jnp.concatenate([bd, bd], axis=0)


def _headsums(xs, pieces, bd2):
    half = RW // 2
    bd = bd2[:pieces * half]
    rows = []
    for x in xs:
        parts, rest = [], x
        for n in range(pieces):
            p = rest.astype(BF16)
            parts.append(p)
            if n + 1 < pieces:
                rest = rest - p.astype(F32)
        for sl in (slice(0, half), slice(half, RW)):
            rows.append(jnp.concatenate([p[:, sl] for p in parts], axis=1))
    out = lax.dot_general(jnp.concatenate(rows, axis=0), bd, (((1,), (0,)), ((), ())), preferred_element_type=F32)
    return [jnp.concatenate([out[2 * n * HEAD_DIM:(2 * n + 1) * HEAD_DIM], out[(2 * n + 1) * HEAD_DIM:(2 * n + 2) * HEAD_DIM]],
                            axis=1) for n in range(len(xs))]


def _headsum(x):
    low = lax.broadcasted_iota(jnp.int32, (HEAD_DIM, LANE), 1) < HEAD_DIM
    tiles = []
    for c in range(RW // LANE):
        xt = x[:, c * LANE:(c + 1) * LANE]
        s_lo = jnp.sum(jnp.where(low, xt, 0.0), axis=1, keepdims=True)
        s_hi = jnp.sum(jnp.where(low, 0.0, xt), axis=1, keepdims=True)
        tiles.append(jnp.where(low, s_lo, s_hi))
    return jnp.concatenate(tiles, axis=1)


def _cols(rows, diag, bd2, pieces=2):
    return _headsums([jnp.where(diag, r, 0.0) for r in rows], pieces, bd2)


def _row(x, diag):
    return jnp.sum(jnp.where(diag, x, 0.0), axis=0, keepdims=True)


def wkv_fwd(r, w, k, v, a, b, *, name, gather=()):
    T = r.shape[0]
    ch = min(WKV_CHUNK, T)
    ngroups = ch // WKV_GROUP
    nchunks = T // ch
    ng = len(gather)

    def body(*refs):
        r_ref, w_ref, k_ref, v_ref, a_ref, b_ref, bd_ref = refs[:7]
        y_ref, st_ref = refs[7 + ng:9 + ng]
        s_scr = refs[9 + 2 * ng]
        step = pl.program_id(0)
        if ng:
            plan = _GatherPlan(refs[7:7 + ng], refs[9 + ng:9 + 2 * ng], refs[10 + 2 * ng:])
            pl.when(step == 0)(plan.start)
            pl.when(step == nchunks // 2)(plan.relay)

        @pl.when(step == 0)
        def _():
            s_scr[...] = jnp.zeros_like(s_scr)

        diag, bd2 = _diag_mask(), bd_ref[...]

        def group(gi, S):
            t0 = pl.multiple_of(gi * WKV_GROUP, WKV_GROUP)
            rows = pl.ds(t0, WKV_GROUP)
            R, W, K, V, A, B = (ref[rows, :] for ref in (r_ref, w_ref, k_ref, v_ref, a_ref, b_ref))
            vcols = _cols([V[s:s + 1] for s in range(WKV_GROUP)], diag, bd2, 1)
            yrows = []
            for s in range(WKV_GROUP):
                sa = _headsum(S * A[s:s + 1])
                S = S * W[s:s + 1] + sa * B[s:s + 1] + vcols[s] * K[s:s + 1]
                st_ref[t0 + s] = S
                yrows.append(_row(_headsums([S * R[s:s + 1]], 1, bd2)[0], diag))
            y_ref[rows, :] = jnp.concatenate(yrows, axis=0)
            return S

        s_scr[...] = lax.fori_loop(0, ngroups, group, s_scr[...])
        if ng:
            pl.when(step == nchunks - 1)(plan.finish_relayed)

    vec = pl.BlockSpec((ch, RW), lambda c: (c, 0))
    return _pcall(
        body, name=name, grid=(nchunks,), in_specs=[vec] * 6 + [_full_spec((RW, RW // 2))] + [_HBM] * ng,
        out_specs=[vec, pl.BlockSpec((ch, HEAD_DIM, RW), lambda c: (c, 0, 0))] + [_HBM] * ng,
        out_shape=[jax.ShapeDtypeStruct((T, RW), F32), jax.ShapeDtypeStruct((T, HEAD_DIM, RW), F32)] + _gathered_shapes(gather),
        scratch_shapes=[pltpu.VMEM((HEAD_DIM, RW), F32)] + (_GatherPlan.sems(ng) if ng else []),
        compiler_params=_params(("arbitrary",)),
    )(r, w, k, v, a, b, _heads_matrix(), *gather)


def wkv_bwd(r, w, k, v, a, b, dy, states, *, name, exchange=()):
    T = r.shape[0]
    ch = min(WKV_CHUNK, T)
    nchunks = T // ch
    ngroups = ch // WKV_GROUP
    ne = len(exchange)

    def body(*refs):
        r_ref, w_ref, k_ref, v_ref, a_ref, b_ref, dy_ref, st_ref, stp_ref, bd_ref = refs[:10]
        dr_ref, dw_ref, dk_ref, dv_ref, da_ref, db_ref = refs[10 + ne:16 + ne]
        ds_scr = refs[16 + 2 * ne]
        step = pl.program_id(0)
        if ne:
            plan = _ExchangePlan(refs[10:10 + ne], refs[16 + ne:16 + 2 * ne], refs[17 + 2 * ne:])
            pl.when(step == 0)(plan.start)

        @pl.when(step == 0)
        def _():
            ds_scr[...] = jnp.zeros_like(ds_scr)

        has_prev_chunk = step < nchunks - 1
        diag, bd2 = _diag_mask(), bd_ref[...]
        colsum = lambda x: jnp.sum(x, axis=0, keepdims=True)

        def group(gj, dS):
            gi = ngroups - 1 - gj
            t0 = pl.multiple_of(gi * WKV_GROUP, WKV_GROUP)
            rows = pl.ds(t0, WKV_GROUP)
            R, W, K, V, A, B, DY = (ref[rows, :] for ref in (r_ref, w_ref, k_ref, v_ref, a_ref, b_ref, dy_ref))
            before = jnp.where(gi > 0, st_ref[jnp.maximum(t0 - 1, 0)], jnp.where(has_prev_chunk, stp_ref[0], 0.0))
            prev_state = lambda s: st_ref[t0 + s - 1] if s > 0 else before
            steps = range(WKV_GROUP)
            dycols = _cols([DY[s:s + 1] for s in steps], diag, bd2, 1)
            vcols = _cols([V[s:s + 1] for s in steps], diag, bd2, 1)
            sas = _headsums([prev_state(s) * A[s:s + 1] for s in steps], 1, bd2)
            got = [[None] * WKV_GROUP for _ in range(6)]
            for s in reversed(steps):
                Sp = prev_state(s)
                dS = dS + dycols[s] * R[s:s + 1]
                got[0][s] = colsum(st_ref[t0 + s] * dycols[s])
                got[3][s] = _row(_headsums([dS * K[s:s + 1]], 1, bd2)[0], diag)
                got[2][s] = colsum(dS * vcols[s])
                dsa = _headsum(dS * B[s:s + 1])
                got[5][s] = colsum(dS * sas[s])
                got[1][s] = colsum(dS * Sp)
                got[4][s] = colsum(Sp * dsa)
                dS = dS * W[s:s + 1] + dsa * A[s:s + 1]
            for q, ref in enumerate((dr_ref, dw_ref, dk_ref, dv_ref, da_ref, db_ref)):
                ref[rows, :] = jnp.concatenate(got[q], axis=0)
            return dS

        ds_scr[...] = lax.fori_loop(0, ngroups, group, ds_scr[...])
        if ne:
            pl.when(step == nchunks - 1)(plan.finish)

    vec = pl.BlockSpec((ch, RW), lambda c: (nchunks - 1 - c, 0))
    st_spec = pl.BlockSpec((ch, HEAD_DIM, RW), lambda c: (nchunks - 1 - c, 0, 0))
    stp_spec = pl.BlockSpec((1, HEAD_DIM, RW), lambda c: (jnp.maximum((nchunks - 1 - c) * ch - 1, 0), 0, 0))
    return _pcall(
        body, name=name, grid=(nchunks,), in_specs=[vec] * 7 + [st_spec, stp_spec, _full_spec((RW, RW // 2))] + [_HBM] * ne,
        out_specs=[vec] * 6 + [_HBM] * ne,
        out_shape=[jax.ShapeDtypeStruct((T, RW), F32)] * 6 + [jax.ShapeDtypeStruct(e.shape, e.dtype) for e in exchange],
        scratch_shapes=[pltpu.VMEM((HEAD_DIM, RW), F32)] + (_ExchangePlan.sems(ne) if ne else []),
        compiler_params=_params(("arbitrary",)),
    )(r, w, k, v, a, b, dy, states, states, _heads_matrix(), *exchange)


_HBM = pl.BlockSpec(memory_space=pltpu.HBM)
_MESH = pl.DeviceIdType.MESH


def _place():
    x, y, c = lax.axis_index("x"), lax.axis_index("y"), lax.axis_index("c")
    return x, y, c, [(1 - x, y), (x, 1 - y), (1 - x, 1 - y)]


def _remote(src, dst, send_sem, recv_sem, to):
    return pltpu.make_async_remote_copy(src_ref=src, dst_ref=dst, send_sem=send_sem, recv_sem=recv_sem, device_id=to,
                                        device_id_type=_MESH)


def _dma_sems(*counts):
    return [pltpu.SemaphoreType.DMA((n,)) for n in counts]


class _GatherPlan:
    def __init__(self, ins, outs, sems):
        self.ins, self.outs, self.n = ins, outs, len(ins)
        self.ici_send, self.ici_recv, self.d2d_send, self.d2d_recv, self.local_sems = sems
        x, y, c, chips = _place()
        self.c, self.me, self.sibling = c, 2 * x + y, (x, y, 1 - c)
        self.peers = [(2 * qx + qy, (qx, qy, c)) for qx, qy in chips]

    @staticmethod
    def sems(n):
        return _dma_sems(3 * n, 3 * n, 3 * n, 3 * n, n)

    def _half(self, i, which):
        rh = self.ins[i].shape[0] // 2
        return pl.ds(which * rh, rh)

    def _local(self, i):
        return pltpu.make_async_copy(self.ins[i], self.outs[i].at[self.me], self.local_sems.at[i])

    def _send(self, i, j):
        k, mine = 3 * i + j, self._half(i, self.c)
        return _remote(self.ins[i].at[mine], self.outs[i].at[self.me, mine], self.ici_send.at[k], self.ici_recv.at[k],
                       self.peers[j][1])

    def _landed(self, i, j):
        k, piece = 3 * i + j, self.outs[i].at[self.peers[j][0], self._half(i, self.c)]
        return _remote(piece, piece, self.ici_send.at[k], self.ici_recv.at[k], self.peers[j][1])

    def _pass(self, i, j, which):
        k, piece = 3 * i + j, self.outs[i].at[self.peers[j][0], self._half(i, which)]
        return _remote(piece, piece, self.d2d_send.at[k], self.d2d_recv.at[k], self.sibling)

    def _all(self):
        return [(i, j) for i in range(self.n) for j in range(3)]

    def start(self):
        for i in range(self.n):
            self._local(i).start()
        for i, j in self._all():
            self._send(i, j).start()

    def relay(self):
        for i, j in self._all():
            self._landed(i, j).wait_recv()
            self._pass(i, j, self.c).start()

    def finish_relayed(self):
        for i, j in self._all():
            self._pass(i, j, 1 - self.c).wait_recv()
        for i, j in self._all():
            self._send(i, j).wait_send()
            self._pass(i, j, self.c).wait_send()
        for i in range(self.n):
            self._local(i).wait()

    def finish(self):
        self.relay()
        self.finish_relayed()

    @staticmethod
    def out_shapes(shards):
        return _gathered_shapes(shards)


def _gathered_shapes(shards):
    return [jax.ShapeDtypeStruct((N_CHIPS,) + s.shape, s.dtype) for s in shards]


def gather_weights(shards, *, name):
    n = len(shards)

    def body(*refs):
        plan = _GatherPlan(refs[:n], refs[n:2 * n], refs[2 * n:])
        plan.start()
        plan.finish()

    return _pcall(body, name=name, in_specs=[_HBM] * n, out_specs=[_HBM] * n, out_shape=_gathered_shapes(shards),
                  scratch_shapes=_GatherPlan.sems(n), compiler_params=_params())(*shards)


class _SiblingPlan:
    halves = True

    def __init__(self, ins, outs, sems):
        self.ins, self.outs, self.n = ins, outs, len(ins)
        self.send_sems, self.recv_sems = sems
        x, y, c, _ = _place()
        self.c, self.sibling = c, (x, y, 1 - c)

    @staticmethod
    def sems(n):
        return _dma_sems(n, n)

    @classmethod
    def out_shapes(cls, arrays):
        if not cls.halves:
            return [jax.ShapeDtypeStruct(a.shape, a.dtype) for a in arrays]
        return [jax.ShapeDtypeStruct((a.shape[0], a.shape[1] // 2, a.shape[2]), a.dtype) for a in arrays]

    def _copy(self, i):
        src = self.ins[i]
        if self.halves:
            rh = src.shape[1] // 2
            src = src.at[:, pl.ds((1 - self.c) * rh, rh)]
        return _remote(src, self.outs[i], self.send_sems.at[i], self.recv_sems.at[i], self.sibling)

    def start(self):
        for i in range(self.n):
            self._copy(i).start()

    def finish(self):
        for i in range(self.n):
            self._copy(i).wait_recv()
        for i in range(self.n):
            self._copy(i).wait_send()


class _SiblingWhole(_SiblingPlan):
    halves = False


def to_sibling(arrays, take_other_half, *, name):
    n = len(arrays)
    plan_cls = _SiblingPlan if take_other_half else _SiblingWhole

    def body(*refs):
        plan = plan_cls(refs[:n], refs[n:2 * n], refs[2 * n:])
        plan.start()
        plan.finish()

    return _pcall(body, name=name, in_specs=[_HBM] * n, out_specs=[_HBM] * n, out_shape=plan_cls.out_shapes(arrays),
                  scratch_shapes=plan_cls.sems(n), compiler_params=_params())(*arrays)


def exchange_chips(arrays, *, name):
    n = len(arrays)

    def body(*refs):
        plan = _ExchangePlan(refs[:n], refs[n:2 * n], refs[2 * n:])
        plan.start()
        plan.finish()

    return _pcall(body, name=name, in_specs=[_HBM] * n, out_specs=[_HBM] * n,
                  out_shape=[jax.ShapeDtypeStruct(a.shape, a.dtype) for a in arrays],
                  scratch_shapes=_ExchangePlan.sems(n), compiler_params=_params())(*arrays)


class _ExchangePlan:
    def __init__(self, ins, outs, sems):
        self.ins, self.outs, self.n = ins, outs, len(ins)
        self.send_sems, self.recv_sems, self.local_sems = sems
        x, y, c, chips = _place()
        self.me = 2 * x + y
        self.peers = [(2 * qx + qy, (qx, qy, c)) for qx, qy in chips]

    @staticmethod
    def sems(n):
        return _dma_sems(3 * n, 3 * n, n)

    @staticmethod
    def out_shapes(arrays):
        return [jax.ShapeDtypeStruct(a.shape, a.dtype) for a in arrays]

    def _local(self, i):
        return pltpu.make_async_copy(self.ins[i].at[self.me], self.outs[i].at[self.me], self.local_sems.at[i])

    def _send(self, i, j):
        k = 3 * i + j
        return _remote(self.ins[i].at[self.peers[j][0]], self.outs[i].at[self.me], self.send_sems.at[k], self.recv_sems.at[k],
                       self.peers[j][1])

    def _landed(self, i, j):
        k, piece = 3 * i + j, self.outs[i].at[self.peers[j][0]]
        return _remote(piece, piece, self.send_sems.at[k], self.recv_sems.at[k], self.peers[j][1])

    def start(self):
        for i in range(self.n):
            self._local(i).start()
            for j in range(3):
                self._send(i, j).start()

    def finish(self):
        for i in range(self.n):
            for j in range(3):
                self._landed(i, j).wait_recv()
        for i in range(self.n):
            for j in range(3):
                self._send(i, j).wait_send()
            self._local(i).wait()


def _core_index():
    return lax.axis_index("c").astype(jnp.int32).reshape(1)


def pair_sum(g, theirs, wire_dtype, *, name):
    _, R, C = g.shape
    rh = R // 2
    tr = _tile(rh, 256, 16)
    nt = rh // tr

    def body(c_ref, g_ref, t_ref, q_ref, qw_ref):
        q = g_ref[...] + t_ref[...]
        q_ref[...] = q
        qw_ref[...] = q.astype(wire_dtype)

    blk = pl.BlockSpec((1, tr, C), lambda b, i, c_ref: (b, i, 0))
    mine = pl.BlockSpec((1, tr, C), lambda b, i, c_ref: (b, c_ref[0] * nt + i, 0))
    grid_spec = pltpu.PrefetchScalarGridSpec(num_scalar_prefetch=1, grid=(N_CHIPS, nt), in_specs=[mine, blk], out_specs=[blk, blk])
    return _pcall(body, name=name, grid_spec=grid_spec,
                  out_shape=[jax.ShapeDtypeStruct((N_CHIPS, rh, C), F32), jax.ShapeDtypeStruct((N_CHIPS, rh, C), wire_dtype)],
                  compiler_params=_params(("parallel", "parallel")))(_core_index(), g, theirs)


def half_sum(own, landed, *, name):
    _, rh, C = own.shape
    tr = _tile(rh, 256, 16)

    def body(me_ref, own_ref, land_ref, o_ref):
        total = None
        for p in range(N_CHIPS):
            term = jnp.where(me_ref[0] == p, own_ref[p], land_ref[p].astype(F32))
            total = term if total is None else total + term
        o_ref[...] = total

    blk = pl.BlockSpec((N_CHIPS, tr, C), lambda i, me_ref: (0, i, 0))
    grid_spec = pltpu.PrefetchScalarGridSpec(num_scalar_prefetch=1, grid=(rh // tr,), in_specs=[blk, blk],
                                             out_specs=pl.BlockSpec((tr, C), lambda i, me_ref: (i, 0)))
    me = (2 * lax.axis_index("x") + lax.axis_index("y")).astype(jnp.int32).reshape(1)
    return _pcall(body, name=name, grid_spec=grid_spec, out_shape=jax.ShapeDtypeStruct((rh, C), F32),
                  compiler_params=_params(("parallel",)))(me, own, landed)


def adamw(w, m, v, mine, theirs, *, name):
    _, R, C = w.shape
    rh = R // 2
    tr = _tile(rh, 256, 8)
    nt = rh // tr

    def body(c_ref, w_ref, m_ref, v_ref, a_ref, b_ref, g_ref, d_ref, nm_ref, nv_ref):
        is_mine = (pl.program_id(0) // nt) == c_ref[0]
        g = jnp.where(is_mine, a_ref[...], b_ref[...])
        g_ref[...] = g
        nm = ADAM_B1 * m_ref[...] + (1.0 - ADAM_B1) * g
        nv = ADAM_B2 * v_ref[...] + (1.0 - ADAM_B2) * (g * g)
        nm_ref[...] = nm
        nv_ref[...] = nv
        m_hat = nm / (1.0 - ADAM_B1 ** ADAM_STEP)
        v_hat = nv / (1.0 - ADAM_B2 ** ADAM_STEP)
        d_ref[...] = -ADAM_LR * (m_hat / (jnp.sqrt(v_hat) + ADAM_EPS) + ADAM_WD * w_ref[...])

    full = pl.BlockSpec((None, tr, C), lambda i, c_ref: (0, i, 0))
    a_spec = pl.BlockSpec((tr, C), lambda i, c_ref: (jnp.clip(i - c_ref[0] * nt, 0, nt - 1), 0))
    b_spec = pl.BlockSpec((tr, C), lambda i, c_ref: (jnp.clip(i - (1 - c_ref[0]) * nt, 0, nt - 1), 0))
    grid_spec = pltpu.PrefetchScalarGridSpec(num_scalar_prefetch=1, grid=(2 * nt,), in_specs=[full] * 3 + [a_spec, b_spec],
                                             out_specs=[full] * 4)
    return _pcall(body, name=name, grid_spec=grid_spec, out_shape=[jax.ShapeDtypeStruct((1, R, C), F32)] * 4,
                  compiler_params=_params(("arbitrary",)))(_core_index(), w, m, v, mine, theirs)


def _to_blocks(full, axis):
    r, c = full.shape
    if axis == 1:
        return full.reshape(r, N_CHIPS, c // N_CHIPS).transpose(1, 0, 2)
    return full.reshape(N_CHIPS, r // N_CHIPS, c)


def _from_blocks(blocks, axis):
    _, r, c = blocks.shape
    if axis == 1:
        return blocks.transpose(1, 0, 2).reshape(r, N_CHIPS * c)
    return blocks.reshape(N_CHIPS * r, c)


def _ffn_fwd(x, norm, wg_t, wu_t, wd, tag):
    h = rms_fwd(x, norm, name=tag + "_norm")
    gate, up, act = mm_fused(h, [wg_t, wu_t], _swiglu, [BF16] * 3, tb=True, name=tag + "_gate_up")
    out = mm(act, wd, scale=0.5, res=x, name=tag + "_down")
    return out, (h, gate, up, act)


def _ffn_bwd(dout, x, saved, norm, wg_t, wu_t, wd, tag, carry=None, reduce=None):
    h, gate, up, act = saved
    dgate, dup, *carried = mm_fused(dout, [wd], _swiglu_bwd, [BF16] * 2, tb=True, extras=[gate, up], name=tag + "_dact",
                                    carry=carry)
    dwd = mm(act, dout, ta=True, scale=0.5, name=tag + "_dwd")
    if reduce:
        down_blocks = _to_blocks(dwd, 0)
        dwg_t, down_from_sibling = mm(dgate, h, ta=True, name=tag + "_dwg", carry=(_SiblingPlan, [down_blocks]))
        down_pair = pair_sum(down_blocks, down_from_sibling, BF16, name="pair_sum_" + reduce[2])
        dwu_t, down_landed = mm(dup, h, ta=True, name=tag + "_dwu", carry=(_ExchangePlan, [down_pair[1]]))
        blocks = [_to_blocks(g, 0) for g in (dwg_t, dwu_t)]
        dh, *from_sibling = mm(dgate, wg_t, name=tag + "_dh_gate", carry=(_SiblingPlan, blocks))
        pair = [pair_sum(g, t, BF16, name="pair_sum_" + n) for g, t, n in zip(blocks, from_sibling, reduce[:2])]
        dx, dnorm, *landed = mm(dup, wu_t, res=dh, name=tag + "_dh_up", norm_bwd=(x, norm, dout),
                                carry=(_ExchangePlan, [q for _, q in pair]))
        return dx, dnorm, dwg_t, dwu_t, dwd, carried, (pair + [down_pair], landed + [down_landed])
    dwg_t = mm(dgate, h, ta=True, name=tag + "_dwg")
    dwu_t = mm(dup, h, ta=True, name=tag + "_dwu")
    dh = mm(dgate, wg_t, name=tag + "_dh_gate")
    dx, dnorm = mm(dup, wu_t, res=dh, name=tag + "_dh_up", norm_bwd=(x, norm, dout))
    return dx, dnorm, dwg_t, dwu_t, dwd, carried, None


TRANSPOSED = ('ffn1_w_gate', 'ffn1_w_up', 'ffn2_w_gate', 'ffn2_w_up')
FIRST_WEIGHTS = ['ffn1_w_gate', 'ffn1_w_up', 'ffn1_w_down']
MID_WEIGHTS = ['w_in', 'rwkv_w_lora_up', 'rwkv_a_lora_up', 'rwkv_g_lora_up']
LATE_WEIGHTS = ['w_branch_rwkv', 'w_branch_attn', 'w_out', 'ffn2_w_gate', 'ffn2_w_up', 'ffn2_w_down']


def _pair_sums(names, blocks, tag):
    from_sibling = to_sibling(blocks, True, name=tag + "_grads_to_sibling")
    return [pair_sum(g, t, F32 if n == 'small' else BF16, name="pair_sum_" + n)
            for g, t, n in zip(blocks, from_sibling, names)]


def _step(A):
    x, tgt = A['x'][0], A['loss_target'][0]
    T = x.shape[0]
    w = {n: A[n][0] for n in WEIGHT_NAMES}
    row = lambda a: a.reshape(1, -1)

    axis_of = {n: (0 if n in TRANSPOSED else axis) for n, axis in BIG}
    natural = lambda n, a: jnp.swapaxes(a, 1, 2) if n in TRANSPOSED else a
    shard = lambda n: natural(n, A[n])[0].astype(BF16)
    n1, nmix, n2, nfin = (row(w[n]) for n in ('ffn1_norm', 'mix_norm', 'ffn2_norm', 'final_norm'))
    gathered = gather_weights([shard(n) for n in FIRST_WEIGHTS[:2]], name="gather_weights")
    full = {n: _from_blocks(b, axis_of[n]) for n, b in zip(FIRST_WEIGHTS[:2], gathered)}
    h1 = rms_fwd(x, n1, name="ffn1_norm")
    w_in_shard = shard('w_in')
    upper, lower = w_in_shard[:D_MODEL // 2], w_in_shard[D_MODEL // 2:]
    gate1, up1, act1, down_blocks, upper_blocks = mm_fused(
        h1, [full['ffn1_w_gate'], full['ffn1_w_up']], _swiglu, [BF16] * 3, tb=True, name="ffn1_gate_up",
        carry=(_GatherPlan, [shard('ffn1_w_down'), upper]))
    full['ffn1_w_down'] = _from_blocks(down_blocks, axis_of['ffn1_w_down'])
    x1, lower_blocks, *gathered = mm(act1, full['ffn1_w_down'], scale=0.5, res=x, name="ffn1_down",
                                     carry=(_GatherPlan, [lower] + [shard(n) for n in MID_WEIGHTS[1:]]))
    full['w_in'] = jnp.concatenate([_from_blocks(upper_blocks, 1), _from_blocks(lower_blocks, 1)], axis=0)
    full.update({n: _from_blocks(b, axis_of[n]) for n, b in zip(MID_WEIGHTS[1:], gathered)})
    ffn1 = (h1, gate1, up1, act1)
    w_in_r = _pad_rwkv_cols(full['w_in'][:, :RWKV_COLS])
    w_in_a = full['w_in'][:, RWKV_COLS:RWKV_COLS + ATT_COLS]
    w_in_g = full['w_in'][:, RWKV_COLS + ATT_COLS:]
    wlw, wla, wlg = (_pad_rows(full[n], 128).astype(F32) for n in ('rwkv_w_lora_up', 'rwkv_a_lora_up', 'rwkv_g_lora_up'))
    mu = _pad_rwkv_cols(row(w['rwkv_mu']))
    w0, a0, k_k, k_a, r_k, ln_w, ln_b = (row(w[n]) for n in ('rwkv_w0', 'rwkv_a0', 'rwkv_k_k', 'rwkv_k_a', 'rwkv_r_k',
                                                               'rwkv_ln_w', 'rwkv_ln_b'))
    qg = jnp.tile(row(w['attn_q_norm']), (1, N_HEADS))
    kg = jnp.tile(row(w['attn_k_norm']), (1, KVW // HEAD_DIM))
    sinks = jnp.pad(row(w['attn_sinks']), ((0, 0), (0, LANE - N_HEADS)))

    h2 = rms_fwd(x1, nmix, name="mix_norm")
    pr = mm(h2, w_in_r, name="proj_rwkv")
    pa = mm(h2, w_in_a, name="proj_att")
    pg = mm(h2, w_in_g, name="proj_gate")
    pr_shift = jnp.pad(pr, ((1, 0), (0, 0)))[:-1]
    r, dec, k2, v, a, b, sg = rwkv_pre_fwd(pr, pr_shift, mu, w0, a0, k_k, k_a, wlw, wla, wlg, name="rwkv_pre")
    y, states, *gathered = wkv_fwd(r, dec, k2, v, a, b, name="wkv_fwd", gather=[shard(n) for n in LATE_WEIGHTS])
    full.update({n: _from_blocks(b, axis_of[n]) for n, b in zip(LATE_WEIGHTS, gathered)})
    yr = rwkv_post_fwd(y, r, k2, v, sg, wlg, ln_w, ln_b, r_k, name="rwkv_post")
    ya = att_fwd(pa, qg, kg, sinks, name="att_fwd")
    br = mm(yr, full['w_branch_rwkv'], name="branch_rwkv")
    ba = mm(ya, full['w_branch_attn'], name="branch_att")
    mg = merge_fwd(br, ba, pg, name="merge")
    x2 = mm(mg, full['w_out'], res=x1, name="mix_out")
    x3, ffn2 = _ffn_fwd(x2, n2, full['ffn2_w_gate'], full['ffn2_w_up'], full['ffn2_w_down'], "ffn2")
    dx3, d_nfin, loss = final_loss(x3, tgt, nfin, name="final_loss")

    G = {'final_norm': d_nfin}
    dx2, G['ffn2_norm'], G['ffn2_w_gate'], G['ffn2_w_up'], G['ffn2_w_down'], _, _ = _ffn_bwd(
        dx3, x2, ffn2, n2, full['ffn2_w_gate'], full['ffn2_w_up'], full['ffn2_w_down'], "ffn2")
    dmg = mm(dx2, full['w_out'], tb=True, name="d_merge")
    G['w_out'] = mm(mg, dx2, ta=True, name="d_w_out")
    dbr, dba, dpg = merge_bwd(dmg, br, ba, pg, name="merge_bwd")
    dyr = mm(dbr, full['w_branch_rwkv'], tb=True, name="d_y_rwkv")
    G['w_branch_rwkv'] = mm(yr, dbr, ta=True, name="d_w_branch_rwkv")
    dya = mm(dba, full['w_branch_attn'], tb=True, name="d_y_att")
    G['w_branch_attn'] = mm(ya, dba, ta=True, name="d_w_branch_att")
    late_blocks = [_to_blocks(G[n], axis_of[n]) for n in LATE_WEIGHTS]
    dy, dz, dg, G['rwkv_ln_w'], G['rwkv_ln_b'], *late_from_sibling = rwkv_post_bwd(
        dyr, y, r, k2, v, sg, wlg, ln_w, ln_b, r_k, name="rwkv_post_bwd", carry=(_SiblingPlan, late_blocks))
    late_pair = [pair_sum(g, t, BF16, name="pair_sum_" + n) for g, t, n in zip(late_blocks, late_from_sibling, LATE_WEIGHTS)]
    res = wkv_bwd(r, dec, k2, v, a, b, dy, states, name="wkv_bwd", exchange=[q for _, q in late_pair])
    wkv_grads, late_landed = res[:6], res[6:]
    (dpr, d_mu, G['rwkv_w0'], G['rwkv_a0'], G['rwkv_k_k'], G['rwkv_k_a'], G['rwkv_r_k'], d_wlw, d_wla, d_wlg) = rwkv_pre_bwd(
        pr, pr_shift, *wkv_grads, dz, dg, mu, w0, a0, k_k, k_a, r_k, wlw, wla, wlg, name="rwkv_pre_bwd")
    G['rwkv_mu'] = _unpad_rwkv_cols(d_mu)
    G['rwkv_w_lora_up'], G['rwkv_a_lora_up'], G['rwkv_g_lora_up'] = d_wlw[:DECAY_LORA], d_wla[:ICLR_LORA], d_wlg[:GATE_LORA]
    dq, dko, dkn, dvo, dvn, G['attn_q_norm'], d_sinks = att_bwd(pa, dya, qg, kg, sinks, name="att_bwd")
    G['attn_sinks'] = d_sinks[:, :N_HEADS]
    dpa, G['attn_k_norm'] = att_kv_bwd(pa, dq, dko, dkn, dvo, dvn, kg, name="att_kv_bwd")
    d_w_in_r = mm(h2, dpr, ta=True, name="d_w_in_rwkv")
    d_w_in_a = mm(h2, dpa, ta=True, name="d_w_in_att")
    d_w_in_g = mm(h2, dpg, ta=True, name="d_w_in_gate")
    G['w_in'] = jnp.concatenate([_unpad_rwkv_cols(d_w_in_r), d_w_in_a, d_w_in_g], axis=1)
    mid_blocks = [_to_blocks(G[n], axis_of[n]) for n in MID_WEIGHTS]
    dh2, *mid_from_sibling = mm(dpr, w_in_r, tb=True, name="d_h2_rwkv", carry=(_SiblingPlan, mid_blocks))
    dh2 = mm(dpa, w_in_a, tb=True, res=dh2, name="d_h2_att")
    dx1, G['mix_norm'] = mm(dpg, w_in_g, tb=True, res=dh2, name="d_h2_gate", norm_bwd=(x1, nmix, dx2))
    mid_pair = [pair_sum(g, t, BF16, name="pair_sum_" + n) for g, t, n in zip(mid_blocks, mid_from_sibling, MID_WEIGHTS)]
    dx0, G['ffn1_norm'], _, _, _, mid_landed, (first_pair, first_landed) = _ffn_bwd(
        dx1, x, ffn1, n1, full['ffn1_w_gate'], full['ffn1_w_up'], full['ffn1_w_down'], "ffn1",
        carry=(_ExchangePlan, [q for _, q in mid_pair]), reduce=FIRST_WEIGHTS)

    small_shapes = [(w[n].size,) for n in SMALL] + [(1,)]

    def small_rows(parts):
        vec = jnp.concatenate([p.reshape(-1) for p in parts])
        return jnp.pad(vec, (0, SMALL_ROWS * FLAT_W - vec.shape[0])).reshape(SMALL_ROWS, FLAT_W)

    small = small_rows([G[n] for n in SMALL] + [loss[0, :1]])
    small_pair = _pair_sums(['small'], [jnp.broadcast_to(small[None], (N_CHIPS,) + small.shape)], "small")
    small_landed = exchange_chips([q for _, q in small_pair], name="exchange_small")
    names = FIRST_WEIGHTS + ['small'] + MID_WEIGHTS + LATE_WEIGHTS
    pair = first_pair + small_pair + mid_pair + late_pair
    landed = list(first_landed) + list(small_landed) + list(mid_landed) + list(late_landed)
    halves = [half_sum(own, l, name="half_sum_" + n) for (own, _), l, n in zip(pair, landed, names)]
    other_halves = to_sibling(halves, False, name="halves_to_sibling")

    def local(prefix, n):
        if n != 'small':
            return natural(n, A[prefix + n])
        return small_rows([A[prefix + s] for s in SMALL] + [jnp.zeros((1,), F32)])[None]

    result = {}
    for n, mine, theirs in zip(names, halves, other_halves):
        outs4 = adamw(local('', n), local('m_', n), local('v_', n), mine, theirs, name="adamw_" + n)
        for kind, o in zip(('grad_', 'delta_', 'new_m_', 'new_v_'), outs4):
            if n != 'small':
                result[kind + n] = natural(n, o)
            else:
                for s, part in zip(SMALL + ['loss'], _unpack_vec(o.reshape(-1), small_shapes)):
                    result[kind + s] = part.reshape(A[s].shape) if s != 'loss' else part.reshape(())
    outs = [result['grad_loss'], dx0[None]]
    for kind in ('grad_', 'delta_', 'new_m_', 'new_v_'):
        outs += [result[kind + n] for n in WEIGHT_NAMES]
    return tuple(outs)


def _unpack_vec(vec, shapes):
    out, off = [], 0
    for (n,) in shapes:
        out.append(vec[off:off + n])
        off += n
    return out


def kernel(x, ffn1_norm, ffn1_w_gate, ffn1_w_up, ffn1_w_down, mix_norm, w_in, rwkv_mu, rwkv_w0, rwkv_w_lora_up, rwkv_a0, rwkv_a_lora_up, rwkv_g_lora_up, rwkv_k_k, rwkv_k_a, rwkv_r_k, rwkv_ln_w, rwkv_ln_b, attn_q_norm, attn_k_norm, attn_sinks, w_branch_rwkv, w_branch_attn, w_out, ffn2_norm, ffn2_w_gate, ffn2_w_up, ffn2_w_down, final_norm, loss_target, m_ffn1_norm, m_ffn1_w_gate, m_ffn1_w_up, m_ffn1_w_down, m_mix_norm, m_w_in, m_rwkv_mu, m_rwkv_w0, m_rwkv_w_lora_up, m_rwkv_a0, m_rwkv_a_lora_up, m_rwkv_g_lora_up, m_rwkv_k_k, m_rwkv_k_a, m_rwkv_r_k, m_rwkv_ln_w, m_rwkv_ln_b, m_attn_q_norm, m_attn_k_norm, m_attn_sinks, m_w_branch_rwkv, m_w_branch_attn, m_w_out, m_ffn2_norm, m_ffn2_w_gate, m_ffn2_w_up, m_ffn2_w_down, m_final_norm, v_ffn1_norm, v_ffn1_w_gate, v_ffn1_w_up, v_ffn1_w_down, v_mix_norm, v_w_in, v_rwkv_mu, v_rwkv_w0, v_rwkv_w_lora_up, v_rwkv_a0, v_rwkv_a_lora_up, v_rwkv_g_lora_up, v_rwkv_k_k, v_rwkv_k_a, v_rwkv_r_k, v_rwkv_ln_w, v_rwkv_ln_b, v_attn_q_norm, v_attn_k_norm, v_attn_sinks, v_w_branch_rwkv, v_w_branch_attn, v_w_out, v_ffn2_norm, v_ffn2_w_gate, v_ffn2_w_up, v_ffn2_w_down, v_final_norm):
    return _step(dict(locals()))
```

```python
import functools

import jax
import jax.numpy as jnp
from jax import lax
from jax.experimental import pallas as pl
from jax.experimental.pallas import tpu as pltpu

F32 = jnp.float32
BF16 = jnp.bfloat16

D_MODEL = 1024
D_FF = 2816
HEAD_DIM = 64
N_HEADS = 8
RW = 512
KVW = 128
ATT_GROUP = 4
WINDOW = 128
BLOCK = 128
DECAY_LORA, ICLR_LORA, GATE_LORA = 32, 32, 96
RWKV_COLS = 3 * RW + DECAY_LORA + ICLR_LORA + GATE_LORA
ATT_COLS = RW + 2 * KVW
GATE_COLS = 2 * D_MODEL
RWKV_PAD = 3 * RW + 3 * 128
RMS_EPS = 1e-6
GN_EPS = 64e-5
N_CHIPS = 4
LANE = 128
FLAT_W = 1024
SMALL_ROWS = 32
NEG_BIG = -1e30

ADAM_LR, ADAM_B1, ADAM_B2, ADAM_EPS, ADAM_WD, ADAM_STEP = 0.001, 0.9, 0.999, 1e-08, 0.01, 10

VMEM_LIMIT = 56 * 1024 * 1024

WEIGHT_NAMES = ['ffn1_norm', 'ffn1_w_gate', 'ffn1_w_up', 'ffn1_w_down', 'mix_norm', 'w_in', 'rwkv_mu', 'rwkv_w0',
                'rwkv_w_lora_up', 'rwkv_a0', 'rwkv_a_lora_up', 'rwkv_g_lora_up', 'rwkv_k_k', 'rwkv_k_a', 'rwkv_r_k',
                'rwkv_ln_w', 'rwkv_ln_b', 'attn_q_norm', 'attn_k_norm', 'attn_sinks', 'w_branch_rwkv',
                'w_branch_attn', 'w_out', 'ffn2_norm', 'ffn2_w_gate', 'ffn2_w_up', 'ffn2_w_down', 'final_norm']
BIG = [('ffn1_w_gate', 1), ('ffn1_w_up', 1), ('ffn1_w_down', 0), ('w_in', 1), ('rwkv_w_lora_up', 1),
       ('rwkv_a_lora_up', 1), ('rwkv_g_lora_up', 1), ('w_branch_rwkv', 1), ('w_branch_attn', 1), ('w_out', 0),
       ('ffn2_w_gate', 1), ('ffn2_w_up', 1), ('ffn2_w_down', 0)]
SMALL = ['ffn1_norm', 'mix_norm', 'rwkv_mu', 'rwkv_w0', 'rwkv_a0', 'rwkv_k_k', 'rwkv_k_a', 'rwkv_r_k', 'rwkv_ln_w',
         'rwkv_ln_b', 'attn_q_norm', 'attn_k_norm', 'attn_sinks', 'ffn2_norm', 'final_norm']


def _pcall(body, **kw):
    return pl.pallas_call(body, **kw)


def _params(sem=None, **kw):
    if sem is not None:
        kw['dimension_semantics'] = sem
    return pltpu.CompilerParams(vmem_limit_bytes=VMEM_LIMIT, **kw)


def _tile(n, cap, mult):
    best = None
    for t in range(mult, min(n, cap) + 1, mult):
        if n % t == 0:
            best = t
    return best or n


def _sigmoid(z):
    return 1.0 / (1.0 + jnp.exp(-z))


def _softplus(z):
    return jnp.maximum(z, 0.0) + jnp.log(1.0 + jnp.exp(-jnp.abs(z)))


def _bdot(a, b, dims=(((1,), (0,)), ((), ()))):
    return lax.dot_general(a.astype(BF16), b.astype(BF16), dims, preferred_element_type=F32)


_NT = (((1,), (1,)), ((), ()))
_TN = (((0,), (0,)), ((), ()))


def _segsum(x, bd):
    hi = x.astype(BF16)
    lo = (x - hi.astype(F32)).astype(BF16)
    dot = functools.partial(lax.dot_general, dimension_numbers=(((1,), (0,)), ((), ())), preferred_element_type=F32)
    return dot(hi, bd) + dot(lo, bd)


_LORA_EDGES = (3 * RW, 3 * RW + DECAY_LORA, 3 * RW + DECAY_LORA + ICLR_LORA, RWKV_COLS)


def _pad_rwkv_cols(x):
    parts = [x[..., :3 * RW]]
    for lo, hi in zip(_LORA_EDGES[:-1], _LORA_EDGES[1:]):
        parts.append(jnp.pad(x[..., lo:hi], [(0, 0)] * (x.ndim - 1) + [(0, 128 - (hi - lo))]))
    return jnp.concatenate(parts, axis=-1)


def _unpad_rwkv_cols(x):
    parts = [x[..., :3 * RW]]
    for j, (lo, hi) in enumerate(zip(_LORA_EDGES[:-1], _LORA_EDGES[1:])):
        parts.append(x[..., 3 * RW + 128 * j:3 * RW + 128 * j + (hi - lo)])
    return jnp.concatenate(parts, axis=-1)


def _pad_rows(x, rows):
    return jnp.pad(x, [(0, rows - x.shape[0])] + [(0, 0)] * (x.ndim - 1))


def mm(a, b, *, name, ta=False, tb=False, scale=None, res=None, out_dtype=F32, carry=None, norm_bwd=None):
    M, K = (a.shape[1], a.shape[0]) if ta else a.shape
    N = b.shape[0] if tb else b.shape[1]
    assert (b.shape[1] if tb else b.shape[0]) == K
    tm, tn, tk = _tile(M, 1408 if ta else 512, 128), _tile(N, 1408, 128), _tile(K, 1408, 128)
    nk = K // tk
    grid = (M // tm, N // tn, nk)
    dims = (((0 if ta else 1,), (1 if tb else 0,)), ((), ()))
    plan_cls, carried = carry if carry else (None, ())
    nn = 3 if norm_bwd else 0
    nd = 1 if norm_bwd else 0
    nc, nin = len(carried), 2 + (res is not None) + nn
    assert not norm_bwd or (tn == N and out_dtype == F32)

    def body(*refs):
        a_ref, b_ref = refs[:2]
        r_ref = refs[2] if res is not None else None
        o_ref, acc_ref = refs[nin + nc], refs[nin + 2 * nc + 1 + nd]
        row_tile, k = pl.program_id(0), pl.program_id(2)
        if nc:
            plan = plan_cls(refs[nin:nin + nc], refs[nin + nc + 1 + nd:nin + 2 * nc + 1 + nd], refs[nin + 2 * nc + 2 + nd:])
            at = lambda which: functools.reduce(jnp.logical_and, [pl.program_id(d) == (0 if which == 0 else grid[d] - 1)
                                                                 for d in range(3)])
            pl.when(at(0))(plan.start)
        part = _bdot(a_ref[...], b_ref[...], dims)

        @pl.when(k == 0)
        def _():
            acc_ref[...] = part

        @pl.when(k > 0)
        def _():
            acc_ref[...] += part

        @pl.when(k == nk - 1)
        def _():
            o = acc_ref[...]
            if scale is not None:
                o = o * scale
            if r_ref is not None:
                o = o + r_ref[...].astype(F32)
            if norm_bwd:
                x_ref, g_ref, dres_ref = refs[nin - 3:nin]
                xv = x_ref[...]
                r = lax.rsqrt(jnp.mean(xv * xv, axis=-1, keepdims=True) + RMS_EPS)
                xh = xv * r
                dxh = o * g_ref[...]
                _acc_rows(refs[nin + nc + 1], jnp.sum(o * xh, axis=0, keepdims=True), row_tile)
                o = dres_ref[...] + r * (dxh - xh * jnp.mean(dxh * xh, axis=-1, keepdims=True))
            o_ref[...] = o.astype(out_dtype)

        if nc:
            pl.when(at(1))(plan.finish)

    a_spec = pl.BlockSpec((tk, tm), lambda i, j, k: (k, i)) if ta else pl.BlockSpec((tm, tk), lambda i, j, k: (i, k))
    b_spec = pl.BlockSpec((tn, tk), lambda i, j, k: (j, k)) if tb else pl.BlockSpec((tk, tn), lambda i, j, k: (k, j))
    o_spec = pl.BlockSpec((tm, tn), lambda i, j, k: (i, j))
    g_spec = pl.BlockSpec((1, N), lambda i, j, k: (0, 0))
    in_specs = [a_spec, b_spec] + ([o_spec] if res is not None else []) + ([o_spec, g_spec, o_spec] if norm_bwd else [])
    args = (a, b) + ((res,) if res is not None else ()) + (tuple(norm_bwd) if norm_bwd else ())
    out_shape = [jax.ShapeDtypeStruct((M, N), out_dtype)] + [jax.ShapeDtypeStruct((1, N), F32)] * nd
    out_specs = [o_spec] + [g_spec] * nd
    if not nc and not nd:
        return _pcall(
            body, name=name, grid=grid, in_specs=in_specs, out_specs=o_spec, out_shape=out_shape[0],
            scratch_shapes=[pltpu.VMEM((tm, tn), F32)], compiler_params=_params(("parallel", "parallel", "arbitrary")),
        )(*args)
    return _pcall(
        body, name=name, grid=grid, in_specs=in_specs + [_HBM] * nc, out_specs=out_specs + [_HBM] * nc,
        out_shape=out_shape + (plan_cls.out_shapes(carried) if nc else []),
        scratch_shapes=[pltpu.VMEM((tm, tn), F32)] + (plan_cls.sems(nc) if nc else []),
        compiler_params=_params(("arbitrary", "arbitrary", "arbitrary")),
    )(*args, *carried)


def mm_fused(a, bs, finish, out_dtypes, *, name, tb=False, extras=(), carry=None):
    M, K = a.shape
    N = bs[0].shape[0] if tb else bs[0].shape[1]
    tm, tn = _tile(M, 512, 128), _tile(N, 1408, 128)
    grid = (M // tm, N // tn)
    dims = (((1,), (1 if tb else 0,)), ((), ()))
    plan_cls, carried = carry if carry else (None, ())
    nc, nb, nx, no = len(carried), len(bs), len(extras), len(out_dtypes)
    nin = 1 + nb + nx

    def body(*refs):
        a_ref, b_refs, x_refs = refs[0], refs[1:1 + nb], refs[1 + nb:nin]
        o_refs = refs[nin + nc:nin + nc + no]
        if nc:
            plan = plan_cls(refs[nin:nin + nc], refs[nin + nc + no:nin + 2 * nc + no], refs[nin + 2 * nc + no:])
            at = lambda which: jnp.logical_and(*[pl.program_id(d) == (0 if which == 0 else grid[d] - 1) for d in range(2)])
            pl.when(at(0))(plan.start)
        av = a_ref[...]
        outs = finish([_bdot(av, b_ref[...], dims) for b_ref in b_refs], [x_ref[...] for x_ref in x_refs])
        for o_ref, o in zip(o_refs, outs):
            o_ref[...] = o.astype(o_ref.dtype)
        if nc:
            pl.when(at(1))(plan.finish)

    a_spec = pl.BlockSpec((tm, K), lambda i, j: (i, 0))
    b_spec = pl.BlockSpec((tn, K), lambda i, j: (j, 0)) if tb else pl.BlockSpec((K, tn), lambda i, j: (0, j))
    o_spec = pl.BlockSpec((tm, tn), lambda i, j: (i, j))
    out_shape = [jax.ShapeDtypeStruct((M, N), d) for d in out_dtypes]
    if not nc:
        return _pcall(body, name=name, grid=grid, in_specs=[a_spec] + [b_spec] * nb + [o_spec] * nx, out_specs=[o_spec] * no,
                      out_shape=out_shape, compiler_params=_params(("parallel", "parallel")))(a, *bs, *extras)
    return _pcall(body, name=name, grid=grid, in_specs=[a_spec] + [b_spec] * nb + [o_spec] * nx + [_HBM] * nc,
                  out_specs=[o_spec] * no + [_HBM] * nc, out_shape=out_shape + plan_cls.out_shapes(carried),
                  scratch_shapes=plan_cls.sems(nc), compiler_params=_params(("arbitrary", "arbitrary")))(a, *bs, *extras, *carried)


def _swiglu(products, _):
    g, u = products
    return g, u, g * _sigmoid(g) * u


def _swiglu_bwd(products, extras):
    (da,), (gate, up) = products, extras
    gv = gate.astype(F32)
    s = _sigmoid(gv)
    return da * 0.5 * up.astype(F32) * s * (1.0 + gv * (1.0 - s)), da * 0.5 * gv * s


def _row_spec(tr, c):
    return pl.BlockSpec((tr, c), lambda i: (i, 0))


def _full_spec(shape):
    return pl.BlockSpec(shape, lambda i: (0,) * len(shape))


def _acc_rows(ref, val, i):
    @pl.when(i == 0)
    def _():
        ref[...] = val

    @pl.when(i > 0)
    def _():
        ref[...] += val


def rms_fwd(x, g, *, name):
    T, D = x.shape
    tr = _tile(T, 512, 8)

    def body(x_ref, g_ref, h_ref):
        xv = x_ref[...]
        r = lax.rsqrt(jnp.mean(xv * xv, axis=-1, keepdims=True) + RMS_EPS)
        h_ref[...] = (xv * r * g_ref[...]).astype(BF16)

    return _pcall(body, name=name, grid=(T // tr,), in_specs=[_row_spec(tr, D), _full_spec((1, D))],
                  out_specs=_row_spec(tr, D), out_shape=jax.ShapeDtypeStruct((T, D), BF16),
                  compiler_params=_params(("parallel",)))(x, g)


def final_loss(x, tgt, g, *, name):
    T, D = x.shape
    tr = _tile(T, 256, 8)

    def body(x_ref, t_ref, g_ref, dx_ref, dg_ref, loss_ref):
        i = pl.program_id(0)
        xv = x_ref[...]
        r = lax.rsqrt(jnp.mean(xv * xv, axis=-1, keepdims=True) + RMS_EPS)
        xh = xv * r
        e = xh * g_ref[...] - t_ref[...]
        part = 0.5 * jnp.sum(jnp.mean(e * e, axis=-1, keepdims=True), axis=0, keepdims=True)
        dy = e * (1.0 / D)
        dxh = dy * g_ref[...]
        dx_ref[...] = r * (dxh - xh * jnp.mean(dxh * xh, axis=-1, keepdims=True))
        _acc_rows(dg_ref, jnp.sum(dy * xh, axis=0, keepdims=True), i)
        _acc_rows(loss_ref, jnp.broadcast_to(part, (1, LANE)), i)

    return _pcall(body, name=name, grid=(T // tr,),
                  in_specs=[_row_spec(tr, D), _row_spec(tr, D), _full_spec((1, D))],
                  out_specs=[_row_spec(tr, D), _full_spec((1, D)), _full_spec((1, LANE))],
                  out_shape=[jax.ShapeDtypeStruct((T, D), F32), jax.ShapeDtypeStruct((1, D), F32),
                             jax.ShapeDtypeStruct((1, LANE), F32)],
                  compiler_params=_params(("arbitrary",)))(x, tgt, g)


def merge_fwd(br, ba, pg, *, name):
    T, D = br.shape
    tr = _tile(T, 256, 8)

    def body(br_ref, ba_ref, pg_ref, o_ref):
        pgv = pg_ref[...]
        o_ref[...] = (_sigmoid(pgv[:, :D]) * br_ref[...] + _sigmoid(pgv[:, D:]) * ba_ref[...]).astype(BF16)

    return _pcall(body, name=name, grid=(T // tr,), in_specs=[_row_spec(tr, D), _row_spec(tr, D), _row_spec(tr, 2 * D)],
                  out_specs=_row_spec(tr, D), out_shape=jax.ShapeDtypeStruct((T, D), BF16),
                  compiler_params=_params(("parallel",)))(br, ba, pg)


def merge_bwd(dm, br, ba, pg, *, name):
    T, D = br.shape
    tr = _tile(T, 256, 8)

    def body(dm_ref, br_ref, ba_ref, pg_ref, dbr_ref, dba_ref, dpg_ref):
        pgv, dmv = pg_ref[...], dm_ref[...]
        sr, sa = _sigmoid(pgv[:, :D]), _sigmoid(pgv[:, D:])
        dbr_ref[...] = (dmv * sr).astype(BF16)
        dba_ref[...] = (dmv * sa).astype(BF16)
        dpg_ref[:, :D] = dmv * br_ref[...] * sr * (1.0 - sr)
        dpg_ref[:, D:] = dmv * ba_ref[...] * sa * (1.0 - sa)

    return _pcall(body, name=name, grid=(T // tr,),
                  in_specs=[_row_spec(tr, D), _row_spec(tr, D), _row_spec(tr, D), _row_spec(tr, 2 * D)],
                  out_specs=[_row_spec(tr, D), _row_spec(tr, D), _row_spec(tr, 2 * D)],
                  out_shape=[jax.ShapeDtypeStruct((T, D), BF16), jax.ShapeDtypeStruct((T, D), BF16),
                             jax.ShapeDtypeStruct((T, 2 * D), F32)],
                  compiler_params=_params(("parallel",)))(dm, br, ba, pg)


def _rwkv_mix(p, prev, mu, w0, a0, k_k, k_a, wlw, wla, wlg, bd):
    pp = p + (prev - p) * mu
    r, k, v = pp[:, 0:RW], pp[:, RW:2 * RW], pp[:, 2 * RW:3 * RW]
    xw, xa, xg = pp[:, 3 * RW:3 * RW + 128], pp[:, 3 * RW + 128:3 * RW + 256], pp[:, 3 * RW + 256:3 * RW + 384]
    th = jnp.tanh(xw)
    z = -(w0 + _bdot(th, wlw))
    e = jnp.exp(-_softplus(z) - 0.5)
    decay = jnp.exp(-e)
    a = _sigmoid(a0 + _bdot(xa, wla))
    sg = _sigmoid(xg)
    kkr = k * k_k
    n = jnp.sqrt(_segsum(kkr * kkr, bd))
    kk = kkr / jnp.maximum(n, 1e-12)
    k2 = k * (1.0 + (a - 1.0) * k_a)
    return dict(r=r, k=k, v=v, xa=xa, th=th, z=z, e=e, decay=decay, a=a, sg=sg, n=n, kk=kk, k2=k2)


def _seg_matrix(n, shift):
    r = lax.shift_right_logical(lax.broadcasted_iota(jnp.int32, (n, n), 0), shift)
    c = lax.shift_right_logical(lax.broadcasted_iota(jnp.int32, (n, n), 1), shift)
    return jnp.where(r == c, 1.0, 0.0).astype(BF16)


def rwkv_pre_fwd(p, pshift, mu, w0, a0, k_k, k_a, wlw, wla, wlg, *, name):
    T = p.shape[0]
    tr = _tile(T, 256, 8)

    def body(p_ref, ps_ref, mu_ref, w0_ref, a0_ref, kk_ref, ka_ref, wlw_ref, wla_ref, wlg_ref,
             r_ref, w_ref, k_ref, v_ref, a_ref, b_ref, g_ref):
        pv, prev = p_ref[...], ps_ref[...]
        m = _rwkv_mix(pv, prev, mu_ref[...], w0_ref[...], a0_ref[...], kk_ref[...], ka_ref[...],
                      wlw_ref[...], wla_ref[...], wlg_ref[...], _seg_matrix(RW, 6))
        r_ref[...] = m['r']
        w_ref[...] = m['decay']
        k_ref[...] = m['k2']
        v_ref[...] = m['v']
        a_ref[...] = -m['kk']
        b_ref[...] = m['kk'] * m['a']
        g_ref[...] = m['sg']

    vec = _row_spec(tr, RW)
    return _pcall(
        body, name=name, grid=(T // tr,),
        in_specs=[_row_spec(tr, RWKV_PAD), _row_spec(tr, RWKV_PAD), _full_spec((1, RWKV_PAD))] + [_full_spec((1, RW))] * 4
        + [_full_spec((128, RW))] * 3,
        out_specs=[vec] * 6 + [_row_spec(tr, 128)],
        out_shape=[jax.ShapeDtypeStruct((T, RW), F32)] * 6 + [jax.ShapeDtypeStruct((T, 128), F32)],
        compiler_params=_params(("parallel",)),
    )(p, pshift, mu, w0, a0, k_k, k_a, wlw, wla, wlg)


def _group_norm(y, bd):
    mean = _segsum(y, bd) * (1.0 / HEAD_DIM)
    yc = y - mean
    rstd = lax.rsqrt(_segsum(yc * yc, bd) * (1.0 / HEAD_DIM) + GN_EPS)
    return yc * rstd, rstd


def rwkv_post_fwd(y, r, k2, v, sg, wlg, ln_w, ln_b, r_k, *, name):
    T = y.shape[0]
    tr = _tile(T, 256, 8)

    def body(y_ref, r_ref, k_ref, v_ref, sg_ref, wlg_ref, lw_ref, lb_ref, rk_ref, o_ref):
        bd = _seg_matrix(RW, 6)
        yn, _ = _group_norm(y_ref[...], bd)
        s = _segsum(r_ref[...] * k_ref[...] * rk_ref[...], bd)
        g = _bdot(sg_ref[...], wlg_ref[...])
        o_ref[...] = ((yn * lw_ref[...] + lb_ref[...] + s * v_ref[...]) * g).astype(BF16)

    vec = _row_spec(tr, RW)
    return _pcall(body, name=name, grid=(T // tr,),
                  in_specs=[vec] * 4 + [_row_spec(tr, 128), _full_spec((128, RW))] + [_full_spec((1, RW))] * 3, out_specs=vec,
                  out_shape=jax.ShapeDtypeStruct((T, RW), BF16), compiler_params=_params(("parallel",)))(
                      y, r, k2, v, sg, wlg, ln_w, ln_b, r_k)


def rwkv_post_bwd(dyr, y, r, k2, v, sg, wlg, ln_w, ln_b, r_k, *, name, carry=None):
    T = y.shape[0]
    tr = _tile(T, 256, 8)
    nt = T // tr
    plan_cls, carried = carry if carry else (None, ())
    nc = len(carried)

    def body(*refs):
        dyr_ref, y_ref, r_ref, k_ref, v_ref, sg_ref, wlg_ref, lw_ref, lb_ref, rk_ref = refs[:10]
        dy_ref, dz_ref, dg_ref, dlw_ref, dlb_ref = refs[10 + nc:15 + nc]
        i = pl.program_id(0)
        if nc:
            plan = plan_cls(refs[10:10 + nc], refs[15 + nc:15 + 2 * nc], refs[15 + 2 * nc:])
            pl.when(i == 0)(plan.start)
        bd = _seg_matrix(RW, 6)
        yn, rstd = _group_norm(y_ref[...], bd)
        s = _segsum(r_ref[...] * k_ref[...] * rk_ref[...], bd)
        dyrv = dyr_ref[...]
        dg_ref[...] = dyrv * (yn * lw_ref[...] + lb_ref[...] + s * v_ref[...])
        dz = dyrv * _bdot(sg_ref[...], wlg_ref[...])
        dz_ref[...] = dz
        dyn = dz * lw_ref[...]
        inv = 1.0 / HEAD_DIM
        dy_ref[...] = rstd * (dyn - _segsum(dyn, bd) * inv - yn * (_segsum(dyn * yn, bd) * inv))
        _acc_rows(dlw_ref, jnp.sum(dz * yn, axis=0, keepdims=True), i)
        _acc_rows(dlb_ref, jnp.sum(dz, axis=0, keepdims=True), i)
        if nc:
            pl.when(i == nt - 1)(plan.finish)

    vec = _row_spec(tr, RW)
    one = _full_spec((1, RW))
    return _pcall(body, name=name, grid=(nt,),
                  in_specs=[vec] * 5 + [_row_spec(tr, 128), _full_spec((128, RW))] + [one] * 3 + [_HBM] * nc,
                  out_specs=[vec] * 3 + [one] * 2 + [_HBM] * nc,
                  out_shape=[jax.ShapeDtypeStruct((T, RW), F32)] * 3 + [jax.ShapeDtypeStruct((1, RW), F32)] * 2
                  + (plan_cls.out_shapes(carried) if nc else []),
                  scratch_shapes=plan_cls.sems(nc) if nc else [],
                  compiler_params=_params(("arbitrary",)))(dyr, y, r, k2, v, sg, wlg, ln_w, ln_b, r_k, *carried)


def rwkv_pre_bwd(p, pshift, dr_w, dw_w, dk_w, dv_w, da_w, db_w, dz, dg, mu, w0, a0, k_k, k_a, r_k, wlw, wla, wlg, *, name):
    T = p.shape[0]
    tr = _tile(T, 256, 8)
    n = T // tr

    def body(p_ref, ps_ref, dr_ref, dw_ref, dk_ref, dv_ref, da_ref, db_ref, dz_ref, dg_ref,
             mu_ref, w0_ref, a0_ref, kk_ref, ka_ref, rk_ref, wlw_ref, wla_ref, wlg_ref,
             dp_ref, dmu_ref, dw0_ref, da0_ref, dkk_ref, dka_ref, drk_ref, dwlw_ref, dwla_ref, dwlg_ref,
             carry, dpp, acc_w, acc_a, acc_g):
        i = pl.program_id(0)

        @pl.when(i == 0)
        def _():
            carry[...] = jnp.zeros_like(carry)

        pv, prev, mu = p_ref[...], ps_ref[...], mu_ref[...]
        bd = _seg_matrix(RW, 6)
        k_k, k_a, r_k = kk_ref[...], ka_ref[...], rk_ref[...]
        m = _rwkv_mix(pv, prev, mu, w0_ref[...], a0_ref[...], k_k, k_a, wlw_ref[...], wla_ref[...], wlg_ref[...], bd)
        r, k, v, a, kk, k2 = m['r'], m['k'], m['v'], m['a'], m['kk'], m['k2']
        dzv, dgv = dz_ref[...], dg_ref[...]
        s = _segsum(r * k2 * r_k, bd)
        ds = _segsum(dzv * v, bd)
        dr = dr_ref[...] + ds * k2 * r_k
        dk2 = dk_ref[...] + ds * r * r_k
        dv = dv_ref[...] + dzv * s
        dbv = db_ref[...]
        dkk = dbv * a - da_ref[...]
        da = dbv * kk + dk2 * k * k_a
        dk = dk2 * (1.0 + (a - 1.0) * k_a)
        nmax = jnp.maximum(m['n'], 1e-12)
        dkkr = jnp.where(m['n'] > 1e-12, dkk - kk * _segsum(dkk * kk, bd), dkk) / nmax
        dk = dk + dkkr * k_k
        dapre = da * a * (1.0 - a)
        dwpre = dw_ref[...] * m['decay'] * (-m['e']) * _sigmoid(m['z'])
        dth = _bdot(dwpre, wlw_ref[...], _NT)
        dxa = _bdot(dapre, wla_ref[...], _NT)
        dsg = _bdot(dgv, wlg_ref[...], _NT)
        dpp[:, 0:RW] = dr
        dpp[:, RW:2 * RW] = dk
        dpp[:, 2 * RW:3 * RW] = dv
        dpp[:, 3 * RW:3 * RW + 128] = dth * (1.0 - m['th'] * m['th'])
        dpp[:, 3 * RW + 128:3 * RW + 256] = dxa
        dpp[:, 3 * RW + 256:3 * RW + 384] = dsg * m['sg'] * (1.0 - m['sg'])
        d = dpp[...]
        zed = d * mu
        last = lax.broadcasted_iota(jnp.int32, pv.shape, 0) == tr - 1
        dp_ref[...] = d * (1.0 - mu) + jnp.where(last, carry[0:1, :], pltpu.roll(zed, tr - 1, 0))
        carry[...] = zed[0:8, :]

        def colsum(x):
            return jnp.sum(x, axis=0, keepdims=True)

        _acc_rows(dmu_ref, colsum(d * (prev - pv)), i)
        _acc_rows(dw0_ref, colsum(dwpre), i)
        _acc_rows(da0_ref, colsum(dapre), i)
        _acc_rows(dkk_ref, colsum(dkkr * k), i)
        _acc_rows(dka_ref, colsum(dk2 * k * (a - 1.0)), i)
        _acc_rows(drk_ref, colsum(ds * r * k2), i)
        _acc_rows(acc_w, _bdot(m['th'], dwpre, _TN), i)
        _acc_rows(acc_a, _bdot(m['xa'], dapre, _TN), i)
        _acc_rows(acc_g, _bdot(m['sg'], dgv, _TN), i)

        @pl.when(i == n - 1)
        def _():
            dwlw_ref[...] = acc_w[...]
            dwla_ref[...] = acc_a[...]
            dwlg_ref[...] = acc_g[...]

    rev = lambda c: pl.BlockSpec((tr, c), lambda i: (n - 1 - i, 0))
    one, lora = _full_spec((1, RW)), _full_spec((128, RW))
    return _pcall(
        body, name=name, grid=(n,),
        in_specs=[rev(RWKV_PAD), rev(RWKV_PAD)] + [rev(RW)] * 8 + [_full_spec((1, RWKV_PAD))] + [one] * 5 + [lora] * 3,
        out_specs=[rev(RWKV_PAD), _full_spec((1, RWKV_PAD))] + [one] * 5 + [lora] * 3,
        out_shape=[jax.ShapeDtypeStruct((T, RWKV_PAD), F32), jax.ShapeDtypeStruct((1, RWKV_PAD), F32)]
        + [jax.ShapeDtypeStruct((1, RW), F32)] * 5 + [jax.ShapeDtypeStruct((128, RW), F32)] * 3,
        scratch_shapes=[pltpu.VMEM((8, RWKV_PAD), F32), pltpu.VMEM((tr, RWKV_PAD), F32)] + [pltpu.VMEM((128, RW), F32)] * 3,
        compiler_params=_params(("arbitrary",)),
    )(p, pshift, dr_w, dw_w, dk_w, dv_w, da_w, db_w, dz, dg, mu, w0, a0, k_k, k_a, r_k, wlw, wla, wlg)


def _qk_norm(x, g, bd):
    r = lax.rsqrt(_segsum(x * x, bd) * (1.0 / HEAD_DIM) + RMS_EPS)
    return x * r * g, r


def _att_mask(i):
    qi = lax.broadcasted_iota(jnp.int32, (BLOCK, 2 * BLOCK), 0)
    kj = lax.broadcasted_iota(jnp.int32, (BLOCK, 2 * BLOCK), 1)
    band = (kj <= qi + BLOCK) & (kj > qi + BLOCK - WINDOW)
    return band & ((kj >= BLOCK) | (i > 0))


_HQK = (((2,), (2,)), ((0,), (0,)))
_HPV = (((2,), (1,)), ((0,), (0,)))
_HTN = (((1,), (1,)), ((0,), (0,)))


def _heads(x, n):
    return jnp.stack([x[:, h * HEAD_DIM:(h + 1) * HEAD_DIM] for h in range(n)])


def _unheads(x3):
    return jnp.concatenate([x3[h] for h in range(x3.shape[0])], axis=1)


def _kv_heads(x):
    x2 = _heads(x, KVW // HEAD_DIM)
    return jnp.concatenate([x2[g:g + 1] for g in range(KVW // HEAD_DIM) for _ in range(ATT_GROUP)], axis=0)


def _sinks3(sk):
    return jnp.stack([sk[0:1, h:h + 1] for h in range(N_HEADS)])


def _att_probs(q3, k3, mask, sink):
    s = _bdot(q3, k3, _HQK) * (HEAD_DIM ** -0.5)
    s = jnp.where(mask[None], s, NEG_BIG)
    m = jnp.maximum(jnp.max(s, axis=-1, keepdims=True), sink)
    pexp = jnp.exp(s - m)
    psink = jnp.exp(sink - m)
    inv = 1.0 / (jnp.sum(pexp, axis=-1, keepdims=True) + psink)
    return pexp * inv, psink * inv


ATT_SUB = 2


def _att_blocks(sub):
    cur = pl.BlockSpec((sub * BLOCK, ATT_COLS), lambda i: (i, 0))
    prev = pl.BlockSpec((BLOCK, ATT_COLS), lambda i: (jnp.maximum(i * sub - 1, 0), 0))
    return cur, prev


def _att_sub(cur_all, prev_first, j):
    cur = cur_all[j * BLOCK:(j + 1) * BLOCK]
    return cur, (prev_first if j == 0 else cur_all[(j - 1) * BLOCK:j * BLOCK])


def _att_qkv(cur, prev, qn_g, kn_g):
    bq, bk = _seg_matrix(RW, 6), _seg_matrix(KVW, 6)
    qn, rq = _qk_norm(cur[:, 0:RW], qn_g, bq)
    kcur, rkc = _qk_norm(cur[:, RW:RW + KVW], kn_g, bk)
    kprev, _ = _qk_norm(prev[:, RW:RW + KVW], kn_g, bk)
    kc = jnp.concatenate([kprev, kcur], axis=0)
    vc = jnp.concatenate([prev[:, RW + KVW:], cur[:, RW + KVW:]], axis=0)
    return qn, rq, kc, vc, rkc


def att_fwd(pa, qn_g, kn_g, sinks, *, name):
    T = pa.shape[0]
    sub = ATT_SUB if (T // BLOCK) % ATT_SUB == 0 else 1
    n = T // (sub * BLOCK)

    def body(cur_ref, prev_ref, qg_ref, kg_ref, sk_ref, o_ref):
        i = pl.program_id(0)
        cur_all, prev_first = cur_ref[...], prev_ref[...]
        for j in range(sub):
            cur, prev = _att_sub(cur_all, prev_first, j)
            qn, _, kc, vc, _ = _att_qkv(cur, prev, qg_ref[...], kg_ref[...])
            probs, _ = _att_probs(_heads(qn, N_HEADS), _kv_heads(kc), _att_mask(i * sub + j), _sinks3(sk_ref[...]))
            o_ref[j * BLOCK:(j + 1) * BLOCK, :] = _unheads(_bdot(probs, _kv_heads(vc), _HPV))

    cur, prev = _att_blocks(sub)
    return _pcall(body, name=name, grid=(n,),
                  in_specs=[cur, prev, _full_spec((1, RW)), _full_spec((1, KVW)), _full_spec((1, LANE))],
                  out_specs=pl.BlockSpec((sub * BLOCK, RW), lambda i: (i, 0)), out_shape=jax.ShapeDtypeStruct((T, RW), F32),
                  compiler_params=_params(("parallel",)))(pa, pa, qn_g, kn_g, sinks)


def att_bwd(pa, do, qn_g, kn_g, sinks, *, name):
    T = pa.shape[0]
    sub = ATT_SUB if (T // BLOCK) % ATT_SUB == 0 else 1
    n = T // (sub * BLOCK)

    def one_block(cur, prev, do, blk, qg_ref, kg_ref, sk_ref, rows, dq_ref, dko_ref, dkn_ref, dvo_ref, dvn_ref):
        qn, rq, kc, vc, _ = _att_qkv(cur, prev, qg_ref[...], kg_ref[...])
        q3, k3, v3, do3 = _heads(qn, N_HEADS), _kv_heads(kc), _kv_heads(vc), _heads(do, N_HEADS)
        probs, psink = _att_probs(q3, k3, _att_mask(blk), _sinks3(sk_ref[...]))
        dprobs = _bdot(do3, v3, _HQK)
        delta = jnp.sum(probs * dprobs, axis=-1, keepdims=True)
        ds = probs * (dprobs - delta) * (HEAD_DIM ** -0.5)
        dsink3 = -jnp.sum(psink * delta, axis=1, keepdims=True)
        lane = lax.broadcasted_iota(jnp.int32, (1, LANE), 1)
        dsink = jnp.zeros((1, LANE), F32)
        for h in range(N_HEADS):
            dsink = dsink + jnp.where(lane == h, dsink3[h], 0.0)
        dqn = _unheads(_bdot(ds, k3, _HPV))

        def per_kv_head(x3):
            groups = [sum(x3[g * ATT_GROUP + j] for j in range(ATT_GROUP)) for g in range(KVW // HEAD_DIM)]
            return jnp.concatenate(groups, axis=1)

        dk, dv = per_kv_head(_bdot(ds, q3, _HTN)), per_kv_head(_bdot(probs, do3, _HTN))
        dkn_ref[rows, :], dko_ref[rows, :] = dk[0:BLOCK], dk[BLOCK:]
        dvn_ref[rows, :], dvo_ref[rows, :] = dv[0:BLOCK], dv[BLOCK:]
        qhat = cur[:, 0:RW] * rq
        dqh = dqn * qg_ref[...]
        dq_ref[rows, :] = rq * (dqh - qhat * (_segsum(dqh * qhat, _seg_matrix(RW, 6)) * (1.0 / HEAD_DIM)))
        prod = dqn * qhat
        fold = prod[:, 0:HEAD_DIM]
        for h in range(1, N_HEADS):
            fold = fold + prod[:, h * HEAD_DIM:(h + 1) * HEAD_DIM]
        return jnp.sum(fold, axis=0, keepdims=True), dsink

    def body(cur_ref, prev_ref, do_ref, qg_ref, kg_ref, sk_ref,
             dq_ref, dko_ref, dkn_ref, dvo_ref, dvn_ref, dqg_ref, dsk_ref):
        i = pl.program_id(0)
        cur_all, prev_first, do_all = cur_ref[...], prev_ref[...], do_ref[...]
        dqg, dsk = None, None
        for j in range(sub):
            cur, prev = _att_sub(cur_all, prev_first, j)
            rows = slice(j * BLOCK, (j + 1) * BLOCK)
            g, s = one_block(cur, prev, do_all[rows], i * sub + j, qg_ref, kg_ref, sk_ref, rows,
                             dq_ref, dko_ref, dkn_ref, dvo_ref, dvn_ref)
            dqg, dsk = (g, s) if dqg is None else (dqg + g, dsk + s)
        _acc_rows(dqg_ref, dqg, i)
        _acc_rows(dsk_ref, dsk, i)

    cur, prev = _att_blocks(sub)
    kvb = pl.BlockSpec((sub * BLOCK, KVW), lambda i: (i, 0))
    qb = pl.BlockSpec((sub * BLOCK, RW), lambda i: (i, 0))
    return _pcall(body, name=name, grid=(n,),
                  in_specs=[cur, prev, qb, _full_spec((1, RW)), _full_spec((1, KVW)), _full_spec((1, LANE))],
                  out_specs=[qb, kvb, kvb, kvb, kvb, _full_spec((1, HEAD_DIM)), _full_spec((1, LANE))],
                  out_shape=[jax.ShapeDtypeStruct((T, RW), F32)] + [jax.ShapeDtypeStruct((T, KVW), F32)] * 4
                  + [jax.ShapeDtypeStruct((1, HEAD_DIM), F32), jax.ShapeDtypeStruct((1, LANE), F32)],
                  compiler_params=_params(("arbitrary",)))(pa, pa, do, qn_g, kn_g, sinks)


def att_kv_bwd(pa, dq, dko, dkn, dvo, dvn, kn_g, *, name):
    T = pa.shape[0]
    n = T // BLOCK

    def body(pa_ref, dq_ref, dko_ref, dkn_ref, dvo_ref, dvn_ref, kg_ref, dpa_ref, dkg_ref):
        i = pl.program_id(0)
        more = i < n - 1
        dkn_tot = dko_ref[...] + jnp.where(more, dkn_ref[...], 0.0)
        dv_tot = dvo_ref[...] + jnp.where(more, dvn_ref[...], 0.0)
        kraw = pa_ref[:, RW:RW + KVW]
        bk = _seg_matrix(KVW, 6)
        _, rk = _qk_norm(kraw, kg_ref[...], bk)
        khat = kraw * rk
        dkh = dkn_tot * kg_ref[...]
        dpa_ref[:, 0:RW] = dq_ref[...]
        dpa_ref[:, RW:RW + KVW] = rk * (dkh - khat * (_segsum(dkh * khat, bk) * (1.0 / HEAD_DIM)))
        dpa_ref[:, RW + KVW:] = dv_tot
        prod = dkn_tot * khat
        _acc_rows(dkg_ref, jnp.sum(prod[:, 0:HEAD_DIM] + prod[:, HEAD_DIM:], axis=0, keepdims=True), i)

    kvb = pl.BlockSpec((BLOCK, KVW), lambda i: (i, 0))
    nxt = pl.BlockSpec((BLOCK, KVW), lambda i: (jnp.minimum(i + 1, n - 1), 0))
    return _pcall(body, name=name, grid=(n,),
                  in_specs=[pl.BlockSpec((BLOCK, ATT_COLS), lambda i: (i, 0)), pl.BlockSpec((BLOCK, RW), lambda i: (i, 0)),
                            kvb, nxt, kvb, nxt, _full_spec((1, KVW))],
                  out_specs=[pl.BlockSpec((BLOCK, ATT_COLS), lambda i: (i, 0)), _full_spec((1, HEAD_DIM))],
                  out_shape=[jax.ShapeDtypeStruct((T, ATT_COLS), F32), jax.ShapeDtypeStruct((1, HEAD_DIM), F32)],
                  compiler_params=_params(("arbitrary",)))(pa, dq, dko, dkn, dvo, dvn, kn_g)


WKV_CHUNK = 128
WKV_GROUP = 32


def _diag_mask():
    i = lax.broadcasted_iota(jnp.int32, (HEAD_DIM, RW), 0)
    j = lax.broadcasted_iota(jnp.int32, (HEAD_DIM, RW), 1) & (HEAD_DIM - 1)
    return i == j


def _heads_matrix():
    head = jnp.arange(RW // 2) // HEAD_DIM
    bd = (head[:, None] == head[None, :]).astype(BF16)
    return jnp.concatenate([bd, bd], axis=0)


def _headsums(xs, pieces, bd2):
    half = RW // 2
    bd = bd2[:pieces * half]
    rows = []
    for x in xs:
        parts, rest = [], x
        for n in range(pieces):
            p = rest.astype(BF16)
            parts.append(p)
            if n + 1 < pieces:
                rest = rest - p.astype(F32)
        for sl in (slice(0, half), slice(half, RW)):
            rows.append(jnp.concatenate([p[:, sl] for p in parts], axis=1))
    out = lax.dot_general(jnp.concatenate(rows, axis=0), bd, (((1,), (0,)), ((), ())), preferred_element_type=F32)
    return [jnp.concatenate([out[2 * n * HEAD_DIM:(2 * n + 1) * HEAD_DIM], out[(2 * n + 1) * HEAD_DIM:(2 * n + 2) * HEAD_DIM]],
                            axis=1) for n in range(len(xs))]


def _headsum(x):
    low = lax.broadcasted_iota(jnp.int32, (HEAD_DIM, LANE), 1) < HEAD_DIM
    tiles = []
    for c in range(RW // LANE):
        xt = x[:, c * LANE:(c + 1) * LANE]
        s_lo = jnp.sum(jnp.where(low, xt, 0.0), axis=1, keepdims=True)
        s_hi = jnp.sum(jnp.where(low, 0.0, xt), axis=1, keepdims=True)
        tiles.append(jnp.where(low, s_lo, s_hi))
    return jnp.concatenate(tiles, axis=1)


def _cols(rows, diag, bd2, pieces=2):
    return _headsums([jnp.where(diag, r, 0.0) for r in rows], pieces, bd2)


def _row(x, diag):
    return jnp.sum(jnp.where(diag, x, 0.0), axis=0, keepdims=True)


def wkv_fwd(r, w, k, v, a, b, *, name, gather=()):
    T = r.shape[0]
    ch = min(WKV_CHUNK, T)
    ngroups = ch // WKV_GROUP
    nchunks = T // ch
    ng = len(gather)

    def body(*refs):
        r_ref, w_ref, k_ref, v_ref, a_ref, b_ref, bd_ref = refs[:7]
        y_ref, st_ref = refs[7 + ng:9 + ng]
        s_scr = refs[9 + 2 * ng]
        step = pl.program_id(0)
        if ng:
            plan = _GatherPlan(refs[7:7 + ng], refs[9 + ng:9 + 2 * ng], refs[10 + 2 * ng:])
            pl.when(step == 0)(plan.start)
            pl.when(step == nchunks // 2)(plan.relay)

        @pl.when(step == 0)
        def _():
            s_scr[...] = jnp.zeros_like(s_scr)

        diag, bd2 = _diag_mask(), bd_ref[...]

        def group(gi, S):
            t0 = pl.multiple_of(gi * WKV_GROUP, WKV_GROUP)
            rows = pl.ds(t0, WKV_GROUP)
            R, W, K, V, A, B = (ref[rows, :] for ref in (r_ref, w_ref, k_ref, v_ref, a_ref, b_ref))
            vcols = _cols([V[s:s + 1] for s in range(WKV_GROUP)], diag, bd2, 1)
            yrows = []
            for s in range(WKV_GROUP):
                sa = _headsum(S * A[s:s + 1])
                S = S * W[s:s + 1] + sa * B[s:s + 1] + vcols[s] * K[s:s + 1]
                st_ref[t0 + s] = S
                yrows.append(_row(_headsums([S * R[s:s + 1]], 1, bd2)[0], diag))
            y_ref[rows, :] = jnp.concatenate(yrows, axis=0)
            return S

        s_scr[...] = lax.fori_loop(0, ngroups, group, s_scr[...])
        if ng:
            pl.when(step == nchunks - 1)(plan.finish_relayed)

    vec = pl.BlockSpec((ch, RW), lambda c: (c, 0))
    return _pcall(
        body, name=name, grid=(nchunks,), in_specs=[vec] * 6 + [_full_spec((RW, RW // 2))] + [_HBM] * ng,
        out_specs=[vec, pl.BlockSpec((ch, HEAD_DIM, RW), lambda c: (c, 0, 0))] + [_HBM] * ng,
        out_shape=[jax.ShapeDtypeStruct((T, RW), F32), jax.ShapeDtypeStruct((T, HEAD_DIM, RW), F32)] + _gathered_shapes(gather),
        scratch_shapes=[pltpu.VMEM((HEAD_DIM, RW), F32)] + (_GatherPlan.sems(ng) if ng else []),
        compiler_params=_params(("arbitrary",)),
    )(r, w, k, v, a, b, _heads_matrix(), *gather)


def wkv_bwd(r, w, k, v, a, b, dy, states, *, name, exchange=()):
    T = r.shape[0]
    ch = min(WKV_CHUNK, T)
    nchunks = T // ch
    ngroups = ch // WKV_GROUP
    ne = len(exchange)

    def body(*refs):
        r_ref, w_ref, k_ref, v_ref, a_ref, b_ref, dy_ref, st_ref, stp_ref, bd_ref = refs[:10]
        dr_ref, dw_ref, dk_ref, dv_ref, da_ref, db_ref = refs[10 + ne:16 + ne]
        ds_scr = refs[16 + 2 * ne]
        step = pl.program_id(0)
        if ne:
            plan = _ExchangePlan(refs[10:10 + ne], refs[16 + ne:16 + 2 * ne], refs[17 + 2 * ne:])
            pl.when(step == 0)(plan.start)

        @pl.when(step == 0)
        def _():
            ds_scr[...] = jnp.zeros_like(ds_scr)

        has_prev_chunk = step < nchunks - 1
        diag, bd2 = _diag_mask(), bd_ref[...]
        colsum = lambda x: jnp.sum(x, axis=0, keepdims=True)

        def group(gj, dS):
            gi = ngroups - 1 - gj
            t0 = pl.multiple_of(gi * WKV_GROUP, WKV_GROUP)
            rows = pl.ds(t0, WKV_GROUP)
            R, W, K, V, A, B, DY = (ref[rows, :] for ref in (r_ref, w_ref, k_ref, v_ref, a_ref, b_ref, dy_ref))
            before = jnp.where(gi > 0, st_ref[jnp.maximum(t0 - 1, 0)], jnp.where(has_prev_chunk, stp_ref[0], 0.0))
            prev_state = lambda s: st_ref[t0 + s - 1] if s > 0 else before
            steps = range(WKV_GROUP)
            dycols = _cols([DY[s:s + 1] for s in steps], diag, bd2, 1)
            vcols = _cols([V[s:s + 1] for s in steps], diag, bd2, 1)
            sas = _headsums([prev_state(s) * A[s:s + 1] for s in steps], 1, bd2)
            got = [[None] * WKV_GROUP for _ in range(6)]
            for s in reversed(steps):
                Sp = prev_state(s)
                dS = dS + dycols[s] * R[s:s + 1]
                got[0][s] = colsum(st_ref[t0 + s] * dycols[s])
                got[3][s] = _row(_headsums([dS * K[s:s + 1]], 1, bd2)[0], diag)
                got[2][s] = colsum(dS * vcols[s])
                dsa = _headsum(dS * B[s:s + 1])
                got[5][s] = colsum(dS * sas[s])
                got[1][s] = colsum(dS * Sp)
                got[4][s] = colsum(Sp * dsa)
                dS = dS * W[s:s + 1] + dsa * A[s:s + 1]
            for q, ref in enumerate((dr_ref, dw_ref, dk_ref, dv_ref, da_ref, db_ref)):
                ref[rows, :] = jnp.concatenate(got[q], axis=0)
            return dS

        ds_scr[...] = lax.fori_loop(0, ngroups, group, ds_scr[...])
        if ne:
            pl.when(step == nchunks - 1)(plan.finish)

    vec = pl.BlockSpec((ch, RW), lambda c: (nchunks - 1 - c, 0))
    st_spec = pl.BlockSpec((ch, HEAD_DIM, RW), lambda c: (nchunks - 1 - c, 0, 0))
    stp_spec = pl.BlockSpec((1, HEAD_DIM, RW), lambda c: (jnp.maximum((nchunks - 1 - c) * ch - 1, 0), 0, 0))
    return _pcall(
        body, name=name, grid=(nchunks,), in_specs=[vec] * 7 + [st_spec, stp_spec, _full_spec((RW, RW // 2))] + [_HBM] * ne,
        out_specs=[vec] * 6 + [_HBM] * ne,
        out_shape=[jax.ShapeDtypeStruct((T, RW), F32)] * 6 + [jax.ShapeDtypeStruct(e.shape, e.dtype) for e in exchange],
        scratch_shapes=[pltpu.VMEM((HEAD_DIM, RW), F32)] + (_ExchangePlan.sems(ne) if ne else []),
        compiler_params=_params(("arbitrary",)),
    )(r, w, k, v, a, b, dy, states, states, _heads_matrix(), *exchange)


_HBM = pl.BlockSpec(memory_space=pltpu.HBM)
_MESH = pl.DeviceIdType.MESH


def _place():
    x, y, c = lax.axis_index("x"), lax.axis_index("y"), lax.axis_index("c")
    return x, y, c, [(1 - x, y), (x, 1 - y), (1 - x, 1 - y)]


def _remote(src, dst, send_sem, recv_sem, to):
    return pltpu.make_async_remote_copy(src_ref=src, dst_ref=dst, send_sem=send_sem, recv_sem=recv_sem, device_id=to,
                                        device_id_type=_MESH)


def _dma_sems(*counts):
    return [pltpu.SemaphoreType.DMA((n,)) for n in counts]


class _GatherPlan:
    def __init__(self, ins, outs, sems):
        self.ins, self.outs, self.n = ins, outs, len(ins)
        self.ici_send, self.ici_recv, self.d2d_send, self.d2d_recv, self.local_sems = sems
        x, y, c, chips = _place()
        self.c, self.me, self.sibling = c, 2 * x + y, (x, y, 1 - c)
        self.peers = [(2 * qx + qy, (qx, qy, c)) for qx, qy in chips]

    @staticmethod
    def sems(n):
        return _dma_sems(3 * n, 3 * n, 3 * n, 3 * n, n)

    def _half(self, i, which):
        rh = self.ins[i].shape[0] // 2
        return pl.ds(which * rh, rh)

    def _local(self, i):
        return pltpu.make_async_copy(self.ins[i], self.outs[i].at[self.me], self.local_sems.at[i])

    def _send(self, i, j):
        k, mine = 3 * i + j, self._half(i, self.c)
        return _remote(self.ins[i].at[mine], self.outs[i].at[self.me, mine], self.ici_send.at[k], self.ici_recv.at[k],
                       self.peers[j][1])

    def _landed(self, i, j):
        k, piece = 3 * i + j, self.outs[i].at[self.peers[j][0], self._half(i, self.c)]
        return _remote(piece, piece, self.ici_send.at[k], self.ici_recv.at[k], self.peers[j][1])

    def _pass(self, i, j, which):
        k, piece = 3 * i + j, self.outs[i].at[self.peers[j][0], self._half(i, which)]
        return _remote(piece, piece, self.d2d_send.at[k], self.d2d_recv.at[k], self.sibling)

    def _all(self):
        return [(i, j) for i in range(self.n) for j in range(3)]

    def start(self):
        for i in range(self.n):
            self._local(i).start()
        for i, j in self._all():
            self._send(i, j).start()

    def relay(self):
        for i, j in self._all():
            self._landed(i, j).wait_recv()
            self._pass(i, j, self.c).start()

    def finish_relayed(self):
        for i, j in self._all():
            self._pass(i, j, 1 - self.c).wait_recv()
        for i, j in self._all():
            self._send(i, j).wait_send()
            self._pass(i, j, self.c).wait_send()
        for i in range(self.n):
            self._local(i).wait()

    def finish(self):
        self.relay()
        self.finish_relayed()

    @staticmethod
    def out_shapes(shards):
        return _gathered_shapes(shards)


def _gathered_shapes(shards):
    return [jax.ShapeDtypeStruct((N_CHIPS,) + s.shape, s.dtype) for s in shards]


def gather_weights(shards, *, name):
    n = len(shards)

    def body(*refs):
        plan = _GatherPlan(refs[:n], refs[n:2 * n], refs[2 * n:])
        plan.start()
        plan.finish()

    return _pcall(body, name=name, in_specs=[_HBM] * n, out_specs=[_HBM] * n, out_shape=_gathered_shapes(shards),
                  scratch_shapes=_GatherPlan.sems(n), compiler_params=_params())(*shards)


class _SiblingPlan:
    halves = True

    def __init__(self, ins, outs, sems):
        self.ins, self.outs, self.n = ins, outs, len(ins)
        self.send_sems, self.recv_sems = sems
        x, y, c, _ = _place()
        self.c, self.sibling = c, (x, y, 1 - c)

    @staticmethod
    def sems(n):
        return _dma_sems(n, n)

    @classmethod
    def out_shapes(cls, arrays):
        if not cls.halves:
            return [jax.ShapeDtypeStruct(a.shape, a.dtype) for a in arrays]
        return [jax.ShapeDtypeStruct((a.shape[0], a.shape[1] // 2, a.shape[2]), a.dtype) for a in arrays]

    def _copy(self, i):
        src = self.ins[i]
        if self.halves:
            rh = src.shape[1] // 2
            src = src.at[:, pl.ds((1 - self.c) * rh, rh)]
        return _remote(src, self.outs[i], self.send_sems.at[i], self.recv_sems.at[i], self.sibling)

    def start(self):
        for i in range(self.n):
            self._copy(i).start()

    def finish(self):
        for i in range(self.n):
            self._copy(i).wait_recv()
        for i in range(self.n):
            self._copy(i).wait_send()


class _SiblingWhole(_SiblingPlan):
    halves = False


def to_sibling(arrays, take_other_half, *, name):
    n = len(arrays)
    plan_cls = _SiblingPlan if take_other_half else _SiblingWhole

    def body(*refs):
        plan = plan_cls(refs[:n], refs[n:2 * n], refs[2 * n:])
        plan.start()
        plan.finish()

    return _pcall(body, name=name, in_specs=[_HBM] * n, out_specs=[_HBM] * n, out_shape=plan_cls.out_shapes(arrays),
                  scratch_shapes=plan_cls.sems(n), compiler_params=_params())(*arrays)


def exchange_chips(arrays, *, name):
    n = len(arrays)

    def body(*refs):
        plan = _ExchangePlan(refs[:n], refs[n:2 * n], refs[2 * n:])
        plan.start()
        plan.finish()

    return _pcall(body, name=name, in_specs=[_HBM] * n, out_specs=[_HBM] * n,
                  out_shape=[jax.ShapeDtypeStruct(a.shape, a.dtype) for a in arrays],
                  scratch_shapes=_ExchangePlan.sems(n), compiler_params=_params())(*arrays)


class _ExchangePlan:
    def __init__(self, ins, outs, sems):
        self.ins, self.outs, self.n = ins, outs, len(ins)
        self.send_sems, self.recv_sems, self.local_sems = sems
        x, y, c, chips = _place()
        self.me = 2 * x + y
        self.peers = [(2 * qx + qy, (qx, qy, c)) for qx, qy in chips]

    @staticmethod
    def sems(n):
        return _dma_sems(3 * n, 3 * n, n)

    @staticmethod
    def out_shapes(arrays):
        return [jax.ShapeDtypeStruct(a.shape, a.dtype) for a in arrays]

    def _local(self, i):
        return pltpu.make_async_copy(self.ins[i].at[self.me], self.outs[i].at[self.me], self.local_sems.at[i])

    def _send(self, i, j):
        k = 3 * i + j
        return _remote(self.ins[i].at[self.peers[j][0]], self.outs[i].at[self.me], self.send_sems.at[k], self.recv_sems.at[k],
                       self.peers[j][1])

    def _landed(self, i, j):
        k, piece = 3 * i + j, self.outs[i].at[self.peers[j][0]]
        return _remote(piece, piece, self.send_sems.at[k], self.recv_sems.at[k], self.peers[j][1])

    def start(self):
        for i in range(self.n):
            self._local(i).start()
            for j in range(3):
                self._send(i, j).start()

    def finish(self):
        for i in range(self.n):
            for j in range(3):
                self._landed(i, j).wait_recv()
        for i in range(self.n):
            for j in range(3):
                self._send(i, j).wait_send()
            self._local(i).wait()


def _core_index():
    return lax.axis_index("c").astype(jnp.int32).reshape(1)


def pair_sum(g, theirs, wire_dtype, *, name):
    _, R, C = g.shape
    rh = R // 2
    tr = _tile(rh, 256, 16)
    nt = rh // tr

    def body(c_ref, g_ref, t_ref, q_ref, qw_ref):
        q = g_ref[...] + t_ref[...]
        q_ref[...] = q
        qw_ref[...] = q.astype(wire_dtype)

    blk = pl.BlockSpec((1, tr, C), lambda b, i, c_ref: (b, i, 0))
    mine = pl.BlockSpec((1, tr, C), lambda b, i, c_ref: (b, c_ref[0] * nt + i, 0))
    grid_spec = pltpu.PrefetchScalarGridSpec(num_scalar_prefetch=1, grid=(N_CHIPS, nt), in_specs=[mine, blk], out_specs=[blk, blk])
    return _pcall(body, name=name, grid_spec=grid_spec,
                  out_shape=[jax.ShapeDtypeStruct((N_CHIPS, rh, C), F32), jax.ShapeDtypeStruct((N_CHIPS, rh, C), wire_dtype)],
                  compiler_params=_params(("parallel", "parallel")))(_core_index(), g, theirs)


def half_sum(own, landed, *, name):
    _, rh, C = own.shape
    tr = _tile(rh, 256, 16)

    def body(me_ref, own_ref, land_ref, o_ref):
        total = None
        for p in range(N_CHIPS):
            term = jnp.where(me_ref[0] == p, own_ref[p], land_ref[p].astype(F32))
            total = term if total is None else total + term
        o_ref[...] = total

    blk = pl.BlockSpec((N_CHIPS, tr, C), lambda i, me_ref: (0, i, 0))
    grid_spec = pltpu.PrefetchScalarGridSpec(num_scalar_prefetch=1, grid=(rh // tr,), in_specs=[blk, blk],
                                             out_specs=pl.BlockSpec((tr, C), lambda i, me_ref: (i, 0)))
    me = (2 * lax.axis_index("x") + lax.axis_index("y")).astype(jnp.int32).reshape(1)
    return _pcall(body, name=name, grid_spec=grid_spec, out_shape=jax.ShapeDtypeStruct((rh, C), F32),
                  compiler_params=_params(("parallel",)))(me, own, landed)


def adamw(w, m, v, mine, theirs, *, name):
    _, R, C = w.shape
    rh = R // 2
    tr = _tile(rh, 256, 8)
    nt = rh // tr

    def body(c_ref, w_ref, m_ref, v_ref, a_ref, b_ref, g_ref, d_ref, nm_ref, nv_ref):
        is_mine = (pl.program_id(0) // nt) == c_ref[0]
        g = jnp.where(is_mine, a_ref[...], b_ref[...])
        g_ref[...] = g
        nm = ADAM_B1 * m_ref[...] + (1.0 - ADAM_B1) * g
        nv = ADAM_B2 * v_ref[...] + (1.0 - ADAM_B2) * (g * g)
        nm_ref[...] = nm
        nv_ref[...] = nv
        m_hat = nm / (1.0 - ADAM_B1 ** ADAM_STEP)
        v_hat = nv / (1.0 - ADAM_B2 ** ADAM_STEP)
        d_ref[...] = -ADAM_LR * (m_hat / (jnp.sqrt(v_hat) + ADAM_EPS) + ADAM_WD * w_ref[...])

    full = pl.BlockSpec((None, tr, C), lambda i, c_ref: (0, i, 0))
    a_spec = pl.BlockSpec((tr, C), lambda i, c_ref: (jnp.clip(i - c_ref[0] * nt, 0, nt - 1), 0))
    b_spec = pl.BlockSpec((tr, C), lambda i, c_ref: (jnp.clip(i - (1 - c_ref[0]) * nt, 0, nt - 1), 0))
    grid_spec = pltpu.PrefetchScalarGridSpec(num_scalar_prefetch=1, grid=(2 * nt,), in_specs=[full] * 3 + [a_spec, b_spec],
                                             out_specs=[full] * 4)
    return _pcall(body, name=name, grid_spec=grid_spec, out_shape=[jax.ShapeDtypeStruct((1, R, C), F32)] * 4,
                  compiler_params=_params(("arbitrary",)))(_core_index(), w, m, v, mine, theirs)


def _to_blocks(full, axis):
    r, c = full.shape
    if axis == 1:
        return full.reshape(r, N_CHIPS, c // N_CHIPS).transpose(1, 0, 2)
    return full.reshape(N_CHIPS, r // N_CHIPS, c)


def _from_blocks(blocks, axis):
    _, r, c = blocks.shape
    if axis == 1:
        return blocks.transpose(1, 0, 2).reshape(r, N_CHIPS * c)
    return blocks.reshape(N_CHIPS * r, c)


def _ffn_fwd(x, norm, wg_t, wu_t, wd, tag):
    h = rms_fwd(x, norm, name=tag + "_norm")
    gate, up, act = mm_fused(h, [wg_t, wu_t], _swiglu, [BF16] * 3, tb=True, name=tag + "_gate_up")
    out = mm(act, wd, scale=0.5, res=x, name=tag + "_down")
    return out, (h, gate, up, act)


def _ffn_bwd(dout, x, saved, norm, wg_t, wu_t, wd, tag, carry=None, reduce=None):
    h, gate, up, act = saved
    dgate, dup, *carried = mm_fused(dout, [wd], _swiglu_bwd, [BF16] * 2, tb=True, extras=[gate, up], name=tag + "_dact",
                                    carry=carry)
    dwd = mm(act, dout, ta=True, scale=0.5, name=tag + "_dwd")
    if reduce:
        down_blocks = _to_blocks(dwd, 0)
        dwg_t, down_from_sibling = mm(dgate, h, ta=True, name=tag + "_dwg", carry=(_SiblingPlan, [down_blocks]))
        down_pair = pair_sum(down_blocks, down_from_sibling, BF16, name="pair_sum_" + reduce[2])
        dwu_t, down_landed = mm(dup, h, ta=True, name=tag + "_dwu", carry=(_ExchangePlan, [down_pair[1]]))
        blocks = [_to_blocks(g, 0) for g in (dwg_t, dwu_t)]
        dh, *from_sibling = mm(dgate, wg_t, name=tag + "_dh_gate", carry=(_SiblingPlan, blocks))
        pair = [pair_sum(g, t, BF16, name="pair_sum_" + n) for g, t, n in zip(blocks, from_sibling, reduce[:2])]
        dx, dnorm, *landed = mm(dup, wu_t, res=dh, name=tag + "_dh_up", norm_bwd=(x, norm, dout),
                                carry=(_ExchangePlan, [q for _, q in pair]))
        return dx, dnorm, dwg_t, dwu_t, dwd, carried, (pair + [down_pair], landed + [down_landed])
    dwg_t = mm(dgate, h, ta=True, name=tag + "_dwg")
    dwu_t = mm(dup, h, ta=True, name=tag + "_dwu")
    dh = mm(dgate, wg_t, name=tag + "_dh_gate")
    dx, dnorm = mm(dup, wu_t, res=dh, name=tag + "_dh_up", norm_bwd=(x, norm, dout))
    return dx, dnorm, dwg_t, dwu_t, dwd, carried, None


TRANSPOSED = ('ffn1_w_gate', 'ffn1_w_up', 'ffn2_w_gate', 'ffn2_w_up')
FIRST_WEIGHTS = ['ffn1_w_gate', 'ffn1_w_up', 'ffn1_w_down']
MID_WEIGHTS = ['w_in', 'rwkv_w_lora_up', 'rwkv_a_lora_up', 'rwkv_g_lora_up']
LATE_WEIGHTS = ['w_branch_rwkv', 'w_branch_attn', 'w_out', 'ffn2_w_gate', 'ffn2_w_up', 'ffn2_w_down']


def _pair_sums(names, blocks, tag):
    from_sibling = to_sibling(blocks, True, name=tag + "_grads_to_sibling")
    return [pair_sum(g, t, F32 if n == 'small' else BF16, name="pair_sum_" + n)
            for g, t, n in zip(blocks, from_sibling, names)]


def _step(A):
    x, tgt = A['x'][0], A['loss_target'][0]
    T = x.shape[0]
    w = {n: A[n][0] for n in WEIGHT_NAMES}
    row = lambda a: a.reshape(1, -1)

    axis_of = {n: (0 if n in TRANSPOSED else axis) for n, axis in BIG}
    natural = lambda n, a: jnp.swapaxes(a, 1, 2) if n in TRANSPOSED else a
    shard = lambda n: natural(n, A[n])[0].astype(BF16)
    n1, nmix, n2, nfin = (row(w[n]) for n in ('ffn1_norm', 'mix_norm', 'ffn2_norm', 'final_norm'))
    gathered = gather_weights([shard(n) for n in FIRST_WEIGHTS[:2]], name="gather_weights")
    full = {n: _from_blocks(b, axis_of[n]) for n, b in zip(FIRST_WEIGHTS[:2], gathered)}
    h1 = rms_fwd(x, n1, name="ffn1_norm")
    w_in_shard = shard('w_in')
    upper, lower = w_in_shard[:D_MODEL // 2], w_in_shard[D_MODEL // 2:]
    gate1, up1, act1, down_blocks, upper_blocks = mm_fused(
        h1, [full['ffn1_w_gate'], full['ffn1_w_up']], _swiglu, [BF16] * 3, tb=True, name="ffn1_gate_up",
        carry=(_GatherPlan, [shard('ffn1_w_down'), upper]))
    full['ffn1_w_down'] = _from_blocks(down_blocks, axis_of['ffn1_w_down'])
    x1, lower_blocks, *gathered = mm(act1, full['ffn1_w_down'], scale=0.5, res=x, name="ffn1_down",
                                     carry=(_GatherPlan, [lower] + [shard(n) for n in MID_WEIGHTS[1:]]))
    full['w_in'] = jnp.concatenate([_from_blocks(upper_blocks, 1), _from_blocks(lower_blocks, 1)], axis=0)
    full.update({n: _from_blocks(b, axis_of[n]) for n, b in zip(MID_WEIGHTS[1:], gathered)})
    ffn1 = (h1, gate1, up1, act1)
    w_in_r = _pad_rwkv_cols(full['w_in'][:, :RWKV_COLS])
    w_in_a = full['w_in'][:, RWKV_COLS:RWKV_COLS + ATT_COLS]
    w_in_g = full['w_in'][:, RWKV_COLS + ATT_COLS:]
    wlw, wla, wlg = (_pad_rows(full[n], 128).astype(F32) for n in ('rwkv_w_lora_up', 'rwkv_a_lora_up', 'rwkv_g_lora_up'))
    mu = _pad_rwkv_cols(row(w['rwkv_mu']))
    w0, a0, k_k, k_a, r_k, ln_w, ln_b = (row(w[n]) for n in ('rwkv_w0', 'rwkv_a0', 'rwkv_k_k', 'rwkv_k_a', 'rwkv_r_k',
                                                               'rwkv_ln_w', 'rwkv_ln_b'))
    qg = jnp.tile(row(w['attn_q_norm']), (1, N_HEADS))
    kg = jnp.tile(row(w['attn_k_norm']), (1, KVW // HEAD_DIM))
    sinks = jnp.pad(row(w['attn_sinks']), ((0, 0), (0, LANE - N_HEADS)))

    h2 = rms_fwd(x1, nmix, name="mix_norm")
    pr = mm(h2, w_in_r, name="proj_rwkv")
    pa = mm(h2, w_in_a, name="proj_att")
    pg = mm(h2, w_in_g, name="proj_gate")
    pr_shift = jnp.pad(pr, ((1, 0), (0, 0)))[:-1]
    r, dec, k2, v, a, b, sg = rwkv_pre_fwd(pr, pr_shift, mu, w0, a0, k_k, k_a, wlw, wla, wlg, name="rwkv_pre")
    y, states, *gathered = wkv_fwd(r, dec, k2, v, a, b, name="wkv_fwd", gather=[shard(n) for n in LATE_WEIGHTS])
    full.update({n: _from_blocks(b, axis_of[n]) for n, b in zip(LATE_WEIGHTS, gathered)})
    yr = rwkv_post_fwd(y, r, k2, v, sg, wlg, ln_w, ln_b, r_k, name="rwkv_post")
    ya = att_fwd(pa, qg, kg, sinks, name="att_fwd")
    br = mm(yr, full['w_branch_rwkv'], name="branch_rwkv")
    ba = mm(ya, full['w_branch_attn'], name="branch_att")
    mg = merge_fwd(br, ba, pg, name="merge")
    x2 = mm(mg, full['w_out'], res=x1, name="mix_out")
    x3, ffn2 = _ffn_fwd(x2, n2, full['ffn2_w_gate'], full['ffn2_w_up'], full['ffn2_w_down'], "ffn2")
    dx3, d_nfin, loss = final_loss(x3, tgt, nfin, name="final_loss")

    G = {'final_norm': d_nfin}
    dx2, G['ffn2_norm'], G['ffn2_w_gate'], G['ffn2_w_up'], G['ffn2_w_down'], _, _ = _ffn_bwd(
        dx3, x2, ffn2, n2, full['ffn2_w_gate'], full['ffn2_w_up'], full['ffn2_w_down'], "ffn2")
    dmg = mm(dx2, full['w_out'], tb=True, name="d_merge")
    G['w_out'] = mm(mg, dx2, ta=True, name="d_w_out")
    dbr, dba, dpg = merge_bwd(dmg, br, ba, pg, name="merge_bwd")
    dyr = mm(dbr, full['w_branch_rwkv'], tb=True, name="d_y_rwkv")
    G['w_branch_rwkv'] = mm(yr, dbr, ta=True, name="d_w_branch_rwkv")
    dya = mm(dba, full['w_branch_attn'], tb=True, name="d_y_att")
    G['w_branch_attn'] = mm(ya, dba, ta=True, name="d_w_branch_att")
    late_blocks = [_to_blocks(G[n], axis_of[n]) for n in LATE_WEIGHTS]
    dy, dz, dg, G['rwkv_ln_w'], G['rwkv_ln_b'], *late_from_sibling = rwkv_post_bwd(
        dyr, y, r, k2, v, sg, wlg, ln_w, ln_b, r_k, name="rwkv_post_bwd", carry=(_SiblingPlan, late_blocks))
    late_pair = [pair_sum(g, t, BF16, name="pair_sum_" + n) for g, t, n in zip(late_blocks, late_from_sibling, LATE_WEIGHTS)]
    res = wkv_bwd(r, dec, k2, v, a, b, dy, states, name="wkv_bwd", exchange=[q for _, q in late_pair])
    wkv_grads, late_landed = res[:6], res[6:]
    (dpr, d_mu, G['rwkv_w0'], G['rwkv_a0'], G['rwkv_k_k'], G['rwkv_k_a'], G['rwkv_r_k'], d_wlw, d_wla, d_wlg) = rwkv_pre_bwd(
        pr, pr_shift, *wkv_grads, dz, dg, mu, w0, a0, k_k, k_a, r_k, wlw, wla, wlg, name="rwkv_pre_bwd")
    G['rwkv_mu'] = _unpad_rwkv_cols(d_mu)
    G['rwkv_w_lora_up'], G['rwkv_a_lora_up'], G['rwkv_g_lora_up'] = d_wlw[:DECAY_LORA], d_wla[:ICLR_LORA], d_wlg[:GATE_LORA]
    dq, dko, dkn, dvo, dvn, G['attn_q_norm'], d_sinks = att_bwd(pa, dya, qg, kg, sinks, name="att_bwd")
    G['attn_sinks'] = d_sinks[:, :N_HEADS]
    dpa, G['attn_k_norm'] = att_kv_bwd(pa, dq, dko, dkn, dvo, dvn, kg, name="att_kv_bwd")
    d_w_in_r = mm(h2, dpr, ta=True, name="d_w_in_rwkv")
    d_w_in_a = mm(h2, dpa, ta=True, name="d_w_in_att")
    d_w_in_g = mm(h2, dpg, ta=True, name="d_w_in_gate")
    G['w_in'] = jnp.concatenate([_unpad_rwkv_cols(d_w_in_r), d_w_in_a, d_w_in_g], axis=1)
    mid_blocks = [_to_blocks(G[n], axis_of[n]) for n in MID_WEIGHTS]
    dh2, *mid_from_sibling = mm(dpr, w_in_r, tb=True, name="d_h2_rwkv", carry=(_SiblingPlan, mid_blocks))
    dh2 = mm(dpa, w_in_a, tb=True, res=dh2, name="d_h2_att")
    dx1, G['mix_norm'] = mm(dpg, w_in_g, tb=True, res=dh2, name="d_h2_gate", norm_bwd=(x1, nmix, dx2))
    mid_pair = [pair_sum(g, t, BF16, name="pair_sum_" + n) for g, t, n in zip(mid_blocks, mid_from_sibling, MID_WEIGHTS)]
    dx0, G['ffn1_norm'], _, _, _, mid_landed, (first_pair, first_landed) = _ffn_bwd(
        dx1, x, ffn1, n1, full['ffn1_w_gate'], full['ffn1_w_up'], full['ffn1_w_down'], "ffn1",
        carry=(_ExchangePlan, [q for _, q in mid_pair]), reduce=FIRST_WEIGHTS)

    small_shapes = [(w[n].size,) for n in SMALL] + [(1,)]

    def small_rows(parts):
        vec = jnp.concatenate([p.reshape(-1) for p in parts])
        return jnp.pad(vec, (0, SMALL_ROWS * FLAT_W - vec.shape[0])).reshape(SMALL_ROWS, FLAT_W)

    small = small_rows([G[n] for n in SMALL] + [loss[0, :1]])
    small_pair = _pair_sums(['small'], [jnp.broadcast_to(small[None], (N_CHIPS,) + small.shape)], "small")
    small_landed = exchange_chips([q for _, q in small_pair], name="exchange_small")
    names = FIRST_WEIGHTS + ['small'] + MID_WEIGHTS + LATE_WEIGHTS
    pair = first_pair + small_pair + mid_pair + late_pair
    landed = list(first_landed) + list(small_landed) + list(mid_landed) + list(late_landed)
    halves = [half_sum(own, l, name="half_sum_" + n) for (own, _), l, n in zip(pair, landed, names)]
    other_halves = to_sibling(halves, False, name="halves_to_sibling")

    def local(prefix, n):
        if n != 'small':
            return natural(n, A[prefix + n])
        return small_rows([A[prefix + s] for s in SMALL] + [jnp.zeros((1,), F32)])[None]

    result = {}
    for n, mine, theirs in zip(names, halves, other_halves):
        outs4 = adamw(local('', n), local('m_', n), local('v_', n), mine, theirs, name="adamw_" + n)
        for kind, o in zip(('grad_', 'delta_', 'new_m_', 'new_v_'), outs4):
            if n != 'small':
                result[kind + n] = natural(n, o)
            else:
                for s, part in zip(SMALL + ['loss'], _unpack_vec(o.reshape(-1), small_shapes)):
                    result[kind + s] = part.reshape(A[s].shape) if s != 'loss' else part.reshape(())
    outs = [result['grad_loss'], dx0[None]]
    for kind in ('grad_', 'delta_', 'new_m_', 'new_v_'):
        outs += [result[kind + n] for n in WEIGHT_NAMES]
    return tuple(outs)


def _unpack_vec(vec, shapes):
    out, off = [], 0
    for (n,) in shapes:
        out.append(vec[off:off + n])
        off += n
    return out


def kernel(x, ffn1_norm, ffn1_w_gate, ffn1_w_up, ffn1_w_down, mix_norm, w_in, rwkv_mu, rwkv_w0, rwkv_w_lora_up, rwkv_a0, rwkv_a_lora_up, rwkv_g_lora_up, rwkv_k_k, rwkv_k_a, rwkv_r_k, rwkv_ln_w, rwkv_ln_b, attn_q_norm, attn_k_norm, attn_sinks, w_branch_rwkv, w_branch_attn, w_out, ffn2_norm, ffn2_w_gate, ffn2_w_up, ffn2_w_down, final_norm, loss_target, m_ffn1_norm, m_ffn1_w_gate, m_ffn1_w_up, m_ffn1_w_down, m_mix_norm, m_w_in, m_rwkv_mu, m_rwkv_w0, m_rwkv_w_lora_up, m_rwkv_a0, m_rwkv_a_lora_up, m_rwkv_g_lora_up, m_rwkv_k_k, m_rwkv_k_a, m_rwkv_r_k, m_rwkv_ln_w, m_rwkv_ln_b, m_attn_q_norm, m_attn_k_norm, m_attn_sinks, m_w_branch_rwkv, m_w_branch_attn, m_w_out, m_ffn2_norm, m_ffn2_w_gate, m_ffn2_w_up, m_ffn2_w_down, m_final_norm, v_ffn1_norm, v_ffn1_w_gate, v_ffn1_w_up, v_ffn1_w_down, v_mix_norm, v_w_in, v_rwkv_mu, v_rwkv_w0, v_rwkv_w_lora_up, v_rwkv_a0, v_rwkv_a_lora_up, v_rwkv_g_lora_up, v_rwkv_k_k, v_rwkv_k_a, v_rwkv_r_k, v_rwkv_ln_w, v_rwkv_ln_b, v_attn_q_norm, v_attn_k_norm, v_attn_sinks, v_w_branch_rwkv, v_w_branch_attn, v_w_out, v_ffn2_norm, v_ffn2_w_gate, v_ffn2_w_up, v_ffn2_w_down, v_final_norm):
    return _step(dict(locals()))
```

```python
import functools

import jax
import jax.numpy as jnp
from jax import lax
from jax.experimental import pallas as pl
from jax.experimental.pallas import tpu as pltpu

F32 = jnp.float32
BF16 = jnp.bfloat16

D_MODEL = 1024
D_FF = 2816
HEAD_DIM = 64
N_HEADS = 8
RW = 512
KVW = 128
ATT_GROUP = 4
WINDOW = 128
BLOCK = 128
DECAY_LORA, ICLR_LORA, GATE_LORA = 32, 32, 96
RWKV_COLS = 3 * RW + DECAY_LORA + ICLR_LORA + GATE_LORA
ATT_COLS = RW + 2 * KVW
GATE_COLS = 2 * D_MODEL
RWKV_PAD = 3 * RW + 3 * 128
RMS_EPS = 1e-6
GN_EPS = 64e-5
N_CHIPS = 4
LANE = 128
FLAT_W = 1024
SMALL_ROWS = 32
NEG_BIG = -1e30

ADAM_LR, ADAM_B1, ADAM_B2, ADAM_EPS, ADAM_WD, ADAM_STEP = 0.001, 0.9, 0.999, 1e-08, 0.01, 10

VMEM_LIMIT = 56 * 1024 * 1024

WEIGHT_NAMES = ['ffn1_norm', 'ffn1_w_gate', 'ffn1_w_up', 'ffn1_w_down', 'mix_norm', 'w_in', 'rwkv_mu', 'rwkv_w0',
                'rwkv_w_lora_up', 'rwkv_a0', 'rwkv_a_lora_up', 'rwkv_g_lora_up', 'rwkv_k_k', 'rwkv_k_a', 'rwkv_r_k',
                'rwkv_ln_w', 'rwkv_ln_b', 'attn_q_norm', 'attn_k_norm', 'attn_sinks', 'w_branch_rwkv',
                'w_branch_attn', 'w_out', 'ffn2_norm', 'ffn2_w_gate', 'ffn2_w_up', 'ffn2_w_down', 'final_norm']
BIG = [('ffn1_w_gate', 1), ('ffn1_w_up', 1), ('ffn1_w_down', 0), ('w_in', 1), ('rwkv_w_lora_up', 1),
       ('rwkv_a_lora_up', 1), ('rwkv_g_lora_up', 1), ('w_branch_rwkv', 1), ('w_branch_attn', 1), ('w_out', 0),
       ('ffn2_w_gate', 1), ('ffn2_w_up', 1), ('ffn2_w_down', 0)]
SMALL = ['ffn1_norm', 'mix_norm', 'rwkv_mu', 'rwkv_w0', 'rwkv_a0', 'rwkv_k_k', 'rwkv_k_a', 'rwkv_r_k', 'rwkv_ln_w',
         'rwkv_ln_b', 'attn_q_norm', 'attn_k_norm', 'attn_sinks', 'ffn2_norm', 'final_norm']


def _pcall(body, **kw):
    return pl.pallas_call(body, **kw)


def _params(sem=None, **kw):
    if sem is not None:
        kw['dimension_semantics'] = sem
    return pltpu.CompilerParams(vmem_limit_bytes=VMEM_LIMIT, **kw)


def _tile(n, cap, mult):
    best = None
    for t in range(mult, min(n, cap) + 1, mult):
        if n % t == 0:
            best = t
    return best or n


def _sigmoid(z):
    return 1.0 / (1.0 + jnp.exp(-z))


def _softplus(z):
    return jnp.maximum(z, 0.0) + jnp.log(1.0 + jnp.exp(-jnp.abs(z)))


def _bdot(a, b, dims=(((1,), (0,)), ((), ()))):
    return lax.dot_general(a.astype(BF16), b.astype(BF16), dims, preferred_element_type=F32)


_NT = (((1,), (1,)), ((), ()))
_TN = (((0,), (0,)), ((), ()))


def _segsum(x, bd):
    hi = x.astype(BF16)
    lo = (x - hi.astype(F32)).astype(BF16)
    dot = functools.partial(lax.dot_general, dimension_numbers=(((1,), (0,)), ((), ())), preferred_element_type=F32)
    return dot(hi, bd) + dot(lo, bd)


_LORA_EDGES = (3 * RW, 3 * RW + DECAY_LORA, 3 * RW + DECAY_LORA + ICLR_LORA, RWKV_COLS)


def _pad_rwkv_cols(x):
    parts = [x[..., :3 * RW]]
    for lo, hi in zip(_LORA_EDGES[:-1], _LORA_EDGES[1:]):
        parts.append(jnp.pad(x[..., lo:hi], [(0, 0)] * (x.ndim - 1) + [(0, 128 - (hi - lo))]))
    return jnp.concatenate(parts, axis=-1)


def _unpad_rwkv_cols(x):
    parts = [x[..., :3 * RW]]
    for j, (lo, hi) in enumerate(zip(_LORA_EDGES[:-1], _LORA_EDGES[1:])):
        parts.append(x[..., 3 * RW + 128 * j:3 * RW + 128 * j + (hi - lo)])
    return jnp.concatenate(parts, axis=-1)


def _pad_rows(x, rows):
    return jnp.pad(x, [(0, rows - x.shape[0])] + [(0, 0)] * (x.ndim - 1))


def mm(a, b, *, name, ta=False, tb=False, scale=None, res=None, out_dtype=F32, carry=None, norm_bwd=None):
    M, K = (a.shape[1], a.shape[0]) if ta else a.shape
    N = b.shape[0] if tb else b.shape[1]
    assert (b.shape[1] if tb else b.shape[0]) == K
    tm, tn, tk = _tile(M, 1408 if ta else 512, 128), _tile(N, 1408, 128), _tile(K, 1408, 128)
    nk = K // tk
    grid = (M // tm, N // tn, nk)
    dims = (((0 if ta else 1,), (1 if tb else 0,)), ((), ()))
    plan_cls, carried = carry if carry else (None, ())
    nn = 3 if norm_bwd else 0
    nd = 1 if norm_bwd else 0
    nc, nin = len(carried), 2 + (res is not None) + nn
    assert not norm_bwd or (tn == N and out_dtype == F32)

    def body(*refs):
        a_ref, b_ref = refs[:2]
        r_ref = refs[2] if res is not None else None
        o_ref, acc_ref = refs[nin + nc], refs[nin + 2 * nc + 1 + nd]
        row_tile, k = pl.program_id(0), pl.program_id(2)
        if nc:
            plan = plan_cls(refs[nin:nin + nc], refs[nin + nc + 1 + nd:nin + 2 * nc + 1 + nd], refs[nin + 2 * nc + 2 + nd:])
            at = lambda which: functools.reduce(jnp.logical_and, [pl.program_id(d) == (0 if which == 0 else grid[d] - 1)
                                                                 for d in range(3)])
            pl.when(at(0))(plan.start)
        part = _bdot(a_ref[...], b_ref[...], dims)

        @pl.when(k == 0)
        def _():
            acc_ref[...] = part

        @pl.when(k > 0)
        def _():
            acc_ref[...] += part

        @pl.when(k == nk - 1)
        def _():
            o = acc_ref[...]
            if scale is not None:
                o = o * scale
            if r_ref is not None:
                o = o + r_ref[...].astype(F32)
            if norm_bwd:
                x_ref, g_ref, dres_ref = refs[nin - 3:nin]
                xv = x_ref[...]
                r = lax.rsqrt(jnp.mean(xv * xv, axis=-1, keepdims=True) + RMS_EPS)
                xh = xv * r
                dxh = o * g_ref[...]
                _acc_rows(refs[nin + nc + 1], jnp.sum(o * xh, axis=0, keepdims=True), row_tile)
                o = dres_ref[...] + r * (dxh - xh * jnp.mean(dxh * xh, axis=-1, keepdims=True))
            o_ref[...] = o.astype(out_dtype)

        if nc:
            pl.when(at(1))(plan.finish)

    a_spec = pl.BlockSpec((tk, tm), lambda i, j, k: (k, i)) if ta else pl.BlockSpec((tm, tk), lambda i, j, k: (i, k))
    b_spec = pl.BlockSpec((tn, tk), lambda i, j, k: (j, k)) if tb else pl.BlockSpec((tk, tn), lambda i, j, k: (k, j))
    o_spec = pl.BlockSpec((tm, tn), lambda i, j, k: (i, j))
    g_spec = pl.BlockSpec((1, N), lambda i, j, k: (0, 0))
    in_specs = [a_spec, b_spec] + ([o_spec] if res is not None else []) + ([o_spec, g_spec, o_spec] if norm_bwd else [])
    args = (a, b) + ((res,) if res is not None else ()) + (tuple(norm_bwd) if norm_bwd else ())
    out_shape = [jax.ShapeDtypeStruct((M, N), out_dtype)] + [jax.ShapeDtypeStruct((1, N), F32)] * nd
    out_specs = [o_spec] + [g_spec] * nd
    if not nc and not nd:
        return _pcall(
            body, name=name, grid=grid, in_specs=in_specs, out_specs=o_spec, out_shape=out_shape[0],
            scratch_shapes=[pltpu.VMEM((tm, tn), F32)], compiler_params=_params(("parallel", "parallel", "arbitrary")),
        )(*args)
    return _pcall(
        body, name=name, grid=grid, in_specs=in_specs + [_HBM] * nc, out_specs=out_specs + [_HBM] * nc,
        out_shape=out_shape + (plan_cls.out_shapes(carried) if nc else []),
        scratch_shapes=[pltpu.VMEM((tm, tn), F32)] + (plan_cls.sems(nc) if nc else []),
        compiler_params=_params(("arbitrary", "arbitrary", "arbitrary")),
    )(*args, *carried)


def mm_fused(a, bs, finish, out_dtypes, *, name, tb=False, extras=(), carry=None):
    M, K = a.shape
    N = bs[0].shape[0] if tb else bs[0].shape[1]
    tm, tn = _tile(M, 512, 128), _tile(N, 1408, 128)
    grid = (M // tm, N // tn)
    dims = (((1,), (1 if tb else 0,)), ((), ()))
    plan_cls, carried = carry if carry else (None, ())
    nc, nb, nx, no = len(carried), len(bs), len(extras), len(out_dtypes)
    nin = 1 + nb + nx

    def body(*refs):
        a_ref, b_refs, x_refs = refs[0], refs[1:1 + nb], refs[1 + nb:nin]
        o_refs = refs[nin + nc:nin + nc + no]
        if nc:
            plan = plan_cls(refs[nin:nin + nc], refs[nin + nc + no:nin + 2 * nc + no], refs[nin + 2 * nc + no:])
            at = lambda which: jnp.logical_and(*[pl.program_id(d) == (0 if which == 0 else grid[d] - 1) for d in range(2)])
            pl.when(at(0))(plan.start)
        av = a_ref[...]
        outs = finish([_bdot(av, b_ref[...], dims) for b_ref in b_refs], [x_ref[...] for x_ref in x_refs])
        for o_ref, o in zip(o_refs, outs):
            o_ref[...] = o.astype(o_ref.dtype)
        if nc:
            pl.when(at(1))(plan.finish)

    a_spec = pl.BlockSpec((tm, K), lambda i, j: (i, 0))
    b_spec = pl.BlockSpec((tn, K), lambda i, j: (j, 0)) if tb else pl.BlockSpec((K, tn), lambda i, j: (0, j))
    o_spec = pl.BlockSpec((tm, tn), lambda i, j: (i, j))
    out_shape = [jax.ShapeDtypeStruct((M, N), d) for d in out_dtypes]
    if not nc:
        return _pcall(body, name=name, grid=grid, in_specs=[a_spec] + [b_spec] * nb + [o_spec] * nx, out_specs=[o_spec] * no,
                      out_shape=out_shape, compiler_params=_params(("parallel", "parallel")))(a, *bs, *extras)
    return _pcall(body, name=name, grid=grid, in_specs=[a_spec] + [b_spec] * nb + [o_spec] * nx + [_HBM] * nc,
                  out_specs=[o_spec] * no + [_HBM] * nc, out_shape=out_shape + plan_cls.out_shapes(carried),
                  scratch_shapes=plan_cls.sems(nc), compiler_params=_params(("arbitrary", "arbitrary")))(a, *bs, *extras, *carried)


def _swiglu(products, _):
    g, u = products
    return g, u, g * _sigmoid(g) * u


def _swiglu_bwd(products, extras):
    (da,), (gate, up) = products, extras
    gv = gate.astype(F32)
    s = _sigmoid(gv)
    return da * 0.5 * up.astype(F32) * s * (1.0 + gv * (1.0 - s)), da * 0.5 * gv * s


def _row_spec(tr, c):
    return pl.BlockSpec((tr, c), lambda i: (i, 0))


def _full_spec(shape):
    return pl.BlockSpec(shape, lambda i: (0,) * len(shape))


def _acc_rows(ref, val, i):
    @pl.when(i == 0)
    def _():
        ref[...] = val

    @pl.when(i > 0)
    def _():
        ref[...] += val


def rms_fwd(x, g, *, name):
    T, D = x.shape
    tr = _tile(T, 512, 8)

    def body(x_ref, g_ref, h_ref):
        xv = x_ref[...]
        r = lax.rsqrt(jnp.mean(xv * xv, axis=-1, keepdims=True) + RMS_EPS)
        h_ref[...] = (xv * r * g_ref[...]).astype(BF16)

    return _pcall(body, name=name, grid=(T // tr,), in_specs=[_row_spec(tr, D), _full_spec((1, D))],
                  out_specs=_row_spec(tr, D), out_shape=jax.ShapeDtypeStruct((T, D), BF16),
                  compiler_params=_params(("parallel",)))(x, g)


def final_loss(x, tgt, g, *, name):
    T, D = x.shape
    tr = _tile(T, 256, 8)

    def body(x_ref, t_ref, g_ref, dx_ref, dg_ref, loss_ref):
        i = pl.program_id(0)
        xv = x_ref[...]
        r = lax.rsqrt(jnp.mean(xv * xv, axis=-1, keepdims=True) + RMS_EPS)
        xh = xv * r
        e = xh * g_ref[...] - t_ref[...]
        part = 0.5 * jnp.sum(jnp.mean(e * e, axis=-1, keepdims=True), axis=0, keepdims=True)
        dy = e * (1.0 / D)
        dxh = dy * g_ref[...]
        dx_ref[...] = r * (dxh - xh * jnp.mean(dxh * xh, axis=-1, keepdims=True))
        _acc_rows(dg_ref, jnp.sum(dy * xh, axis=0, keepdims=True), i)
        _acc_rows(loss_ref, jnp.broadcast_to(part, (1, LANE)), i)

    return _pcall(body, name=name, grid=(T // tr,),
                  in_specs=[_row_spec(tr, D), _row_spec(tr, D), _full_spec((1, D))],
                  out_specs=[_row_spec(tr, D), _full_spec((1, D)), _full_spec((1, LANE))],
                  out_shape=[jax.ShapeDtypeStruct((T, D), F32), jax.ShapeDtypeStruct((1, D), F32),
                             jax.ShapeDtypeStruct((1, LANE), F32)],
                  compiler_params=_params(("arbitrary",)))(x, tgt, g)


def merge_fwd(br, ba, pg, *, name):
    T, D = br.shape
    tr = _tile(T, 256, 8)

    def body(br_ref, ba_ref, pg_ref, o_ref):
        pgv = pg_ref[...]
        o_ref[...] = (_sigmoid(pgv[:, :D]) * br_ref[...] + _sigmoid(pgv[:, D:]) * ba_ref[...]).astype(BF16)

    return _pcall(body, name=name, grid=(T // tr,), in_specs=[_row_spec(tr, D), _row_spec(tr, D), _row_spec(tr, 2 * D)],
                  out_specs=_row_spec(tr, D), out_shape=jax.ShapeDtypeStruct((T, D), BF16),
                  compiler_params=_params(("parallel",)))(br, ba, pg)


def merge_bwd(dm, br, ba, pg, *, name):
    T, D = br.shape
    tr = _tile(T, 256, 8)

    def body(dm_ref, br_ref, ba_ref, pg_ref, dbr_ref, dba_ref, dpg_ref):
        pgv, dmv = pg_ref[...], dm_ref[...]
        sr, sa = _sigmoid(pgv[:, :D]), _sigmoid(pgv[:, D:])
        dbr_ref[...] = (dmv * sr).astype(BF16)
        dba_ref[...] = (dmv * sa).astype(BF16)
        dpg_ref[:, :D] = dmv * br_ref[...] * sr * (1.0 - sr)
        dpg_ref[:, D:] = dmv * ba_ref[...] * sa * (1.0 - sa)

    return _pcall(body, name=name, grid=(T // tr,),
                  in_specs=[_row_spec(tr, D), _row_spec(tr, D), _row_spec(tr, D), _row_spec(tr, 2 * D)],
                  out_specs=[_row_spec(tr, D), _row_spec(tr, D), _row_spec(tr, 2 * D)],
                  out_shape=[jax.ShapeDtypeStruct((T, D), BF16), jax.ShapeDtypeStruct((T, D), BF16),
                             jax.ShapeDtypeStruct((T, 2 * D), F32)],
                  compiler_params=_params(("parallel",)))(dm, br, ba, pg)


def _rwkv_mix(p, prev, mu, w0, a0, k_k, k_a, wlw, wla, wlg, bd):
    pp = p + (prev - p) * mu
    r, k, v = pp[:, 0:RW], pp[:, RW:2 * RW], pp[:, 2 * RW:3 * RW]
    xw, xa, xg = pp[:, 3 * RW:3 * RW + 128], pp[:, 3 * RW + 128:3 * RW + 256], pp[:, 3 * RW + 256:3 * RW + 384]
    th = jnp.tanh(xw)
    z = -(w0 + _bdot(th, wlw))
    e = jnp.exp(-_softplus(z) - 0.5)
    decay = jnp.exp(-e)
    a = _sigmoid(a0 + _bdot(xa, wla))
    sg = _sigmoid(xg)
    kkr = k * k_k
    n = jnp.sqrt(_segsum(kkr * kkr, bd))
    kk = kkr / jnp.maximum(n, 1e-12)
    k2 = k * (1.0 + (a - 1.0) * k_a)
    return dict(r=r, k=k, v=v, xa=xa, th=th, z=z, e=e, decay=decay, a=a, sg=sg, n=n, kk=kk, k2=k2)


def _seg_matrix(n, shift):
    r = lax.shift_right_logical(lax.broadcasted_iota(jnp.int32, (n, n), 0), shift)
    c = lax.shift_right_logical(lax.broadcasted_iota(jnp.int32, (n, n), 1), shift)
    return jnp.where(r == c, 1.0, 0.0).astype(BF16)


def rwkv_pre_fwd(p, pshift, mu, w0, a0, k_k, k_a, wlw, wla, wlg, *, name):
    T = p.shape[0]
    tr = _tile(T, 256, 8)

    def body(p_ref, ps_ref, mu_ref, w0_ref, a0_ref, kk_ref, ka_ref, wlw_ref, wla_ref, wlg_ref,
             r_ref, w_ref, k_ref, v_ref, a_ref, b_ref, g_ref):
        pv, prev = p_ref[...], ps_ref[...]
        m = _rwkv_mix(pv, prev, mu_ref[...], w0_ref[...], a0_ref[...], kk_ref[...], ka_ref[...],
                      wlw_ref[...], wla_ref[...], wlg_ref[...], _seg_matrix(RW, 6))
        r_ref[...] = m['r']
        w_ref[...] = m['decay']
        k_ref[...] = m['k2']
        v_ref[...] = m['v']
        a_ref[...] = -m['kk']
        b_ref[...] = m['kk'] * m['a']
        g_ref[...] = m['sg']

    vec = _row_spec(tr, RW)
    return _pcall(
        body, name=name, grid=(T // tr,),
        in_specs=[_row_spec(tr, RWKV_PAD), _row_spec(tr, RWKV_PAD), _full_spec((1, RWKV_PAD))] + [_full_spec((1, RW))] * 4
        + [_full_spec((128, RW))] * 3,
        out_specs=[vec] * 6 + [_row_spec(tr, 128)],
        out_shape=[jax.ShapeDtypeStruct((T, RW), F32)] * 6 + [jax.ShapeDtypeStruct((T, 128), F32)],
        compiler_params=_params(("parallel",)),
    )(p, pshift, mu, w0, a0, k_k, k_a, wlw, wla, wlg)


def _group_norm(y, bd):
    mean = _segsum(y, bd) * (1.0 / HEAD_DIM)
    yc = y - mean
    rstd = lax.rsqrt(_segsum(yc * yc, bd) * (1.0 / HEAD_DIM) + GN_EPS)
    return yc * rstd, rstd


def rwkv_post_fwd(y, r, k2, v, sg, wlg, ln_w, ln_b, r_k, *, name):
    T = y.shape[0]
    tr = _tile(T, 256, 8)

    def body(y_ref, r_ref, k_ref, v_ref, sg_ref, wlg_ref, lw_ref, lb_ref, rk_ref, o_ref):
        bd = _seg_matrix(RW, 6)
        yn, _ = _group_norm(y_ref[...], bd)
        s = _segsum(r_ref[...] * k_ref[...] * rk_ref[...], bd)
        g = _bdot(sg_ref[...], wlg_ref[...])
        o_ref[...] = ((yn * lw_ref[...] + lb_ref[...] + s * v_ref[...]) * g).astype(BF16)

    vec = _row_spec(tr, RW)
    return _pcall(body, name=name, grid=(T // tr,),
                  in_specs=[vec] * 4 + [_row_spec(tr, 128), _full_spec((128, RW))] + [_full_spec((1, RW))] * 3, out_specs=vec,
                  out_shape=jax.ShapeDtypeStruct((T, RW), BF16), compiler_params=_params(("parallel",)))(
                      y, r, k2, v, sg, wlg, ln_w, ln_b, r_k)


def rwkv_post_bwd(dyr, y, r, k2, v, sg, wlg, ln_w, ln_b, r_k, *, name, carry=None):
    T = y.shape[0]
    tr = _tile(T, 256, 8)
    nt = T // tr
    plan_cls, carried = carry if carry else (None, ())
    nc = len(carried)

    def body(*refs):
        dyr_ref, y_ref, r_ref, k_ref, v_ref, sg_ref, wlg_ref, lw_ref, lb_ref, rk_ref = refs[:10]
        dy_ref, dz_ref, dg_ref, dlw_ref, dlb_ref = refs[10 + nc:15 + nc]
        i = pl.program_id(0)
        if nc:
            plan = plan_cls(refs[10:10 + nc], refs[15 + nc:15 + 2 * nc], refs[15 + 2 * nc:])
            pl.when(i == 0)(plan.start)
        bd = _seg_matrix(RW, 6)
        yn, rstd = _group_norm(y_ref[...], bd)
        s = _segsum(r_ref[...] * k_ref[...] * rk_ref[...], bd)
        dyrv = dyr_ref[...]
        dg_ref[...] = dyrv * (yn * lw_ref[...] + lb_ref[...] + s * v_ref[...])
        dz = dyrv * _bdot(sg_ref[...], wlg_ref[...])
        dz_ref[...] = dz
        dyn = dz * lw_ref[...]
        inv = 1.0 / HEAD_DIM
        dy_ref[...] = rstd * (dyn - _segsum(dyn, bd) * inv - yn * (_segsum(dyn * yn, bd) * inv))
        _acc_rows(dlw_ref, jnp.sum(dz * yn, axis=0, keepdims=True), i)
        _acc_rows(dlb_ref, jnp.sum(dz, axis=0, keepdims=True), i)
        if nc:
            pl.when(i == nt - 1)(plan.finish)

    vec = _row_spec(tr, RW)
    one = _full_spec((1, RW))
    return _pcall(body, name=name, grid=(nt,),
                  in_specs=[vec] * 5 + [_row_spec(tr, 128), _full_spec((128, RW))] + [one] * 3 + [_HBM] * nc,
                  out_specs=[vec] * 3 + [one] * 2 + [_HBM] * nc,
                  out_shape=[jax.ShapeDtypeStruct((T, RW), F32)] * 3 + [jax.ShapeDtypeStruct((1, RW), F32)] * 2
                  + (plan_cls.out_shapes(carried) if nc else []),
                  scratch_shapes=plan_cls.sems(nc) if nc else [],
                  compiler_params=_params(("arbitrary",)))(dyr, y, r, k2, v, sg, wlg, ln_w, ln_b, r_k, *carried)


def rwkv_pre_bwd(p, pshift, dr_w, dw_w, dk_w, dv_w, da_w, db_w, dz, dg, mu, w0, a0, k_k, k_a, r_k, wlw, wla, wlg, *, name):
    T = p.shape[0]
    tr = _tile(T, 256, 8)
    n = T // tr

    def body(p_ref, ps_ref, dr_ref, dw_ref, dk_ref, dv_ref, da_ref, db_ref, dz_ref, dg_ref,
             mu_ref, w0_ref, a0_ref, kk_ref, ka_ref, rk_ref, wlw_ref, wla_ref, wlg_ref,
             dp_ref, dmu_ref, dw0_ref, da0_ref, dkk_ref, dka_ref, drk_ref, dwlw_ref, dwla_ref, dwlg_ref,
             carry, dpp, acc_w, acc_a, acc_g):
        i = pl.program_id(0)

        @pl.when(i == 0)
        def _():
            carry[...] = jnp.zeros_like(carry)

        pv, prev, mu = p_ref[...], ps_ref[...], mu_ref[...]
        bd = _seg_matrix(RW, 6)
        k_k, k_a, r_k = kk_ref[...], ka_ref[...], rk_ref[...]
        m = _rwkv_mix(pv, prev, mu, w0_ref[...], a0_ref[...], k_k, k_a, wlw_ref[...], wla_ref[...], wlg_ref[...], bd)
        r, k, v, a, kk, k2 = m['r'], m['k'], m['v'], m['a'], m['kk'], m['k2']
        dzv, dgv = dz_ref[...], dg_ref[...]
        s = _segsum(r * k2 * r_k, bd)
        ds = _segsum(dzv * v, bd)
        dr = dr_ref[...] + ds * k2 * r_k
        dk2 = dk_ref[...] + ds * r * r_k
        dv = dv_ref[...] + dzv * s
        dbv = db_ref[...]
        dkk = dbv * a - da_ref[...]
        da = dbv * kk + dk2 * k * k_a
        dk = dk2 * (1.0 + (a - 1.0) * k_a)
        nmax = jnp.maximum(m['n'], 1e-12)
        dkkr = jnp.where(m['n'] > 1e-12, dkk - kk * _segsum(dkk * kk, bd), dkk) / nmax
        dk = dk + dkkr * k_k
        dapre = da * a * (1.0 - a)
        dwpre = dw_ref[...] * m['decay'] * (-m['e']) * _sigmoid(m['z'])
        dth = _bdot(dwpre, wlw_ref[...], _NT)
        dxa = _bdot(dapre, wla_ref[...], _NT)
        dsg = _bdot(dgv, wlg_ref[...], _NT)
        dpp[:, 0:RW] = dr
        dpp[:, RW:2 * RW] = dk
        dpp[:, 2 * RW:3 * RW] = dv
        dpp[:, 3 * RW:3 * RW + 128] = dth * (1.0 - m['th'] * m['th'])
        dpp[:, 3 * RW + 128:3 * RW + 256] = dxa
        dpp[:, 3 * RW + 256:3 * RW + 384] = dsg * m['sg'] * (1.0 - m['sg'])
        d = dpp[...]
        zed = d * mu
        last = lax.broadcasted_iota(jnp.int32, pv.shape, 0) == tr - 1
        dp_ref[...] = d * (1.0 - mu) + jnp.where(last, carry[0:1, :], pltpu.roll(zed, tr - 1, 0))
        carry[...] = zed[0:8, :]

        def colsum(x):
            return jnp.sum(x, axis=0, keepdims=True)

        _acc_rows(dmu_ref, colsum(d * (prev - pv)), i)
        _acc_rows(dw0_ref, colsum(dwpre), i)
        _acc_rows(da0_ref, colsum(dapre), i)
        _acc_rows(dkk_ref, colsum(dkkr * k), i)
        _acc_rows(dka_ref, colsum(dk2 * k * (a - 1.0)), i)
        _acc_rows(drk_ref, colsum(ds * r * k2), i)
        _acc_rows(acc_w, _bdot(m['th'], dwpre, _TN), i)
        _acc_rows(acc_a, _bdot(m['xa'], dapre, _TN), i)
        _acc_rows(acc_g, _bdot(m['sg'], dgv, _TN), i)

        @pl.when(i == n - 1)
        def _():
            dwlw_ref[...] = acc_w[...]
            dwla_ref[...] = acc_a[...]
            dwlg_ref[...] = acc_g[...]

    rev = lambda c: pl.BlockSpec((tr, c), lambda i: (n - 1 - i, 0))
    one, lora = _full_spec((1, RW)), _full_spec((128, RW))
    return _pcall(
        body, name=name, grid=(n,),
        in_specs=[rev(RWKV_PAD), rev(RWKV_PAD)] + [rev(RW)] * 8 + [_full_spec((1, RWKV_PAD))] + [one] * 5 + [lora] * 3,
        out_specs=[rev(RWKV_PAD), _full_spec((1, RWKV_PAD))] + [one] * 5 + [lora] * 3,
        out_shape=[jax.ShapeDtypeStruct((T, RWKV_PAD), F32), jax.ShapeDtypeStruct((1, RWKV_PAD), F32)]
        + [jax.ShapeDtypeStruct((1, RW), F32)] * 5 + [jax.ShapeDtypeStruct((128, RW), F32)] * 3,
        scratch_shapes=[pltpu.VMEM((8, RWKV_PAD), F32), pltpu.VMEM((tr, RWKV_PAD), F32)] + [pltpu.VMEM((128, RW), F32)] * 3,
        compiler_params=_params(("arbitrary",)),
    )(p, pshift, dr_w, dw_w, dk_w, dv_w, da_w, db_w, dz, dg, mu, w0, a0, k_k, k_a, r_k, wlw, wla, wlg)


def _qk_norm(x, g, bd):
    r = lax.rsqrt(_segsum(x * x, bd) * (1.0 / HEAD_DIM) + RMS_EPS)
    return x * r * g, r


def _att_mask(i):
    qi = lax.broadcasted_iota(jnp.int32, (BLOCK, 2 * BLOCK), 0)
    kj = lax.broadcasted_iota(jnp.int32, (BLOCK, 2 * BLOCK), 1)
    band = (kj <= qi + BLOCK) & (kj > qi + BLOCK - WINDOW)
    return band & ((kj >= BLOCK) | (i > 0))


_HQK = (((2,), (2,)), ((0,), (0,)))
_HPV = (((2,), (1,)), ((0,), (0,)))
_HTN = (((1,), (1,)), ((0,), (0,)))


def _heads(x, n):
    return jnp.stack([x[:, h * HEAD_DIM:(h + 1) * HEAD_DIM] for h in range(n)])


def _unheads(x3):
    return jnp.concatenate([x3[h] for h in range(x3.shape[0])], axis=1)


def _kv_heads(x):
    x2 = _heads(x, KVW // HEAD_DIM)
    return jnp.concatenate([x2[g:g + 1] for g in range(KVW // HEAD_DIM) for _ in range(ATT_GROUP)], axis=0)


def _sinks3(sk):
    return jnp.stack([sk[0:1, h:h + 1] for h in range(N_HEADS)])


def _att_probs(q3, k3, mask, sink):
    s = _bdot(q3, k3, _HQK) * (HEAD_DIM ** -0.5)
    s = jnp.where(mask[None], s, NEG_BIG)
    m = jnp.maximum(jnp.max(s, axis=-1, keepdims=True), sink)
    pexp = jnp.exp(s - m)
    psink = jnp.exp(sink - m)
    inv = 1.0 / (jnp.sum(pexp, axis=-1, keepdims=True) + psink)
    return pexp * inv, psink * inv


ATT_SUB = 2


def _att_blocks(sub):
    cur = pl.BlockSpec((sub * BLOCK, ATT_COLS), lambda i: (i, 0))
    prev = pl.BlockSpec((BLOCK, ATT_COLS), lambda i: (jnp.maximum(i * sub - 1, 0), 0))
    return cur, prev


def _att_sub(cur_all, prev_first, j):
    cur = cur_all[j * BLOCK:(j + 1) * BLOCK]
    return cur, (prev_first if j == 0 else cur_all[(j - 1) * BLOCK:j * BLOCK])


def _att_qkv(cur, prev, qn_g, kn_g):
    bq, bk = _seg_matrix(RW, 6), _seg_matrix(KVW, 6)
    qn, rq = _qk_norm(cur[:, 0:RW], qn_g, bq)
    kcur, rkc = _qk_norm(cur[:, RW:RW + KVW], kn_g, bk)
    kprev, _ = _qk_norm(prev[:, RW:RW + KVW], kn_g, bk)
    kc = jnp.concatenate([kprev, kcur], axis=0)
    vc = jnp.concatenate([prev[:, RW + KVW:], cur[:, RW + KVW:]], axis=0)
    return qn, rq, kc, vc, rkc


def att_fwd(pa, qn_g, kn_g, sinks, *, name):
    T = pa.shape[0]
    sub = ATT_SUB if (T // BLOCK) % ATT_SUB == 0 else 1
    n = T // (sub * BLOCK)

    def body(cur_ref, prev_ref, qg_ref, kg_ref, sk_ref, o_ref):
        i = pl.program_id(0)
        cur_all, prev_first = cur_ref[...], prev_ref[...]
        for j in range(sub):
            cur, prev = _att_sub(cur_all, prev_first, j)
            qn, _, kc, vc, _ = _att_qkv(cur, prev, qg_ref[...], kg_ref[...])
            probs, _ = _att_probs(_heads(qn, N_HEADS), _kv_heads(kc), _att_mask(i * sub + j), _sinks3(sk_ref[...]))
            o_ref[j * BLOCK:(j + 1) * BLOCK, :] = _unheads(_bdot(probs, _kv_heads(vc), _HPV))

    cur, prev = _att_blocks(sub)
    return _pcall(body, name=name, grid=(n,),
                  in_specs=[cur, prev, _full_spec((1, RW)), _full_spec((1, KVW)), _full_spec((1, LANE))],
                  out_specs=pl.BlockSpec((sub * BLOCK, RW), lambda i: (i, 0)), out_shape=jax.ShapeDtypeStruct((T, RW), F32),
                  compiler_params=_params(("parallel",)))(pa, pa, qn_g, kn_g, sinks)


def att_bwd(pa, do, qn_g, kn_g, sinks, *, name):
    T = pa.shape[0]
    sub = ATT_SUB if (T // BLOCK) % ATT_SUB == 0 else 1
    n = T // (sub * BLOCK)

    def one_block(cur, prev, do, blk, qg_ref, kg_ref, sk_ref, rows, dq_ref, dko_ref, dkn_ref, dvo_ref, dvn_ref):
        qn, rq, kc, vc, _ = _att_qkv(cur, prev, qg_ref[...], kg_ref[...])
        q3, k3, v3, do3 = _heads(qn, N_HEADS), _kv_heads(kc), _kv_heads(vc), _heads(do, N_HEADS)
        probs, psink = _att_probs(q3, k3, _att_mask(blk), _sinks3(sk_ref[...]))
        dprobs = _bdot(do3, v3, _HQK)
        delta = jnp.sum(probs * dprobs, axis=-1, keepdims=True)
        ds = probs * (dprobs - delta) * (HEAD_DIM ** -0.5)
        dsink3 = -jnp.sum(psink * delta, axis=1, keepdims=True)
        lane = lax.broadcasted_iota(jnp.int32, (1, LANE), 1)
        dsink = jnp.zeros((1, LANE), F32)
        for h in range(N_HEADS):
            dsink = dsink + jnp.where(lane == h, dsink3[h], 0.0)
        dqn = _unheads(_bdot(ds, k3, _HPV))

        def per_kv_head(x3):
            groups = [sum(x3[g * ATT_GROUP + j] for j in range(ATT_GROUP)) for g in range(KVW // HEAD_DIM)]
            return jnp.concatenate(groups, axis=1)

        dk, dv = per_kv_head(_bdot(ds, q3, _HTN)), per_kv_head(_bdot(probs, do3, _HTN))
        dkn_ref[rows, :], dko_ref[rows, :] = dk[0:BLOCK], dk[BLOCK:]
        dvn_ref[rows, :], dvo_ref[rows, :] = dv[0:BLOCK], dv[BLOCK:]
        qhat = cur[:, 0:RW] * rq
        dqh = dqn * qg_ref[...]
        dq_ref[rows, :] = rq * (dqh - qhat * (_segsum(dqh * qhat, _seg_matrix(RW, 6)) * (1.0 / HEAD_DIM)))
        prod = dqn * qhat
        fold = prod[:, 0:HEAD_DIM]
        for h in range(1, N_HEADS):
            fold = fold + prod[:, h * HEAD_DIM:(h + 1) * HEAD_DIM]
        return jnp.sum(fold, axis=0, keepdims=True), dsink

    def body(cur_ref, prev_ref, do_ref, qg_ref, kg_ref, sk_ref,
             dq_ref, dko_ref, dkn_ref, dvo_ref, dvn_ref, dqg_ref, dsk_ref):
        i = pl.program_id(0)
        cur_all, prev_first, do_all = cur_ref[...], prev_ref[...], do_ref[...]
        dqg, dsk = None, None
        for j in range(sub):
            cur, prev = _att_sub(cur_all, prev_first, j)
            rows = slice(j * BLOCK, (j + 1) * BLOCK)
            g, s = one_block(cur, prev, do_all[rows], i * sub + j, qg_ref, kg_ref, sk_ref, rows,
                             dq_ref, dko_ref, dkn_ref, dvo_ref, dvn_ref)
            dqg, dsk = (g, s) if dqg is None else (dqg + g, dsk + s)
        _acc_rows(dqg_ref, dqg, i)
        _acc_rows(dsk_ref, dsk, i)

    cur, prev = _att_blocks(sub)
    kvb = pl.BlockSpec((sub * BLOCK, KVW), lambda i: (i, 0))
    qb = pl.BlockSpec((sub * BLOCK, RW), lambda i: (i, 0))
    return _pcall(body, name=name, grid=(n,),
                  in_specs=[cur, prev, qb, _full_spec((1, RW)), _full_spec((1, KVW)), _full_spec((1, LANE))],
                  out_specs=[qb, kvb, kvb, kvb, kvb, _full_spec((1, HEAD_DIM)), _full_spec((1, LANE))],
                  out_shape=[jax.ShapeDtypeStruct((T, RW), F32)] + [jax.ShapeDtypeStruct((T, KVW), F32)] * 4
                  + [jax.ShapeDtypeStruct((1, HEAD_DIM), F32), jax.ShapeDtypeStruct((1, LANE), F32)],
                  compiler_params=_params(("arbitrary",)))(pa, pa, do, qn_g, kn_g, sinks)


def att_kv_bwd(pa, dq, dko, dkn, dvo, dvn, kn_g, *, name):
    T = pa.shape[0]
    n = T // BLOCK

    def body(pa_ref, dq_ref, dko_ref, dkn_ref, dvo_ref, dvn_ref, kg_ref, dpa_ref, dkg_ref):
        i = pl.program_id(0)
        more = i < n - 1
        dkn_tot = dko_ref[...] + jnp.where(more, dkn_ref[...], 0.0)
        dv_tot = dvo_ref[...] + jnp.where(more, dvn_ref[...], 0.0)
        kraw = pa_ref[:, RW:RW + KVW]
        bk = _seg_matrix(KVW, 6)
        _, rk = _qk_norm(kraw, kg_ref[...], bk)
        khat = kraw * rk
        dkh = dkn_tot * kg_ref[...]
        dpa_ref[:, 0:RW] = dq_ref[...]
        dpa_ref[:, RW:RW + KVW] = rk * (dkh - khat * (_segsum(dkh * khat, bk) * (1.0 / HEAD_DIM)))
        dpa_ref[:, RW + KVW:] = dv_tot
        prod = dkn_tot * khat
        _acc_rows(dkg_ref, jnp.sum(prod[:, 0:HEAD_DIM] + prod[:, HEAD_DIM:], axis=0, keepdims=True), i)

    kvb = pl.BlockSpec((BLOCK, KVW), lambda i: (i, 0))
    nxt = pl.BlockSpec((BLOCK, KVW), lambda i: (jnp.minimum(i + 1, n - 1), 0))
    return _pcall(body, name=name, grid=(n,),
                  in_specs=[pl.BlockSpec((BLOCK, ATT_COLS), lambda i: (i, 0)), pl.BlockSpec((BLOCK, RW), lambda i: (i, 0)),
                            kvb, nxt, kvb, nxt, _full_spec((1, KVW))],
                  out_specs=[pl.BlockSpec((BLOCK, ATT_COLS), lambda i: (i, 0)), _full_spec((1, HEAD_DIM))],
                  out_shape=[jax.ShapeDtypeStruct((T, ATT_COLS), F32), jax.ShapeDtypeStruct((1, HEAD_DIM), F32)],
                  compiler_params=_params(("arbitrary",)))(pa, dq, dko, dkn, dvo, dvn, kn_g)


WKV_CHUNK = 128
WKV_GROUP = 32
WKV_GROUP_FWD = 64


def _diag_mask():
    i = lax.broadcasted_iota(jnp.int32, (HEAD_DIM, RW), 0)
    j = lax.broadcasted_iota(jnp.int32, (HEAD_DIM, RW), 1) & (HEAD_DIM - 1)
    return i == j


def _heads_matrix():
    head = jnp.arange(RW // 2) // HEAD_DIM
    bd = (head[:, None] == head[None, :]).astype(BF16)
    return jnp.concatenate([bd, bd], axis=0)


def _headsums(xs, pieces, bd2):
    half = RW // 2
    bd = bd2[:pieces * half]
    rows = []
    for x in xs:
        parts, rest = [], x
        for n in range(pieces):
            p = rest.astype(BF16)
            parts.append(p)
            if n + 1 < pieces:
                rest = rest - p.astype(F32)
        for sl in (slice(0, half), slice(half, RW)):
            rows.append(jnp.concatenate([p[:, sl] for p in parts], axis=1))
    out = lax.dot_general(jnp.concatenate(rows, axis=0), bd, (((1,), (0,)), ((), ())), preferred_element_type=F32)
    return [jnp.concatenate([out[2 * n * HEAD_DIM:(2 * n + 1) * HEAD_DIM], out[(2 * n + 1) * HEAD_DIM:(2 * n + 2) * HEAD_DIM]],
                            axis=1) for n in range(len(xs))]


def _headsum(x):
    low = lax.broadcasted_iota(jnp.int32, (HEAD_DIM, LANE), 1) < HEAD_DIM
    tiles = []
    for c in range(RW // LANE):
        xt = x[:, c * LANE:(c + 1) * LANE]
        s_lo = jnp.sum(jnp.where(low, xt, 0.0), axis=1, keepdims=True)
        s_hi = jnp.sum(jnp.where(low, 0.0, xt), axis=1, keepdims=True)
        tiles.append(jnp.where(low, s_lo, s_hi))
    return jnp.concatenate(tiles, axis=1)


def _cols(rows, diag, bd2, pieces=2):
    return _headsums([jnp.where(diag, r, 0.0) for r in rows], pieces, bd2)


def _row(x, diag):
    return jnp.sum(jnp.where(diag, x, 0.0), axis=0, keepdims=True)


def wkv_fwd(r, w, k, v, a, b, *, name, gather=()):
    T = r.shape[0]
    ch = min(WKV_CHUNK, T)
    ngroups = ch // WKV_GROUP_FWD
    nchunks = T // ch
    ng = len(gather)

    def body(*refs):
        r_ref, w_ref, k_ref, v_ref, a_ref, b_ref, bd_ref = refs[:7]
        y_ref, st_ref = refs[7 + ng:9 + ng]
        s_scr = refs[9 + 2 * ng]
        step = pl.program_id(0)
        if ng:
            plan = _GatherPlan(refs[7:7 + ng], refs[9 + ng:9 + 2 * ng], refs[10 + 2 * ng:])
            pl.when(step == 0)(plan.start)
            pl.when(step == nchunks // 2)(plan.relay)

        @pl.when(step == 0)
        def _():
            s_scr[...] = jnp.zeros_like(s_scr)

        diag, bd2 = _diag_mask(), bd_ref[...]

        def group(gi, S):
            t0 = pl.multiple_of(gi * WKV_GROUP_FWD, WKV_GROUP_FWD)
            rows = pl.ds(t0, WKV_GROUP_FWD)
            R, W, K, V, A, B = (ref[rows, :] for ref in (r_ref, w_ref, k_ref, v_ref, a_ref, b_ref))
            vcols = _cols([V[s:s + 1] for s in range(WKV_GROUP_FWD)], diag, bd2, 1)
            yrows = []
            for s in range(WKV_GROUP_FWD):
                sa = _headsum(S * A[s:s + 1])
                S = S * W[s:s + 1] + sa * B[s:s + 1] + vcols[s] * K[s:s + 1]
                st_ref[t0 + s] = S
                yrows.append(_row(_headsums([S * R[s:s + 1]], 1, bd2)[0], diag))
            y_ref[rows, :] = jnp.concatenate(yrows, axis=0)
            return S

        s_scr[...] = lax.fori_loop(0, ngroups, group, s_scr[...])
        if ng:
            pl.when(step == nchunks - 1)(plan.finish_relayed)

    vec = pl.BlockSpec((ch, RW), lambda c: (c, 0))
    return _pcall(
        body, name=name, grid=(nchunks,), in_specs=[vec] * 6 + [_full_spec((RW, RW // 2))] + [_HBM] * ng,
        out_specs=[vec, pl.BlockSpec((ch, HEAD_DIM, RW), lambda c: (c, 0, 0))] + [_HBM] * ng,
        out_shape=[jax.ShapeDtypeStruct((T, RW), F32), jax.ShapeDtypeStruct((T, HEAD_DIM, RW), F32)] + _gathered_shapes(gather),
        scratch_shapes=[pltpu.VMEM((HEAD_DIM, RW), F32)] + (_GatherPlan.sems(ng) if ng else []),
        compiler_params=_params(("arbitrary",)),
    )(r, w, k, v, a, b, _heads_matrix(), *gather)


def wkv_bwd(r, w, k, v, a, b, dy, states, *, name, exchange=()):
    T = r.shape[0]
    ch = min(WKV_CHUNK, T)
    nchunks = T // ch
    ngroups = ch // WKV_GROUP
    ne = len(exchange)

    def body(*refs):
        r_ref, w_ref, k_ref, v_ref, a_ref, b_ref, dy_ref, st_ref, stp_ref, bd_ref = refs[:10]
        dr_ref, dw_ref, dk_ref, dv_ref, da_ref, db_ref = refs[10 + ne:16 + ne]
        ds_scr = refs[16 + 2 * ne]
        step = pl.program_id(0)
        if ne:
            plan = _ExchangePlan(refs[10:10 + ne], refs[16 + ne:16 + 2 * ne], refs[17 + 2 * ne:])
            pl.when(step == 0)(plan.start)

        @pl.when(step == 0)
        def _():
            ds_scr[...] = jnp.zeros_like(ds_scr)

        has_prev_chunk = step < nchunks - 1
        diag, bd2 = _diag_mask(), bd_ref[...]
        colsum = lambda x: jnp.sum(x, axis=0, keepdims=True)

        def group(gj, dS):
            gi = ngroups - 1 - gj
            t0 = pl.multiple_of(gi * WKV_GROUP, WKV_GROUP)
            rows = pl.ds(t0, WKV_GROUP)
            R, W, K, V, A, B, DY = (ref[rows, :] for ref in (r_ref, w_ref, k_ref, v_ref, a_ref, b_ref, dy_ref))
            before = jnp.where(gi > 0, st_ref[jnp.maximum(t0 - 1, 0)], jnp.where(has_prev_chunk, stp_ref[0], 0.0))
            prev_state = lambda s: st_ref[t0 + s - 1] if s > 0 else before
            steps = range(WKV_GROUP)
            dycols = _cols([DY[s:s + 1] for s in steps], diag, bd2, 1)
            vcols = _cols([V[s:s + 1] for s in steps], diag, bd2, 1)
            sas = _headsums([prev_state(s) * A[s:s + 1] for s in steps], 1, bd2)
            got = [[None] * WKV_GROUP for _ in range(6)]
            for s in reversed(steps):
                Sp = prev_state(s)
                dS = dS + dycols[s] * R[s:s + 1]
                got[0][s] = colsum(st_ref[t0 + s] * dycols[s])
                got[3][s] = _row(_headsums([dS * K[s:s + 1]], 1, bd2)[0], diag)
                got[2][s] = colsum(dS * vcols[s])
                dsa = _headsum(dS * B[s:s + 1])
                got[5][s] = colsum(dS * sas[s])
                got[1][s] = colsum(dS * Sp)
                got[4][s] = colsum(Sp * dsa)
                dS = dS * W[s:s + 1] + dsa * A[s:s + 1]
            for q, ref in enumerate((dr_ref, dw_ref, dk_ref, dv_ref, da_ref, db_ref)):
                ref[rows, :] = jnp.concatenate(got[q], axis=0)
            return dS

        ds_scr[...] = lax.fori_loop(0, ngroups, group, ds_scr[...])
        if ne:
            pl.when(step == nchunks - 1)(plan.finish)

    vec = pl.BlockSpec((ch, RW), lambda c: (nchunks - 1 - c, 0))
    st_spec = pl.BlockSpec((ch, HEAD_DIM, RW), lambda c: (nchunks - 1 - c, 0, 0))
    stp_spec = pl.BlockSpec((1, HEAD_DIM, RW), lambda c: (jnp.maximum((nchunks - 1 - c) * ch - 1, 0), 0, 0))
    return _pcall(
        body, name=name, grid=(nchunks,), in_specs=[vec] * 7 + [st_spec, stp_spec, _full_spec((RW, RW // 2))] + [_HBM] * ne,
        out_specs=[vec] * 6 + [_HBM] * ne,
        out_shape=[jax.ShapeDtypeStruct((T, RW), F32)] * 6 + [jax.ShapeDtypeStruct(e.shape, e.dtype) for e in exchange],
        scratch_shapes=[pltpu.VMEM((HEAD_DIM, RW), F32)] + (_ExchangePlan.sems(ne) if ne else []),
        compiler_params=_params(("arbitrary",)),
    )(r, w, k, v, a, b, dy, states, states, _heads_matrix(), *exchange)


_HBM = pl.BlockSpec(memory_space=pltpu.HBM)
_MESH = pl.DeviceIdType.MESH


def _place():
    x, y, c = lax.axis_index("x"), lax.axis_index("y"), lax.axis_index("c")
    return x, y, c, [(1 - x, y), (x, 1 - y), (1 - x, 1 - y)]


def _remote(src, dst, send_sem, recv_sem, to):
    return pltpu.make_async_remote_copy(src_ref=src, dst_ref=dst, send_sem=send_sem, recv_sem=recv_sem, device_id=to,
                                        device_id_type=_MESH)


def _dma_sems(*counts):
    return [pltpu.SemaphoreType.DMA((n,)) for n in counts]


class _GatherPlan:
    def __init__(self, ins, outs, sems):
        self.ins, self.outs, self.n = ins, outs, len(ins)
        self.ici_send, self.ici_recv, self.d2d_send, self.d2d_recv, self.local_sems = sems
        x, y, c, chips = _place()
        self.c, self.me, self.sibling = c, 2 * x + y, (x, y, 1 - c)
        self.peers = [(2 * qx + qy, (qx, qy, c)) for qx, qy in chips]

    @staticmethod
    def sems(n):
        return _dma_sems(3 * n, 3 * n, 3 * n, 3 * n, n)

    def _half(self, i, which):
        rh = self.ins[i].shape[0] // 2
        return pl.ds(which * rh, rh)

    def _local(self, i):
        return pltpu.make_async_copy(self.ins[i], self.outs[i].at[self.me], self.local_sems.at[i])

    def _send(self, i, j):
        k, mine = 3 * i + j, self._half(i, self.c)
        return _remote(self.ins[i].at[mine], self.outs[i].at[self.me, mine], self.ici_send.at[k], self.ici_recv.at[k],
                       self.peers[j][1])

    def _landed(self, i, j):
        k, piece = 3 * i + j, self.outs[i].at[self.peers[j][0], self._half(i, self.c)]
        return _remote(piece, piece, self.ici_send.at[k], self.ici_recv.at[k], self.peers[j][1])

    def _pass(self, i, j, which):
        k, piece = 3 * i + j, self.outs[i].at[self.peers[j][0], self._half(i, which)]
        return _remote(piece, piece, self.d2d_send.at[k], self.d2d_recv.at[k], self.sibling)

    def _all(self):
        return [(i, j) for i in range(self.n) for j in range(3)]

    def start(self):
        for i in range(self.n):
            self._local(i).start()
        for i, j in self._all():
            self._send(i, j).start()

    def relay(self):
        for i, j in self._all():
            self._landed(i, j).wait_recv()
            self._pass(i, j, self.c).start()

    def finish_relayed(self):
        for i, j in self._all():
            self._pass(i, j, 1 - self.c).wait_recv()
        for i, j in self._all():
            self._send(i, j).wait_send()
            self._pass(i, j, self.c).wait_send()
        for i in range(self.n):
            self._local(i).wait()

    def finish(self):
        self.relay()
        self.finish_relayed()

    @staticmethod
    def out_shapes(shards):
        return _gathered_shapes(shards)


def _gathered_shapes(shards):
    return [jax.ShapeDtypeStruct((N_CHIPS,) + s.shape, s.dtype) for s in shards]


def gather_weights(shards, *, name):
    n = len(shards)

    def body(*refs):
        plan = _GatherPlan(refs[:n], refs[n:2 * n], refs[2 * n:])
        plan.start()
        plan.finish()

    return _pcall(body, name=name, in_specs=[_HBM] * n, out_specs=[_HBM] * n, out_shape=_gathered_shapes(shards),
                  scratch_shapes=_GatherPlan.sems(n), compiler_params=_params())(*shards)


class _SiblingPlan:
    halves = True

    def __init__(self, ins, outs, sems):
        self.ins, self.outs, self.n = ins, outs, len(ins)
        self.send_sems, self.recv_sems = sems
        x, y, c, _ = _place()
        self.c, self.sibling = c, (x, y, 1 - c)

    @staticmethod
    def sems(n):
        return _dma_sems(n, n)

    @classmethod
    def out_shapes(cls, arrays):
        if not cls.halves:
            return [jax.ShapeDtypeStruct(a.shape, a.dtype) for a in arrays]
        return [jax.ShapeDtypeStruct((a.shape[0], a.shape[1] // 2, a.shape[2]), a.dtype) for a in arrays]

    def _copy(self, i):
        src = self.ins[i]
        if self.halves:
            rh = src.shape[1] // 2
            src = src.at[:, pl.ds((1 - self.c) * rh, rh)]
        return _remote(src, self.outs[i], self.send_sems.at[i], self.recv_sems.at[i], self.sibling)

    def start(self):
        for i in range(self.n):
            self._copy(i).start()

    def finish(self):
        for i in range(self.n):
            self._copy(i).wait_recv()
        for i in range(self.n):
            self._copy(i).wait_send()


class _SiblingWhole(_SiblingPlan):
    halves = False


def to_sibling(arrays, take_other_half, *, name):
    n = len(arrays)
    plan_cls = _SiblingPlan if take_other_half else _SiblingWhole

    def body(*refs):
        plan = plan_cls(refs[:n], refs[n:2 * n], refs[2 * n:])
        plan.start()
        plan.finish()

    return _pcall(body, name=name, in_specs=[_HBM] * n, out_specs=[_HBM] * n, out_shape=plan_cls.out_shapes(arrays),
                  scratch_shapes=plan_cls.sems(n), compiler_params=_params())(*arrays)


def exchange_chips(arrays, *, name):
    n = len(arrays)

    def body(*refs):
        plan = _ExchangePlan(refs[:n], refs[n:2 * n], refs[2 * n:])
        plan.start()
        plan.finish()

    return _pcall(body, name=name, in_specs=[_HBM] * n, out_specs=[_HBM] * n,
                  out_shape=[jax.ShapeDtypeStruct(a.shape, a.dtype) for a in arrays],
                  scratch_shapes=_ExchangePlan.sems(n), compiler_params=_params())(*arrays)


class _ExchangePlan:
    def __init__(self, ins, outs, sems):
        self.ins, self.outs, self.n = ins, outs, len(ins)
        self.send_sems, self.recv_sems, self.local_sems = sems
        x, y, c, chips = _place()
        self.me = 2 * x + y
        self.peers = [(2 * qx + qy, (qx, qy, c)) for qx, qy in chips]

    @staticmethod
    def sems(n):
        return _dma_sems(3 * n, 3 * n, n)

    @staticmethod
    def out_shapes(arrays):
        return [jax.ShapeDtypeStruct(a.shape, a.dtype) for a in arrays]

    def _local(self, i):
        return pltpu.make_async_copy(self.ins[i].at[self.me], self.outs[i].at[self.me], self.local_sems.at[i])

    def _send(self, i, j):
        k = 3 * i + j
        return _remote(self.ins[i].at[self.peers[j][0]], self.outs[i].at[self.me], self.send_sems.at[k], self.recv_sems.at[k],
                       self.peers[j][1])

    def _landed(self, i, j):
        k, piece = 3 * i + j, self.outs[i].at[self.peers[j][0]]
        return _remote(piece, piece, self.send_sems.at[k], self.recv_sems.at[k], self.peers[j][1])

    def start(self):
        for i in range(self.n):
            self._local(i).start()
            for j in range(3):
                self._send(i, j).start()

    def finish(self):
        for i in range(self.n):
            for j in range(3):
                self._landed(i, j).wait_recv()
        for i in range(self.n):
            for j in range(3):
                self._send(i, j).wait_send()
            self._local(i).wait()


def _core_index():
    return lax.axis_index("c").astype(jnp.int32).reshape(1)


def pair_sum(g, theirs, wire_dtype, *, name):
    _, R, C = g.shape
    rh = R // 2
    tr = _tile(rh, 256, 16)
    nt = rh // tr

    def body(c_ref, g_ref, t_ref, q_ref, qw_ref):
        q = g_ref[...] + t_ref[...]
        q_ref[...] = q
        qw_ref[...] = q.astype(wire_dtype)

    blk = pl.BlockSpec((1, tr, C), lambda b, i, c_ref: (b, i, 0))
    mine = pl.BlockSpec((1, tr, C), lambda b, i, c_ref: (b, c_ref[0] * nt + i, 0))
    grid_spec = pltpu.PrefetchScalarGridSpec(num_scalar_prefetch=1, grid=(N_CHIPS, nt), in_specs=[mine, blk], out_specs=[blk, blk])
    return _pcall(body, name=name, grid_spec=grid_spec,
                  out_shape=[jax.ShapeDtypeStruct((N_CHIPS, rh, C), F32), jax.ShapeDtypeStruct((N_CHIPS, rh, C), wire_dtype)],
                  compiler_params=_params(("parallel", "parallel")))(_core_index(), g, theirs)


def half_sum(own, landed, *, name):
    _, rh, C = own.shape
    tr = _tile(rh, 256, 16)

    def body(me_ref, own_ref, land_ref, o_ref):
        total = None
        for p in range(N_CHIPS):
            term = jnp.where(me_ref[0] == p, own_ref[p], land_ref[p].astype(F32))
            total = term if total is None else total + term
        o_ref[...] = total

    blk = pl.BlockSpec((N_CHIPS, tr, C), lambda i, me_ref: (0, i, 0))
    grid_spec = pltpu.PrefetchScalarGridSpec(num_scalar_prefetch=1, grid=(rh // tr,), in_specs=[blk, blk],
                                             out_specs=pl.BlockSpec((tr, C), lambda i, me_ref: (i, 0)))
    me = (2 * lax.axis_index("x") + lax.axis_index("y")).astype(jnp.int32).reshape(1)
    return _pcall(body, name=name, grid_spec=grid_spec, out_shape=jax.ShapeDtypeStruct((rh, C), F32),
                  compiler_params=_params(("parallel",)))(me, own, landed)


def adamw(w, m, v, mine, theirs, *, name):
    _, R, C = w.shape
    rh = R // 2
    tr = _tile(rh, 256, 8)
    nt = rh // tr

    def body(c_ref, w_ref, m_ref, v_ref, a_ref, b_ref, g_ref, d_ref, nm_ref, nv_ref):
        is_mine = (pl.program_id(0) // nt) == c_ref[0]
        g = jnp.where(is_mine, a_ref[...], b_ref[...])
        g_ref[...] = g
        nm = ADAM_B1 * m_ref[...] + (1.0 - ADAM_B1) * g
        nv = ADAM_B2 * v_ref[...] + (1.0 - ADAM_B2) * (g * g)
        nm_ref[...] = nm
        nv_ref[...] = nv
        m_hat = nm / (1.0 - ADAM_B1 ** ADAM_STEP)
        v_hat = nv / (1.0 - ADAM_B2 ** ADAM_STEP)
        d_ref[...] = -ADAM_LR * (m_hat / (jnp.sqrt(v_hat) + ADAM_EPS) + ADAM_WD * w_ref[...])

    full = pl.BlockSpec((None, tr, C), lambda i, c_ref: (0, i, 0))
    a_spec = pl.BlockSpec((tr, C), lambda i, c_ref: (jnp.clip(i - c_ref[0] * nt, 0, nt - 1), 0))
    b_spec = pl.BlockSpec((tr, C), lambda i, c_ref: (jnp.clip(i - (1 - c_ref[0]) * nt, 0, nt - 1), 0))
    grid_spec = pltpu.PrefetchScalarGridSpec(num_scalar_prefetch=1, grid=(2 * nt,), in_specs=[full] * 3 + [a_spec, b_spec],
                                             out_specs=[full] * 4)
    return _pcall(body, name=name, grid_spec=grid_spec, out_shape=[jax.ShapeDtypeStruct((1, R, C), F32)] * 4,
                  compiler_params=_params(("arbitrary",)))(_core_index(), w, m, v, mine, theirs)


def _to_blocks(full, axis):
    r, c = full.shape
    if axis == 1:
        return full.reshape(r, N_CHIPS, c // N_CHIPS).transpose(1, 0, 2)
    return full.reshape(N_CHIPS, r // N_CHIPS, c)


def _from_blocks(blocks, axis):
    _, r, c = blocks.shape
    if axis == 1:
        return blocks.transpose(1, 0, 2).reshape(r, N_CHIPS * c)
    return blocks.reshape(N_CHIPS * r, c)


def _ffn_fwd(x, norm, wg_t, wu_t, wd, tag):
    h = rms_fwd(x, norm, name=tag + "_norm")
    gate, up, act = mm_fused(h, [wg_t, wu_t], _swiglu, [BF16] * 3, tb=True, name=tag + "_gate_up")
    out = mm(act, wd, scale=0.5, res=x, name=tag + "_down")
    return out, (h, gate, up, act)


def _ffn_bwd(dout, x, saved, norm, wg_t, wu_t, wd, tag, carry=None, reduce=None):
    h, gate, up, act = saved
    dgate, dup, *carried = mm_fused(dout, [wd], _swiglu_bwd, [BF16] * 2, tb=True, extras=[gate, up], name=tag + "_dact",
                                    carry=carry)
    dwd = mm(act, dout, ta=True, scale=0.5, name=tag + "_dwd")
    if reduce:
        down_blocks = _to_blocks(dwd, 0)
        dwg_t, down_from_sibling = mm(dgate, h, ta=True, name=tag + "_dwg", carry=(_SiblingPlan, [down_blocks]))
        down_pair = pair_sum(down_blocks, down_from_sibling, BF16, name="pair_sum_" + reduce[2])
        dwu_t, down_landed = mm(dup, h, ta=True, name=tag + "_dwu", carry=(_ExchangePlan, [down_pair[1]]))
        blocks = [_to_blocks(g, 0) for g in (dwg_t, dwu_t)]
        dh, *from_sibling = mm(dgate, wg_t, name=tag + "_dh_gate", carry=(_SiblingPlan, blocks))
        pair = [pair_sum(g, t, BF16, name="pair_sum_" + n) for g, t, n in zip(blocks, from_sibling, reduce[:2])]
        dx, dnorm, *landed = mm(dup, wu_t, res=dh, name=tag + "_dh_up", norm_bwd=(x, norm, dout),
                                carry=(_ExchangePlan, [q for _, q in pair]))
        return dx, dnorm, dwg_t, dwu_t, dwd, carried, (pair + [down_pair], landed + [down_landed])
    dwg_t = mm(dgate, h, ta=True, name=tag + "_dwg")
    dwu_t = mm(dup, h, ta=True, name=tag + "_dwu")
    dh = mm(dgate, wg_t, name=tag + "_dh_gate")
    dx, dnorm = mm(dup, wu_t, res=dh, name=tag + "_dh_up", norm_bwd=(x, norm, dout))
    return dx, dnorm, dwg_t, dwu_t, dwd, carried, None


TRANSPOSED = ('ffn1_w_gate', 'ffn1_w_up', 'ffn2_w_gate', 'ffn2_w_up')
FIRST_WEIGHTS = ['ffn1_w_gate', 'ffn1_w_up', 'ffn1_w_down']
MID_WEIGHTS = ['w_in', 'rwkv_w_lora_up', 'rwkv_a_lora_up', 'rwkv_g_lora_up']
LATE_WEIGHTS = ['w_branch_rwkv', 'w_branch_attn', 'w_out', 'ffn2_w_gate', 'ffn2_w_up', 'ffn2_w_down']


def _pair_sums(names, blocks, tag):
    from_sibling = to_sibling(blocks, True, name=tag + "_grads_to_sibling")
    return [pair_sum(g, t, F32 if n == 'small' else BF16, name="pair_sum_" + n)
            for g, t, n in zip(blocks, from_sibling, names)]


def _step(A):
    x, tgt = A['x'][0], A['loss_target'][0]
    T = x.shape[0]
    w = {n: A[n][0] for n in WEIGHT_NAMES}
    row = lambda a: a.reshape(1, -1)

    axis_of = {n: (0 if n in TRANSPOSED else axis) for n, axis in BIG}
    natural = lambda n, a: jnp.swapaxes(a, 1, 2) if n in TRANSPOSED else a
    shard = lambda n: natural(n, A[n])[0].astype(BF16)
    n1, nmix, n2, nfin = (row(w[n]) for n in ('ffn1_norm', 'mix_norm', 'ffn2_norm', 'final_norm'))
    gathered = gather_weights([shard(n) for n in FIRST_WEIGHTS[:2]], name="gather_weights")
    full = {n: _from_blocks(b, axis_of[n]) for n, b in zip(FIRST_WEIGHTS[:2], gathered)}
    h1 = rms_fwd(x, n1, name="ffn1_norm")
    w_in_shard = shard('w_in')
    upper, lower = w_in_shard[:D_MODEL // 2], w_in_shard[D_MODEL // 2:]
    gate1, up1, act1, down_blocks, upper_blocks = mm_fused(
        h1, [full['ffn1_w_gate'], full['ffn1_w_up']], _swiglu, [BF16] * 3, tb=True, name="ffn1_gate_up",
        carry=(_GatherPlan, [shard('ffn1_w_down'), upper]))
    full['ffn1_w_down'] = _from_blocks(down_blocks, axis_of['ffn1_w_down'])
    x1, lower_blocks, *gathered = mm(act1, full['ffn1_w_down'], scale=0.5, res=x, name="ffn1_down",
                                     carry=(_GatherPlan, [lower] + [shard(n) for n in MID_WEIGHTS[1:]]))
    full['w_in'] = jnp.concatenate([_from_blocks(upper_blocks, 1), _from_blocks(lower_blocks, 1)], axis=0)
    full.update({n: _from_blocks(b, axis_of[n]) for n, b in zip(MID_WEIGHTS[1:], gathered)})
    ffn1 = (h1, gate1, up1, act1)
    w_in_r = _pad_rwkv_cols(full['w_in'][:, :RWKV_COLS])
    w_in_a = full['w_in'][:, RWKV_COLS:RWKV_COLS + ATT_COLS]
    w_in_g = full['w_in'][:, RWKV_COLS + ATT_COLS:]
    wlw, wla, wlg = (_pad_rows(full[n], 128).astype(F32) for n in ('rwkv_w_lora_up', 'rwkv_a_lora_up', 'rwkv_g_lora_up'))
    mu = _pad_rwkv_cols(row(w['rwkv_mu']))
    w0, a0, k_k, k_a, r_k, ln_w, ln_b = (row(w[n]) for n in ('rwkv_w0', 'rwkv_a0', 'rwkv_k_k', 'rwkv_k_a', 'rwkv_r_k',
                                                               'rwkv_ln_w', 'rwkv_ln_b'))
    qg = jnp.tile(row(w['attn_q_norm']), (1, N_HEADS))
    kg = jnp.tile(row(w['attn_k_norm']), (1, KVW // HEAD_DIM))
    sinks = jnp.pad(row(w['attn_sinks']), ((0, 0), (0, LANE - N_HEADS)))

    h2 = rms_fwd(x1, nmix, name="mix_norm")
    pr = mm(h2, w_in_r, name="proj_rwkv")
    pa = mm(h2, w_in_a, name="proj_att")
    pg = mm(h2, w_in_g, name="proj_gate")
    pr_shift = jnp.pad(pr, ((1, 0), (0, 0)))[:-1]
    r, dec, k2, v, a, b, sg = rwkv_pre_fwd(pr, pr_shift, mu, w0, a0, k_k, k_a, wlw, wla, wlg, name="rwkv_pre")
    y, states, *gathered = wkv_fwd(r, dec, k2, v, a, b, name="wkv_fwd", gather=[shard(n) for n in LATE_WEIGHTS])
    full.update({n: _from_blocks(b, axis_of[n]) for n, b in zip(LATE_WEIGHTS, gathered)})
    yr = rwkv_post_fwd(y, r, k2, v, sg, wlg, ln_w, ln_b, r_k, name="rwkv_post")
    ya = att_fwd(pa, qg, kg, sinks, name="att_fwd")
    br = mm(yr, full['w_branch_rwkv'], name="branch_rwkv")
    ba = mm(ya, full['w_branch_attn'], name="branch_att")
    mg = merge_fwd(br, ba, pg, name="merge")
    x2 = mm(mg, full['w_out'], res=x1, name="mix_out")
    x3, ffn2 = _ffn_fwd(x2, n2, full['ffn2_w_gate'], full['ffn2_w_up'], full['ffn2_w_down'], "ffn2")
    dx3, d_nfin, loss = final_loss(x3, tgt, nfin, name="final_loss")

    G = {'final_norm': d_nfin}
    dx2, G['ffn2_norm'], G['ffn2_w_gate'], G['ffn2_w_up'], G['ffn2_w_down'], _, _ = _ffn_bwd(
        dx3, x2, ffn2, n2, full['ffn2_w_gate'], full['ffn2_w_up'], full['ffn2_w_down'], "ffn2")
    dmg = mm(dx2, full['w_out'], tb=True, name="d_merge")
    G['w_out'] = mm(mg, dx2, ta=True, name="d_w_out")
    dbr, dba, dpg = merge_bwd(dmg, br, ba, pg, name="merge_bwd")
    dyr = mm(dbr, full['w_branch_rwkv'], tb=True, name="d_y_rwkv")
    G['w_branch_rwkv'] = mm(yr, dbr, ta=True, name="d_w_branch_rwkv")
    dya = mm(dba, full['w_branch_attn'], tb=True, name="d_y_att")
    G['w_branch_attn'] = mm(ya, dba, ta=True, name="d_w_branch_att")
    late_blocks = [_to_blocks(G[n], axis_of[n]) for n in LATE_WEIGHTS]
    dy, dz, dg, G['rwkv_ln_w'], G['rwkv_ln_b'], *late_from_sibling = rwkv_post_bwd(
        dyr, y, r, k2, v, sg, wlg, ln_w, ln_b, r_k, name="rwkv_post_bwd", carry=(_SiblingPlan, late_blocks))
    late_pair = [pair_sum(g, t, BF16, name="pair_sum_" + n) for g, t, n in zip(late_blocks, late_from_sibling, LATE_WEIGHTS)]
    res = wkv_bwd(r, dec, k2, v, a, b, dy, states, name="wkv_bwd", exchange=[q for _, q in late_pair])
    wkv_grads, late_landed = res[:6], res[6:]
    (dpr, d_mu, G['rwkv_w0'], G['rwkv_a0'], G['rwkv_k_k'], G['rwkv_k_a'], G['rwkv_r_k'], d_wlw, d_wla, d_wlg) = rwkv_pre_bwd(
        pr, pr_shift, *wkv_grads, dz, dg, mu, w0, a0, k_k, k_a, r_k, wlw, wla, wlg, name="rwkv_pre_bwd")
    G['rwkv_mu'] = _unpad_rwkv_cols(d_mu)
    G['rwkv_w_lora_up'], G['rwkv_a_lora_up'], G['rwkv_g_lora_up'] = d_wlw[:DECAY_LORA], d_wla[:ICLR_LORA], d_wlg[:GATE_LORA]
    dq, dko, dkn, dvo, dvn, G['attn_q_norm'], d_sinks = att_bwd(pa, dya, qg, kg, sinks, name="att_bwd")
    G['attn_sinks'] = d_sinks[:, :N_HEADS]
    dpa, G['attn_k_norm'] = att_kv_bwd(pa, dq, dko, dkn, dvo, dvn, kg, name="att_kv_bwd")
    d_w_in_r = mm(h2, dpr, ta=True, name="d_w_in_rwkv")
    d_w_in_a = mm(h2, dpa, ta=True, name="d_w_in_att")
    d_w_in_g = mm(h2, dpg, ta=True, name="d_w_in_gate")
    G['w_in'] = jnp.concatenate([_unpad_rwkv_cols(d_w_in_r), d_w_in_a, d_w_in_g], axis=1)
    mid_blocks = [_to_blocks(G[n], axis_of[n]) for n in MID_WEIGHTS]
    dh2, *mid_from_sibling = mm(dpr, w_in_r, tb=True, name="d_h2_rwkv", carry=(_SiblingPlan, mid_blocks))
    dh2 = mm(dpa, w_in_a, tb=True, res=dh2, name="d_h2_att")
    dx1, G['mix_norm'] = mm(dpg, w_in_g, tb=True, res=dh2, name="d_h2_gate", norm_bwd=(x1, nmix, dx2))
    mid_pair = [pair_sum(g, t, BF16, name="pair_sum_" + n) for g, t, n in zip(mid_blocks, mid_from_sibling, MID_WEIGHTS)]
    dx0, G['ffn1_norm'], _, _, _, mid_landed, (first_pair, first_landed) = _ffn_bwd(
        dx1, x, ffn1, n1, full['ffn1_w_gate'], full['ffn1_w_up'], full['ffn1_w_down'], "ffn1",
        carry=(_ExchangePlan, [q for _, q in mid_pair]), reduce=FIRST_WEIGHTS)

    small_shapes = [(w[n].size,) for n in SMALL] + [(1,)]

    def small_rows(parts):
        vec = jnp.concatenate([p.reshape(-1) for p in parts])
        return jnp.pad(vec, (0, SMALL_ROWS * FLAT_W - vec.shape[0])).reshape(SMALL_ROWS, FLAT_W)

    small = small_rows([G[n] for n in SMALL] + [loss[0, :1]])
    small_pair = _pair_sums(['small'], [jnp.broadcast_to(small[None], (N_CHIPS,) + small.shape)], "small")
    small_landed = exchange_chips([q for _, q in small_pair], name="exchange_small")
    names = FIRST_WEIGHTS + ['small'] + MID_WEIGHTS + LATE_WEIGHTS
    pair = first_pair + small_pair + mid_pair + late_pair
    landed = list(first_landed) + list(small_landed) + list(mid_landed) + list(late_landed)
    halves = [half_sum(own, l, name="half_sum_" + n) for (own, _), l, n in zip(pair, landed, names)]
    other_halves = to_sibling(halves, False, name="halves_to_sibling")

    def local(prefix, n):
        if n != 'small':
            return natural(n, A[prefix + n])
        return small_rows([A[prefix + s] for s in SMALL] + [jnp.zeros((1,), F32)])[None]

    result = {}
    for n, mine, theirs in zip(names, halves, other_halves):
        outs4 = adamw(local('', n), local('m_', n), local('v_', n), mine, theirs, name="adamw_" + n)
        for kind, o in zip(('grad_', 'delta_', 'new_m_', 'new_v_'), outs4):
            if n != 'small':
                result[kind + n] = natural(n, o)
            else:
                for s, part in zip(SMALL + ['loss'], _unpack_vec(o.reshape(-1), small_shapes)):
                    result[kind + s] = part.reshape(A[s].shape) if s != 'loss' else part.reshape(())
    outs = [result['grad_loss'], dx0[None]]
    for kind in ('grad_', 'delta_', 'new_m_', 'new_v_'):
        outs += [result[kind + n] for n in WEIGHT_NAMES]
    return tuple(outs)


def _unpack_vec(vec, shapes):
    out, off = [], 0
    for (n,) in shapes:
        out.append(vec[off:off + n])
        off += n
    return out


def kernel(x, ffn1_norm, ffn1_w_gate, ffn1_w_up, ffn1_w_down, mix_norm, w_in, rwkv_mu, rwkv_w0, rwkv_w_lora_up, rwkv_a0, rwkv_a_lora_up, rwkv_g_lora_up, rwkv_k_k, rwkv_k_a, rwkv_r_k, rwkv_ln_w, rwkv_ln_b, attn_q_norm, attn_k_norm, attn_sinks, w_branch_rwkv, w_branch_attn, w_out, ffn2_norm, ffn2_w_gate, ffn2_w_up, ffn2_w_down, final_norm, loss_target, m_ffn1_norm, m_ffn1_w_gate, m_ffn1_w_up, m_ffn1_w_down, m_mix_norm, m_w_in, m_rwkv_mu, m_rwkv_w0, m_rwkv_w_lora_up, m_rwkv_a0, m_rwkv_a_lora_up, m_rwkv_g_lora_up, m_rwkv_k_k, m_rwkv_k_a, m_rwkv_r_k, m_rwkv_ln_w, m_rwkv_ln_b, m_attn_q_norm, m_attn_k_norm, m_attn_sinks, m_w_branch_rwkv, m_w_branch_attn, m_w_out, m_ffn2_norm, m_ffn2_w_gate, m_ffn2_w_up, m_ffn2_w_down, m_final_norm, v_ffn1_norm, v_ffn1_w_gate, v_ffn1_w_up, v_ffn1_w_down, v_mix_norm, v_w_in, v_rwkv_mu, v_rwkv_w0, v_rwkv_w_lora_up, v_rwkv_a0, v_rwkv_a_lora_up, v_rwkv_g_lora_up, v_rwkv_k_k, v_rwkv_k_a, v_rwkv_r_k, v_rwkv_ln_w, v_rwkv_ln_b, v_attn_q_norm, v_attn_k_norm, v_attn_sinks, v_w_branch_rwkv, v_w_branch_attn, v_w_out, v_ffn2_norm, v_ffn2_w_gate, v_ffn2_w_up, v_ffn2_w_down, v_final_norm):
    return _step(dict(locals()))
```
